```python
import math
import jax, jax.numpy as jnp
from jax import lax
import numpy as np

D_MODEL = 1024
BATCH = 8
SEQ = 16384
DEPTH = 1

HEAD_DIM = 64
N_HEADS_A = 8
N_KV_A = 2
N_HEADS_B = 8
D_A = N_HEADS_A * HEAD_DIM
D_KV_A = N_KV_A * HEAD_DIM
D_B = N_HEADS_B * HEAD_DIM
D_MIX = D_A + D_B
D_IN = D_A + 2 * D_KV_A + 3 * D_B
D_FF = 4 * D_MODEL
D_PLE = 256
GRID_W = 64
ROPE_THETA = 10000.0
ROPE_HALF = HEAD_DIM // 2
Q_BLOCK = 128
DILATED_PATTERNS = ((128, 1), (512, 4), (2048, 16))
N_BUCKETS = 32
MAX_DISTANCE = 1024
EPS = 1e-6
NEG_BIG = -1e30

kernel_name = "hymba_axial_gqa_dilated_swa_encoder_layer"


def rms_norm(x, g):
    xf = x.astype(jnp.float32)
    y = xf * lax.rsqrt(jnp.mean(xf * xf, axis=-1, keepdims=True) + EPS)
    return (y * g.astype(jnp.float32)).astype(x.dtype)


def axial_rope_tables(n_tokens):
    rows = n_tokens // GRID_W
    row = jnp.broadcast_to(jnp.arange(rows)[:, None], (rows, GRID_W)).reshape(-1).astype(jnp.float32)
    col = jnp.broadcast_to(jnp.arange(GRID_W)[None, :], (rows, GRID_W)).reshape(-1).astype(jnp.float32)
    n_axis = ROPE_HALF // 2
    inv_freq = ROPE_THETA ** (-jnp.arange(n_axis, dtype=jnp.float32) / n_axis)
    ang = jnp.concatenate([row[:, None] * inv_freq, col[:, None] * inv_freq], axis=-1)
    return jnp.cos(ang), jnp.sin(ang)


def apply_rope(x, cos, sin):
    xf = x.astype(jnp.float32)
    x1, x2 = xf[..., :ROPE_HALF], xf[..., ROPE_HALF:]
    c, s = cos[None, :, None, :], sin[None, :, None, :]
    return jnp.concatenate([x1 * c - x2 * s, x2 * c + x1 * s], axis=-1).astype(x.dtype)


def mixer_a(q, k, v, g_q, g_k, cos, sin):
    q = apply_rope(rms_norm(q, g_q), cos, sin)
    k = apply_rope(rms_norm(k, g_k), cos, sin)
    b, s_len = q.shape[0], q.shape[1]
    grp = N_HEADS_A // N_KV_A
    nblk = s_len // Q_BLOCK
    qb = q.reshape(b, nblk, Q_BLOCK, N_KV_A, grp, HEAD_DIM).transpose(1, 0, 2, 3, 4, 5)
    scale = HEAD_DIM ** -0.5

    def block(q_blk):
        sc = jnp.einsum('bqkgd,bskd->bkgqs', q_blk, k, preferred_element_type=jnp.float32) * scale
        pr = jax.nn.softmax(sc, axis=-1)
        return jnp.einsum('bkgqs,bskd->bqkgd', pr.astype(v.dtype), v)

    o = lax.map(block, qb)
    return o.transpose(1, 0, 2, 3, 4, 5).reshape(b, s_len, D_A)


def t5_bucket(rel):
    nb = N_BUCKETS // 2
    max_exact = nb // 2
    side = jnp.where(rel > 0, nb, 0)
    n = jnp.abs(rel)
    large = max_exact + (jnp.log(jnp.maximum(n, max_exact).astype(jnp.float32) / max_exact)
                         / math.log(MAX_DISTANCE / max_exact) * (nb - max_exact)).astype(jnp.int32)
    large = jnp.minimum(large, nb - 1)
    return side + jnp.where(n < max_exact, n, large)


def dilated_pattern(q, k, v, rel_bias, window, dilation):
    b, s_len, h, d = q.shape
    length = s_len // dilation
    half = window // (2 * dilation)
    blk = half
    nblk = -(-length // blk)
    lp = nblk * blk

    def to_sub(x):
        x = x.reshape(b, length, dilation, h, d).transpose(0, 2, 1, 3, 4)
        return jnp.pad(x, ((0, 0), (0, 0), (0, lp - length), (0, 0), (0, 0)))

    def band(x):
        xp = jnp.pad(to_sub(x), ((0, 0), (0, 0), (blk, blk), (0, 0), (0, 0)))
        xb = xp.reshape(b, dilation, nblk + 2, blk, h, d)
        return jnp.concatenate([xb[:, :, :-2], xb[:, :, 1:-1], xb[:, :, 2:]], axis=3)

    qs = to_sub(q).reshape(b, dilation, nblk, blk, h, d)
    ks, vs = band(k), band(v)
    qi = jnp.arange(blk)
    kj = jnp.arange(3 * blk)
    rel = kj[None, :] - blk - qi[:, None]
    key_m = jnp.arange(nblk)[:, None] * blk - blk + kj[None, :]
    valid = ((jnp.abs(rel) <= half)[None, :, :]
             & (key_m >= 0)[:, None, :] & (key_m < length)[:, None, :])
    bias = rel_bias[t5_bucket(rel * dilation)].astype(jnp.float32)
    bias = bias.transpose(0, 2, 1)[None, None, None]
    sc = jnp.einsum('brnqhd,brnkhd->brnqhk', qs, ks, preferred_element_type=jnp.float32) * (HEAD_DIM ** -0.5)
    sc = jnp.where(valid[None, None, :, :, None, :], sc + bias, NEG_BIG)
    mx = jnp.max(sc, axis=-1)
    e = jnp.exp(sc - mx[..., None])
    den = jnp.sum(e, axis=-1)
    o = jnp.einsum('brnqhk,brnkhd->brnqhd', e.astype(vs.dtype), vs,
                   preferred_element_type=jnp.float32) / den[..., None]

    def from_sub(y):
        y = y.reshape((b, dilation, lp) + y.shape[4:])[:, :, :length]
        return jnp.moveaxis(y, 1, 2).reshape((b, s_len) + y.shape[3:])

    return from_sub(o), from_sub(mx), from_sub(den)


def mixer_b(q, k, v, rel_bias):
    outs, mxs, dens = [], [], []
    for window, dilation in DILATED_PATTERNS:
        o, mx, den = dilated_pattern(q, k, v, rel_bias, window, dilation)
        outs.append(o); mxs.append(mx); dens.append(den)
    m_all = jnp.maximum(jnp.maximum(mxs[0], mxs[1]), mxs[2])
    w = [den * jnp.exp(mx - m_all) for den, mx in zip(dens, mxs)]
    w_sum = w[0] + w[1] + w[2]
    y = (w[0][..., None] * outs[0] + w[1][..., None] * outs[1] + w[2][..., None] * outs[2]) / w_sum[..., None]
    b, s_len = q.shape[0], q.shape[1]
    return y.reshape(b, s_len, D_B).astype(q.dtype)


def _fwd_setup_inputs(seed: int = 0) -> dict:
    key = jax.random.key(seed)
    ks = jax.random.split(key, 20)
    nrm = lambda k, shape, scale: jax.random.normal(k, shape, jnp.float32) * scale
    gain = lambda k, shape: 1.0 + 0.05 * jax.random.normal(k, shape, jnp.float32)
    return {
        "x": nrm(ks[0], (BATCH, SEQ, D_MODEL), 1.0),
        "p": nrm(ks[1], (DEPTH, BATCH, SEQ, D_PLE), 1.0),
        "w_in": nrm(ks[2], (DEPTH, D_MODEL, D_IN), D_MODEL ** -0.5),
        "g_attn_pre": gain(ks[3], (DEPTH, D_MODEL)),
        "g_q": gain(ks[4], (DEPTH, HEAD_DIM)),
        "g_k": gain(ks[5], (DEPTH, HEAD_DIM)),
        "g_out_a": gain(ks[6], (DEPTH, D_A)),
        "g_out_b": gain(ks[7], (DEPTH, D_B)),
        "w_out": nrm(ks[8], (DEPTH, D_MIX, D_MODEL), D_MIX ** -0.5),
        "g_attn_post": gain(ks[9], (DEPTH, D_MODEL)),
        "rel_bias": nrm(ks[10], (N_BUCKETS, N_HEADS_B), 0.5),
        "g_mlp_pre": gain(ks[11], (DEPTH, D_MODEL)),
        "w_ff1": nrm(ks[12], (DEPTH, D_MODEL, D_FF), D_MODEL ** -0.5),
        "w_ff2": nrm(ks[13], (DEPTH, D_FF, D_MODEL), D_FF ** -0.5),
        "g_mlp_post": gain(ks[14], (DEPTH, D_MODEL)),
        "g_ple": gain(ks[15], (DEPTH, D_MODEL)),
        "w_ple_gate": nrm(ks[16], (DEPTH, D_MODEL, D_MODEL), D_MODEL ** -0.5),
        "w_ple_proj": nrm(ks[17], (DEPTH, D_PLE, D_MODEL), D_PLE ** -0.5),
    }


def _fwd_reference(x, p, w_in, g_attn_pre, g_q, g_k, g_out_a, g_out_b, w_out, g_attn_post, rel_bias,
              g_mlp_pre, w_ff1, w_ff2, g_mlp_post, g_ple, w_ple_gate, w_ple_proj):
    b, s_len = x.shape[0], x.shape[1]
    cos, sin = axial_rope_tables(s_len)
    split_at = [D_A, D_A + D_KV_A, D_A + 2 * D_KV_A, D_A + 2 * D_KV_A + D_B, D_A + 2 * D_KV_A + 2 * D_B]
    h = x
    for i in range(DEPTH):
        xn = rms_norm(h, g_attn_pre[i])
        proj = jnp.einsum('bsd,de->bse', xn, w_in[i])
        qa, ka, va, qb, kb, vb = jnp.split(proj, split_at, axis=-1)
        ya = mixer_a(qa.reshape(b, s_len, N_HEADS_A, HEAD_DIM),
                     ka.reshape(b, s_len, N_KV_A, HEAD_DIM),
                     va.reshape(b, s_len, N_KV_A, HEAD_DIM), g_q[i], g_k[i], cos, sin)
        yb = mixer_b(qb.reshape(b, s_len, N_HEADS_B, HEAD_DIM),
                     kb.reshape(b, s_len, N_HEADS_B, HEAD_DIM),
                     vb.reshape(b, s_len, N_HEADS_B, HEAD_DIM), rel_bias)
        y = jnp.concatenate([rms_norm(ya, g_out_a[i]), rms_norm(yb, g_out_b[i])], axis=-1)
        y = jnp.einsum('bse,ed->bsd', y, w_out[i])
        h = h + rms_norm(y, g_attn_post[i])
        xn = rms_norm(h, g_mlp_pre[i])
        f = jnp.square(jax.nn.relu(jnp.einsum('bsd,df->bsf', xn, w_ff1[i])))
        f = jnp.einsum('bsf,fd->bsd', f, w_ff2[i])
        h = h + rms_norm(f, g_mlp_post[i])
        gate = jax.nn.sigmoid(jnp.einsum('bsd,de->bse', rms_norm(h, g_ple[i]), w_ple_gate[i]))
        h = h + gate * jnp.einsum('bsp,pd->bsd', p[i], w_ple_proj[i])
    return h


import jax as _jax
import jax.numpy as _jnp

TWIN_FORMAT = 'train_step'
FWD_PARAMS = ['x', 'p', 'w_in', 'g_attn_pre', 'g_q', 'g_k', 'g_out_a', 'g_out_b', 'w_out', 'g_attn_post', 'rel_bias', 'g_mlp_pre', 'w_ff1', 'w_ff2', 'g_mlp_post', 'g_ple', 'w_ple_gate', 'w_ple_proj']
TWIN_WEIGHTS = ['w_in', 'g_attn_pre', 'g_q', 'g_k', 'g_out_a', 'g_out_b', 'w_out', 'g_attn_post', 'rel_bias', 'g_mlp_pre', 'w_ff1', 'w_ff2', 'g_mlp_post', 'g_ple', 'w_ple_gate', 'w_ple_proj']
TWIN_DIFF_INPUT = 'x'
TWIN_INPUTS = ['x', 'p', 'w_in', 'g_attn_pre', 'g_q', 'g_k', 'g_out_a', 'g_out_b', 'w_out', 'g_attn_post', 'rel_bias', 'g_mlp_pre', 'w_ff1', 'w_ff2', 'g_mlp_post', 'g_ple', 'w_ple_gate', 'w_ple_proj', 'loss_target', 'm_w_in', 'm_g_attn_pre', 'm_g_q', 'm_g_k', 'm_g_out_a', 'm_g_out_b', 'm_w_out', 'm_g_attn_post', 'm_rel_bias', 'm_g_mlp_pre', 'm_w_ff1', 'm_w_ff2', 'm_g_mlp_post', 'm_g_ple', 'm_w_ple_gate', 'm_w_ple_proj', 'v_w_in', 'v_g_attn_pre', 'v_g_q', 'v_g_k', 'v_g_out_a', 'v_g_out_b', 'v_w_out', 'v_g_attn_post', 'v_rel_bias', 'v_g_mlp_pre', 'v_w_ff1', 'v_w_ff2', 'v_g_mlp_post', 'v_g_ple', 'v_w_ple_gate', 'v_w_ple_proj']
TWIN_OUTPUTS = ['loss', 'grad_x', 'grad_w_in', 'grad_g_attn_pre', 'grad_g_q', 'grad_g_k', 'grad_g_out_a', 'grad_g_out_b', 'grad_w_out', 'grad_g_attn_post', 'grad_rel_bias', 'grad_g_mlp_pre', 'grad_w_ff1', 'grad_w_ff2', 'grad_g_mlp_post', 'grad_g_ple', 'grad_w_ple_gate', 'grad_w_ple_proj', 'delta_w_in', 'delta_g_attn_pre', 'delta_g_q', 'delta_g_k', 'delta_g_out_a', 'delta_g_out_b', 'delta_w_out', 'delta_g_attn_post', 'delta_rel_bias', 'delta_g_mlp_pre', 'delta_w_ff1', 'delta_w_ff2', 'delta_g_mlp_post', 'delta_g_ple', 'delta_w_ple_gate', 'delta_w_ple_proj', 'new_m_w_in', 'new_m_g_attn_pre', 'new_m_g_q', 'new_m_g_k', 'new_m_g_out_a', 'new_m_g_out_b', 'new_m_w_out', 'new_m_g_attn_post', 'new_m_rel_bias', 'new_m_g_mlp_pre', 'new_m_w_ff1', 'new_m_w_ff2', 'new_m_g_mlp_post', 'new_m_g_ple', 'new_m_w_ple_gate', 'new_m_w_ple_proj', 'new_v_w_in', 'new_v_g_attn_pre', 'new_v_g_q', 'new_v_g_k', 'new_v_g_out_a', 'new_v_g_out_b', 'new_v_w_out', 'new_v_g_attn_post', 'new_v_rel_bias', 'new_v_g_mlp_pre', 'new_v_w_ff1', 'new_v_w_ff2', 'new_v_g_mlp_post', 'new_v_g_ple', 'new_v_w_ple_gate', 'new_v_w_ple_proj']
TWIN_LEAF_KINDS = {'loss': 'loss', 'grad_x': 'grad_x', 'grad_w_in': 'grad_w', 'grad_g_attn_pre': 'grad_w', 'grad_g_q': 'grad_w', 'grad_g_k': 'grad_w', 'grad_g_out_a': 'grad_w', 'grad_g_out_b': 'grad_w', 'grad_w_out': 'grad_w', 'grad_g_attn_post': 'grad_w', 'grad_rel_bias': 'grad_w', 'grad_g_mlp_pre': 'grad_w', 'grad_w_ff1': 'grad_w', 'grad_w_ff2': 'grad_w', 'grad_g_mlp_post': 'grad_w', 'grad_g_ple': 'grad_w', 'grad_w_ple_gate': 'grad_w', 'grad_w_ple_proj': 'grad_w', 'delta_w_in': 'delta_w', 'delta_g_attn_pre': 'delta_w', 'delta_g_q': 'delta_w', 'delta_g_k': 'delta_w', 'delta_g_out_a': 'delta_w', 'delta_g_out_b': 'delta_w', 'delta_w_out': 'delta_w', 'delta_g_attn_post': 'delta_w', 'delta_rel_bias': 'delta_w', 'delta_g_mlp_pre': 'delta_w', 'delta_w_ff1': 'delta_w', 'delta_w_ff2': 'delta_w', 'delta_g_mlp_post': 'delta_w', 'delta_g_ple': 'delta_w', 'delta_w_ple_gate': 'delta_w', 'delta_w_ple_proj': 'delta_w', 'new_m_w_in': 'new_m', 'new_m_g_attn_pre': 'new_m', 'new_m_g_q': 'new_m', 'new_m_g_k': 'new_m', 'new_m_g_out_a': 'new_m', 'new_m_g_out_b': 'new_m', 'new_m_w_out': 'new_m', 'new_m_g_attn_post': 'new_m', 'new_m_rel_bias': 'new_m', 'new_m_g_mlp_pre': 'new_m', 'new_m_w_ff1': 'new_m', 'new_m_w_ff2': 'new_m', 'new_m_g_mlp_post': 'new_m', 'new_m_g_ple': 'new_m', 'new_m_w_ple_gate': 'new_m', 'new_m_w_ple_proj': 'new_m', 'new_v_w_in': 'new_v', 'new_v_g_attn_pre': 'new_v', 'new_v_g_q': 'new_v', 'new_v_g_k': 'new_v', 'new_v_g_out_a': 'new_v', 'new_v_g_out_b': 'new_v', 'new_v_w_out': 'new_v', 'new_v_g_attn_post': 'new_v', 'new_v_rel_bias': 'new_v', 'new_v_g_mlp_pre': 'new_v', 'new_v_w_ff1': 'new_v', 'new_v_w_ff2': 'new_v', 'new_v_g_mlp_post': 'new_v', 'new_v_g_ple': 'new_v', 'new_v_w_ple_gate': 'new_v', 'new_v_w_ple_proj': 'new_v'}


def _forward(args):
    return _fwd_reference(*[args[k] for k in FWD_PARAMS])


def _output_shape():
    def fwd():
        inp = _fwd_setup_inputs(0)
        return _fwd_reference(*[inp[k] for k in FWD_PARAMS])
    out = _jax.eval_shape(fwd)
    return out.shape, out.dtype

N_MICROBATCH = 1
ADAM_LR = 0.001
ADAM_B1 = 0.9
ADAM_B2 = 0.999
ADAM_EPS = 1e-08
ADAM_WD = 0.01
ADAM_STEP = 10
PER_EXAMPLE_BATCH_AXIS = {'x': 0, 'p': 1, 'loss_target': 0}
SHARED_INPUTS = []
_WEIGHT_DTYPES = {'w_in': _jnp.float32, 'g_attn_pre': _jnp.float32, 'g_q': _jnp.float32, 'g_k': _jnp.float32, 'g_out_a': _jnp.float32, 'g_out_b': _jnp.float32, 'w_out': _jnp.float32, 'g_attn_post': _jnp.float32, 'rel_bias': _jnp.float32, 'g_mlp_pre': _jnp.float32, 'w_ff1': _jnp.float32, 'w_ff2': _jnp.float32, 'g_mlp_post': _jnp.float32, 'g_ple': _jnp.float32, 'w_ple_gate': _jnp.float32, 'w_ple_proj': _jnp.float32}
MOMENT_SCALE = {'w_in': 3.430270e+00, 'g_attn_pre': 5.846437e+00, 'g_q': 6.147860e+00, 'g_k': 6.952018e+00, 'g_out_a': 7.664204e+00, 'g_out_b': 2.793124e+00, 'w_out': 5.648711e+00, 'g_attn_post': 1.270072e+02, 'rel_bias': 1.226767e+00, 'g_mlp_pre': 3.726178e+00, 'w_ff1': 1.776128e+00, 'w_ff2': 6.642932e+00, 'g_mlp_post': 1.304303e+02, 'g_ple': 4.927707e+00, 'w_ple_gate': 2.830124e+00, 'w_ple_proj': 1.494895e+00}


def _to_microbatches(a, axis):
    t = _jnp.moveaxis(a, axis, 0)
    t = t.reshape((N_MICROBATCH, t.shape[0] // N_MICROBATCH) + t.shape[1:])
    return _jnp.moveaxis(t, 1, axis + 1)


def setup_inputs(seed: int = 0) -> dict:
    inp = _fwd_setup_inputs(seed)
    key = _jax.random.fold_in(_jax.random.key(seed), 7919)
    shape, _ = _output_shape()
    out = dict(inp)
    out["loss_target"] = _jax.random.normal(_jax.random.fold_in(key, 0), shape, _jnp.float32)
    for i, name in enumerate(TWIN_WEIGHTS):
        w = inp[name].astype(_jnp.float32)
        if MOMENT_SCALE is None:
            s = _jnp.sqrt(_jnp.mean(_jnp.square(w)) + 1e-30)
        else:
            s = MOMENT_SCALE[name]
        km, kv = _jax.random.split(_jax.random.fold_in(key, i + 1))
        out[name] = w
        out["m_" + name] = s * _jax.random.normal(km, w.shape, _jnp.float32)
        out["v_" + name] = (s * s) * _jax.random.uniform(kv, w.shape, _jnp.float32, 0.5, 1.5)
    if N_MICROBATCH > 1:
        for name, axis in PER_EXAMPLE_BATCH_AXIS.items():
            out[name] = _to_microbatches(out[name], axis)
    return {'x': out['x'], 'p': out['p'], 'w_in': out['w_in'], 'g_attn_pre': out['g_attn_pre'], 'g_q': out['g_q'], 'g_k': out['g_k'], 'g_out_a': out['g_out_a'], 'g_out_b': out['g_out_b'], 'w_out': out['w_out'], 'g_attn_post': out['g_attn_post'], 'rel_bias': out['rel_bias'], 'g_mlp_pre': out['g_mlp_pre'], 'w_ff1': out['w_ff1'], 'w_ff2': out['w_ff2'], 'g_mlp_post': out['g_mlp_post'], 'g_ple': out['g_ple'], 'w_ple_gate': out['w_ple_gate'], 'w_ple_proj': out['w_ple_proj'], 'loss_target': out['loss_target'], 'm_w_in': out['m_w_in'], 'm_g_attn_pre': out['m_g_attn_pre'], 'm_g_q': out['m_g_q'], 'm_g_k': out['m_g_k'], 'm_g_out_a': out['m_g_out_a'], 'm_g_out_b': out['m_g_out_b'], 'm_w_out': out['m_w_out'], 'm_g_attn_post': out['m_g_attn_post'], 'm_rel_bias': out['m_rel_bias'], 'm_g_mlp_pre': out['m_g_mlp_pre'], 'm_w_ff1': out['m_w_ff1'], 'm_w_ff2': out['m_w_ff2'], 'm_g_mlp_post': out['m_g_mlp_post'], 'm_g_ple': out['m_g_ple'], 'm_w_ple_gate': out['m_w_ple_gate'], 'm_w_ple_proj': out['m_w_ple_proj'], 'v_w_in': out['v_w_in'], 'v_g_attn_pre': out['v_g_attn_pre'], 'v_g_q': out['v_g_q'], 'v_g_k': out['v_g_k'], 'v_g_out_a': out['v_g_out_a'], 'v_g_out_b': out['v_g_out_b'], 'v_w_out': out['v_w_out'], 'v_g_attn_post': out['v_g_attn_post'], 'v_rel_bias': out['v_rel_bias'], 'v_g_mlp_pre': out['v_g_mlp_pre'], 'v_w_ff1': out['v_w_ff1'], 'v_w_ff2': out['v_w_ff2'], 'v_g_mlp_post': out['v_g_mlp_post'], 'v_g_ple': out['v_g_ple'], 'v_w_ple_gate': out['v_w_ple_gate'], 'v_w_ple_proj': out['v_w_ple_proj']}


def _loss(weights, diff, rest, loss_target):
    with _jax.named_scope("forward"):
        args = {**rest, TWIN_DIFF_INPUT: diff, **{k: w.astype(_WEIGHT_DTYPES[k]) for k, w in weights.items()}}
        y = _forward(args)
    with _jax.named_scope("loss_head"):
        err = _jnp.square(y.astype(_jnp.float32) - loss_target)
        return 0.5 * _jnp.sum(_jnp.mean(err, axis=-1)) if err.ndim else 0.5 * err


def _adamw(w, g, m, v):
    m = ADAM_B1 * m + (1.0 - ADAM_B1) * g
    v = ADAM_B2 * v + (1.0 - ADAM_B2) * _jnp.square(g)
    m_hat = m / (1.0 - ADAM_B1 ** ADAM_STEP)
    v_hat = v / (1.0 - ADAM_B2 ** ADAM_STEP)
    delta = -ADAM_LR * (m_hat / (_jnp.sqrt(v_hat) + ADAM_EPS) + ADAM_WD * w)
    return delta, m, v


def reference(x, p, w_in, g_attn_pre, g_q, g_k, g_out_a, g_out_b, w_out, g_attn_post, rel_bias, g_mlp_pre, w_ff1, w_ff2, g_mlp_post, g_ple, w_ple_gate, w_ple_proj, loss_target, m_w_in, m_g_attn_pre, m_g_q, m_g_k, m_g_out_a, m_g_out_b, m_w_out, m_g_attn_post, m_rel_bias, m_g_mlp_pre, m_w_ff1, m_w_ff2, m_g_mlp_post, m_g_ple, m_w_ple_gate, m_w_ple_proj, v_w_in, v_g_attn_pre, v_g_q, v_g_k, v_g_out_a, v_g_out_b, v_w_out, v_g_attn_post, v_rel_bias, v_g_mlp_pre, v_w_ff1, v_w_ff2, v_g_mlp_post, v_g_ple, v_w_ple_gate, v_w_ple_proj):
    given = dict(x=x, p=p, w_in=w_in, g_attn_pre=g_attn_pre, g_q=g_q, g_k=g_k, g_out_a=g_out_a, g_out_b=g_out_b, w_out=w_out, g_attn_post=g_attn_post, rel_bias=rel_bias, g_mlp_pre=g_mlp_pre, w_ff1=w_ff1, w_ff2=w_ff2, g_mlp_post=g_mlp_post, g_ple=g_ple, w_ple_gate=w_ple_gate, w_ple_proj=w_ple_proj, loss_target=loss_target, m_w_in=m_w_in, m_g_attn_pre=m_g_attn_pre, m_g_q=m_g_q, m_g_k=m_g_k, m_g_out_a=m_g_out_a, m_g_out_b=m_g_out_b, m_w_out=m_w_out, m_g_attn_post=m_g_attn_post, m_rel_bias=m_rel_bias, m_g_mlp_pre=m_g_mlp_pre, m_w_ff1=m_w_ff1, m_w_ff2=m_w_ff2, m_g_mlp_post=m_g_mlp_post, m_g_ple=m_g_ple, m_w_ple_gate=m_w_ple_gate, m_w_ple_proj=m_w_ple_proj, v_w_in=v_w_in, v_g_attn_pre=v_g_attn_pre, v_g_q=v_g_q, v_g_k=v_g_k, v_g_out_a=v_g_out_a, v_g_out_b=v_g_out_b, v_w_out=v_w_out, v_g_attn_post=v_g_attn_post, v_rel_bias=v_rel_bias, v_g_mlp_pre=v_g_mlp_pre, v_w_ff1=v_w_ff1, v_w_ff2=v_w_ff2, v_g_mlp_post=v_g_mlp_post, v_g_ple=v_g_ple, v_w_ple_gate=v_w_ple_gate, v_w_ple_proj=v_w_ple_proj)
    weights = {n: given[n] for n in TWIN_WEIGHTS}
    shared = {n: given[n] for n in SHARED_INPUTS}
    per_example = {n: given[n] for n in ['x', 'p']}
    grad_fn = _jax.value_and_grad(_loss, argnums=(0, 1))

    def one_microbatch(ex, loss_target):
        ex = dict(ex)
        diff = ex.pop(TWIN_DIFF_INPUT)
        return grad_fn(weights, diff, {**shared, **ex}, loss_target)

    if N_MICROBATCH == 1:
        loss, (grad_w, grad_x) = one_microbatch(per_example, given["loss_target"])
    else:
        def body(carry, xs):
            loss_sum, grad_sum = carry
            l_k, (gw_k, gx_k) = one_microbatch(xs[0], xs[1])
            with _jax.named_scope("update"):
                return (loss_sum + l_k, _jax.tree.map(_jnp.add, grad_sum, gw_k)), gx_k

        init = (_jnp.zeros((), _jnp.float32), _jax.tree.map(_jnp.zeros_like, weights))
        (loss, grad_w), grad_x = _jax.lax.scan(body, init, (per_example, given["loss_target"]))
    with _jax.named_scope("update"):
        delta_w, new_m, new_v = {}, {}, {}
        for n in TWIN_WEIGHTS:
            delta_w[n], new_m[n], new_v[n] = _adamw(weights[n], grad_w[n], given["m_" + n], given["v_" + n])
    return (loss, grad_x, *[grad_w[n] for n in TWIN_WEIGHTS], *[delta_w[n] for n in TWIN_WEIGHTS],
            *[new_m[n] for n in TWIN_WEIGHTS], *[new_v[n] for n in TWIN_WEIGHTS])
```

```python
import functools
import math

import jax
import jax.numpy as jnp
from jax import lax
from jax.experimental import pallas as pl
from jax.experimental.pallas import tpu as pltpu

F32 = jnp.float32
BF16 = jnp.bfloat16

D_MODEL = 1024
HEAD_DIM = 64
N_HEADS_A = 8
N_KV_A = 2
GROUP_A = N_HEADS_A // N_KV_A
N_HEADS_B = 8
D_A = N_HEADS_A * HEAD_DIM
D_KV_A = N_KV_A * HEAD_DIM
D_B = N_HEADS_B * HEAD_DIM
D_IN = D_A + 2 * D_KV_A + 3 * D_B
D_FF = 4 * D_MODEL
D_PLE = 256
GRID_W = 64
ROPE_THETA = 10000.0
DILATIONS = (1, 4, 16)
HALF_WIN = 64
N_BUCKETS = 32
MAX_DISTANCE = 1024
EPS = 1e-6
NEG_BIG = -1e30
Q_SCALE = HEAD_DIM ** -0.5

ADAM_LR = 0.001
ADAM_B1 = 0.9
ADAM_B2 = 0.999
ADAM_EPS = 1e-08
ADAM_WD = 0.01
ADAM_STEP = 10

N_CHIPS = 4
MESH = pl.DeviceIdType.MESH

ROW_TILE = 512
ATT_TQ = 256
ATT_TK = 512
SWA_TQ = 256
DW_TS = 1024
VMEM_LIMIT = 56 * 1024 * 1024

NT = (((1,), (1,)), ((), ()))
TN = (((0,), (0,)), ((), ()))


def _cparams(sem=None, vmem=VMEM_LIMIT):
    return pltpu.CompilerParams(dimension_semantics=sem, vmem_limit_bytes=vmem)


def _full(shape):
    n = len(shape)
    return pl.BlockSpec(shape, lambda *_: (0,) * n)


def _rows(tm, width):
    return pl.BlockSpec((tm, width), lambda i: (i, 0))


def _split3(a):
    a1 = a.astype(BF16)
    r = a - a1.astype(F32)
    a2 = r.astype(BF16)
    a3 = (r - a2.astype(F32)).astype(BF16)
    return a1, a2, a3


def _xdot(a, sel):
    a1, a2, a3 = _split3(a)
    d = lambda p: jnp.dot(p, sel, preferred_element_type=F32)
    return d(a1) + d(a2) + d(a3)


def _mm(a, b):
    return jnp.dot(a, b, preferred_element_type=F32)


def _mm_nt(a, b):
    return lax.dot_general(a, b, NT, preferred_element_type=F32)


def _mm_tn(a, b):
    return lax.dot_general(a, b, TN, preferred_element_type=F32)


def _rms_stats(x):
    r = lax.rsqrt(jnp.mean(x * x, axis=-1, keepdims=True) + EPS)
    return x * r, r


def _rms_bwd(dy, xh, r, g):
    gdy = dy * g
    dx = r * (gdy - xh * jnp.mean(gdy * xh, axis=-1, keepdims=True))
    dg = jnp.sum(dy * xh, axis=0, keepdims=True)
    return dx, dg


def _acc_out(ref, val):
    @pl.when(pl.program_id(0) == 0)
    def _():
        ref[...] = jnp.zeros_like(ref)

    ref[...] += val


def _swap_halves(x, first_half):
    return jnp.where(first_half, pltpu.roll(x, 96, 1), pltpu.roll(x, 32, 1))


def _first_half_mask(shape):
    return (lax.broadcasted_iota(jnp.int32, shape, 1) % HEAD_DIM) < (HEAD_DIM // 2)


def _rope_tables(s_len):
    t = jnp.arange(s_len)
    row = (t // GRID_W).astype(F32)
    col = (t % GRID_W).astype(F32)
    n_axis = HEAD_DIM // 4
    inv_freq = ROPE_THETA ** (-jnp.arange(n_axis, dtype=F32) / n_axis)
    ang = jnp.concatenate([row[:, None] * inv_freq, col[:, None] * inv_freq], axis=-1)
    c, s = jnp.cos(ang), jnp.sin(ang)
    cc = jnp.concatenate([c, c, c, c], axis=-1)
    ss = jnp.concatenate([-s, s, -s, s], axis=-1)
    return cc, ss


def _group_ones(width):
    i = jnp.arange(width)
    return (i[:, None] // HEAD_DIM == i[None, :] // HEAD_DIM).astype(BF16)


def _head_expand(n_heads):
    i = jnp.arange(n_heads * HEAD_DIM)
    return (jnp.arange(n_heads)[:, None] == i[None, :] // HEAD_DIM).astype(BF16)


def _head_fold(width):
    i = jnp.arange(width)
    return (i[:, None] % HEAD_DIM == jnp.arange(HEAD_DIM)[None, :]).astype(BF16)


def _t5_bucket(rel):
    nb = N_BUCKETS // 2
    max_exact = nb // 2
    side = jnp.where(rel > 0, nb, 0)
    n = jnp.abs(rel)
    large = max_exact + (jnp.log(jnp.maximum(n, max_exact).astype(F32) / max_exact)
                         / math.log(MAX_DISTANCE / max_exact) * (nb - max_exact)).astype(jnp.int32)
    large = jnp.minimum(large, nb - 1)
    return side + jnp.where(n < max_exact, n, large)


def _bucket_onehot(tq, dilation):
    qi = jnp.arange(tq)
    kj = jnp.arange(tq + 2 * HALF_WIN)
    rel = kj[None, :] - HALF_WIN - qi[:, None]
    bucket = _t5_bucket(rel * dilation).reshape(-1)
    return (bucket[:, None] == jnp.arange(128)[None, :]).astype(BF16)


def _in_proj(x, g1, w_in, cc, ss, gq2, gk2, ones128):
    s_len = x.shape[0]
    tm = min(ROW_TILE, s_len)

    def body(x_ref, g_ref, w_ref, cc_ref, ss_ref, gq_ref, gk_ref, one_ref,
             xn_ref, qpre_ref, kpre_ref, qa_ref, ka_ref, va_ref, qb_ref, kb_ref, vb_ref):
        xh, _ = _rms_stats(x_ref[...])
        xn = (xh * g_ref[...]).astype(BF16)
        xn_ref[...] = xn
        proj = _mm(xn, w_ref[...])
        first_half = _first_half_mask((tm, 128))
        ones = one_ref[...]
        cc_t, ss_t = cc_ref[...], ss_ref[...]

        def norm_rope(xc, gain):
            ms = _xdot(xc * xc, ones) * (1.0 / HEAD_DIM)
            y = xc * lax.rsqrt(ms + EPS) * gain
            return y * cc_t + _swap_halves(y, first_half) * ss_t

        qpre_ref[...] = proj[:, :D_A]
        kpre_ref[...] = proj[:, D_A:D_A + D_KV_A]
        for c in range(D_A // 128):
            y = norm_rope(proj[:, 128 * c:128 * (c + 1)], gq_ref[...])
            qa_ref[:, 128 * c:128 * (c + 1)] = (y * Q_SCALE).astype(BF16)
        ka_ref[...] = norm_rope(proj[:, D_A:D_A + D_KV_A], gk_ref[...]).astype(BF16)
        o = D_A + D_KV_A
        va_ref[...] = proj[:, o:o + D_KV_A].astype(BF16)
        o += D_KV_A
        qb_ref[...] = (proj[:, o:o + D_B] * Q_SCALE).astype(BF16)
        kb_ref[...] = proj[:, o + D_B:o + 2 * D_B].astype(BF16)
        vb_ref[...] = proj[:, o + 2 * D_B:o + 3 * D_B].astype(BF16)

    sds = jax.ShapeDtypeStruct
    return pl.pallas_call(
        body, name="in_proj", grid=(s_len // tm,),
        in_specs=[_rows(tm, D_MODEL), _full((1, D_MODEL)), _full((D_MODEL, D_IN)), _rows(tm, 128), _rows(tm, 128),
                  _full((1, 128)), _full((1, 128)), _full((128, 128))],
        out_specs=[_rows(tm, D_MODEL), _rows(tm, D_A), _rows(tm, D_KV_A), _rows(tm, D_A), _rows(tm, D_KV_A),
                   _rows(tm, D_KV_A), _rows(tm, D_B), _rows(tm, D_B), _rows(tm, D_B)],
        out_shape=[sds((s_len, D_MODEL), BF16), sds((s_len, D_A), F32), sds((s_len, D_KV_A), F32),
                   sds((s_len, D_A), BF16), sds((s_len, D_KV_A), BF16), sds((s_len, D_KV_A), BF16),
                   sds((s_len, D_B), BF16), sds((s_len, D_B), BF16), sds((s_len, D_B), BF16)],
        compiler_params=_cparams(("parallel",)),
    )(x, g1, w_in, cc, ss, gq2, gk2, ones128)


def _stat_spec(tm):
    return pl.BlockSpec((N_HEADS_B, tm, 1), lambda i: (0, i, 0))


def _merge_b(accs, ms, ls):
    s_len = accs[0].shape[0]
    tm = min(ROW_TILE, s_len)

    def body(a0, a1, a2, m0, m1, m2, l0, l1, l2, yb_ref, lse_ref):
        coef = []
        for h in range(N_HEADS_B):
            m_all = jnp.maximum(jnp.maximum(m0[h], m1[h]), m2[h])
            w = [jnp.exp(m[h] - m_all) for m in (m0, m1, m2)]
            den = w[0] * l0[h] + w[1] * l1[h] + w[2] * l2[h]
            inv = 1.0 / den
            coef.append([wp * inv for wp in w])
            lse_ref[h] = m_all + jnp.log(den)
        low = lax.broadcasted_iota(jnp.int32, (tm, 128), 1) < HEAD_DIM
        for c in range(D_B // 128):
            sl = slice(128 * c, 128 * (c + 1))
            yb = jnp.zeros((tm, 128), F32)
            for p_i, ap in enumerate((a0, a1, a2)):
                yb = yb + ap[:, sl] * jnp.where(low, coef[2 * c][p_i], coef[2 * c + 1][p_i])
            yb_ref[:, sl] = yb

    return pl.pallas_call(
        body, name="merge_b", grid=(s_len // tm,),
        in_specs=[_rows(tm, D_B)] * 3 + [_stat_spec(tm)] * 6,
        out_specs=[_rows(tm, D_B), _stat_spec(tm)],
        out_shape=[jax.ShapeDtypeStruct((s_len, D_B), F32), jax.ShapeDtypeStruct((N_HEADS_B, s_len, 1), F32)],
        compiler_params=_cparams(("parallel",)),
    )(*accs, *ms, *ls)


def _out_proj(ya, yb, x, g_a, g_b, w_out, g_post, g_mlp_pre):
    s_len = x.shape[0]
    tm = min(ROW_TILE, s_len)

    def body(ya_ref, yb_ref, x_ref, ga_ref, gb_ref, w_ref, gp_ref, gm_ref, ycat_ref, y2_ref, h1_ref, xn2_ref):
        ah, _ = _rms_stats(ya_ref[...])
        bh, _ = _rms_stats(yb_ref[...])
        ycat = jnp.concatenate([ah * ga_ref[...], bh * gb_ref[...]], axis=-1).astype(BF16)
        ycat_ref[...] = ycat
        y2 = _mm(ycat, w_ref[...])
        y2_ref[...] = y2
        y2h, _ = _rms_stats(y2)
        h1 = x_ref[...] + y2h * gp_ref[...]
        h1_ref[...] = h1
        h1h, _ = _rms_stats(h1)
        xn2_ref[...] = (h1h * gm_ref[...]).astype(BF16)

    sds = jax.ShapeDtypeStruct
    return pl.pallas_call(
        body, name="out_proj", grid=(s_len // tm,),
        in_specs=[_rows(tm, D_A), _rows(tm, D_B), _rows(tm, D_MODEL), _full((1, D_A)), _full((1, D_B)),
                  _full((D_MODEL, D_MODEL)), _full((1, D_MODEL)), _full((1, D_MODEL))],
        out_specs=[_rows(tm, D_MODEL)] * 4,
        out_shape=[sds((s_len, D_MODEL), BF16), sds((s_len, D_MODEL), F32), sds((s_len, D_MODEL), F32),
                   sds((s_len, D_MODEL), BF16)],
        compiler_params=_cparams(("parallel",)),
    )(ya, yb, x, g_a, g_b, w_out, g_post, g_mlp_pre)


def _ff1(xn2, w_ff1):
    s_len = xn2.shape[0]
    tm = min(ROW_TILE, s_len)

    def body(x_ref, w_ref, u_ref):
        u_ref[...] = _mm(x_ref[...], w_ref[...])

    return pl.pallas_call(
        body, name="ff1", grid=(s_len // tm,),
        in_specs=[_rows(tm, D_MODEL), _full((D_MODEL, D_FF))],
        out_specs=_rows(tm, D_FF),
        out_shape=jax.ShapeDtypeStruct((s_len, D_FF), F32),
        compiler_params=_cparams(("parallel",)),
    )(xn2, w_ff1)


def _ff2(u, w_ff2, h1, g_post, g_ple):
    s_len = u.shape[0]
    tm = min(ROW_TILE, s_len)

    def body(u_ref, w_ref, h1_ref, gp_ref, gl_ref, f2_ref, h2_ref, xn3_ref):
        f = jnp.square(jnp.maximum(u_ref[...], 0.0)).astype(BF16)
        f2 = _mm(f, w_ref[...])
        f2_ref[...] = f2
        f2h, _ = _rms_stats(f2)
        h2 = h1_ref[...] + f2h * gp_ref[...]
        h2_ref[...] = h2
        h2h, _ = _rms_stats(h2)
        xn3_ref[...] = (h2h * gl_ref[...]).astype(BF16)

    sds = jax.ShapeDtypeStruct
    return pl.pallas_call(
        body, name="ff2", grid=(s_len // tm,),
        in_specs=[_rows(tm, D_FF), _full((D_FF, D_MODEL)), _rows(tm, D_MODEL), _full((1, D_MODEL)),
                  _full((1, D_MODEL))],
        out_specs=[_rows(tm, D_MODEL)] * 3,
        out_shape=[sds((s_len, D_MODEL), F32), sds((s_len, D_MODEL), F32), sds((s_len, D_MODEL), BF16)],
        compiler_params=_cparams(("parallel",)),
    )(u, w_ff2, h1, g_post, g_ple)


def _ple_loss(xn3, p, h2, f2, tgt, w_gate, w_ple, g_ple, g_mlp_post):
    s_len = h2.shape[0]
    tm = min(ROW_TILE, s_len)

    def body(xn3_ref, p_ref, h2_ref, f2_ref, t_ref, wg_ref, wp_ref, gl_ref, gp_ref,
             dh2_ref, df2_ref, dgl_ref, dpp_ref, loss_ref, dgple_ref, dgpost_ref):
        gate = jax.nn.sigmoid(_mm(xn3_ref[...], wg_ref[...]))
        pp = _mm(p_ref[...].astype(BF16), wp_ref[...])
        h2 = h2_ref[...]
        err = h2 + gate * pp - t_ref[...]
        sq = jnp.sum(jnp.sum(err * err, axis=1, keepdims=True), axis=0, keepdims=True)
        _acc_out(loss_ref, sq * (0.5 / D_MODEL))
        dh3 = err * (1.0 / D_MODEL)
        dgl = (dh3 * pp) * gate * (1.0 - gate)
        dgl_b = dgl.astype(BF16)
        dgl_ref[...] = dgl_b
        dpp_ref[...] = (dh3 * gate).astype(BF16)
        dxn3 = _mm_nt(dgl_b, wg_ref[...])
        h2h, r2 = _rms_stats(h2)
        dx, dg = _rms_bwd(dxn3, h2h, r2, gl_ref[...])
        _acc_out(dgple_ref, dg)
        dh2 = dh3 + dx
        dh2_ref[...] = dh2
        f2h, rf = _rms_stats(f2_ref[...])
        df2, dg = _rms_bwd(dh2, f2h, rf, gp_ref[...])
        _acc_out(dgpost_ref, dg)
        df2_ref[...] = df2.astype(BF16)

    sds = jax.ShapeDtypeStruct
    return pl.pallas_call(
        body, name="ple_loss", grid=(s_len // tm,),
        in_specs=[_rows(tm, D_MODEL), _rows(tm, D_PLE), _rows(tm, D_MODEL), _rows(tm, D_MODEL), _rows(tm, D_MODEL),
                  _full((D_MODEL, D_MODEL)), _full((D_PLE, D_MODEL)), _full((1, D_MODEL)), _full((1, D_MODEL))],
        out_specs=[_rows(tm, D_MODEL)] * 3 + [_rows(tm, D_MODEL), _full((1, 1)), _full((1, D_MODEL)),
                                              _full((1, D_MODEL))],
        out_shape=[sds((s_len, D_MODEL), F32), sds((s_len, D_MODEL), BF16), sds((s_len, D_MODEL), BF16),
                   sds((s_len, D_MODEL), BF16), sds((1, 1), F32), sds((1, D_MODEL), F32), sds((1, D_MODEL), F32)],
        compiler_params=_cparams(("arbitrary",)),
    )(xn3, p, h2, f2, tgt, w_gate, w_ple, g_ple, g_mlp_post)


def _ff2_bwd(df2, w_ff2, u):
    s_len = u.shape[0]
    tm = min(ROW_TILE, s_len)

    def body(d_ref, w_ref, u_ref, du_ref):
        df = _mm_nt(d_ref[...], w_ref[...])
        du_ref[...] = (df * (2.0 * jnp.maximum(u_ref[...], 0.0))).astype(BF16)

    return pl.pallas_call(
        body, name="ff2_bwd", grid=(s_len // tm,),
        in_specs=[_rows(tm, D_MODEL), _full((D_FF, D_MODEL)), _rows(tm, D_FF)],
        out_specs=_rows(tm, D_FF),
        out_shape=jax.ShapeDtypeStruct((s_len, D_FF), BF16),
        compiler_params=_cparams(("parallel",)),
    )(df2, w_ff2, u)


def _ff1_bwd(du, w_ff1, dh2, h1, y2, g_mlp_pre, g_post):
    s_len = du.shape[0]
    tm = min(ROW_TILE, s_len)

    def body(du_ref, w_ref, dh2_ref, h1_ref, y2_ref, gm_ref, gp_ref, dh1_ref, dy2_ref, dgm_ref, dgp_ref):
        dxn2 = _mm_nt(du_ref[...], w_ref[...])
        h1h, r1 = _rms_stats(h1_ref[...])
        dx, dg = _rms_bwd(dxn2, h1h, r1, gm_ref[...])
        _acc_out(dgm_ref, dg)
        dh1 = dh2_ref[...] + dx
        dh1_ref[...] = dh1
        y2h, ry = _rms_stats(y2_ref[...])
        dy2, dg = _rms_bwd(dh1, y2h, ry, gp_ref[...])
        _acc_out(dgp_ref, dg)
        dy2_ref[...] = dy2.astype(BF16)

    sds = jax.ShapeDtypeStruct
    return pl.pallas_call(
        body, name="ff1_bwd", grid=(s_len // tm,),
        in_specs=[_rows(tm, D_FF), _full((D_MODEL, D_FF)), _rows(tm, D_MODEL), _rows(tm, D_MODEL),
                  _rows(tm, D_MODEL), _full((1, D_MODEL)), _full((1, D_MODEL))],
        out_specs=[_rows(tm, D_MODEL), _rows(tm, D_MODEL), _full((1, D_MODEL)), _full((1, D_MODEL))],
        out_shape=[sds((s_len, D_MODEL), F32), sds((s_len, D_MODEL), BF16), sds((1, D_MODEL), F32),
                   sds((1, D_MODEL), F32)],
        compiler_params=_cparams(("arbitrary",)),
    )(du, w_ff1, dh2, h1, y2, g_mlp_pre, g_post)


def _out_proj_bwd(dy2, w_out, ya, yb, g_a, g_b):
    s_len = ya.shape[0]
    tm = min(ROW_TILE, s_len)

    def body(d_ref, w_ref, ya_ref, yb_ref, ga_ref, gb_ref, dya_ref, dyb_ref, da_ref, db_ref, dga_ref, dgb_ref):
        dycat = _mm_nt(d_ref[...], w_ref[...])
        low = lax.broadcasted_iota(jnp.int32, (tm, 128), 1) < HEAD_DIM
        for y_ref, g_ref, dy_ref, dl_ref, dg_ref, lo in ((ya_ref, ga_ref, dya_ref, da_ref, dga_ref, 0),
                                                         (yb_ref, gb_ref, dyb_ref, db_ref, dgb_ref, D_A)):
            y = y_ref[...]
            yh, r = _rms_stats(y)
            dx, dg = _rms_bwd(dycat[:, lo:lo + D_A], yh, r, g_ref[...])
            _acc_out(dg_ref, dg)
            dy_ref[...] = dx
            prod = dx * y
            for c in range(D_A // 128):
                pc = prod[:, 128 * c:128 * (c + 1)]
                dl_ref[2 * c] = jnp.sum(jnp.where(low, pc, 0.0), axis=1, keepdims=True)
                dl_ref[2 * c + 1] = jnp.sum(jnp.where(low, 0.0, pc), axis=1, keepdims=True)

    sds = jax.ShapeDtypeStruct
    return pl.pallas_call(
        body, name="out_proj_bwd", grid=(s_len // tm,),
        in_specs=[_rows(tm, D_MODEL), _full((D_MODEL, D_MODEL)), _rows(tm, D_A), _rows(tm, D_B), _full((1, D_A)),
                  _full((1, D_B))],
        out_specs=[_rows(tm, D_A), _rows(tm, D_B), _stat_spec(tm), _stat_spec(tm), _full((1, D_A)),
                   _full((1, D_B))],
        out_shape=[sds((s_len, D_A), F32), sds((s_len, D_B), F32), sds((N_HEADS_A, s_len, 1), F32),
                   sds((N_HEADS_B, s_len, 1), F32), sds((1, D_A), F32), sds((1, D_B), F32)],
        compiler_params=_cparams(("arbitrary",)),
    )(dy2, w_out, ya, yb, g_a, g_b)


def _in_proj_bwd(dqr, dkr, dva, dqb, dkb, dvb, qpre, kpre, x, dh1, g1, w_in, cc, ss, gq2, gk2, ones128):
    s_len = x.shape[0]
    tm = min(ROW_TILE // 2, s_len)

    def body(dqr_ref, dkr_ref, dva_ref, dq0, dq1, dq2, dk0, dk1, dk2, dv0, dv1, dv2, qpre_ref, kpre_ref, x_ref,
             dh1_ref, g_ref, w_ref, cc_ref, ss_ref, gq_ref, gk_ref, one_ref, dproj_ref, gx_ref, dg1_ref, dgq_ref,
             dgk_ref):
        first_half = _first_half_mask((tm, 128))
        ones = one_ref[...]
        cc_t, ss_t = cc_ref[...], ss_ref[...]

        def norm_rope_bwd(dy, xc, gain):
            dn = dy * cc_t - _swap_halves(dy, first_half) * ss_t
            r = lax.rsqrt(_xdot(xc * xc, ones) * (1.0 / HEAD_DIM) + EPS)
            xh = xc * r
            gdy = dn * gain
            dx = r * (gdy - xh * (_xdot(gdy * xh, ones) * (1.0 / HEAD_DIM)))
            return dx, jnp.sum(dn * xh, axis=0, keepdims=True)

        dgq = jnp.zeros((1, 128), F32)
        parts = []
        for c in range(D_A // 128):
            sl = slice(128 * c, 128 * (c + 1))
            dx, dg = norm_rope_bwd(dqr_ref[:, sl] * Q_SCALE, qpre_ref[:, sl], gq_ref[...])
            parts.append(dx)
            dgq = dgq + dg
        dxk, dgk = norm_rope_bwd(dkr_ref[...], kpre_ref[...], gk_ref[...])
        _acc_out(dgq_ref, dgq)
        _acc_out(dgk_ref, dgk)
        parts += [dxk, dva_ref[...], (dq0[...] + dq1[...] + dq2[...]) * Q_SCALE, dk0[...] + dk1[...] + dk2[...],
                  dv0[...] + dv1[...] + dv2[...]]
        dproj = jnp.concatenate(parts, axis=-1).astype(BF16)
        dproj_ref[...] = dproj
        dxn = _mm_nt(dproj, w_ref[...])
        xh, r = _rms_stats(x_ref[...])
        dx, dg = _rms_bwd(dxn, xh, r, g_ref[...])
        _acc_out(dg1_ref, dg)
        gx_ref[...] = dh1_ref[...] + dx

    sds = jax.ShapeDtypeStruct
    return pl.pallas_call(
        body, name="in_proj_bwd", grid=(s_len // tm,),
        in_specs=[_rows(tm, D_A), _rows(tm, D_KV_A), _rows(tm, D_KV_A)] + [_rows(tm, D_B)] * 9
                 + [_rows(tm, D_A), _rows(tm, D_KV_A), _rows(tm, D_MODEL), _rows(tm, D_MODEL),
                    _full((1, D_MODEL)), _full((D_MODEL, D_IN)), _rows(tm, 128), _rows(tm, 128), _full((1, 128)),
                    _full((1, 128)), _full((128, 128))],
        out_specs=[_rows(tm, D_IN), _rows(tm, D_MODEL), _full((1, D_MODEL)), _full((1, 128)), _full((1, 128))],
        out_shape=[sds((s_len, D_IN), BF16), sds((s_len, D_MODEL), F32), sds((1, D_MODEL), F32),
                   sds((1, 128), F32), sds((1, 128), F32)],
        compiler_params=_cparams(("arbitrary",)),
    )(dqr, dkr, dva, *dqb, *dkb, *dvb, qpre, kpre, x, dh1, g1, w_in, cc, ss, gq2, gk2, ones128)


def _dw(a, b, name, relu2=False):
    s_len, ka = a.shape
    n = b.shape[1]
    ts = min(DW_TS, s_len)
    bk = min(ka, 1024)
    bn = n if n % 1024 else 1024

    def body(a_ref, b_ref, o_ref):
        @pl.when(pl.program_id(2) == 0)
        def _():
            o_ref[...] = jnp.zeros_like(o_ref)

        av = a_ref[...]
        if relu2:
            av = jnp.square(jnp.maximum(av, 0.0))
        o_ref[...] += _mm_tn(av.astype(BF16), b_ref[...])

    return pl.pallas_call(
        body, name=name, grid=(ka // bk, n // bn, s_len // ts),
        in_specs=[pl.BlockSpec((ts, bk), lambda i, j, k: (k, i)), pl.BlockSpec((ts, bn), lambda i, j, k: (k, j))],
        out_specs=pl.BlockSpec((bk, bn), lambda i, j, k: (i, j)),
        out_shape=jax.ShapeDtypeStruct((ka, n), F32),
        compiler_params=_cparams(("parallel", "parallel", "arbitrary")),
    )(a, b)


def _attn_a_fwd(qpad, kv):
    s_len = kv.shape[1]
    tq = min(ATT_TQ, s_len)
    tk = min(ATT_TK, s_len)
    rows = GROUP_A * tq

    def body(q_ref, kv_ref, o_ref, lse_ref):
        q = q_ref[...].reshape(rows, 128)

        def step(j, carry):
            m, l, acc = carry
            kvj = kv_ref[0, pl.ds(pl.multiple_of(j * tk, tk), tk), :]
            s = _mm_nt(q, kvj)
            m_new = jnp.maximum(m, jnp.max(s, axis=1, keepdims=True))
            alpha = jnp.exp(m - m_new)
            p = jnp.exp(s - m_new)
            l = alpha * l + jnp.sum(p, axis=1, keepdims=True)
            acc = alpha * acc + _mm(p.astype(BF16), kvj)
            return m_new, l, acc

        init = (jnp.full((rows, 1), -jnp.inf, F32), jnp.zeros((rows, 1), F32), jnp.zeros((rows, 128), F32))
        m, l, acc = lax.fori_loop(0, s_len // tk, step, init)
        o_ref[...] = (acc / l).reshape(GROUP_A, tq, 128)
        lse_ref[...] = (m + jnp.log(l)).reshape(GROUP_A, tq, 1)

    return pl.pallas_call(
        body, name="attn_a_fwd", grid=(N_KV_A, s_len // tq),
        in_specs=[pl.BlockSpec((GROUP_A, tq, 128), lambda g, i: (g, i, 0)),
                  pl.BlockSpec((1, s_len, 128), lambda g, i: (g, 0, 0))],
        out_specs=[pl.BlockSpec((GROUP_A, tq, 128), lambda g, i: (g, i, 0)),
                   pl.BlockSpec((GROUP_A, tq, 1), lambda g, i: (g, i, 0))],
        out_shape=[jax.ShapeDtypeStruct((N_HEADS_A, s_len, 128), F32),
                   jax.ShapeDtypeStruct((N_HEADS_A, s_len, 1), F32)],
        compiler_params=_cparams(("parallel", "parallel")),
    )(qpad, kv)


def _attn_a_bwd(qpad, dopad, kv, lse, delta):
    s_len = kv.shape[1]
    tq = min(ATT_TQ, s_len)
    tk = min(ATT_TK, s_len)
    rows = GROUP_A * tq

    def body(q_ref, do_ref, kv_ref, lse_ref, dl_ref, dq_ref, dkv_ref):
        @pl.when(pl.program_id(1) == 0)
        def _():
            dkv_ref[...] = jnp.zeros_like(dkv_ref)

        q = q_ref[...].reshape(rows, 128)
        do = do_ref[...].reshape(rows, 128)
        lse_t = lse_ref[...].reshape(rows, 1)
        dl_t = dl_ref[...].reshape(rows, 1)

        def step(j, dq):
            span = pl.ds(pl.multiple_of(j * tk, tk), tk)
            kvj = kv_ref[0, span, :]
            p = jnp.exp(_mm_nt(q, kvj) - lse_t)
            ds = (p * (_mm_nt(do, kvj) - dl_t)).astype(BF16)
            dkv_ref[0, span, :] += _mm_tn(ds, q) + _mm_tn(p.astype(BF16), do)
            return dq + _mm(ds, kvj)

        dq = lax.fori_loop(0, s_len // tk, step, jnp.zeros((rows, 128), F32))
        dq_ref[...] = dq.reshape(GROUP_A, tq, 128)

    return pl.pallas_call(
        body, name="attn_a_bwd", grid=(N_KV_A, s_len // tq),
        in_specs=[pl.BlockSpec((GROUP_A, tq, 128), lambda g, i: (g, i, 0)),
                  pl.BlockSpec((GROUP_A, tq, 128), lambda g, i: (g, i, 0)),
                  pl.BlockSpec((1, s_len, 128), lambda g, i: (g, 0, 0)),
                  pl.BlockSpec((GROUP_A, tq, 1), lambda g, i: (g, i, 0)),
                  pl.BlockSpec((GROUP_A, tq, 1), lambda g, i: (g, i, 0))],
        out_specs=[pl.BlockSpec((GROUP_A, tq, 128), lambda g, i: (g, i, 0)),
                   pl.BlockSpec((1, s_len, 128), lambda g, i: (g, 0, 0))],
        out_shape=[jax.ShapeDtypeStruct((N_HEADS_A, s_len, 128), F32),
                   jax.ShapeDtypeStruct((N_KV_A, s_len, 128), F32)],
        compiler_params=_cparams(("parallel", "arbitrary")),
    )(qpad, dopad, kv, lse, delta)


def _band_specs(tq, nblk, width):
    cur = pl.BlockSpec((1, N_HEADS_B, tq, width), lambda r, i: (r, 0, i, 0))
    prev = pl.BlockSpec((1, N_HEADS_B, tq, width), lambda r, i: (r, 0, jnp.maximum(i - 1, 0), 0))
    nxt = pl.BlockSpec((1, N_HEADS_B, tq, width), lambda r, i: (r, 0, jnp.minimum(i + 1, nblk - 1), 0))
    return prev, cur, nxt


def _band(prev_ref, cur_ref, next_ref, h, tq):
    return jnp.concatenate([prev_ref[0, h, tq - HALF_WIN:, :], cur_ref[0, h], next_ref[0, h, :HALF_WIN, :]], axis=0)


def _band_valid(tq, length):
    i = pl.program_id(1)
    qi = lax.broadcasted_iota(jnp.int32, (tq, tq + 2 * HALF_WIN), 0)
    kj = lax.broadcasted_iota(jnp.int32, (tq, tq + 2 * HALF_WIN), 1)
    key = i * tq - HALF_WIN + kj
    return (jnp.abs(kj - HALF_WIN - qi) <= HALF_WIN) & (key >= 0) & (key < length)


def _swa_fwd(q, k, v, bias, tag):
    n_sub, _, length, _ = q.shape
    tq = min(SWA_TQ, length)
    nblk = length // tq
    kprev, kcur, knext = _band_specs(tq, nblk, HEAD_DIM)

    def body(q_ref, kp, kc, kn, vp, vc, vn, b_ref, acc_ref, m_ref, l_ref):
        valid = _band_valid(tq, length)
        for h in range(N_HEADS_B):
            s = _mm_nt(q_ref[0, h], _band(kp, kc, kn, h, tq))
            s = jnp.where(valid, s + b_ref[h], NEG_BIG)
            m = jnp.max(s, axis=1, keepdims=True)
            e = jnp.exp(s - m)
            acc_ref[0, h] = _mm(e.astype(BF16), _band(vp, vc, vn, h, tq))
            m_ref[0, h] = m
            l_ref[0, h] = jnp.sum(e, axis=1, keepdims=True)

    sds = jax.ShapeDtypeStruct
    stat = pl.BlockSpec((1, N_HEADS_B, tq, 1), lambda r, i: (r, 0, i, 0))
    return pl.pallas_call(
        body, name="swa_fwd_" + tag, grid=(n_sub, nblk),
        in_specs=[kcur, kprev, kcur, knext, kprev, kcur, knext, _full(bias.shape)],
        out_specs=[kcur, stat, stat],
        out_shape=[sds(q.shape, F32), sds(q.shape[:3] + (1,), F32), sds(q.shape[:3] + (1,), F32)],
        compiler_params=_cparams(("parallel", "parallel")),
    )(q, k, k, k, v, v, v, bias)


def _swa_bwd_q(q, k, v, dy, lse, delta, bias, tag):
    n_sub, _, length, _ = q.shape
    tq = min(SWA_TQ, length)
    nblk = length // tq
    kprev, kcur, knext = _band_specs(tq, nblk, HEAD_DIM)

    def body(q_ref, kp, kc, kn, vp, vc, vn, dy_ref, lse_ref, dl_ref, b_ref, dq_ref, db_ref):
        @pl.when((pl.program_id(0) == 0) & (pl.program_id(1) == 0))
        def _():
            db_ref[...] = jnp.zeros_like(db_ref)

        valid = _band_valid(tq, length)
        for h in range(N_HEADS_B):
            kb = _band(kp, kc, kn, h, tq)
            s = jnp.where(valid, _mm_nt(q_ref[0, h], kb) + b_ref[h], NEG_BIG)
            p = jnp.exp(s - lse_ref[0, h])
            dp = _mm_nt(dy_ref[0, h].astype(BF16), _band(vp, vc, vn, h, tq))
            ds = p * (dp - dl_ref[0, h])
            db_ref[h] += ds
            dq_ref[0, h] = _mm(ds.astype(BF16), kb)

    stat = pl.BlockSpec((1, N_HEADS_B, tq, 1), lambda r, i: (r, 0, i, 0))
    return pl.pallas_call(
        body, name="swa_bwd_q_" + tag, grid=(n_sub, nblk),
        in_specs=[kcur, kprev, kcur, knext, kprev, kcur, knext, kcur, stat, stat, _full(bias.shape)],
        out_specs=[kcur, _full(bias.shape)],
        out_shape=[jax.ShapeDtypeStruct(q.shape, F32), jax.ShapeDtypeStruct(bias.shape, F32)],
        compiler_params=_cparams(("arbitrary", "arbitrary")),
    )(q, k, k, k, v, v, v, dy, lse, delta, bias)


def _swa_bwd_kv(k_aug, v_aug, q_aug, dy_aug, bias_t, tag):
    n_sub, _, length, _ = k_aug.shape
    tq = min(SWA_TQ, length)
    nblk = length // tq
    kprev, kcur, knext = _band_specs(tq, nblk, 128)

    def body(k_ref, v_ref, qp, qc, qn, dp_, dc_, dn_, b_ref, dk_ref, dv_ref):
        valid = _band_valid(tq, length)
        for h in range(N_HEADS_B):
            qb = _band(qp, qc, qn, h, tq)
            dyb = _band(dp_, dc_, dn_, h, tq)
            pt = jnp.exp(jnp.where(valid, _mm_nt(k_ref[0, h], qb) + b_ref[h], NEG_BIG))
            dst = pt * _mm_nt(v_ref[0, h], dyb)
            dv_ref[0, h] = _mm(pt.astype(BF16), dyb)
            dk_ref[0, h] = _mm(dst.astype(BF16), qb)

    return pl.pallas_call(
        body, name="swa_bwd_kv_" + tag, grid=(n_sub, nblk),
        in_specs=[kcur, kcur, kprev, kcur, knext, kprev, kcur, knext, _full(bias_t.shape)],
        out_specs=[kcur, kcur],
        out_shape=[jax.ShapeDtypeStruct(k_aug.shape, F32), jax.ShapeDtypeStruct(k_aug.shape, F32)],
        compiler_params=_cparams(("parallel", "parallel")),
    )(k_aug, v_aug, q_aug, q_aug, q_aug, dy_aug, dy_aug, dy_aug, bias_t)


BIAS_ROWS = 16
BIAS_TN = 8192


def _bias_tiles(onehot, rel_bias_t):
    n = onehot.shape[0]

    def body(oh_ref, rb_ref, o_ref):
        o_ref[...] = sum(_mm_nt(piece, oh_ref[...]) for piece in _split3(rb_ref[...]))

    return pl.pallas_call(
        body, name="bias_tiles", grid=(n // BIAS_TN,),
        in_specs=[_rows(BIAS_TN, 128), _full((BIAS_ROWS, 128))],
        out_specs=pl.BlockSpec((BIAS_ROWS, BIAS_TN), lambda i: (0, i)),
        out_shape=jax.ShapeDtypeStruct((BIAS_ROWS, n), F32),
        compiler_params=_cparams(("parallel",)),
    )(onehot, rel_bias_t)


def _bias_bwd(onehots, dbias_rows):
    n = onehots[0].shape[0]

    def body(o0, o1, o2, d0, d1, d2, g_ref):
        acc = jnp.zeros((BIAS_ROWS, 128), F32)
        for oh, d in ((o0, d0), (o1, d1), (o2, d2)):
            hi, lo, _ = _split3(d[...])
            acc = acc + _mm(hi, oh[...]) + _mm(lo, oh[...])
        _acc_out(g_ref, acc)

    cols = pl.BlockSpec((BIAS_ROWS, BIAS_TN), lambda i: (0, i))
    return pl.pallas_call(
        body, name="bias_bwd", grid=(n // BIAS_TN,),
        in_specs=[_rows(BIAS_TN, 128)] * 3 + [cols] * 3,
        out_specs=_full((BIAS_ROWS, 128)),
        out_shape=jax.ShapeDtypeStruct((BIAS_ROWS, 128), F32),
        compiler_params=_cparams(("arbitrary",)),
    )(*onehots, *dbias_rows)


def _to_heads(a, n_heads):
    return a.reshape(a.shape[0], n_heads, HEAD_DIM).transpose(1, 0, 2)


def _from_heads(a):
    return a.transpose(1, 0, 2).reshape(a.shape[1], a.shape[0] * HEAD_DIM)


def _to_sub(a, r):
    s_len = a.shape[0]
    w = a.shape[1] // N_HEADS_B
    return a.reshape(s_len // r, r, N_HEADS_B, w).transpose(1, 2, 0, 3)


def _from_sub(a):
    r, h, length, w = a.shape
    return a.transpose(2, 0, 1, 3).reshape(length * r, h * w)


def _pad_lanes(a, left):
    z = jnp.zeros_like(a)
    return jnp.concatenate([z, a] if left else [a, z], axis=-1)


def _stat_to_sub(a, r):
    return a.reshape(a.shape[0], a.shape[1] // r, r, 1).transpose(2, 0, 1, 3)


def _stat_from_sub(a):
    r, h, length, _ = a.shape
    return a.transpose(1, 2, 0, 3).reshape(h, length * r, 1)


def _augment(a, stat):
    if stat is None:
        extra = jnp.ones(a.shape[:-1] + (3,), BF16)
    else:
        extra = jnp.concatenate(_split3(-stat), axis=-1)
    return jnp.concatenate([a, extra, jnp.zeros(a.shape[:-1] + (HEAD_DIM - 3,), BF16)], axis=-1)


def _local_step(x, p, tgt, w_in, w_out, w_ff1, w_ff2, w_gate, w_ple, g_attn_pre, g_q, g_k, g_out_a, g_out_b,
                g_attn_post, rel_bias, g_mlp_pre, g_mlp_post, g_ple):
    s_len = x.shape[0]
    cc, ss = _rope_tables(s_len)
    gq2 = jnp.concatenate([g_q, g_q], axis=-1)
    gk2 = jnp.concatenate([g_k, g_k], axis=-1)
    ones128 = _group_ones(128)
    rel_bias_t = jnp.zeros((BIAS_ROWS, 128), F32).at[:N_HEADS_B, :N_BUCKETS].set(rel_bias.T)

    xn1, qpre, kpre, qa, ka, va, qb, kb, vb = _in_proj(x, g_attn_pre, w_in, cc, ss, gq2, gk2, ones128)

    qpad = _pad_lanes(_to_heads(qa, N_HEADS_A), left=False)
    kv = jnp.concatenate([_to_heads(ka, N_KV_A), _to_heads(va, N_KV_A)], axis=-1)
    oa, lse_a = _attn_a_fwd(qpad, kv)
    ya = _from_heads(oa[:, :, HEAD_DIM:])

    tiles, subs, stats = [], [], []
    for r in DILATIONS:
        tq = min(SWA_TQ, s_len // r)
        onehot = _bucket_onehot(tq, r)
        bias = _bias_tiles(onehot, rel_bias_t)[:N_HEADS_B].reshape(N_HEADS_B, tq, tq + 2 * HALF_WIN)
        qs, ks, vs = _to_sub(qb, r), _to_sub(kb, r), _to_sub(vb, r)
        acc, m, l = _swa_fwd(qs, ks, vs, bias, str(r))
        tiles.append((onehot, bias))
        subs.append((qs, ks, vs))
        stats.append((_from_sub(acc), _stat_from_sub(m), _stat_from_sub(l)))
    yb, lse_b = _merge_b([s[0] for s in stats], [s[1] for s in stats], [s[2] for s in stats])

    ycat, y2, h1, xn2 = _out_proj(ya, yb, x, g_out_a, g_out_b, w_out, g_attn_post, g_mlp_pre)
    u = _ff1(xn2, w_ff1)
    f2, h2, xn3 = _ff2(u, w_ff2, h1, g_mlp_post, g_ple)
    dh2, df2, dgl, dpp, loss, dg_ple, dg_mlp_post = _ple_loss(xn3, p, h2, f2, tgt, w_gate, w_ple, g_ple, g_mlp_post)

    grads = {"g_ple": dg_ple, "g_mlp_post": dg_mlp_post}
    grads["w_ple_gate"] = _dw(xn3, dgl, "dw_gate")
    grads["w_ple_proj"] = _dw(p, dpp, "dw_ple")
    grads["w_ff2"] = _dw(u, df2, "dw_ff2", relu2=True)
    du = _ff2_bwd(df2, w_ff2, u)
    grads["w_ff1"] = _dw(xn2, du, "dw_ff1")
    dh1, dy2, grads["g_mlp_pre"], grads["g_attn_post"] = _ff1_bwd(du, w_ff1, dh2, h1, y2, g_mlp_pre, g_attn_post)
    grads["w_out"] = _dw(ycat, dy2, "dw_out")
    dya, dyb, delta_a, delta_b, grads["g_out_a"], grads["g_out_b"] = _out_proj_bwd(dy2, w_out, ya, yb, g_out_a,
                                                                                 g_out_b)

    dopad = _pad_lanes(_to_heads(dya, N_HEADS_A).astype(BF16), left=True)
    dq_a, dkv_a = _attn_a_bwd(qpad, dopad, kv, lse_a, delta_a)
    dqr = _from_heads(dq_a[:, :, :HEAD_DIM])
    dkr = _from_heads(dkv_a[:, :, :HEAD_DIM])
    dva = _from_heads(dkv_a[:, :, HEAD_DIM:])

    dqs, dks, dvs, dbias_rows = [], [], [], []
    for r, (onehot, bias), (qs, ks, vs) in zip(DILATIONS, tiles, subs):
        dys = _to_sub(dyb, r)
        lse_s, dl_s = _stat_to_sub(lse_b, r), _stat_to_sub(delta_b, r)
        dq_s, dbias = _swa_bwd_q(qs, ks, vs, dys, lse_s, dl_s, bias, str(r))
        bias_t = jnp.flip(bias, axis=(1, 2))
        dk_s, dv_s = _swa_bwd_kv(_augment(ks, None), _augment(vs, None), _augment(qs, lse_s),
                                 _augment(dys.astype(BF16), dl_s), bias_t, str(r))
        dbias_rows.append(jnp.pad(dbias.reshape(N_HEADS_B, -1), ((0, BIAS_ROWS - N_HEADS_B), (0, 0))))
        dqs.append(_from_sub(dq_s))
        dks.append(_from_sub(dk_s[..., :HEAD_DIM]))
        dvs.append(_from_sub(dv_s[..., :HEAD_DIM]))
    grads["rel_bias"] = _bias_bwd([t[0] for t in tiles], dbias_rows)[:N_HEADS_B, :N_BUCKETS].T

    dproj, grad_x, grads["g_attn_pre"], dgq2, dgk2 = _in_proj_bwd(
        dqr, dkr, dva, dqs, dks, dvs, qpre, kpre, x, dh1, g_attn_pre, w_in, cc, ss, gq2, gk2, ones128)
    grads["g_q"] = dgq2[:, :HEAD_DIM] + dgq2[:, HEAD_DIM:]
    grads["g_k"] = dgk2[:, :HEAD_DIM] + dgk2[:, HEAD_DIM:]
    grads["w_in"] = _dw(xn1, dproj, "dw_in")
    return loss, grad_x, grads


ANY = pl.BlockSpec(memory_space=pl.ANY)


def _position():
    return lax.axis_index("x"), lax.axis_index("y"), lax.axis_index("c")


def _other_chips(x, y):
    return [(2 * (1 - x) + y, (1 - x, y)), (2 * x + (1 - y), (x, 1 - y)), (2 * (1 - x) + (1 - y), (1 - x, 1 - y))]


def _cast_shards(shards):
    def body(*refs):
        n = len(refs) // 2
        for i_ref, o_ref in zip(refs[:n], refs[n:]):
            o_ref[...] = i_ref[...].astype(BF16)

    return pl.pallas_call(
        body, name="cast_shards",
        in_specs=[pl.BlockSpec(memory_space=pltpu.VMEM)] * len(shards),
        out_specs=[pl.BlockSpec(memory_space=pltpu.VMEM)] * len(shards),
        out_shape=[jax.ShapeDtypeStruct(s.shape, BF16) for s in shards],
        compiler_params=_cparams(),
    )(*shards)


def _gather_weights(shards):
    n = len(shards)

    def body(*refs):
        ins, outs = refs[:n], refs[n:2 * n]
        send_sems, recv_sems, local_sems = refs[2 * n:]
        x, y, c = _position()
        me = 2 * x + y
        copies = []
        for a in range(n):
            local = pltpu.make_async_copy(ins[a], outs[a].at[me], local_sems.at[a])
            local.start()
            copies.append(local)
        sends = []
        for k, (_, chip) in enumerate(_other_chips(x, y)):
            for a in range(n):
                cp = pltpu.make_async_remote_copy(ins[a], outs[a].at[me], send_sems.at[k, a], recv_sems.at[k, a],
                                                  device_id=(*chip, c), device_id_type=MESH)
                cp.start()
                sends.append(cp)
        for k, (num, chip) in enumerate(_other_chips(x, y)):
            for a in range(n):
                pltpu.make_async_remote_copy(ins[a], outs[a].at[num], send_sems.at[k, a], recv_sems.at[k, a],
                                             device_id=(*chip, c), device_id_type=MESH).wait_recv()
        for cp in sends:
            cp.wait_send()
        for cp in copies:
            cp.wait()

    return pl.pallas_call(
        body, name="gather_weights",
        in_specs=[ANY] * n, out_specs=[ANY] * n,
        out_shape=[jax.ShapeDtypeStruct((N_CHIPS,) + s.shape, s.dtype) for s in shards],
        scratch_shapes=[pltpu.SemaphoreType.DMA((3, n)), pltpu.SemaphoreType.DMA((3, n)),
                        pltpu.SemaphoreType.DMA((n,))],
    )(*shards)


def _send_sibling_half(grads):
    n = len(grads)

    def body(*refs):
        ins, outs = refs[:n], refs[n:2 * n]
        send_sems, recv_sems = refs[2 * n:]
        x, y, c = _position()
        copies = []
        for a in range(n):
            half = ins[a].shape[1] // 2
            theirs = ins[a].at[:, pl.ds(pl.multiple_of((1 - c) * half, 8), half), :]
            cp = pltpu.make_async_remote_copy(theirs, outs[a], send_sems.at[a], recv_sems.at[a],
                                              device_id=(x, y, 1 - c), device_id_type=MESH)
            cp.start()
            copies.append(cp)
        for cp in copies:
            cp.wait()

    return pl.pallas_call(
        body, name="send_sibling_half",
        in_specs=[ANY] * n, out_specs=[ANY] * n,
        out_shape=[jax.ShapeDtypeStruct((g.shape[0], g.shape[1] // 2, g.shape[2]), g.dtype) for g in grads],
        scratch_shapes=[pltpu.SemaphoreType.DMA((n,)), pltpu.SemaphoreType.DMA((n,))],
    )(*grads)


def _scatter_to_chips(pairs):
    n = len(pairs)

    def body(*refs):
        ins, outs = refs[:n], refs[n:2 * n]
        send_sems, recv_sems, local_sems = refs[2 * n:]
        x, y, c = _position()
        me = 2 * x + y
        copies = []
        for a in range(n):
            local = pltpu.make_async_copy(ins[a].at[me], outs[a].at[me], local_sems.at[a])
            local.start()
            copies.append(local)
        sends = []
        for k, (num, chip) in enumerate(_other_chips(x, y)):
            for a in range(n):
                cp = pltpu.make_async_remote_copy(ins[a].at[num], outs[a].at[me], send_sems.at[k, a],
                                                  recv_sems.at[k, a], device_id=(*chip, c), device_id_type=MESH)
                cp.start()
                sends.append(cp)
        for k, (num, chip) in enumerate(_other_chips(x, y)):
            for a in range(n):
                pltpu.make_async_remote_copy(ins[a].at[me], outs[a].at[num], send_sems.at[k, a], recv_sems.at[k, a],
                                             device_id=(*chip, c), device_id_type=MESH).wait_recv()
        for cp in sends:
            cp.wait_send()
        for cp in copies:
            cp.wait()

    return pl.pallas_call(
        body, name="scatter_to_chips",
        in_specs=[ANY] * n, out_specs=[ANY] * n,
        out_shape=[jax.ShapeDtypeStruct(g.shape, g.dtype) for g in pairs],
        scratch_shapes=[pltpu.SemaphoreType.DMA((3, n)), pltpu.SemaphoreType.DMA((3, n)),
                        pltpu.SemaphoreType.DMA((n,))],
    )(*pairs)


def _exchange_halves(halves):
    n = len(halves)

    def body(*refs):
        ins, outs = refs[:n], refs[n:2 * n]
        send_sems, recv_sems, local_sems = refs[2 * n:]
        x, y, c = _position()
        copies = []
        for a in range(n):
            local = pltpu.make_async_copy(ins[a], outs[a].at[c], local_sems.at[a])
            local.start()
            cp = pltpu.make_async_remote_copy(ins[a], outs[a].at[c], send_sems.at[a], recv_sems.at[a],
                                              device_id=(x, y, 1 - c), device_id_type=MESH)
            cp.start()
            copies += [local, cp]
        for a in range(n):
            pltpu.make_async_remote_copy(ins[a], outs[a].at[1 - c], send_sems.at[a], recv_sems.at[a],
                                         device_id=(x, y, 1 - c), device_id_type=MESH).wait_recv()
        for a in range(n):
            copies[2 * a].wait()
            copies[2 * a + 1].wait_send()

    return pl.pallas_call(
        body, name="exchange_halves",
        in_specs=[ANY] * n, out_specs=[ANY] * n,
        out_shape=[jax.ShapeDtypeStruct((2,) + h.shape, h.dtype) for h in halves],
        scratch_shapes=[pltpu.SemaphoreType.DMA((n,)), pltpu.SemaphoreType.DMA((n,)),
                        pltpu.SemaphoreType.DMA((n,))],
    )(*halves)


def _allreduce_small(v):
    def body(v_ref, o_ref, buf, send_sems, recv_sems):
        x, y, c = _position()
        me = 4 * x + 2 * y + c
        peers = [(1 - x, y, c), (x, 1 - y, c), (x, y, 1 - c), (1 - x, 1 - y, c), (1 - x, y, 1 - c), (x, 1 - y, 1 - c),
                 (1 - x, 1 - y, 1 - c)]
        num = lambda d: 4 * d[0] + 2 * d[1] + d[2]
        buf[me] = v_ref[...]
        sends = []
        for k, peer in enumerate(peers):
            cp = pltpu.make_async_remote_copy(v_ref, buf.at[me], send_sems.at[k], recv_sems.at[k], device_id=peer,
                                              device_id_type=MESH)
            cp.start()
            sends.append(cp)
        for k, peer in enumerate(peers):
            pltpu.make_async_remote_copy(v_ref, buf.at[num(peer)], send_sems.at[k], recv_sems.at[k], device_id=peer,
                                         device_id_type=MESH).wait_recv()
        for cp in sends:
            cp.wait_send()
        total = buf[0]
        for d in range(1, 8):
            total = total + buf[d]
        o_ref[...] = total

    return pl.pallas_call(
        body, name="allreduce_small",
        in_specs=[pl.BlockSpec(memory_space=pltpu.VMEM)], out_specs=pl.BlockSpec(memory_space=pltpu.VMEM),
        out_shape=jax.ShapeDtypeStruct(v.shape, v.dtype),
        scratch_shapes=[pltpu.VMEM((8,) + v.shape, v.dtype), pltpu.SemaphoreType.DMA((7,)),
                        pltpu.SemaphoreType.DMA((7,))],
    )(v)


def _sum_leading(a, name):
    k, r, c = a.shape
    tr = min(r, 256)

    def body(a_ref, o_ref):
        total = a_ref[0]
        for i in range(1, k):
            total = total + a_ref[i]
        o_ref[...] = total

    return pl.pallas_call(
        body, name=name, grid=(r // tr,),
        in_specs=[pl.BlockSpec((k, tr, c), lambda i: (0, i, 0))],
        out_specs=pl.BlockSpec((tr, c), lambda i: (i, 0)),
        out_shape=jax.ShapeDtypeStruct((r, c), a.dtype),
        compiler_params=_cparams(("parallel",)),
    )(a)


def _add(a, b, name):
    k, r, c = a.shape
    tr = min(r, 256)
    spec = pl.BlockSpec((k, tr, c), lambda i: (0, i, 0))

    def body(a_ref, b_ref, o_ref):
        o_ref[...] = a_ref[...] + b_ref[...]

    return pl.pallas_call(
        body, name=name, grid=(r // tr,), in_specs=[spec, spec], out_specs=spec,
        out_shape=jax.ShapeDtypeStruct(a.shape, a.dtype), compiler_params=_cparams(("parallel",)),
    )(a, b)


def _adamw(w, g, m, v, name):
    r, c = w.shape
    tr = min(r, 256)
    spec = pl.BlockSpec((tr, c), lambda i: (i, 0))

    def body(w_ref, g_ref, m_ref, v_ref, d_ref, nm_ref, nv_ref):
        gv = g_ref[...]
        nm = ADAM_B1 * m_ref[...] + (1.0 - ADAM_B1) * gv
        nv = ADAM_B2 * v_ref[...] + (1.0 - ADAM_B2) * jnp.square(gv)
        m_hat = nm / (1.0 - ADAM_B1 ** ADAM_STEP)
        v_hat = nv / (1.0 - ADAM_B2 ** ADAM_STEP)
        d_ref[...] = -ADAM_LR * (m_hat / (jnp.sqrt(v_hat) + ADAM_EPS) + ADAM_WD * w_ref[...])
        nm_ref[...] = nm
        nv_ref[...] = nv

    return pl.pallas_call(
        body, name=name, grid=(r // tr,), in_specs=[spec] * 4, out_specs=[spec] * 3,
        out_shape=[jax.ShapeDtypeStruct(w.shape, F32)] * 3, compiler_params=_cparams(("parallel",)),
    )(w, g, m, v)


MATRICES = ("w_in", "w_out", "w_ff1", "w_ff2", "w_ple_gate", "w_ple_proj")
COLUMN_SHARDED = ("w_in", "w_ff1", "w_ple_proj")
SMALL = ("g_attn_pre", "g_q", "g_k", "g_out_a", "g_out_b", "g_attn_post", "rel_bias", "g_mlp_pre", "g_mlp_post",
         "g_ple")
WEIGHT_ORDER = ("w_in", "g_attn_pre", "g_q", "g_k", "g_out_a", "g_out_b", "w_out", "g_attn_post", "rel_bias",
                "g_mlp_pre", "w_ff1", "w_ff2", "g_mlp_post", "g_ple", "w_ple_gate", "w_ple_proj")
PACK_ROWS, PACK_COLS = 8, 1024


def _pack_small(values, extra=None):
    flat = [values[n].reshape(-1) for n in SMALL]
    used = sum(f.shape[0] for f in flat)
    tail = jnp.zeros((PACK_ROWS * PACK_COLS - used - 1,), F32)
    last = jnp.zeros((1,), F32) if extra is None else extra.reshape(1)
    return jnp.concatenate(flat + [tail, last]).reshape(PACK_ROWS, PACK_COLS)


def _unpack_small(packed, like):
    flat = packed.reshape(-1)
    out, o = {}, 0
    for n in SMALL:
        size = like[n].size
        out[n] = flat[o:o + size].reshape(like[n].shape)
        o += size
    return out, flat[-1]


def kernel(x, p, w_in, g_attn_pre, g_q, g_k, g_out_a, g_out_b, w_out, g_attn_post, rel_bias, g_mlp_pre, w_ff1, w_ff2, g_mlp_post, g_ple, w_ple_gate, w_ple_proj, loss_target, m_w_in, m_g_attn_pre, m_g_q, m_g_k, m_g_out_a, m_g_out_b, m_w_out, m_g_attn_post, m_rel_bias, m_g_mlp_pre, m_w_ff1, m_w_ff2, m_g_mlp_post, m_g_ple, m_w_ple_gate, m_w_ple_proj, v_w_in, v_g_attn_pre, v_g_q, v_g_k, v_g_out_a, v_g_out_b, v_w_out, v_g_attn_post, v_rel_bias, v_g_mlp_pre, v_w_ff1, v_w_ff2, v_g_mlp_post, v_g_ple, v_w_ple_gate, v_w_ple_proj):
    given = dict(locals())
    weights = {n: given[n] for n in WEIGHT_ORDER}
    shards = {n: weights[n][0] for n in MATRICES}

    gathered = _gather_weights(_cast_shards([shards[n] for n in MATRICES]))
    whole = {}
    for n, g in zip(MATRICES, gathered):
        if n in COLUMN_SHARDED:
            whole[n] = g.transpose(1, 0, 2).reshape(g.shape[1], N_CHIPS * g.shape[2])
        else:
            whole[n] = g.reshape(N_CHIPS * g.shape[1], g.shape[2])

    loss, grad_x, grads = _local_step(
        x[0], p[0, 0], loss_target[0], whole["w_in"], whole["w_out"], whole["w_ff1"], whole["w_ff2"],
        whole["w_ple_gate"], whole["w_ple_proj"], g_attn_pre, g_q, g_k, g_out_a, g_out_b, g_attn_post, rel_bias,
        g_mlp_pre, g_mlp_post, g_ple)

    by_chip = []
    for n in MATRICES:
        g = grads[n]
        if n in COLUMN_SHARDED:
            by_chip.append(g.reshape(g.shape[0], N_CHIPS, g.shape[1] // N_CHIPS).transpose(1, 0, 2))
        else:
            by_chip.append(g.reshape(N_CHIPS, g.shape[0] // N_CHIPS, g.shape[1]))
    from_sibling = _send_sibling_half(by_chip)
    c = lax.axis_index("c")
    pairs = []
    for n, g, other in zip(MATRICES, by_chip, from_sibling):
        half = g.shape[1] // 2
        mine = lax.dynamic_slice_in_dim(g, c * half, half, axis=1)
        pairs.append(_add(mine, other, "pair_sum_" + n))
    from_chips = _scatter_to_chips(pairs)
    halves = [_sum_leading(g, "chip_sum_" + n) for n, g in zip(MATRICES, from_chips)]
    both = _exchange_halves(halves)
    grad_w = {n: g.reshape(2 * g.shape[1], g.shape[2]) for n, g in zip(MATRICES, both)}

    small_like = {n: weights[n] for n in SMALL}
    reduced = _allreduce_small(_pack_small({n: grads[n] for n in SMALL}, extra=loss))
    grad_small, loss_total = _unpack_small(reduced, small_like)

    delta, new_m, new_v = {}, {}, {}
    for n in MATRICES:
        d, nm, nv = _adamw(shards[n], grad_w[n], given["m_" + n][0], given["v_" + n][0], "adamw_" + n)
        delta[n], new_m[n], new_v[n] = d[None], nm[None], nv[None]
        grad_w[n] = grad_w[n][None]
    d, nm, nv = _adamw(_pack_small(small_like), reduced, _pack_small({n: given["m_" + n] for n in SMALL}),
                       _pack_small({n: given["v_" + n] for n in SMALL}), "adamw_small")
    d_small, nm_small, nv_small = (_unpack_small(a, small_like)[0] for a in (d, nm, nv))
    for n in SMALL:
        grad_w[n], delta[n], new_m[n], new_v[n] = grad_small[n], d_small[n], nm_small[n], nv_small[n]

    return (loss_total, grad_x[None], *[grad_w[n] for n in WEIGHT_ORDER], *[delta[n] for n in WEIGHT_ORDER],
            *[new_m[n] for n in WEIGHT_ORDER], *[new_v[n] for n in WEIGHT_ORDER])
```

```python
import functools
import math

import jax
import jax.numpy as jnp
from jax import lax
from jax.experimental import pallas as pl
from jax.experimental.pallas import tpu as pltpu

F32 = jnp.float32
BF16 = jnp.bfloat16

D_MODEL = 1024
HEAD_DIM = 64
N_HEADS_A = 8
N_KV_A = 2
GROUP_A = N_HEADS_A // N_KV_A
N_HEADS_B = 8
D_A = N_HEADS_A * HEAD_DIM
D_KV_A = N_KV_A * HEAD_DIM
D_B = N_HEADS_B * HEAD_DIM
D_IN = D_A + 2 * D_KV_A + 3 * D_B
D_FF = 4 * D_MODEL
D_PLE = 256
GRID_W = 64
ROPE_THETA = 10000.0
DILATIONS = (1, 4, 16)
HALF_WIN = 64
N_BUCKETS = 32
MAX_DISTANCE = 1024
EPS = 1e-6
NEG_BIG = -1e30
Q_SCALE = HEAD_DIM ** -0.5

ADAM_LR = 0.001
ADAM_B1 = 0.9
ADAM_B2 = 0.999
ADAM_EPS = 1e-08
ADAM_WD = 0.01
ADAM_STEP = 10

N_CHIPS = 4
MESH = pl.DeviceIdType.MESH

ROW_TILE = 512
ATT_TQ = 256
ATT_TK_FWD = 2048
ATT_TK_BWD = 1024
SWA_TQ = 256
DW_TS = 1024
VMEM_LIMIT = 56 * 1024 * 1024

NT = (((1,), (1,)), ((), ()))
TN = (((0,), (0,)), ((), ()))


def _cparams(sem=None, vmem=VMEM_LIMIT):
    return pltpu.CompilerParams(dimension_semantics=sem, vmem_limit_bytes=vmem)


def _full(shape):
    n = len(shape)
    return pl.BlockSpec(shape, lambda *_: (0,) * n)


def _rows(tm, width):
    return pl.BlockSpec((tm, width), lambda i: (i, 0))


def _split3(a):
    a1 = a.astype(BF16)
    r = a - a1.astype(F32)
    a2 = r.astype(BF16)
    a3 = (r - a2.astype(F32)).astype(BF16)
    return a1, a2, a3


def _xdot(a, sel):
    a1, a2, a3 = _split3(a)
    d = lambda p: jnp.dot(p, sel, preferred_element_type=F32)
    return d(a1) + d(a2) + d(a3)


def _mm(a, b):
    return jnp.dot(a, b, preferred_element_type=F32)


def _mm_nt(a, b):
    return lax.dot_general(a, b, NT, preferred_element_type=F32)


def _mm_tn(a, b):
    return lax.dot_general(a, b, TN, preferred_element_type=F32)


def _rms_stats(x):
    r = lax.rsqrt(jnp.mean(x * x, axis=-1, keepdims=True) + EPS)
    return x * r, r


def _rms_bwd(dy, xh, r, g):
    gdy = dy * g
    dx = r * (gdy - xh * jnp.mean(gdy * xh, axis=-1, keepdims=True))
    dg = jnp.sum(dy * xh, axis=0, keepdims=True)
    return dx, dg


def _acc_out(ref, val):
    @pl.when(pl.program_id(0) == 0)
    def _():
        ref[...] = jnp.zeros_like(ref)

    ref[...] += val


def _swap_halves(x, first_half):
    return jnp.where(first_half, pltpu.roll(x, 96, 1), pltpu.roll(x, 32, 1))


def _first_half_mask(shape):
    return (lax.broadcasted_iota(jnp.int32, shape, 1) % HEAD_DIM) < (HEAD_DIM // 2)


def _rope_tables(s_len):
    t = jnp.arange(s_len)
    row = (t // GRID_W).astype(F32)
    col = (t % GRID_W).astype(F32)
    n_axis = HEAD_DIM // 4
    inv_freq = ROPE_THETA ** (-jnp.arange(n_axis, dtype=F32) / n_axis)
    ang = jnp.concatenate([row[:, None] * inv_freq, col[:, None] * inv_freq], axis=-1)
    c, s = jnp.cos(ang), jnp.sin(ang)
    cc = jnp.concatenate([c, c, c, c], axis=-1)
    ss = jnp.concatenate([-s, s, -s, s], axis=-1)
    return cc, ss


def _group_ones(width):
    i = jnp.arange(width)
    return (i[:, None] // HEAD_DIM == i[None, :] // HEAD_DIM).astype(BF16)


def _t5_bucket(rel):
    nb = N_BUCKETS // 2
    max_exact = nb // 2
    side = jnp.where(rel > 0, nb, 0)
    n = jnp.abs(rel)
    large = max_exact + (jnp.log(jnp.maximum(n, max_exact).astype(F32) / max_exact)
                         / math.log(MAX_DISTANCE / max_exact) * (nb - max_exact)).astype(jnp.int32)
    large = jnp.minimum(large, nb - 1)
    return side + jnp.where(n < max_exact, n, large)


def _bucket_onehot(tq, dilation):
    qi = jnp.arange(tq)
    kj = jnp.arange(tq + 2 * HALF_WIN)
    rel = kj[None, :] - HALF_WIN - qi[:, None]
    bucket = _t5_bucket(rel * dilation).reshape(-1)
    return (bucket[:, None] == jnp.arange(128)[None, :]).astype(BF16)


def _in_proj(x, g1, w_in, cc, ss, gq2, gk2, ones128):
    s_len = x.shape[0]
    tm = min(ROW_TILE, s_len)

    def body(x_ref, g_ref, w_ref, cc_ref, ss_ref, gq_ref, gk_ref, one_ref,
             xn_ref, qpre_ref, kpre_ref, qa_ref, ka_ref, va_ref, qb_ref, kb_ref, vb_ref):
        xh, _ = _rms_stats(x_ref[...])
        xn = (xh * g_ref[...]).astype(BF16)
        xn_ref[...] = xn
        proj = _mm(xn, w_ref[...])
        first_half = _first_half_mask((tm, 128))
        ones = one_ref[...]
        cc_t, ss_t = cc_ref[...], ss_ref[...]

        def norm_rope(xc, gain):
            ms = _xdot(xc * xc, ones) * (1.0 / HEAD_DIM)
            y = xc * lax.rsqrt(ms + EPS) * gain
            return y * cc_t + _swap_halves(y, first_half) * ss_t

        qpre_ref[...] = proj[:, :D_A]
        kpre_ref[...] = proj[:, D_A:D_A + D_KV_A]
        for c in range(D_A // 128):
            y = norm_rope(proj[:, 128 * c:128 * (c + 1)], gq_ref[...])
            qa_ref[:, 128 * c:128 * (c + 1)] = (y * Q_SCALE).astype(BF16)
        ka_ref[...] = norm_rope(proj[:, D_A:D_A + D_KV_A], gk_ref[...]).astype(BF16)
        o = D_A + D_KV_A
        va_ref[...] = proj[:, o:o + D_KV_A].astype(BF16)
        o += D_KV_A
        qb_ref[...] = (proj[:, o:o + D_B] * Q_SCALE).astype(BF16)
        kb_ref[...] = proj[:, o + D_B:o + 2 * D_B].astype(BF16)
        vb_ref[...] = proj[:, o + 2 * D_B:o + 3 * D_B].astype(BF16)

    sds = jax.ShapeDtypeStruct
    return pl.pallas_call(
        body, name="in_proj", grid=(s_len // tm,),
        in_specs=[_rows(tm, D_MODEL), _full((1, D_MODEL)), _full((D_MODEL, D_IN)), _rows(tm, 128), _rows(tm, 128),
                  _full((1, 128)), _full((1, 128)), _full((128, 128))],
        out_specs=[_rows(tm, D_MODEL), _rows(tm, D_A), _rows(tm, D_KV_A), _rows(tm, D_A), _rows(tm, D_KV_A),
                   _rows(tm, D_KV_A), _rows(tm, D_B), _rows(tm, D_B), _rows(tm, D_B)],
        out_shape=[sds((s_len, D_MODEL), BF16), sds((s_len, D_A), F32), sds((s_len, D_KV_A), F32),
                   sds((s_len, D_A), BF16), sds((s_len, D_KV_A), BF16), sds((s_len, D_KV_A), BF16),
                   sds((s_len, D_B), BF16), sds((s_len, D_B), BF16), sds((s_len, D_B), BF16)],
        compiler_params=_cparams(("parallel",)),
    )(x, g1, w_in, cc, ss, gq2, gk2, ones128)


def _stat_spec(tm):
    return pl.BlockSpec((N_HEADS_B, tm, 1), lambda i: (0, i, 0))


def _merge_b(accs, ms, ls):
    s_len = accs[0].shape[0]
    tm = min(ROW_TILE, s_len)

    def body(a0, a1, a2, m0, m1, m2, l0, l1, l2, yb_ref, lse_ref):
        coef = []
        for h in range(N_HEADS_B):
            m_all = jnp.maximum(jnp.maximum(m0[h], m1[h]), m2[h])
            w = [jnp.exp(m[h] - m_all) for m in (m0, m1, m2)]
            den = w[0] * l0[h] + w[1] * l1[h] + w[2] * l2[h]
            inv = 1.0 / den
            coef.append([wp * inv for wp in w])
            lse_ref[h] = m_all + jnp.log(den)
        low = lax.broadcasted_iota(jnp.int32, (tm, 128), 1) < HEAD_DIM
        for c in range(D_B // 128):
            sl = slice(128 * c, 128 * (c + 1))
            yb = jnp.zeros((tm, 128), F32)
            for p_i, ap in enumerate((a0, a1, a2)):
                yb = yb + ap[:, sl] * jnp.where(low, coef[2 * c][p_i], coef[2 * c + 1][p_i])
            yb_ref[:, sl] = yb

    return pl.pallas_call(
        body, name="merge_b", grid=(s_len // tm,),
        in_specs=[_rows(tm, D_B)] * 3 + [_stat_spec(tm)] * 6,
        out_specs=[_rows(tm, D_B), _stat_spec(tm)],
        out_shape=[jax.ShapeDtypeStruct((s_len, D_B), F32), jax.ShapeDtypeStruct((N_HEADS_B, s_len, 1), F32)],
        compiler_params=_cparams(("parallel",)),
    )(*accs, *ms, *ls)


def _out_proj(ya, yb, x, g_a, g_b, w_out, g_post, g_mlp_pre):
    s_len = x.shape[0]
    tm = min(ROW_TILE, s_len)

    def body(ya_ref, yb_ref, x_ref, ga_ref, gb_ref, w_ref, gp_ref, gm_ref, ycat_ref, y2_ref, h1_ref, xn2_ref):
        ah, _ = _rms_stats(ya_ref[...])
        bh, _ = _rms_stats(yb_ref[...])
        ycat = jnp.concatenate([ah * ga_ref[...], bh * gb_ref[...]], axis=-1).astype(BF16)
        ycat_ref[...] = ycat
        y2 = _mm(ycat, w_ref[...])
        y2_ref[...] = y2
        y2h, _ = _rms_stats(y2)
        h1 = x_ref[...] + y2h * gp_ref[...]
        h1_ref[...] = h1
        h1h, _ = _rms_stats(h1)
        xn2_ref[...] = (h1h * gm_ref[...]).astype(BF16)

    sds = jax.ShapeDtypeStruct
    return pl.pallas_call(
        body, name="out_proj", grid=(s_len // tm,),
        in_specs=[_rows(tm, D_A), _rows(tm, D_B), _rows(tm, D_MODEL), _full((1, D_A)), _full((1, D_B)),
                  _full((D_MODEL, D_MODEL)), _full((1, D_MODEL)), _full((1, D_MODEL))],
        out_specs=[_rows(tm, D_MODEL)] * 4,
        out_shape=[sds((s_len, D_MODEL), BF16), sds((s_len, D_MODEL), F32), sds((s_len, D_MODEL), F32),
                   sds((s_len, D_MODEL), BF16)],
        compiler_params=_cparams(("parallel",)),
    )(ya, yb, x, g_a, g_b, w_out, g_post, g_mlp_pre)


def _ff1(xn2, w_ff1):
    s_len = xn2.shape[0]
    tm = min(ROW_TILE, s_len)

    def body(x_ref, w_ref, u_ref):
        u_ref[...] = _mm(x_ref[...], w_ref[...])

    return pl.pallas_call(
        body, name="ff1", grid=(s_len // tm,),
        in_specs=[_rows(tm, D_MODEL), _full((D_MODEL, D_FF))],
        out_specs=_rows(tm, D_FF),
        out_shape=jax.ShapeDtypeStruct((s_len, D_FF), F32),
        compiler_params=_cparams(("parallel",)),
    )(xn2, w_ff1)


def _ff2(u, w_ff2, h1, g_post, g_ple):
    s_len = u.shape[0]
    tm = min(ROW_TILE, s_len)

    def body(u_ref, w_ref, h1_ref, gp_ref, gl_ref, f2_ref, h2_ref, xn3_ref):
        f = jnp.square(jnp.maximum(u_ref[...], 0.0)).astype(BF16)
        f2 = _mm(f, w_ref[...])
        f2_ref[...] = f2
        f2h, _ = _rms_stats(f2)
        h2 = h1_ref[...] + f2h * gp_ref[...]
        h2_ref[...] = h2
        h2h, _ = _rms_stats(h2)
        xn3_ref[...] = (h2h * gl_ref[...]).astype(BF16)

    sds = jax.ShapeDtypeStruct
    return pl.pallas_call(
        body, name="ff2", grid=(s_len // tm,),
        in_specs=[_rows(tm, D_FF), _full((D_FF, D_MODEL)), _rows(tm, D_MODEL), _full((1, D_MODEL)),
                  _full((1, D_MODEL))],
        out_specs=[_rows(tm, D_MODEL)] * 3,
        out_shape=[sds((s_len, D_MODEL), F32), sds((s_len, D_MODEL), F32), sds((s_len, D_MODEL), BF16)],
        compiler_params=_cparams(("parallel",)),
    )(u, w_ff2, h1, g_post, g_ple)


def _ple_loss(xn3, p, h2, f2, tgt, w_gate, w_ple, g_ple, g_mlp_post):
    s_len = h2.shape[0]
    tm = min(ROW_TILE, s_len)

    def body(xn3_ref, p_ref, h2_ref, f2_ref, t_ref, wg_ref, wp_ref, gl_ref, gp_ref,
             dh2_ref, df2_ref, dgl_ref, dpp_ref, loss_ref, dgple_ref, dgpost_ref):
        gate = jax.nn.sigmoid(_mm(xn3_ref[...], wg_ref[...]))
        pp = _mm(p_ref[...].astype(BF16), wp_ref[...])
        h2 = h2_ref[...]
        err = h2 + gate * pp - t_ref[...]
        sq = jnp.sum(jnp.sum(err * err, axis=1, keepdims=True), axis=0, keepdims=True)
        _acc_out(loss_ref, sq * (0.5 / D_MODEL))
        dh3 = err * (1.0 / D_MODEL)
        dgl = (dh3 * pp) * gate * (1.0 - gate)
        dgl_b = dgl.astype(BF16)
        dgl_ref[...] = dgl_b
        dpp_ref[...] = (dh3 * gate).astype(BF16)
        dxn3 = _mm_nt(dgl_b, wg_ref[...])
        h2h, r2 = _rms_stats(h2)
        dx, dg = _rms_bwd(dxn3, h2h, r2, gl_ref[...])
        _acc_out(dgple_ref, dg)
        dh2 = dh3 + dx
        dh2_ref[...] = dh2
        f2h, rf = _rms_stats(f2_ref[...])
        df2, dg = _rms_bwd(dh2, f2h, rf, gp_ref[...])
        _acc_out(dgpost_ref, dg)
        df2_ref[...] = df2.astype(BF16)

    sds = jax.ShapeDtypeStruct
    return pl.pallas_call(
        body, name="ple_loss", grid=(s_len // tm,),
        in_specs=[_rows(tm, D_MODEL), _rows(tm, D_PLE), _rows(tm, D_MODEL), _rows(tm, D_MODEL), _rows(tm, D_MODEL),
                  _full((D_MODEL, D_MODEL)), _full((D_PLE, D_MODEL)), _full((1, D_MODEL)), _full((1, D_MODEL))],
        out_specs=[_rows(tm, D_MODEL)] * 3 + [_rows(tm, D_MODEL), _full((1, 1)), _full((1, D_MODEL)),
                                              _full((1, D_MODEL))],
        out_shape=[sds((s_len, D_MODEL), F32), sds((s_len, D_MODEL), BF16), sds((s_len, D_MODEL), BF16),
                   sds((s_len, D_MODEL), BF16), sds((1, 1), F32), sds((1, D_MODEL), F32), sds((1, D_MODEL), F32)],
        compiler_params=_cparams(("arbitrary",)),
    )(xn3, p, h2, f2, tgt, w_gate, w_ple, g_ple, g_mlp_post)


def _ff2_bwd(df2, w_ff2, u):
    s_len = u.shape[0]
    tm = min(ROW_TILE, s_len)

    def body(d_ref, w_ref, u_ref, du_ref):
        df = _mm_nt(d_ref[...], w_ref[...])
        du_ref[...] = (df * (2.0 * jnp.maximum(u_ref[...], 0.0))).astype(BF16)

    return pl.pallas_call(
        body, name="ff2_bwd", grid=(s_len // tm,),
        in_specs=[_rows(tm, D_MODEL), _full((D_FF, D_MODEL)), _rows(tm, D_FF)],
        out_specs=_rows(tm, D_FF),
        out_shape=jax.ShapeDtypeStruct((s_len, D_FF), BF16),
        compiler_params=_cparams(("parallel",)),
    )(df2, w_ff2, u)


def _ff1_bwd(du, w_ff1, dh2, h1, y2, g_mlp_pre, g_post):
    s_len = du.shape[0]
    tm = min(ROW_TILE, s_len)

    def body(du_ref, w_ref, dh2_ref, h1_ref, y2_ref, gm_ref, gp_ref, dh1_ref, dy2_ref, dgm_ref, dgp_ref):
        dxn2 = _mm_nt(du_ref[...], w_ref[...])
        h1h, r1 = _rms_stats(h1_ref[...])
        dx, dg = _rms_bwd(dxn2, h1h, r1, gm_ref[...])
        _acc_out(dgm_ref, dg)
        dh1 = dh2_ref[...] + dx
        dh1_ref[...] = dh1
        y2h, ry = _rms_stats(y2_ref[...])
        dy2, dg = _rms_bwd(dh1, y2h, ry, gp_ref[...])
        _acc_out(dgp_ref, dg)
        dy2_ref[...] = dy2.astype(BF16)

    sds = jax.ShapeDtypeStruct
    return pl.pallas_call(
        body, name="ff1_bwd", grid=(s_len // tm,),
        in_specs=[_rows(tm, D_FF), _full((D_MODEL, D_FF)), _rows(tm, D_MODEL), _rows(tm, D_MODEL),
                  _rows(tm, D_MODEL), _full((1, D_MODEL)), _full((1, D_MODEL))],
        out_specs=[_rows(tm, D_MODEL), _rows(tm, D_MODEL), _full((1, D_MODEL)), _full((1, D_MODEL))],
        out_shape=[sds((s_len, D_MODEL), F32), sds((s_len, D_MODEL), BF16), sds((1, D_MODEL), F32),
                   sds((1, D_MODEL), F32)],
        compiler_params=_cparams(("arbitrary",)),
    )(du, w_ff1, dh2, h1, y2, g_mlp_pre, g_post)


def _out_proj_bwd(dy2, w_out, ya, yb, g_a, g_b):
    s_len = ya.shape[0]
    tm = min(ROW_TILE, s_len)

    def body(d_ref, w_ref, ya_ref, yb_ref, ga_ref, gb_ref, dya_ref, dyb_ref, da_ref, db_ref, dga_ref, dgb_ref):
        dycat = _mm_nt(d_ref[...], w_ref[...])
        low = lax.broadcasted_iota(jnp.int32, (tm, 128), 1) < HEAD_DIM
        for y_ref, g_ref, dy_ref, dl_ref, dg_ref, lo in ((ya_ref, ga_ref, dya_ref, da_ref, dga_ref, 0),
                                                         (yb_ref, gb_ref, dyb_ref, db_ref, dgb_ref, D_A)):
            y = y_ref[...]
            yh, r = _rms_stats(y)
            dx, dg = _rms_bwd(dycat[:, lo:lo + D_A], yh, r, g_ref[...])
            _acc_out(dg_ref, dg)
            dy_ref[...] = dx
            prod = dx * y
            for c in range(D_A // 128):
                pc = prod[:, 128 * c:128 * (c + 1)]
                dl_ref[2 * c] = jnp.sum(jnp.where(low, pc, 0.0), axis=1, keepdims=True)
                dl_ref[2 * c + 1] = jnp.sum(jnp.where(low, 0.0, pc), axis=1, keepdims=True)

    sds = jax.ShapeDtypeStruct
    return pl.pallas_call(
        body, name="out_proj_bwd", grid=(s_len // tm,),
        in_specs=[_rows(tm, D_MODEL), _full((D_MODEL, D_MODEL)), _rows(tm, D_A), _rows(tm, D_B), _full((1, D_A)),
                  _full((1, D_B))],
        out_specs=[_rows(tm, D_A), _rows(tm, D_B), _stat_spec(tm), _stat_spec(tm), _full((1, D_A)),
                   _full((1, D_B))],
        out_shape=[sds((s_len, D_A), F32), sds((s_len, D_B), F32), sds((N_HEADS_A, s_len, 1), F32),
                   sds((N_HEADS_B, s_len, 1), F32), sds((1, D_A), F32), sds((1, D_B), F32)],
        compiler_params=_cparams(("arbitrary",)),
    )(dy2, w_out, ya, yb, g_a, g_b)


def _in_proj_bwd(dqr, dkr, dva, dqb, dkb, dvb, qpre, kpre, x, dh1, g1, w_in, cc, ss, gq2, gk2, ones128):
    s_len = x.shape[0]
    tm = min(ROW_TILE // 2, s_len)

    def body(dqr_ref, dkr_ref, dva_ref, dq0, dq1, dq2, dk0, dk1, dk2, dv0, dv1, dv2, qpre_ref, kpre_ref, x_ref,
             dh1_ref, g_ref, w_ref, cc_ref, ss_ref, gq_ref, gk_ref, one_ref, dproj_ref, gx_ref, dg1_ref, dgq_ref,
             dgk_ref):
        first_half = _first_half_mask((tm, 128))
        ones = one_ref[...]
        cc_t, ss_t = cc_ref[...], ss_ref[...]

        def norm_rope_bwd(dy, xc, gain):
            dn = dy * cc_t - _swap_halves(dy, first_half) * ss_t
            r = lax.rsqrt(_xdot(xc * xc, ones) * (1.0 / HEAD_DIM) + EPS)
            xh = xc * r
            gdy = dn * gain
            dx = r * (gdy - xh * (_xdot(gdy * xh, ones) * (1.0 / HEAD_DIM)))
            return dx, jnp.sum(dn * xh, axis=0, keepdims=True)

        dgq = jnp.zeros((1, 128), F32)
        parts = []
        for c in range(D_A // 128):
            sl = slice(128 * c, 128 * (c + 1))
            dx, dg = norm_rope_bwd(dqr_ref[:, sl] * Q_SCALE, qpre_ref[:, sl], gq_ref[...])
            parts.append(dx)
            dgq = dgq + dg
        dxk, dgk = norm_rope_bwd(dkr_ref[...], kpre_ref[...], gk_ref[...])
        _acc_out(dgq_ref, dgq)
        _acc_out(dgk_ref, dgk)
        parts += [dxk, dva_ref[...], (dq0[...] + dq1[...] + dq2[...]) * Q_SCALE, dk0[...] + dk1[...] + dk2[...],
                  dv0[...] + dv1[...] + dv2[...]]
        dproj = jnp.concatenate(parts, axis=-1).astype(BF16)
        dproj_ref[...] = dproj
        dxn = _mm_nt(dproj, w_ref[...])
        xh, r = _rms_stats(x_ref[...])
        dx, dg = _rms_bwd(dxn, xh, r, g_ref[...])
        _acc_out(dg1_ref, dg)
        gx_ref[...] = dh1_ref[...] + dx

    sds = jax.ShapeDtypeStruct
    return pl.pallas_call(
        body, name="in_proj_bwd", grid=(s_len // tm,),
        in_specs=[_rows(tm, D_A), _rows(tm, D_KV_A), _rows(tm, D_KV_A)] + [_rows(tm, D_B)] * 9
                 + [_rows(tm, D_A), _rows(tm, D_KV_A), _rows(tm, D_MODEL), _rows(tm, D_MODEL),
                    _full((1, D_MODEL)), _full((D_MODEL, D_IN)), _rows(tm, 128), _rows(tm, 128), _full((1, 128)),
                    _full((1, 128)), _full((128, 128))],
        out_specs=[_rows(tm, D_IN), _rows(tm, D_MODEL), _full((1, D_MODEL)), _full((1, 128)), _full((1, 128))],
        out_shape=[sds((s_len, D_IN), BF16), sds((s_len, D_MODEL), F32), sds((1, D_MODEL), F32),
                   sds((1, 128), F32), sds((1, 128), F32)],
        compiler_params=_cparams(("arbitrary",)),
    )(dqr, dkr, dva, *dqb, *dkb, *dvb, qpre, kpre, x, dh1, g1, w_in, cc, ss, gq2, gk2, ones128)


def _dw(a, b, name, relu2=False):
    s_len, ka = a.shape
    n = b.shape[1]
    ts = min(DW_TS, s_len)
    bk = min(ka, 1024)
    bn = n if n % 1024 else 1024

    def body(a_ref, b_ref, o_ref):
        @pl.when(pl.program_id(2) == 0)
        def _():
            o_ref[...] = jnp.zeros_like(o_ref)

        av = a_ref[...]
        if relu2:
            av = jnp.square(jnp.maximum(av, 0.0))
        o_ref[...] += _mm_tn(av.astype(BF16), b_ref[...])

    return pl.pallas_call(
        body, name=name, grid=(ka // bk, n // bn, s_len // ts),
        in_specs=[pl.BlockSpec((ts, bk), lambda i, j, k: (k, i)), pl.BlockSpec((ts, bn), lambda i, j, k: (k, j))],
        out_specs=pl.BlockSpec((bk, bn), lambda i, j, k: (i, j)),
        out_shape=jax.ShapeDtypeStruct((ka, n), F32),
        compiler_params=_cparams(("parallel", "parallel", "arbitrary")),
    )(a, b)


def _attn_a_fwd(qpad, kv, ov):
    s_len = kv.shape[1]
    tq = min(ATT_TQ, s_len)
    tk = min(ATT_TK_FWD, s_len)
    rows = GROUP_A * tq

    def body(q_ref, kv_ref, ov_ref, o_ref, lse_ref):
        q = q_ref[...].reshape(rows, 128)

        def step(j, carry):
            m, acc = carry
            span = pl.ds(pl.multiple_of(j * tk, tk), tk)
            s = _mm_nt(q, kv_ref[0, span, :])
            m_new = jnp.maximum(m, jnp.max(s, axis=1, keepdims=True))
            p = jnp.exp(s - m_new).astype(BF16)
            acc = jnp.exp(m - m_new) * acc + _mm(p, ov_ref[0, span, :])
            return m_new, acc

        init = (jnp.full((rows, 1), -jnp.inf, F32), jnp.zeros((rows, 128), F32))
        m, acc = lax.fori_loop(0, s_len // tk, step, init)
        o_ref[...] = (acc / pltpu.roll(acc, HEAD_DIM, 1)).reshape(GROUP_A, tq, 128)
        lse_ref[...] = (m + jnp.log(acc[:, :1])).reshape(GROUP_A, tq, 1)

    return pl.pallas_call(
        body, name="attn_a_fwd", grid=(N_KV_A, s_len // tq),
        in_specs=[pl.BlockSpec((GROUP_A, tq, 128), lambda g, i: (g, i, 0)),
                  pl.BlockSpec((1, s_len, 128), lambda g, i: (g, 0, 0)),
                  pl.BlockSpec((1, s_len, 128), lambda g, i: (g, 0, 0))],
        out_specs=[pl.BlockSpec((GROUP_A, tq, 128), lambda g, i: (g, i, 0)),
                   pl.BlockSpec((GROUP_A, tq, 1), lambda g, i: (g, i, 0))],
        out_shape=[jax.ShapeDtypeStruct((N_HEADS_A, s_len, 128), F32),
                   jax.ShapeDtypeStruct((N_HEADS_A, s_len, 1), F32)],
        compiler_params=_cparams(("parallel", "parallel")),
    )(qpad, kv, ov)


def _attn_a_bwd(qpad, dopad, kv, lse, delta):
    s_len = kv.shape[1]
    tq = min(ATT_TQ, s_len)
    tk = min(ATT_TK_BWD, s_len)
    rows = GROUP_A * tq

    def body(q_ref, do_ref, kv_ref, lse_ref, dl_ref, dq_ref, dkv_ref):
        @pl.when(pl.program_id(1) == 0)
        def _():
            dkv_ref[...] = jnp.zeros_like(dkv_ref)

        q = q_ref[...].reshape(rows, 128)
        do = do_ref[...].reshape(rows, 128)
        lse_t = lse_ref[...].reshape(rows, 1)
        dl_t = dl_ref[...].reshape(rows, 1)

        def step(j, dq):
            span = pl.ds(pl.multiple_of(j * tk, tk), tk)
            kvj = kv_ref[0, span, :]
            p = jnp.exp(_mm_nt(q, kvj) - lse_t)
            ds = (p * (_mm_nt(do, kvj) - dl_t)).astype(BF16)
            dkv_ref[0, span, :] += _mm_tn(ds, q) + _mm_tn(p.astype(BF16), do)
            return dq + _mm(ds, kvj)

        dq = lax.fori_loop(0, s_len // tk, step, jnp.zeros((rows, 128), F32))
        dq_ref[...] = dq.reshape(GROUP_A, tq, 128)

    return pl.pallas_call(
        body, name="attn_a_bwd", grid=(N_KV_A, s_len // tq),
        in_specs=[pl.BlockSpec((GROUP_A, tq, 128), lambda g, i: (g, i, 0)),
                  pl.BlockSpec((GROUP_A, tq, 128), lambda g, i: (g, i, 0)),
                  pl.BlockSpec((1, s_len, 128), lambda g, i: (g, 0, 0)),
                  pl.BlockSpec((GROUP_A, tq, 1), lambda g, i: (g, i, 0)),
                  pl.BlockSpec((GROUP_A, tq, 1), lambda g, i: (g, i, 0))],
        out_specs=[pl.BlockSpec((GROUP_A, tq, 128), lambda g, i: (g, i, 0)),
                   pl.BlockSpec((1, s_len, 128), lambda g, i: (g, 0, 0))],
        out_shape=[jax.ShapeDtypeStruct((N_HEADS_A, s_len, 128), F32),
                   jax.ShapeDtypeStruct((N_KV_A, s_len, 128), F32)],
        compiler_params=_cparams(("parallel", "arbitrary")),
    )(qpad, dopad, kv, lse, delta)


def _band_specs(tq, nblk, width):
    cur = pl.BlockSpec((1, N_HEADS_B, tq, width), lambda r, i: (r, 0, i, 0))
    prev = pl.BlockSpec((1, N_HEADS_B, tq, width), lambda r, i: (r, 0, jnp.maximum(i - 1, 0), 0))
    nxt = pl.BlockSpec((1, N_HEADS_B, tq, width), lambda r, i: (r, 0, jnp.minimum(i + 1, nblk - 1), 0))
    return prev, cur, nxt


def _band(prev_ref, cur_ref, next_ref, h, tq):
    return jnp.concatenate([prev_ref[0, h, tq - HALF_WIN:, :], cur_ref[0, h], next_ref[0, h, :HALF_WIN, :]], axis=0)


def _band_valid(tq, length):
    i = pl.program_id(1)
    qi = lax.broadcasted_iota(jnp.int32, (tq, tq + 2 * HALF_WIN), 0)
    kj = lax.broadcasted_iota(jnp.int32, (tq, tq + 2 * HALF_WIN), 1)
    key = i * tq - HALF_WIN + kj
    return (jnp.abs(kj - HALF_WIN - qi) <= HALF_WIN) & (key >= 0) & (key < length)


def _swa_fwd(q, k, v, bias, tag):
    n_sub, _, length, _ = q.shape
    tq = min(SWA_TQ, length)
    nblk = length // tq
    kprev, kcur, knext = _band_specs(tq, nblk, HEAD_DIM)

    def body(q_ref, kp, kc, kn, vp, vc, vn, b_ref, acc_ref, m_ref, l_ref):
        valid = _band_valid(tq, length)
        for h in range(N_HEADS_B):
            s = _mm_nt(q_ref[0, h], _band(kp, kc, kn, h, tq))
            s = jnp.where(valid, s + b_ref[h], NEG_BIG)
            m = jnp.max(s, axis=1, keepdims=True)
            e = jnp.exp(s - m)
            acc_ref[0, h] = _mm(e.astype(BF16), _band(vp, vc, vn, h, tq))
            m_ref[0, h] = m
            l_ref[0, h] = jnp.sum(e, axis=1, keepdims=True)

    sds = jax.ShapeDtypeStruct
    stat = pl.BlockSpec((1, N_HEADS_B, tq, 1), lambda r, i: (r, 0, i, 0))
    return pl.pallas_call(
        body, name="swa_fwd_" + tag, grid=(n_sub, nblk),
        in_specs=[kcur, kprev, kcur, knext, kprev, kcur, knext, _full(bias.shape)],
        out_specs=[kcur, stat, stat],
        out_shape=[sds(q.shape, F32), sds(q.shape[:3] + (1,), F32), sds(q.shape[:3] + (1,), F32)],
        compiler_params=_cparams(("parallel", "parallel")),
    )(q, k, k, k, v, v, v, bias)


def _swa_bwd_q(q, k, v, dy, lse, delta, bias, tag):
    n_sub, _, length, _ = q.shape
    tq = min(SWA_TQ, length)
    nblk = length // tq
    kprev, kcur, knext = _band_specs(tq, nblk, HEAD_DIM)

    def body(q_ref, kp, kc, kn, vp, vc, vn, dy_ref, lse_ref, dl_ref, b_ref, dq_ref, db_ref):
        @pl.when((pl.program_id(0) == 0) & (pl.program_id(1) == 0))
        def _():
            db_ref[...] = jnp.zeros_like(db_ref)

        valid = _band_valid(tq, length)
        for h in range(N_HEADS_B):
            kb = _band(kp, kc, kn, h, tq)
            s = jnp.where(valid, _mm_nt(q_ref[0, h], kb) + b_ref[h], NEG_BIG)
            p = jnp.exp(s - lse_ref[0, h])
            dp = _mm_nt(dy_ref[0, h].astype(BF16), _band(vp, vc, vn, h, tq))
            ds = p * (dp - dl_ref[0, h])
            db_ref[h] += ds
            dq_ref[0, h] = _mm(ds.astype(BF16), kb)

    stat = pl.BlockSpec((1, N_HEADS_B, tq, 1), lambda r, i: (r, 0, i, 0))
    return pl.pallas_call(
        body, name="swa_bwd_q_" + tag, grid=(n_sub, nblk),
        in_specs=[kcur, kprev, kcur, knext, kprev, kcur, knext, kcur, stat, stat, _full(bias.shape)],
        out_specs=[kcur, _full(bias.shape)],
        out_shape=[jax.ShapeDtypeStruct(q.shape, F32), jax.ShapeDtypeStruct(bias.shape, F32)],
        compiler_params=_cparams(("arbitrary", "arbitrary")),
    )(q, k, k, k, v, v, v, dy, lse, delta, bias)


def _swa_bwd_kv(k_aug, v_aug, q_aug, dy_aug, bias_t, tag):
    n_sub, _, length, _ = k_aug.shape
    tq = min(SWA_TQ, length)
    nblk = length // tq
    kprev, kcur, knext = _band_specs(tq, nblk, 128)

    def body(k_ref, v_ref, qp, qc, qn, dp_, dc_, dn_, b_ref, dk_ref, dv_ref):
        valid = _band_valid(tq, length)
        for h in range(N_HEADS_B):
            qb = _band(qp, qc, qn, h, tq)
            dyb = _band(dp_, dc_, dn_, h, tq)
            pt = jnp.exp(jnp.where(valid, _mm_nt(k_ref[0, h], qb) + b_ref[h], NEG_BIG))
            dst = pt * _mm_nt(v_ref[0, h], dyb)
            dv_ref[0, h] = _mm(pt.astype(BF16), dyb)
            dk_ref[0, h] = _mm(dst.astype(BF16), qb)

    return pl.pallas_call(
        body, name="swa_bwd_kv_" + tag, grid=(n_sub, nblk),
        in_specs=[kcur, kcur, kprev, kcur, knext, kprev, kcur, knext, _full(bias_t.shape)],
        out_specs=[kcur, kcur],
        out_shape=[jax.ShapeDtypeStruct(k_aug.shape, F32), jax.ShapeDtypeStruct(k_aug.shape, F32)],
        compiler_params=_cparams(("parallel", "parallel")),
    )(k_aug, v_aug, q_aug, q_aug, q_aug, dy_aug, dy_aug, dy_aug, bias_t)


BIAS_ROWS = 16
BIAS_TN = 8192


def _bias_tiles(onehot, rel_bias_t):
    n = onehot.shape[0]

    def body(oh_ref, rb_ref, o_ref):
        o_ref[...] = sum(_mm_nt(piece, oh_ref[...]) for piece in _split3(rb_ref[...]))

    return pl.pallas_call(
        body, name="bias_tiles", grid=(n // BIAS_TN,),
        in_specs=[_rows(BIAS_TN, 128), _full((BIAS_ROWS, 128))],
        out_specs=pl.BlockSpec((BIAS_ROWS, BIAS_TN), lambda i: (0, i)),
        out_shape=jax.ShapeDtypeStruct((BIAS_ROWS, n), F32),
        compiler_params=_cparams(("parallel",)),
    )(onehot, rel_bias_t)


def _bias_bwd(onehots, dbias_rows):
    n = onehots[0].shape[0]

    def body(o0, o1, o2, d0, d1, d2, g_ref):
        acc = jnp.zeros((BIAS_ROWS, 128), F32)
        for oh, d in ((o0, d0), (o1, d1), (o2, d2)):
            hi, lo, _ = _split3(d[...])
            acc = acc + _mm(hi, oh[...]) + _mm(lo, oh[...])
        _acc_out(g_ref, acc)

    cols = pl.BlockSpec((BIAS_ROWS, BIAS_TN), lambda i: (0, i))
    return pl.pallas_call(
        body, name="bias_bwd", grid=(n // BIAS_TN,),
        in_specs=[_rows(BIAS_TN, 128)] * 3 + [cols] * 3,
        out_specs=_full((BIAS_ROWS, 128)),
        out_shape=jax.ShapeDtypeStruct((BIAS_ROWS, 128), F32),
        compiler_params=_cparams(("arbitrary",)),
    )(*onehots, *dbias_rows)


def _to_heads(a, n_heads):
    return a.reshape(a.shape[0], n_heads, HEAD_DIM).transpose(1, 0, 2)


def _from_heads(a):
    return a.transpose(1, 0, 2).reshape(a.shape[1], a.shape[0] * HEAD_DIM)


def _to_sub(a, r):
    s_len = a.shape[0]
    w = a.shape[1] // N_HEADS_B
    return a.reshape(s_len // r, r, N_HEADS_B, w).transpose(1, 2, 0, 3)


def _from_sub(a):
    r, h, length, w = a.shape
    return a.transpose(2, 0, 1, 3).reshape(length * r, h * w)


def _pad_lanes(a, left):
    z = jnp.zeros_like(a)
    return jnp.concatenate([z, a] if left else [a, z], axis=-1)


def _stat_to_sub(a, r):
    return a.reshape(a.shape[0], a.shape[1] // r, r, 1).transpose(2, 0, 1, 3)


def _stat_from_sub(a):
    r, h, length, _ = a.shape
    return a.transpose(1, 2, 0, 3).reshape(h, length * r, 1)


def _augment(a, stat):
    if stat is None:
        extra = jnp.ones(a.shape[:-1] + (3,), BF16)
    else:
        extra = jnp.concatenate(_split3(-stat), axis=-1)
    return jnp.concatenate([a, extra, jnp.zeros(a.shape[:-1] + (HEAD_DIM - 3,), BF16)], axis=-1)


def _local_step(x, p, tgt, w_in, w_out, w_ff1, w_ff2, w_gate, w_ple, g_attn_pre, g_q, g_k, g_out_a, g_out_b,
                g_attn_post, rel_bias, g_mlp_pre, g_mlp_post, g_ple):
    s_len = x.shape[0]
    cc, ss = _rope_tables(s_len)
    gq2 = jnp.concatenate([g_q, g_q], axis=-1)
    gk2 = jnp.concatenate([g_k, g_k], axis=-1)
    ones128 = _group_ones(128)
    rel_bias_t = jnp.zeros((BIAS_ROWS, 128), F32).at[:N_HEADS_B, :N_BUCKETS].set(rel_bias.T)

    xn1, qpre, kpre, qa, ka, va, qb, kb, vb = _in_proj(x, g_attn_pre, w_in, cc, ss, gq2, gk2, ones128)

    qpad = _pad_lanes(_to_heads(qa, N_HEADS_A), left=False)
    v_heads = _to_heads(va, N_KV_A)
    kv = jnp.concatenate([_to_heads(ka, N_KV_A), v_heads], axis=-1)
    oa, lse_a = _attn_a_fwd(qpad, kv, jnp.concatenate([jnp.ones_like(v_heads), v_heads], axis=-1))
    ya = _from_heads(oa[:, :, HEAD_DIM:])

    tiles, subs, stats = [], [], []
    for r in DILATIONS:
        tq = min(SWA_TQ, s_len // r)
        onehot = _bucket_onehot(tq, r)
        bias = _bias_tiles(onehot, rel_bias_t)[:N_HEADS_B].reshape(N_HEADS_B, tq, tq + 2 * HALF_WIN)
        qs, ks, vs = _to_sub(qb, r), _to_sub(kb, r), _to_sub(vb, r)
        acc, m, l = _swa_fwd(qs, ks, vs, bias, str(r))
        tiles.append((onehot, bias))
        subs.append((qs, ks, vs))
        stats.append((_from_sub(acc), _stat_from_sub(m), _stat_from_sub(l)))
    yb, lse_b = _merge_b([s[0] for s in stats], [s[1] for s in stats], [s[2] for s in stats])

    ycat, y2, h1, xn2 = _out_proj(ya, yb, x, g_out_a, g_out_b, w_out, g_attn_post, g_mlp_pre)
    u = _ff1(xn2, w_ff1)
    f2, h2, xn3 = _ff2(u, w_ff2, h1, g_mlp_post, g_ple)
    dh2, df2, dgl, dpp, loss, dg_ple, dg_mlp_post = _ple_loss(xn3, p, h2, f2, tgt, w_gate, w_ple, g_ple, g_mlp_post)

    grads = {"g_ple": dg_ple, "g_mlp_post": dg_mlp_post}
    grads["w_ple_gate"] = _dw(xn3, dgl, "dw_gate")
    grads["w_ple_proj"] = _dw(p, dpp, "dw_ple")
    grads["w_ff2"] = _dw(u, df2, "dw_ff2", relu2=True)
    du = _ff2_bwd(df2, w_ff2, u)
    grads["w_ff1"] = _dw(xn2, du, "dw_ff1")
    dh1, dy2, grads["g_mlp_pre"], grads["g_attn_post"] = _ff1_bwd(du, w_ff1, dh2, h1, y2, g_mlp_pre, g_attn_post)
    grads["w_out"] = _dw(ycat, dy2, "dw_out")
    dya, dyb, delta_a, delta_b, grads["g_out_a"], grads["g_out_b"] = _out_proj_bwd(dy2, w_out, ya, yb, g_out_a,
                                                                                 g_out_b)

    dopad = _pad_lanes(_to_heads(dya, N_HEADS_A).astype(BF16), left=True)
    dq_a, dkv_a = _attn_a_bwd(qpad, dopad, kv, lse_a, delta_a)
    dqr = _from_heads(dq_a[:, :, :HEAD_DIM])
    dkr = _from_heads(dkv_a[:, :, :HEAD_DIM])
    dva = _from_heads(dkv_a[:, :, HEAD_DIM:])

    dqs, dks, dvs, dbias_rows = [], [], [], []
    for r, (onehot, bias), (qs, ks, vs) in zip(DILATIONS, tiles, subs):
        dys = _to_sub(dyb, r)
        lse_s, dl_s = _stat_to_sub(lse_b, r), _stat_to_sub(delta_b, r)
        dq_s, dbias = _swa_bwd_q(qs, ks, vs, dys, lse_s, dl_s, bias, str(r))
        bias_t = jnp.flip(bias, axis=(1, 2))
        dk_s, dv_s = _swa_bwd_kv(_augment(ks, None), _augment(vs, None), _augment(qs, lse_s),
                                 _augment(dys.astype(BF16), dl_s), bias_t, str(r))
        dbias_rows.append(jnp.pad(dbias.reshape(N_HEADS_B, -1), ((0, BIAS_ROWS - N_HEADS_B), (0, 0))))
        dqs.append(_from_sub(dq_s))
        dks.append(_from_sub(dk_s[..., :HEAD_DIM]))
        dvs.append(_from_sub(dv_s[..., :HEAD_DIM]))
    grads["rel_bias"] = _bias_bwd([t[0] for t in tiles], dbias_rows)[:N_HEADS_B, :N_BUCKETS].T

    dproj, grad_x, grads["g_attn_pre"], dgq2, dgk2 = _in_proj_bwd(
        dqr, dkr, dva, dqs, dks, dvs, qpre, kpre, x, dh1, g_attn_pre, w_in, cc, ss, gq2, gk2, ones128)
    grads["g_q"] = dgq2[:, :HEAD_DIM] + dgq2[:, HEAD_DIM:]
    grads["g_k"] = dgk2[:, :HEAD_DIM] + dgk2[:, HEAD_DIM:]
    grads["w_in"] = _dw(xn1, dproj, "dw_in")
    return loss, grad_x, grads


ANY = pl.BlockSpec(memory_space=pl.ANY)


def _position():
    return lax.axis_index("x"), lax.axis_index("y"), lax.axis_index("c")


def _other_chips(x, y):
    return [(2 * (1 - x) + y, (1 - x, y)), (2 * x + (1 - y), (x, 1 - y)), (2 * (1 - x) + (1 - y), (1 - x, 1 - y))]


def _cast_shards(shards):
    def body(*refs):
        n = len(refs) // 2
        for i_ref, o_ref in zip(refs[:n], refs[n:]):
            o_ref[...] = i_ref[...].astype(BF16)

    return pl.pallas_call(
        body, name="cast_shards",
        in_specs=[pl.BlockSpec(memory_space=pltpu.VMEM)] * len(shards),
        out_specs=[pl.BlockSpec(memory_space=pltpu.VMEM)] * len(shards),
        out_shape=[jax.ShapeDtypeStruct(s.shape, BF16) for s in shards],
        compiler_params=_cparams(),
    )(*shards)


def _gather_weights(shards):
    n = len(shards)

    def body(*refs):
        ins, outs = refs[:n], refs[n:2 * n]
        send_sems, recv_sems, local_sems = refs[2 * n:]
        x, y, c = _position()
        me = 2 * x + y
        copies = []
        for a in range(n):
            local = pltpu.make_async_copy(ins[a], outs[a].at[me], local_sems.at[a])
            local.start()
            copies.append(local)
        sends = []
        for k, (_, chip) in enumerate(_other_chips(x, y)):
            for a in range(n):
                cp = pltpu.make_async_remote_copy(ins[a], outs[a].at[me], send_sems.at[k, a], recv_sems.at[k, a],
                                                  device_id=(*chip, c), device_id_type=MESH)
                cp.start()
                sends.append(cp)
        for k, (num, chip) in enumerate(_other_chips(x, y)):
            for a in range(n):
                pltpu.make_async_remote_copy(ins[a], outs[a].at[num], send_sems.at[k, a], recv_sems.at[k, a],
                                             device_id=(*chip, c), device_id_type=MESH).wait_recv()
        for cp in sends:
            cp.wait_send()
        for cp in copies:
            cp.wait()

    return pl.pallas_call(
        body, name="gather_weights",
        in_specs=[ANY] * n, out_specs=[ANY] * n,
        out_shape=[jax.ShapeDtypeStruct((N_CHIPS,) + s.shape, s.dtype) for s in shards],
        scratch_shapes=[pltpu.SemaphoreType.DMA((3, n)), pltpu.SemaphoreType.DMA((3, n)),
                        pltpu.SemaphoreType.DMA((n,))],
    )(*shards)


def _send_sibling_half(grads):
    n = len(grads)

    def body(*refs):
        ins, outs = refs[:n], refs[n:2 * n]
        send_sems, recv_sems = refs[2 * n:]
        x, y, c = _position()
        copies = []
        for a in range(n):
            half = ins[a].shape[1] // 2
            theirs = ins[a].at[:, pl.ds(pl.multiple_of((1 - c) * half, 8), half), :]
            cp = pltpu.make_async_remote_copy(theirs, outs[a], send_sems.at[a], recv_sems.at[a],
                                              device_id=(x, y, 1 - c), device_id_type=MESH)
            cp.start()
            copies.append(cp)
        for cp in copies:
            cp.wait()

    return pl.pallas_call(
        body, name="send_sibling_half",
        in_specs=[ANY] * n, out_specs=[ANY] * n,
        out_shape=[jax.ShapeDtypeStruct((g.shape[0], g.shape[1] // 2, g.shape[2]), g.dtype) for g in grads],
        scratch_shapes=[pltpu.SemaphoreType.DMA((n,)), pltpu.SemaphoreType.DMA((n,))],
    )(*grads)


def _scatter_to_chips(pairs):
    n = len(pairs)

    def body(*refs):
        ins, outs = refs[:n], refs[n:2 * n]
        send_sems, recv_sems, local_sems = refs[2 * n:]
        x, y, c = _position()
        me = 2 * x + y
        copies = []
        for a in range(n):
            local = pltpu.make_async_copy(ins[a].at[me], outs[a].at[me], local_sems.at[a])
            local.start()
            copies.append(local)
        sends = []
        for k, (num, chip) in enumerate(_other_chips(x, y)):
            for a in range(n):
                cp = pltpu.make_async_remote_copy(ins[a].at[num], outs[a].at[me], send_sems.at[k, a],
                                                  recv_sems.at[k, a], device_id=(*chip, c), device_id_type=MESH)
                cp.start()
                sends.append(cp)
        for k, (num, chip) in enumerate(_other_chips(x, y)):
            for a in range(n):
                pltpu.make_async_remote_copy(ins[a].at[me], outs[a].at[num], send_sems.at[k, a], recv_sems.at[k, a],
                                             device_id=(*chip, c), device_id_type=MESH).wait_recv()
        for cp in sends:
            cp.wait_send()
        for cp in copies:
            cp.wait()

    return pl.pallas_call(
        body, name="scatter_to_chips",
        in_specs=[ANY] * n, out_specs=[ANY] * n,
        out_shape=[jax.ShapeDtypeStruct(g.shape, g.dtype) for g in pairs],
        scratch_shapes=[pltpu.SemaphoreType.DMA((3, n)), pltpu.SemaphoreType.DMA((3, n)),
                        pltpu.SemaphoreType.DMA((n,))],
    )(*pairs)


def _exchange_halves(halves):
    n = len(halves)

    def body(*refs):
        ins, outs = refs[:n], refs[n:2 * n]
        send_sems, recv_sems, local_sems = refs[2 * n:]
        x, y, c = _position()
        copies = []
        for a in range(n):
            local = pltpu.make_async_copy(ins[a], outs[a].at[c], local_sems.at[a])
            local.start()
            cp = pltpu.make_async_remote_copy(ins[a], outs[a].at[c], send_sems.at[a], recv_sems.at[a],
                                              device_id=(x, y, 1 - c), device_id_type=MESH)
            cp.start()
            copies += [local, cp]
        for a in range(n):
            pltpu.make_async_remote_copy(ins[a], outs[a].at[1 - c], send_sems.at[a], recv_sems.at[a],
                                         device_id=(x, y, 1 - c), device_id_type=MESH).wait_recv()
        for a in range(n):
            copies[2 * a].wait()
            copies[2 * a + 1].wait_send()

    return pl.pallas_call(
        body, name="exchange_halves",
        in_specs=[ANY] * n, out_specs=[ANY] * n,
        out_shape=[jax.ShapeDtypeStruct((2,) + h.shape, h.dtype) for h in halves],
        scratch_shapes=[pltpu.SemaphoreType.DMA((n,)), pltpu.SemaphoreType.DMA((n,)),
                        pltpu.SemaphoreType.DMA((n,))],
    )(*halves)


def _allreduce_small(v):
    def body(v_ref, o_ref, buf, send_sems, recv_sems):
        x, y, c = _position()
        me = 4 * x + 2 * y + c
        peers = [(1 - x, y, c), (x, 1 - y, c), (x, y, 1 - c), (1 - x, 1 - y, c), (1 - x, y, 1 - c), (x, 1 - y, 1 - c),
                 (1 - x, 1 - y, 1 - c)]
        num = lambda d: 4 * d[0] + 2 * d[1] + d[2]
        buf[me] = v_ref[...]
        sends = []
        for k, peer in enumerate(peers):
            cp = pltpu.make_async_remote_copy(v_ref, buf.at[me], send_sems.at[k], recv_sems.at[k], device_id=peer,
                                              device_id_type=MESH)
            cp.start()
            sends.append(cp)
        for k, peer in enumerate(peers):
            pltpu.make_async_remote_copy(v_ref, buf.at[num(peer)], send_sems.at[k], recv_sems.at[k], device_id=peer,
                                         device_id_type=MESH).wait_recv()
        for cp in sends:
            cp.wait_send()
        total = buf[0]
        for d in range(1, 8):
            total = total + buf[d]
        o_ref[...] = total

    return pl.pallas_call(
        body, name="allreduce_small",
        in_specs=[pl.BlockSpec(memory_space=pltpu.VMEM)], out_specs=pl.BlockSpec(memory_space=pltpu.VMEM),
        out_shape=jax.ShapeDtypeStruct(v.shape, v.dtype),
        scratch_shapes=[pltpu.VMEM((8,) + v.shape, v.dtype), pltpu.SemaphoreType.DMA((7,)),
                        pltpu.SemaphoreType.DMA((7,))],
    )(v)


def _sum_leading(a, name):
    k, r, c = a.shape
    tr = min(r, 256)

    def body(a_ref, o_ref):
        total = a_ref[0]
        for i in range(1, k):
            total = total + a_ref[i]
        o_ref[...] = total

    return pl.pallas_call(
        body, name=name, grid=(r // tr,),
        in_specs=[pl.BlockSpec((k, tr, c), lambda i: (0, i, 0))],
        out_specs=pl.BlockSpec((tr, c), lambda i: (i, 0)),
        out_shape=jax.ShapeDtypeStruct((r, c), a.dtype),
        compiler_params=_cparams(("parallel",)),
    )(a)


def _add(a, b, name):
    k, r, c = a.shape
    tr = min(r, 256)
    spec = pl.BlockSpec((k, tr, c), lambda i: (0, i, 0))

    def body(a_ref, b_ref, o_ref):
        o_ref[...] = a_ref[...] + b_ref[...]

    return pl.pallas_call(
        body, name=name, grid=(r // tr,), in_specs=[spec, spec], out_specs=spec,
        out_shape=jax.ShapeDtypeStruct(a.shape, a.dtype), compiler_params=_cparams(("parallel",)),
    )(a, b)


def _adamw(w, g, m, v, name):
    r, c = w.shape
    tr = min(r, 256)
    spec = pl.BlockSpec((tr, c), lambda i: (i, 0))

    def body(w_ref, g_ref, m_ref, v_ref, d_ref, nm_ref, nv_ref):
        gv = g_ref[...]
        nm = ADAM_B1 * m_ref[...] + (1.0 - ADAM_B1) * gv
        nv = ADAM_B2 * v_ref[...] + (1.0 - ADAM_B2) * jnp.square(gv)
        m_hat = nm / (1.0 - ADAM_B1 ** ADAM_STEP)
        v_hat = nv / (1.0 - ADAM_B2 ** ADAM_STEP)
        d_ref[...] = -ADAM_LR * (m_hat / (jnp.sqrt(v_hat) + ADAM_EPS) + ADAM_WD * w_ref[...])
        nm_ref[...] = nm
        nv_ref[...] = nv

    return pl.pallas_call(
        body, name=name, grid=(r // tr,), in_specs=[spec] * 4, out_specs=[spec] * 3,
        out_shape=[jax.ShapeDtypeStruct(w.shape, F32)] * 3, compiler_params=_cparams(("parallel",)),
    )(w, g, m, v)


MATRICES = ("w_in", "w_out", "w_ff1", "w_ff2", "w_ple_gate", "w_ple_proj")
COLUMN_SHARDED = ("w_in", "w_ff1", "w_ple_proj")
SMALL = ("g_attn_pre", "g_q", "g_k", "g_out_a", "g_out_b", "g_attn_post", "rel_bias", "g_mlp_pre", "g_mlp_post",
         "g_ple")
WEIGHT_ORDER = ("w_in", "g_attn_pre", "g_q", "g_k", "g_out_a", "g_out_b", "w_out", "g_attn_post", "rel_bias",
                "g_mlp_pre", "w_ff1", "w_ff2", "g_mlp_post", "g_ple", "w_ple_gate", "w_ple_proj")
PACK_ROWS, PACK_COLS = 8, 1024


def _pack_small(values, extra=None):
    flat = [values[n].reshape(-1) for n in SMALL]
    used = sum(f.shape[0] for f in flat)
    tail = jnp.zeros((PACK_ROWS * PACK_COLS - used - 1,), F32)
    last = jnp.zeros((1,), F32) if extra is None else extra.reshape(1)
    return jnp.concatenate(flat + [tail, last]).reshape(PACK_ROWS, PACK_COLS)


def _unpack_small(packed, like):
    flat = packed.reshape(-1)
    out, o = {}, 0
    for n in SMALL:
        size = like[n].size
        out[n] = flat[o:o + size].reshape(like[n].shape)
        o += size
    return out, flat[-1]


def kernel(x, p, w_in, g_attn_pre, g_q, g_k, g_out_a, g_out_b, w_out, g_attn_post, rel_bias, g_mlp_pre, w_ff1, w_ff2, g_mlp_post, g_ple, w_ple_gate, w_ple_proj, loss_target, m_w_in, m_g_attn_pre, m_g_q, m_g_k, m_g_out_a, m_g_out_b, m_w_out, m_g_attn_post, m_rel_bias, m_g_mlp_pre, m_w_ff1, m_w_ff2, m_g_mlp_post, m_g_ple, m_w_ple_gate, m_w_ple_proj, v_w_in, v_g_attn_pre, v_g_q, v_g_k, v_g_out_a, v_g_out_b, v_w_out, v_g_attn_post, v_rel_bias, v_g_mlp_pre, v_w_ff1, v_w_ff2, v_g_mlp_post, v_g_ple, v_w_ple_gate, v_w_ple_proj):
    given = dict(locals())
    weights = {n: given[n] for n in WEIGHT_ORDER}
    shards = {n: weights[n][0] for n in MATRICES}

    gathered = _gather_weights(_cast_shards([shards[n] for n in MATRICES]))
    whole = {}
    for n, g in zip(MATRICES, gathered):
        if n in COLUMN_SHARDED:
            whole[n] = g.transpose(1, 0, 2).reshape(g.shape[1], N_CHIPS * g.shape[2])
        else:
            whole[n] = g.reshape(N_CHIPS * g.shape[1], g.shape[2])

    loss, grad_x, grads = _local_step(
        x[0], p[0, 0], loss_target[0], whole["w_in"], whole["w_out"], whole["w_ff1"], whole["w_ff2"],
        whole["w_ple_gate"], whole["w_ple_proj"], g_attn_pre, g_q, g_k, g_out_a, g_out_b, g_attn_post, rel_bias,
        g_mlp_pre, g_mlp_post, g_ple)

    by_chip = []
    for n in MATRICES:
        g = grads[n]
        if n in COLUMN_SHARDED:
            by_chip.append(g.reshape(g.shape[0], N_CHIPS, g.shape[1] // N_CHIPS).transpose(1, 0, 2))
        else:
            by_chip.append(g.reshape(N_CHIPS, g.shape[0] // N_CHIPS, g.shape[1]))
    from_sibling = _send_sibling_half(by_chip)
    c = lax.axis_index("c")
    pairs = []
    for n, g, other in zip(MATRICES, by_chip, from_sibling):
        half = g.shape[1] // 2
        mine = lax.dynamic_slice_in_dim(g, c * half, half, axis=1)
        pairs.append(_add(mine, other, "pair_sum_" + n))
    from_chips = _scatter_to_chips(pairs)
    halves = [_sum_leading(g, "chip_sum_" + n) for n, g in zip(MATRICES, from_chips)]
    both = _exchange_halves(halves)
    grad_w = {n: g.reshape(2 * g.shape[1], g.shape[2]) for n, g in zip(MATRICES, both)}

    small_like = {n: weights[n] for n in SMALL}
    reduced = _allreduce_small(_pack_small({n: grads[n] for n in SMALL}, extra=loss))
    grad_small, loss_total = _unpack_small(reduced, small_like)

    delta, new_m, new_v = {}, {}, {}
    for n in MATRICES:
        d, nm, nv = _adamw(shards[n], grad_w[n], given["m_" + n][0], given["v_" + n][0], "adamw_" + n)
        delta[n], new_m[n], new_v[n] = d[None], nm[None], nv[None]
        grad_w[n] = grad_w[n][None]
    d, nm, nv = _adamw(_pack_small(small_like), reduced, _pack_small({n: given["m_" + n] for n in SMALL}),
                       _pack_small({n: given["v_" + n] for n in SMALL}), "adamw_small")
    d_small, nm_small, nv_small = (_unpack_small(a, small_like)[0] for a in (d, nm, nv))
    for n in SMALL:
        grad_w[n], delta[n], new_m[n], new_v[n] = grad_small[n], d_small[n], nm_small[n], nv_small[n]

    return (loss_total, grad_x[None], *[grad_w[n] for n in WEIGHT_ORDER], *[delta[n] for n in WEIGHT_ORDER],
            *[new_m[n] for n in WEIGHT_ORDER], *[new_v[n] for n in WEIGHT_ORDER])
```

```python
import functools
import math

import jax
import jax.numpy as jnp
from jax import lax
from jax.experimental import pallas as pl
from jax.experimental.pallas import tpu as pltpu

F32 = jnp.float32
BF16 = jnp.bfloat16

D_MODEL = 1024
HEAD_DIM = 64
N_HEADS_A = 8
N_KV_A = 2
GROUP_A = N_HEADS_A // N_KV_A
N_HEADS_B = 8
D_A = N_HEADS_A * HEAD_DIM
D_KV_A = N_KV_A * HEAD_DIM
D_B = N_HEADS_B * HEAD_DIM
D_IN = D_A + 2 * D_KV_A + 3 * D_B
D_FF = 4 * D_MODEL
D_PLE = 256
GRID_W = 64
ROPE_THETA = 10000.0
DILATIONS = (1, 4, 16)
HALF_WIN = 64
N_BUCKETS = 32
MAX_DISTANCE = 1024
EPS = 1e-6
NEG_BIG = -1e30
Q_SCALE = HEAD_DIM ** -0.5

ADAM_LR = 0.001
ADAM_B1 = 0.9
ADAM_B2 = 0.999
ADAM_EPS = 1e-08
ADAM_WD = 0.01
ADAM_STEP = 10

N_CHIPS = 4
MESH = pl.DeviceIdType.MESH

ROW_TILE = 512
ATT_TQ = 256
ATT_TK_FWD = 2048
ATT_TK_BWD = 1024
SWA_TQ = 256
SWA_MIN_BLOCK = 1024
DW_TS = 1024
VMEM_LIMIT = 56 * 1024 * 1024

NT = (((1,), (1,)), ((), ()))
TN = (((0,), (0,)), ((), ()))


def _cparams(sem=None, vmem=VMEM_LIMIT):
    return pltpu.CompilerParams(dimension_semantics=sem, vmem_limit_bytes=vmem)


def _full(shape):
    n = len(shape)
    return pl.BlockSpec(shape, lambda *_: (0,) * n)


def _rows(tm, width):
    return pl.BlockSpec((tm, width), lambda i: (i, 0))


def _split3(a):
    a1 = a.astype(BF16)
    r = a - a1.astype(F32)
    a2 = r.astype(BF16)
    a3 = (r - a2.astype(F32)).astype(BF16)
    return a1, a2, a3


def _xdot(a, sel):
    a1, a2, a3 = _split3(a)
    d = lambda p: jnp.dot(p, sel, preferred_element_type=F32)
    return d(a1) + d(a2) + d(a3)


def _mm(a, b):
    return jnp.dot(a, b, preferred_element_type=F32)


def _mm_nt(a, b):
    return lax.dot_general(a, b, NT, preferred_element_type=F32)


def _mm_tn(a, b):
    return lax.dot_general(a, b, TN, preferred_element_type=F32)


def _rms_stats(x):
    r = lax.rsqrt(jnp.mean(x * x, axis=-1, keepdims=True) + EPS)
    return x * r, r


def _rms_bwd(dy, xh, r, g):
    gdy = dy * g
    dx = r * (gdy - xh * jnp.mean(gdy * xh, axis=-1, keepdims=True))
    dg = jnp.sum(dy * xh, axis=0, keepdims=True)
    return dx, dg


def _acc_out(ref, val):
    @pl.when(pl.program_id(0) == 0)
    def _():
        ref[...] = jnp.zeros_like(ref)

    ref[...] += val


def _swap_halves(x, first_half):
    return jnp.where(first_half, pltpu.roll(x, 96, 1), pltpu.roll(x, 32, 1))


def _first_half_mask(shape):
    return (lax.broadcasted_iota(jnp.int32, shape, 1) % HEAD_DIM) < (HEAD_DIM // 2)


def _rope_tables(s_len):
    t = jnp.arange(s_len)
    row = (t // GRID_W).astype(F32)
    col = (t % GRID_W).astype(F32)
    n_axis = HEAD_DIM // 4
    inv_freq = ROPE_THETA ** (-jnp.arange(n_axis, dtype=F32) / n_axis)
    ang = jnp.concatenate([row[:, None] * inv_freq, col[:, None] * inv_freq], axis=-1)
    c, s = jnp.cos(ang), jnp.sin(ang)
    cc = jnp.concatenate([c, c, c, c], axis=-1)
    ss = jnp.concatenate([-s, s, -s, s], axis=-1)
    return cc, ss


def _group_ones(width):
    i = jnp.arange(width)
    return (i[:, None] // HEAD_DIM == i[None, :] // HEAD_DIM).astype(BF16)


def _t5_bucket(rel):
    nb = N_BUCKETS // 2
    max_exact = nb // 2
    side = jnp.where(rel > 0, nb, 0)
    n = jnp.abs(rel)
    large = max_exact + (jnp.log(jnp.maximum(n, max_exact).astype(F32) / max_exact)
                         / math.log(MAX_DISTANCE / max_exact) * (nb - max_exact)).astype(jnp.int32)
    large = jnp.minimum(large, nb - 1)
    return side + jnp.where(n < max_exact, n, large)


def _bucket_onehot(tq, dilation):
    qi = jnp.arange(tq)
    kj = jnp.arange(tq + 2 * HALF_WIN)
    rel = kj[None, :] - HALF_WIN - qi[:, None]
    bucket = _t5_bucket(rel * dilation).reshape(-1)
    return (bucket[:, None] == jnp.arange(128)[None, :]).astype(BF16)


def _in_proj(x, g1, w_in, cc, ss, gq2, gk2, ones128):
    s_len = x.shape[0]
    tm = min(ROW_TILE, s_len)

    def body(x_ref, g_ref, w_ref, cc_ref, ss_ref, gq_ref, gk_ref, one_ref,
             xn_ref, qpre_ref, kpre_ref, qa_ref, ka_ref, va_ref, qb_ref, kb_ref, vb_ref):
        xh, _ = _rms_stats(x_ref[...])
        xn = (xh * g_ref[...]).astype(BF16)
        xn_ref[...] = xn
        proj = _mm(xn, w_ref[...])
        first_half = _first_half_mask((tm, 128))
        ones = one_ref[...]
        cc_t, ss_t = cc_ref[...], ss_ref[...]

        def norm_rope(xc, gain):
            ms = _xdot(xc * xc, ones) * (1.0 / HEAD_DIM)
            y = xc * lax.rsqrt(ms + EPS) * gain
            return y * cc_t + _swap_halves(y, first_half) * ss_t

        qpre_ref[...] = proj[:, :D_A]
        kpre_ref[...] = proj[:, D_A:D_A + D_KV_A]
        for c in range(D_A // 128):
            y = norm_rope(proj[:, 128 * c:128 * (c + 1)], gq_ref[...])
            qa_ref[:, 128 * c:128 * (c + 1)] = (y * Q_SCALE).astype(BF16)
        ka_ref[...] = norm_rope(proj[:, D_A:D_A + D_KV_A], gk_ref[...]).astype(BF16)
        o = D_A + D_KV_A
        va_ref[...] = proj[:, o:o + D_KV_A].astype(BF16)
        o += D_KV_A
        qb_ref[...] = (proj[:, o:o + D_B] * Q_SCALE).astype(BF16)
        kb_ref[...] = proj[:, o + D_B:o + 2 * D_B].astype(BF16)
        vb_ref[...] = proj[:, o + 2 * D_B:o + 3 * D_B].astype(BF16)

    sds = jax.ShapeDtypeStruct
    return pl.pallas_call(
        body, name="in_proj", grid=(s_len // tm,),
        in_specs=[_rows(tm, D_MODEL), _full((1, D_MODEL)), _full((D_MODEL, D_IN)), _rows(tm, 128), _rows(tm, 128),
                  _full((1, 128)), _full((1, 128)), _full((128, 128))],
        out_specs=[_rows(tm, D_MODEL), _rows(tm, D_A), _rows(tm, D_KV_A), _rows(tm, D_A), _rows(tm, D_KV_A),
                   _rows(tm, D_KV_A), _rows(tm, D_B), _rows(tm, D_B), _rows(tm, D_B)],
        out_shape=[sds((s_len, D_MODEL), BF16), sds((s_len, D_A), F32), sds((s_len, D_KV_A), F32),
                   sds((s_len, D_A), BF16), sds((s_len, D_KV_A), BF16), sds((s_len, D_KV_A), BF16),
                   sds((s_len, D_B), BF16), sds((s_len, D_B), BF16), sds((s_len, D_B), BF16)],
        compiler_params=_cparams(("parallel",)),
    )(x, g1, w_in, cc, ss, gq2, gk2, ones128)


def _stat_spec(tm):
    return pl.BlockSpec((N_HEADS_B, tm, 1), lambda i: (0, i, 0))


def _merge_b(outs, lses):
    s_len = outs[0].shape[0]
    tm = min(ROW_TILE, s_len)

    def body(o0, o1, o2, l0, l1, l2, yb_ref, lse_ref):
        m_all = jnp.maximum(jnp.maximum(l0[...], l1[...]), l2[...])
        w = [jnp.exp(l[...] - m_all) for l in (l0, l1, l2)]
        den = w[0] + w[1] + w[2]
        yb_ref[...] = (w[0] * o0[...] + w[1] * o1[...] + w[2] * o2[...]) / den
        lse_ref[...] = m_all + jnp.log(den)

    return pl.pallas_call(
        body, name="merge_b", grid=(s_len // tm,),
        in_specs=[_rows(tm, D_B)] * 6,
        out_specs=[_rows(tm, D_B), _rows(tm, D_B)],
        out_shape=[jax.ShapeDtypeStruct((s_len, D_B), F32), jax.ShapeDtypeStruct((s_len, D_B), F32)],
        compiler_params=_cparams(("parallel",)),
    )(*outs, *lses)


def _out_proj(ya, yb, x, g_a, g_b, w_out, g_post, g_mlp_pre):
    s_len = x.shape[0]
    tm = min(ROW_TILE, s_len)

    def body(ya_ref, yb_ref, x_ref, ga_ref, gb_ref, w_ref, gp_ref, gm_ref, ycat_ref, y2_ref, h1_ref, xn2_ref):
        ah, _ = _rms_stats(ya_ref[...])
        bh, _ = _rms_stats(yb_ref[...])
        ycat = jnp.concatenate([ah * ga_ref[...], bh * gb_ref[...]], axis=-1).astype(BF16)
        ycat_ref[...] = ycat
        y2 = _mm(ycat, w_ref[...])
        y2_ref[...] = y2
        y2h, _ = _rms_stats(y2)
        h1 = x_ref[...] + y2h * gp_ref[...]
        h1_ref[...] = h1
        h1h, _ = _rms_stats(h1)
        xn2_ref[...] = (h1h * gm_ref[...]).astype(BF16)

    sds = jax.ShapeDtypeStruct
    return pl.pallas_call(
        body, name="out_proj", grid=(s_len // tm,),
        in_specs=[_rows(tm, D_A), _rows(tm, D_B), _rows(tm, D_MODEL), _full((1, D_A)), _full((1, D_B)),
                  _full((D_MODEL, D_MODEL)), _full((1, D_MODEL)), _full((1, D_MODEL))],
        out_specs=[_rows(tm, D_MODEL)] * 4,
        out_shape=[sds((s_len, D_MODEL), BF16), sds((s_len, D_MODEL), F32), sds((s_len, D_MODEL), F32),
                   sds((s_len, D_MODEL), BF16)],
        compiler_params=_cparams(("parallel",)),
    )(ya, yb, x, g_a, g_b, w_out, g_post, g_mlp_pre)


def _ff1(xn2, w_ff1):
    s_len = xn2.shape[0]
    tm = min(ROW_TILE, s_len)

    def body(x_ref, w_ref, u_ref):
        u_ref[...] = _mm(x_ref[...], w_ref[...])

    return pl.pallas_call(
        body, name="ff1", grid=(s_len // tm,),
        in_specs=[_rows(tm, D_MODEL), _full((D_MODEL, D_FF))],
        out_specs=_rows(tm, D_FF),
        out_shape=jax.ShapeDtypeStruct((s_len, D_FF), F32),
        compiler_params=_cparams(("parallel",)),
    )(xn2, w_ff1)


def _ff2(u, w_ff2, h1, g_post, g_ple):
    s_len = u.shape[0]
    tm = min(ROW_TILE, s_len)

    def body(u_ref, w_ref, h1_ref, gp_ref, gl_ref, f2_ref, h2_ref, xn3_ref):
        f = jnp.square(jnp.maximum(u_ref[...], 0.0)).astype(BF16)
        f2 = _mm(f, w_ref[...])
        f2_ref[...] = f2
        f2h, _ = _rms_stats(f2)
        h2 = h1_ref[...] + f2h * gp_ref[...]
        h2_ref[...] = h2
        h2h, _ = _rms_stats(h2)
        xn3_ref[...] = (h2h * gl_ref[...]).astype(BF16)

    sds = jax.ShapeDtypeStruct
    return pl.pallas_call(
        body, name="ff2", grid=(s_len // tm,),
        in_specs=[_rows(tm, D_FF), _full((D_FF, D_MODEL)), _rows(tm, D_MODEL), _full((1, D_MODEL)),
                  _full((1, D_MODEL))],
        out_specs=[_rows(tm, D_MODEL)] * 3,
        out_shape=[sds((s_len, D_MODEL), F32), sds((s_len, D_MODEL), F32), sds((s_len, D_MODEL), BF16)],
        compiler_params=_cparams(("parallel",)),
    )(u, w_ff2, h1, g_post, g_ple)


def _ple_loss(xn3, p, h2, f2, tgt, w_gate, w_ple, g_ple, g_mlp_post):
    s_len = h2.shape[0]
    tm = min(ROW_TILE, s_len)

    def body(xn3_ref, p_ref, h2_ref, f2_ref, t_ref, wg_ref, wp_ref, gl_ref, gp_ref,
             dh2_ref, df2_ref, dgl_ref, dpp_ref, loss_ref, dgple_ref, dgpost_ref):
        gate = jax.nn.sigmoid(_mm(xn3_ref[...], wg_ref[...]))
        pp = _mm(p_ref[...].astype(BF16), wp_ref[...])
        h2 = h2_ref[...]
        err = h2 + gate * pp - t_ref[...]
        sq = jnp.sum(jnp.sum(err * err, axis=1, keepdims=True), axis=0, keepdims=True)
        _acc_out(loss_ref, sq * (0.5 / D_MODEL))
        dh3 = err * (1.0 / D_MODEL)
        dgl = (dh3 * pp) * gate * (1.0 - gate)
        dgl_b = dgl.astype(BF16)
        dgl_ref[...] = dgl_b
        dpp_ref[...] = (dh3 * gate).astype(BF16)
        dxn3 = _mm_nt(dgl_b, wg_ref[...])
        h2h, r2 = _rms_stats(h2)
        dx, dg = _rms_bwd(dxn3, h2h, r2, gl_ref[...])
        _acc_out(dgple_ref, dg)
        dh2 = dh3 + dx
        dh2_ref[...] = dh2
        f2h, rf = _rms_stats(f2_ref[...])
        df2, dg = _rms_bwd(dh2, f2h, rf, gp_ref[...])
        _acc_out(dgpost_ref, dg)
        df2_ref[...] = df2.astype(BF16)

    sds = jax.ShapeDtypeStruct
    return pl.pallas_call(
        body, name="ple_loss", grid=(s_len // tm,),
        in_specs=[_rows(tm, D_MODEL), _rows(tm, D_PLE), _rows(tm, D_MODEL), _rows(tm, D_MODEL), _rows(tm, D_MODEL),
                  _full((D_MODEL, D_MODEL)), _full((D_PLE, D_MODEL)), _full((1, D_MODEL)), _full((1, D_MODEL))],
        out_specs=[_rows(tm, D_MODEL)] * 3 + [_rows(tm, D_MODEL), _full((1, 1)), _full((1, D_MODEL)),
                                              _full((1, D_MODEL))],
        out_shape=[sds((s_len, D_MODEL), F32), sds((s_len, D_MODEL), BF16), sds((s_len, D_MODEL), BF16),
                   sds((s_len, D_MODEL), BF16), sds((1, 1), F32), sds((1, D_MODEL), F32), sds((1, D_MODEL), F32)],
        compiler_params=_cparams(("arbitrary",)),
    )(xn3, p, h2, f2, tgt, w_gate, w_ple, g_ple, g_mlp_post)


def _ff2_bwd(df2, w_ff2, u):
    s_len = u.shape[0]
    tm = min(ROW_TILE, s_len)

    def body(d_ref, w_ref, u_ref, du_ref):
        df = _mm_nt(d_ref[...], w_ref[...])
        du_ref[...] = (df * (2.0 * jnp.maximum(u_ref[...], 0.0))).astype(BF16)

    return pl.pallas_call(
        body, name="ff2_bwd", grid=(s_len // tm,),
        in_specs=[_rows(tm, D_MODEL), _full((D_FF, D_MODEL)), _rows(tm, D_FF)],
        out_specs=_rows(tm, D_FF),
        out_shape=jax.ShapeDtypeStruct((s_len, D_FF), BF16),
        compiler_params=_cparams(("parallel",)),
    )(df2, w_ff2, u)


def _ff1_bwd(du, w_ff1, dh2, h1, y2, g_mlp_pre, g_post):
    s_len = du.shape[0]
    tm = min(ROW_TILE, s_len)

    def body(du_ref, w_ref, dh2_ref, h1_ref, y2_ref, gm_ref, gp_ref, dh1_ref, dy2_ref, dgm_ref, dgp_ref):
        dxn2 = _mm_nt(du_ref[...], w_ref[...])
        h1h, r1 = _rms_stats(h1_ref[...])
        dx, dg = _rms_bwd(dxn2, h1h, r1, gm_ref[...])
        _acc_out(dgm_ref, dg)
        dh1 = dh2_ref[...] + dx
        dh1_ref[...] = dh1
        y2h, ry = _rms_stats(y2_ref[...])
        dy2, dg = _rms_bwd(dh1, y2h, ry, gp_ref[...])
        _acc_out(dgp_ref, dg)
        dy2_ref[...] = dy2.astype(BF16)

    sds = jax.ShapeDtypeStruct
    return pl.pallas_call(
        body, name="ff1_bwd", grid=(s_len // tm,),
        in_specs=[_rows(tm, D_FF), _full((D_MODEL, D_FF)), _rows(tm, D_MODEL), _rows(tm, D_MODEL),
                  _rows(tm, D_MODEL), _full((1, D_MODEL)), _full((1, D_MODEL))],
        out_specs=[_rows(tm, D_MODEL), _rows(tm, D_MODEL), _full((1, D_MODEL)), _full((1, D_MODEL))],
        out_shape=[sds((s_len, D_MODEL), F32), sds((s_len, D_MODEL), BF16), sds((1, D_MODEL), F32),
                   sds((1, D_MODEL), F32)],
        compiler_params=_cparams(("arbitrary",)),
    )(du, w_ff1, dh2, h1, y2, g_mlp_pre, g_post)


def _out_proj_bwd(dy2, w_out, ya, yb, lse_b, g_a, g_b):
    s_len = ya.shape[0]
    tm = min(ROW_TILE, s_len)

    def body(d_ref, w_ref, ya_ref, yb_ref, lse_ref, ga_ref, gb_ref, dya_ref, dyb_ref, da_ref, st_ref, dga_ref,
             dgb_ref):
        dycat = _mm_nt(d_ref[...], w_ref[...])
        lane = lax.broadcasted_iota(jnp.int32, (tm, 128), 1)
        low = lane < HEAD_DIM
        is_lse = (lane % HEAD_DIM) < (HEAD_DIM // 2)

        def head_sums(prod_chunk):
            return (jnp.sum(jnp.where(low, prod_chunk, 0.0), axis=1, keepdims=True),
                    jnp.sum(jnp.where(low, 0.0, prod_chunk), axis=1, keepdims=True))

        ya = ya_ref[...]
        yh, r = _rms_stats(ya)
        dya, dg = _rms_bwd(dycat[:, :D_A], yh, r, ga_ref[...])
        _acc_out(dga_ref, dg)
        dya_ref[...] = dya
        prod = dya * ya
        for c in range(D_A // 128):
            da_ref[2 * c], da_ref[2 * c + 1] = head_sums(prod[:, 128 * c:128 * (c + 1)])

        yb = yb_ref[...]
        yh, r = _rms_stats(yb)
        dyb, dg = _rms_bwd(dycat[:, D_A:], yh, r, gb_ref[...])
        _acc_out(dgb_ref, dg)
        dyb_ref[...] = dyb.astype(BF16)
        prod = dyb * yb
        for c in range(D_B // 128):
            sl = slice(128 * c, 128 * (c + 1))
            d_lo, d_hi = head_sums(prod[:, sl])
            st_ref[:, sl] = jnp.where(is_lse, lse_ref[:, sl], jnp.where(low, d_lo, d_hi))

    sds = jax.ShapeDtypeStruct
    return pl.pallas_call(
        body, name="out_proj_bwd", grid=(s_len // tm,),
        in_specs=[_rows(tm, D_MODEL), _full((D_MODEL, D_MODEL)), _rows(tm, D_A), _rows(tm, D_B), _rows(tm, D_B),
                  _full((1, D_A)), _full((1, D_B))],
        out_specs=[_rows(tm, D_A), _rows(tm, D_B), _stat_spec(tm), _rows(tm, D_B), _full((1, D_A)),
                   _full((1, D_B))],
        out_shape=[sds((s_len, D_A), F32), sds((s_len, D_B), BF16), sds((N_HEADS_A, s_len, 1), F32),
                   sds((s_len, D_B), F32), sds((1, D_A), F32), sds((1, D_B), F32)],
        compiler_params=_cparams(("arbitrary",)),
    )(dy2, w_out, ya, yb, lse_b, g_a, g_b)


def _in_proj_bwd(dqr, dkr, dva, dqb, dkb, dvb, qpre, kpre, x, dh1, g1, w_in, cc, ss, gq2, gk2, ones128):
    s_len = x.shape[0]
    tm = min(ROW_TILE // 2, s_len)

    def body(dqr_ref, dkr_ref, dva_ref, dq0, dq1, dq2, dk0, dk1, dk2, dv0, dv1, dv2, qpre_ref, kpre_ref, x_ref,
             dh1_ref, g_ref, w_ref, cc_ref, ss_ref, gq_ref, gk_ref, one_ref, dproj_ref, gx_ref, dg1_ref, dgq_ref,
             dgk_ref):
        first_half = _first_half_mask((tm, 128))
        ones = one_ref[...]
        cc_t, ss_t = cc_ref[...], ss_ref[...]

        def norm_rope_bwd(dy, xc, gain):
            dn = dy * cc_t - _swap_halves(dy, first_half) * ss_t
            r = lax.rsqrt(_xdot(xc * xc, ones) * (1.0 / HEAD_DIM) + EPS)
            xh = xc * r
            gdy = dn * gain
            dx = r * (gdy - xh * (_xdot(gdy * xh, ones) * (1.0 / HEAD_DIM)))
            return dx, jnp.sum(dn * xh, axis=0, keepdims=True)

        dgq = jnp.zeros((1, 128), F32)
        parts = []
        for c in range(D_A // 128):
            sl = slice(128 * c, 128 * (c + 1))
            dx, dg = norm_rope_bwd(dqr_ref[:, sl] * Q_SCALE, qpre_ref[:, sl], gq_ref[...])
            parts.append(dx)
            dgq = dgq + dg
        dxk, dgk = norm_rope_bwd(dkr_ref[...], kpre_ref[...], gk_ref[...])
        _acc_out(dgq_ref, dgq)
        _acc_out(dgk_ref, dgk)
        parts += [dxk, dva_ref[...], (dq0[...] + dq1[...] + dq2[...]) * Q_SCALE, dk0[...] + dk1[...] + dk2[...],
                  dv0[...] + dv1[...] + dv2[...]]
        dproj = jnp.concatenate(parts, axis=-1).astype(BF16)
        dproj_ref[...] = dproj
        dxn = _mm_nt(dproj, w_ref[...])
        xh, r = _rms_stats(x_ref[...])
        dx, dg = _rms_bwd(dxn, xh, r, g_ref[...])
        _acc_out(dg1_ref, dg)
        gx_ref[...] = dh1_ref[...] + dx

    sds = jax.ShapeDtypeStruct
    return pl.pallas_call(
        body, name="in_proj_bwd", grid=(s_len // tm,),
        in_specs=[_rows(tm, D_A), _rows(tm, D_KV_A), _rows(tm, D_KV_A)] + [_rows(tm, D_B)] * 9
                 + [_rows(tm, D_A), _rows(tm, D_KV_A), _rows(tm, D_MODEL), _rows(tm, D_MODEL),
                    _full((1, D_MODEL)), _full((D_MODEL, D_IN)), _rows(tm, 128), _rows(tm, 128), _full((1, 128)),
                    _full((1, 128)), _full((128, 128))],
        out_specs=[_rows(tm, D_IN), _rows(tm, D_MODEL), _full((1, D_MODEL)), _full((1, 128)), _full((1, 128))],
        out_shape=[sds((s_len, D_IN), BF16), sds((s_len, D_MODEL), F32), sds((1, D_MODEL), F32),
                   sds((1, 128), F32), sds((1, 128), F32)],
        compiler_params=_cparams(("arbitrary",)),
    )(dqr, dkr, dva, *dqb, *dkb, *dvb, qpre, kpre, x, dh1, g1, w_in, cc, ss, gq2, gk2, ones128)


def _dw(a, b, name, relu2=False):
    s_len, ka = a.shape
    n = b.shape[1]
    ts = min(DW_TS, s_len)
    bk = min(ka, 1024)
    bn = n if n % 1024 else 1024

    def body(a_ref, b_ref, o_ref):
        @pl.when(pl.program_id(2) == 0)
        def _():
            o_ref[...] = jnp.zeros_like(o_ref)

        av = a_ref[...]
        if relu2:
            av = jnp.square(jnp.maximum(av, 0.0))
        o_ref[...] += _mm_tn(av.astype(BF16), b_ref[...])

    return pl.pallas_call(
        body, name=name, grid=(ka // bk, n // bn, s_len // ts),
        in_specs=[pl.BlockSpec((ts, bk), lambda i, j, k: (k, i)), pl.BlockSpec((ts, bn), lambda i, j, k: (k, j))],
        out_specs=pl.BlockSpec((bk, bn), lambda i, j, k: (i, j)),
        out_shape=jax.ShapeDtypeStruct((ka, n), F32),
        compiler_params=_cparams(("parallel", "parallel", "arbitrary")),
    )(a, b)


def _attn_a_fwd(qpad, kv, ov):
    s_len = kv.shape[1]
    tq = min(ATT_TQ, s_len)
    tk = min(ATT_TK_FWD, s_len)
    rows = GROUP_A * tq

    def body(q_ref, kv_ref, ov_ref, o_ref, lse_ref):
        q = q_ref[...].reshape(rows, 128)

        def step(j, carry):
            m, acc = carry
            span = pl.ds(pl.multiple_of(j * tk, tk), tk)
            s = _mm_nt(q, kv_ref[0, span, :])
            m_new = jnp.maximum(m, jnp.max(s, axis=1, keepdims=True))
            p = jnp.exp(s - m_new).astype(BF16)
            acc = jnp.exp(m - m_new) * acc + _mm(p, ov_ref[0, span, :])
            return m_new, acc

        init = (jnp.full((rows, 1), -jnp.inf, F32), jnp.zeros((rows, 128), F32))
        m, acc = lax.fori_loop(0, s_len // tk, step, init)
        o_ref[...] = (acc / pltpu.roll(acc, HEAD_DIM, 1)).reshape(GROUP_A, tq, 128)
        lse_ref[...] = (m + jnp.log(acc[:, :1])).reshape(GROUP_A, tq, 1)

    return pl.pallas_call(
        body, name="attn_a_fwd", grid=(N_KV_A, s_len // tq),
        in_specs=[pl.BlockSpec((GROUP_A, tq, 128), lambda g, i: (g, i, 0)),
                  pl.BlockSpec((1, s_len, 128), lambda g, i: (g, 0, 0)),
                  pl.BlockSpec((1, s_len, 128), lambda g, i: (g, 0, 0))],
        out_specs=[pl.BlockSpec((GROUP_A, tq, 128), lambda g, i: (g, i, 0)),
                   pl.BlockSpec((GROUP_A, tq, 1), lambda g, i: (g, i, 0))],
        out_shape=[jax.ShapeDtypeStruct((N_HEADS_A, s_len, 128), F32),
                   jax.ShapeDtypeStruct((N_HEADS_A, s_len, 1), F32)],
        compiler_params=_cparams(("parallel", "parallel")),
    )(qpad, kv, ov)


def _attn_a_bwd(qpad, dopad, kv, lse, delta):
    s_len = kv.shape[1]
    tq = min(ATT_TQ, s_len)
    tk = min(ATT_TK_BWD, s_len)
    rows = GROUP_A * tq

    def body(q_ref, do_ref, kv_ref, lse_ref, dl_ref, dq_ref, dkv_ref):
        @pl.when(pl.program_id(1) == 0)
        def _():
            dkv_ref[...] = jnp.zeros_like(dkv_ref)

        q = q_ref[...].reshape(rows, 128)
        do = do_ref[...].reshape(rows, 128)
        lse_t = lse_ref[...].reshape(rows, 1)
        dl_t = dl_ref[...].reshape(rows, 1)

        def step(j, dq):
            span = pl.ds(pl.multiple_of(j * tk, tk), tk)
            kvj = kv_ref[0, span, :]
            p = jnp.exp(_mm_nt(q, kvj) - lse_t)
            ds = (p * (_mm_nt(do, kvj) - dl_t)).astype(BF16)
            dkv_ref[0, span, :] += _mm_tn(ds, q) + _mm_tn(p.astype(BF16), do)
            return dq + _mm(ds, kvj)

        dq = lax.fori_loop(0, s_len // tk, step, jnp.zeros((rows, 128), F32))
        dq_ref[...] = dq.reshape(GROUP_A, tq, 128)

    return pl.pallas_call(
        body, name="attn_a_bwd", grid=(N_KV_A, s_len // tq),
        in_specs=[pl.BlockSpec((GROUP_A, tq, 128), lambda g, i: (g, i, 0)),
                  pl.BlockSpec((GROUP_A, tq, 128), lambda g, i: (g, i, 0)),
                  pl.BlockSpec((1, s_len, 128), lambda g, i: (g, 0, 0)),
                  pl.BlockSpec((GROUP_A, tq, 1), lambda g, i: (g, i, 0)),
                  pl.BlockSpec((GROUP_A, tq, 1), lambda g, i: (g, i, 0))],
        out_specs=[pl.BlockSpec((GROUP_A, tq, 128), lambda g, i: (g, i, 0)),
                   pl.BlockSpec((1, s_len, 128), lambda g, i: (g, 0, 0))],
        out_shape=[jax.ShapeDtypeStruct((N_HEADS_A, s_len, 128), F32),
                   jax.ShapeDtypeStruct((N_KV_A, s_len, 128), F32)],
        compiler_params=_cparams(("parallel", "arbitrary")),
    )(qpad, dopad, kv, lse, delta)


class _SwaGeometry:
    def __init__(self, s_len, r):
        self.r = r
        self.tq = SWA_TQ
        self.block = min(max(SWA_MIN_BLOCK, SWA_TQ * r), s_len)
        self.halo = HALF_WIN * r
        self.nsub = self.block // (self.tq * r)
        self.band = self.tq + 2 * HALF_WIN
        self.length = s_len // r
        self.nblk = s_len // self.block
        self.nhalo = s_len // self.halo
        assert self.nsub * self.tq * r == self.block and self.block % self.halo == 0

    def specs(self):
        per = self.block // self.halo
        cur = pl.BlockSpec((self.block, 128), lambda c, i: (i, c))
        prev = pl.BlockSpec((self.halo, 128), lambda c, i: (jnp.maximum(i * per - 1, 0), c))
        nxt = pl.BlockSpec((self.halo, 128), lambda c, i: (jnp.minimum((i + 1) * per, self.nhalo - 1), c))
        return prev, cur, nxt

    def tiles(self):
        return [(rho + self.r * j * self.tq, self.halo + rho + self.r * (j * self.tq - HALF_WIN), j)
                for j in range(self.nsub) for rho in range(self.r)]

    def own(self, start):
        return pl.ds(start, self.tq, stride=self.r)

    def around(self, start):
        return pl.ds(start, self.band, stride=self.r)

    def fill(self, dst, prev_ref, cur_ref, next_ref):
        dst[:self.halo, :] = prev_ref[...].astype(F32)
        dst[self.halo:self.halo + self.block, :] = cur_ref[...].astype(F32)
        dst[self.halo + self.block:, :] = next_ref[...].astype(F32)

    def first_position(self, j):
        return (pl.program_id(1) * self.block) // self.r + j * self.tq

    def extended(self):
        return pltpu.VMEM((self.block + 2 * self.halo, 128), F32)

    def plain(self):
        return pltpu.VMEM((self.block, 128), F32)


def _low_lanes(rows):
    return lax.broadcasted_iota(jnp.int32, (rows, 128), 1) < HEAD_DIM


def _one_head(x, low, half):
    return jnp.where(low if half == 0 else jnp.logical_not(low), x, 0.0).astype(BF16)


def _swa_fwd(q, k, v, bias, r):
    geo = _SwaGeometry(q.shape[0], r)
    prev, cur, nxt = geo.specs()

    def body(q_ref, kp, kc, kn, vp, vc, vn, b_ref, o_ref, lse_ref, qf, kf, vf):
        qf[...] = q_ref[...].astype(F32)
        geo.fill(kf, kp, kc, kn)
        geo.fill(vf, vp, vc, vn)
        low_q, low_b = _low_lanes(geo.tq), _low_lanes(geo.band)
        row = lax.broadcasted_iota(jnp.int32, (geo.tq, geo.band), 0)
        col = lax.broadcasted_iota(jnp.int32, (geo.tq, geo.band), 1)
        in_window = jnp.abs(col - HALF_WIN - row) <= HALF_WIN
        for own, around, j in geo.tiles():
            key = geo.first_position(j) - HALF_WIN + col
            valid = in_window & (key >= 0) & (key < geo.length)
            qs = qf[geo.own(own), :]
            kb = kf[geo.around(around), :].astype(BF16)
            vb = vf[geo.around(around), :]
            outs, lses = [], []
            for half in range(2):
                s = jnp.where(valid, _mm_nt(_one_head(qs, low_q, half), kb) + b_ref[half], NEG_BIG)
                m = jnp.max(s, axis=1, keepdims=True)
                e = jnp.exp(s - m)
                l = jnp.sum(e, axis=1, keepdims=True)
                outs.append(_mm(e.astype(BF16), _one_head(vb, low_b, half)) / l)
                lses.append(m + jnp.log(l))
            o_ref[geo.own(own), :] = outs[0] + outs[1]
            lse_ref[geo.own(own), :] = jnp.where(low_q, lses[0], lses[1])

    sds = jax.ShapeDtypeStruct
    return pl.pallas_call(
        body, name="swa_fwd_%d" % r, grid=(D_B // 128, geo.nblk),
        in_specs=[cur, prev, cur, nxt, prev, cur, nxt, pl.BlockSpec((2, geo.tq, geo.band), lambda c, i: (c, 0, 0))],
        out_specs=[cur, cur],
        out_shape=[sds(q.shape, F32), sds(q.shape, F32)],
        scratch_shapes=[geo.plain(), geo.extended(), geo.extended()],
        compiler_params=_cparams(("parallel", "parallel")),
    )(q, k, k, k, v, v, v, bias)


def _head_stats(st, half):
    lo = HEAD_DIM * half
    return st[:, lo:lo + 1], st[:, lo + HEAD_DIM // 2:lo + HEAD_DIM // 2 + 1]


def _swa_bwd_q(q, k, v, dy, st, bias, r):
    geo = _SwaGeometry(q.shape[0], r)
    prev, cur, nxt = geo.specs()
    bias_spec = pl.BlockSpec((2, geo.tq, geo.band), lambda c, i: (c, 0, 0))

    def body(q_ref, kp, kc, kn, vp, vc, vn, dy_ref, st_ref, b_ref, dq_ref, db_ref, qf, kf, vf, dyf):
        @pl.when(pl.program_id(1) == 0)
        def _():
            db_ref[...] = jnp.zeros_like(db_ref)

        qf[...] = q_ref[...].astype(F32)
        dyf[...] = dy_ref[...].astype(F32)
        geo.fill(kf, kp, kc, kn)
        geo.fill(vf, vp, vc, vn)
        low_q, low_b = _low_lanes(geo.tq), _low_lanes(geo.band)
        row = lax.broadcasted_iota(jnp.int32, (geo.tq, geo.band), 0)
        col = lax.broadcasted_iota(jnp.int32, (geo.tq, geo.band), 1)
        in_window = jnp.abs(col - HALF_WIN - row) <= HALF_WIN
        for own, around, j in geo.tiles():
            key = geo.first_position(j) - HALF_WIN + col
            valid = in_window & (key >= 0) & (key < geo.length)
            qs = qf[geo.own(own), :]
            dys = dyf[geo.own(own), :]
            sts = st_ref[geo.own(own), :]
            kb = kf[geo.around(around), :]
            vb = vf[geo.around(around), :].astype(BF16)
            dq = jnp.zeros((geo.tq, 128), F32)
            for half in range(2):
                lse, delta = _head_stats(sts, half)
                s = jnp.where(valid, _mm_nt(_one_head(qs, low_q, half), kb.astype(BF16)) + b_ref[half], NEG_BIG)
                p = jnp.exp(s - lse)
                ds = p * (_mm_nt(_one_head(dys, low_q, half), vb) - delta)
                db_ref[half] += ds
                dq = dq + _mm(ds.astype(BF16), _one_head(kb, low_b, half))
            dq_ref[geo.own(own), :] = dq

    return pl.pallas_call(
        body, name="swa_bwd_q_%d" % r, grid=(D_B // 128, geo.nblk),
        in_specs=[cur, prev, cur, nxt, prev, cur, nxt, cur, cur, bias_spec],
        out_specs=[cur, bias_spec],
        out_shape=[jax.ShapeDtypeStruct(q.shape, F32), jax.ShapeDtypeStruct(bias.shape, F32)],
        scratch_shapes=[geo.plain(), geo.extended(), geo.extended(), geo.plain()],
        compiler_params=_cparams(("parallel", "arbitrary")),
    )(q, k, k, k, v, v, v, dy, st, bias)


def _swa_bwd_kv(q, k, v, dy, st, bias_kv, r):
    geo = _SwaGeometry(q.shape[0], r)
    prev, cur, nxt = geo.specs()

    def body(k_ref, v_ref, qp, qc, qn, dp_, dc_, dn_, sp, sc, sn, b_ref, dk_ref, dv_ref, kf, vf, qf, dyf, stf):
        kf[...] = k_ref[...].astype(F32)
        vf[...] = v_ref[...].astype(F32)
        geo.fill(qf, qp, qc, qn)
        geo.fill(dyf, dp_, dc_, dn_)
        geo.fill(stf, sp, sc, sn)
        low_k, low_b = _low_lanes(geo.tq), _low_lanes(geo.band)
        row = lax.broadcasted_iota(jnp.int32, (geo.band, geo.tq), 0)
        col = lax.broadcasted_iota(jnp.int32, (geo.band, geo.tq), 1)
        in_window = jnp.abs(col + HALF_WIN - row) <= HALF_WIN
        for own, around, j in geo.tiles():
            query = geo.first_position(j) - HALF_WIN + row
            valid = in_window & (query >= 0) & (query < geo.length)
            ks = kf[geo.own(own), :].astype(BF16)
            vs = vf[geo.own(own), :].astype(BF16)
            qb = qf[geo.around(around), :]
            dyb = dyf[geo.around(around), :]
            stb = stf[geo.around(around), :]
            dk = jnp.zeros((geo.tq, 128), F32)
            dv = jnp.zeros((geo.tq, 128), F32)
            for half in range(2):
                lse, delta = _head_stats(stb, half)
                q_h, dy_h = _one_head(qb, low_b, half), _one_head(dyb, low_b, half)
                s = jnp.where(valid, _mm_nt(q_h, ks) + b_ref[half], NEG_BIG)
                p = jnp.exp(s - lse)
                ds = p * (_mm_nt(dy_h, vs) - delta)
                dv = dv + _mm_tn(p.astype(BF16), dy_h)
                dk = dk + _mm_tn(ds.astype(BF16), q_h)
            dk_ref[geo.own(own), :] = dk
            dv_ref[geo.own(own), :] = dv

    return pl.pallas_call(
        body, name="swa_bwd_kv_%d" % r, grid=(D_B // 128, geo.nblk),
        in_specs=[cur, cur, prev, cur, nxt, prev, cur, nxt, prev, cur, nxt,
                  pl.BlockSpec((2, geo.band, geo.tq), lambda c, i: (c, 0, 0))],
        out_specs=[cur, cur],
        out_shape=[jax.ShapeDtypeStruct(q.shape, F32), jax.ShapeDtypeStruct(q.shape, F32)],
        scratch_shapes=[geo.plain(), geo.plain(), geo.extended(), geo.extended(), geo.extended()],
        compiler_params=_cparams(("parallel", "parallel")),
    )(k, v, q, q, q, dy, dy, dy, st, st, st, bias_kv)


BIAS_ROWS = 16
BIAS_TN = 4096


def _bias_tiles(onehot, rel_bias_t):
    n = onehot.shape[0]

    def body(oh_ref, rb_ref, o_ref):
        o_ref[...] = sum(_mm_nt(piece, oh_ref[...]) for piece in _split3(rb_ref[...]))

    return pl.pallas_call(
        body, name="bias_tiles", grid=(n // BIAS_TN,),
        in_specs=[_rows(BIAS_TN, 128), _full((BIAS_ROWS, 128))],
        out_specs=pl.BlockSpec((BIAS_ROWS, BIAS_TN), lambda i: (0, i)),
        out_shape=jax.ShapeDtypeStruct((BIAS_ROWS, n), F32),
        compiler_params=_cparams(("parallel",)),
    )(onehot, rel_bias_t)


def _bias_bwd(onehot, dbias_rows, so_far, r):
    n = onehot.shape[0]

    def body(oh, d, prev_ref, g_ref):
        @pl.when(pl.program_id(0) == 0)
        def _():
            g_ref[...] = prev_ref[...]

        hi, lo, _ = _split3(d[...])
        g_ref[...] += _mm(hi, oh[...]) + _mm(lo, oh[...])

    return pl.pallas_call(
        body, name="bias_bwd_%d" % r, grid=(n // BIAS_TN,),
        in_specs=[_rows(BIAS_TN, 128), pl.BlockSpec((BIAS_ROWS, BIAS_TN), lambda i: (0, i)), _full((BIAS_ROWS, 128))],
        out_specs=_full((BIAS_ROWS, 128)),
        out_shape=jax.ShapeDtypeStruct((BIAS_ROWS, 128), F32),
        compiler_params=_cparams(("arbitrary",)),
    )(onehot, dbias_rows, so_far)


def _to_heads(a, n_heads):
    return a.reshape(a.shape[0], n_heads, HEAD_DIM).transpose(1, 0, 2)


def _from_heads(a):
    return a.transpose(1, 0, 2).reshape(a.shape[1], a.shape[0] * HEAD_DIM)


def _pad_lanes(a, left):
    z = jnp.zeros_like(a)
    return jnp.concatenate([z, a] if left else [a, z], axis=-1)


def _local_step(x, p, tgt, w_in, w_out, w_ff1, w_ff2, w_gate, w_ple, g_attn_pre, g_q, g_k, g_out_a, g_out_b,
                g_attn_post, rel_bias, g_mlp_pre, g_mlp_post, g_ple):
    s_len = x.shape[0]
    cc, ss = _rope_tables(s_len)
    gq2 = jnp.concatenate([g_q, g_q], axis=-1)
    gk2 = jnp.concatenate([g_k, g_k], axis=-1)
    ones128 = _group_ones(128)
    rel_bias_t = jnp.zeros((BIAS_ROWS, 128), F32).at[:N_HEADS_B, :N_BUCKETS].set(rel_bias.T)

    xn1, qpre, kpre, qa, ka, va, qb, kb, vb = _in_proj(x, g_attn_pre, w_in, cc, ss, gq2, gk2, ones128)

    qpad = _pad_lanes(_to_heads(qa, N_HEADS_A), left=False)
    v_heads = _to_heads(va, N_KV_A)
    kv = jnp.concatenate([_to_heads(ka, N_KV_A), v_heads], axis=-1)
    oa, lse_a = _attn_a_fwd(qpad, kv, jnp.concatenate([jnp.ones_like(v_heads), v_heads], axis=-1))
    ya = _from_heads(oa[:, :, HEAD_DIM:])

    tiles, outs, lses = [], [], []
    for r in DILATIONS:
        tq = SWA_TQ
        onehot = _bucket_onehot(tq, r)
        bias = _bias_tiles(onehot, rel_bias_t)[:N_HEADS_B].reshape(N_HEADS_B, tq, tq + 2 * HALF_WIN)
        o_r, lse_r = _swa_fwd(qb, kb, vb, bias, r)
        tiles.append((onehot, bias))
        outs.append(o_r)
        lses.append(lse_r)
    yb, lse_b = _merge_b(outs, lses)

    ycat, y2, h1, xn2 = _out_proj(ya, yb, x, g_out_a, g_out_b, w_out, g_attn_post, g_mlp_pre)
    u = _ff1(xn2, w_ff1)
    f2, h2, xn3 = _ff2(u, w_ff2, h1, g_mlp_post, g_ple)
    dh2, df2, dgl, dpp, loss, dg_ple, dg_mlp_post = _ple_loss(xn3, p, h2, f2, tgt, w_gate, w_ple, g_ple, g_mlp_post)

    grads = {"g_ple": dg_ple, "g_mlp_post": dg_mlp_post}
    grads["w_ple_gate"] = _dw(xn3, dgl, "dw_gate")
    grads["w_ple_proj"] = _dw(p, dpp, "dw_ple")
    grads["w_ff2"] = _dw(u, df2, "dw_ff2", relu2=True)
    du = _ff2_bwd(df2, w_ff2, u)
    grads["w_ff1"] = _dw(xn2, du, "dw_ff1")
    dh1, dy2, grads["g_mlp_pre"], grads["g_attn_post"] = _ff1_bwd(du, w_ff1, dh2, h1, y2, g_mlp_pre, g_attn_post)
    grads["w_out"] = _dw(ycat, dy2, "dw_out")
    dya, dyb, delta_a, st_b, grads["g_out_a"], grads["g_out_b"] = _out_proj_bwd(dy2, w_out, ya, yb, lse_b, g_out_a,
                                                                              g_out_b)

    dopad = _pad_lanes(_to_heads(dya, N_HEADS_A).astype(BF16), left=True)
    dq_a, dkv_a = _attn_a_bwd(qpad, dopad, kv, lse_a, delta_a)
    dqr = _from_heads(dq_a[:, :, :HEAD_DIM])
    dkr = _from_heads(dkv_a[:, :, :HEAD_DIM])
    dva = _from_heads(dkv_a[:, :, HEAD_DIM:])

    dqs, dks, dvs = [], [], []
    d_rel = jnp.zeros((BIAS_ROWS, 128), F32)
    for r, (onehot, bias) in zip(DILATIONS, tiles):
        dq_r, dbias = _swa_bwd_q(qb, kb, vb, dyb, st_b, bias, r)
        bias_kv = jnp.flip(bias, axis=(1, 2)).transpose(0, 2, 1)
        dk_r, dv_r = _swa_bwd_kv(qb, kb, vb, dyb, st_b, bias_kv, r)
        dbias_rows = jnp.pad(dbias.reshape(N_HEADS_B, -1), ((0, BIAS_ROWS - N_HEADS_B), (0, 0)))
        d_rel = _bias_bwd(onehot, dbias_rows, d_rel, r)
        dqs.append(dq_r)
        dks.append(dk_r)
        dvs.append(dv_r)
    grads["rel_bias"] = d_rel[:N_HEADS_B, :N_BUCKETS].T

    dproj, grad_x, grads["g_attn_pre"], dgq2, dgk2 = _in_proj_bwd(
        dqr, dkr, dva, dqs, dks, dvs, qpre, kpre, x, dh1, g_attn_pre, w_in, cc, ss, gq2, gk2, ones128)
    grads["g_q"] = dgq2[:, :HEAD_DIM] + dgq2[:, HEAD_DIM:]
    grads["g_k"] = dgk2[:, :HEAD_DIM] + dgk2[:, HEAD_DIM:]
    grads["w_in"] = _dw(xn1, dproj, "dw_in")
    return loss, grad_x, grads


ANY = pl.BlockSpec(memory_space=pl.ANY)


def _position():
    return lax.axis_index("x"), lax.axis_index("y"), lax.axis_index("c")


def _other_chips(x, y):
    return [(2 * (1 - x) + y, (1 - x, y)), (2 * x + (1 - y), (x, 1 - y)), (2 * (1 - x) + (1 - y), (1 - x, 1 - y))]


def _cast_shards(shards):
    def body(*refs):
        n = len(refs) // 2
        for i_ref, o_ref in zip(refs[:n], refs[n:]):
            o_ref[...] = i_ref[...].astype(BF16)

    return pl.pallas_call(
        body, name="cast_shards",
        in_specs=[pl.BlockSpec(memory_space=pltpu.VMEM)] * len(shards),
        out_specs=[pl.BlockSpec(memory_space=pltpu.VMEM)] * len(shards),
        out_shape=[jax.ShapeDtypeStruct(s.shape, BF16) for s in shards],
        compiler_params=_cparams(),
    )(*shards)


def _gather_weights(shards):
    n = len(shards)

    def body(*refs):
        ins, outs = refs[:n], refs[n:2 * n]
        send_sems, recv_sems, local_sems = refs[2 * n:]
        x, y, c = _position()
        me = 2 * x + y
        copies = []
        for a in range(n):
            local = pltpu.make_async_copy(ins[a], outs[a].at[me], local_sems.at[a])
            local.start()
            copies.append(local)
        sends = []
        for k, (_, chip) in enumerate(_other_chips(x, y)):
            for a in range(n):
                cp = pltpu.make_async_remote_copy(ins[a], outs[a].at[me], send_sems.at[k, a], recv_sems.at[k, a],
                                                  device_id=(*chip, c), device_id_type=MESH)
                cp.start()
                sends.append(cp)
        for k, (num, chip) in enumerate(_other_chips(x, y)):
            for a in range(n):
                pltpu.make_async_remote_copy(ins[a], outs[a].at[num], send_sems.at[k, a], recv_sems.at[k, a],
                                             device_id=(*chip, c), device_id_type=MESH).wait_recv()
        for cp in sends:
            cp.wait_send()
        for cp in copies:
            cp.wait()

    return pl.pallas_call(
        body, name="gather_weights",
        in_specs=[ANY] * n, out_specs=[ANY] * n,
        out_shape=[jax.ShapeDtypeStruct((N_CHIPS,) + s.shape, s.dtype) for s in shards],
        scratch_shapes=[pltpu.SemaphoreType.DMA((3, n)), pltpu.SemaphoreType.DMA((3, n)),
                        pltpu.SemaphoreType.DMA((n,))],
    )(*shards)


def _send_sibling_half(grads):
    n = len(grads)

    def body(*refs):
        ins, outs = refs[:n], refs[n:2 * n]
        send_sems, recv_sems = refs[2 * n:]
        x, y, c = _position()
        copies = []
        for a in range(n):
            half = ins[a].shape[1] // 2
            theirs = ins[a].at[:, pl.ds(pl.multiple_of((1 - c) * half, 8), half), :]
            cp = pltpu.make_async_remote_copy(theirs, outs[a], send_sems.at[a], recv_sems.at[a],
                                              device_id=(x, y, 1 - c), device_id_type=MESH)
            cp.start()
            copies.append(cp)
        for cp in copies:
            cp.wait()

    return pl.pallas_call(
        body, name="send_sibling_half",
        in_specs=[ANY] * n, out_specs=[ANY] * n,
        out_shape=[jax.ShapeDtypeStruct((g.shape[0], g.shape[1] // 2, g.shape[2]), g.dtype) for g in grads],
        scratch_shapes=[pltpu.SemaphoreType.DMA((n,)), pltpu.SemaphoreType.DMA((n,))],
    )(*grads)


def _scatter_to_chips(pairs):
    n = len(pairs)

    def body(*refs):
        ins, outs = refs[:n], refs[n:2 * n]
        send_sems, recv_sems, local_sems = refs[2 * n:]
        x, y, c = _position()
        me = 2 * x + y
        copies = []
        for a in range(n):
            local = pltpu.make_async_copy(ins[a].at[me], outs[a].at[me], local_sems.at[a])
            local.start()
            copies.append(local)
        sends = []
        for k, (num, chip) in enumerate(_other_chips(x, y)):
            for a in range(n):
                cp = pltpu.make_async_remote_copy(ins[a].at[num], outs[a].at[me], send_sems.at[k, a],
                                                  recv_sems.at[k, a], device_id=(*chip, c), device_id_type=MESH)
                cp.start()
                sends.append(cp)
        for k, (num, chip) in enumerate(_other_chips(x, y)):
            for a in range(n):
                pltpu.make_async_remote_copy(ins[a].at[me], outs[a].at[num], send_sems.at[k, a], recv_sems.at[k, a],
                                             device_id=(*chip, c), device_id_type=MESH).wait_recv()
        for cp in sends:
            cp.wait_send()
        for cp in copies:
            cp.wait()

    return pl.pallas_call(
        body, name="scatter_to_chips",
        in_specs=[ANY] * n, out_specs=[ANY] * n,
        out_shape=[jax.ShapeDtypeStruct(g.shape, g.dtype) for g in pairs],
        scratch_shapes=[pltpu.SemaphoreType.DMA((3, n)), pltpu.SemaphoreType.DMA((3, n)),
                        pltpu.SemaphoreType.DMA((n,))],
    )(*pairs)


def _exchange_halves(halves):
    n = len(halves)

    def body(*refs):
        ins, outs = refs[:n], refs[n:2 * n]
        send_sems, recv_sems, local_sems = refs[2 * n:]
        x, y, c = _position()
        copies = []
        for a in range(n):
            local = pltpu.make_async_copy(ins[a], outs[a].at[c], local_sems.at[a])
            local.start()
            cp = pltpu.make_async_remote_copy(ins[a], outs[a].at[c], send_sems.at[a], recv_sems.at[a],
                                              device_id=(x, y, 1 - c), device_id_type=MESH)
            cp.start()
            copies += [local, cp]
        for a in range(n):
            pltpu.make_async_remote_copy(ins[a], outs[a].at[1 - c], send_sems.at[a], recv_sems.at[a],
                                         device_id=(x, y, 1 - c), device_id_type=MESH).wait_recv()
        for a in range(n):
            copies[2 * a].wait()
            copies[2 * a + 1].wait_send()

    return pl.pallas_call(
        body, name="exchange_halves",
        in_specs=[ANY] * n, out_specs=[ANY] * n,
        out_shape=[jax.ShapeDtypeStruct((2,) + h.shape, h.dtype) for h in halves],
        scratch_shapes=[pltpu.SemaphoreType.DMA((n,)), pltpu.SemaphoreType.DMA((n,)),
                        pltpu.SemaphoreType.DMA((n,))],
    )(*halves)


def _allreduce_small(v):
    def body(v_ref, o_ref, buf, send_sems, recv_sems):
        x, y, c = _position()
        me = 4 * x + 2 * y + c
        peers = [(1 - x, y, c), (x, 1 - y, c), (x, y, 1 - c), (1 - x, 1 - y, c), (1 - x, y, 1 - c), (x, 1 - y, 1 - c),
                 (1 - x, 1 - y, 1 - c)]
        num = lambda d: 4 * d[0] + 2 * d[1] + d[2]
        buf[me] = v_ref[...]
        sends = []
        for k, peer in enumerate(peers):
            cp = pltpu.make_async_remote_copy(v_ref, buf.at[me], send_sems.at[k], recv_sems.at[k], device_id=peer,
                                              device_id_type=MESH)
            cp.start()
            sends.append(cp)
        for k, peer in enumerate(peers):
            pltpu.make_async_remote_copy(v_ref, buf.at[num(peer)], send_sems.at[k], recv_sems.at[k], device_id=peer,
                                         device_id_type=MESH).wait_recv()
        for cp in sends:
            cp.wait_send()
        total = buf[0]
        for d in range(1, 8):
            total = total + buf[d]
        o_ref[...] = total

    return pl.pallas_call(
        body, name="allreduce_small",
        in_specs=[pl.BlockSpec(memory_space=pltpu.VMEM)], out_specs=pl.BlockSpec(memory_space=pltpu.VMEM),
        out_shape=jax.ShapeDtypeStruct(v.shape, v.dtype),
        scratch_shapes=[pltpu.VMEM((8,) + v.shape, v.dtype), pltpu.SemaphoreType.DMA((7,)),
                        pltpu.SemaphoreType.DMA((7,))],
    )(v)


def _sum_leading(a, name):
    k, r, c = a.shape
    tr = min(r, 256)

    def body(a_ref, o_ref):
        total = a_ref[0]
        for i in range(1, k):
            total = total + a_ref[i]
        o_ref[...] = total

    return pl.pallas_call(
        body, name=name, grid=(r // tr,),
        in_specs=[pl.BlockSpec((k, tr, c), lambda i: (0, i, 0))],
        out_specs=pl.BlockSpec((tr, c), lambda i: (i, 0)),
        out_shape=jax.ShapeDtypeStruct((r, c), a.dtype),
        compiler_params=_cparams(("parallel",)),
    )(a)


def _add(a, b, name):
    k, r, c = a.shape
    tr = min(r, 256)
    spec = pl.BlockSpec((k, tr, c), lambda i: (0, i, 0))

    def body(a_ref, b_ref, o_ref):
        o_ref[...] = a_ref[...] + b_ref[...]

    return pl.pallas_call(
        body, name=name, grid=(r // tr,), in_specs=[spec, spec], out_specs=spec,
        out_shape=jax.ShapeDtypeStruct(a.shape, a.dtype), compiler_params=_cparams(("parallel",)),
    )(a, b)


def _adamw(w, g, m, v, name):
    r, c = w.shape
    tr = min(r, 256)
    spec = pl.BlockSpec((tr, c), lambda i: (i, 0))

    def body(w_ref, g_ref, m_ref, v_ref, d_ref, nm_ref, nv_ref):
        gv = g_ref[...]
        nm = ADAM_B1 * m_ref[...] + (1.0 - ADAM_B1) * gv
        nv = ADAM_B2 * v_ref[...] + (1.0 - ADAM_B2) * jnp.square(gv)
        m_hat = nm / (1.0 - ADAM_B1 ** ADAM_STEP)
        v_hat = nv / (1.0 - ADAM_B2 ** ADAM_STEP)
        d_ref[...] = -ADAM_LR * (m_hat / (jnp.sqrt(v_hat) + ADAM_EPS) + ADAM_WD * w_ref[...])
        nm_ref[...] = nm
        nv_ref[...] = nv

    return pl.pallas_call(
        body, name=name, grid=(r // tr,), in_specs=[spec] * 4, out_specs=[spec] * 3,
        out_shape=[jax.ShapeDtypeStruct(w.shape, F32)] * 3, compiler_params=_cparams(("parallel",)),
    )(w, g, m, v)


MATRICES = ("w_in", "w_out", "w_ff1", "w_ff2", "w_ple_gate", "w_ple_proj")
COLUMN_SHARDED = ("w_in", "w_ff1", "w_ple_proj")
SMALL = ("g_attn_pre", "g_q", "g_k", "g_out_a", "g_out_b", "g_attn_post", "rel_bias", "g_mlp_pre", "g_mlp_post",
         "g_ple")
WEIGHT_ORDER = ("w_in", "g_attn_pre", "g_q", "g_k", "g_out_a", "g_out_b", "w_out", "g_attn_post", "rel_bias",
                "g_mlp_pre", "w_ff1", "w_ff2", "g_mlp_post", "g_ple", "w_ple_gate", "w_ple_proj")
PACK_ROWS, PACK_COLS = 8, 1024


def _pack_small(values, extra=None):
    flat = [values[n].reshape(-1) for n in SMALL]
    used = sum(f.shape[0] for f in flat)
    tail = jnp.zeros((PACK_ROWS * PACK_COLS - used - 1,), F32)
    last = jnp.zeros((1,), F32) if extra is None else extra.reshape(1)
    return jnp.concatenate(flat + [tail, last]).reshape(PACK_ROWS, PACK_COLS)


def _unpack_small(packed, like):
    flat = packed.reshape(-1)
    out, o = {}, 0
    for n in SMALL:
        size = like[n].size
        out[n] = flat[o:o + size].reshape(like[n].shape)
        o += size
    return out, flat[-1]


def kernel(x, p, w_in, g_attn_pre, g_q, g_k, g_out_a, g_out_b, w_out, g_attn_post, rel_bias, g_mlp_pre, w_ff1, w_ff2, g_mlp_post, g_ple, w_ple_gate, w_ple_proj, loss_target, m_w_in, m_g_attn_pre, m_g_q, m_g_k, m_g_out_a, m_g_out_b, m_w_out, m_g_attn_post, m_rel_bias, m_g_mlp_pre, m_w_ff1, m_w_ff2, m_g_mlp_post, m_g_ple, m_w_ple_gate, m_w_ple_proj, v_w_in, v_g_attn_pre, v_g_q, v_g_k, v_g_out_a, v_g_out_b, v_w_out, v_g_attn_post, v_rel_bias, v_g_mlp_pre, v_w_ff1, v_w_ff2, v_g_mlp_post, v_g_ple, v_w_ple_gate, v_w_ple_proj):
    given = dict(locals())
    weights = {n: given[n] for n in WEIGHT_ORDER}
    shards = {n: weights[n][0] for n in MATRICES}

    gathered = _gather_weights(_cast_shards([shards[n] for n in MATRICES]))
    whole = {}
    for n, g in zip(MATRICES, gathered):
        if n in COLUMN_SHARDED:
            whole[n] = g.transpose(1, 0, 2).reshape(g.shape[1], N_CHIPS * g.shape[2])
        else:
            whole[n] = g.reshape(N_CHIPS * g.shape[1], g.shape[2])

    loss, grad_x, grads = _local_step(
        x[0], p[0, 0], loss_target[0], whole["w_in"], whole["w_out"], whole["w_ff1"], whole["w_ff2"],
        whole["w_ple_gate"], whole["w_ple_proj"], g_attn_pre, g_q, g_k, g_out_a, g_out_b, g_attn_post, rel_bias,
        g_mlp_pre, g_mlp_post, g_ple)

    by_chip = []
    for n in MATRICES:
        g = grads[n]
        if n in COLUMN_SHARDED:
            by_chip.append(g.reshape(g.shape[0], N_CHIPS, g.shape[1] // N_CHIPS).transpose(1, 0, 2))
        else:
            by_chip.append(g.reshape(N_CHIPS, g.shape[0] // N_CHIPS, g.shape[1]))
    from_sibling = _send_sibling_half(by_chip)
    c = lax.axis_index("c")
    pairs = []
    for n, g, other in zip(MATRICES, by_chip, from_sibling):
        half = g.shape[1] // 2
        mine = lax.dynamic_slice_in_dim(g, c * half, half, axis=1)
        pairs.append(_add(mine, other, "pair_sum_" + n))
    from_chips = _scatter_to_chips(pairs)
    halves = [_sum_leading(g, "chip_sum_" + n) for n, g in zip(MATRICES, from_chips)]
    both = _exchange_halves(halves)
    grad_w = {n: g.reshape(2 * g.shape[1], g.shape[2]) for n, g in zip(MATRICES, both)}

    small_like = {n: weights[n] for n in SMALL}
    reduced = _allreduce_small(_pack_small({n: grads[n] for n in SMALL}, extra=loss))
    grad_small, loss_total = _unpack_small(reduced, small_like)

    delta, new_m, new_v = {}, {}, {}
    for n in MATRICES:
        d, nm, nv = _adamw(shards[n], grad_w[n], given["m_" + n][0], given["v_" + n][0], "adamw_" + n)
        delta[n], new_m[n], new_v[n] = d[None], nm[None], nv[None]
        grad_w[n] = grad_w[n][None]
    d, nm, nv = _adamw(_pack_small(small_like), reduced, _pack_small({n: given["m_" + n] for n in SMALL}),
                       _pack_small({n: given["v_" + n] for n in SMALL}), "adamw_small")
    d_small, nm_small, nv_small = (_unpack_small(a, small_like)[0] for a in (d, nm, nv))
    for n in SMALL:
        grad_w[n], delta[n], new_m[n], new_v[n] = grad_small[n], d_small[n], nm_small[n], nv_small[n]

    return (loss_total, grad_x[None], *[grad_w[n] for n in WEIGHT_ORDER], *[delta[n] for n in WEIGHT_ORDER],
            *[new_m[n] for n in WEIGHT_ORDER], *[new_v[n] for n in WEIGHT_ORDER])
```

```python
import functools
import math

import jax
import jax.numpy as jnp
from jax import lax
from jax.experimental import pallas as pl
from jax.experimental.pallas import tpu as pltpu

F32 = jnp.float32
BF16 = jnp.bfloat16

D_MODEL = 1024
HEAD_DIM = 64
N_HEADS_A = 8
N_KV_A = 2
GROUP_A = N_HEADS_A // N_KV_A
N_HEADS_B = 8
D_A = N_HEADS_A * HEAD_DIM
D_KV_A = N_KV_A * HEAD_DIM
D_B = N_HEADS_B * HEAD_DIM
D_IN = D_A + 2 * D_KV_A + 3 * D_B
D_FF = 4 * D_MODEL
D_PLE = 256
GRID_W = 64
ROPE_THETA = 10000.0
DILATIONS = (1, 4, 16)
HALF_WIN = 64
N_BUCKETS = 32
MAX_DISTANCE = 1024
EPS = 1e-6
NEG_BIG = -1e30
Q_SCALE = HEAD_DIM ** -0.5

ADAM_LR = 0.001
ADAM_B1 = 0.9
ADAM_B2 = 0.999
ADAM_EPS = 1e-08
ADAM_WD = 0.01
ADAM_STEP = 10

N_CHIPS = 4
MESH = pl.DeviceIdType.MESH

ROW_TILE = 512
ATT_TQ = 256
ATT_TK_FWD = 2048
ATT_TK_BWD = 1024
SWA_TQ = 256
SWA_MIN_BLOCK = 1024
DW_TS = 1024
VMEM_LIMIT = 56 * 1024 * 1024

NT = (((1,), (1,)), ((), ()))
TN = (((0,), (0,)), ((), ()))


def _cparams(sem=None, vmem=VMEM_LIMIT):
    return pltpu.CompilerParams(dimension_semantics=sem, vmem_limit_bytes=vmem)


def _full(shape):
    n = len(shape)
    return pl.BlockSpec(shape, lambda *_: (0,) * n)


def _rows(tm, width):
    return pl.BlockSpec((tm, width), lambda i: (i, 0))


def _split3(a):
    a1 = a.astype(BF16)
    r = a - a1.astype(F32)
    a2 = r.astype(BF16)
    a3 = (r - a2.astype(F32)).astype(BF16)
    return a1, a2, a3


def _xdot(a, sel):
    a1, a2, a3 = _split3(a)
    d = lambda p: jnp.dot(p, sel, preferred_element_type=F32)
    return d(a1) + d(a2) + d(a3)


def _mm(a, b):
    return jnp.dot(a, b, preferred_element_type=F32)


def _mm_nt(a, b):
    return lax.dot_general(a, b, NT, preferred_element_type=F32)


def _mm_tn(a, b):
    return lax.dot_general(a, b, TN, preferred_element_type=F32)


def _rms_stats(x):
    r = lax.rsqrt(jnp.mean(x * x, axis=-1, keepdims=True) + EPS)
    return x * r, r


def _rms_bwd(dy, xh, r, g):
    gdy = dy * g
    dx = r * (gdy - xh * jnp.mean(gdy * xh, axis=-1, keepdims=True))
    dg = jnp.sum(dy * xh, axis=0, keepdims=True)
    return dx, dg


def _acc_out(ref, val):
    @pl.when(pl.program_id(0) == 0)
    def _():
        ref[...] = jnp.zeros_like(ref)

    ref[...] += val


def _swap_halves(x, first_half):
    return jnp.where(first_half, pltpu.roll(x, 96, 1), pltpu.roll(x, 32, 1))


def _first_half_mask(shape):
    return (lax.broadcasted_iota(jnp.int32, shape, 1) % HEAD_DIM) < (HEAD_DIM // 2)


def _rope_tables(s_len):
    t = jnp.arange(s_len)
    row = (t // GRID_W).astype(F32)
    col = (t % GRID_W).astype(F32)
    n_axis = HEAD_DIM // 4
    inv_freq = ROPE_THETA ** (-jnp.arange(n_axis, dtype=F32) / n_axis)
    ang = jnp.concatenate([row[:, None] * inv_freq, col[:, None] * inv_freq], axis=-1)
    c, s = jnp.cos(ang), jnp.sin(ang)
    cc = jnp.concatenate([c, c, c, c], axis=-1)
    ss = jnp.concatenate([-s, s, -s, s], axis=-1)
    return cc, ss


def _group_ones(width):
    i = jnp.arange(width)
    return (i[:, None] // HEAD_DIM == i[None, :] // HEAD_DIM).astype(BF16)


def _t5_bucket(rel):
    nb = N_BUCKETS // 2
    max_exact = nb // 2
    side = jnp.where(rel > 0, nb, 0)
    n = jnp.abs(rel)
    large = max_exact + (jnp.log(jnp.maximum(n, max_exact).astype(F32) / max_exact)
                         / math.log(MAX_DISTANCE / max_exact) * (nb - max_exact)).astype(jnp.int32)
    large = jnp.minimum(large, nb - 1)
    return side + jnp.where(n < max_exact, n, large)


def _bucket_onehot(tq, dilation):
    qi = jnp.arange(tq)
    kj = jnp.arange(tq + 2 * HALF_WIN)
    rel = kj[None, :] - HALF_WIN - qi[:, None]
    bucket = _t5_bucket(rel * dilation).reshape(-1)
    return (bucket[:, None] == jnp.arange(128)[None, :]).astype(BF16)


def _in_proj(x, g1, w_in, cc, ss, gq2, gk2, ones128):
    s_len = x.shape[0]
    tm = min(ROW_TILE, s_len)

    def body(x_ref, g_ref, w_ref, cc_ref, ss_ref, gq_ref, gk_ref, one_ref,
             xn_ref, qpre_ref, kpre_ref, qa_ref, kv_ref, qb_ref, kb_ref, vb_ref):
        xh, _ = _rms_stats(x_ref[...])
        xn = (xh * g_ref[...]).astype(BF16)
        xn_ref[...] = xn
        proj = _mm(xn, w_ref[...])
        first_half = _first_half_mask((tm, 128))
        ones = one_ref[...]
        cc_t, ss_t = cc_ref[...], ss_ref[...]

        def norm_rope(xc, gain):
            ms = _xdot(xc * xc, ones) * (1.0 / HEAD_DIM)
            y = xc * lax.rsqrt(ms + EPS) * gain
            return y * cc_t + _swap_halves(y, first_half) * ss_t

        qpre_ref[...] = proj[:, :D_A]
        kpre_ref[...] = proj[:, D_A:D_A + D_KV_A]
        for c in range(D_A // 128):
            y = norm_rope(proj[:, 128 * c:128 * (c + 1)], gq_ref[...])
            qa_ref[:, 128 * c:128 * (c + 1)] = (y * Q_SCALE).astype(BF16)
        ka = norm_rope(proj[:, D_A:D_A + D_KV_A], gk_ref[...])
        o = D_A + D_KV_A
        va = proj[:, o:o + D_KV_A]
        low = _low_lanes(tm)
        kv_ref[0] = jnp.where(low, ka, pltpu.roll(va, HEAD_DIM, 1)).astype(BF16)
        kv_ref[1] = jnp.where(low, pltpu.roll(ka, HEAD_DIM, 1), va).astype(BF16)
        o += D_KV_A
        qb_ref[...] = (proj[:, o:o + D_B] * Q_SCALE).astype(BF16)
        kb_ref[...] = proj[:, o + D_B:o + 2 * D_B].astype(BF16)
        vb_ref[...] = proj[:, o + 2 * D_B:o + 3 * D_B].astype(BF16)

    sds = jax.ShapeDtypeStruct
    return pl.pallas_call(
        body, name="in_proj", grid=(s_len // tm,),
        in_specs=[_rows(tm, D_MODEL), _full((1, D_MODEL)), _full((D_MODEL, D_IN)), _rows(tm, 128), _rows(tm, 128),
                  _full((1, 128)), _full((1, 128)), _full((128, 128))],
        out_specs=[_rows(tm, D_MODEL), _rows(tm, D_A), _rows(tm, D_KV_A), _rows(tm, D_A),
                   pl.BlockSpec((N_KV_A, tm, 128), lambda i: (0, i, 0)), _rows(tm, D_B), _rows(tm, D_B),
                   _rows(tm, D_B)],
        out_shape=[sds((s_len, D_MODEL), BF16), sds((s_len, D_A), F32), sds((s_len, D_KV_A), F32),
                   sds((s_len, D_A), BF16), sds((N_KV_A, s_len, 128), BF16),
                   sds((s_len, D_B), BF16), sds((s_len, D_B), BF16), sds((s_len, D_B), BF16)],
        compiler_params=_cparams(("parallel",)),
    )(x, g1, w_in, cc, ss, gq2, gk2, ones128)


def _stat_spec(tm):
    return pl.BlockSpec((N_HEADS_B, tm, 1), lambda i: (0, i, 0))


def _merge_b(outs, lses):
    s_len = outs[0].shape[0]
    tm = min(ROW_TILE, s_len)

    def body(o0, o1, o2, l0, l1, l2, yb_ref, lse_ref):
        m_all = jnp.maximum(jnp.maximum(l0[...], l1[...]), l2[...])
        w = [jnp.exp(l[...] - m_all) for l in (l0, l1, l2)]
        den = w[0] + w[1] + w[2]
        yb_ref[...] = (w[0] * o0[...] + w[1] * o1[...] + w[2] * o2[...]) / den
        lse_ref[...] = m_all + jnp.log(den)

    return pl.pallas_call(
        body, name="merge_b", grid=(s_len // tm,),
        in_specs=[_rows(tm, D_B)] * 6,
        out_specs=[_rows(tm, D_B), _rows(tm, D_B)],
        out_shape=[jax.ShapeDtypeStruct((s_len, D_B), F32), jax.ShapeDtypeStruct((s_len, D_B), F32)],
        compiler_params=_cparams(("parallel",)),
    )(*outs, *lses)


def _out_proj(ya, yb, x, g_a, g_b, w_out, g_post, g_mlp_pre):
    s_len = x.shape[0]
    tm = min(ROW_TILE, s_len)

    def body(ya_ref, yb_ref, x_ref, ga_ref, gb_ref, w_ref, gp_ref, gm_ref, ycat_ref, y2_ref, h1_ref, xn2_ref):
        ah, _ = _rms_stats(ya_ref[...])
        bh, _ = _rms_stats(yb_ref[...])
        ycat = jnp.concatenate([ah * ga_ref[...], bh * gb_ref[...]], axis=-1).astype(BF16)
        ycat_ref[...] = ycat
        y2 = _mm(ycat, w_ref[...])
        y2_ref[...] = y2
        y2h, _ = _rms_stats(y2)
        h1 = x_ref[...] + y2h * gp_ref[...]
        h1_ref[...] = h1
        h1h, _ = _rms_stats(h1)
        xn2_ref[...] = (h1h * gm_ref[...]).astype(BF16)

    sds = jax.ShapeDtypeStruct
    return pl.pallas_call(
        body, name="out_proj", grid=(s_len // tm,),
        in_specs=[_rows(tm, D_A), _rows(tm, D_B), _rows(tm, D_MODEL), _full((1, D_A)), _full((1, D_B)),
                  _full((D_MODEL, D_MODEL)), _full((1, D_MODEL)), _full((1, D_MODEL))],
        out_specs=[_rows(tm, D_MODEL)] * 4,
        out_shape=[sds((s_len, D_MODEL), BF16), sds((s_len, D_MODEL), F32), sds((s_len, D_MODEL), F32),
                   sds((s_len, D_MODEL), BF16)],
        compiler_params=_cparams(("parallel",)),
    )(ya, yb, x, g_a, g_b, w_out, g_post, g_mlp_pre)


def _ff1(xn2, w_ff1):
    s_len = xn2.shape[0]
    tm = min(ROW_TILE, s_len)

    def body(x_ref, w_ref, u_ref):
        u_ref[...] = _mm(x_ref[...], w_ref[...])

    return pl.pallas_call(
        body, name="ff1", grid=(s_len // tm,),
        in_specs=[_rows(tm, D_MODEL), _full((D_MODEL, D_FF))],
        out_specs=_rows(tm, D_FF),
        out_shape=jax.ShapeDtypeStruct((s_len, D_FF), F32),
        compiler_params=_cparams(("parallel",)),
    )(xn2, w_ff1)


def _ff2(u, w_ff2, h1, g_post, g_ple):
    s_len = u.shape[0]
    tm = min(ROW_TILE, s_len)

    def body(u_ref, w_ref, h1_ref, gp_ref, gl_ref, f2_ref, h2_ref, xn3_ref):
        f = jnp.square(jnp.maximum(u_ref[...], 0.0)).astype(BF16)
        f2 = _mm(f, w_ref[...])
        f2_ref[...] = f2
        f2h, _ = _rms_stats(f2)
        h2 = h1_ref[...] + f2h * gp_ref[...]
        h2_ref[...] = h2
        h2h, _ = _rms_stats(h2)
        xn3_ref[...] = (h2h * gl_ref[...]).astype(BF16)

    sds = jax.ShapeDtypeStruct
    return pl.pallas_call(
        body, name="ff2", grid=(s_len // tm,),
        in_specs=[_rows(tm, D_FF), _full((D_FF, D_MODEL)), _rows(tm, D_MODEL), _full((1, D_MODEL)),
                  _full((1, D_MODEL))],
        out_specs=[_rows(tm, D_MODEL)] * 3,
        out_shape=[sds((s_len, D_MODEL), F32), sds((s_len, D_MODEL), F32), sds((s_len, D_MODEL), BF16)],
        compiler_params=_cparams(("parallel",)),
    )(u, w_ff2, h1, g_post, g_ple)


def _ple_loss(xn3, p, h2, f2, tgt, w_gate, w_ple, g_ple, g_mlp_post):
    s_len = h2.shape[0]
    tm = min(ROW_TILE, s_len)

    def body(xn3_ref, p_ref, h2_ref, f2_ref, t_ref, wg_ref, wp_ref, gl_ref, gp_ref,
             dh2_ref, df2_ref, dgl_ref, dpp_ref, loss_ref, dgple_ref, dgpost_ref):
        gate = jax.nn.sigmoid(_mm(xn3_ref[...], wg_ref[...]))
        pp = _mm(p_ref[...].astype(BF16), wp_ref[...])
        h2 = h2_ref[...]
        err = h2 + gate * pp - t_ref[...]
        sq = jnp.sum(jnp.sum(err * err, axis=1, keepdims=True), axis=0, keepdims=True)
        _acc_out(loss_ref, sq * (0.5 / D_MODEL))
        dh3 = err * (1.0 / D_MODEL)
        dgl = (dh3 * pp) * gate * (1.0 - gate)
        dgl_b = dgl.astype(BF16)
        dgl_ref[...] = dgl_b
        dpp_ref[...] = (dh3 * gate).astype(BF16)
        dxn3 = _mm_nt(dgl_b, wg_ref[...])
        h2h, r2 = _rms_stats(h2)
        dx, dg = _rms_bwd(dxn3, h2h, r2, gl_ref[...])
        _acc_out(dgple_ref, dg)
        dh2 = dh3 + dx
        dh2_ref[...] = dh2
        f2h, rf = _rms_stats(f2_ref[...])
        df2, dg = _rms_bwd(dh2, f2h, rf, gp_ref[...])
        _acc_out(dgpost_ref, dg)
        df2_ref[...] = df2.astype(BF16)

    sds = jax.ShapeDtypeStruct
    return pl.pallas_call(
        body, name="ple_loss", grid=(s_len // tm,),
        in_specs=[_rows(tm, D_MODEL), _rows(tm, D_PLE), _rows(tm, D_MODEL), _rows(tm, D_MODEL), _rows(tm, D_MODEL),
                  _full((D_MODEL, D_MODEL)), _full((D_PLE, D_MODEL)), _full((1, D_MODEL)), _full((1, D_MODEL))],
        out_specs=[_rows(tm, D_MODEL)] * 3 + [_rows(tm, D_MODEL), _full((1, 1)), _full((1, D_MODEL)),
                                              _full((1, D_MODEL))],
        out_shape=[sds((s_len, D_MODEL), F32), sds((s_len, D_MODEL), BF16), sds((s_len, D_MODEL), BF16),
                   sds((s_len, D_MODEL), BF16), sds((1, 1), F32), sds((1, D_MODEL), F32), sds((1, D_MODEL), F32)],
        compiler_params=_cparams(("arbitrary",)),
    )(xn3, p, h2, f2, tgt, w_gate, w_ple, g_ple, g_mlp_post)


def _ff2_bwd(df2, w_ff2, u):
    s_len = u.shape[0]
    tm = min(ROW_TILE, s_len)

    def body(d_ref, w_ref, u_ref, du_ref):
        df = _mm_nt(d_ref[...], w_ref[...])
        du_ref[...] = (df * (2.0 * jnp.maximum(u_ref[...], 0.0))).astype(BF16)

    return pl.pallas_call(
        body, name="ff2_bwd", grid=(s_len // tm,),
        in_specs=[_rows(tm, D_MODEL), _full((D_FF, D_MODEL)), _rows(tm, D_FF)],
        out_specs=_rows(tm, D_FF),
        out_shape=jax.ShapeDtypeStruct((s_len, D_FF), BF16),
        compiler_params=_cparams(("parallel",)),
    )(df2, w_ff2, u)


def _ff1_bwd(du, w_ff1, dh2, h1, y2, g_mlp_pre, g_post):
    s_len = du.shape[0]
    tm = min(ROW_TILE, s_len)

    def body(du_ref, w_ref, dh2_ref, h1_ref, y2_ref, gm_ref, gp_ref, dh1_ref, dy2_ref, dgm_ref, dgp_ref):
        dxn2 = _mm_nt(du_ref[...], w_ref[...])
        h1h, r1 = _rms_stats(h1_ref[...])
        dx, dg = _rms_bwd(dxn2, h1h, r1, gm_ref[...])
        _acc_out(dgm_ref, dg)
        dh1 = dh2_ref[...] + dx
        dh1_ref[...] = dh1
        y2h, ry = _rms_stats(y2_ref[...])
        dy2, dg = _rms_bwd(dh1, y2h, ry, gp_ref[...])
        _acc_out(dgp_ref, dg)
        dy2_ref[...] = dy2.astype(BF16)

    sds = jax.ShapeDtypeStruct
    return pl.pallas_call(
        body, name="ff1_bwd", grid=(s_len // tm,),
        in_specs=[_rows(tm, D_FF), _full((D_MODEL, D_FF)), _rows(tm, D_MODEL), _rows(tm, D_MODEL),
                  _rows(tm, D_MODEL), _full((1, D_MODEL)), _full((1, D_MODEL))],
        out_specs=[_rows(tm, D_MODEL), _rows(tm, D_MODEL), _full((1, D_MODEL)), _full((1, D_MODEL))],
        out_shape=[sds((s_len, D_MODEL), F32), sds((s_len, D_MODEL), BF16), sds((1, D_MODEL), F32),
                   sds((1, D_MODEL), F32)],
        compiler_params=_cparams(("arbitrary",)),
    )(du, w_ff1, dh2, h1, y2, g_mlp_pre, g_post)


def _out_proj_bwd(dy2, w_out, ya, yb, lse_b, g_a, g_b):
    s_len = ya.shape[0]
    tm = min(ROW_TILE, s_len)

    def body(d_ref, w_ref, ya_ref, yb_ref, lse_ref, ga_ref, gb_ref, dya_ref, dyb_ref, da_ref, st_ref, dga_ref,
             dgb_ref):
        dycat = _mm_nt(d_ref[...], w_ref[...])
        lane = lax.broadcasted_iota(jnp.int32, (tm, 128), 1)
        low = lane < HEAD_DIM
        is_lse = (lane % HEAD_DIM) < (HEAD_DIM // 2)

        def head_sums(prod_chunk):
            return (jnp.sum(jnp.where(low, prod_chunk, 0.0), axis=1, keepdims=True),
                    jnp.sum(jnp.where(low, 0.0, prod_chunk), axis=1, keepdims=True))

        ya = ya_ref[...]
        yh, r = _rms_stats(ya)
        dya, dg = _rms_bwd(dycat[:, :D_A], yh, r, ga_ref[...])
        _acc_out(dga_ref, dg)
        dya_ref[...] = dya
        prod = dya * ya
        for c in range(D_A // 128):
            da_ref[2 * c], da_ref[2 * c + 1] = head_sums(prod[:, 128 * c:128 * (c + 1)])

        yb = yb_ref[...]
        yh, r = _rms_stats(yb)
        dyb, dg = _rms_bwd(dycat[:, D_A:], yh, r, gb_ref[...])
        _acc_out(dgb_ref, dg)
        dyb_ref[...] = dyb.astype(BF16)
        prod = dyb * yb
        for c in range(D_B // 128):
            sl = slice(128 * c, 128 * (c + 1))
            d_lo, d_hi = head_sums(prod[:, sl])
            st_ref[:, sl] = jnp.where(is_lse, lse_ref[:, sl], jnp.where(low, d_lo, d_hi))

    sds = jax.ShapeDtypeStruct
    return pl.pallas_call(
        body, name="out_proj_bwd", grid=(s_len // tm,),
        in_specs=[_rows(tm, D_MODEL), _full((D_MODEL, D_MODEL)), _rows(tm, D_A), _rows(tm, D_B), _rows(tm, D_B),
                  _full((1, D_A)), _full((1, D_B))],
        out_specs=[_rows(tm, D_A), _rows(tm, D_B), _stat_spec(tm), _rows(tm, D_B), _full((1, D_A)),
                   _full((1, D_B))],
        out_shape=[sds((s_len, D_A), F32), sds((s_len, D_B), BF16), sds((N_HEADS_A, s_len, 1), F32),
                   sds((s_len, D_B), F32), sds((1, D_A), F32), sds((1, D_B), F32)],
        compiler_params=_cparams(("arbitrary",)),
    )(dy2, w_out, ya, yb, lse_b, g_a, g_b)


def _in_proj_bwd(dqr, dkv, dqb, dkb, dvb, qpre, kpre, x, dh1, g1, w_in, cc, ss, gq2, gk2, ones128):
    s_len = x.shape[0]
    tm = min(ROW_TILE // 2, s_len)

    def body(dqr_ref, dkv_ref, dq0, dq1, dq2, dk0, dk1, dk2, dv0, dv1, dv2, qpre_ref, kpre_ref, x_ref,
             dh1_ref, g_ref, w_ref, cc_ref, ss_ref, gq_ref, gk_ref, one_ref, dproj_ref, gx_ref, dg1_ref, dgq_ref,
             dgk_ref):
        low = _low_lanes(tm)
        dkr = jnp.where(low, dkv_ref[0], pltpu.roll(dkv_ref[1], HEAD_DIM, 1))
        dva = jnp.where(low, pltpu.roll(dkv_ref[0], HEAD_DIM, 1), dkv_ref[1])
        first_half = _first_half_mask((tm, 128))
        ones = one_ref[...]
        cc_t, ss_t = cc_ref[...], ss_ref[...]

        def norm_rope_bwd(dy, xc, gain):
            dn = dy * cc_t - _swap_halves(dy, first_half) * ss_t
            r = lax.rsqrt(_xdot(xc * xc, ones) * (1.0 / HEAD_DIM) + EPS)
            xh = xc * r
            gdy = dn * gain
            dx = r * (gdy - xh * (_xdot(gdy * xh, ones) * (1.0 / HEAD_DIM)))
            return dx, jnp.sum(dn * xh, axis=0, keepdims=True)

        dgq = jnp.zeros((1, 128), F32)
        parts = []
        for c in range(D_A // 128):
            sl = slice(128 * c, 128 * (c + 1))
            dx, dg = norm_rope_bwd(dqr_ref[:, sl] * Q_SCALE, qpre_ref[:, sl], gq_ref[...])
            parts.append(dx)
            dgq = dgq + dg
        dxk, dgk = norm_rope_bwd(dkr, kpre_ref[...], gk_ref[...])
        _acc_out(dgq_ref, dgq)
        _acc_out(dgk_ref, dgk)
        parts += [dxk, dva, (dq0[...] + dq1[...] + dq2[...]) * Q_SCALE, dk0[...] + dk1[...] + dk2[...],
                  dv0[...] + dv1[...] + dv2[...]]
        dproj = jnp.concatenate(parts, axis=-1).astype(BF16)
        dproj_ref[...] = dproj
        dxn = _mm_nt(dproj, w_ref[...])
        xh, r = _rms_stats(x_ref[...])
        dx, dg = _rms_bwd(dxn, xh, r, g_ref[...])
        _acc_out(dg1_ref, dg)
        gx_ref[...] = dh1_ref[...] + dx

    sds = jax.ShapeDtypeStruct
    return pl.pallas_call(
        body, name="in_proj_bwd", grid=(s_len // tm,),
        in_specs=[_rows(tm, D_A), pl.BlockSpec((N_KV_A, tm, 128), lambda i: (0, i, 0))] + [_rows(tm, D_B)] * 9
                 + [_rows(tm, D_A), _rows(tm, D_KV_A), _rows(tm, D_MODEL), _rows(tm, D_MODEL),
                    _full((1, D_MODEL)), _full((D_MODEL, D_IN)), _rows(tm, 128), _rows(tm, 128), _full((1, 128)),
                    _full((1, 128)), _full((128, 128))],
        out_specs=[_rows(tm, D_IN), _rows(tm, D_MODEL), _full((1, D_MODEL)), _full((1, 128)), _full((1, 128))],
        out_shape=[sds((s_len, D_IN), BF16), sds((s_len, D_MODEL), F32), sds((1, D_MODEL), F32),
                   sds((1, 128), F32), sds((1, 128), F32)],
        compiler_params=_cparams(("arbitrary",)),
    )(dqr, dkv, *dqb, *dkb, *dvb, qpre, kpre, x, dh1, g1, w_in, cc, ss, gq2, gk2, ones128)


def _dw(a, b, name, relu2=False):
    s_len, ka = a.shape
    n = b.shape[1]
    ts = min(DW_TS, s_len)
    bk = min(ka, 1024)
    bn = n if n % 1024 else 1024

    def body(a_ref, b_ref, o_ref):
        @pl.when(pl.program_id(2) == 0)
        def _():
            o_ref[...] = jnp.zeros_like(o_ref)

        av = a_ref[...]
        if relu2:
            av = jnp.square(jnp.maximum(av, 0.0))
        o_ref[...] += _mm_tn(av.astype(BF16), b_ref[...])

    return pl.pallas_call(
        body, name=name, grid=(ka // bk, n // bn, s_len // ts),
        in_specs=[pl.BlockSpec((ts, bk), lambda i, j, k: (k, i)), pl.BlockSpec((ts, bn), lambda i, j, k: (k, j))],
        out_specs=pl.BlockSpec((bk, bn), lambda i, j, k: (i, j)),
        out_shape=jax.ShapeDtypeStruct((ka, n), F32),
        compiler_params=_cparams(("parallel", "parallel", "arbitrary")),
    )(a, b)


def _stack_heads(block, low, data_low):
    parts = []
    for c in range(GROUP_A // 2):
        chunk = block[:, 128 * c:128 * (c + 1)]
        swapped = pltpu.roll(chunk, HEAD_DIM, 1)
        for h_low in (chunk, swapped) if data_low else (swapped, chunk):
            parts.append(jnp.where(low, h_low, 0.0) if data_low else jnp.where(low, 0.0, h_low))
    return jnp.concatenate(parts, axis=0).astype(BF16)


def _unstack_heads(stacked, low, tq, data_low):
    chunks = []
    for c in range(GROUP_A // 2):
        even = stacked[2 * c * tq:(2 * c + 1) * tq]
        odd = stacked[(2 * c + 1) * tq:(2 * c + 2) * tq]
        if data_low:
            chunks.append(jnp.where(low, even, pltpu.roll(odd, HEAD_DIM, 1)))
        else:
            chunks.append(jnp.where(low, pltpu.roll(even, HEAD_DIM, 1), odd))
    return chunks


def _attn_a_fwd(qa, kv):
    s_len = kv.shape[1]
    tq = min(ATT_TQ, s_len)
    tk = min(ATT_TK_FWD, s_len)
    rows = GROUP_A * tq

    def body(q_ref, kv_ref, o_ref, lse_ref):
        low = _low_lanes(tq)
        low_k = _low_lanes(tk)
        q = _stack_heads(q_ref[...].astype(F32), low, data_low=True)

        def step(j, carry):
            m, acc = carry
            kvj = kv_ref[0, pl.ds(pl.multiple_of(j * tk, tk), tk), :]
            s = _mm_nt(q, kvj)
            m_new = jnp.maximum(m, jnp.max(s, axis=1, keepdims=True))
            p = jnp.exp(s - m_new).astype(BF16)
            acc = jnp.exp(m - m_new) * acc + _mm(p, jnp.where(low_k, jnp.ones_like(kvj), kvj))
            return m_new, acc

        init = (jnp.full((rows, 1), -jnp.inf, F32), jnp.zeros((rows, 128), F32))
        m, acc = lax.fori_loop(0, s_len // tk, step, init)
        for c, chunk in enumerate(_unstack_heads(acc / pltpu.roll(acc, HEAD_DIM, 1), low, tq, data_low=False)):
            o_ref[:, 128 * c:128 * (c + 1)] = chunk
        lse_ref[...] = (m + jnp.log(acc[:, :1])).reshape(GROUP_A, tq, 1)

    return pl.pallas_call(
        body, name="attn_a_fwd", grid=(N_KV_A, s_len // tq),
        in_specs=[pl.BlockSpec((tq, 256), lambda g, i: (i, g)),
                  pl.BlockSpec((1, s_len, 128), lambda g, i: (g, 0, 0))],
        out_specs=[pl.BlockSpec((tq, 256), lambda g, i: (i, g)),
                   pl.BlockSpec((GROUP_A, tq, 1), lambda g, i: (g, i, 0))],
        out_shape=[jax.ShapeDtypeStruct((s_len, D_A), F32),
                   jax.ShapeDtypeStruct((N_HEADS_A, s_len, 1), F32)],
        compiler_params=_cparams(("parallel", "parallel")),
    )(qa, kv)


def _attn_a_bwd(qa, dya, kv, lse, delta):
    s_len = kv.shape[1]
    tq = min(ATT_TQ, s_len)
    tk = min(ATT_TK_BWD, s_len)
    rows = GROUP_A * tq

    def body(q_ref, do_ref, kv_ref, lse_ref, dl_ref, dq_ref, dkv_ref):
        @pl.when(pl.program_id(1) == 0)
        def _():
            dkv_ref[...] = jnp.zeros_like(dkv_ref)

        low = _low_lanes(tq)
        q = _stack_heads(q_ref[...].astype(F32), low, data_low=True)
        do = _stack_heads(do_ref[...], low, data_low=False)
        lse_t = lse_ref[...].reshape(rows, 1)
        dl_t = dl_ref[...].reshape(rows, 1)

        def step(j, dq):
            span = pl.ds(pl.multiple_of(j * tk, tk), tk)
            kvj = kv_ref[0, span, :]
            p = jnp.exp(_mm_nt(q, kvj) - lse_t)
            ds = (p * (_mm_nt(do, kvj) - dl_t)).astype(BF16)
            dkv_ref[0, span, :] += _mm_tn(ds, q) + _mm_tn(p.astype(BF16), do)
            return dq + _mm(ds, kvj)

        dq = lax.fori_loop(0, s_len // tk, step, jnp.zeros((rows, 128), F32))
        for c, chunk in enumerate(_unstack_heads(dq, low, tq, data_low=True)):
            dq_ref[:, 128 * c:128 * (c + 1)] = chunk

    return pl.pallas_call(
        body, name="attn_a_bwd", grid=(N_KV_A, s_len // tq),
        in_specs=[pl.BlockSpec((tq, 256), lambda g, i: (i, g)),
                  pl.BlockSpec((tq, 256), lambda g, i: (i, g)),
                  pl.BlockSpec((1, s_len, 128), lambda g, i: (g, 0, 0)),
                  pl.BlockSpec((GROUP_A, tq, 1), lambda g, i: (g, i, 0)),
                  pl.BlockSpec((GROUP_A, tq, 1), lambda g, i: (g, i, 0))],
        out_specs=[pl.BlockSpec((tq, 256), lambda g, i: (i, g)),
                   pl.BlockSpec((1, s_len, 128), lambda g, i: (g, 0, 0))],
        out_shape=[jax.ShapeDtypeStruct((s_len, D_A), F32),
                   jax.ShapeDtypeStruct((N_KV_A, s_len, 128), F32)],
        compiler_params=_cparams(("parallel", "arbitrary")),
    )(qa, dya, kv, lse, delta)


class _SwaGeometry:
    def __init__(self, s_len, r):
        self.r = r
        self.tq = SWA_TQ
        self.block = min(max(SWA_MIN_BLOCK, SWA_TQ * r), s_len)
        self.halo = HALF_WIN * r
        self.nsub = self.block // (self.tq * r)
        self.band = self.tq + 2 * HALF_WIN
        self.length = s_len // r
        self.nblk = s_len // self.block
        self.nhalo = s_len // self.halo
        assert self.nsub * self.tq * r == self.block and self.block % self.halo == 0

    def specs(self):
        per = self.block // self.halo
        cur = pl.BlockSpec((self.block, 128), lambda c, i: (i, c))
        prev = pl.BlockSpec((self.halo, 128), lambda c, i: (jnp.maximum(i * per - 1, 0), c))
        nxt = pl.BlockSpec((self.halo, 128), lambda c, i: (jnp.minimum((i + 1) * per, self.nhalo - 1), c))
        return prev, cur, nxt

    def tiles(self):
        return [(rho + self.r * j * self.tq, self.halo + rho + self.r * (j * self.tq - HALF_WIN), j)
                for j in range(self.nsub) for rho in range(self.r)]

    def own(self, start):
        return pl.ds(start, self.tq, stride=self.r)

    def around(self, start):
        return pl.ds(start, self.band, stride=self.r)

    def fill(self, dst, prev_ref, cur_ref, next_ref):
        dst[:self.halo, :] = prev_ref[...].astype(F32)
        dst[self.halo:self.halo + self.block, :] = cur_ref[...].astype(F32)
        dst[self.halo + self.block:, :] = next_ref[...].astype(F32)

    def first_position(self, j):
        return (pl.program_id(1) * self.block) // self.r + j * self.tq

    def extended(self):
        return pltpu.VMEM((self.block + 2 * self.halo, 128), F32)

    def plain(self):
        return pltpu.VMEM((self.block, 128), F32)


def _low_lanes(rows):
    return lax.broadcasted_iota(jnp.int32, (rows, 128), 1) < HEAD_DIM


def _one_head(x, low, half):
    return jnp.where(low if half == 0 else jnp.logical_not(low), x, 0.0).astype(BF16)


def _swa_fwd(q, k, v, bias, r):
    geo = _SwaGeometry(q.shape[0], r)
    prev, cur, nxt = geo.specs()

    def body(q_ref, kp, kc, kn, vp, vc, vn, b_ref, o_ref, lse_ref, qf, kf, vf):
        qf[...] = q_ref[...].astype(F32)
        geo.fill(kf, kp, kc, kn)
        geo.fill(vf, vp, vc, vn)
        low_q, low_b = _low_lanes(geo.tq), _low_lanes(geo.band)
        row = lax.broadcasted_iota(jnp.int32, (geo.tq, geo.band), 0)
        col = lax.broadcasted_iota(jnp.int32, (geo.tq, geo.band), 1)
        in_window = jnp.abs(col - HALF_WIN - row) <= HALF_WIN
        for own, around, j in geo.tiles():
            key = geo.first_position(j) - HALF_WIN + col
            valid = in_window & (key >= 0) & (key < geo.length)
            qs = qf[geo.own(own), :]
            kb = kf[geo.around(around), :].astype(BF16)
            vb = vf[geo.around(around), :]
            outs, lses = [], []
            for half in range(2):
                s = jnp.where(valid, _mm_nt(_one_head(qs, low_q, half), kb) + b_ref[half], NEG_BIG)
                m = jnp.max(s, axis=1, keepdims=True)
                e = jnp.exp(s - m)
                l = jnp.sum(e, axis=1, keepdims=True)
                outs.append(_mm(e.astype(BF16), _one_head(vb, low_b, half)) / l)
                lses.append(m + jnp.log(l))
            o_ref[geo.own(own), :] = outs[0] + outs[1]
            lse_ref[geo.own(own), :] = jnp.where(low_q, lses[0], lses[1])

    sds = jax.ShapeDtypeStruct
    return pl.pallas_call(
        body, name="swa_fwd_%d" % r, grid=(D_B // 128, geo.nblk),
        in_specs=[cur, prev, cur, nxt, prev, cur, nxt, pl.BlockSpec((2, geo.tq, geo.band), lambda c, i: (c, 0, 0))],
        out_specs=[cur, cur],
        out_shape=[sds(q.shape, F32), sds(q.shape, F32)],
        scratch_shapes=[geo.plain(), geo.extended(), geo.extended()],
        compiler_params=_cparams(("parallel", "parallel")),
    )(q, k, k, k, v, v, v, bias)


def _head_stats(st, half):
    lo = HEAD_DIM * half
    return st[:, lo:lo + 1], st[:, lo + HEAD_DIM // 2:lo + HEAD_DIM // 2 + 1]


def _swa_bwd_q(q, k, v, dy, st, bias, r):
    geo = _SwaGeometry(q.shape[0], r)
    prev, cur, nxt = geo.specs()
    bias_spec = pl.BlockSpec((2, geo.tq, geo.band), lambda c, i: (c, 0, 0))

    def body(q_ref, kp, kc, kn, vp, vc, vn, dy_ref, st_ref, b_ref, dq_ref, db_ref, qf, kf, vf, dyf):
        @pl.when(pl.program_id(1) == 0)
        def _():
            db_ref[...] = jnp.zeros_like(db_ref)

        qf[...] = q_ref[...].astype(F32)
        dyf[...] = dy_ref[...].astype(F32)
        geo.fill(kf, kp, kc, kn)
        geo.fill(vf, vp, vc, vn)
        low_q, low_b = _low_lanes(geo.tq), _low_lanes(geo.band)
        row = lax.broadcasted_iota(jnp.int32, (geo.tq, geo.band), 0)
        col = lax.broadcasted_iota(jnp.int32, (geo.tq, geo.band), 1)
        in_window = jnp.abs(col - HALF_WIN - row) <= HALF_WIN
        for own, around, j in geo.tiles():
            key = geo.first_position(j) - HALF_WIN + col
            valid = in_window & (key >= 0) & (key < geo.length)
            qs = qf[geo.own(own), :]
            dys = dyf[geo.own(own), :]
            sts = st_ref[geo.own(own), :]
            kb = kf[geo.around(around), :]
            vb = vf[geo.around(around), :].astype(BF16)
            dq = jnp.zeros((geo.tq, 128), F32)
            for half in range(2):
                lse, delta = _head_stats(sts, half)
                s = jnp.where(valid, _mm_nt(_one_head(qs, low_q, half), kb.astype(BF16)) + b_ref[half], NEG_BIG)
                p = jnp.exp(s - lse)
                ds = p * (_mm_nt(_one_head(dys, low_q, half), vb) - delta)
                db_ref[half] += ds
                dq = dq + _mm(ds.astype(BF16), _one_head(kb, low_b, half))
            dq_ref[geo.own(own), :] = dq

    return pl.pallas_call(
        body, name="swa_bwd_q_%d" % r, grid=(D_B // 128, geo.nblk),
        in_specs=[cur, prev, cur, nxt, prev, cur, nxt, cur, cur, bias_spec],
        out_specs=[cur, bias_spec],
        out_shape=[jax.ShapeDtypeStruct(q.shape, F32), jax.ShapeDtypeStruct(bias.shape, F32)],
        scratch_shapes=[geo.plain(), geo.extended(), geo.extended(), geo.plain()],
        compiler_params=_cparams(("parallel", "arbitrary")),
    )(q, k, k, k, v, v, v, dy, st, bias)


def _swa_bwd_kv(q, k, v, dy, st, bias_kv, r):
    geo = _SwaGeometry(q.shape[0], r)
    prev, cur, nxt = geo.specs()

    def body(k_ref, v_ref, qp, qc, qn, dp_, dc_, dn_, sp, sc, sn, b_ref, dk_ref, dv_ref, kf, vf, qf, dyf, stf):
        kf[...] = k_ref[...].astype(F32)
        vf[...] = v_ref[...].astype(F32)
        geo.fill(qf, qp, qc, qn)
        geo.fill(dyf, dp_, dc_, dn_)
        geo.fill(stf, sp, sc, sn)
        low_k, low_b = _low_lanes(geo.tq), _low_lanes(geo.band)
        row = lax.broadcasted_iota(jnp.int32, (geo.band, geo.tq), 0)
        col = lax.broadcasted_iota(jnp.int32, (geo.band, geo.tq), 1)
        in_window = jnp.abs(col + HALF_WIN - row) <= HALF_WIN
        for own, around, j in geo.tiles():
            query = geo.first_position(j) - HALF_WIN + row
            valid = in_window & (query >= 0) & (query < geo.length)
            ks = kf[geo.own(own), :].astype(BF16)
            vs = vf[geo.own(own), :].astype(BF16)
            qb = qf[geo.around(around), :]
            dyb = dyf[geo.around(around), :]
            stb = stf[geo.around(around), :]
            dk = jnp.zeros((geo.tq, 128), F32)
            dv = jnp.zeros((geo.tq, 128), F32)
            for half in range(2):
                lse, delta = _head_stats(stb, half)
                q_h, dy_h = _one_head(qb, low_b, half), _one_head(dyb, low_b, half)
                s = jnp.where(valid, _mm_nt(q_h, ks) + b_ref[half], NEG_BIG)
                p = jnp.exp(s - lse)
                ds = p * (_mm_nt(dy_h, vs) - delta)
                dv = dv + _mm_tn(p.astype(BF16), dy_h)
                dk = dk + _mm_tn(ds.astype(BF16), q_h)
            dk_ref[geo.own(own), :] = dk
            dv_ref[geo.own(own), :] = dv

    return pl.pallas_call(
        body, name="swa_bwd_kv_%d" % r, grid=(D_B // 128, geo.nblk),
        in_specs=[cur, cur, prev, cur, nxt, prev, cur, nxt, prev, cur, nxt,
                  pl.BlockSpec((2, geo.band, geo.tq), lambda c, i: (c, 0, 0))],
        out_specs=[cur, cur],
        out_shape=[jax.ShapeDtypeStruct(q.shape, F32), jax.ShapeDtypeStruct(q.shape, F32)],
        scratch_shapes=[geo.plain(), geo.plain(), geo.extended(), geo.extended(), geo.extended()],
        compiler_params=_cparams(("parallel", "parallel")),
    )(k, v, q, q, q, dy, dy, dy, st, st, st, bias_kv)


BIAS_ROWS = 16
BIAS_TN = 4096


def _bias_tiles(onehot, rel_bias_t):
    n = onehot.shape[0]

    def body(oh_ref, rb_ref, o_ref):
        o_ref[...] = sum(_mm_nt(piece, oh_ref[...]) for piece in _split3(rb_ref[...]))

    return pl.pallas_call(
        body, name="bias_tiles", grid=(n // BIAS_TN,),
        in_specs=[_rows(BIAS_TN, 128), _full((BIAS_ROWS, 128))],
        out_specs=pl.BlockSpec((BIAS_ROWS, BIAS_TN), lambda i: (0, i)),
        out_shape=jax.ShapeDtypeStruct((BIAS_ROWS, n), F32),
        compiler_params=_cparams(("parallel",)),
    )(onehot, rel_bias_t)


def _bias_bwd(onehot, dbias_rows, so_far, r):
    n = onehot.shape[0]

    def body(oh, d, prev_ref, g_ref):
        @pl.when(pl.program_id(0) == 0)
        def _():
            g_ref[...] = prev_ref[...]

        hi, lo, _ = _split3(d[...])
        g_ref[...] += _mm(hi, oh[...]) + _mm(lo, oh[...])

    return pl.pallas_call(
        body, name="bias_bwd_%d" % r, grid=(n // BIAS_TN,),
        in_specs=[_rows(BIAS_TN, 128), pl.BlockSpec((BIAS_ROWS, BIAS_TN), lambda i: (0, i)), _full((BIAS_ROWS, 128))],
        out_specs=_full((BIAS_ROWS, 128)),
        out_shape=jax.ShapeDtypeStruct((BIAS_ROWS, 128), F32),
        compiler_params=_cparams(("arbitrary",)),
    )(onehot, dbias_rows, so_far)


def _local_step(x, p, tgt, w_in, w_out, w_ff1, w_ff2, w_gate, w_ple, g_attn_pre, g_q, g_k, g_out_a, g_out_b,
                g_attn_post, rel_bias, g_mlp_pre, g_mlp_post, g_ple):
    s_len = x.shape[0]
    cc, ss = _rope_tables(s_len)
    gq2 = jnp.concatenate([g_q, g_q], axis=-1)
    gk2 = jnp.concatenate([g_k, g_k], axis=-1)
    ones128 = _group_ones(128)
    rel_bias_t = jnp.zeros((BIAS_ROWS, 128), F32).at[:N_HEADS_B, :N_BUCKETS].set(rel_bias.T)

    xn1, qpre, kpre, qa, kv, qb, kb, vb = _in_proj(x, g_attn_pre, w_in, cc, ss, gq2, gk2, ones128)
    ya, lse_a = _attn_a_fwd(qa, kv)

    tiles, outs, lses = [], [], []
    for r in DILATIONS:
        tq = SWA_TQ
        onehot = _bucket_onehot(tq, r)
        bias = _bias_tiles(onehot, rel_bias_t)[:N_HEADS_B].reshape(N_HEADS_B, tq, tq + 2 * HALF_WIN)
        o_r, lse_r = _swa_fwd(qb, kb, vb, bias, r)
        tiles.append((onehot, bias))
        outs.append(o_r)
        lses.append(lse_r)
    yb, lse_b = _merge_b(outs, lses)

    ycat, y2, h1, xn2 = _out_proj(ya, yb, x, g_out_a, g_out_b, w_out, g_attn_post, g_mlp_pre)
    u = _ff1(xn2, w_ff1)
    f2, h2, xn3 = _ff2(u, w_ff2, h1, g_mlp_post, g_ple)
    dh2, df2, dgl, dpp, loss, dg_ple, dg_mlp_post = _ple_loss(xn3, p, h2, f2, tgt, w_gate, w_ple, g_ple, g_mlp_post)

    grads = {"g_ple": dg_ple, "g_mlp_post": dg_mlp_post}
    grads["w_ple_gate"] = _dw(xn3, dgl, "dw_gate")
    grads["w_ple_proj"] = _dw(p, dpp, "dw_ple")
    grads["w_ff2"] = _dw(u, df2, "dw_ff2", relu2=True)
    du = _ff2_bwd(df2, w_ff2, u)
    grads["w_ff1"] = _dw(xn2, du, "dw_ff1")
    dh1, dy2, grads["g_mlp_pre"], grads["g_attn_post"] = _ff1_bwd(du, w_ff1, dh2, h1, y2, g_mlp_pre, g_attn_post)
    grads["w_out"] = _dw(ycat, dy2, "dw_out")
    dya, dyb, delta_a, st_b, grads["g_out_a"], grads["g_out_b"] = _out_proj_bwd(dy2, w_out, ya, yb, lse_b, g_out_a,
                                                                              g_out_b)

    dqr, dkv_a = _attn_a_bwd(qa, dya, kv, lse_a, delta_a)

    dqs, dks, dvs = [], [], []
    d_rel = jnp.zeros((BIAS_ROWS, 128), F32)
    for r, (onehot, bias) in zip(DILATIONS, tiles):
        dq_r, dbias = _swa_bwd_q(qb, kb, vb, dyb, st_b, bias, r)
        bias_kv = jnp.flip(bias, axis=(1, 2)).transpose(0, 2, 1)
        dk_r, dv_r = _swa_bwd_kv(qb, kb, vb, dyb, st_b, bias_kv, r)
        dbias_rows = jnp.pad(dbias.reshape(N_HEADS_B, -1), ((0, BIAS_ROWS - N_HEADS_B), (0, 0)))
        d_rel = _bias_bwd(onehot, dbias_rows, d_rel, r)
        dqs.append(dq_r)
        dks.append(dk_r)
        dvs.append(dv_r)
    grads["rel_bias"] = d_rel[:N_HEADS_B, :N_BUCKETS].T

    dproj, grad_x, grads["g_attn_pre"], dgq2, dgk2 = _in_proj_bwd(
        dqr, dkv_a, dqs, dks, dvs, qpre, kpre, x, dh1, g_attn_pre, w_in, cc, ss, gq2, gk2, ones128)
    grads["g_q"] = dgq2[:, :HEAD_DIM] + dgq2[:, HEAD_DIM:]
    grads["g_k"] = dgk2[:, :HEAD_DIM] + dgk2[:, HEAD_DIM:]
    grads["w_in"] = _dw(xn1, dproj, "dw_in")
    return loss, grad_x, grads


ANY = pl.BlockSpec(memory_space=pl.ANY)


def _position():
    return lax.axis_index("x"), lax.axis_index("y"), lax.axis_index("c")


def _other_chips(x, y):
    return [(2 * (1 - x) + y, (1 - x, y)), (2 * x + (1 - y), (x, 1 - y)), (2 * (1 - x) + (1 - y), (1 - x, 1 - y))]


def _cast_shards(shards):
    def body(*refs):
        n = len(refs) // 2
        for i_ref, o_ref in zip(refs[:n], refs[n:]):
            o_ref[...] = i_ref[...].astype(BF16)

    return pl.pallas_call(
        body, name="cast_shards",
        in_specs=[pl.BlockSpec(memory_space=pltpu.VMEM)] * len(shards),
        out_specs=[pl.BlockSpec(memory_space=pltpu.VMEM)] * len(shards),
        out_shape=[jax.ShapeDtypeStruct(s.shape, BF16) for s in shards],
        compiler_params=_cparams(),
    )(*shards)


def _gather_weights(shards):
    n = len(shards)

    def body(*refs):
        ins, outs = refs[:n], refs[n:2 * n]
        send_sems, recv_sems, pass_send_sems, pass_recv_sems = refs[2 * n:]
        x, y, c = _position()
        me = 2 * x + y
        sibling = (x, y, 1 - c)

        def rows(a, core):
            half = ins[a].shape[0] // 2
            return pl.ds(pl.multiple_of(core * half, 16), half)

        sends = []
        for k, (_, chip) in enumerate(_other_chips(x, y)):
            for a in range(n):
                cp = pltpu.make_async_remote_copy(ins[a].at[rows(a, c), :], outs[a].at[me, rows(a, c), :],
                                                  send_sems.at[k, a], recv_sems.at[k, a], device_id=(*chip, c),
                                                  device_id_type=MESH)
                cp.start()
                sends.append(cp)
        for k, (num, chip) in enumerate(_other_chips(x, y)):
            for a in range(n):
                landed = outs[a].at[num, rows(a, c), :]
                pltpu.make_async_remote_copy(landed, landed, send_sems.at[k, a], recv_sems.at[k, a],
                                             device_id=(*chip, c), device_id_type=MESH).wait_recv()
                cp = pltpu.make_async_remote_copy(landed, landed, pass_send_sems.at[k, a], pass_recv_sems.at[k, a],
                                                  device_id=sibling, device_id_type=MESH)
                cp.start()
                sends.append(cp)
        for k, (num, _) in enumerate(_other_chips(x, y)):
            for a in range(n):
                passed = outs[a].at[num, rows(a, 1 - c), :]
                pltpu.make_async_remote_copy(passed, passed, pass_send_sems.at[k, a], pass_recv_sems.at[k, a],
                                             device_id=sibling, device_id_type=MESH).wait_recv()
        for cp in sends:
            cp.wait_send()

    return pl.pallas_call(
        body, name="gather_weights",
        in_specs=[ANY] * n, out_specs=[ANY] * n,
        out_shape=[jax.ShapeDtypeStruct((N_CHIPS,) + s.shape, s.dtype) for s in shards],
        scratch_shapes=[pltpu.SemaphoreType.DMA((3, n))] * 4,
    )(*shards)


def _send_sibling_half(grads):
    n = len(grads)

    def body(*refs):
        ins, outs = refs[:n], refs[n:2 * n]
        send_sems, recv_sems = refs[2 * n:]
        x, y, c = _position()
        copies = []
        for a in range(n):
            half = ins[a].shape[1] // 2
            theirs = ins[a].at[:, pl.ds(pl.multiple_of((1 - c) * half, 8), half), :]
            cp = pltpu.make_async_remote_copy(theirs, outs[a], send_sems.at[a], recv_sems.at[a],
                                              device_id=(x, y, 1 - c), device_id_type=MESH)
            cp.start()
            copies.append(cp)
        for cp in copies:
            cp.wait()

    return pl.pallas_call(
        body, name="send_sibling_half",
        in_specs=[ANY] * n, out_specs=[ANY] * n,
        out_shape=[jax.ShapeDtypeStruct((g.shape[0], g.shape[1] // 2, g.shape[2]), g.dtype) for g in grads],
        scratch_shapes=[pltpu.SemaphoreType.DMA((n,)), pltpu.SemaphoreType.DMA((n,))],
    )(*grads)


def _scatter_to_chips(pairs):
    n = len(pairs)

    def body(*refs):
        ins, outs = refs[:n], refs[n:2 * n]
        send_sems, recv_sems = refs[2 * n:]
        x, y, c = _position()
        me = 2 * x + y
        sends = []
        for k, (num, chip) in enumerate(_other_chips(x, y)):
            for a in range(n):
                cp = pltpu.make_async_remote_copy(ins[a].at[num], outs[a].at[me], send_sems.at[k, a],
                                                  recv_sems.at[k, a], device_id=(*chip, c), device_id_type=MESH)
                cp.start()
                sends.append(cp)
        for k, (num, chip) in enumerate(_other_chips(x, y)):
            for a in range(n):
                pltpu.make_async_remote_copy(ins[a].at[me], outs[a].at[num], send_sems.at[k, a], recv_sems.at[k, a],
                                             device_id=(*chip, c), device_id_type=MESH).wait_recv()
        for cp in sends:
            cp.wait_send()

    return pl.pallas_call(
        body, name="scatter_to_chips",
        in_specs=[ANY] * n, out_specs=[ANY] * n,
        out_shape=[jax.ShapeDtypeStruct(g.shape, g.dtype) for g in pairs],
        scratch_shapes=[pltpu.SemaphoreType.DMA((3, n)), pltpu.SemaphoreType.DMA((3, n))],
    )(*pairs)


def _exchange_halves(halves):
    n = len(halves)

    def body(*refs):
        ins, outs = refs[:n], refs[n:2 * n]
        send_sems, recv_sems = refs[2 * n:]
        x, y, c = _position()
        copies = []
        for a in range(n):
            cp = pltpu.make_async_remote_copy(ins[a], outs[a], send_sems.at[a], recv_sems.at[a],
                                              device_id=(x, y, 1 - c), device_id_type=MESH)
            cp.start()
            copies.append(cp)
        for cp in copies:
            cp.wait()

    return pl.pallas_call(
        body, name="exchange_halves",
        in_specs=[ANY] * n, out_specs=[ANY] * n,
        out_shape=[jax.ShapeDtypeStruct(h.shape, h.dtype) for h in halves],
        scratch_shapes=[pltpu.SemaphoreType.DMA((n,)), pltpu.SemaphoreType.DMA((n,))],
    )(*halves)


def _allreduce_small(v):
    def body(v_ref, o_ref, buf, send_sems, recv_sems):
        x, y, c = _position()
        me = 4 * x + 2 * y + c
        peers = [(1 - x, y, c), (x, 1 - y, c), (x, y, 1 - c), (1 - x, 1 - y, c), (1 - x, y, 1 - c), (x, 1 - y, 1 - c),
                 (1 - x, 1 - y, 1 - c)]
        num = lambda d: 4 * d[0] + 2 * d[1] + d[2]
        buf[me] = v_ref[...]
        sends = []
        for k, peer in enumerate(peers):
            cp = pltpu.make_async_remote_copy(v_ref, buf.at[me], send_sems.at[k], recv_sems.at[k], device_id=peer,
                                              device_id_type=MESH)
            cp.start()
            sends.append(cp)
        for k, peer in enumerate(peers):
            pltpu.make_async_remote_copy(v_ref, buf.at[num(peer)], send_sems.at[k], recv_sems.at[k], device_id=peer,
                                         device_id_type=MESH).wait_recv()
        for cp in sends:
            cp.wait_send()
        total = buf[0]
        for d in range(1, 8):
            total = total + buf[d]
        o_ref[...] = total

    return pl.pallas_call(
        body, name="allreduce_small",
        in_specs=[pl.BlockSpec(memory_space=pltpu.VMEM)], out_specs=pl.BlockSpec(memory_space=pltpu.VMEM),
        out_shape=jax.ShapeDtypeStruct(v.shape, v.dtype),
        scratch_shapes=[pltpu.VMEM((8,) + v.shape, v.dtype), pltpu.SemaphoreType.DMA((7,)),
                        pltpu.SemaphoreType.DMA((7,))],
    )(v)


def _sum_leading(a, name):
    k, r, c = a.shape
    tr = min(r, 256)

    def body(a_ref, o_ref):
        total = a_ref[0].astype(F32)
        for i in range(1, k):
            total = total + a_ref[i].astype(F32)
        o_ref[...] = total

    return pl.pallas_call(
        body, name=name, grid=(r // tr,),
        in_specs=[pl.BlockSpec((k, tr, c), lambda i: (0, i, 0))],
        out_specs=pl.BlockSpec((tr, c), lambda i: (i, 0)),
        out_shape=jax.ShapeDtypeStruct((r, c), F32),
        compiler_params=_cparams(("parallel",)),
    )(a)


def _add(a, b, name):
    k, r, c = a.shape
    tr = min(r, 256)
    spec = pl.BlockSpec((k, tr, c), lambda i: (0, i, 0))

    def body(a_ref, b_ref, o_ref):
        o_ref[...] = (a_ref[...] + b_ref[...]).astype(BF16)

    return pl.pallas_call(
        body, name=name, grid=(r // tr,), in_specs=[spec, spec], out_specs=spec,
        out_shape=jax.ShapeDtypeStruct(a.shape, BF16), compiler_params=_cparams(("parallel",)),
    )(a, b)


def _adamw(w, g, m, v, name):
    r, c = w.shape
    tr = min(r, 256)
    spec = pl.BlockSpec((tr, c), lambda i: (i, 0))

    def body(w_ref, g_ref, m_ref, v_ref, d_ref, nm_ref, nv_ref):
        gv = g_ref[...]
        nm = ADAM_B1 * m_ref[...] + (1.0 - ADAM_B1) * gv
        nv = ADAM_B2 * v_ref[...] + (1.0 - ADAM_B2) * jnp.square(gv)
        m_hat = nm / (1.0 - ADAM_B1 ** ADAM_STEP)
        v_hat = nv / (1.0 - ADAM_B2 ** ADAM_STEP)
        d_ref[...] = -ADAM_LR * (m_hat / (jnp.sqrt(v_hat) + ADAM_EPS) + ADAM_WD * w_ref[...])
        nm_ref[...] = nm
        nv_ref[...] = nv

    return pl.pallas_call(
        body, name=name, grid=(r // tr,), in_specs=[spec] * 4, out_specs=[spec] * 3,
        out_shape=[jax.ShapeDtypeStruct(w.shape, F32)] * 3, compiler_params=_cparams(("parallel",)),
    )(w, g, m, v)


MATRICES = ("w_in", "w_out", "w_ff1", "w_ff2", "w_ple_gate", "w_ple_proj")
COLUMN_SHARDED = ("w_in", "w_ff1", "w_ple_proj")
SMALL = ("g_attn_pre", "g_q", "g_k", "g_out_a", "g_out_b", "g_attn_post", "rel_bias", "g_mlp_pre", "g_mlp_post",
         "g_ple")
WEIGHT_ORDER = ("w_in", "g_attn_pre", "g_q", "g_k", "g_out_a", "g_out_b", "w_out", "g_attn_post", "rel_bias",
                "g_mlp_pre", "w_ff1", "w_ff2", "g_mlp_post", "g_ple", "w_ple_gate", "w_ple_proj")
PACK_ROWS, PACK_COLS = 8, 1024


def _pack_small(values, extra=None):
    flat = [values[n].reshape(-1) for n in SMALL]
    used = sum(f.shape[0] for f in flat)
    tail = jnp.zeros((PACK_ROWS * PACK_COLS - used - 1,), F32)
    last = jnp.zeros((1,), F32) if extra is None else extra.reshape(1)
    return jnp.concatenate(flat + [tail, last]).reshape(PACK_ROWS, PACK_COLS)


def _unpack_small(packed, like):
    flat = packed.reshape(-1)
    out, o = {}, 0
    for n in SMALL:
        size = like[n].size
        out[n] = flat[o:o + size].reshape(like[n].shape)
        o += size
    return out, flat[-1]


def kernel(x, p, w_in, g_attn_pre, g_q, g_k, g_out_a, g_out_b, w_out, g_attn_post, rel_bias, g_mlp_pre, w_ff1, w_ff2, g_mlp_post, g_ple, w_ple_gate, w_ple_proj, loss_target, m_w_in, m_g_attn_pre, m_g_q, m_g_k, m_g_out_a, m_g_out_b, m_w_out, m_g_attn_post, m_rel_bias, m_g_mlp_pre, m_w_ff1, m_w_ff2, m_g_mlp_post, m_g_ple, m_w_ple_gate, m_w_ple_proj, v_w_in, v_g_attn_pre, v_g_q, v_g_k, v_g_out_a, v_g_out_b, v_w_out, v_g_attn_post, v_rel_bias, v_g_mlp_pre, v_w_ff1, v_w_ff2, v_g_mlp_post, v_g_ple, v_w_ple_gate, v_w_ple_proj):
    given = dict(locals())
    weights = {n: given[n] for n in WEIGHT_ORDER}
    shards = {n: weights[n][0] for n in MATRICES}

    chip = 2 * lax.axis_index("x") + lax.axis_index("y")
    c = lax.axis_index("c")
    own = _cast_shards([shards[n] for n in MATRICES])
    whole = {}
    for n, g, mine in zip(MATRICES, _gather_weights(own), own):
        g = lax.dynamic_update_slice_in_dim(g, mine[None], chip, axis=0)
        if n in COLUMN_SHARDED:
            whole[n] = g.transpose(1, 0, 2).reshape(g.shape[1], N_CHIPS * g.shape[2])
        else:
            whole[n] = g.reshape(N_CHIPS * g.shape[1], g.shape[2])

    loss, grad_x, grads = _local_step(
        x[0], p[0, 0], loss_target[0], whole["w_in"], whole["w_out"], whole["w_ff1"], whole["w_ff2"],
        whole["w_ple_gate"], whole["w_ple_proj"], g_attn_pre, g_q, g_k, g_out_a, g_out_b, g_attn_post, rel_bias,
        g_mlp_pre, g_mlp_post, g_ple)

    by_chip = []
    for n in MATRICES:
        g = grads[n]
        if n in COLUMN_SHARDED:
            by_chip.append(g.reshape(g.shape[0], N_CHIPS, g.shape[1] // N_CHIPS).transpose(1, 0, 2))
        else:
            by_chip.append(g.reshape(N_CHIPS, g.shape[0] // N_CHIPS, g.shape[1]))
    from_sibling = _send_sibling_half(by_chip)
    pairs = []
    for n, g, other in zip(MATRICES, by_chip, from_sibling):
        half = g.shape[1] // 2
        mine = lax.dynamic_slice_in_dim(g, c * half, half, axis=1)
        pairs.append(_add(mine, other, "pair_sum_" + n))
    halves = []
    for n, pair, got in zip(MATRICES, pairs, _scatter_to_chips(pairs)):
        got = lax.dynamic_update_slice_in_dim(got, lax.dynamic_slice_in_dim(pair, chip, 1, axis=0), chip, axis=0)
        halves.append(_sum_leading(got, "chip_sum_" + n))
    grad_w = {}
    for n, mine, theirs in zip(MATRICES, halves, _exchange_halves(halves)):
        half = mine.shape[0]
        g = jnp.zeros((2 * half, mine.shape[1]), F32)
        g = lax.dynamic_update_slice_in_dim(g, mine, c * half, axis=0)
        grad_w[n] = lax.dynamic_update_slice_in_dim(g, theirs, (1 - c) * half, axis=0)

    small_like = {n: weights[n] for n in SMALL}
    reduced = _allreduce_small(_pack_small({n: grads[n] for n in SMALL}, extra=loss))
    grad_small, loss_total = _unpack_small(reduced, small_like)

    delta, new_m, new_v = {}, {}, {}
    for n in MATRICES:
        d, nm, nv = _adamw(shards[n], grad_w[n], given["m_" + n][0], given["v_" + n][0], "adamw_" + n)
        delta[n], new_m[n], new_v[n] = d[None], nm[None], nv[None]
        grad_w[n] = grad_w[n][None]
    d, nm, nv = _adamw(_pack_small(small_like), reduced, _pack_small({n: given["m_" + n] for n in SMALL}),
                       _pack_small({n: given["v_" + n] for n in SMALL}), "adamw_small")
    d_small, nm_small, nv_small = (_unpack_small(a, small_like)[0] for a in (d, nm, nv))
    for n in SMALL:
        grad_w[n], delta[n], new_m[n], new_v[n] = grad_small[n], d_small[n], nm_small[n], nv_small[n]

    return (loss_total, grad_x[None], *[grad_w[n] for n in WEIGHT_ORDER], *[delta[n] for n in WEIGHT_ORDER],
            *[new_m[n] for n in WEIGHT_ORDER], *[new_v[n] for n in WEIGHT_ORDER])
```

```python
import functools
import math

import jax
import jax.numpy as jnp
from jax import lax
from jax.experimental import pallas as pl
from jax.experimental.pallas import tpu as pltpu

F32 = jnp.float32
BF16 = jnp.bfloat16

D_MODEL = 1024
HEAD_DIM = 64
N_HEADS_A = 8
N_KV_A = 2
GROUP_A = N_HEADS_A // N_KV_A
N_HEADS_B = 8
D_A = N_HEADS_A * HEAD_DIM
D_KV_A = N_KV_A * HEAD_DIM
D_B = N_HEADS_B * HEAD_DIM
D_IN = D_A + 2 * D_KV_A + 3 * D_B
D_FF = 4 * D_MODEL
D_PLE = 256
GRID_W = 64
ROPE_THETA = 10000.0
DILATIONS = (1, 4, 16)
HALF_WIN = 64
N_BUCKETS = 32
MAX_DISTANCE = 1024
EPS = 1e-6
NEG_BIG = -1e30
Q_SCALE = HEAD_DIM ** -0.5

ADAM_LR = 0.001
ADAM_B1 = 0.9
ADAM_B2 = 0.999
ADAM_EPS = 1e-08
ADAM_WD = 0.01
ADAM_STEP = 10

N_CHIPS = 4
MESH = pl.DeviceIdType.MESH

ROW_TILE = 512
ATT_TQ = 256
ATT_TK_FWD = 2048
ATT_SUB_FWD = 512
ATT_TK_BWD = 1024
SWA_TQ = 128
SWA_MIN_BLOCK = 1024
DW_TS = 1024
VMEM_LIMIT = 56 * 1024 * 1024

NT = (((1,), (1,)), ((), ()))
TN = (((0,), (0,)), ((), ()))


def _cparams(sem=None, vmem=VMEM_LIMIT):
    return pltpu.CompilerParams(dimension_semantics=sem, vmem_limit_bytes=vmem)


def _full(shape):
    n = len(shape)
    return pl.BlockSpec(shape, lambda *_: (0,) * n)


def _rows(tm, width):
    return pl.BlockSpec((tm, width), lambda i: (i, 0))


def _split3(a):
    a1 = a.astype(BF16)
    r = a - a1.astype(F32)
    a2 = r.astype(BF16)
    a3 = (r - a2.astype(F32)).astype(BF16)
    return a1, a2, a3


def _xdot(a, sel):
    a1, a2, a3 = _split3(a)
    d = lambda p: jnp.dot(p, sel, preferred_element_type=F32)
    return d(a1) + d(a2) + d(a3)


def _mm(a, b):
    return jnp.dot(a, b, preferred_element_type=F32)


def _mm_nt(a, b):
    return lax.dot_general(a, b, NT, preferred_element_type=F32)


def _mm_tn(a, b):
    return lax.dot_general(a, b, TN, preferred_element_type=F32)


def _rms_stats(x):
    r = lax.rsqrt(jnp.mean(x * x, axis=-1, keepdims=True) + EPS)
    return x * r, r


def _rms_bwd(dy, xh, r, g):
    gdy = dy * g
    dx = r * (gdy - xh * jnp.mean(gdy * xh, axis=-1, keepdims=True))
    dg = jnp.sum(dy * xh, axis=0, keepdims=True)
    return dx, dg


def _acc_out(ref, val):
    @pl.when(pl.program_id(0) == 0)
    def _():
        ref[...] = jnp.zeros_like(ref)

    ref[...] += val


def _swap_halves(x, first_half):
    return jnp.where(first_half, pltpu.roll(x, 96, 1), pltpu.roll(x, 32, 1))


def _first_half_mask(shape):
    return (lax.broadcasted_iota(jnp.int32, shape, 1) % HEAD_DIM) < (HEAD_DIM // 2)


def _rope_tables(s_len):
    t = jnp.arange(s_len)
    row = (t // GRID_W).astype(F32)
    col = (t % GRID_W).astype(F32)
    n_axis = HEAD_DIM // 4
    inv_freq = ROPE_THETA ** (-jnp.arange(n_axis, dtype=F32) / n_axis)
    ang = jnp.concatenate([row[:, None] * inv_freq, col[:, None] * inv_freq], axis=-1)
    c, s = jnp.cos(ang), jnp.sin(ang)
    cc = jnp.concatenate([c, c, c, c], axis=-1)
    ss = jnp.concatenate([-s, s, -s, s], axis=-1)
    return cc, ss


def _group_ones(width):
    i = jnp.arange(width)
    return (i[:, None] // HEAD_DIM == i[None, :] // HEAD_DIM).astype(BF16)


def _t5_bucket(rel):
    nb = N_BUCKETS // 2
    max_exact = nb // 2
    side = jnp.where(rel > 0, nb, 0)
    n = jnp.abs(rel)
    large = max_exact + (jnp.log(jnp.maximum(n, max_exact).astype(F32) / max_exact)
                         / math.log(MAX_DISTANCE / max_exact) * (nb - max_exact)).astype(jnp.int32)
    large = jnp.minimum(large, nb - 1)
    return side + jnp.where(n < max_exact, n, large)


def _bucket_onehot(tq, dilation):
    qi = jnp.arange(tq)
    kj = jnp.arange(tq + 2 * HALF_WIN)
    rel = kj[None, :] - HALF_WIN - qi[:, None]
    bucket = _t5_bucket(rel * dilation).reshape(-1)
    return (bucket[:, None] == jnp.arange(128)[None, :]).astype(BF16)


def _in_proj(x, g1, w_in, cc, ss, gq2, gk2, ones128):
    s_len = x.shape[0]
    tm = min(ROW_TILE, s_len)

    def body(x_ref, g_ref, w_ref, cc_ref, ss_ref, gq_ref, gk_ref, one_ref,
             xn_ref, qpre_ref, kpre_ref, qa_ref, kv_ref, qb_ref, kb_ref, vb_ref):
        xh, _ = _rms_stats(x_ref[...])
        xn = (xh * g_ref[...]).astype(BF16)
        xn_ref[...] = xn
        proj = _mm(xn, w_ref[...])
        first_half = _first_half_mask((tm, 128))
        ones = one_ref[...]
        cc_t, ss_t = cc_ref[...], ss_ref[...]

        def norm_rope(xc, gain):
            ms = _xdot(xc * xc, ones) * (1.0 / HEAD_DIM)
            y = xc * lax.rsqrt(ms + EPS) * gain
            return y * cc_t + _swap_halves(y, first_half) * ss_t

        qpre_ref[...] = proj[:, :D_A]
        kpre_ref[...] = proj[:, D_A:D_A + D_KV_A]
        for c in range(D_A // 128):
            y = norm_rope(proj[:, 128 * c:128 * (c + 1)], gq_ref[...])
            qa_ref[:, 128 * c:128 * (c + 1)] = (y * Q_SCALE).astype(BF16)
        ka = norm_rope(proj[:, D_A:D_A + D_KV_A], gk_ref[...])
        o = D_A + D_KV_A
        va = proj[:, o:o + D_KV_A]
        low = _low_lanes(tm)
        kv_ref[0] = jnp.where(low, ka, pltpu.roll(va, HEAD_DIM, 1)).astype(BF16)
        kv_ref[1] = jnp.where(low, pltpu.roll(ka, HEAD_DIM, 1), va).astype(BF16)
        o += D_KV_A
        qb_ref[...] = (proj[:, o:o + D_B] * Q_SCALE).astype(BF16)
        kb_ref[...] = proj[:, o + D_B:o + 2 * D_B].astype(BF16)
        vb_ref[...] = proj[:, o + 2 * D_B:o + 3 * D_B].astype(BF16)

    sds = jax.ShapeDtypeStruct
    return pl.pallas_call(
        body, name="in_proj", grid=(s_len // tm,),
        in_specs=[_rows(tm, D_MODEL), _full((1, D_MODEL)), _full((D_MODEL, D_IN)), _rows(tm, 128), _rows(tm, 128),
                  _full((1, 128)), _full((1, 128)), _full((128, 128))],
        out_specs=[_rows(tm, D_MODEL), _rows(tm, D_A), _rows(tm, D_KV_A), _rows(tm, D_A),
                   pl.BlockSpec((N_KV_A, tm, 128), lambda i: (0, i, 0)), _rows(tm, D_B), _rows(tm, D_B),
                   _rows(tm, D_B)],
        out_shape=[sds((s_len, D_MODEL), BF16), sds((s_len, D_A), F32), sds((s_len, D_KV_A), F32),
                   sds((s_len, D_A), BF16), sds((N_KV_A, s_len, 128), BF16),
                   sds((s_len, D_B), BF16), sds((s_len, D_B), BF16), sds((s_len, D_B), BF16)],
        compiler_params=_cparams(("parallel",)),
    )(x, g1, w_in, cc, ss, gq2, gk2, ones128)


def _stat_spec(tm):
    return pl.BlockSpec((N_HEADS_B, tm, 1), lambda i: (0, i, 0))


def _merge_b(outs, lses):
    s_len = outs[0].shape[0]
    tm = min(ROW_TILE, s_len)

    def body(o0, o1, o2, l0, l1, l2, yb_ref, lse_ref):
        m_all = jnp.maximum(jnp.maximum(l0[...], l1[...]), l2[...])
        w = [jnp.exp(l[...] - m_all) for l in (l0, l1, l2)]
        den = w[0] + w[1] + w[2]
        yb_ref[...] = (w[0] * o0[...] + w[1] * o1[...] + w[2] * o2[...]) / den
        lse_ref[...] = m_all + jnp.log(den)

    return pl.pallas_call(
        body, name="merge_b", grid=(s_len // tm,),
        in_specs=[_rows(tm, D_B)] * 6,
        out_specs=[_rows(tm, D_B), _rows(tm, D_B)],
        out_shape=[jax.ShapeDtypeStruct((s_len, D_B), F32), jax.ShapeDtypeStruct((s_len, D_B), F32)],
        compiler_params=_cparams(("parallel",)),
    )(*outs, *lses)


def _out_proj(ya, yb, x, g_a, g_b, w_out, g_post, g_mlp_pre):
    s_len = x.shape[0]
    tm = min(ROW_TILE, s_len)

    def body(ya_ref, yb_ref, x_ref, ga_ref, gb_ref, w_ref, gp_ref, gm_ref, ycat_ref, y2_ref, h1_ref, xn2_ref):
        ah, _ = _rms_stats(ya_ref[...])
        bh, _ = _rms_stats(yb_ref[...])
        ycat = jnp.concatenate([ah * ga_ref[...], bh * gb_ref[...]], axis=-1).astype(BF16)
        ycat_ref[...] = ycat
        y2 = _mm(ycat, w_ref[...])
        y2_ref[...] = y2
        y2h, _ = _rms_stats(y2)
        h1 = x_ref[...] + y2h * gp_ref[...]
        h1_ref[...] = h1
        h1h, _ = _rms_stats(h1)
        xn2_ref[...] = (h1h * gm_ref[...]).astype(BF16)

    sds = jax.ShapeDtypeStruct
    return pl.pallas_call(
        body, name="out_proj", grid=(s_len // tm,),
        in_specs=[_rows(tm, D_A), _rows(tm, D_B), _rows(tm, D_MODEL), _full((1, D_A)), _full((1, D_B)),
                  _full((D_MODEL, D_MODEL)), _full((1, D_MODEL)), _full((1, D_MODEL))],
        out_specs=[_rows(tm, D_MODEL)] * 4,
        out_shape=[sds((s_len, D_MODEL), BF16), sds((s_len, D_MODEL), F32), sds((s_len, D_MODEL), F32),
                   sds((s_len, D_MODEL), BF16)],
        compiler_params=_cparams(("parallel",)),
    )(ya, yb, x, g_a, g_b, w_out, g_post, g_mlp_pre)


def _ff1(xn2, w_ff1):
    s_len = xn2.shape[0]
    tm = min(ROW_TILE, s_len)

    def body(x_ref, w_ref, u_ref):
        u_ref[...] = _mm(x_ref[...], w_ref[...])

    return pl.pallas_call(
        body, name="ff1", grid=(s_len // tm,),
        in_specs=[_rows(tm, D_MODEL), _full((D_MODEL, D_FF))],
        out_specs=_rows(tm, D_FF),
        out_shape=jax.ShapeDtypeStruct((s_len, D_FF), F32),
        compiler_params=_cparams(("parallel",)),
    )(xn2, w_ff1)


def _ff2(u, w_ff2, h1, g_post, g_ple):
    s_len = u.shape[0]
    tm = min(ROW_TILE, s_len)

    def body(u_ref, w_ref, h1_ref, gp_ref, gl_ref, f2_ref, h2_ref, xn3_ref):
        f = jnp.square(jnp.maximum(u_ref[...], 0.0)).astype(BF16)
        f2 = _mm(f, w_ref[...])
        f2_ref[...] = f2
        f2h, _ = _rms_stats(f2)
        h2 = h1_ref[...] + f2h * gp_ref[...]
        h2_ref[...] = h2
        h2h, _ = _rms_stats(h2)
        xn3_ref[...] = (h2h * gl_ref[...]).astype(BF16)

    sds = jax.ShapeDtypeStruct
    return pl.pallas_call(
        body, name="ff2", grid=(s_len // tm,),
        in_specs=[_rows(tm, D_FF), _full((D_FF, D_MODEL)), _rows(tm, D_MODEL), _full((1, D_MODEL)),
                  _full((1, D_MODEL))],
        out_specs=[_rows(tm, D_MODEL)] * 3,
        out_shape=[sds((s_len, D_MODEL), F32), sds((s_len, D_MODEL), F32), sds((s_len, D_MODEL), BF16)],
        compiler_params=_cparams(("parallel",)),
    )(u, w_ff2, h1, g_post, g_ple)


def _ple_loss(xn3, p, h2, f2, tgt, w_gate, w_ple, g_ple, g_mlp_post):
    s_len = h2.shape[0]
    tm = min(ROW_TILE, s_len)

    def body(xn3_ref, p_ref, h2_ref, f2_ref, t_ref, wg_ref, wp_ref, gl_ref, gp_ref,
             dh2_ref, df2_ref, dgl_ref, dpp_ref, loss_ref, dgple_ref, dgpost_ref):
        gate = jax.nn.sigmoid(_mm(xn3_ref[...], wg_ref[...]))
        pp = _mm(p_ref[...].astype(BF16), wp_ref[...])
        h2 = h2_ref[...]
        err = h2 + gate * pp - t_ref[...]
        sq = jnp.sum(jnp.sum(err * err, axis=1, keepdims=True), axis=0, keepdims=True)
        _acc_out(loss_ref, sq * (0.5 / D_MODEL))
        dh3 = err * (1.0 / D_MODEL)
        dgl = (dh3 * pp) * gate * (1.0 - gate)
        dgl_b = dgl.astype(BF16)
        dgl_ref[...] = dgl_b
        dpp_ref[...] = (dh3 * gate).astype(BF16)
        dxn3 = _mm_nt(dgl_b, wg_ref[...])
        h2h, r2 = _rms_stats(h2)
        dx, dg = _rms_bwd(dxn3, h2h, r2, gl_ref[...])
        _acc_out(dgple_ref, dg)
        dh2 = dh3 + dx
        dh2_ref[...] = dh2
        f2h, rf = _rms_stats(f2_ref[...])
        df2, dg = _rms_bwd(dh2, f2h, rf, gp_ref[...])
        _acc_out(dgpost_ref, dg)
        df2_ref[...] = df2.astype(BF16)

    sds = jax.ShapeDtypeStruct
    return pl.pallas_call(
        body, name="ple_loss", grid=(s_len // tm,),
        in_specs=[_rows(tm, D_MODEL), _rows(tm, D_PLE), _rows(tm, D_MODEL), _rows(tm, D_MODEL), _rows(tm, D_MODEL),
                  _full((D_MODEL, D_MODEL)), _full((D_PLE, D_MODEL)), _full((1, D_MODEL)), _full((1, D_MODEL))],
        out_specs=[_rows(tm, D_MODEL)] * 3 + [_rows(tm, D_MODEL), _full((1, 1)), _full((1, D_MODEL)),
                                              _full((1, D_MODEL))],
        out_shape=[sds((s_len, D_MODEL), F32), sds((s_len, D_MODEL), BF16), sds((s_len, D_MODEL), BF16),
                   sds((s_len, D_MODEL), BF16), sds((1, 1), F32), sds((1, D_MODEL), F32), sds((1, D_MODEL), F32)],
        compiler_params=_cparams(("arbitrary",)),
    )(xn3, p, h2, f2, tgt, w_gate, w_ple, g_ple, g_mlp_post)


def _ff2_bwd(df2, w_ff2, u):
    s_len = u.shape[0]
    tm = min(ROW_TILE, s_len)

    def body(d_ref, w_ref, u_ref, du_ref):
        df = _mm_nt(d_ref[...], w_ref[...])
        du_ref[...] = (df * (2.0 * jnp.maximum(u_ref[...], 0.0))).astype(BF16)

    return pl.pallas_call(
        body, name="ff2_bwd", grid=(s_len // tm,),
        in_specs=[_rows(tm, D_MODEL), _full((D_FF, D_MODEL)), _rows(tm, D_FF)],
        out_specs=_rows(tm, D_FF),
        out_shape=jax.ShapeDtypeStruct((s_len, D_FF), BF16),
        compiler_params=_cparams(("parallel",)),
    )(df2, w_ff2, u)


def _ff1_bwd(du, w_ff1, dh2, h1, y2, g_mlp_pre, g_post):
    s_len = du.shape[0]
    tm = min(ROW_TILE, s_len)

    def body(du_ref, w_ref, dh2_ref, h1_ref, y2_ref, gm_ref, gp_ref, dh1_ref, dy2_ref, dgm_ref, dgp_ref):
        dxn2 = _mm_nt(du_ref[...], w_ref[...])
        h1h, r1 = _rms_stats(h1_ref[...])
        dx, dg = _rms_bwd(dxn2, h1h, r1, gm_ref[...])
        _acc_out(dgm_ref, dg)
        dh1 = dh2_ref[...] + dx
        dh1_ref[...] = dh1
        y2h, ry = _rms_stats(y2_ref[...])
        dy2, dg = _rms_bwd(dh1, y2h, ry, gp_ref[...])
        _acc_out(dgp_ref, dg)
        dy2_ref[...] = dy2.astype(BF16)

    sds = jax.ShapeDtypeStruct
    return pl.pallas_call(
        body, name="ff1_bwd", grid=(s_len // tm,),
        in_specs=[_rows(tm, D_FF), _full((D_MODEL, D_FF)), _rows(tm, D_MODEL), _rows(tm, D_MODEL),
                  _rows(tm, D_MODEL), _full((1, D_MODEL)), _full((1, D_MODEL))],
        out_specs=[_rows(tm, D_MODEL), _rows(tm, D_MODEL), _full((1, D_MODEL)), _full((1, D_MODEL))],
        out_shape=[sds((s_len, D_MODEL), F32), sds((s_len, D_MODEL), BF16), sds((1, D_MODEL), F32),
                   sds((1, D_MODEL), F32)],
        compiler_params=_cparams(("arbitrary",)),
    )(du, w_ff1, dh2, h1, y2, g_mlp_pre, g_post)


def _out_proj_bwd(dy2, w_out, ya, yb, lse_b, g_a, g_b):
    s_len = ya.shape[0]
    tm = min(ROW_TILE, s_len)

    def body(d_ref, w_ref, ya_ref, yb_ref, lse_ref, ga_ref, gb_ref, dya_ref, dyb_ref, da_ref, st_ref, dga_ref,
             dgb_ref):
        dycat = _mm_nt(d_ref[...], w_ref[...])
        lane = lax.broadcasted_iota(jnp.int32, (tm, 128), 1)
        low = lane < HEAD_DIM
        is_lse = (lane % HEAD_DIM) < (HEAD_DIM // 2)

        def head_sums(prod_chunk):
            return (jnp.sum(jnp.where(low, prod_chunk, 0.0), axis=1, keepdims=True),
                    jnp.sum(jnp.where(low, 0.0, prod_chunk), axis=1, keepdims=True))

        ya = ya_ref[...]
        yh, r = _rms_stats(ya)
        dya, dg = _rms_bwd(dycat[:, :D_A], yh, r, ga_ref[...])
        _acc_out(dga_ref, dg)
        dya_ref[...] = dya
        prod = dya * ya
        for c in range(D_A // 128):
            da_ref[2 * c], da_ref[2 * c + 1] = head_sums(prod[:, 128 * c:128 * (c + 1)])

        yb = yb_ref[...]
        yh, r = _rms_stats(yb)
        dyb, dg = _rms_bwd(dycat[:, D_A:], yh, r, gb_ref[...])
        _acc_out(dgb_ref, dg)
        dyb_ref[...] = dyb.astype(BF16)
        prod = dyb * yb
        for c in range(D_B // 128):
            sl = slice(128 * c, 128 * (c + 1))
            d_lo, d_hi = head_sums(prod[:, sl])
            st_ref[:, sl] = jnp.where(is_lse, lse_ref[:, sl], jnp.where(low, d_lo, d_hi))

    sds = jax.ShapeDtypeStruct
    return pl.pallas_call(
        body, name="out_proj_bwd", grid=(s_len // tm,),
        in_specs=[_rows(tm, D_MODEL), _full((D_MODEL, D_MODEL)), _rows(tm, D_A), _rows(tm, D_B), _rows(tm, D_B),
                  _full((1, D_A)), _full((1, D_B))],
        out_specs=[_rows(tm, D_A), _rows(tm, D_B), _stat_spec(tm), _rows(tm, D_B), _full((1, D_A)),
                   _full((1, D_B))],
        out_shape=[sds((s_len, D_A), F32), sds((s_len, D_B), BF16), sds((N_HEADS_A, s_len, 1), F32),
                   sds((s_len, D_B), F32), sds((1, D_A), F32), sds((1, D_B), F32)],
        compiler_params=_cparams(("arbitrary",)),
    )(dy2, w_out, ya, yb, lse_b, g_a, g_b)


def _in_proj_bwd(dqr, dkv, dqb, dkb, dvb, qpre, kpre, x, dh1, g1, w_in, cc, ss, gq2, gk2, ones128):
    s_len = x.shape[0]
    tm = min(ROW_TILE // 2, s_len)

    def body(dqr_ref, dkv_ref, dq0, dq1, dq2, dk0, dk1, dk2, dv0, dv1, dv2, qpre_ref, kpre_ref, x_ref,
             dh1_ref, g_ref, w_ref, cc_ref, ss_ref, gq_ref, gk_ref, one_ref, dproj_ref, gx_ref, dg1_ref, dgq_ref,
             dgk_ref):
        low = _low_lanes(tm)
        dkr = jnp.where(low, dkv_ref[0], pltpu.roll(dkv_ref[1], HEAD_DIM, 1))
        dva = jnp.where(low, pltpu.roll(dkv_ref[0], HEAD_DIM, 1), dkv_ref[1])
        first_half = _first_half_mask((tm, 128))
        ones = one_ref[...]
        cc_t, ss_t = cc_ref[...], ss_ref[...]

        def norm_rope_bwd(dy, xc, gain):
            dn = dy * cc_t - _swap_halves(dy, first_half) * ss_t
            r = lax.rsqrt(_xdot(xc * xc, ones) * (1.0 / HEAD_DIM) + EPS)
            xh = xc * r
            gdy = dn * gain
            dx = r * (gdy - xh * (_xdot(gdy * xh, ones) * (1.0 / HEAD_DIM)))
            return dx, jnp.sum(dn * xh, axis=0, keepdims=True)

        dgq = jnp.zeros((1, 128), F32)
        parts = []
        for c in range(D_A // 128):
            sl = slice(128 * c, 128 * (c + 1))
            dx, dg = norm_rope_bwd(dqr_ref[:, sl] * Q_SCALE, qpre_ref[:, sl], gq_ref[...])
            parts.append(dx)
            dgq = dgq + dg
        dxk, dgk = norm_rope_bwd(dkr, kpre_ref[...], gk_ref[...])
        _acc_out(dgq_ref, dgq)
        _acc_out(dgk_ref, dgk)
        parts += [dxk, dva, (dq0[...] + dq1[...] + dq2[...]) * Q_SCALE, dk0[...] + dk1[...] + dk2[...],
                  dv0[...] + dv1[...] + dv2[...]]
        dproj = jnp.concatenate(parts, axis=-1).astype(BF16)
        dproj_ref[...] = dproj
        dxn = _mm_nt(dproj, w_ref[...])
        xh, r = _rms_stats(x_ref[...])
        dx, dg = _rms_bwd(dxn, xh, r, g_ref[...])
        _acc_out(dg1_ref, dg)
        gx_ref[...] = dh1_ref[...] + dx

    sds = jax.ShapeDtypeStruct
    return pl.pallas_call(
        body, name="in_proj_bwd", grid=(s_len // tm,),
        in_specs=[_rows(tm, D_A), pl.BlockSpec((N_KV_A, tm, 128), lambda i: (0, i, 0))] + [_rows(tm, D_B)] * 9
                 + [_rows(tm, D_A), _rows(tm, D_KV_A), _rows(tm, D_MODEL), _rows(tm, D_MODEL),
                    _full((1, D_MODEL)), _full((D_MODEL, D_IN)), _rows(tm, 128), _rows(tm, 128), _full((1, 128)),
                    _full((1, 128)), _full((128, 128))],
        out_specs=[_rows(tm, D_IN), _rows(tm, D_MODEL), _full((1, D_MODEL)), _full((1, 128)), _full((1, 128))],
        out_shape=[sds((s_len, D_IN), BF16), sds((s_len, D_MODEL), F32), sds((1, D_MODEL), F32),
                   sds((1, 128), F32), sds((1, 128), F32)],
        compiler_params=_cparams(("arbitrary",)),
    )(dqr, dkv, *dqb, *dkb, *dvb, qpre, kpre, x, dh1, g1, w_in, cc, ss, gq2, gk2, ones128)


def _dw(a, b, name, relu2=False):
    s_len, ka = a.shape
    n = b.shape[1]
    ts = min(DW_TS, s_len)
    bk = min(ka, 1024)
    bn = n if n % 1024 else 1024

    def body(a_ref, b_ref, o_ref):
        @pl.when(pl.program_id(2) == 0)
        def _():
            o_ref[...] = jnp.zeros_like(o_ref)

        av = a_ref[...]
        if relu2:
            av = jnp.square(jnp.maximum(av, 0.0))
        o_ref[...] += _mm_tn(av.astype(BF16), b_ref[...])

    return pl.pallas_call(
        body, name=name, grid=(ka // bk, n // bn, s_len // ts),
        in_specs=[pl.BlockSpec((ts, bk), lambda i, j, k: (k, i)), pl.BlockSpec((ts, bn), lambda i, j, k: (k, j))],
        out_specs=pl.BlockSpec((bk, bn), lambda i, j, k: (i, j)),
        out_shape=jax.ShapeDtypeStruct((ka, n), F32),
        compiler_params=_cparams(("parallel", "parallel", "arbitrary")),
    )(a, b)


def _stack_heads(block, low, data_low):
    parts = []
    for c in range(GROUP_A // 2):
        chunk = block[:, 128 * c:128 * (c + 1)]
        swapped = pltpu.roll(chunk, HEAD_DIM, 1)
        for h_low in (chunk, swapped) if data_low else (swapped, chunk):
            parts.append(jnp.where(low, h_low, 0.0) if data_low else jnp.where(low, 0.0, h_low))
    return jnp.concatenate(parts, axis=0).astype(BF16)


def _unstack_heads(stacked, low, tq, data_low):
    chunks = []
    for c in range(GROUP_A // 2):
        even = stacked[2 * c * tq:(2 * c + 1) * tq]
        odd = stacked[(2 * c + 1) * tq:(2 * c + 2) * tq]
        if data_low:
            chunks.append(jnp.where(low, even, pltpu.roll(odd, HEAD_DIM, 1)))
        else:
            chunks.append(jnp.where(low, pltpu.roll(even, HEAD_DIM, 1), odd))
    return chunks


def _attn_a_fwd(qa, kv, kv_t):
    s_len = kv.shape[1]
    tq = min(ATT_TQ, s_len)
    tk = min(ATT_TK_FWD, s_len)
    sub = min(ATT_SUB_FWD, tk)
    rows = GROUP_A * tq

    def body(q_ref, kv_ref, kvt_ref, o_ref, lse_ref):
        low = _low_lanes(tq)
        q_t = _stack_heads(q_ref[...].astype(F32), low, data_low=True).astype(F32).T.astype(BF16)
        ones_rows = 16
        lo = HEAD_DIM - ones_rows
        is_v = lax.broadcasted_iota(jnp.int32, (128 - lo, sub), 0) >= ones_rows

        def step(j, carry):
            m, acc = carry
            spans = [pl.ds(pl.multiple_of(j * tk + u * sub, sub), sub) for u in range(tk // sub)]
            ahead = _mm(kv_ref[0, spans[0], :], q_t)
            for u, span in enumerate(spans):
                s_t = ahead
                if u + 1 < len(spans):
                    ahead = _mm(kv_ref[0, spans[u + 1], :], q_t)
                m_new = jnp.maximum(m, jnp.max(s_t, axis=0, keepdims=True))
                p_t = jnp.exp(s_t - m_new).astype(BF16)
                kvt = kvt_ref[0, lo:, span]
                acc = jnp.exp(m - m_new) * acc + _mm(jnp.where(is_v, kvt, jnp.ones_like(kvt)), p_t)
                m = m_new
            return m, acc

        init = (jnp.full((1, rows), -jnp.inf, F32), jnp.zeros((128 - lo, rows), F32))
        m, acc = lax.fori_loop(0, s_len // tk, step, init)
        l = acc[:1, :]
        lse_rows = jnp.broadcast_to(m + jnp.log(l), (HEAD_DIM, rows))
        out = jnp.concatenate([lse_rows, acc[ones_rows:, :] / l], axis=0).T
        for c, chunk in enumerate(_unstack_heads(out, low, tq, data_low=False)):
            o_ref[:, 128 * c:128 * (c + 1)] = chunk
        lse_ref[...] = out[:, :1].reshape(GROUP_A, tq, 1)

    return pl.pallas_call(
        body, name="attn_a_fwd", grid=(N_KV_A, s_len // tq),
        in_specs=[pl.BlockSpec((tq, 256), lambda g, i: (i, g)),
                  pl.BlockSpec((1, s_len, 128), lambda g, i: (g, 0, 0)),
                  pl.BlockSpec((1, 128, s_len), lambda g, i: (g, 0, 0))],
        out_specs=[pl.BlockSpec((tq, 256), lambda g, i: (i, g)),
                   pl.BlockSpec((GROUP_A, tq, 1), lambda g, i: (g, i, 0))],
        out_shape=[jax.ShapeDtypeStruct((s_len, D_A), F32),
                   jax.ShapeDtypeStruct((N_HEADS_A, s_len, 1), F32)],
        compiler_params=_cparams(("parallel", "parallel")),
    )(qa, kv, kv_t)


def _attn_a_bwd(qa, dya, kv, lse, delta):
    s_len = kv.shape[1]
    tq = min(ATT_TQ, s_len)
    tk = min(ATT_TK_BWD, s_len)
    rows = GROUP_A * tq

    def body(q_ref, do_ref, kv_ref, lse_ref, dl_ref, dq_ref, dkv_ref):
        @pl.when(pl.program_id(1) == 0)
        def _():
            dkv_ref[...] = jnp.zeros_like(dkv_ref)

        low = _low_lanes(tq)
        q = _stack_heads(q_ref[...].astype(F32), low, data_low=True)
        do = _stack_heads(do_ref[...], low, data_low=False)
        lse_t = lse_ref[...].reshape(rows, 1)
        dl_t = dl_ref[...].reshape(rows, 1)
        q_t = q.astype(F32).T.astype(BF16)
        do_t = do.astype(F32).T.astype(BF16)

        def step(j, dq):
            span = pl.ds(pl.multiple_of(j * tk, tk), tk)
            kvj = kv_ref[0, span, :]
            p = jnp.exp(_mm_nt(q, kvj) - lse_t)
            ds = (p * (_mm_nt(do, kvj) - dl_t)).astype(BF16)
            dkv_ref[0, :, span] += _mm(q_t, ds) + _mm(do_t, p.astype(BF16))
            return dq + _mm(ds, kvj)

        dq = lax.fori_loop(0, s_len // tk, step, jnp.zeros((rows, 128), F32))
        for c, chunk in enumerate(_unstack_heads(dq, low, tq, data_low=True)):
            dq_ref[:, 128 * c:128 * (c + 1)] = chunk

    return pl.pallas_call(
        body, name="attn_a_bwd", grid=(N_KV_A, s_len // tq),
        in_specs=[pl.BlockSpec((tq, 256), lambda g, i: (i, g)),
                  pl.BlockSpec((tq, 256), lambda g, i: (i, g)),
                  pl.BlockSpec((1, s_len, 128), lambda g, i: (g, 0, 0)),
                  pl.BlockSpec((GROUP_A, tq, 1), lambda g, i: (g, i, 0)),
                  pl.BlockSpec((GROUP_A, tq, 1), lambda g, i: (g, i, 0))],
        out_specs=[pl.BlockSpec((tq, 256), lambda g, i: (i, g)),
                   pl.BlockSpec((1, 128, s_len), lambda g, i: (g, 0, 0))],
        out_shape=[jax.ShapeDtypeStruct((s_len, D_A), F32),
                   jax.ShapeDtypeStruct((N_KV_A, 128, s_len), F32)],
        compiler_params=_cparams(("parallel", "arbitrary")),
    )(qa, dya, kv, lse, delta)


class _SwaGeometry:
    def __init__(self, s_len, r):
        self.r = r
        self.tq = SWA_TQ
        self.block = min(max(SWA_MIN_BLOCK, 2 * SWA_TQ * r), s_len)
        self.halo = HALF_WIN * r
        self.nsub = self.block // (self.tq * r)
        self.band = self.tq + 2 * HALF_WIN
        self.length = s_len // r
        self.nblk = s_len // self.block
        self.nhalo = s_len // self.halo
        assert self.nsub * self.tq * r == self.block and self.block % self.halo == 0

    def specs(self):
        per = self.block // self.halo
        cur = pl.BlockSpec((self.block, 128), lambda c, i: (i, c))
        prev = pl.BlockSpec((self.halo, 128), lambda c, i: (jnp.maximum(i * per - 1, 0), c))
        nxt = pl.BlockSpec((self.halo, 128), lambda c, i: (jnp.minimum((i + 1) * per, self.nhalo - 1), c))
        return prev, cur, nxt

    def tiles(self):
        return [(rho + self.r * j * self.tq, self.halo + rho + self.r * (j * self.tq - HALF_WIN), j)
                for j in range(self.nsub) for rho in range(self.r)]

    def own(self, start):
        return pl.ds(start, self.tq, stride=self.r)

    def around(self, start):
        return pl.ds(start, self.band, stride=self.r)

    def fill(self, dst, prev_ref, cur_ref, next_ref):
        dst[:self.halo, :] = prev_ref[...].astype(F32)
        dst[self.halo:self.halo + self.block, :] = cur_ref[...].astype(F32)
        dst[self.halo + self.block:, :] = next_ref[...].astype(F32)

    def first_position(self, j):
        return (pl.program_id(1) * self.block) // self.r + j * self.tq

    def extended(self):
        return pltpu.VMEM((self.block + 2 * self.halo, 128), F32)

    def plain(self):
        return pltpu.VMEM((self.block, 128), F32)


def _low_lanes(rows):
    return lax.broadcasted_iota(jnp.int32, (rows, 128), 1) < HEAD_DIM


def _one_head(x, low, half):
    return jnp.where(low if half == 0 else jnp.logical_not(low), x, 0.0).astype(BF16)


def _swa_fwd(q, k, v, bias, r):
    geo = _SwaGeometry(q.shape[0], r)
    prev, cur, nxt = geo.specs()

    def body(q_ref, kp, kc, kn, vp, vc, vn, b_ref, o_ref, lse_ref, qf, kf, vf):
        qf[...] = q_ref[...].astype(F32)
        geo.fill(kf, kp, kc, kn)
        geo.fill(vf, vp, vc, vn)
        low_q, low_b = _low_lanes(geo.tq), _low_lanes(geo.band)
        row = lax.broadcasted_iota(jnp.int32, (geo.tq, geo.band), 0)
        col = lax.broadcasted_iota(jnp.int32, (geo.tq, geo.band), 1)
        in_window = jnp.abs(col - HALF_WIN - row) <= HALF_WIN
        for own, around, j in geo.tiles():
            key = geo.first_position(j) - HALF_WIN + col
            valid = in_window & (key >= 0) & (key < geo.length)
            qs = qf[geo.own(own), :]
            kb = kf[geo.around(around), :].astype(BF16)
            vb = vf[geo.around(around), :]
            outs, lses = [], []
            for half in range(2):
                s = jnp.where(valid, _mm_nt(_one_head(qs, low_q, half), kb) + b_ref[half], NEG_BIG)
                m = jnp.max(s, axis=1, keepdims=True)
                e = jnp.exp(s - m)
                l = jnp.sum(e, axis=1, keepdims=True)
                outs.append(_mm(e.astype(BF16), _one_head(vb, low_b, half)) / l)
                lses.append(m + jnp.log(l))
            o_ref[geo.own(own), :] = outs[0] + outs[1]
            lse_ref[geo.own(own), :] = jnp.where(low_q, lses[0], lses[1])

    sds = jax.ShapeDtypeStruct
    return pl.pallas_call(
        body, name="swa_fwd_%d" % r, grid=(D_B // 128, geo.nblk),
        in_specs=[cur, prev, cur, nxt, prev, cur, nxt, pl.BlockSpec((2, geo.tq, geo.band), lambda c, i: (c, 0, 0))],
        out_specs=[cur, cur],
        out_shape=[sds(q.shape, F32), sds(q.shape, F32)],
        scratch_shapes=[geo.plain(), geo.extended(), geo.extended()],
        compiler_params=_cparams(("parallel", "parallel")),
    )(q, k, k, k, v, v, v, bias)


def _head_stats(st, half):
    lo = HEAD_DIM * half
    return st[:, lo:lo + 1], st[:, lo + HEAD_DIM // 2:lo + HEAD_DIM // 2 + 1]


def _swa_bwd_q(q, k, v, dy, st, bias, r):
    geo = _SwaGeometry(q.shape[0], r)
    prev, cur, nxt = geo.specs()
    bias_spec = pl.BlockSpec((2, geo.tq, geo.band), lambda c, i: (c, 0, 0))

    def body(q_ref, kp, kc, kn, vp, vc, vn, dy_ref, st_ref, b_ref, dq_ref, db_ref, qf, kf, vf, dyf):
        @pl.when(pl.program_id(1) == 0)
        def _():
            db_ref[...] = jnp.zeros_like(db_ref)

        qf[...] = q_ref[...].astype(F32)
        dyf[...] = dy_ref[...].astype(F32)
        geo.fill(kf, kp, kc, kn)
        geo.fill(vf, vp, vc, vn)
        low_q, low_b = _low_lanes(geo.tq), _low_lanes(geo.band)
        row = lax.broadcasted_iota(jnp.int32, (geo.tq, geo.band), 0)
        col = lax.broadcasted_iota(jnp.int32, (geo.tq, geo.band), 1)
        in_window = jnp.abs(col - HALF_WIN - row) <= HALF_WIN
        for own, around, j in geo.tiles():
            key = geo.first_position(j) - HALF_WIN + col
            valid = in_window & (key >= 0) & (key < geo.length)
            qs = qf[geo.own(own), :]
            dys = dyf[geo.own(own), :]
            sts = st_ref[geo.own(own), :]
            kb = kf[geo.around(around), :]
            vb = vf[geo.around(around), :].astype(BF16)
            dq = jnp.zeros((geo.tq, 128), F32)
            for half in range(2):
                lse, delta = _head_stats(sts, half)
                s = jnp.where(valid, _mm_nt(_one_head(qs, low_q, half), kb.astype(BF16)) + b_ref[half], NEG_BIG)
                p = jnp.exp(s - lse)
                ds = p * (_mm_nt(_one_head(dys, low_q, half), vb) - delta)
                db_ref[half] += ds
                dq = dq + _mm(ds.astype(BF16), _one_head(kb, low_b, half))
            dq_ref[geo.own(own), :] = dq

    return pl.pallas_call(
        body, name="swa_bwd_q_%d" % r, grid=(D_B // 128, geo.nblk),
        in_specs=[cur, prev, cur, nxt, prev, cur, nxt, cur, cur, bias_spec],
        out_specs=[cur, bias_spec],
        out_shape=[jax.ShapeDtypeStruct(q.shape, F32), jax.ShapeDtypeStruct(bias.shape, F32)],
        scratch_shapes=[geo.plain(), geo.extended(), geo.extended(), geo.plain()],
        compiler_params=_cparams(("parallel", "arbitrary")),
    )(q, k, k, k, v, v, v, dy, st, bias)


def _swa_bwd_kv(q, k, v, dy, st, bias_kv, r):
    geo = _SwaGeometry(q.shape[0], r)
    prev, cur, nxt = geo.specs()

    def body(k_ref, v_ref, qp, qc, qn, dp_, dc_, dn_, sp, sc, sn, b_ref, dk_ref, dv_ref, kf, vf, qf, dyf, stf):
        kf[...] = k_ref[...].astype(F32)
        vf[...] = v_ref[...].astype(F32)
        geo.fill(qf, qp, qc, qn)
        geo.fill(dyf, dp_, dc_, dn_)
        geo.fill(stf, sp, sc, sn)
        low_b = _low_lanes(geo.band)
        row = lax.broadcasted_iota(jnp.int32, (geo.tq, geo.band), 0)
        col = lax.broadcasted_iota(jnp.int32, (geo.tq, geo.band), 1)
        in_window = jnp.abs(row + HALF_WIN - col) <= HALF_WIN
        for own, around, j in geo.tiles():
            query = geo.first_position(j) - HALF_WIN + col
            valid = in_window & (query >= 0) & (query < geo.length)
            ks = kf[geo.own(own), :].astype(BF16)
            vs = vf[geo.own(own), :].astype(BF16)
            qb = qf[geo.around(around), :]
            dyb = dyf[geo.around(around), :]
            st_t = stf[geo.around(around), :].T
            dk = jnp.zeros((geo.tq, 128), F32)
            dv = jnp.zeros((geo.tq, 128), F32)
            for half in range(2):
                lo = HEAD_DIM * half
                lse, delta = st_t[lo:lo + 1, :], st_t[lo + HEAD_DIM // 2:lo + HEAD_DIM // 2 + 1, :]
                q_h, dy_h = _one_head(qb, low_b, half), _one_head(dyb, low_b, half)
                s = jnp.where(valid, _mm_nt(ks, q_h) + b_ref[half], NEG_BIG)
                p = jnp.exp(s - lse)
                ds = p * (_mm_nt(vs, dy_h) - delta)
                dv = dv + _mm(p.astype(BF16), dy_h)
                dk = dk + _mm(ds.astype(BF16), q_h)
            dk_ref[geo.own(own), :] = dk
            dv_ref[geo.own(own), :] = dv

    return pl.pallas_call(
        body, name="swa_bwd_kv_%d" % r, grid=(D_B // 128, geo.nblk),
        in_specs=[cur, cur, prev, cur, nxt, prev, cur, nxt, prev, cur, nxt,
                  pl.BlockSpec((2, geo.tq, geo.band), lambda c, i: (c, 0, 0))],
        out_specs=[cur, cur],
        out_shape=[jax.ShapeDtypeStruct(q.shape, F32), jax.ShapeDtypeStruct(q.shape, F32)],
        scratch_shapes=[geo.plain(), geo.plain(), geo.extended(), geo.extended(), geo.extended()],
        compiler_params=_cparams(("parallel", "parallel")),
    )(k, v, q, q, q, dy, dy, dy, st, st, st, bias_kv)


BIAS_ROWS = 16
BIAS_TN = 4096


def _bias_tiles(onehot, rel_bias_t):
    n = onehot.shape[0]

    def body(oh_ref, rb_ref, o_ref):
        o_ref[...] = sum(_mm_nt(piece, oh_ref[...]) for piece in _split3(rb_ref[...]))

    return pl.pallas_call(
        body, name="bias_tiles", grid=(n // BIAS_TN,),
        in_specs=[_rows(BIAS_TN, 128), _full((BIAS_ROWS, 128))],
        out_specs=pl.BlockSpec((BIAS_ROWS, BIAS_TN), lambda i: (0, i)),
        out_shape=jax.ShapeDtypeStruct((BIAS_ROWS, n), F32),
        compiler_params=_cparams(("parallel",)),
    )(onehot, rel_bias_t)


def _bias_bwd(onehot, dbias_rows, so_far, r):
    n = onehot.shape[0]

    def body(oh, d, prev_ref, g_ref):
        @pl.when(pl.program_id(0) == 0)
        def _():
            g_ref[...] = prev_ref[...]

        hi, lo, _ = _split3(d[...])
        g_ref[...] += _mm(hi, oh[...]) + _mm(lo, oh[...])

    return pl.pallas_call(
        body, name="bias_bwd_%d" % r, grid=(n // BIAS_TN,),
        in_specs=[_rows(BIAS_TN, 128), pl.BlockSpec((BIAS_ROWS, BIAS_TN), lambda i: (0, i)), _full((BIAS_ROWS, 128))],
        out_specs=_full((BIAS_ROWS, 128)),
        out_shape=jax.ShapeDtypeStruct((BIAS_ROWS, 128), F32),
        compiler_params=_cparams(("arbitrary",)),
    )(onehot, dbias_rows, so_far)


def _local_step(x, p, tgt, w_in, w_out, w_ff1, w_ff2, w_gate, w_ple, g_attn_pre, g_q, g_k, g_out_a, g_out_b,
                g_attn_post, rel_bias, g_mlp_pre, g_mlp_post, g_ple):
    s_len = x.shape[0]
    cc, ss = _rope_tables(s_len)
    gq2 = jnp.concatenate([g_q, g_q], axis=-1)
    gk2 = jnp.concatenate([g_k, g_k], axis=-1)
    ones128 = _group_ones(128)
    rel_bias_t = jnp.zeros((BIAS_ROWS, 128), F32).at[:N_HEADS_B, :N_BUCKETS].set(rel_bias.T)

    xn1, qpre, kpre, qa, kv, qb, kb, vb = _in_proj(x, g_attn_pre, w_in, cc, ss, gq2, gk2, ones128)
    ya, lse_a = _attn_a_fwd(qa, kv, kv.transpose(0, 2, 1))

    tiles, outs, lses = [], [], []
    for r in DILATIONS:
        tq = SWA_TQ
        onehot = _bucket_onehot(tq, r)
        bias = _bias_tiles(onehot, rel_bias_t)[:N_HEADS_B].reshape(N_HEADS_B, tq, tq + 2 * HALF_WIN)
        o_r, lse_r = _swa_fwd(qb, kb, vb, bias, r)
        tiles.append((onehot, bias))
        outs.append(o_r)
        lses.append(lse_r)
    yb, lse_b = _merge_b(outs, lses)

    ycat, y2, h1, xn2 = _out_proj(ya, yb, x, g_out_a, g_out_b, w_out, g_attn_post, g_mlp_pre)
    u = _ff1(xn2, w_ff1)
    f2, h2, xn3 = _ff2(u, w_ff2, h1, g_mlp_post, g_ple)
    dh2, df2, dgl, dpp, loss, dg_ple, dg_mlp_post = _ple_loss(xn3, p, h2, f2, tgt, w_gate, w_ple, g_ple, g_mlp_post)

    grads = {"g_ple": dg_ple, "g_mlp_post": dg_mlp_post}
    grads["w_ple_gate"] = _dw(xn3, dgl, "dw_gate")
    grads["w_ple_proj"] = _dw(p, dpp, "dw_ple")
    grads["w_ff2"] = _dw(u, df2, "dw_ff2", relu2=True)
    du = _ff2_bwd(df2, w_ff2, u)
    grads["w_ff1"] = _dw(xn2, du, "dw_ff1")
    dh1, dy2, grads["g_mlp_pre"], grads["g_attn_post"] = _ff1_bwd(du, w_ff1, dh2, h1, y2, g_mlp_pre, g_attn_post)
    grads["w_out"] = _dw(ycat, dy2, "dw_out")
    dya, dyb, delta_a, st_b, grads["g_out_a"], grads["g_out_b"] = _out_proj_bwd(dy2, w_out, ya, yb, lse_b, g_out_a,
                                                                              g_out_b)

    dqr, dkv_t = _attn_a_bwd(qa, dya, kv, lse_a, delta_a)
    dkv_a = dkv_t.transpose(0, 2, 1)

    dqs, dks, dvs = [], [], []
    d_rel = jnp.zeros((BIAS_ROWS, 128), F32)
    for r, (onehot, bias) in zip(DILATIONS, tiles):
        dq_r, dbias = _swa_bwd_q(qb, kb, vb, dyb, st_b, bias, r)
        bias_kv = jnp.flip(bias, axis=(1, 2))
        dk_r, dv_r = _swa_bwd_kv(qb, kb, vb, dyb, st_b, bias_kv, r)
        dbias_rows = jnp.pad(dbias.reshape(N_HEADS_B, -1), ((0, BIAS_ROWS - N_HEADS_B), (0, 0)))
        d_rel = _bias_bwd(onehot, dbias_rows, d_rel, r)
        dqs.append(dq_r)
        dks.append(dk_r)
        dvs.append(dv_r)
    grads["rel_bias"] = d_rel[:N_HEADS_B, :N_BUCKETS].T

    dproj, grad_x, grads["g_attn_pre"], dgq2, dgk2 = _in_proj_bwd(
        dqr, dkv_a, dqs, dks, dvs, qpre, kpre, x, dh1, g_attn_pre, w_in, cc, ss, gq2, gk2, ones128)
    grads["g_q"] = dgq2[:, :HEAD_DIM] + dgq2[:, HEAD_DIM:]
    grads["g_k"] = dgk2[:, :HEAD_DIM] + dgk2[:, HEAD_DIM:]
    grads["w_in"] = _dw(xn1, dproj, "dw_in")
    return loss, grad_x, grads


ANY = pl.BlockSpec(memory_space=pl.ANY)


def _position():
    return lax.axis_index("x"), lax.axis_index("y"), lax.axis_index("c")


def _other_chips(x, y):
    return [(2 * (1 - x) + y, (1 - x, y)), (2 * x + (1 - y), (x, 1 - y)), (2 * (1 - x) + (1 - y), (1 - x, 1 - y))]


def _cast_shards(shards):
    def body(*refs):
        n = len(refs) // 2
        for i_ref, o_ref in zip(refs[:n], refs[n:]):
            o_ref[...] = i_ref[...].astype(BF16)

    return pl.pallas_call(
        body, name="cast_shards",
        in_specs=[pl.BlockSpec(memory_space=pltpu.VMEM)] * len(shards),
        out_specs=[pl.BlockSpec(memory_space=pltpu.VMEM)] * len(shards),
        out_shape=[jax.ShapeDtypeStruct(s.shape, BF16) for s in shards],
        compiler_params=_cparams(),
    )(*shards)


def _gather_weights(shards):
    n = len(shards)

    def body(*refs):
        ins, outs = refs[:n], refs[n:2 * n]
        send_sems, recv_sems, pass_send_sems, pass_recv_sems = refs[2 * n:]
        x, y, c = _position()
        me = 2 * x + y
        sibling = (x, y, 1 - c)

        def rows(a, core):
            half = ins[a].shape[0] // 2
            return pl.ds(pl.multiple_of(core * half, 16), half)

        sends = []
        for k, (_, chip) in enumerate(_other_chips(x, y)):
            for a in range(n):
                cp = pltpu.make_async_remote_copy(ins[a].at[rows(a, c), :], outs[a].at[me, rows(a, c), :],
                                                  send_sems.at[k, a], recv_sems.at[k, a], device_id=(*chip, c),
                                                  device_id_type=MESH)
                cp.start()
                sends.append(cp)
        for k, (num, chip) in enumerate(_other_chips(x, y)):
            for a in range(n):
                landed = outs[a].at[num, rows(a, c), :]
                pltpu.make_async_remote_copy(landed, landed, send_sems.at[k, a], recv_sems.at[k, a],
                                             device_id=(*chip, c), device_id_type=MESH).wait_recv()
                cp = pltpu.make_async_remote_copy(landed, landed, pass_send_sems.at[k, a], pass_recv_sems.at[k, a],
                                                  device_id=sibling, device_id_type=MESH)
                cp.start()
                sends.append(cp)
        for k, (num, _) in enumerate(_other_chips(x, y)):
            for a in range(n):
                passed = outs[a].at[num, rows(a, 1 - c), :]
                pltpu.make_async_remote_copy(passed, passed, pass_send_sems.at[k, a], pass_recv_sems.at[k, a],
                                             device_id=sibling, device_id_type=MESH).wait_recv()
        for cp in sends:
            cp.wait_send()

    return pl.pallas_call(
        body, name="gather_weights",
        in_specs=[ANY] * n, out_specs=[ANY] * n,
        out_shape=[jax.ShapeDtypeStruct((N_CHIPS,) + s.shape, s.dtype) for s in shards],
        scratch_shapes=[pltpu.SemaphoreType.DMA((3, n))] * 4,
    )(*shards)


def _send_sibling_half(grads):
    n = len(grads)

    def body(*refs):
        ins, outs = refs[:n], refs[n:2 * n]
        send_sems, recv_sems = refs[2 * n:]
        x, y, c = _position()
        copies = []
        for a in range(n):
            half = ins[a].shape[1] // 2
            theirs = ins[a].at[:, pl.ds(pl.multiple_of((1 - c) * half, 8), half), :]
            cp = pltpu.make_async_remote_copy(theirs, outs[a], send_sems.at[a], recv_sems.at[a],
                                              device_id=(x, y, 1 - c), device_id_type=MESH)
            cp.start()
            copies.append(cp)
        for cp in copies:
            cp.wait()

    return pl.pallas_call(
        body, name="send_sibling_half",
        in_specs=[ANY] * n, out_specs=[ANY] * n,
        out_shape=[jax.ShapeDtypeStruct((g.shape[0], g.shape[1] // 2, g.shape[2]), g.dtype) for g in grads],
        scratch_shapes=[pltpu.SemaphoreType.DMA((n,)), pltpu.SemaphoreType.DMA((n,))],
    )(*grads)


def _scatter_to_chips(pairs):
    n = len(pairs)

    def body(*refs):
        ins, outs = refs[:n], refs[n:2 * n]
        send_sems, recv_sems = refs[2 * n:]
        x, y, c = _position()
        me = 2 * x + y
        sends = []
        for k, (num, chip) in enumerate(_other_chips(x, y)):
            for a in range(n):
                cp = pltpu.make_async_remote_copy(ins[a].at[num], outs[a].at[me], send_sems.at[k, a],
                                                  recv_sems.at[k, a], device_id=(*chip, c), device_id_type=MESH)
                cp.start()
                sends.append(cp)
        for k, (num, chip) in enumerate(_other_chips(x, y)):
            for a in range(n):
                pltpu.make_async_remote_copy(ins[a].at[me], outs[a].at[num], send_sems.at[k, a], recv_sems.at[k, a],
                                             device_id=(*chip, c), device_id_type=MESH).wait_recv()
        for cp in sends:
            cp.wait_send()

    return pl.pallas_call(
        body, name="scatter_to_chips",
        in_specs=[ANY] * n, out_specs=[ANY] * n,
        out_shape=[jax.ShapeDtypeStruct(g.shape, g.dtype) for g in pairs],
        scratch_shapes=[pltpu.SemaphoreType.DMA((3, n)), pltpu.SemaphoreType.DMA((3, n))],
    )(*pairs)


def _exchange_halves(halves):
    n = len(halves)

    def body(*refs):
        ins, outs = refs[:n], refs[n:2 * n]
        send_sems, recv_sems = refs[2 * n:]
        x, y, c = _position()
        copies = []
        for a in range(n):
            cp = pltpu.make_async_remote_copy(ins[a], outs[a], send_sems.at[a], recv_sems.at[a],
                                              device_id=(x, y, 1 - c), device_id_type=MESH)
            cp.start()
            copies.append(cp)
        for cp in copies:
            cp.wait()

    return pl.pallas_call(
        body, name="exchange_halves",
        in_specs=[ANY] * n, out_specs=[ANY] * n,
        out_shape=[jax.ShapeDtypeStruct(h.shape, h.dtype) for h in halves],
        scratch_shapes=[pltpu.SemaphoreType.DMA((n,)), pltpu.SemaphoreType.DMA((n,))],
    )(*halves)


def _allreduce_small(v):
    def body(v_ref, o_ref, buf, send_sems, recv_sems):
        x, y, c = _position()
        me = 4 * x + 2 * y + c
        peers = [(1 - x, y, c), (x, 1 - y, c), (x, y, 1 - c), (1 - x, 1 - y, c), (1 - x, y, 1 - c), (x, 1 - y, 1 - c),
                 (1 - x, 1 - y, 1 - c)]
        num = lambda d: 4 * d[0] + 2 * d[1] + d[2]
        buf[me] = v_ref[...]
        sends = []
        for k, peer in enumerate(peers):
            cp = pltpu.make_async_remote_copy(v_ref, buf.at[me], send_sems.at[k], recv_sems.at[k], device_id=peer,
                                              device_id_type=MESH)
            cp.start()
            sends.append(cp)
        for k, peer in enumerate(peers):
            pltpu.make_async_remote_copy(v_ref, buf.at[num(peer)], send_sems.at[k], recv_sems.at[k], device_id=peer,
                                         device_id_type=MESH).wait_recv()
        for cp in sends:
            cp.wait_send()
        total = buf[0]
        for d in range(1, 8):
            total = total + buf[d]
        o_ref[...] = total

    return pl.pallas_call(
        body, name="allreduce_small",
        in_specs=[pl.BlockSpec(memory_space=pltpu.VMEM)], out_specs=pl.BlockSpec(memory_space=pltpu.VMEM),
        out_shape=jax.ShapeDtypeStruct(v.shape, v.dtype),
        scratch_shapes=[pltpu.VMEM((8,) + v.shape, v.dtype), pltpu.SemaphoreType.DMA((7,)),
                        pltpu.SemaphoreType.DMA((7,))],
    )(v)


def _sum_leading(a, name):
    k, r, c = a.shape
    tr = min(r, 256)

    def body(a_ref, o_ref):
        total = a_ref[0].astype(F32)
        for i in range(1, k):
            total = total + a_ref[i].astype(F32)
        o_ref[...] = total

    return pl.pallas_call(
        body, name=name, grid=(r // tr,),
        in_specs=[pl.BlockSpec((k, tr, c), lambda i: (0, i, 0))],
        out_specs=pl.BlockSpec((tr, c), lambda i: (i, 0)),
        out_shape=jax.ShapeDtypeStruct((r, c), F32),
        compiler_params=_cparams(("parallel",)),
    )(a)


def _add(a, b, name):
    k, r, c = a.shape
    tr = min(r, 256)
    spec = pl.BlockSpec((k, tr, c), lambda i: (0, i, 0))

    def body(a_ref, b_ref, o_ref):
        o_ref[...] = (a_ref[...] + b_ref[...]).astype(BF16)

    return pl.pallas_call(
        body, name=name, grid=(r // tr,), in_specs=[spec, spec], out_specs=spec,
        out_shape=jax.ShapeDtypeStruct(a.shape, BF16), compiler_params=_cparams(("parallel",)),
    )(a, b)


def _adamw(w, g, m, v, name):
    r, c = w.shape
    tr = min(r, 256)
    spec = pl.BlockSpec((tr, c), lambda i: (i, 0))

    def body(w_ref, g_ref, m_ref, v_ref, d_ref, nm_ref, nv_ref):
        gv = g_ref[...]
        nm = ADAM_B1 * m_ref[...] + (1.0 - ADAM_B1) * gv
        nv = ADAM_B2 * v_ref[...] + (1.0 - ADAM_B2) * jnp.square(gv)
        m_hat = nm / (1.0 - ADAM_B1 ** ADAM_STEP)
        v_hat = nv / (1.0 - ADAM_B2 ** ADAM_STEP)
        d_ref[...] = -ADAM_LR * (m_hat / (jnp.sqrt(v_hat) + ADAM_EPS) + ADAM_WD * w_ref[...])
        nm_ref[...] = nm
        nv_ref[...] = nv

    return pl.pallas_call(
        body, name=name, grid=(r // tr,), in_specs=[spec] * 4, out_specs=[spec] * 3,
        out_shape=[jax.ShapeDtypeStruct(w.shape, F32)] * 3, compiler_params=_cparams(("parallel",)),
    )(w, g, m, v)


MATRICES = ("w_in", "w_out", "w_ff1", "w_ff2", "w_ple_gate", "w_ple_proj")
COLUMN_SHARDED = ("w_in", "w_ff1", "w_ple_proj")
SMALL = ("g_attn_pre", "g_q", "g_k", "g_out_a", "g_out_b", "g_attn_post", "rel_bias", "g_mlp_pre", "g_mlp_post",
         "g_ple")
WEIGHT_ORDER = ("w_in", "g_attn_pre", "g_q", "g_k", "g_out_a", "g_out_b", "w_out", "g_attn_post", "rel_bias",
                "g_mlp_pre", "w_ff1", "w_ff2", "g_mlp_post", "g_ple", "w_ple_gate", "w_ple_proj")
PACK_ROWS, PACK_COLS = 8, 1024


def _pack_small(values, extra=None):
    flat = [values[n].reshape(-1) for n in SMALL]
    used = sum(f.shape[0] for f in flat)
    tail = jnp.zeros((PACK_ROWS * PACK_COLS - used - 1,), F32)
    last = jnp.zeros((1,), F32) if extra is None else extra.reshape(1)
    return jnp.concatenate(flat + [tail, last]).reshape(PACK_ROWS, PACK_COLS)


def _unpack_small(packed, like):
    flat = packed.reshape(-1)
    out, o = {}, 0
    for n in SMALL:
        size = like[n].size
        out[n] = flat[o:o + size].reshape(like[n].shape)
        o += size
    return out, flat[-1]


def kernel(x, p, w_in, g_attn_pre, g_q, g_k, g_out_a, g_out_b, w_out, g_attn_post, rel_bias, g_mlp_pre, w_ff1, w_ff2, g_mlp_post, g_ple, w_ple_gate, w_ple_proj, loss_target, m_w_in, m_g_attn_pre, m_g_q, m_g_k, m_g_out_a, m_g_out_b, m_w_out, m_g_attn_post, m_rel_bias, m_g_mlp_pre, m_w_ff1, m_w_ff2, m_g_mlp_post, m_g_ple, m_w_ple_gate, m_w_ple_proj, v_w_in, v_g_attn_pre, v_g_q, v_g_k, v_g_out_a, v_g_out_b, v_w_out, v_g_attn_post, v_rel_bias, v_g_mlp_pre, v_w_ff1, v_w_ff2, v_g_mlp_post, v_g_ple, v_w_ple_gate, v_w_ple_proj):
    given = dict(locals())
    weights = {n: given[n] for n in WEIGHT_ORDER}
    shards = {n: weights[n][0] for n in MATRICES}

    chip = 2 * lax.axis_index("x") + lax.axis_index("y")
    c = lax.axis_index("c")
    own = _cast_shards([shards[n] for n in MATRICES])
    whole = {}
    for n, g, mine in zip(MATRICES, _gather_weights(own), own):
        g = lax.dynamic_update_slice_in_dim(g, mine[None], chip, axis=0)
        if n in COLUMN_SHARDED:
            whole[n] = g.transpose(1, 0, 2).reshape(g.shape[1], N_CHIPS * g.shape[2])
        else:
            whole[n] = g.reshape(N_CHIPS * g.shape[1], g.shape[2])

    loss, grad_x, grads = _local_step(
        x[0], p[0, 0], loss_target[0], whole["w_in"], whole["w_out"], whole["w_ff1"], whole["w_ff2"],
        whole["w_ple_gate"], whole["w_ple_proj"], g_attn_pre, g_q, g_k, g_out_a, g_out_b, g_attn_post, rel_bias,
        g_mlp_pre, g_mlp_post, g_ple)

    by_chip = []
    for n in MATRICES:
        g = grads[n]
        if n in COLUMN_SHARDED:
            by_chip.append(g.reshape(g.shape[0], N_CHIPS, g.shape[1] // N_CHIPS).transpose(1, 0, 2))
        else:
            by_chip.append(g.reshape(N_CHIPS, g.shape[0] // N_CHIPS, g.shape[1]))
    from_sibling = _send_sibling_half(by_chip)
    pairs = []
    for n, g, other in zip(MATRICES, by_chip, from_sibling):
        half = g.shape[1] // 2
        mine = lax.dynamic_slice_in_dim(g, c * half, half, axis=1)
        pairs.append(_add(mine, other, "pair_sum_" + n))
    halves = []
    for n, pair, got in zip(MATRICES, pairs, _scatter_to_chips(pairs)):
        got = lax.dynamic_update_slice_in_dim(got, lax.dynamic_slice_in_dim(pair, chip, 1, axis=0), chip, axis=0)
        halves.append(_sum_leading(got, "chip_sum_" + n))
    grad_w = {}
    for n, mine, theirs in zip(MATRICES, halves, _exchange_halves(halves)):
        half = mine.shape[0]
        g = jnp.zeros((2 * half, mine.shape[1]), F32)
        g = lax.dynamic_update_slice_in_dim(g, mine, c * half, axis=0)
        grad_w[n] = lax.dynamic_update_slice_in_dim(g, theirs, (1 - c) * half, axis=0)

    small_like = {n: weights[n] for n in SMALL}
    reduced = _allreduce_small(_pack_small({n: grads[n] for n in SMALL}, extra=loss))
    grad_small, loss_total = _unpack_small(reduced, small_like)

    delta, new_m, new_v = {}, {}, {}
    for n in MATRICES:
        d, nm, nv = _adamw(shards[n], grad_w[n], given["m_" + n][0], given["v_" + n][0], "adamw_" + n)
        delta[n], new_m[n], new_v[n] = d[None], nm[None], nv[None]
        grad_w[n] = grad_w[n][None]
    d, nm, nv = _adamw(_pack_small(small_like), reduced, _pack_small({n: given["m_" + n] for n in SMALL}),
                       _pack_small({n: given["v_" + n] for n in SMALL}), "adamw_small")
    d_small, nm_small, nv_small = (_unpack_small(a, small_like)[0] for a in (d, nm, nv))
    for n in SMALL:
        grad_w[n], delta[n], new_m[n], new_v[n] = grad_small[n], d_small[n], nm_small[n], nv_small[n]

    return (loss_total, grad_x[None], *[grad_w[n] for n in WEIGHT_ORDER], *[delta[n] for n in WEIGHT_ORDER],
            *[new_m[n] for n in WEIGHT_ORDER], *[new_v[n] for n in WEIGHT_ORDER])
```

```python
import functools
import math

import jax
import jax.numpy as jnp
from jax import lax
from jax.experimental import pallas as pl
from jax.experimental.pallas import tpu as pltpu

F32 = jnp.float32
BF16 = jnp.bfloat16

D_MODEL = 1024
HEAD_DIM = 64
N_HEADS_A = 8
N_KV_A = 2
GROUP_A = N_HEADS_A // N_KV_A
N_HEADS_B = 8
D_A = N_HEADS_A * HEAD_DIM
D_KV_A = N_KV_A * HEAD_DIM
D_B = N_HEADS_B * HEAD_DIM
D_IN = D_A + 2 * D_KV_A + 3 * D_B
D_FF = 4 * D_MODEL
D_PLE = 256
GRID_W = 64
ROPE_THETA = 10000.0
DILATIONS = (1, 4, 16)
HALF_WIN = 64
N_BUCKETS = 32
MAX_DISTANCE = 1024
EPS = 1e-6
NEG_BIG = -1e30
Q_SCALE = HEAD_DIM ** -0.5

ADAM_LR = 0.001
ADAM_B1 = 0.9
ADAM_B2 = 0.999
ADAM_EPS = 1e-08
ADAM_WD = 0.01
ADAM_STEP = 10

N_CHIPS = 4
MESH = pl.DeviceIdType.MESH

ROW_TILE = 512
ATT_TQ = 256
ATT_TQ_BWD = 512
ATT_TK_FWD = 2048
ATT_TK_BWD = 512
SWA_TQ = 128
SWA_MIN_BLOCK = 1024
DW_TS = 1024
VMEM_LIMIT = 56 * 1024 * 1024

NT = (((1,), (1,)), ((), ()))
TN = (((0,), (0,)), ((), ()))


def _cparams(sem=None, vmem=VMEM_LIMIT):
    return pltpu.CompilerParams(dimension_semantics=sem, vmem_limit_bytes=vmem)


def _full(shape):
    n = len(shape)
    return pl.BlockSpec(shape, lambda *_: (0,) * n)


def _rows(tm, width):
    return pl.BlockSpec((tm, width), lambda i: (i, 0))


def _split3(a):
    a1 = a.astype(BF16)
    r = a - a1.astype(F32)
    a2 = r.astype(BF16)
    a3 = (r - a2.astype(F32)).astype(BF16)
    return a1, a2, a3


def _xdot(a, sel):
    a1, a2, a3 = _split3(a)
    d = lambda p: jnp.dot(p, sel, preferred_element_type=F32)
    return d(a1) + d(a2) + d(a3)


def _mm(a, b):
    return jnp.dot(a, b, preferred_element_type=F32)


def _mm_nt(a, b):
    return lax.dot_general(a, b, NT, preferred_element_type=F32)


def _mm_tn(a, b):
    return lax.dot_general(a, b, TN, preferred_element_type=F32)


def _rms_stats(x):
    r = lax.rsqrt(jnp.mean(x * x, axis=-1, keepdims=True) + EPS)
    return x * r, r


def _rms_bwd(dy, xh, r, g):
    gdy = dy * g
    dx = r * (gdy - xh * jnp.mean(gdy * xh, axis=-1, keepdims=True))
    dg = jnp.sum(dy * xh, axis=0, keepdims=True)
    return dx, dg


def _acc_out(ref, val):
    @pl.when(pl.program_id(0) == 0)
    def _():
        ref[...] = jnp.zeros_like(ref)

    ref[...] += val


def _swap_halves(x, first_half):
    return jnp.where(first_half, pltpu.roll(x, 96, 1), pltpu.roll(x, 32, 1))


def _first_half_mask(shape):
    return (lax.broadcasted_iota(jnp.int32, shape, 1) % HEAD_DIM) < (HEAD_DIM // 2)


def _rope_tables(s_len):
    t = jnp.arange(s_len)
    row = (t // GRID_W).astype(F32)
    col = (t % GRID_W).astype(F32)
    n_axis = HEAD_DIM // 4
    inv_freq = ROPE_THETA ** (-jnp.arange(n_axis, dtype=F32) / n_axis)
    ang = jnp.concatenate([row[:, None] * inv_freq, col[:, None] * inv_freq], axis=-1)
    c, s = jnp.cos(ang), jnp.sin(ang)
    cc = jnp.concatenate([c, c, c, c], axis=-1)
    ss = jnp.concatenate([-s, s, -s, s], axis=-1)
    return cc, ss


def _group_ones(width):
    i = jnp.arange(width)
    return (i[:, None] // HEAD_DIM == i[None, :] // HEAD_DIM).astype(BF16)


def _t5_bucket(rel):
    nb = N_BUCKETS // 2
    max_exact = nb // 2
    side = jnp.where(rel > 0, nb, 0)
    n = jnp.abs(rel)
    large = max_exact + (jnp.log(jnp.maximum(n, max_exact).astype(F32) / max_exact)
                         / math.log(MAX_DISTANCE / max_exact) * (nb - max_exact)).astype(jnp.int32)
    large = jnp.minimum(large, nb - 1)
    return side + jnp.where(n < max_exact, n, large)


def _bucket_onehot(tq, dilation):
    qi = jnp.arange(tq)
    kj = jnp.arange(tq + 2 * HALF_WIN)
    rel = kj[None, :] - HALF_WIN - qi[:, None]
    bucket = _t5_bucket(rel * dilation).reshape(-1)
    return (bucket[:, None] == jnp.arange(128)[None, :]).astype(BF16)


def _in_proj(x, g1, w_in, cc, ss, gq2, gk2, ones128):
    s_len = x.shape[0]
    tm = min(ROW_TILE, s_len)

    def body(x_ref, g_ref, w_ref, cc_ref, ss_ref, gq_ref, gk_ref, one_ref,
             xn_ref, qpre_ref, kpre_ref, qa_ref, kv_ref, qb_ref, kb_ref, vb_ref):
        xh, _ = _rms_stats(x_ref[...])
        xn = (xh * g_ref[...]).astype(BF16)
        xn_ref[...] = xn
        proj = _mm(xn, w_ref[...])
        first_half = _first_half_mask((tm, 128))
        ones = one_ref[...]
        cc_t, ss_t = cc_ref[...], ss_ref[...]

        def norm_rope(xc, gain):
            ms = _xdot(xc * xc, ones) * (1.0 / HEAD_DIM)
            y = xc * lax.rsqrt(ms + EPS) * gain
            return y * cc_t + _swap_halves(y, first_half) * ss_t

        qpre_ref[...] = proj[:, :D_A]
        kpre_ref[...] = proj[:, D_A:D_A + D_KV_A]
        for c in range(D_A // 128):
            y = norm_rope(proj[:, 128 * c:128 * (c + 1)], gq_ref[...])
            qa_ref[:, 128 * c:128 * (c + 1)] = (y * Q_SCALE).astype(BF16)
        ka = norm_rope(proj[:, D_A:D_A + D_KV_A], gk_ref[...])
        o = D_A + D_KV_A
        va = proj[:, o:o + D_KV_A]
        low = _low_lanes(tm)
        kv_ref[0] = jnp.where(low, ka, pltpu.roll(va, HEAD_DIM, 1)).astype(BF16)
        kv_ref[1] = jnp.where(low, pltpu.roll(ka, HEAD_DIM, 1), va).astype(BF16)
        o += D_KV_A
        qb_ref[...] = (proj[:, o:o + D_B] * Q_SCALE).astype(BF16)
        kb_ref[...] = proj[:, o + D_B:o + 2 * D_B].astype(BF16)
        vb_ref[...] = proj[:, o + 2 * D_B:o + 3 * D_B].astype(BF16)

    sds = jax.ShapeDtypeStruct
    return pl.pallas_call(
        body, name="in_proj", grid=(s_len // tm,),
        in_specs=[_rows(tm, D_MODEL), _full((1, D_MODEL)), _full((D_MODEL, D_IN)), _rows(tm, 128), _rows(tm, 128),
                  _full((1, 128)), _full((1, 128)), _full((128, 128))],
        out_specs=[_rows(tm, D_MODEL), _rows(tm, D_A), _rows(tm, D_KV_A), _rows(tm, D_A),
                   pl.BlockSpec((N_KV_A, tm, 128), lambda i: (0, i, 0)), _rows(tm, D_B), _rows(tm, D_B),
                   _rows(tm, D_B)],
        out_shape=[sds((s_len, D_MODEL), BF16), sds((s_len, D_A), F32), sds((s_len, D_KV_A), F32),
                   sds((s_len, D_A), BF16), sds((N_KV_A, s_len, 128), BF16),
                   sds((s_len, D_B), BF16), sds((s_len, D_B), BF16), sds((s_len, D_B), BF16)],
        compiler_params=_cparams(("parallel",)),
    )(x, g1, w_in, cc, ss, gq2, gk2, ones128)


def _stat_spec(tm):
    return pl.BlockSpec((N_HEADS_B, tm, 1), lambda i: (0, i, 0))


def _merge_b(outs, lses):
    s_len = outs[0].shape[0]
    tm = min(ROW_TILE, s_len)

    def body(o0, o1, o2, l0, l1, l2, yb_ref, lse_ref):
        m_all = jnp.maximum(jnp.maximum(l0[...], l1[...]), l2[...])
        w = [jnp.exp(l[...] - m_all) for l in (l0, l1, l2)]
        den = w[0] + w[1] + w[2]
        yb_ref[...] = (w[0] * o0[...] + w[1] * o1[...] + w[2] * o2[...]) / den
        lse_ref[...] = m_all + jnp.log(den)

    return pl.pallas_call(
        body, name="merge_b", grid=(s_len // tm,),
        in_specs=[_rows(tm, D_B)] * 6,
        out_specs=[_rows(tm, D_B), _rows(tm, D_B)],
        out_shape=[jax.ShapeDtypeStruct((s_len, D_B), F32), jax.ShapeDtypeStruct((s_len, D_B), F32)],
        compiler_params=_cparams(("parallel",)),
    )(*outs, *lses)


def _out_proj(ya, yb, x, g_a, g_b, w_out, g_post, g_mlp_pre):
    s_len = x.shape[0]
    tm = min(ROW_TILE, s_len)

    def body(ya_ref, yb_ref, x_ref, ga_ref, gb_ref, w_ref, gp_ref, gm_ref, ycat_ref, y2_ref, h1_ref, xn2_ref):
        ah, _ = _rms_stats(ya_ref[...])
        bh, _ = _rms_stats(yb_ref[...])
        ycat = jnp.concatenate([ah * ga_ref[...], bh * gb_ref[...]], axis=-1).astype(BF16)
        ycat_ref[...] = ycat
        y2 = _mm(ycat, w_ref[...])
        y2_ref[...] = y2
        y2h, _ = _rms_stats(y2)
        h1 = x_ref[...] + y2h * gp_ref[...]
        h1_ref[...] = h1
        h1h, _ = _rms_stats(h1)
        xn2_ref[...] = (h1h * gm_ref[...]).astype(BF16)

    sds = jax.ShapeDtypeStruct
    return pl.pallas_call(
        body, name="out_proj", grid=(s_len // tm,),
        in_specs=[_rows(tm, D_A), _rows(tm, D_B), _rows(tm, D_MODEL), _full((1, D_A)), _full((1, D_B)),
                  _full((D_MODEL, D_MODEL)), _full((1, D_MODEL)), _full((1, D_MODEL))],
        out_specs=[_rows(tm, D_MODEL)] * 4,
        out_shape=[sds((s_len, D_MODEL), BF16), sds((s_len, D_MODEL), F32), sds((s_len, D_MODEL), F32),
                   sds((s_len, D_MODEL), BF16)],
        compiler_params=_cparams(("parallel",)),
    )(ya, yb, x, g_a, g_b, w_out, g_post, g_mlp_pre)


def _ff1(xn2, w_ff1):
    s_len = xn2.shape[0]
    tm = min(ROW_TILE, s_len)

    def body(x_ref, w_ref, u_ref):
        u_ref[...] = _mm(x_ref[...], w_ref[...])

    return pl.pallas_call(
        body, name="ff1", grid=(s_len // tm,),
        in_specs=[_rows(tm, D_MODEL), _full((D_MODEL, D_FF))],
        out_specs=_rows(tm, D_FF),
        out_shape=jax.ShapeDtypeStruct((s_len, D_FF), F32),
        compiler_params=_cparams(("parallel",)),
    )(xn2, w_ff1)


def _ff2(u, w_ff2, h1, g_post, g_ple):
    s_len = u.shape[0]
    tm = min(ROW_TILE, s_len)

    def body(u_ref, w_ref, h1_ref, gp_ref, gl_ref, f2_ref, h2_ref, xn3_ref):
        f = jnp.square(jnp.maximum(u_ref[...], 0.0)).astype(BF16)
        f2 = _mm(f, w_ref[...])
        f2_ref[...] = f2
        f2h, _ = _rms_stats(f2)
        h2 = h1_ref[...] + f2h * gp_ref[...]
        h2_ref[...] = h2
        h2h, _ = _rms_stats(h2)
        xn3_ref[...] = (h2h * gl_ref[...]).astype(BF16)

    sds = jax.ShapeDtypeStruct
    return pl.pallas_call(
        body, name="ff2", grid=(s_len // tm,),
        in_specs=[_rows(tm, D_FF), _full((D_FF, D_MODEL)), _rows(tm, D_MODEL), _full((1, D_MODEL)),
                  _full((1, D_MODEL))],
        out_specs=[_rows(tm, D_MODEL)] * 3,
        out_shape=[sds((s_len, D_MODEL), F32), sds((s_len, D_MODEL), F32), sds((s_len, D_MODEL), BF16)],
        compiler_params=_cparams(("parallel",)),
    )(u, w_ff2, h1, g_post, g_ple)


def _ple_loss(xn3, p, h2, f2, tgt, w_gate, w_ple, g_ple, g_mlp_post):
    s_len = h2.shape[0]
    tm = min(ROW_TILE, s_len)

    def body(xn3_ref, p_ref, h2_ref, f2_ref, t_ref, wg_ref, wp_ref, gl_ref, gp_ref,
             dh2_ref, df2_ref, dgl_ref, dpp_ref, loss_ref, dgple_ref, dgpost_ref):
        gate = jax.nn.sigmoid(_mm(xn3_ref[...], wg_ref[...]))
        pp = _mm(p_ref[...].astype(BF16), wp_ref[...])
        h2 = h2_ref[...]
        err = h2 + gate * pp - t_ref[...]
        sq = jnp.sum(jnp.sum(err * err, axis=1, keepdims=True), axis=0, keepdims=True)
        _acc_out(loss_ref, sq * (0.5 / D_MODEL))
        dh3 = err * (1.0 / D_MODEL)
        dgl = (dh3 * pp) * gate * (1.0 - gate)
        dgl_b = dgl.astype(BF16)
        dgl_ref[...] = dgl_b
        dpp_ref[...] = (dh3 * gate).astype(BF16)
        dxn3 = _mm_nt(dgl_b, wg_ref[...])
        h2h, r2 = _rms_stats(h2)
        dx, dg = _rms_bwd(dxn3, h2h, r2, gl_ref[...])
        _acc_out(dgple_ref, dg)
        dh2 = dh3 + dx
        dh2_ref[...] = dh2
        f2h, rf = _rms_stats(f2_ref[...])
        df2, dg = _rms_bwd(dh2, f2h, rf, gp_ref[...])
        _acc_out(dgpost_ref, dg)
        df2_ref[...] = df2.astype(BF16)

    sds = jax.ShapeDtypeStruct
    return pl.pallas_call(
        body, name="ple_loss", grid=(s_len // tm,),
        in_specs=[_rows(tm, D_MODEL), _rows(tm, D_PLE), _rows(tm, D_MODEL), _rows(tm, D_MODEL), _rows(tm, D_MODEL),
                  _full((D_MODEL, D_MODEL)), _full((D_PLE, D_MODEL)), _full((1, D_MODEL)), _full((1, D_MODEL))],
        out_specs=[_rows(tm, D_MODEL)] * 3 + [_rows(tm, D_MODEL), _full((1, 1)), _full((1, D_MODEL)),
                                              _full((1, D_MODEL))],
        out_shape=[sds((s_len, D_MODEL), F32), sds((s_len, D_MODEL), BF16), sds((s_len, D_MODEL), BF16),
                   sds((s_len, D_MODEL), BF16), sds((1, 1), F32), sds((1, D_MODEL), F32), sds((1, D_MODEL), F32)],
        compiler_params=_cparams(("arbitrary",)),
    )(xn3, p, h2, f2, tgt, w_gate, w_ple, g_ple, g_mlp_post)


def _ff2_bwd(df2, w_ff2, u):
    s_len = u.shape[0]
    tm = min(ROW_TILE, s_len)

    def body(d_ref, w_ref, u_ref, du_ref):
        df = _mm_nt(d_ref[...], w_ref[...])
        du_ref[...] = (df * (2.0 * jnp.maximum(u_ref[...], 0.0))).astype(BF16)

    return pl.pallas_call(
        body, name="ff2_bwd", grid=(s_len // tm,),
        in_specs=[_rows(tm, D_MODEL), _full((D_FF, D_MODEL)), _rows(tm, D_FF)],
        out_specs=_rows(tm, D_FF),
        out_shape=jax.ShapeDtypeStruct((s_len, D_FF), BF16),
        compiler_params=_cparams(("parallel",)),
    )(df2, w_ff2, u)


def _ff1_bwd(du, w_ff1, dh2, h1, y2, g_mlp_pre, g_post):
    s_len = du.shape[0]
    tm = min(ROW_TILE, s_len)

    def body(du_ref, w_ref, dh2_ref, h1_ref, y2_ref, gm_ref, gp_ref, dh1_ref, dy2_ref, dgm_ref, dgp_ref):
        dxn2 = _mm_nt(du_ref[...], w_ref[...])
        h1h, r1 = _rms_stats(h1_ref[...])
        dx, dg = _rms_bwd(dxn2, h1h, r1, gm_ref[...])
        _acc_out(dgm_ref, dg)
        dh1 = dh2_ref[...] + dx
        dh1_ref[...] = dh1
        y2h, ry = _rms_stats(y2_ref[...])
        dy2, dg = _rms_bwd(dh1, y2h, ry, gp_ref[...])
        _acc_out(dgp_ref, dg)
        dy2_ref[...] = dy2.astype(BF16)

    sds = jax.ShapeDtypeStruct
    return pl.pallas_call(
        body, name="ff1_bwd", grid=(s_len // tm,),
        in_specs=[_rows(tm, D_FF), _full((D_MODEL, D_FF)), _rows(tm, D_MODEL), _rows(tm, D_MODEL),
                  _rows(tm, D_MODEL), _full((1, D_MODEL)), _full((1, D_MODEL))],
        out_specs=[_rows(tm, D_MODEL), _rows(tm, D_MODEL), _full((1, D_MODEL)), _full((1, D_MODEL))],
        out_shape=[sds((s_len, D_MODEL), F32), sds((s_len, D_MODEL), BF16), sds((1, D_MODEL), F32),
                   sds((1, D_MODEL), F32)],
        compiler_params=_cparams(("arbitrary",)),
    )(du, w_ff1, dh2, h1, y2, g_mlp_pre, g_post)


def _out_proj_bwd(dy2, w_out, ya, yb, lse_b, g_a, g_b):
    s_len = ya.shape[0]
    tm = min(ROW_TILE, s_len)

    def body(d_ref, w_ref, ya_ref, yb_ref, lse_ref, ga_ref, gb_ref, dya_ref, dyb_ref, da_ref, st_ref, dga_ref,
             dgb_ref):
        dycat = _mm_nt(d_ref[...], w_ref[...])
        lane = lax.broadcasted_iota(jnp.int32, (tm, 128), 1)
        low = lane < HEAD_DIM
        is_lse = (lane % HEAD_DIM) < (HEAD_DIM // 2)

        def head_sums(prod_chunk):
            return (jnp.sum(jnp.where(low, prod_chunk, 0.0), axis=1, keepdims=True),
                    jnp.sum(jnp.where(low, 0.0, prod_chunk), axis=1, keepdims=True))

        ya = ya_ref[...]
        yh, r = _rms_stats(ya)
        dya, dg = _rms_bwd(dycat[:, :D_A], yh, r, ga_ref[...])
        _acc_out(dga_ref, dg)
        dya_ref[...] = dya
        prod = dya * ya
        for c in range(D_A // 128):
            da_ref[2 * c], da_ref[2 * c + 1] = head_sums(prod[:, 128 * c:128 * (c + 1)])

        yb = yb_ref[...]
        yh, r = _rms_stats(yb)
        dyb, dg = _rms_bwd(dycat[:, D_A:], yh, r, gb_ref[...])
        _acc_out(dgb_ref, dg)
        dyb_ref[...] = dyb.astype(BF16)
        prod = dyb * yb
        for c in range(D_B // 128):
            sl = slice(128 * c, 128 * (c + 1))
            d_lo, d_hi = head_sums(prod[:, sl])
            st_ref[:, sl] = jnp.where(is_lse, lse_ref[:, sl], jnp.where(low, d_lo, d_hi))

    sds = jax.ShapeDtypeStruct
    return pl.pallas_call(
        body, name="out_proj_bwd", grid=(s_len // tm,),
        in_specs=[_rows(tm, D_MODEL), _full((D_MODEL, D_MODEL)), _rows(tm, D_A), _rows(tm, D_B), _rows(tm, D_B),
                  _full((1, D_A)), _full((1, D_B))],
        out_specs=[_rows(tm, D_A), _rows(tm, D_B), _stat_spec(tm), _rows(tm, D_B), _full((1, D_A)),
                   _full((1, D_B))],
        out_shape=[sds((s_len, D_A), F32), sds((s_len, D_B), BF16), sds((N_HEADS_A, s_len, 1), F32),
                   sds((s_len, D_B), F32), sds((1, D_A), F32), sds((1, D_B), F32)],
        compiler_params=_cparams(("arbitrary",)),
    )(dy2, w_out, ya, yb, lse_b, g_a, g_b)


def _in_proj_bwd(dqr, dkv, dqb, dkb, dvb, qpre, kpre, x, dh1, g1, w_in, cc, ss, gq2, gk2, ones128):
    s_len = x.shape[0]
    tm = min(ROW_TILE // 2, s_len)

    def body(dqr_ref, dkv_ref, dq0, dq1, dq2, dk0, dk1, dk2, dv0, dv1, dv2, qpre_ref, kpre_ref, x_ref,
             dh1_ref, g_ref, w_ref, cc_ref, ss_ref, gq_ref, gk_ref, one_ref, dproj_ref, gx_ref, dg1_ref, dgq_ref,
             dgk_ref):
        low = _low_lanes(tm)
        dkr = jnp.where(low, dkv_ref[0], pltpu.roll(dkv_ref[1], HEAD_DIM, 1))
        dva = jnp.where(low, pltpu.roll(dkv_ref[0], HEAD_DIM, 1), dkv_ref[1])
        first_half = _first_half_mask((tm, 128))
        ones = one_ref[...]
        cc_t, ss_t = cc_ref[...], ss_ref[...]

        def norm_rope_bwd(dy, xc, gain):
            dn = dy * cc_t - _swap_halves(dy, first_half) * ss_t
            r = lax.rsqrt(_xdot(xc * xc, ones) * (1.0 / HEAD_DIM) + EPS)
            xh = xc * r
            gdy = dn * gain
            dx = r * (gdy - xh * (_xdot(gdy * xh, ones) * (1.0 / HEAD_DIM)))
            return dx, jnp.sum(dn * xh, axis=0, keepdims=True)

        dgq = jnp.zeros((1, 128), F32)
        parts = []
        for c in range(D_A // 128):
            sl = slice(128 * c, 128 * (c + 1))
            dx, dg = norm_rope_bwd(dqr_ref[:, sl] * Q_SCALE, qpre_ref[:, sl], gq_ref[...])
            parts.append(dx)
            dgq = dgq + dg
        dxk, dgk = norm_rope_bwd(dkr, kpre_ref[...], gk_ref[...])
        _acc_out(dgq_ref, dgq)
        _acc_out(dgk_ref, dgk)
        parts += [dxk, dva, (dq0[...] + dq1[...] + dq2[...]) * Q_SCALE, dk0[...] + dk1[...] + dk2[...],
                  dv0[...] + dv1[...] + dv2[...]]
        dproj = jnp.concatenate(parts, axis=-1).astype(BF16)
        dproj_ref[...] = dproj
        dxn = _mm_nt(dproj, w_ref[...])
        xh, r = _rms_stats(x_ref[...])
        dx, dg = _rms_bwd(dxn, xh, r, g_ref[...])
        _acc_out(dg1_ref, dg)
        gx_ref[...] = dh1_ref[...] + dx

    sds = jax.ShapeDtypeStruct
    return pl.pallas_call(
        body, name="in_proj_bwd", grid=(s_len // tm,),
        in_specs=[_rows(tm, D_A), pl.BlockSpec((N_KV_A, tm, 128), lambda i: (0, i, 0))] + [_rows(tm, D_B)] * 9
                 + [_rows(tm, D_A), _rows(tm, D_KV_A), _rows(tm, D_MODEL), _rows(tm, D_MODEL),
                    _full((1, D_MODEL)), _full((D_MODEL, D_IN)), _rows(tm, 128), _rows(tm, 128), _full((1, 128)),
                    _full((1, 128)), _full((128, 128))],
        out_specs=[_rows(tm, D_IN), _rows(tm, D_MODEL), _full((1, D_MODEL)), _full((1, 128)), _full((1, 128))],
        out_shape=[sds((s_len, D_IN), BF16), sds((s_len, D_MODEL), F32), sds((1, D_MODEL), F32),
                   sds((1, 128), F32), sds((1, 128), F32)],
        compiler_params=_cparams(("arbitrary",)),
    )(dqr, dkv, *dqb, *dkb, *dvb, qpre, kpre, x, dh1, g1, w_in, cc, ss, gq2, gk2, ones128)


def _dw(a, b, name, relu2=False):
    s_len, ka = a.shape
    n = b.shape[1]
    ts = min(DW_TS, s_len)
    bk = min(ka, 1024)
    bn = n if n % 1024 else 1024

    def body(a_ref, b_ref, o_ref):
        @pl.when(pl.program_id(2) == 0)
        def _():
            o_ref[...] = jnp.zeros_like(o_ref)

        av = a_ref[...]
        if relu2:
            av = jnp.square(jnp.maximum(av, 0.0))
        o_ref[...] += _mm_tn(av.astype(BF16), b_ref[...])

    return pl.pallas_call(
        body, name=name, grid=(ka // bk, n // bn, s_len // ts),
        in_specs=[pl.BlockSpec((ts, bk), lambda i, j, k: (k, i)), pl.BlockSpec((ts, bn), lambda i, j, k: (k, j))],
        out_specs=pl.BlockSpec((bk, bn), lambda i, j, k: (i, j)),
        out_shape=jax.ShapeDtypeStruct((ka, n), F32),
        compiler_params=_cparams(("parallel", "parallel", "arbitrary")),
    )(a, b)


def _stack_heads(block, low, data_low):
    parts = []
    for c in range(GROUP_A // 2):
        chunk = block[:, 128 * c:128 * (c + 1)]
        swapped = pltpu.roll(chunk, HEAD_DIM, 1)
        for h_low in (chunk, swapped) if data_low else (swapped, chunk):
            parts.append(jnp.where(low, h_low, 0.0) if data_low else jnp.where(low, 0.0, h_low))
    return jnp.concatenate(parts, axis=0).astype(BF16)


def _unstack_heads(stacked, low, tq, data_low):
    chunks = []
    for c in range(GROUP_A // 2):
        even = stacked[2 * c * tq:(2 * c + 1) * tq]
        odd = stacked[(2 * c + 1) * tq:(2 * c + 2) * tq]
        if data_low:
            chunks.append(jnp.where(low, even, pltpu.roll(odd, HEAD_DIM, 1)))
        else:
            chunks.append(jnp.where(low, pltpu.roll(even, HEAD_DIM, 1), odd))
    return chunks


def _attn_a_fwd(qa, kv):
    s_len = kv.shape[1]
    tq = min(ATT_TQ, s_len)
    tk = min(ATT_TK_FWD, s_len)
    rows = GROUP_A * tq

    def body(q_ref, kv_ref, o_ref, lse_ref):
        low = _low_lanes(tq)
        low_k = _low_lanes(tk)
        q = _stack_heads(q_ref[...].astype(F32), low, data_low=True)

        def step(j, carry):
            m, acc = carry
            kvj = kv_ref[0, pl.ds(pl.multiple_of(j * tk, tk), tk), :]
            s = _mm_nt(q, kvj)
            m_new = jnp.maximum(m, jnp.max(s, axis=1, keepdims=True))
            p = jnp.exp(s - m_new).astype(BF16)
            acc = jnp.exp(m - m_new) * acc + _mm(p, jnp.where(low_k, jnp.ones_like(kvj), kvj))
            return m_new, acc

        init = (jnp.full((rows, 1), -jnp.inf, F32), jnp.zeros((rows, 128), F32))
        m, acc = lax.fori_loop(0, s_len // tk, step, init)
        for c, chunk in enumerate(_unstack_heads(acc / pltpu.roll(acc, HEAD_DIM, 1), low, tq, data_low=False)):
            o_ref[:, 128 * c:128 * (c + 1)] = chunk
        lse_ref[...] = (m + jnp.log(acc[:, :1])).reshape(GROUP_A, tq, 1)

    return pl.pallas_call(
        body, name="attn_a_fwd", grid=(N_KV_A, s_len // tq),
        in_specs=[pl.BlockSpec((tq, 256), lambda g, i: (i, g)),
                  pl.BlockSpec((1, s_len, 128), lambda g, i: (g, 0, 0))],
        out_specs=[pl.BlockSpec((tq, 256), lambda g, i: (i, g)),
                   pl.BlockSpec((GROUP_A, tq, 1), lambda g, i: (g, i, 0))],
        out_shape=[jax.ShapeDtypeStruct((s_len, D_A), F32),
                   jax.ShapeDtypeStruct((N_HEADS_A, s_len, 1), F32)],
        compiler_params=_cparams(("parallel", "parallel")),
    )(qa, kv)


def _attn_a_bwd(qa, dya, kv, lse, delta):
    s_len = kv.shape[1]
    tq = min(ATT_TQ_BWD, s_len)
    tk = min(ATT_TK_BWD, s_len)
    rows = GROUP_A * tq

    def body(q_ref, do_ref, kv_ref, lse_ref, dl_ref, dq_ref, dkv_ref):
        @pl.when(pl.program_id(1) == 0)
        def _():
            dkv_ref[...] = jnp.zeros_like(dkv_ref)

        low = _low_lanes(tq)
        q = _stack_heads(q_ref[...].astype(F32), low, data_low=True)
        do = _stack_heads(do_ref[...], low, data_low=False)
        lse_t = lse_ref[...].reshape(rows, 1)
        dl_t = dl_ref[...].reshape(rows, 1)
        q_t = q.astype(F32).T.astype(BF16)
        do_t = do.astype(F32).T.astype(BF16)

        def step(j, dq):
            span = pl.ds(pl.multiple_of(j * tk, tk), tk)
            kvj = kv_ref[0, span, :]
            p = jnp.exp(_mm_nt(q, kvj) - lse_t)
            ds = (p * (_mm_nt(do, kvj) - dl_t)).astype(BF16)
            dkv_ref[0, :, span] += _mm(q_t, ds) + _mm(do_t, p.astype(BF16))
            return dq + _mm(ds, kvj)

        dq = lax.fori_loop(0, s_len // tk, step, jnp.zeros((rows, 128), F32))
        for c, chunk in enumerate(_unstack_heads(dq, low, tq, data_low=True)):
            dq_ref[:, 128 * c:128 * (c + 1)] = chunk

    return pl.pallas_call(
        body, name="attn_a_bwd", grid=(N_KV_A, s_len // tq),
        in_specs=[pl.BlockSpec((tq, 256), lambda g, i: (i, g)),
                  pl.BlockSpec((tq, 256), lambda g, i: (i, g)),
                  pl.BlockSpec((1, s_len, 128), lambda g, i: (g, 0, 0)),
                  pl.BlockSpec((GROUP_A, tq, 1), lambda g, i: (g, i, 0)),
                  pl.BlockSpec((GROUP_A, tq, 1), lambda g, i: (g, i, 0))],
        out_specs=[pl.BlockSpec((tq, 256), lambda g, i: (i, g)),
                   pl.BlockSpec((1, 128, s_len), lambda g, i: (g, 0, 0))],
        out_shape=[jax.ShapeDtypeStruct((s_len, D_A), F32),
                   jax.ShapeDtypeStruct((N_KV_A, 128, s_len), F32)],
        compiler_params=_cparams(("parallel", "arbitrary")),
    )(qa, dya, kv, lse, delta)


class _SwaGeometry:
    def __init__(self, s_len, r):
        self.r = r
        self.tq = SWA_TQ
        self.block = min(max(SWA_MIN_BLOCK, 2 * SWA_TQ * r), s_len)
        self.halo = HALF_WIN * r
        self.nsub = self.block // (self.tq * r)
        self.band = self.tq + 2 * HALF_WIN
        self.length = s_len // r
        self.nblk = s_len // self.block
        self.nhalo = s_len // self.halo
        assert self.nsub * self.tq * r == self.block and self.block % self.halo == 0

    def specs(self):
        per = self.block // self.halo
        cur = pl.BlockSpec((self.block, 128), lambda c, i: (i, c))
        prev = pl.BlockSpec((self.halo, 128), lambda c, i: (jnp.maximum(i * per - 1, 0), c))
        nxt = pl.BlockSpec((self.halo, 128), lambda c, i: (jnp.minimum((i + 1) * per, self.nhalo - 1), c))
        return prev, cur, nxt

    def tiles(self):
        return [(rho + self.r * j * self.tq, self.halo + rho + self.r * (j * self.tq - HALF_WIN), j)
                for j in range(self.nsub) for rho in range(self.r)]

    def own(self, start):
        return pl.ds(start, self.tq, stride=self.r)

    def around(self, start):
        return pl.ds(start, self.band, stride=self.r)

    def fill(self, dst, prev_ref, cur_ref, next_ref):
        dst[:self.halo, :] = prev_ref[...].astype(F32)
        dst[self.halo:self.halo + self.block, :] = cur_ref[...].astype(F32)
        dst[self.halo + self.block:, :] = next_ref[...].astype(F32)

    def first_position(self, j):
        return (pl.program_id(1) * self.block) // self.r + j * self.tq

    def extended(self):
        return pltpu.VMEM((self.block + 2 * self.halo, 128), F32)

    def plain(self):
        return pltpu.VMEM((self.block, 128), F32)


def _low_lanes(rows):
    return lax.broadcasted_iota(jnp.int32, (rows, 128), 1) < HEAD_DIM


def _one_head(x, low, half):
    return jnp.where(low if half == 0 else jnp.logical_not(low), x, 0.0).astype(BF16)


def _swa_fwd(q, k, v, bias, r):
    geo = _SwaGeometry(q.shape[0], r)
    prev, cur, nxt = geo.specs()

    def body(q_ref, kp, kc, kn, vp, vc, vn, b_ref, o_ref, lse_ref, qf, kf, vf):
        qf[...] = q_ref[...].astype(F32)
        geo.fill(kf, kp, kc, kn)
        geo.fill(vf, vp, vc, vn)
        low_q, low_b = _low_lanes(geo.tq), _low_lanes(geo.band)
        row = lax.broadcasted_iota(jnp.int32, (geo.tq, geo.band), 0)
        col = lax.broadcasted_iota(jnp.int32, (geo.tq, geo.band), 1)
        in_window = jnp.abs(col - HALF_WIN - row) <= HALF_WIN
        for own, around, j in geo.tiles():
            key = geo.first_position(j) - HALF_WIN + col
            valid = in_window & (key >= 0) & (key < geo.length)
            qs = qf[geo.own(own), :]
            kb = kf[geo.around(around), :].astype(BF16)
            vb = vf[geo.around(around), :]
            outs, lses = [], []
            for half in range(2):
                s = jnp.where(valid, _mm_nt(_one_head(qs, low_q, half), kb) + b_ref[half], NEG_BIG)
                m = jnp.max(s, axis=1, keepdims=True)
                e = jnp.exp(s - m)
                l = jnp.sum(e, axis=1, keepdims=True)
                outs.append(_mm(e.astype(BF16), _one_head(vb, low_b, half)) / l)
                lses.append(m + jnp.log(l))
            o_ref[geo.own(own), :] = outs[0] + outs[1]
            lse_ref[geo.own(own), :] = jnp.where(low_q, lses[0], lses[1])

    sds = jax.ShapeDtypeStruct
    return pl.pallas_call(
        body, name="swa_fwd_%d" % r, grid=(D_B // 128, geo.nblk),
        in_specs=[cur, prev, cur, nxt, prev, cur, nxt, pl.BlockSpec((2, geo.tq, geo.band), lambda c, i: (c, 0, 0))],
        out_specs=[cur, cur],
        out_shape=[sds(q.shape, F32), sds(q.shape, F32)],
        scratch_shapes=[geo.plain(), geo.extended(), geo.extended()],
        compiler_params=_cparams(("parallel", "parallel")),
    )(q, k, k, k, v, v, v, bias)


def _head_stats(st, half):
    lo = HEAD_DIM * half
    return st[:, lo:lo + 1], st[:, lo + HEAD_DIM // 2:lo + HEAD_DIM // 2 + 1]


def _swa_bwd_q(q, k, v, dy, st, bias, r):
    geo = _SwaGeometry(q.shape[0], r)
    prev, cur, nxt = geo.specs()
    bias_spec = pl.BlockSpec((2, geo.tq, geo.band), lambda c, i: (c, 0, 0))

    def body(q_ref, kp, kc, kn, vp, vc, vn, dy_ref, st_ref, b_ref, dq_ref, db_ref, qf, kf, vf, dyf):
        @pl.when(pl.program_id(1) == 0)
        def _():
            db_ref[...] = jnp.zeros_like(db_ref)

        qf[...] = q_ref[...].astype(F32)
        dyf[...] = dy_ref[...].astype(F32)
        geo.fill(kf, kp, kc, kn)
        geo.fill(vf, vp, vc, vn)
        low_q, low_b = _low_lanes(geo.tq), _low_lanes(geo.band)
        row = lax.broadcasted_iota(jnp.int32, (geo.tq, geo.band), 0)
        col = lax.broadcasted_iota(jnp.int32, (geo.tq, geo.band), 1)
        in_window = jnp.abs(col - HALF_WIN - row) <= HALF_WIN
        for own, around, j in geo.tiles():
            key = geo.first_position(j) - HALF_WIN + col
            valid = in_window & (key >= 0) & (key < geo.length)
            qs = qf[geo.own(own), :]
            dys = dyf[geo.own(own), :]
            sts = st_ref[geo.own(own), :]
            kb = kf[geo.around(around), :]
            vb = vf[geo.around(around), :].astype(BF16)
            dq = jnp.zeros((geo.tq, 128), F32)
            for half in range(2):
                lse, delta = _head_stats(sts, half)
                s = jnp.where(valid, _mm_nt(_one_head(qs, low_q, half), kb.astype(BF16)) + b_ref[half], NEG_BIG)
                p = jnp.exp(s - lse)
                ds = p * (_mm_nt(_one_head(dys, low_q, half), vb) - delta)
                db_ref[half] += ds
                dq = dq + _mm(ds.astype(BF16), _one_head(kb, low_b, half))
            dq_ref[geo.own(own), :] = dq

    return pl.pallas_call(
        body, name="swa_bwd_q_%d" % r, grid=(D_B // 128, geo.nblk),
        in_specs=[cur, prev, cur, nxt, prev, cur, nxt, cur, cur, bias_spec],
        out_specs=[cur, bias_spec],
        out_shape=[jax.ShapeDtypeStruct(q.shape, F32), jax.ShapeDtypeStruct(bias.shape, F32)],
        scratch_shapes=[geo.plain(), geo.extended(), geo.extended(), geo.plain()],
        compiler_params=_cparams(("parallel", "arbitrary")),
    )(q, k, k, k, v, v, v, dy, st, bias)


def _swa_bwd_kv(q, k, v, dy, st, bias_kv, r):
    geo = _SwaGeometry(q.shape[0], r)
    prev, cur, nxt = geo.specs()

    def body(k_ref, v_ref, qp, qc, qn, dp_, dc_, dn_, sp, sc, sn, b_ref, dk_ref, dv_ref, kf, vf, qf, dyf, stf):
        kf[...] = k_ref[...].astype(F32)
        vf[...] = v_ref[...].astype(F32)
        geo.fill(qf, qp, qc, qn)
        geo.fill(dyf, dp_, dc_, dn_)
        geo.fill(stf, sp, sc, sn)
        low_b = _low_lanes(geo.band)
        row = lax.broadcasted_iota(jnp.int32, (geo.tq, geo.band), 0)
        col = lax.broadcasted_iota(jnp.int32, (geo.tq, geo.band), 1)
        in_window = jnp.abs(row + HALF_WIN - col) <= HALF_WIN
        for own, around, j in geo.tiles():
            query = geo.first_position(j) - HALF_WIN + col
            valid = in_window & (query >= 0) & (query < geo.length)
            ks = kf[geo.own(own), :].astype(BF16)
            vs = vf[geo.own(own), :].astype(BF16)
            qb = qf[geo.around(around), :]
            dyb = dyf[geo.around(around), :]
            st_t = stf[geo.around(around), :].T
            dk = jnp.zeros((geo.tq, 128), F32)
            dv = jnp.zeros((geo.tq, 128), F32)
            for half in range(2):
                lo = HEAD_DIM * half
                lse, delta = st_t[lo:lo + 1, :], st_t[lo + HEAD_DIM // 2:lo + HEAD_DIM // 2 + 1, :]
                q_h, dy_h = _one_head(qb, low_b, half), _one_head(dyb, low_b, half)
                s = jnp.where(valid, _mm_nt(ks, q_h) + b_ref[half], NEG_BIG)
                p = jnp.exp(s - lse)
                ds = p * (_mm_nt(vs, dy_h) - delta)
                dv = dv + _mm(p.astype(BF16), dy_h)
                dk = dk + _mm(ds.astype(BF16), q_h)
            dk_ref[geo.own(own), :] = dk
            dv_ref[geo.own(own), :] = dv

    return pl.pallas_call(
        body, name="swa_bwd_kv_%d" % r, grid=(D_B // 128, geo.nblk),
        in_specs=[cur, cur, prev, cur, nxt, prev, cur, nxt, prev, cur, nxt,
                  pl.BlockSpec((2, geo.tq, geo.band), lambda c, i: (c, 0, 0))],
        out_specs=[cur, cur],
        out_shape=[jax.ShapeDtypeStruct(q.shape, F32), jax.ShapeDtypeStruct(q.shape, F32)],
        scratch_shapes=[geo.plain(), geo.plain(), geo.extended(), geo.extended(), geo.extended()],
        compiler_params=_cparams(("parallel", "parallel")),
    )(k, v, q, q, q, dy, dy, dy, st, st, st, bias_kv)


BIAS_ROWS = 16
BIAS_TN = 4096


def _bias_tiles(onehot, rel_bias_t):
    n = onehot.shape[0]

    def body(oh_ref, rb_ref, o_ref):
        o_ref[...] = sum(_mm_nt(piece, oh_ref[...]) for piece in _split3(rb_ref[...]))

    return pl.pallas_call(
        body, name="bias_tiles", grid=(n // BIAS_TN,),
        in_specs=[_rows(BIAS_TN, 128), _full((BIAS_ROWS, 128))],
        out_specs=pl.BlockSpec((BIAS_ROWS, BIAS_TN), lambda i: (0, i)),
        out_shape=jax.ShapeDtypeStruct((BIAS_ROWS, n), F32),
        compiler_params=_cparams(("parallel",)),
    )(onehot, rel_bias_t)


def _bias_bwd(onehot, dbias_rows, so_far, r):
    n = onehot.shape[0]

    def body(oh, d, prev_ref, g_ref):
        @pl.when(pl.program_id(0) == 0)
        def _():
            g_ref[...] = prev_ref[...]

        hi, lo, _ = _split3(d[...])
        g_ref[...] += _mm(hi, oh[...]) + _mm(lo, oh[...])

    return pl.pallas_call(
        body, name="bias_bwd_%d" % r, grid=(n // BIAS_TN,),
        in_specs=[_rows(BIAS_TN, 128), pl.BlockSpec((BIAS_ROWS, BIAS_TN), lambda i: (0, i)), _full((BIAS_ROWS, 128))],
        out_specs=_full((BIAS_ROWS, 128)),
        out_shape=jax.ShapeDtypeStruct((BIAS_ROWS, 128), F32),
        compiler_params=_cparams(("arbitrary",)),
    )(onehot, dbias_rows, so_far)


def _local_step(x, p, tgt, w_in, w_out, w_ff1, w_ff2, w_gate, w_ple, g_attn_pre, g_q, g_k, g_out_a, g_out_b,
                g_attn_post, rel_bias, g_mlp_pre, g_mlp_post, g_ple):
    s_len = x.shape[0]
    cc, ss = _rope_tables(s_len)
    gq2 = jnp.concatenate([g_q, g_q], axis=-1)
    gk2 = jnp.concatenate([g_k, g_k], axis=-1)
    ones128 = _group_ones(128)
    rel_bias_t = jnp.zeros((BIAS_ROWS, 128), F32).at[:N_HEADS_B, :N_BUCKETS].set(rel_bias.T)

    xn1, qpre, kpre, qa, kv, qb, kb, vb = _in_proj(x, g_attn_pre, w_in, cc, ss, gq2, gk2, ones128)
    ya, lse_a = _attn_a_fwd(qa, kv)

    tiles, outs, lses = [], [], []
    for r in DILATIONS:
        tq = SWA_TQ
        onehot = _bucket_onehot(tq, r)
        bias = _bias_tiles(onehot, rel_bias_t)[:N_HEADS_B].reshape(N_HEADS_B, tq, tq + 2 * HALF_WIN)
        o_r, lse_r = _swa_fwd(qb, kb, vb, bias, r)
        tiles.append((onehot, bias))
        outs.append(o_r)
        lses.append(lse_r)
    yb, lse_b = _merge_b(outs, lses)

    ycat, y2, h1, xn2 = _out_proj(ya, yb, x, g_out_a, g_out_b, w_out, g_attn_post, g_mlp_pre)
    u = _ff1(xn2, w_ff1)
    f2, h2, xn3 = _ff2(u, w_ff2, h1, g_mlp_post, g_ple)
    dh2, df2, dgl, dpp, loss, dg_ple, dg_mlp_post = _ple_loss(xn3, p, h2, f2, tgt, w_gate, w_ple, g_ple, g_mlp_post)

    grads = {"g_ple": dg_ple, "g_mlp_post": dg_mlp_post}
    grads["w_ple_gate"] = _dw(xn3, dgl, "dw_gate")
    grads["w_ple_proj"] = _dw(p, dpp, "dw_ple")
    grads["w_ff2"] = _dw(u, df2, "dw_ff2", relu2=True)
    du = _ff2_bwd(df2, w_ff2, u)
    grads["w_ff1"] = _dw(xn2, du, "dw_ff1")
    dh1, dy2, grads["g_mlp_pre"], grads["g_attn_post"] = _ff1_bwd(du, w_ff1, dh2, h1, y2, g_mlp_pre, g_attn_post)
    grads["w_out"] = _dw(ycat, dy2, "dw_out")
    dya, dyb, delta_a, st_b, grads["g_out_a"], grads["g_out_b"] = _out_proj_bwd(dy2, w_out, ya, yb, lse_b, g_out_a,
                                                                              g_out_b)

    dqr, dkv_t = _attn_a_bwd(qa, dya, kv, lse_a, delta_a)
    dkv_a = dkv_t.transpose(0, 2, 1)

    dqs, dks, dvs = [], [], []
    d_rel = jnp.zeros((BIAS_ROWS, 128), F32)
    for r, (onehot, bias) in zip(DILATIONS, tiles):
        dq_r, dbias = _swa_bwd_q(qb, kb, vb, dyb, st_b, bias, r)
        bias_kv = jnp.flip(bias, axis=(1, 2))
        dk_r, dv_r = _swa_bwd_kv(qb, kb, vb, dyb, st_b, bias_kv, r)
        dbias_rows = jnp.pad(dbias.reshape(N_HEADS_B, -1), ((0, BIAS_ROWS - N_HEADS_B), (0, 0)))
        d_rel = _bias_bwd(onehot, dbias_rows, d_rel, r)
        dqs.append(dq_r)
        dks.append(dk_r)
        dvs.append(dv_r)
    grads["rel_bias"] = d_rel[:N_HEADS_B, :N_BUCKETS].T

    dproj, grad_x, grads["g_attn_pre"], dgq2, dgk2 = _in_proj_bwd(
        dqr, dkv_a, dqs, dks, dvs, qpre, kpre, x, dh1, g_attn_pre, w_in, cc, ss, gq2, gk2, ones128)
    grads["g_q"] = dgq2[:, :HEAD_DIM] + dgq2[:, HEAD_DIM:]
    grads["g_k"] = dgk2[:, :HEAD_DIM] + dgk2[:, HEAD_DIM:]
    grads["w_in"] = _dw(xn1, dproj, "dw_in")
    return loss, grad_x, grads


ANY = pl.BlockSpec(memory_space=pl.ANY)


def _position():
    return lax.axis_index("x"), lax.axis_index("y"), lax.axis_index("c")


def _other_chips(x, y):
    return [(2 * (1 - x) + y, (1 - x, y)), (2 * x + (1 - y), (x, 1 - y)), (2 * (1 - x) + (1 - y), (1 - x, 1 - y))]


def _cast_shards(shards):
    def body(*refs):
        n = len(refs) // 2
        for i_ref, o_ref in zip(refs[:n], refs[n:]):
            o_ref[...] = i_ref[...].astype(BF16)

    return pl.pallas_call(
        body, name="cast_shards",
        in_specs=[pl.BlockSpec(memory_space=pltpu.VMEM)] * len(shards),
        out_specs=[pl.BlockSpec(memory_space=pltpu.VMEM)] * len(shards),
        out_shape=[jax.ShapeDtypeStruct(s.shape, BF16) for s in shards],
        compiler_params=_cparams(),
    )(*shards)


def _gather_weights(shards):
    n = len(shards)

    def body(*refs):
        ins, outs = refs[:n], refs[n:2 * n]
        send_sems, recv_sems, pass_send_sems, pass_recv_sems = refs[2 * n:]
        x, y, c = _position()
        me = 2 * x + y
        sibling = (x, y, 1 - c)

        def rows(a, core):
            half = ins[a].shape[0] // 2
            return pl.ds(pl.multiple_of(core * half, 16), half)

        sends = []
        for k, (_, chip) in enumerate(_other_chips(x, y)):
            for a in range(n):
                cp = pltpu.make_async_remote_copy(ins[a].at[rows(a, c), :], outs[a].at[me, rows(a, c), :],
                                                  send_sems.at[k, a], recv_sems.at[k, a], device_id=(*chip, c),
                                                  device_id_type=MESH)
                cp.start()
                sends.append(cp)
        for k, (num, chip) in enumerate(_other_chips(x, y)):
            for a in range(n):
                landed = outs[a].at[num, rows(a, c), :]
                pltpu.make_async_remote_copy(landed, landed, send_sems.at[k, a], recv_sems.at[k, a],
                                             device_id=(*chip, c), device_id_type=MESH).wait_recv()
                cp = pltpu.make_async_remote_copy(landed, landed, pass_send_sems.at[k, a], pass_recv_sems.at[k, a],
                                                  device_id=sibling, device_id_type=MESH)
                cp.start()
                sends.append(cp)
        for k, (num, _) in enumerate(_other_chips(x, y)):
            for a in range(n):
                passed = outs[a].at[num, rows(a, 1 - c), :]
                pltpu.make_async_remote_copy(passed, passed, pass_send_sems.at[k, a], pass_recv_sems.at[k, a],
                                             device_id=sibling, device_id_type=MESH).wait_recv()
        for cp in sends:
            cp.wait_send()

    return pl.pallas_call(
        body, name="gather_weights",
        in_specs=[ANY] * n, out_specs=[ANY] * n,
        out_shape=[jax.ShapeDtypeStruct((N_CHIPS,) + s.shape, s.dtype) for s in shards],
        scratch_shapes=[pltpu.SemaphoreType.DMA((3, n))] * 4,
    )(*shards)


def _send_sibling_half(grads):
    n = len(grads)

    def body(*refs):
        ins, outs = refs[:n], refs[n:2 * n]
        send_sems, recv_sems = refs[2 * n:]
        x, y, c = _position()
        copies = []
        for a in range(n):
            half = ins[a].shape[1] // 2
            theirs = ins[a].at[:, pl.ds(pl.multiple_of((1 - c) * half, 8), half), :]
            cp = pltpu.make_async_remote_copy(theirs, outs[a], send_sems.at[a], recv_sems.at[a],
                                              device_id=(x, y, 1 - c), device_id_type=MESH)
            cp.start()
            copies.append(cp)
        for cp in copies:
            cp.wait()

    return pl.pallas_call(
        body, name="send_sibling_half",
        in_specs=[ANY] * n, out_specs=[ANY] * n,
        out_shape=[jax.ShapeDtypeStruct((g.shape[0], g.shape[1] // 2, g.shape[2]), g.dtype) for g in grads],
        scratch_shapes=[pltpu.SemaphoreType.DMA((n,)), pltpu.SemaphoreType.DMA((n,))],
    )(*grads)


def _scatter_to_chips(pairs):
    n = len(pairs)

    def body(*refs):
        ins, outs = refs[:n], refs[n:2 * n]
        send_sems, recv_sems = refs[2 * n:]
        x, y, c = _position()
        me = 2 * x + y
        sends = []
        for k, (num, chip) in enumerate(_other_chips(x, y)):
            for a in range(n):
                cp = pltpu.make_async_remote_copy(ins[a].at[num], outs[a].at[me], send_sems.at[k, a],
                                                  recv_sems.at[k, a], device_id=(*chip, c), device_id_type=MESH)
                cp.start()
                sends.append(cp)
        for k, (num, chip) in enumerate(_other_chips(x, y)):
            for a in range(n):
                pltpu.make_async_remote_copy(ins[a].at[me], outs[a].at[num], send_sems.at[k, a], recv_sems.at[k, a],
                                             device_id=(*chip, c), device_id_type=MESH).wait_recv()
        for cp in sends:
            cp.wait_send()

    return pl.pallas_call(
        body, name="scatter_to_chips",
        in_specs=[ANY] * n, out_specs=[ANY] * n,
        out_shape=[jax.ShapeDtypeStruct(g.shape, g.dtype) for g in pairs],
        scratch_shapes=[pltpu.SemaphoreType.DMA((3, n)), pltpu.SemaphoreType.DMA((3, n))],
    )(*pairs)


def _exchange_halves(halves):
    n = len(halves)

    def body(*refs):
        ins, outs = refs[:n], refs[n:2 * n]
        send_sems, recv_sems = refs[2 * n:]
        x, y, c = _position()
        copies = []
        for a in range(n):
            cp = pltpu.make_async_remote_copy(ins[a], outs[a], send_sems.at[a], recv_sems.at[a],
                                              device_id=(x, y, 1 - c), device_id_type=MESH)
            cp.start()
            copies.append(cp)
        for cp in copies:
            cp.wait()

    return pl.pallas_call(
        body, name="exchange_halves",
        in_specs=[ANY] * n, out_specs=[ANY] * n,
        out_shape=[jax.ShapeDtypeStruct(h.shape, h.dtype) for h in halves],
        scratch_shapes=[pltpu.SemaphoreType.DMA((n,)), pltpu.SemaphoreType.DMA((n,))],
    )(*halves)


def _allreduce_small(v):
    def body(v_ref, o_ref, buf, send_sems, recv_sems):
        x, y, c = _position()
        me = 4 * x + 2 * y + c
        peers = [(1 - x, y, c), (x, 1 - y, c), (x, y, 1 - c), (1 - x, 1 - y, c), (1 - x, y, 1 - c), (x, 1 - y, 1 - c),
                 (1 - x, 1 - y, 1 - c)]
        num = lambda d: 4 * d[0] + 2 * d[1] + d[2]
        buf[me] = v_ref[...]
        sends = []
        for k, peer in enumerate(peers):
            cp = pltpu.make_async_remote_copy(v_ref, buf.at[me], send_sems.at[k], recv_sems.at[k], device_id=peer,
                                              device_id_type=MESH)
            cp.start()
            sends.append(cp)
        for k, peer in enumerate(peers):
            pltpu.make_async_remote_copy(v_ref, buf.at[num(peer)], send_sems.at[k], recv_sems.at[k], device_id=peer,
                                         device_id_type=MESH).wait_recv()
        for cp in sends:
            cp.wait_send()
        total = buf[0]
        for d in range(1, 8):
            total = total + buf[d]
        o_ref[...] = total

    return pl.pallas_call(
        body, name="allreduce_small",
        in_specs=[pl.BlockSpec(memory_space=pltpu.VMEM)], out_specs=pl.BlockSpec(memory_space=pltpu.VMEM),
        out_shape=jax.ShapeDtypeStruct(v.shape, v.dtype),
        scratch_shapes=[pltpu.VMEM((8,) + v.shape, v.dtype), pltpu.SemaphoreType.DMA((7,)),
                        pltpu.SemaphoreType.DMA((7,))],
    )(v)


def _sum_leading(a, name):
    k, r, c = a.shape
    tr = min(r, 256)

    def body(a_ref, o_ref):
        total = a_ref[0].astype(F32)
        for i in range(1, k):
            total = total + a_ref[i].astype(F32)
        o_ref[...] = total

    return pl.pallas_call(
        body, name=name, grid=(r // tr,),
        in_specs=[pl.BlockSpec((k, tr, c), lambda i: (0, i, 0))],
        out_specs=pl.BlockSpec((tr, c), lambda i: (i, 0)),
        out_shape=jax.ShapeDtypeStruct((r, c), F32),
        compiler_params=_cparams(("parallel",)),
    )(a)


def _add(a, b, name):
    k, r, c = a.shape
    tr = min(r, 256)
    spec = pl.BlockSpec((k, tr, c), lambda i: (0, i, 0))

    def body(a_ref, b_ref, o_ref):
        o_ref[...] = (a_ref[...] + b_ref[...]).astype(BF16)

    return pl.pallas_call(
        body, name=name, grid=(r // tr,), in_specs=[spec, spec], out_specs=spec,
        out_shape=jax.ShapeDtypeStruct(a.shape, BF16), compiler_params=_cparams(("parallel",)),
    )(a, b)


def _adamw(w, g, m, v, name):
    r, c = w.shape
    tr = min(r, 256)
    spec = pl.BlockSpec((tr, c), lambda i: (i, 0))

    def body(w_ref, g_ref, m_ref, v_ref, d_ref, nm_ref, nv_ref):
        gv = g_ref[...]
        nm = ADAM_B1 * m_ref[...] + (1.0 - ADAM_B1) * gv
        nv = ADAM_B2 * v_ref[...] + (1.0 - ADAM_B2) * jnp.square(gv)
        m_hat = nm / (1.0 - ADAM_B1 ** ADAM_STEP)
        v_hat = nv / (1.0 - ADAM_B2 ** ADAM_STEP)
        d_ref[...] = -ADAM_LR * (m_hat / (jnp.sqrt(v_hat) + ADAM_EPS) + ADAM_WD * w_ref[...])
        nm_ref[...] = nm
        nv_ref[...] = nv

    return pl.pallas_call(
        body, name=name, grid=(r // tr,), in_specs=[spec] * 4, out_specs=[spec] * 3,
        out_shape=[jax.ShapeDtypeStruct(w.shape, F32)] * 3, compiler_params=_cparams(("parallel",)),
    )(w, g, m, v)


MATRICES = ("w_in", "w_out", "w_ff1", "w_ff2", "w_ple_gate", "w_ple_proj")
COLUMN_SHARDED = ("w_in", "w_ff1", "w_ple_proj")
SMALL = ("g_attn_pre", "g_q", "g_k", "g_out_a", "g_out_b", "g_attn_post", "rel_bias", "g_mlp_pre", "g_mlp_post",
         "g_ple")
WEIGHT_ORDER = ("w_in", "g_attn_pre", "g_q", "g_k", "g_out_a", "g_out_b", "w_out", "g_attn_post", "rel_bias",
                "g_mlp_pre", "w_ff1", "w_ff2", "g_mlp_post", "g_ple", "w_ple_gate", "w_ple_proj")
PACK_ROWS, PACK_COLS = 8, 1024


def _pack_small(values, extra=None):
    flat = [values[n].reshape(-1) for n in SMALL]
    used = sum(f.shape[0] for f in flat)
    tail = jnp.zeros((PACK_ROWS * PACK_COLS - used - 1,), F32)
    last = jnp.zeros((1,), F32) if extra is None else extra.reshape(1)
    return jnp.concatenate(flat + [tail, last]).reshape(PACK_ROWS, PACK_COLS)


def _unpack_small(packed, like):
    flat = packed.reshape(-1)
    out, o = {}, 0
    for n in SMALL:
        size = like[n].size
        out[n] = flat[o:o + size].reshape(like[n].shape)
        o += size
    return out, flat[-1]


def kernel(x, p, w_in, g_attn_pre, g_q, g_k, g_out_a, g_out_b, w_out, g_attn_post, rel_bias, g_mlp_pre, w_ff1, w_ff2, g_mlp_post, g_ple, w_ple_gate, w_ple_proj, loss_target, m_w_in, m_g_attn_pre, m_g_q, m_g_k, m_g_out_a, m_g_out_b, m_w_out, m_g_attn_post, m_rel_bias, m_g_mlp_pre, m_w_ff1, m_w_ff2, m_g_mlp_post, m_g_ple, m_w_ple_gate, m_w_ple_proj, v_w_in, v_g_attn_pre, v_g_q, v_g_k, v_g_out_a, v_g_out_b, v_w_out, v_g_attn_post, v_rel_bias, v_g_mlp_pre, v_w_ff1, v_w_ff2, v_g_mlp_post, v_g_ple, v_w_ple_gate, v_w_ple_proj):
    given = dict(locals())
    weights = {n: given[n] for n in WEIGHT_ORDER}
    shards = {n: weights[n][0] for n in MATRICES}

    chip = 2 * lax.axis_index("x") + lax.axis_index("y")
    c = lax.axis_index("c")
    own = _cast_shards([shards[n] for n in MATRICES])
    whole = {}
    for n, g, mine in zip(MATRICES, _gather_weights(own), own):
        g = lax.dynamic_update_slice_in_dim(g, mine[None], chip, axis=0)
        if n in COLUMN_SHARDED:
            whole[n] = g.transpose(1, 0, 2).reshape(g.shape[1], N_CHIPS * g.shape[2])
        else:
            whole[n] = g.reshape(N_CHIPS * g.shape[1], g.shape[2])

    loss, grad_x, grads = _local_step(
        x[0], p[0, 0], loss_target[0], whole["w_in"], whole["w_out"], whole["w_ff1"], whole["w_ff2"],
        whole["w_ple_gate"], whole["w_ple_proj"], g_attn_pre, g_q, g_k, g_out_a, g_out_b, g_attn_post, rel_bias,
        g_mlp_pre, g_mlp_post, g_ple)

    by_chip = []
    for n in MATRICES:
        g = grads[n]
        if n in COLUMN_SHARDED:
            by_chip.append(g.reshape(g.shape[0], N_CHIPS, g.shape[1] // N_CHIPS).transpose(1, 0, 2))
        else:
            by_chip.append(g.reshape(N_CHIPS, g.shape[0] // N_CHIPS, g.shape[1]))
    from_sibling = _send_sibling_half(by_chip)
    pairs = []
    for n, g, other in zip(MATRICES, by_chip, from_sibling):
        half = g.shape[1] // 2
        mine = lax.dynamic_slice_in_dim(g, c * half, half, axis=1)
        pairs.append(_add(mine, other, "pair_sum_" + n))
    halves = []
    for n, pair, got in zip(MATRICES, pairs, _scatter_to_chips(pairs)):
        got = lax.dynamic_update_slice_in_dim(got, lax.dynamic_slice_in_dim(pair, chip, 1, axis=0), chip, axis=0)
        halves.append(_sum_leading(got, "chip_sum_" + n))
    grad_w = {}
    for n, mine, theirs in zip(MATRICES, halves, _exchange_halves(halves)):
        half = mine.shape[0]
        g = jnp.zeros((2 * half, mine.shape[1]), F32)
        g = lax.dynamic_update_slice_in_dim(g, mine, c * half, axis=0)
        grad_w[n] = lax.dynamic_update_slice_in_dim(g, theirs, (1 - c) * half, axis=0)

    small_like = {n: weights[n] for n in SMALL}
    reduced = _allreduce_small(_pack_small({n: grads[n] for n in SMALL}, extra=loss))
    grad_small, loss_total = _unpack_small(reduced, small_like)

    delta, new_m, new_v = {}, {}, {}
    for n in MATRICES:
        d, nm, nv = _adamw(shards[n], grad_w[n], given["m_" + n][0], given["v_" + n][0], "adamw_" + n)
        delta[n], new_m[n], new_v[n] = d[None], nm[None], nv[None]
        grad_w[n] = grad_w[n][None]
    d, nm, nv = _adamw(_pack_small(small_like), reduced, _pack_small({n: given["m_" + n] for n in SMALL}),
                       _pack_small({n: given["v_" + n] for n in SMALL}), "adamw_small")
    d_small, nm_small, nv_small = (_unpack_small(a, small_like)[0] for a in (d, nm, nv))
    for n in SMALL:
        grad_w[n], delta[n], new_m[n], new_v[n] = grad_small[n], d_small[n], nm_small[n], nv_small[n]

    return (loss_total, grad_x[None], *[grad_w[n] for n in WEIGHT_ORDER], *[delta[n] for n in WEIGHT_ORDER],
            *[new_m[n] for n in WEIGHT_ORDER], *[new_v[n] for n in WEIGHT_ORDER])
```

```python
import functools
import math

import jax
import jax.numpy as jnp
from jax import lax
from jax.experimental import pallas as pl
from jax.experimental.pallas import tpu as pltpu

F32 = jnp.float32
BF16 = jnp.bfloat16

D_MODEL = 1024
HEAD_DIM = 64
N_HEADS_A = 8
N_KV_A = 2
GROUP_A = N_HEADS_A // N_KV_A
N_HEADS_B = 8
D_A = N_HEADS_A * HEAD_DIM
D_KV_A = N_KV_A * HEAD_DIM
D_B = N_HEADS_B * HEAD_DIM
D_IN = D_A + 2 * D_KV_A + 3 * D_B
D_FF = 4 * D_MODEL
D_PLE = 256
GRID_W = 64
ROPE_THETA = 10000.0
DILATIONS = (1, 4, 16)
HALF_WIN = 64
N_BUCKETS = 32
MAX_DISTANCE = 1024
EPS = 1e-6
NEG_BIG = -1e30
Q_SCALE = HEAD_DIM ** -0.5

ADAM_LR = 0.001
ADAM_B1 = 0.9
ADAM_B2 = 0.999
ADAM_EPS = 1e-08
ADAM_WD = 0.01
ADAM_STEP = 10

N_CHIPS = 4
MESH = pl.DeviceIdType.MESH

ROW_TILE = 512
ATT_TQ = 256
ATT_TQ_BWD = 512
ATT_TK_FWD = 2048
ATT_TK_BWD = 512
SWA_TQ = 128
SWA_MIN_BLOCK = 1024
DW_TS = 1024
VMEM_LIMIT = 56 * 1024 * 1024

NT = (((1,), (1,)), ((), ()))
TN = (((0,), (0,)), ((), ()))


def _cparams(sem=None, vmem=VMEM_LIMIT):
    return pltpu.CompilerParams(dimension_semantics=sem, vmem_limit_bytes=vmem)


def _full(shape):
    n = len(shape)
    return pl.BlockSpec(shape, lambda *_: (0,) * n)


def _rows(tm, width):
    return pl.BlockSpec((tm, width), lambda i: (i, 0))


def _split3(a):
    a1 = a.astype(BF16)
    r = a - a1.astype(F32)
    a2 = r.astype(BF16)
    a3 = (r - a2.astype(F32)).astype(BF16)
    return a1, a2, a3


def _xdot(a, sel):
    a1, a2, a3 = _split3(a)
    d = lambda p: jnp.dot(p, sel, preferred_element_type=F32)
    return d(a1) + d(a2) + d(a3)


def _mm(a, b):
    return jnp.dot(a, b, preferred_element_type=F32)


def _mm_nt(a, b):
    return lax.dot_general(a, b, NT, preferred_element_type=F32)


def _mm_tn(a, b):
    return lax.dot_general(a, b, TN, preferred_element_type=F32)


def _rms_stats(x):
    r = lax.rsqrt(jnp.mean(x * x, axis=-1, keepdims=True) + EPS)
    return x * r, r


def _rms_bwd(dy, xh, r, g):
    gdy = dy * g
    dx = r * (gdy - xh * jnp.mean(gdy * xh, axis=-1, keepdims=True))
    dg = jnp.sum(dy * xh, axis=0, keepdims=True)
    return dx, dg


def _acc_out(ref, val):
    @pl.when(pl.program_id(0) == 0)
    def _():
        ref[...] = jnp.zeros_like(ref)

    ref[...] += val


def _swap_halves(x, first_half):
    return jnp.where(first_half, pltpu.roll(x, 96, 1), pltpu.roll(x, 32, 1))


def _first_half_mask(shape):
    return (lax.broadcasted_iota(jnp.int32, shape, 1) % HEAD_DIM) < (HEAD_DIM // 2)


def _rope_tables(s_len):
    t = jnp.arange(s_len)
    row = (t // GRID_W).astype(F32)
    col = (t % GRID_W).astype(F32)
    n_axis = HEAD_DIM // 4
    inv_freq = ROPE_THETA ** (-jnp.arange(n_axis, dtype=F32) / n_axis)
    ang = jnp.concatenate([row[:, None] * inv_freq, col[:, None] * inv_freq], axis=-1)
    c, s = jnp.cos(ang), jnp.sin(ang)
    cc = jnp.concatenate([c, c, c, c], axis=-1)
    ss = jnp.concatenate([-s, s, -s, s], axis=-1)
    return cc, ss


def _group_ones(width):
    i = jnp.arange(width)
    return (i[:, None] // HEAD_DIM == i[None, :] // HEAD_DIM).astype(BF16)


def _t5_bucket(rel):
    nb = N_BUCKETS // 2
    max_exact = nb // 2
    side = jnp.where(rel > 0, nb, 0)
    n = jnp.abs(rel)
    large = max_exact + (jnp.log(jnp.maximum(n, max_exact).astype(F32) / max_exact)
                         / math.log(MAX_DISTANCE / max_exact) * (nb - max_exact)).astype(jnp.int32)
    large = jnp.minimum(large, nb - 1)
    return side + jnp.where(n < max_exact, n, large)


def _bucket_onehot(tq, dilation):
    qi = jnp.arange(tq)
    kj = jnp.arange(tq + 2 * HALF_WIN)
    rel = kj[None, :] - HALF_WIN - qi[:, None]
    bucket = _t5_bucket(rel * dilation).reshape(-1)
    return (bucket[:, None] == jnp.arange(128)[None, :]).astype(BF16)


def _in_proj(x, g1, w_in, cc, ss, gq2, gk2, ones128):
    s_len = x.shape[0]
    tm = min(ROW_TILE, s_len)

    def body(x_ref, g_ref, w_ref, cc_ref, ss_ref, gq_ref, gk_ref, one_ref,
             xn_ref, qpre_ref, kpre_ref, qa_ref, kv_ref, qb_ref, kb_ref, vb_ref):
        xh, _ = _rms_stats(x_ref[...])
        xn = (xh * g_ref[...]).astype(BF16)
        xn_ref[...] = xn
        proj = _mm(xn, w_ref[...])
        first_half = _first_half_mask((tm, 128))
        ones = one_ref[...]
        cc_t, ss_t = cc_ref[...], ss_ref[...]

        def norm_rope(xc, gain):
            ms = _xdot(xc * xc, ones) * (1.0 / HEAD_DIM)
            y = xc * lax.rsqrt(ms + EPS) * gain
            return y * cc_t + _swap_halves(y, first_half) * ss_t

        qpre_ref[...] = proj[:, :D_A]
        kpre_ref[...] = proj[:, D_A:D_A + D_KV_A]
        for c in range(D_A // 128):
            y = norm_rope(proj[:, 128 * c:128 * (c + 1)], gq_ref[...])
            qa_ref[:, 128 * c:128 * (c + 1)] = (y * Q_SCALE).astype(BF16)
        ka = norm_rope(proj[:, D_A:D_A + D_KV_A], gk_ref[...])
        o = D_A + D_KV_A
        va = proj[:, o:o + D_KV_A]
        low = _low_lanes(tm)
        kv_ref[0] = jnp.where(low, ka, pltpu.roll(va, HEAD_DIM, 1)).astype(BF16)
        kv_ref[1] = jnp.where(low, pltpu.roll(ka, HEAD_DIM, 1), va).astype(BF16)
        o += D_KV_A
        qb_ref[...] = (proj[:, o:o + D_B] * Q_SCALE).astype(BF16)
        kb_ref[...] = proj[:, o + D_B:o + 2 * D_B].astype(BF16)
        vb_ref[...] = proj[:, o + 2 * D_B:o + 3 * D_B].astype(BF16)

    sds = jax.ShapeDtypeStruct
    return pl.pallas_call(
        body, name="in_proj", grid=(s_len // tm,),
        in_specs=[_rows(tm, D_MODEL), _full((1, D_MODEL)), _full((D_MODEL, D_IN)), _rows(tm, 128), _rows(tm, 128),
                  _full((1, 128)), _full((1, 128)), _full((128, 128))],
        out_specs=[_rows(tm, D_MODEL), _rows(tm, D_A), _rows(tm, D_KV_A), _rows(tm, D_A),
                   pl.BlockSpec((N_KV_A, tm, 128), lambda i: (0, i, 0)), _rows(tm, D_B), _rows(tm, D_B),
                   _rows(tm, D_B)],
        out_shape=[sds((s_len, D_MODEL), BF16), sds((s_len, D_A), F32), sds((s_len, D_KV_A), F32),
                   sds((s_len, D_A), BF16), sds((N_KV_A, s_len, 128), BF16),
                   sds((s_len, D_B), BF16), sds((s_len, D_B), BF16), sds((s_len, D_B), BF16)],
        compiler_params=_cparams(("parallel",)),
    )(x, g1, w_in, cc, ss, gq2, gk2, ones128)


def _stat_spec(tm):
    return pl.BlockSpec((N_HEADS_B, tm, 1), lambda i: (0, i, 0))


def _merge_b(outs, lses):
    s_len = outs[0].shape[0]
    tm = min(ROW_TILE, s_len)

    def body(o0, o1, o2, l0, l1, l2, yb_ref, lse_ref):
        m_all = jnp.maximum(jnp.maximum(l0[...], l1[...]), l2[...])
        w = [jnp.exp(l[...] - m_all) for l in (l0, l1, l2)]
        den = w[0] + w[1] + w[2]
        yb_ref[...] = (w[0] * o0[...] + w[1] * o1[...] + w[2] * o2[...]) / den
        lse_ref[...] = m_all + jnp.log(den)

    return pl.pallas_call(
        body, name="merge_b", grid=(s_len // tm,),
        in_specs=[_rows(tm, D_B)] * 6,
        out_specs=[_rows(tm, D_B), _rows(tm, D_B)],
        out_shape=[jax.ShapeDtypeStruct((s_len, D_B), F32), jax.ShapeDtypeStruct((s_len, D_B), F32)],
        compiler_params=_cparams(("parallel",)),
    )(*outs, *lses)


def _out_proj(ya, yb, x, g_a, g_b, w_out, g_post, g_mlp_pre):
    s_len = x.shape[0]
    tm = min(ROW_TILE, s_len)

    def body(ya_ref, yb_ref, x_ref, ga_ref, gb_ref, w_ref, gp_ref, gm_ref, ycat_ref, y2_ref, h1_ref, xn2_ref):
        ah, _ = _rms_stats(ya_ref[...])
        bh, _ = _rms_stats(yb_ref[...])
        ycat = jnp.concatenate([ah * ga_ref[...], bh * gb_ref[...]], axis=-1).astype(BF16)
        ycat_ref[...] = ycat
        y2 = _mm(ycat, w_ref[...])
        y2_ref[...] = y2
        y2h, _ = _rms_stats(y2)
        h1 = x_ref[...] + y2h * gp_ref[...]
        h1_ref[...] = h1
        h1h, _ = _rms_stats(h1)
        xn2_ref[...] = (h1h * gm_ref[...]).astype(BF16)

    sds = jax.ShapeDtypeStruct
    return pl.pallas_call(
        body, name="out_proj", grid=(s_len // tm,),
        in_specs=[_rows(tm, D_A), _rows(tm, D_B), _rows(tm, D_MODEL), _full((1, D_A)), _full((1, D_B)),
                  _full((D_MODEL, D_MODEL)), _full((1, D_MODEL)), _full((1, D_MODEL))],
        out_specs=[_rows(tm, D_MODEL)] * 4,
        out_shape=[sds((s_len, D_MODEL), BF16), sds((s_len, D_MODEL), F32), sds((s_len, D_MODEL), F32),
                   sds((s_len, D_MODEL), BF16)],
        compiler_params=_cparams(("parallel",)),
    )(ya, yb, x, g_a, g_b, w_out, g_post, g_mlp_pre)


def _ff1(xn2, w_ff1):
    s_len = xn2.shape[0]
    tm = min(ROW_TILE, s_len)

    def body(x_ref, w_ref, u_ref):
        u_ref[...] = _mm(x_ref[...], w_ref[...])

    return pl.pallas_call(
        body, name="ff1", grid=(s_len // tm,),
        in_specs=[_rows(tm, D_MODEL), _full((D_MODEL, D_FF))],
        out_specs=_rows(tm, D_FF),
        out_shape=jax.ShapeDtypeStruct((s_len, D_FF), F32),
        compiler_params=_cparams(("parallel",)),
    )(xn2, w_ff1)


def _ff2(u, w_ff2, h1, g_post, g_ple):
    s_len = u.shape[0]
    tm = min(ROW_TILE, s_len)

    def body(u_ref, w_ref, h1_ref, gp_ref, gl_ref, f2_ref, h2_ref, xn3_ref):
        f = jnp.square(jnp.maximum(u_ref[...], 0.0)).astype(BF16)
        f2 = _mm(f, w_ref[...])
        f2_ref[...] = f2
        f2h, _ = _rms_stats(f2)
        h2 = h1_ref[...] + f2h * gp_ref[...]
        h2_ref[...] = h2
        h2h, _ = _rms_stats(h2)
        xn3_ref[...] = (h2h * gl_ref[...]).astype(BF16)

    sds = jax.ShapeDtypeStruct
    return pl.pallas_call(
        body, name="ff2", grid=(s_len // tm,),
        in_specs=[_rows(tm, D_FF), _full((D_FF, D_MODEL)), _rows(tm, D_MODEL), _full((1, D_MODEL)),
                  _full((1, D_MODEL))],
        out_specs=[_rows(tm, D_MODEL)] * 3,
        out_shape=[sds((s_len, D_MODEL), F32), sds((s_len, D_MODEL), F32), sds((s_len, D_MODEL), BF16)],
        compiler_params=_cparams(("parallel",)),
    )(u, w_ff2, h1, g_post, g_ple)


def _ple_loss(xn3, p, h2, f2, tgt, w_gate, w_ple, g_ple, g_mlp_post):
    s_len = h2.shape[0]
    tm = min(ROW_TILE, s_len)

    def body(xn3_ref, p_ref, h2_ref, f2_ref, t_ref, wg_ref, wp_ref, gl_ref, gp_ref,
             dh2_ref, df2_ref, dgl_ref, dpp_ref, loss_ref, dgple_ref, dgpost_ref):
        gate = jax.nn.sigmoid(_mm(xn3_ref[...], wg_ref[...]))
        pp = _mm(p_ref[...].astype(BF16), wp_ref[...])
        h2 = h2_ref[...]
        err = h2 + gate * pp - t_ref[...]
        sq = jnp.sum(jnp.sum(err * err, axis=1, keepdims=True), axis=0, keepdims=True)
        _acc_out(loss_ref, sq * (0.5 / D_MODEL))
        dh3 = err * (1.0 / D_MODEL)
        dgl = (dh3 * pp) * gate * (1.0 - gate)
        dgl_b = dgl.astype(BF16)
        dgl_ref[...] = dgl_b
        dpp_ref[...] = (dh3 * gate).astype(BF16)
        dxn3 = _mm_nt(dgl_b, wg_ref[...])
        h2h, r2 = _rms_stats(h2)
        dx, dg = _rms_bwd(dxn3, h2h, r2, gl_ref[...])
        _acc_out(dgple_ref, dg)
        dh2 = dh3 + dx
        dh2_ref[...] = dh2
        f2h, rf = _rms_stats(f2_ref[...])
        df2, dg = _rms_bwd(dh2, f2h, rf, gp_ref[...])
        _acc_out(dgpost_ref, dg)
        df2_ref[...] = df2.astype(BF16)

    sds = jax.ShapeDtypeStruct
    return pl.pallas_call(
        body, name="ple_loss", grid=(s_len // tm,),
        in_specs=[_rows(tm, D_MODEL), _rows(tm, D_PLE), _rows(tm, D_MODEL), _rows(tm, D_MODEL), _rows(tm, D_MODEL),
                  _full((D_MODEL, D_MODEL)), _full((D_PLE, D_MODEL)), _full((1, D_MODEL)), _full((1, D_MODEL))],
        out_specs=[_rows(tm, D_MODEL)] * 3 + [_rows(tm, D_MODEL), _full((1, 1)), _full((1, D_MODEL)),
                                              _full((1, D_MODEL))],
        out_shape=[sds((s_len, D_MODEL), F32), sds((s_len, D_MODEL), BF16), sds((s_len, D_MODEL), BF16),
                   sds((s_len, D_MODEL), BF16), sds((1, 1), F32), sds((1, D_MODEL), F32), sds((1, D_MODEL), F32)],
        compiler_params=_cparams(("arbitrary",)),
    )(xn3, p, h2, f2, tgt, w_gate, w_ple, g_ple, g_mlp_post)


def _ff2_bwd(df2, w_ff2, u):
    s_len = u.shape[0]
    tm = min(ROW_TILE, s_len)

    def body(d_ref, w_ref, u_ref, du_ref):
        df = _mm_nt(d_ref[...], w_ref[...])
        du_ref[...] = (df * (2.0 * jnp.maximum(u_ref[...], 0.0))).astype(BF16)

    return pl.pallas_call(
        body, name="ff2_bwd", grid=(s_len // tm,),
        in_specs=[_rows(tm, D_MODEL), _full((D_FF, D_MODEL)), _rows(tm, D_FF)],
        out_specs=_rows(tm, D_FF),
        out_shape=jax.ShapeDtypeStruct((s_len, D_FF), BF16),
        compiler_params=_cparams(("parallel",)),
    )(df2, w_ff2, u)


def _ff1_bwd(du, w_ff1, dh2, h1, y2, g_mlp_pre, g_post):
    s_len = du.shape[0]
    tm = min(ROW_TILE, s_len)

    def body(du_ref, w_ref, dh2_ref, h1_ref, y2_ref, gm_ref, gp_ref, dh1_ref, dy2_ref, dgm_ref, dgp_ref):
        dxn2 = _mm_nt(du_ref[...], w_ref[...])
        h1h, r1 = _rms_stats(h1_ref[...])
        dx, dg = _rms_bwd(dxn2, h1h, r1, gm_ref[...])
        _acc_out(dgm_ref, dg)
        dh1 = dh2_ref[...] + dx
        dh1_ref[...] = dh1
        y2h, ry = _rms_stats(y2_ref[...])
        dy2, dg = _rms_bwd(dh1, y2h, ry, gp_ref[...])
        _acc_out(dgp_ref, dg)
        dy2_ref[...] = dy2.astype(BF16)

    sds = jax.ShapeDtypeStruct
    return pl.pallas_call(
        body, name="ff1_bwd", grid=(s_len // tm,),
        in_specs=[_rows(tm, D_FF), _full((D_MODEL, D_FF)), _rows(tm, D_MODEL), _rows(tm, D_MODEL),
                  _rows(tm, D_MODEL), _full((1, D_MODEL)), _full((1, D_MODEL))],
        out_specs=[_rows(tm, D_MODEL), _rows(tm, D_MODEL), _full((1, D_MODEL)), _full((1, D_MODEL))],
        out_shape=[sds((s_len, D_MODEL), F32), sds((s_len, D_MODEL), BF16), sds((1, D_MODEL), F32),
                   sds((1, D_MODEL), F32)],
        compiler_params=_cparams(("arbitrary",)),
    )(du, w_ff1, dh2, h1, y2, g_mlp_pre, g_post)


def _out_proj_bwd(dy2, w_out, ya, yb, lse_b, g_a, g_b):
    s_len = ya.shape[0]
    tm = min(ROW_TILE, s_len)

    def body(d_ref, w_ref, ya_ref, yb_ref, lse_ref, ga_ref, gb_ref, dya_ref, dyb_ref, da_ref, st_ref, dga_ref,
             dgb_ref):
        dycat = _mm_nt(d_ref[...], w_ref[...])
        lane = lax.broadcasted_iota(jnp.int32, (tm, 128), 1)
        low = lane < HEAD_DIM
        is_lse = (lane % HEAD_DIM) < (HEAD_DIM // 2)

        def head_sums(prod_chunk):
            return (jnp.sum(jnp.where(low, prod_chunk, 0.0), axis=1, keepdims=True),
                    jnp.sum(jnp.where(low, 0.0, prod_chunk), axis=1, keepdims=True))

        ya = ya_ref[...]
        yh, r = _rms_stats(ya)
        dya, dg = _rms_bwd(dycat[:, :D_A], yh, r, ga_ref[...])
        _acc_out(dga_ref, dg)
        dya_ref[...] = dya
        prod = dya * ya
        for c in range(D_A // 128):
            da_ref[2 * c], da_ref[2 * c + 1] = head_sums(prod[:, 128 * c:128 * (c + 1)])

        yb = yb_ref[...]
        yh, r = _rms_stats(yb)
        dyb, dg = _rms_bwd(dycat[:, D_A:], yh, r, gb_ref[...])
        _acc_out(dgb_ref, dg)
        dyb_ref[...] = dyb.astype(BF16)
        prod = dyb * yb
        for c in range(D_B // 128):
            sl = slice(128 * c, 128 * (c + 1))
            d_lo, d_hi = head_sums(prod[:, sl])
            st_ref[:, sl] = jnp.where(is_lse, lse_ref[:, sl], jnp.where(low, d_lo, d_hi))

    sds = jax.ShapeDtypeStruct
    return pl.pallas_call(
        body, name="out_proj_bwd", grid=(s_len // tm,),
        in_specs=[_rows(tm, D_MODEL), _full((D_MODEL, D_MODEL)), _rows(tm, D_A), _rows(tm, D_B), _rows(tm, D_B),
                  _full((1, D_A)), _full((1, D_B))],
        out_specs=[_rows(tm, D_A), _rows(tm, D_B), _stat_spec(tm), _rows(tm, D_B), _full((1, D_A)),
                   _full((1, D_B))],
        out_shape=[sds((s_len, D_A), F32), sds((s_len, D_B), BF16), sds((N_HEADS_A, s_len, 1), F32),
                   sds((s_len, D_B), F32), sds((1, D_A), F32), sds((1, D_B), F32)],
        compiler_params=_cparams(("arbitrary",)),
    )(dy2, w_out, ya, yb, lse_b, g_a, g_b)


def _in_proj_bwd(dqr, dkv, dqb, dkb, dvb, qpre, kpre, x, dh1, g1, w_in, cc, ss, gq2, gk2, ones128):
    s_len = x.shape[0]
    tm = min(ROW_TILE // 2, s_len)

    def body(dqr_ref, dkv_ref, dq0, dq1, dq2, dk0, dk1, dk2, dv0, dv1, dv2, qpre_ref, kpre_ref, x_ref,
             dh1_ref, g_ref, w_ref, cc_ref, ss_ref, gq_ref, gk_ref, one_ref, dproj_ref, gx_ref, dg1_ref, dgq_ref,
             dgk_ref):
        low = _low_lanes(tm)
        dkr = jnp.where(low, dkv_ref[0], pltpu.roll(dkv_ref[1], HEAD_DIM, 1))
        dva = jnp.where(low, pltpu.roll(dkv_ref[0], HEAD_DIM, 1), dkv_ref[1])
        first_half = _first_half_mask((tm, 128))
        ones = one_ref[...]
        cc_t, ss_t = cc_ref[...], ss_ref[...]

        def norm_rope_bwd(dy, xc, gain):
            dn = dy * cc_t - _swap_halves(dy, first_half) * ss_t
            r = lax.rsqrt(_xdot(xc * xc, ones) * (1.0 / HEAD_DIM) + EPS)
            xh = xc * r
            gdy = dn * gain
            dx = r * (gdy - xh * (_xdot(gdy * xh, ones) * (1.0 / HEAD_DIM)))
            return dx, jnp.sum(dn * xh, axis=0, keepdims=True)

        dgq = jnp.zeros((1, 128), F32)
        parts = []
        for c in range(D_A // 128):
            sl = slice(128 * c, 128 * (c + 1))
            dx, dg = norm_rope_bwd(dqr_ref[:, sl] * Q_SCALE, qpre_ref[:, sl], gq_ref[...])
            parts.append(dx)
            dgq = dgq + dg
        dxk, dgk = norm_rope_bwd(dkr, kpre_ref[...], gk_ref[...])
        _acc_out(dgq_ref, dgq)
        _acc_out(dgk_ref, dgk)
        parts += [dxk, dva, (dq0[...] + dq1[...] + dq2[...]) * Q_SCALE, dk0[...] + dk1[...] + dk2[...],
                  dv0[...] + dv1[...] + dv2[...]]
        dproj = jnp.concatenate(parts, axis=-1).astype(BF16)
        dproj_ref[...] = dproj
        dxn = _mm_nt(dproj, w_ref[...])
        xh, r = _rms_stats(x_ref[...])
        dx, dg = _rms_bwd(dxn, xh, r, g_ref[...])
        _acc_out(dg1_ref, dg)
        gx_ref[...] = dh1_ref[...] + dx

    sds = jax.ShapeDtypeStruct
    return pl.pallas_call(
        body, name="in_proj_bwd", grid=(s_len // tm,),
        in_specs=[_rows(tm, D_A), pl.BlockSpec((N_KV_A, tm, 128), lambda i: (0, i, 0))] + [_rows(tm, D_B)] * 9
                 + [_rows(tm, D_A), _rows(tm, D_KV_A), _rows(tm, D_MODEL), _rows(tm, D_MODEL),
                    _full((1, D_MODEL)), _full((D_MODEL, D_IN)), _rows(tm, 128), _rows(tm, 128), _full((1, 128)),
                    _full((1, 128)), _full((128, 128))],
        out_specs=[_rows(tm, D_IN), _rows(tm, D_MODEL), _full((1, D_MODEL)), _full((1, 128)), _full((1, 128))],
        out_shape=[sds((s_len, D_IN), BF16), sds((s_len, D_MODEL), F32), sds((1, D_MODEL), F32),
                   sds((1, 128), F32), sds((1, 128), F32)],
        compiler_params=_cparams(("arbitrary",)),
    )(dqr, dkv, *dqb, *dkb, *dvb, qpre, kpre, x, dh1, g1, w_in, cc, ss, gq2, gk2, ones128)


def _dw(a, b, name, relu2=False):
    s_len, ka = a.shape
    n = b.shape[1]
    ts = min(DW_TS, s_len)
    bk = min(ka, 1024)
    bn = n if n % 1024 else 1024

    def body(a_ref, b_ref, o_ref):
        @pl.when(pl.program_id(2) == 0)
        def _():
            o_ref[...] = jnp.zeros_like(o_ref)

        av = a_ref[...]
        if relu2:
            av = jnp.square(jnp.maximum(av, 0.0))
        o_ref[...] += _mm_tn(av.astype(BF16), b_ref[...])

    return pl.pallas_call(
        body, name=name, grid=(ka // bk, n // bn, s_len // ts),
        in_specs=[pl.BlockSpec((ts, bk), lambda i, j, k: (k, i)), pl.BlockSpec((ts, bn), lambda i, j, k: (k, j))],
        out_specs=pl.BlockSpec((bk, bn), lambda i, j, k: (i, j)),
        out_shape=jax.ShapeDtypeStruct((ka, n), F32),
        compiler_params=_cparams(("parallel", "parallel", "arbitrary")),
    )(a, b)


def _stack_heads(block, low, data_low):
    parts = []
    for c in range(GROUP_A // 2):
        chunk = block[:, 128 * c:128 * (c + 1)]
        swapped = pltpu.roll(chunk, HEAD_DIM, 1)
        for h_low in (chunk, swapped) if data_low else (swapped, chunk):
            parts.append(jnp.where(low, h_low, 0.0) if data_low else jnp.where(low, 0.0, h_low))
    return jnp.concatenate(parts, axis=0).astype(BF16)


def _unstack_heads(stacked, low, tq, data_low):
    chunks = []
    for c in range(GROUP_A // 2):
        even = stacked[2 * c * tq:(2 * c + 1) * tq]
        odd = stacked[(2 * c + 1) * tq:(2 * c + 2) * tq]
        if data_low:
            chunks.append(jnp.where(low, even, pltpu.roll(odd, HEAD_DIM, 1)))
        else:
            chunks.append(jnp.where(low, pltpu.roll(even, HEAD_DIM, 1), odd))
    return chunks


def _attn_a_fwd(qa, kv):
    s_len = kv.shape[1]
    tq = min(ATT_TQ, s_len)
    tk = min(ATT_TK_FWD, s_len)
    rows = GROUP_A * tq

    def body(q_ref, kv_ref, o_ref, lse_ref):
        low = _low_lanes(tq)
        low_k = _low_lanes(tk)
        q = _stack_heads(q_ref[...].astype(F32), low, data_low=True)

        def step(j, carry):
            m, acc = carry
            kvj = kv_ref[0, pl.ds(pl.multiple_of(j * tk, tk), tk), :]
            s = _mm_nt(q, kvj)
            m_new = jnp.maximum(m, jnp.max(s, axis=1, keepdims=True))
            p = jnp.exp(s - m_new).astype(BF16)
            acc = jnp.exp(m - m_new) * acc + _mm(p, jnp.where(low_k, jnp.ones_like(kvj), kvj))
            return m_new, acc

        init = (jnp.full((rows, 1), -jnp.inf, F32), jnp.zeros((rows, 128), F32))
        m, acc = lax.fori_loop(0, s_len // tk, step, init)
        for c, chunk in enumerate(_unstack_heads(acc / pltpu.roll(acc, HEAD_DIM, 1), low, tq, data_low=False)):
            o_ref[:, 128 * c:128 * (c + 1)] = chunk
        lse_ref[...] = (m + jnp.log(acc[:, :1])).reshape(GROUP_A, tq, 1)

    return pl.pallas_call(
        body, name="attn_a_fwd", grid=(N_KV_A, s_len // tq),
        in_specs=[pl.BlockSpec((tq, 256), lambda g, i: (i, g)),
                  pl.BlockSpec((1, s_len, 128), lambda g, i: (g, 0, 0))],
        out_specs=[pl.BlockSpec((tq, 256), lambda g, i: (i, g)),
                   pl.BlockSpec((GROUP_A, tq, 1), lambda g, i: (g, i, 0))],
        out_shape=[jax.ShapeDtypeStruct((s_len, D_A), F32),
                   jax.ShapeDtypeStruct((N_HEADS_A, s_len, 1), F32)],
        compiler_params=_cparams(("parallel", "parallel")),
    )(qa, kv)


def _attn_a_bwd(qa, dya, kv, lse, delta):
    s_len = kv.shape[1]
    tq = min(ATT_TQ_BWD, s_len)
    tk = min(ATT_TK_BWD, s_len)
    rows = GROUP_A * tq

    def body(q_ref, do_ref, kv_ref, lse_ref, dl_ref, dq_ref, dkv_ref):
        @pl.when(pl.program_id(1) == 0)
        def _():
            dkv_ref[...] = jnp.zeros_like(dkv_ref)

        low = _low_lanes(tq)
        q = _stack_heads(q_ref[...].astype(F32), low, data_low=True)
        do = _stack_heads(do_ref[...], low, data_low=False)
        lse_t = lse_ref[...].reshape(rows, 1)
        dl_t = dl_ref[...].reshape(rows, 1)
        q_t = q.astype(F32).T.astype(BF16)
        do_t = do.astype(F32).T.astype(BF16)

        def step(j, dq):
            span = pl.ds(pl.multiple_of(j * tk, tk), tk)
            kvj = kv_ref[0, span, :]
            p = jnp.exp(_mm_nt(q, kvj) - lse_t)
            ds = (p * (_mm_nt(do, kvj) - dl_t)).astype(BF16)
            dkv_ref[0, :, span] += _mm(q_t, ds) + _mm(do_t, p.astype(BF16))
            return dq + _mm(ds, kvj)

        dq = lax.fori_loop(0, s_len // tk, step, jnp.zeros((rows, 128), F32))
        for c, chunk in enumerate(_unstack_heads(dq, low, tq, data_low=True)):
            dq_ref[:, 128 * c:128 * (c + 1)] = chunk

    return pl.pallas_call(
        body, name="attn_a_bwd", grid=(N_KV_A, s_len // tq),
        in_specs=[pl.BlockSpec((tq, 256), lambda g, i: (i, g)),
                  pl.BlockSpec((tq, 256), lambda g, i: (i, g)),
                  pl.BlockSpec((1, s_len, 128), lambda g, i: (g, 0, 0)),
                  pl.BlockSpec((GROUP_A, tq, 1), lambda g, i: (g, i, 0)),
                  pl.BlockSpec((GROUP_A, tq, 1), lambda g, i: (g, i, 0))],
        out_specs=[pl.BlockSpec((tq, 256), lambda g, i: (i, g)),
                   pl.BlockSpec((1, 128, s_len), lambda g, i: (g, 0, 0))],
        out_shape=[jax.ShapeDtypeStruct((s_len, D_A), F32),
                   jax.ShapeDtypeStruct((N_KV_A, 128, s_len), F32)],
        compiler_params=_cparams(("parallel", "arbitrary")),
    )(qa, dya, kv, lse, delta)


class _SwaGeometry:
    def __init__(self, s_len, r):
        self.r = r
        self.tq = SWA_TQ
        self.block = min(max(SWA_MIN_BLOCK, 2 * SWA_TQ * r), s_len)
        self.halo = HALF_WIN * r
        self.nsub = self.block // (self.tq * r)
        self.band = self.tq + 2 * HALF_WIN
        self.length = s_len // r
        self.nblk = s_len // self.block
        self.nhalo = s_len // self.halo
        assert self.nsub * self.tq * r == self.block and self.block % self.halo == 0

    def specs(self):
        per = self.block // self.halo
        cur = pl.BlockSpec((self.block, 128), lambda c, i: (i, c))
        prev = pl.BlockSpec((self.halo, 128), lambda c, i: (jnp.maximum(i * per - 1, 0), c))
        nxt = pl.BlockSpec((self.halo, 128), lambda c, i: (jnp.minimum((i + 1) * per, self.nhalo - 1), c))
        return prev, cur, nxt

    def tiles(self):
        return [(rho + self.r * j * self.tq, self.halo + rho + self.r * (j * self.tq - HALF_WIN), j)
                for j in range(self.nsub) for rho in range(self.r)]

    def own(self, start):
        return pl.ds(start, self.tq, stride=self.r)

    def around(self, start):
        return pl.ds(start, self.band, stride=self.r)

    def fill(self, dst, prev_ref, cur_ref, next_ref):
        dst[:self.halo, :] = prev_ref[...].astype(F32)
        dst[self.halo:self.halo + self.block, :] = cur_ref[...].astype(F32)
        dst[self.halo + self.block:, :] = next_ref[...].astype(F32)

    def first_position(self, j):
        return (pl.program_id(1) * self.block) // self.r + j * self.tq

    def window(self, rows):
        row = lax.broadcasted_iota(jnp.int32, (rows, self.band), 0) % self.tq
        col = lax.broadcasted_iota(jnp.int32, (rows, self.band), 1)
        return jnp.abs(col - HALF_WIN - row) <= HALF_WIN, col

    def extended(self):
        return pltpu.VMEM((self.block + 2 * self.halo, 128), F32)

    def plain(self):
        return pltpu.VMEM((self.block, 128), F32)


def _low_lanes(rows):
    return lax.broadcasted_iota(jnp.int32, (rows, 128), 1) < HEAD_DIM


def _one_head(x, low, half):
    return jnp.where(low if half == 0 else jnp.logical_not(low), x, 0.0).astype(BF16)


def _two_heads(x, low):
    return jnp.concatenate([_one_head(x, low, 0), _one_head(x, low, 1)], axis=0)


def _carry_ride(base_body, n_in, n_out, n_scratch, ride, grid):
    if ride is None:
        return base_body
    n = ride.n

    def body(*refs):
        o = n_in + n
        ins, ride_ins = refs[:n_in], refs[n_in:o]
        outs, ride_outs = refs[o:o + n_out], refs[o + n_out:o + n_out + n]
        o += n_out + n
        scratch, sems = refs[o:o + n_scratch], refs[o + n_scratch:]
        at_first = (pl.program_id(0) == 0) & (pl.program_id(1) == 0)
        at_last = (pl.program_id(0) == grid[0] - 1) & (pl.program_id(1) == grid[1] - 1)

        @pl.when(at_first)
        def _():
            ride.start(ride_ins, ride_outs, sems)

        base_body(*ins, *outs, *scratch)

        @pl.when(at_last)
        def _():
            ride.finish(ride_ins, ride_outs, sems)

    return body


def _ride_call(base_body, name, grid, in_specs, out_specs, out_shape, scratch_shapes, operands, ride):
    n = 0 if ride is None else ride.n
    extra = [] if ride is None else ride.operands
    outs = pl.pallas_call(
        _carry_ride(base_body, len(in_specs), len(out_specs), len(scratch_shapes), ride, grid), name=name, grid=grid,
        in_specs=list(in_specs) + [ANY] * n, out_specs=list(out_specs) + [ANY] * n,
        out_shape=list(out_shape) + ([] if ride is None else ride.out_shape()),
        scratch_shapes=list(scratch_shapes) + ([] if ride is None else ride.scratch_shapes()),
        compiler_params=_cparams(("arbitrary", "arbitrary")),
    )(*operands, *extra)
    return outs[:len(out_specs)], outs[len(out_specs):]


def _swa_fwd(q, k, v, bias, r, ride=None):
    geo = _SwaGeometry(q.shape[0], r)
    prev, cur, nxt = geo.specs()

    def body(q_ref, kp, kc, kn, vp, vc, vn, b_ref, o_ref, lse_ref, qf, kf, vf):
        qf[...] = q_ref[...].astype(F32)
        geo.fill(kf, kp, kc, kn)
        geo.fill(vf, vp, vc, vn)
        tq = geo.tq
        low_q = _low_lanes(tq)
        in_window, col = geo.window(2 * tq)
        bias = b_ref[...].reshape(2 * tq, geo.band)
        for own, around, j in geo.tiles():
            key = geo.first_position(j) - HALF_WIN + col
            valid = in_window & (key >= 0) & (key < geo.length)
            q2 = _two_heads(qf[geo.own(own), :], low_q)
            kb = kf[geo.around(around), :].astype(BF16)
            vb = vf[geo.around(around), :].astype(BF16)
            s = jnp.where(valid, _mm_nt(q2, kb) + bias, NEG_BIG)
            m = jnp.max(s, axis=1, keepdims=True)
            e = jnp.exp(s - m)
            l = jnp.sum(e, axis=1, keepdims=True)
            o2 = _mm(e.astype(BF16), vb) / l
            lse2 = m + jnp.log(l)
            o_ref[geo.own(own), :] = jnp.where(low_q, o2[:tq], o2[tq:])
            lse_ref[geo.own(own), :] = jnp.where(low_q, lse2[:tq], lse2[tq:])

    sds = jax.ShapeDtypeStruct
    (o, lse), carried = _ride_call(
        body, "swa_fwd_%d" % r, (D_B // 128, geo.nblk),
        [cur, prev, cur, nxt, prev, cur, nxt, pl.BlockSpec((2, geo.tq, geo.band), lambda c, i: (c, 0, 0))],
        [cur, cur], [sds(q.shape, F32), sds(q.shape, F32)], [geo.plain(), geo.extended(), geo.extended()],
        (q, k, k, k, v, v, v, bias), ride)
    return o, lse, carried


def _head_stats(st, half):
    lo = HEAD_DIM * half
    return st[:, lo:lo + 1], st[:, lo + HEAD_DIM // 2:lo + HEAD_DIM // 2 + 1]


def _swa_bwd_q(q, k, v, dy, st, bias, r, ride=None):
    geo = _SwaGeometry(q.shape[0], r)
    prev, cur, nxt = geo.specs()
    bias_spec = pl.BlockSpec((2, geo.tq, geo.band), lambda c, i: (c, 0, 0))

    def body(q_ref, kp, kc, kn, vp, vc, vn, dy_ref, st_ref, b_ref, dq_ref, db_ref, qf, kf, vf, dyf):
        @pl.when(pl.program_id(1) == 0)
        def _():
            db_ref[...] = jnp.zeros_like(db_ref)

        qf[...] = q_ref[...].astype(F32)
        dyf[...] = dy_ref[...].astype(F32)
        geo.fill(kf, kp, kc, kn)
        geo.fill(vf, vp, vc, vn)
        tq = geo.tq
        low_q = _low_lanes(tq)
        in_window, col = geo.window(2 * tq)
        bias = b_ref[...].reshape(2 * tq, geo.band)
        for own, around, j in geo.tiles():
            key = geo.first_position(j) - HALF_WIN + col
            valid = in_window & (key >= 0) & (key < geo.length)
            sts = st_ref[geo.own(own), :]
            (lse0, delta0), (lse1, delta1) = _head_stats(sts, 0), _head_stats(sts, 1)
            lse = jnp.concatenate([lse0, lse1], axis=0)
            delta = jnp.concatenate([delta0, delta1], axis=0)
            kb = kf[geo.around(around), :].astype(BF16)
            vb = vf[geo.around(around), :].astype(BF16)
            s = jnp.where(valid, _mm_nt(_two_heads(qf[geo.own(own), :], low_q), kb) + bias, NEG_BIG)
            p = jnp.exp(s - lse)
            ds = p * (_mm_nt(_two_heads(dyf[geo.own(own), :], low_q), vb) - delta)
            db_ref[...] += ds.reshape(2, tq, geo.band)
            dq2 = _mm(ds.astype(BF16), kb)
            dq_ref[geo.own(own), :] = jnp.where(low_q, dq2[:tq], dq2[tq:])

    (dq, dbias), carried = _ride_call(
        body, "swa_bwd_q_%d" % r, (D_B // 128, geo.nblk),
        [cur, prev, cur, nxt, prev, cur, nxt, cur, cur, bias_spec], [cur, bias_spec],
        [jax.ShapeDtypeStruct(q.shape, F32), jax.ShapeDtypeStruct(bias.shape, F32)],
        [geo.plain(), geo.extended(), geo.extended(), geo.plain()], (q, k, k, k, v, v, v, dy, st, bias), ride)
    return dq, dbias, carried


def _swa_bwd_kv(q, k, v, dy, st, bias_kv, r):
    geo = _SwaGeometry(q.shape[0], r)
    prev, cur, nxt = geo.specs()

    def body(k_ref, v_ref, qp, qc, qn, dp_, dc_, dn_, sp, sc, sn, b_ref, dk_ref, dv_ref, kf, vf, qf, dyf, stf):
        kf[...] = k_ref[...].astype(F32)
        vf[...] = v_ref[...].astype(F32)
        geo.fill(qf, qp, qc, qn)
        geo.fill(dyf, dp_, dc_, dn_)
        geo.fill(stf, sp, sc, sn)
        band = geo.band
        low_b = _low_lanes(band)
        row = lax.broadcasted_iota(jnp.int32, (geo.tq, 2 * band), 0)
        col = lax.broadcasted_iota(jnp.int32, (geo.tq, 2 * band), 1) % band
        in_window = jnp.abs(row + HALF_WIN - col) <= HALF_WIN
        bias = jnp.concatenate([b_ref[0], b_ref[1]], axis=1)
        half_lanes = HEAD_DIM // 2
        for own, around, j in geo.tiles():
            query = geo.first_position(j) - HALF_WIN + col
            valid = in_window & (query >= 0) & (query < geo.length)
            ks = kf[geo.own(own), :].astype(BF16)
            vs = vf[geo.own(own), :].astype(BF16)
            q2 = _two_heads(qf[geo.around(around), :], low_b)
            dy2 = _two_heads(dyf[geo.around(around), :], low_b)
            st_t = stf[geo.around(around), :].T
            lse = jnp.concatenate([st_t[:1, :], st_t[HEAD_DIM:HEAD_DIM + 1, :]], axis=1)
            delta = jnp.concatenate([st_t[half_lanes:half_lanes + 1, :],
                                     st_t[HEAD_DIM + half_lanes:HEAD_DIM + half_lanes + 1, :]], axis=1)
            s = jnp.where(valid, _mm_nt(ks, q2) + bias, NEG_BIG)
            p = jnp.exp(s - lse)
            ds = p * (_mm_nt(vs, dy2) - delta)
            dv_ref[geo.own(own), :] = _mm(p.astype(BF16), dy2)
            dk_ref[geo.own(own), :] = _mm(ds.astype(BF16), q2)

    return pl.pallas_call(
        body, name="swa_bwd_kv_%d" % r, grid=(D_B // 128, geo.nblk),
        in_specs=[cur, cur, prev, cur, nxt, prev, cur, nxt, prev, cur, nxt,
                  pl.BlockSpec((2, geo.tq, geo.band), lambda c, i: (c, 0, 0))],
        out_specs=[cur, cur],
        out_shape=[jax.ShapeDtypeStruct(q.shape, F32), jax.ShapeDtypeStruct(q.shape, F32)],
        scratch_shapes=[geo.plain(), geo.plain(), geo.extended(), geo.extended(), geo.extended()],
        compiler_params=_cparams(("parallel", "parallel")),
    )(k, v, q, q, q, dy, dy, dy, st, st, st, bias_kv)


BIAS_ROWS = 16
BIAS_TN = 4096


def _bias_tiles(onehot, rel_bias_t):
    n = onehot.shape[0]

    def body(oh_ref, rb_ref, o_ref):
        o_ref[...] = sum(_mm_nt(piece, oh_ref[...]) for piece in _split3(rb_ref[...]))

    return pl.pallas_call(
        body, name="bias_tiles", grid=(n // BIAS_TN,),
        in_specs=[_rows(BIAS_TN, 128), _full((BIAS_ROWS, 128))],
        out_specs=pl.BlockSpec((BIAS_ROWS, BIAS_TN), lambda i: (0, i)),
        out_shape=jax.ShapeDtypeStruct((BIAS_ROWS, n), F32),
        compiler_params=_cparams(("parallel",)),
    )(onehot, rel_bias_t)


def _bias_bwd(onehot, dbias_rows, so_far, r):
    n = onehot.shape[0]

    def body(oh, d, prev_ref, g_ref):
        @pl.when(pl.program_id(0) == 0)
        def _():
            g_ref[...] = prev_ref[...]

        hi, lo, _ = _split3(d[...])
        g_ref[...] += _mm(hi, oh[...]) + _mm(lo, oh[...])

    return pl.pallas_call(
        body, name="bias_bwd_%d" % r, grid=(n // BIAS_TN,),
        in_specs=[_rows(BIAS_TN, 128), pl.BlockSpec((BIAS_ROWS, BIAS_TN), lambda i: (0, i)), _full((BIAS_ROWS, 128))],
        out_specs=_full((BIAS_ROWS, 128)),
        out_shape=jax.ShapeDtypeStruct((BIAS_ROWS, 128), F32),
        compiler_params=_cparams(("arbitrary",)),
    )(onehot, dbias_rows, so_far)


LATE = ("w_out", "w_ff1", "w_ff2", "w_ple_gate", "w_ple_proj")


def _local_step(x, p, tgt, w_in, late_shards, g_attn_pre, g_q, g_k, g_out_a, g_out_b, g_attn_post, rel_bias,
                g_mlp_pre, g_mlp_post, g_ple):
    s_len = x.shape[0]
    cc, ss = _rope_tables(s_len)
    gq2 = jnp.concatenate([g_q, g_q], axis=-1)
    gk2 = jnp.concatenate([g_k, g_k], axis=-1)
    ones128 = _group_ones(128)
    rel_bias_t = jnp.zeros((BIAS_ROWS, 128), F32).at[:N_HEADS_B, :N_BUCKETS].set(rel_bias.T)

    xn1, qpre, kpre, qa, kv, qb, kb, vb = _in_proj(x, g_attn_pre, w_in, cc, ss, gq2, gk2, ones128)
    ya, lse_a = _attn_a_fwd(qa, kv)

    tiles, outs, lses = [], [], []
    for r in DILATIONS:
        tq = SWA_TQ
        onehot = _bucket_onehot(tq, r)
        bias = _bias_tiles(onehot, rel_bias_t)[:N_HEADS_B].reshape(N_HEADS_B, tq, tq + 2 * HALF_WIN)
        o_r, lse_r, gathered = _swa_fwd(qb, kb, vb, bias, r, _GatherRide(late_shards) if r == DILATIONS[-1] else None)
        tiles.append((onehot, bias))
        outs.append(o_r)
        lses.append(lse_r)
    yb, lse_b = _merge_b(outs, lses)
    w_out, w_ff1, w_ff2, w_gate, w_ple = (_whole(n, g, mine) for n, g, mine in zip(LATE, gathered, late_shards))

    ycat, y2, h1, xn2 = _out_proj(ya, yb, x, g_out_a, g_out_b, w_out, g_attn_post, g_mlp_pre)
    u = _ff1(xn2, w_ff1)
    f2, h2, xn3 = _ff2(u, w_ff2, h1, g_mlp_post, g_ple)
    dh2, df2, dgl, dpp, loss, dg_ple, dg_mlp_post = _ple_loss(xn3, p, h2, f2, tgt, w_gate, w_ple, g_ple, g_mlp_post)

    grads = {"g_ple": dg_ple, "g_mlp_post": dg_mlp_post}
    grads["w_ple_gate"] = _dw(xn3, dgl, "dw_gate")
    grads["w_ple_proj"] = _dw(p, dpp, "dw_ple")
    grads["w_ff2"] = _dw(u, df2, "dw_ff2", relu2=True)
    du = _ff2_bwd(df2, w_ff2, u)
    grads["w_ff1"] = _dw(xn2, du, "dw_ff1")
    dh1, dy2, grads["g_mlp_pre"], grads["g_attn_post"] = _ff1_bwd(du, w_ff1, dh2, h1, y2, g_mlp_pre, g_attn_post)
    grads["w_out"] = _dw(ycat, dy2, "dw_out")
    dya, dyb, delta_a, st_b, grads["g_out_a"], grads["g_out_b"] = _out_proj_bwd(dy2, w_out, ya, yb, lse_b, g_out_a,
                                                                              g_out_b)

    pairs = _pair_sums(LATE, [grads[n] for n in LATE])

    dqr, dkv_t = _attn_a_bwd(qa, dya, kv, lse_a, delta_a)
    dkv_a = dkv_t.transpose(0, 2, 1)

    dqs, dks, dvs = [], [], []
    d_rel = jnp.zeros((BIAS_ROWS, 128), F32)
    for r, (onehot, bias) in zip(DILATIONS, tiles):
        dq_r, dbias, scattered = _swa_bwd_q(qb, kb, vb, dyb, st_b, bias, r,
                                            _ScatterRide(pairs) if r == DILATIONS[0] else None)
        if scattered:
            for n, half in zip(LATE, _chip_sums(LATE, pairs, scattered)):
                grads[n] = half
        bias_kv = jnp.flip(bias, axis=(1, 2))
        dk_r, dv_r = _swa_bwd_kv(qb, kb, vb, dyb, st_b, bias_kv, r)
        dbias_rows = jnp.pad(dbias.reshape(N_HEADS_B, -1), ((0, BIAS_ROWS - N_HEADS_B), (0, 0)))
        d_rel = _bias_bwd(onehot, dbias_rows, d_rel, r)
        dqs.append(dq_r)
        dks.append(dk_r)
        dvs.append(dv_r)
    grads["rel_bias"] = d_rel[:N_HEADS_B, :N_BUCKETS].T

    dproj, grad_x, grads["g_attn_pre"], dgq2, dgk2 = _in_proj_bwd(
        dqr, dkv_a, dqs, dks, dvs, qpre, kpre, x, dh1, g_attn_pre, w_in, cc, ss, gq2, gk2, ones128)
    grads["g_q"] = dgq2[:, :HEAD_DIM] + dgq2[:, HEAD_DIM:]
    grads["g_k"] = dgk2[:, :HEAD_DIM] + dgk2[:, HEAD_DIM:]
    grads["w_in"] = _dw(xn1, dproj, "dw_in")
    return loss, grad_x, grads


ANY = pl.BlockSpec(memory_space=pl.ANY)


def _position():
    return lax.axis_index("x"), lax.axis_index("y"), lax.axis_index("c")


def _other_chips(x, y):
    return [(2 * (1 - x) + y, (1 - x, y)), (2 * x + (1 - y), (x, 1 - y)), (2 * (1 - x) + (1 - y), (1 - x, 1 - y))]


def _cast_shards(shards):
    def body(*refs):
        n = len(refs) // 2
        for i_ref, o_ref in zip(refs[:n], refs[n:]):
            o_ref[...] = i_ref[...].astype(BF16)

    return pl.pallas_call(
        body, name="cast_shards",
        in_specs=[pl.BlockSpec(memory_space=pltpu.VMEM)] * len(shards),
        out_specs=[pl.BlockSpec(memory_space=pltpu.VMEM)] * len(shards),
        out_shape=[jax.ShapeDtypeStruct(s.shape, BF16) for s in shards],
        compiler_params=_cparams(),
    )(*shards)


def _gather_weights(shards):
    n = len(shards)

    ride = _GatherRide(shards)

    def body(*refs):
        ride.start(refs[:n], refs[n:2 * n], refs[2 * n:])
        ride.finish(refs[:n], refs[n:2 * n], refs[2 * n:])

    return pl.pallas_call(
        body, name="gather_weights",
        in_specs=[ANY] * n, out_specs=[ANY] * n,
        out_shape=ride.out_shape(), scratch_shapes=ride.scratch_shapes(),
    )(*shards)


class _GatherRide:
    def __init__(self, shards):
        self.operands = list(shards)
        self.n = len(shards)

    def out_shape(self):
        return [jax.ShapeDtypeStruct((N_CHIPS,) + s.shape, s.dtype) for s in self.operands]

    def scratch_shapes(self):
        return [pltpu.SemaphoreType.DMA((3, self.n))] * 4

    @staticmethod
    def _rows(ref, core):
        half = ref.shape[0] // 2
        return pl.ds(pl.multiple_of(core * half, 16), half)

    def _ici(self, ins, outs, sems, k, a, chip):
        x, y, c = _position()
        return pltpu.make_async_remote_copy(ins[a].at[self._rows(ins[a], c), :],
                                            outs[a].at[2 * x + y, self._rows(ins[a], c), :], sems[0].at[k, a],
                                            sems[1].at[k, a], device_id=(*chip, c), device_id_type=MESH)

    def _pass_on(self, ins, outs, sems, k, a, num, core):
        x, y, c = _position()
        half = outs[a].at[num, self._rows(ins[a], core), :]
        return pltpu.make_async_remote_copy(half, half, sems[2].at[k, a], sems[3].at[k, a], device_id=(x, y, 1 - c),
                                            device_id_type=MESH)

    def start(self, ins, outs, sems):
        x, y, _ = _position()
        for k, (_, chip) in enumerate(_other_chips(x, y)):
            for a in range(self.n):
                self._ici(ins, outs, sems, k, a, chip).start()

    def finish(self, ins, outs, sems):
        x, y, c = _position()
        others = _other_chips(x, y)
        for k, (num, chip) in enumerate(others):
            for a in range(self.n):
                landed = outs[a].at[num, self._rows(ins[a], c), :]
                pltpu.make_async_remote_copy(landed, landed, sems[0].at[k, a], sems[1].at[k, a], device_id=(*chip, c),
                                             device_id_type=MESH).wait_recv()
                self._pass_on(ins, outs, sems, k, a, num, c).start()
        for k, (num, chip) in enumerate(others):
            for a in range(self.n):
                self._pass_on(ins, outs, sems, k, a, num, 1 - c).wait_recv()
        for k, (num, chip) in enumerate(others):
            for a in range(self.n):
                self._ici(ins, outs, sems, k, a, chip).wait_send()
                self._pass_on(ins, outs, sems, k, a, num, c).wait_send()


def _send_sibling_half(grads, tag):
    n = len(grads)

    def body(*refs):
        ins, outs = refs[:n], refs[n:2 * n]
        send_sems, recv_sems = refs[2 * n:]
        x, y, c = _position()
        copies = []
        for a in range(n):
            half = ins[a].shape[1] // 2
            theirs = ins[a].at[:, pl.ds(pl.multiple_of((1 - c) * half, 8), half), :]
            cp = pltpu.make_async_remote_copy(theirs, outs[a], send_sems.at[a], recv_sems.at[a],
                                              device_id=(x, y, 1 - c), device_id_type=MESH)
            cp.start()
            copies.append(cp)
        for cp in copies:
            cp.wait()

    return pl.pallas_call(
        body, name="send_sibling_half_" + tag,
        in_specs=[ANY] * n, out_specs=[ANY] * n,
        out_shape=[jax.ShapeDtypeStruct((g.shape[0], g.shape[1] // 2, g.shape[2]), g.dtype) for g in grads],
        scratch_shapes=[pltpu.SemaphoreType.DMA((n,)), pltpu.SemaphoreType.DMA((n,))],
    )(*grads)


def _scatter_to_chips(pairs):
    n = len(pairs)
    ride = _ScatterRide(pairs)

    def body(*refs):
        ride.start(refs[:n], refs[n:2 * n], refs[2 * n:])
        ride.finish(refs[:n], refs[n:2 * n], refs[2 * n:])

    return pl.pallas_call(
        body, name="scatter_to_chips",
        in_specs=[ANY] * n, out_specs=[ANY] * n,
        out_shape=ride.out_shape(), scratch_shapes=ride.scratch_shapes(),
    )(*pairs)


class _ScatterRide:
    def __init__(self, pairs):
        self.operands = list(pairs)
        self.n = len(pairs)

    def out_shape(self):
        return [jax.ShapeDtypeStruct(g.shape, g.dtype) for g in self.operands]

    def scratch_shapes(self):
        return [pltpu.SemaphoreType.DMA((3, self.n))] * 2

    @staticmethod
    def _copy(ins, outs, sems, k, a, src_slot, dst_slot, chip):
        _, _, c = _position()
        return pltpu.make_async_remote_copy(ins[a].at[src_slot], outs[a].at[dst_slot], sems[0].at[k, a],
                                            sems[1].at[k, a], device_id=(*chip, c), device_id_type=MESH)

    def start(self, ins, outs, sems):
        x, y, _ = _position()
        for k, (num, chip) in enumerate(_other_chips(x, y)):
            for a in range(self.n):
                self._copy(ins, outs, sems, k, a, num, 2 * x + y, chip).start()

    def finish(self, ins, outs, sems):
        x, y, _ = _position()
        for k, (num, chip) in enumerate(_other_chips(x, y)):
            for a in range(self.n):
                self._copy(ins, outs, sems, k, a, 2 * x + y, num, chip).wait_recv()
        for k, (num, chip) in enumerate(_other_chips(x, y)):
            for a in range(self.n):
                self._copy(ins, outs, sems, k, a, num, 2 * x + y, chip).wait_send()


def _exchange_halves(halves):
    n = len(halves)

    def body(*refs):
        ins, outs = refs[:n], refs[n:2 * n]
        send_sems, recv_sems = refs[2 * n:]
        x, y, c = _position()
        copies = []
        for a in range(n):
            cp = pltpu.make_async_remote_copy(ins[a], outs[a], send_sems.at[a], recv_sems.at[a],
                                              device_id=(x, y, 1 - c), device_id_type=MESH)
            cp.start()
            copies.append(cp)
        for cp in copies:
            cp.wait()

    return pl.pallas_call(
        body, name="exchange_halves",
        in_specs=[ANY] * n, out_specs=[ANY] * n,
        out_shape=[jax.ShapeDtypeStruct(h.shape, h.dtype) for h in halves],
        scratch_shapes=[pltpu.SemaphoreType.DMA((n,)), pltpu.SemaphoreType.DMA((n,))],
    )(*halves)


def _allreduce_small(v):
    def body(v_ref, o_ref, buf, send_sems, recv_sems):
        x, y, c = _position()
        me = 4 * x + 2 * y + c
        peers = [(1 - x, y, c), (x, 1 - y, c), (x, y, 1 - c), (1 - x, 1 - y, c), (1 - x, y, 1 - c), (x, 1 - y, 1 - c),
                 (1 - x, 1 - y, 1 - c)]
        num = lambda d: 4 * d[0] + 2 * d[1] + d[2]
        buf[me] = v_ref[...]
        sends = []
        for k, peer in enumerate(peers):
            cp = pltpu.make_async_remote_copy(v_ref, buf.at[me], send_sems.at[k], recv_sems.at[k], device_id=peer,
                                              device_id_type=MESH)
            cp.start()
            sends.append(cp)
        for k, peer in enumerate(peers):
            pltpu.make_async_remote_copy(v_ref, buf.at[num(peer)], send_sems.at[k], recv_sems.at[k], device_id=peer,
                                         device_id_type=MESH).wait_recv()
        for cp in sends:
            cp.wait_send()
        total = buf[0]
        for d in range(1, 8):
            total = total + buf[d]
        o_ref[...] = total

    return pl.pallas_call(
        body, name="allreduce_small",
        in_specs=[pl.BlockSpec(memory_space=pltpu.VMEM)], out_specs=pl.BlockSpec(memory_space=pltpu.VMEM),
        out_shape=jax.ShapeDtypeStruct(v.shape, v.dtype),
        scratch_shapes=[pltpu.VMEM((8,) + v.shape, v.dtype), pltpu.SemaphoreType.DMA((7,)),
                        pltpu.SemaphoreType.DMA((7,))],
    )(v)


def _sum_leading(a, name):
    k, r, c = a.shape
    tr = min(r, 256)

    def body(a_ref, o_ref):
        total = a_ref[0].astype(F32)
        for i in range(1, k):
            total = total + a_ref[i].astype(F32)
        o_ref[...] = total

    return pl.pallas_call(
        body, name=name, grid=(r // tr,),
        in_specs=[pl.BlockSpec((k, tr, c), lambda i: (0, i, 0))],
        out_specs=pl.BlockSpec((tr, c), lambda i: (i, 0)),
        out_shape=jax.ShapeDtypeStruct((r, c), F32),
        compiler_params=_cparams(("parallel",)),
    )(a)


def _add(a, b, name):
    k, r, c = a.shape
    tr = min(r, 256)
    spec = pl.BlockSpec((k, tr, c), lambda i: (0, i, 0))

    def body(a_ref, b_ref, o_ref):
        o_ref[...] = (a_ref[...] + b_ref[...]).astype(BF16)

    return pl.pallas_call(
        body, name=name, grid=(r // tr,), in_specs=[spec, spec], out_specs=spec,
        out_shape=jax.ShapeDtypeStruct(a.shape, BF16), compiler_params=_cparams(("parallel",)),
    )(a, b)


def _adamw(w, g, m, v, name):
    r, c = w.shape
    tr = min(r, 256)
    spec = pl.BlockSpec((tr, c), lambda i: (i, 0))

    def body(w_ref, g_ref, m_ref, v_ref, d_ref, nm_ref, nv_ref):
        gv = g_ref[...]
        nm = ADAM_B1 * m_ref[...] + (1.0 - ADAM_B1) * gv
        nv = ADAM_B2 * v_ref[...] + (1.0 - ADAM_B2) * jnp.square(gv)
        m_hat = nm / (1.0 - ADAM_B1 ** ADAM_STEP)
        v_hat = nv / (1.0 - ADAM_B2 ** ADAM_STEP)
        d_ref[...] = -ADAM_LR * (m_hat / (jnp.sqrt(v_hat) + ADAM_EPS) + ADAM_WD * w_ref[...])
        nm_ref[...] = nm
        nv_ref[...] = nv

    return pl.pallas_call(
        body, name=name, grid=(r // tr,), in_specs=[spec] * 4, out_specs=[spec] * 3,
        out_shape=[jax.ShapeDtypeStruct(w.shape, F32)] * 3, compiler_params=_cparams(("parallel",)),
    )(w, g, m, v)


MATRICES = ("w_in", "w_out", "w_ff1", "w_ff2", "w_ple_gate", "w_ple_proj")
COLUMN_SHARDED = ("w_in", "w_ff1", "w_ple_proj")
SMALL = ("g_attn_pre", "g_q", "g_k", "g_out_a", "g_out_b", "g_attn_post", "rel_bias", "g_mlp_pre", "g_mlp_post",
         "g_ple")
WEIGHT_ORDER = ("w_in", "g_attn_pre", "g_q", "g_k", "g_out_a", "g_out_b", "w_out", "g_attn_post", "rel_bias",
                "g_mlp_pre", "w_ff1", "w_ff2", "g_mlp_post", "g_ple", "w_ple_gate", "w_ple_proj")
PACK_ROWS, PACK_COLS = 8, 1024


def _chip():
    return 2 * lax.axis_index("x") + lax.axis_index("y")


def _whole(name, gathered, mine):
    g = lax.dynamic_update_slice_in_dim(gathered, mine[None], _chip(), axis=0)
    if name in COLUMN_SHARDED:
        return g.transpose(1, 0, 2).reshape(g.shape[1], N_CHIPS * g.shape[2])
    return g.reshape(N_CHIPS * g.shape[1], g.shape[2])


def _pair_sums(names, grads):
    by_chip = []
    for n, g in zip(names, grads):
        if n in COLUMN_SHARDED:
            by_chip.append(g.reshape(g.shape[0], N_CHIPS, g.shape[1] // N_CHIPS).transpose(1, 0, 2))
        else:
            by_chip.append(g.reshape(N_CHIPS, g.shape[0] // N_CHIPS, g.shape[1]))
    c = lax.axis_index("c")
    pairs = []
    for n, g, other in zip(names, by_chip, _send_sibling_half(by_chip, names[0])):
        half = g.shape[1] // 2
        pairs.append(_add(lax.dynamic_slice_in_dim(g, c * half, half, axis=1), other, "pair_sum_" + n))
    return pairs


def _chip_sums(names, pairs, scattered):
    halves = []
    for n, pair, got in zip(names, pairs, scattered):
        own = lax.dynamic_slice_in_dim(pair, _chip(), 1, axis=0)
        halves.append(_sum_leading(lax.dynamic_update_slice_in_dim(got, own, _chip(), axis=0), "chip_sum_" + n))
    return halves


def _pack_small(values, extra=None):
    flat = [values[n].reshape(-1) for n in SMALL]
    used = sum(f.shape[0] for f in flat)
    tail = jnp.zeros((PACK_ROWS * PACK_COLS - used - 1,), F32)
    last = jnp.zeros((1,), F32) if extra is None else extra.reshape(1)
    return jnp.concatenate(flat + [tail, last]).reshape(PACK_ROWS, PACK_COLS)


def _unpack_small(packed, like):
    flat = packed.reshape(-1)
    out, o = {}, 0
    for n in SMALL:
        size = like[n].size
        out[n] = flat[o:o + size].reshape(like[n].shape)
        o += size
    return out, flat[-1]


def kernel(x, p, w_in, g_attn_pre, g_q, g_k, g_out_a, g_out_b, w_out, g_attn_post, rel_bias, g_mlp_pre, w_ff1, w_ff2, g_mlp_post, g_ple, w_ple_gate, w_ple_proj, loss_target, m_w_in, m_g_attn_pre, m_g_q, m_g_k, m_g_out_a, m_g_out_b, m_w_out, m_g_attn_post, m_rel_bias, m_g_mlp_pre, m_w_ff1, m_w_ff2, m_g_mlp_post, m_g_ple, m_w_ple_gate, m_w_ple_proj, v_w_in, v_g_attn_pre, v_g_q, v_g_k, v_g_out_a, v_g_out_b, v_w_out, v_g_attn_post, v_rel_bias, v_g_mlp_pre, v_w_ff1, v_w_ff2, v_g_mlp_post, v_g_ple, v_w_ple_gate, v_w_ple_proj):
    given = dict(locals())
    weights = {n: given[n] for n in WEIGHT_ORDER}
    shards = {n: weights[n][0] for n in MATRICES}

    c = lax.axis_index("c")
    own = dict(zip(MATRICES, _cast_shards([shards[n] for n in MATRICES])))
    w_in_whole = _whole("w_in", _gather_weights([own["w_in"]])[0], own["w_in"])

    loss, grad_x, grads = _local_step(
        x[0], p[0, 0], loss_target[0], w_in_whole, [own[n] for n in LATE], g_attn_pre, g_q, g_k, g_out_a, g_out_b,
        g_attn_post, rel_bias, g_mlp_pre, g_mlp_post, g_ple)

    pairs = _pair_sums(["w_in"], [grads["w_in"]])
    grads["w_in"] = _chip_sums(["w_in"], pairs, _scatter_to_chips(pairs))[0]
    halves = [grads[n] for n in MATRICES]
    grad_w = {}
    for n, mine, theirs in zip(MATRICES, halves, _exchange_halves(halves)):
        half = mine.shape[0]
        g = jnp.zeros((2 * half, mine.shape[1]), F32)
        g = lax.dynamic_update_slice_in_dim(g, mine, c * half, axis=0)
        grad_w[n] = lax.dynamic_update_slice_in_dim(g, theirs, (1 - c) * half, axis=0)

    small_like = {n: weights[n] for n in SMALL}
    reduced = _allreduce_small(_pack_small({n: grads[n] for n in SMALL}, extra=loss))
    grad_small, loss_total = _unpack_small(reduced, small_like)

    delta, new_m, new_v = {}, {}, {}
    for n in MATRICES:
        d, nm, nv = _adamw(shards[n], grad_w[n], given["m_" + n][0], given["v_" + n][0], "adamw_" + n)
        delta[n], new_m[n], new_v[n] = d[None], nm[None], nv[None]
        grad_w[n] = grad_w[n][None]
    d, nm, nv = _adamw(_pack_small(small_like), reduced, _pack_small({n: given["m_" + n] for n in SMALL}),
                       _pack_small({n: given["v_" + n] for n in SMALL}), "adamw_small")
    d_small, nm_small, nv_small = (_unpack_small(a, small_like)[0] for a in (d, nm, nv))
    for n in SMALL:
        grad_w[n], delta[n], new_m[n], new_v[n] = grad_small[n], d_small[n], nm_small[n], nv_small[n]

    return (loss_total, grad_x[None], *[grad_w[n] for n in WEIGHT_ORDER], *[delta[n] for n in WEIGHT_ORDER],
            *[new_m[n] for n in WEIGHT_ORDER], *[new_v[n] for n in WEIGHT_ORDER])
```

```python
import functools
import math

import jax
import jax.numpy as jnp
from jax import lax
from jax.experimental import pallas as pl
from jax.experimental.pallas import tpu as pltpu

F32 = jnp.float32
BF16 = jnp.bfloat16

D_MODEL = 1024
HEAD_DIM = 64
N_HEADS_A = 8
N_KV_A = 2
GROUP_A = N_HEADS_A // N_KV_A
N_HEADS_B = 8
D_A = N_HEADS_A * HEAD_DIM
D_KV_A = N_KV_A * HEAD_DIM
D_B = N_HEADS_B * HEAD_DIM
D_IN = D_A + 2 * D_KV_A + 3 * D_B
D_FF = 4 * D_MODEL
D_PLE = 256
GRID_W = 64
ROPE_THETA = 10000.0
DILATIONS = (1, 4, 16)
HALF_WIN = 64
N_BUCKETS = 32
MAX_DISTANCE = 1024
EPS = 1e-6
NEG_BIG = -1e30
Q_SCALE = HEAD_DIM ** -0.5

ADAM_LR = 0.001
ADAM_B1 = 0.9
ADAM_B2 = 0.999
ADAM_EPS = 1e-08
ADAM_WD = 0.01
ADAM_STEP = 10

N_CHIPS = 4
MESH = pl.DeviceIdType.MESH

ROW_TILE = 512
ATT_TQ = 256
ATT_TQ_BWD = 512
ATT_TK_FWD = 2048
ATT_TK_BWD = 512
SWA_TQ = 128
SWA_MIN_BLOCK = 1024
DW_TS = 1024
VMEM_LIMIT = 56 * 1024 * 1024

NT = (((1,), (1,)), ((), ()))
TN = (((0,), (0,)), ((), ()))


def _cparams(sem=None, vmem=VMEM_LIMIT):
    return pltpu.CompilerParams(dimension_semantics=sem, vmem_limit_bytes=vmem)


def _full(shape):
    n = len(shape)
    return pl.BlockSpec(shape, lambda *_: (0,) * n)


def _rows(tm, width):
    return pl.BlockSpec((tm, width), lambda i: (i, 0))


def _split3(a):
    a1 = a.astype(BF16)
    r = a - a1.astype(F32)
    a2 = r.astype(BF16)
    a3 = (r - a2.astype(F32)).astype(BF16)
    return a1, a2, a3


def _xdot(a, sel):
    a1, a2, a3 = _split3(a)
    d = lambda p: jnp.dot(p, sel, preferred_element_type=F32)
    return d(a1) + d(a2) + d(a3)


def _mm(a, b):
    return jnp.dot(a, b, preferred_element_type=F32)


def _mm_nt(a, b):
    return lax.dot_general(a, b, NT, preferred_element_type=F32)


def _mm_tn(a, b):
    return lax.dot_general(a, b, TN, preferred_element_type=F32)


def _rms_stats(x):
    r = lax.rsqrt(jnp.mean(x * x, axis=-1, keepdims=True) + EPS)
    return x * r, r


def _rms_bwd(dy, xh, r, g):
    gdy = dy * g
    dx = r * (gdy - xh * jnp.mean(gdy * xh, axis=-1, keepdims=True))
    dg = jnp.sum(dy * xh, axis=0, keepdims=True)
    return dx, dg


def _acc_out(ref, val):
    @pl.when(pl.program_id(0) == 0)
    def _():
        ref[...] = jnp.zeros_like(ref)

    ref[...] += val


def _swap_halves(x, first_half):
    return jnp.where(first_half, pltpu.roll(x, 96, 1), pltpu.roll(x, 32, 1))


def _first_half_mask(shape):
    return (lax.broadcasted_iota(jnp.int32, shape, 1) % HEAD_DIM) < (HEAD_DIM // 2)


def _rope_tables(s_len):
    t = jnp.arange(s_len)
    row = (t // GRID_W).astype(F32)
    col = (t % GRID_W).astype(F32)
    n_axis = HEAD_DIM // 4
    inv_freq = ROPE_THETA ** (-jnp.arange(n_axis, dtype=F32) / n_axis)
    ang = jnp.concatenate([row[:, None] * inv_freq, col[:, None] * inv_freq], axis=-1)
    c, s = jnp.cos(ang), jnp.sin(ang)
    cc = jnp.concatenate([c, c, c, c], axis=-1)
    ss = jnp.concatenate([-s, s, -s, s], axis=-1)
    return cc, ss


def _group_ones(width):
    i = jnp.arange(width)
    return (i[:, None] // HEAD_DIM == i[None, :] // HEAD_DIM).astype(BF16)


def _t5_bucket(rel):
    nb = N_BUCKETS // 2
    max_exact = nb // 2
    side = jnp.where(rel > 0, nb, 0)
    n = jnp.abs(rel)
    large = max_exact + (jnp.log(jnp.maximum(n, max_exact).astype(F32) / max_exact)
                         / math.log(MAX_DISTANCE / max_exact) * (nb - max_exact)).astype(jnp.int32)
    large = jnp.minimum(large, nb - 1)
    return side + jnp.where(n < max_exact, n, large)


def _in_window(tq):
    qi = jnp.arange(tq)
    kj = jnp.arange(tq + 2 * HALF_WIN)
    return jnp.abs(kj[None, :] - HALF_WIN - qi[:, None]) <= HALF_WIN


def _bucket_onehot(tq, dilation):
    qi = jnp.arange(tq)
    kj = jnp.arange(tq + 2 * HALF_WIN)
    rel = kj[None, :] - HALF_WIN - qi[:, None]
    bucket = _t5_bucket(rel * dilation).reshape(-1)
    return (bucket[:, None] == jnp.arange(128)[None, :]).astype(BF16)


def _in_proj(x, g1, w_in, cc, ss, gq2, gk2, ones128):
    s_len = x.shape[0]
    tm = min(ROW_TILE, s_len)

    def body(x_ref, g_ref, w_ref, cc_ref, ss_ref, gq_ref, gk_ref, one_ref,
             xn_ref, qpre_ref, kpre_ref, qa_ref, kv_ref, qb_ref, kb_ref, vb_ref):
        xh, _ = _rms_stats(x_ref[...])
        xn = (xh * g_ref[...]).astype(BF16)
        xn_ref[...] = xn
        proj = _mm(xn, w_ref[...])
        first_half = _first_half_mask((tm, 128))
        ones = one_ref[...]
        cc_t, ss_t = cc_ref[...], ss_ref[...]

        def norm_rope(xc, gain):
            ms = _xdot(xc * xc, ones) * (1.0 / HEAD_DIM)
            y = xc * lax.rsqrt(ms + EPS) * gain
            return y * cc_t + _swap_halves(y, first_half) * ss_t

        qpre_ref[...] = proj[:, :D_A]
        kpre_ref[...] = proj[:, D_A:D_A + D_KV_A]
        for c in range(D_A // 128):
            y = norm_rope(proj[:, 128 * c:128 * (c + 1)], gq_ref[...])
            qa_ref[:, 128 * c:128 * (c + 1)] = (y * Q_SCALE).astype(BF16)
        ka = norm_rope(proj[:, D_A:D_A + D_KV_A], gk_ref[...])
        o = D_A + D_KV_A
        va = proj[:, o:o + D_KV_A]
        low = _low_lanes(tm)
        kv_ref[0] = jnp.where(low, ka, pltpu.roll(va, HEAD_DIM, 1)).astype(BF16)
        kv_ref[1] = jnp.where(low, pltpu.roll(ka, HEAD_DIM, 1), va).astype(BF16)
        o += D_KV_A
        qb_ref[...] = (proj[:, o:o + D_B] * Q_SCALE).astype(BF16)
        kb_ref[...] = proj[:, o + D_B:o + 2 * D_B].astype(BF16)
        vb_ref[...] = proj[:, o + 2 * D_B:o + 3 * D_B].astype(BF16)

    sds = jax.ShapeDtypeStruct
    return pl.pallas_call(
        body, name="in_proj", grid=(s_len // tm,),
        in_specs=[_rows(tm, D_MODEL), _full((1, D_MODEL)), _full((D_MODEL, D_IN)), _rows(tm, 128), _rows(tm, 128),
                  _full((1, 128)), _full((1, 128)), _full((128, 128))],
        out_specs=[_rows(tm, D_MODEL), _rows(tm, D_A), _rows(tm, D_KV_A), _rows(tm, D_A),
                   pl.BlockSpec((N_KV_A, tm, 128), lambda i: (0, i, 0)), _rows(tm, D_B), _rows(tm, D_B),
                   _rows(tm, D_B)],
        out_shape=[sds((s_len, D_MODEL), BF16), sds((s_len, D_A), F32), sds((s_len, D_KV_A), F32),
                   sds((s_len, D_A), BF16), sds((N_KV_A, s_len, 128), BF16),
                   sds((s_len, D_B), BF16), sds((s_len, D_B), BF16), sds((s_len, D_B), BF16)],
        compiler_params=_cparams(("parallel",)),
    )(x, g1, w_in, cc, ss, gq2, gk2, ones128)


def _stat_spec(tm):
    return pl.BlockSpec((N_HEADS_B, tm, 1), lambda i: (0, i, 0))


def _merge_b(outs, lses):
    s_len = outs[0].shape[0]
    tm = min(ROW_TILE, s_len)

    def body(o0, o1, o2, l0, l1, l2, yb_ref, lse_ref):
        m_all = jnp.maximum(jnp.maximum(l0[...], l1[...]), l2[...])
        w = [jnp.exp(l[...] - m_all) for l in (l0, l1, l2)]
        den = w[0] + w[1] + w[2]
        yb_ref[...] = (w[0] * o0[...] + w[1] * o1[...] + w[2] * o2[...]) / den
        lse_ref[...] = m_all + jnp.log(den)

    return pl.pallas_call(
        body, name="merge_b", grid=(s_len // tm,),
        in_specs=[_rows(tm, D_B)] * 6,
        out_specs=[_rows(tm, D_B), _rows(tm, D_B)],
        out_shape=[jax.ShapeDtypeStruct((s_len, D_B), F32), jax.ShapeDtypeStruct((s_len, D_B), F32)],
        compiler_params=_cparams(("parallel",)),
    )(*outs, *lses)


def _out_proj(ya, yb, x, g_a, g_b, w_out, g_post, g_mlp_pre):
    s_len = x.shape[0]
    tm = min(ROW_TILE, s_len)

    def body(ya_ref, yb_ref, x_ref, ga_ref, gb_ref, w_ref, gp_ref, gm_ref, ycat_ref, y2_ref, h1_ref, xn2_ref):
        ah, _ = _rms_stats(ya_ref[...])
        bh, _ = _rms_stats(yb_ref[...])
        ycat = jnp.concatenate([ah * ga_ref[...], bh * gb_ref[...]], axis=-1).astype(BF16)
        ycat_ref[...] = ycat
        y2 = _mm(ycat, w_ref[...])
        y2_ref[...] = y2
        y2h, _ = _rms_stats(y2)
        h1 = x_ref[...] + y2h * gp_ref[...]
        h1_ref[...] = h1
        h1h, _ = _rms_stats(h1)
        xn2_ref[...] = (h1h * gm_ref[...]).astype(BF16)

    sds = jax.ShapeDtypeStruct
    return pl.pallas_call(
        body, name="out_proj", grid=(s_len // tm,),
        in_specs=[_rows(tm, D_A), _rows(tm, D_B), _rows(tm, D_MODEL), _full((1, D_A)), _full((1, D_B)),
                  _full((D_MODEL, D_MODEL)), _full((1, D_MODEL)), _full((1, D_MODEL))],
        out_specs=[_rows(tm, D_MODEL)] * 4,
        out_shape=[sds((s_len, D_MODEL), BF16), sds((s_len, D_MODEL), F32), sds((s_len, D_MODEL), F32),
                   sds((s_len, D_MODEL), BF16)],
        compiler_params=_cparams(("parallel",)),
    )(ya, yb, x, g_a, g_b, w_out, g_post, g_mlp_pre)


def _ff1(xn2, w_ff1):
    s_len = xn2.shape[0]
    tm = min(ROW_TILE, s_len)

    def body(x_ref, w_ref, u_ref):
        u_ref[...] = _mm(x_ref[...], w_ref[...])

    return pl.pallas_call(
        body, name="ff1", grid=(s_len // tm,),
        in_specs=[_rows(tm, D_MODEL), _full((D_MODEL, D_FF))],
        out_specs=_rows(tm, D_FF),
        out_shape=jax.ShapeDtypeStruct((s_len, D_FF), F32),
        compiler_params=_cparams(("parallel",)),
    )(xn2, w_ff1)


def _ff2(u, w_ff2, h1, g_post, g_ple):
    s_len = u.shape[0]
    tm = min(ROW_TILE, s_len)

    def body(u_ref, w_ref, h1_ref, gp_ref, gl_ref, f2_ref, h2_ref, xn3_ref):
        f = jnp.square(jnp.maximum(u_ref[...], 0.0)).astype(BF16)
        f2 = _mm(f, w_ref[...])
        f2_ref[...] = f2
        f2h, _ = _rms_stats(f2)
        h2 = h1_ref[...] + f2h * gp_ref[...]
        h2_ref[...] = h2
        h2h, _ = _rms_stats(h2)
        xn3_ref[...] = (h2h * gl_ref[...]).astype(BF16)

    sds = jax.ShapeDtypeStruct
    return pl.pallas_call(
        body, name="ff2", grid=(s_len // tm,),
        in_specs=[_rows(tm, D_FF), _full((D_FF, D_MODEL)), _rows(tm, D_MODEL), _full((1, D_MODEL)),
                  _full((1, D_MODEL))],
        out_specs=[_rows(tm, D_MODEL)] * 3,
        out_shape=[sds((s_len, D_MODEL), F32), sds((s_len, D_MODEL), F32), sds((s_len, D_MODEL), BF16)],
        compiler_params=_cparams(("parallel",)),
    )(u, w_ff2, h1, g_post, g_ple)


def _ple_loss(xn3, p, h2, f2, tgt, w_gate, w_ple, g_ple, g_mlp_post):
    s_len = h2.shape[0]
    tm = min(ROW_TILE, s_len)

    def body(xn3_ref, p_ref, h2_ref, f2_ref, t_ref, wg_ref, wp_ref, gl_ref, gp_ref,
             dh2_ref, df2_ref, dgl_ref, dpp_ref, loss_ref, dgple_ref, dgpost_ref):
        gate = jax.nn.sigmoid(_mm(xn3_ref[...], wg_ref[...]))
        pp = _mm(p_ref[...].astype(BF16), wp_ref[...])
        h2 = h2_ref[...]
        err = h2 + gate * pp - t_ref[...]
        sq = jnp.sum(jnp.sum(err * err, axis=1, keepdims=True), axis=0, keepdims=True)
        _acc_out(loss_ref, sq * (0.5 / D_MODEL))
        dh3 = err * (1.0 / D_MODEL)
        dgl = (dh3 * pp) * gate * (1.0 - gate)
        dgl_b = dgl.astype(BF16)
        dgl_ref[...] = dgl_b
        dpp_ref[...] = (dh3 * gate).astype(BF16)
        dxn3 = _mm_nt(dgl_b, wg_ref[...])
        h2h, r2 = _rms_stats(h2)
        dx, dg = _rms_bwd(dxn3, h2h, r2, gl_ref[...])
        _acc_out(dgple_ref, dg)
        dh2 = dh3 + dx
        dh2_ref[...] = dh2
        f2h, rf = _rms_stats(f2_ref[...])
        df2, dg = _rms_bwd(dh2, f2h, rf, gp_ref[...])
        _acc_out(dgpost_ref, dg)
        df2_ref[...] = df2.astype(BF16)

    sds = jax.ShapeDtypeStruct
    return pl.pallas_call(
        body, name="ple_loss", grid=(s_len // tm,),
        in_specs=[_rows(tm, D_MODEL), _rows(tm, D_PLE), _rows(tm, D_MODEL), _rows(tm, D_MODEL), _rows(tm, D_MODEL),
                  _full((D_MODEL, D_MODEL)), _full((D_PLE, D_MODEL)), _full((1, D_MODEL)), _full((1, D_MODEL))],
        out_specs=[_rows(tm, D_MODEL)] * 3 + [_rows(tm, D_MODEL), _full((1, 1)), _full((1, D_MODEL)),
                                              _full((1, D_MODEL))],
        out_shape=[sds((s_len, D_MODEL), F32), sds((s_len, D_MODEL), BF16), sds((s_len, D_MODEL), BF16),
                   sds((s_len, D_MODEL), BF16), sds((1, 1), F32), sds((1, D_MODEL), F32), sds((1, D_MODEL), F32)],
        compiler_params=_cparams(("arbitrary",)),
    )(xn3, p, h2, f2, tgt, w_gate, w_ple, g_ple, g_mlp_post)


def _ff2_bwd(df2, w_ff2, u):
    s_len = u.shape[0]
    tm = min(ROW_TILE, s_len)

    def body(d_ref, w_ref, u_ref, du_ref):
        df = _mm_nt(d_ref[...], w_ref[...])
        du_ref[...] = (df * (2.0 * jnp.maximum(u_ref[...], 0.0))).astype(BF16)

    return pl.pallas_call(
        body, name="ff2_bwd", grid=(s_len // tm,),
        in_specs=[_rows(tm, D_MODEL), _full((D_FF, D_MODEL)), _rows(tm, D_FF)],
        out_specs=_rows(tm, D_FF),
        out_shape=jax.ShapeDtypeStruct((s_len, D_FF), BF16),
        compiler_params=_cparams(("parallel",)),
    )(df2, w_ff2, u)


def _ff1_bwd(du, w_ff1, dh2, h1, y2, g_mlp_pre, g_post):
    s_len = du.shape[0]
    tm = min(ROW_TILE, s_len)

    def body(du_ref, w_ref, dh2_ref, h1_ref, y2_ref, gm_ref, gp_ref, dh1_ref, dy2_ref, dgm_ref, dgp_ref):
        dxn2 = _mm_nt(du_ref[...], w_ref[...])
        h1h, r1 = _rms_stats(h1_ref[...])
        dx, dg = _rms_bwd(dxn2, h1h, r1, gm_ref[...])
        _acc_out(dgm_ref, dg)
        dh1 = dh2_ref[...] + dx
        dh1_ref[...] = dh1
        y2h, ry = _rms_stats(y2_ref[...])
        dy2, dg = _rms_bwd(dh1, y2h, ry, gp_ref[...])
        _acc_out(dgp_ref, dg)
        dy2_ref[...] = dy2.astype(BF16)

    sds = jax.ShapeDtypeStruct
    return pl.pallas_call(
        body, name="ff1_bwd", grid=(s_len // tm,),
        in_specs=[_rows(tm, D_FF), _full((D_MODEL, D_FF)), _rows(tm, D_MODEL), _rows(tm, D_MODEL),
                  _rows(tm, D_MODEL), _full((1, D_MODEL)), _full((1, D_MODEL))],
        out_specs=[_rows(tm, D_MODEL), _rows(tm, D_MODEL), _full((1, D_MODEL)), _full((1, D_MODEL))],
        out_shape=[sds((s_len, D_MODEL), F32), sds((s_len, D_MODEL), BF16), sds((1, D_MODEL), F32),
                   sds((1, D_MODEL), F32)],
        compiler_params=_cparams(("arbitrary",)),
    )(du, w_ff1, dh2, h1, y2, g_mlp_pre, g_post)


def _out_proj_bwd(dy2, w_out, ya, yb, lse_b, g_a, g_b):
    s_len = ya.shape[0]
    tm = min(ROW_TILE, s_len)

    def body(d_ref, w_ref, ya_ref, yb_ref, lse_ref, ga_ref, gb_ref, dya_ref, dyb_ref, da_ref, st_ref, dga_ref,
             dgb_ref):
        dycat = _mm_nt(d_ref[...], w_ref[...])
        lane = lax.broadcasted_iota(jnp.int32, (tm, 128), 1)
        low = lane < HEAD_DIM
        is_lse = (lane % HEAD_DIM) < (HEAD_DIM // 2)

        def head_sums(prod_chunk):
            return (jnp.sum(jnp.where(low, prod_chunk, 0.0), axis=1, keepdims=True),
                    jnp.sum(jnp.where(low, 0.0, prod_chunk), axis=1, keepdims=True))

        ya = ya_ref[...]
        yh, r = _rms_stats(ya)
        dya, dg = _rms_bwd(dycat[:, :D_A], yh, r, ga_ref[...])
        _acc_out(dga_ref, dg)
        dya_ref[...] = dya
        prod = dya * ya
        for c in range(D_A // 128):
            da_ref[2 * c], da_ref[2 * c + 1] = head_sums(prod[:, 128 * c:128 * (c + 1)])

        yb = yb_ref[...]
        yh, r = _rms_stats(yb)
        dyb, dg = _rms_bwd(dycat[:, D_A:], yh, r, gb_ref[...])
        _acc_out(dgb_ref, dg)
        dyb_ref[...] = dyb.astype(BF16)
        prod = dyb * yb
        for c in range(D_B // 128):
            sl = slice(128 * c, 128 * (c + 1))
            d_lo, d_hi = head_sums(prod[:, sl])
            st_ref[:, sl] = jnp.where(is_lse, lse_ref[:, sl], jnp.where(low, d_lo, d_hi))

    sds = jax.ShapeDtypeStruct
    return pl.pallas_call(
        body, name="out_proj_bwd", grid=(s_len // tm,),
        in_specs=[_rows(tm, D_MODEL), _full((D_MODEL, D_MODEL)), _rows(tm, D_A), _rows(tm, D_B), _rows(tm, D_B),
                  _full((1, D_A)), _full((1, D_B))],
        out_specs=[_rows(tm, D_A), _rows(tm, D_B), _stat_spec(tm), _rows(tm, D_B), _full((1, D_A)),
                   _full((1, D_B))],
        out_shape=[sds((s_len, D_A), F32), sds((s_len, D_B), BF16), sds((N_HEADS_A, s_len, 1), F32),
                   sds((s_len, D_B), F32), sds((1, D_A), F32), sds((1, D_B), F32)],
        compiler_params=_cparams(("arbitrary",)),
    )(dy2, w_out, ya, yb, lse_b, g_a, g_b)


def _in_proj_bwd(dqr, dkv, dqb, dkb, dvb, qpre, kpre, x, dh1, g1, w_in, cc, ss, gq2, gk2, ones128):
    s_len = x.shape[0]
    tm = min(ROW_TILE // 2, s_len)

    def body(dqr_ref, dkv_ref, dq0, dq1, dq2, dk0, dk1, dk2, dv0, dv1, dv2, qpre_ref, kpre_ref, x_ref,
             dh1_ref, g_ref, w_ref, cc_ref, ss_ref, gq_ref, gk_ref, one_ref, dproj_ref, gx_ref, dg1_ref, dgq_ref,
             dgk_ref):
        low = _low_lanes(tm)
        dkr = jnp.where(low, dkv_ref[0], pltpu.roll(dkv_ref[1], HEAD_DIM, 1))
        dva = jnp.where(low, pltpu.roll(dkv_ref[0], HEAD_DIM, 1), dkv_ref[1])
        first_half = _first_half_mask((tm, 128))
        ones = one_ref[...]
        cc_t, ss_t = cc_ref[...], ss_ref[...]

        def norm_rope_bwd(dy, xc, gain):
            dn = dy * cc_t - _swap_halves(dy, first_half) * ss_t
            r = lax.rsqrt(_xdot(xc * xc, ones) * (1.0 / HEAD_DIM) + EPS)
            xh = xc * r
            gdy = dn * gain
            dx = r * (gdy - xh * (_xdot(gdy * xh, ones) * (1.0 / HEAD_DIM)))
            return dx, jnp.sum(dn * xh, axis=0, keepdims=True)

        dgq = jnp.zeros((1, 128), F32)
        parts = []
        for c in range(D_A // 128):
            sl = slice(128 * c, 128 * (c + 1))
            dx, dg = norm_rope_bwd(dqr_ref[:, sl] * Q_SCALE, qpre_ref[:, sl], gq_ref[...])
            parts.append(dx)
            dgq = dgq + dg
        dxk, dgk = norm_rope_bwd(dkr, kpre_ref[...], gk_ref[...])
        _acc_out(dgq_ref, dgq)
        _acc_out(dgk_ref, dgk)
        parts += [dxk, dva, (dq0[...] + dq1[...] + dq2[...]) * Q_SCALE, dk0[...] + dk1[...] + dk2[...],
                  dv0[...] + dv1[...] + dv2[...]]
        dproj = jnp.concatenate(parts, axis=-1).astype(BF16)
        dproj_ref[...] = dproj
        dxn = _mm_nt(dproj, w_ref[...])
        xh, r = _rms_stats(x_ref[...])
        dx, dg = _rms_bwd(dxn, xh, r, g_ref[...])
        _acc_out(dg1_ref, dg)
        gx_ref[...] = dh1_ref[...] + dx

    sds = jax.ShapeDtypeStruct
    return pl.pallas_call(
        body, name="in_proj_bwd", grid=(s_len // tm,),
        in_specs=[_rows(tm, D_A), pl.BlockSpec((N_KV_A, tm, 128), lambda i: (0, i, 0))] + [_rows(tm, D_B)] * 9
                 + [_rows(tm, D_A), _rows(tm, D_KV_A), _rows(tm, D_MODEL), _rows(tm, D_MODEL),
                    _full((1, D_MODEL)), _full((D_MODEL, D_IN)), _rows(tm, 128), _rows(tm, 128), _full((1, 128)),
                    _full((1, 128)), _full((128, 128))],
        out_specs=[_rows(tm, D_IN), _rows(tm, D_MODEL), _full((1, D_MODEL)), _full((1, 128)), _full((1, 128))],
        out_shape=[sds((s_len, D_IN), BF16), sds((s_len, D_MODEL), F32), sds((1, D_MODEL), F32),
                   sds((1, 128), F32), sds((1, 128), F32)],
        compiler_params=_cparams(("arbitrary",)),
    )(dqr, dkv, *dqb, *dkb, *dvb, qpre, kpre, x, dh1, g1, w_in, cc, ss, gq2, gk2, ones128)


def _dw(a, b, name, relu2=False):
    s_len, ka = a.shape
    n = b.shape[1]
    ts = min(DW_TS, s_len)
    bk = min(ka, 1024)
    bn = n if n % 1024 else 1024

    def body(a_ref, b_ref, o_ref):
        @pl.when(pl.program_id(2) == 0)
        def _():
            o_ref[...] = jnp.zeros_like(o_ref)

        av = a_ref[...]
        if relu2:
            av = jnp.square(jnp.maximum(av, 0.0))
        o_ref[...] += _mm_tn(av.astype(BF16), b_ref[...])

    return pl.pallas_call(
        body, name=name, grid=(ka // bk, n // bn, s_len // ts),
        in_specs=[pl.BlockSpec((ts, bk), lambda i, j, k: (k, i)), pl.BlockSpec((ts, bn), lambda i, j, k: (k, j))],
        out_specs=pl.BlockSpec((bk, bn), lambda i, j, k: (i, j)),
        out_shape=jax.ShapeDtypeStruct((ka, n), F32),
        compiler_params=_cparams(("parallel", "parallel", "arbitrary")),
    )(a, b)


def _stack_heads(block, low, data_low):
    parts = []
    for c in range(GROUP_A // 2):
        chunk = block[:, 128 * c:128 * (c + 1)]
        swapped = pltpu.roll(chunk, HEAD_DIM, 1)
        for h_low in (chunk, swapped) if data_low else (swapped, chunk):
            parts.append(jnp.where(low, h_low, 0.0) if data_low else jnp.where(low, 0.0, h_low))
    return jnp.concatenate(parts, axis=0).astype(BF16)


def _unstack_heads(stacked, low, tq, data_low):
    chunks = []
    for c in range(GROUP_A // 2):
        even = stacked[2 * c * tq:(2 * c + 1) * tq]
        odd = stacked[(2 * c + 1) * tq:(2 * c + 2) * tq]
        if data_low:
            chunks.append(jnp.where(low, even, pltpu.roll(odd, HEAD_DIM, 1)))
        else:
            chunks.append(jnp.where(low, pltpu.roll(even, HEAD_DIM, 1), odd))
    return chunks


def _attn_a_fwd(qa, kv):
    s_len = kv.shape[1]
    tq = min(ATT_TQ, s_len)
    tk = min(ATT_TK_FWD, s_len)
    rows = GROUP_A * tq

    def body(q_ref, kv_ref, o_ref, lse_ref):
        low = _low_lanes(tq)
        low_k = _low_lanes(tk)
        q = _stack_heads(q_ref[...].astype(F32), low, data_low=True)

        def step(j, carry):
            m, acc = carry
            kvj = kv_ref[0, pl.ds(pl.multiple_of(j * tk, tk), tk), :]
            s = _mm_nt(q, kvj)
            m_new = jnp.maximum(m, jnp.max(s, axis=1, keepdims=True))
            p = jnp.exp(s - m_new).astype(BF16)
            acc = jnp.exp(m - m_new) * acc + _mm(p, jnp.where(low_k, jnp.ones_like(kvj), kvj))
            return m_new, acc

        init = (jnp.full((rows, 1), -jnp.inf, F32), jnp.zeros((rows, 128), F32))
        m, acc = lax.fori_loop(0, s_len // tk, step, init)
        for c, chunk in enumerate(_unstack_heads(acc / pltpu.roll(acc, HEAD_DIM, 1), low, tq, data_low=False)):
            o_ref[:, 128 * c:128 * (c + 1)] = chunk
        lse_ref[...] = (m + jnp.log(acc[:, :1])).reshape(GROUP_A, tq, 1)

    return pl.pallas_call(
        body, name="attn_a_fwd", grid=(N_KV_A, s_len // tq),
        in_specs=[pl.BlockSpec((tq, 256), lambda g, i: (i, g)),
                  pl.BlockSpec((1, s_len, 128), lambda g, i: (g, 0, 0))],
        out_specs=[pl.BlockSpec((tq, 256), lambda g, i: (i, g)),
                   pl.BlockSpec((GROUP_A, tq, 1), lambda g, i: (g, i, 0))],
        out_shape=[jax.ShapeDtypeStruct((s_len, D_A), F32),
                   jax.ShapeDtypeStruct((N_HEADS_A, s_len, 1), F32)],
        compiler_params=_cparams(("parallel", "parallel")),
    )(qa, kv)


def _attn_a_bwd(qa, dya, kv, lse, delta):
    s_len = kv.shape[1]
    tq = min(ATT_TQ_BWD, s_len)
    tk = min(ATT_TK_BWD, s_len)
    rows = GROUP_A * tq

    def body(q_ref, do_ref, kv_ref, lse_ref, dl_ref, dq_ref, dkv_ref):
        @pl.when(pl.program_id(1) == 0)
        def _():
            dkv_ref[...] = jnp.zeros_like(dkv_ref)

        low = _low_lanes(tq)
        q = _stack_heads(q_ref[...].astype(F32), low, data_low=True)
        do = _stack_heads(do_ref[...], low, data_low=False)
        lse_t = lse_ref[...].reshape(rows, 1)
        dl_t = dl_ref[...].reshape(rows, 1)
        q_t = q.astype(F32).T.astype(BF16)
        do_t = do.astype(F32).T.astype(BF16)

        def step(j, dq):
            span = pl.ds(pl.multiple_of(j * tk, tk), tk)
            kvj = kv_ref[0, span, :]
            p = jnp.exp(_mm_nt(q, kvj) - lse_t)
            ds = (p * (_mm_nt(do, kvj) - dl_t)).astype(BF16)
            dkv_ref[0, :, span] += _mm(q_t, ds) + _mm(do_t, p.astype(BF16))
            return dq + _mm(ds, kvj)

        dq = lax.fori_loop(0, s_len // tk, step, jnp.zeros((rows, 128), F32))
        for c, chunk in enumerate(_unstack_heads(dq, low, tq, data_low=True)):
            dq_ref[:, 128 * c:128 * (c + 1)] = chunk

    return pl.pallas_call(
        body, name="attn_a_bwd", grid=(N_KV_A, s_len // tq),
        in_specs=[pl.BlockSpec((tq, 256), lambda g, i: (i, g)),
                  pl.BlockSpec((tq, 256), lambda g, i: (i, g)),
                  pl.BlockSpec((1, s_len, 128), lambda g, i: (g, 0, 0)),
                  pl.BlockSpec((GROUP_A, tq, 1), lambda g, i: (g, i, 0)),
                  pl.BlockSpec((GROUP_A, tq, 1), lambda g, i: (g, i, 0))],
        out_specs=[pl.BlockSpec((tq, 256), lambda g, i: (i, g)),
                   pl.BlockSpec((1, 128, s_len), lambda g, i: (g, 0, 0))],
        out_shape=[jax.ShapeDtypeStruct((s_len, D_A), F32),
                   jax.ShapeDtypeStruct((N_KV_A, 128, s_len), F32)],
        compiler_params=_cparams(("parallel", "arbitrary")),
    )(qa, dya, kv, lse, delta)


class _SwaGeometry:
    def __init__(self, s_len, r):
        self.r = r
        self.tq = SWA_TQ
        self.block = min(max(SWA_MIN_BLOCK, 2 * SWA_TQ * r), s_len)
        self.halo = HALF_WIN * r
        self.nsub = self.block // (self.tq * r)
        self.band = self.tq + 2 * HALF_WIN
        self.length = s_len // r
        self.nblk = s_len // self.block
        self.nhalo = s_len // self.halo
        assert self.nsub * self.tq * r == self.block and self.block % self.halo == 0

    def specs(self):
        per = self.block // self.halo
        cur = pl.BlockSpec((self.block, 128), lambda c, i: (i, c))
        prev = pl.BlockSpec((self.halo, 128), lambda c, i: (jnp.maximum(i * per - 1, 0), c))
        nxt = pl.BlockSpec((self.halo, 128), lambda c, i: (jnp.minimum((i + 1) * per, self.nhalo - 1), c))
        return prev, cur, nxt

    def tiles(self):
        return [(rho + self.r * j * self.tq, self.halo + rho + self.r * (j * self.tq - HALF_WIN), j)
                for j in range(self.nsub) for rho in range(self.r)]

    def own(self, start):
        return pl.ds(start, self.tq, stride=self.r)

    def around(self, start):
        return pl.ds(start, self.band, stride=self.r)

    def fill(self, dst, prev_ref, cur_ref, next_ref):
        dst[:self.halo, :] = prev_ref[...].astype(F32)
        dst[self.halo:self.halo + self.block, :] = cur_ref[...].astype(F32)
        dst[self.halo + self.block:, :] = next_ref[...].astype(F32)

    def first_position(self, j):
        return (pl.program_id(1) * self.block) // self.r + j * self.tq

    def outside(self, j, copies=1):
        pos = self.first_position(j) - HALF_WIN + lax.broadcasted_iota(jnp.int32, (1, copies * self.band), 1) % self.band
        return jnp.where((pos >= 0) & (pos < self.length), 0.0, NEG_BIG)

    def extended(self):
        return pltpu.VMEM((self.block + 2 * self.halo, 128), F32)

    def plain(self):
        return pltpu.VMEM((self.block, 128), F32)


def _low_lanes(rows):
    return lax.broadcasted_iota(jnp.int32, (rows, 128), 1) < HEAD_DIM


def _one_head(x, low, half):
    return jnp.where(low if half == 0 else jnp.logical_not(low), x, 0.0).astype(BF16)


def _two_heads(x, low):
    return jnp.concatenate([_one_head(x, low, 0), _one_head(x, low, 1)], axis=0)


def _carry_ride(base_body, n_in, n_out, n_scratch, ride, grid):
    if ride is None:
        return base_body
    n = ride.n

    def body(*refs):
        o = n_in + n
        ins, ride_ins = refs[:n_in], refs[n_in:o]
        outs, ride_outs = refs[o:o + n_out], refs[o + n_out:o + n_out + n]
        o += n_out + n
        scratch, sems = refs[o:o + n_scratch], refs[o + n_scratch:]
        at_first = (pl.program_id(0) == 0) & (pl.program_id(1) == 0)
        at_last = (pl.program_id(0) == grid[0] - 1) & (pl.program_id(1) == grid[1] - 1)

        @pl.when(at_first)
        def _():
            ride.start(ride_ins, ride_outs, sems)

        base_body(*ins, *outs, *scratch)

        @pl.when(at_last)
        def _():
            ride.finish(ride_ins, ride_outs, sems)

    return body


def _ride_call(base_body, name, grid, in_specs, out_specs, out_shape, scratch_shapes, operands, ride):
    n = 0 if ride is None else ride.n
    extra = [] if ride is None else ride.operands
    outs = pl.pallas_call(
        _carry_ride(base_body, len(in_specs), len(out_specs), len(scratch_shapes), ride, grid), name=name, grid=grid,
        in_specs=list(in_specs) + [ANY] * n, out_specs=list(out_specs) + [ANY] * n,
        out_shape=list(out_shape) + ([] if ride is None else ride.out_shape()),
        scratch_shapes=list(scratch_shapes) + ([] if ride is None else ride.scratch_shapes()),
        compiler_params=_cparams(("arbitrary", "arbitrary")),
    )(*operands, *extra)
    return outs[:len(out_specs)], outs[len(out_specs):]


def _swa_fwd(q, k, v, bias, r, ride=None):
    geo = _SwaGeometry(q.shape[0], r)
    prev, cur, nxt = geo.specs()

    def body(q_ref, kp, kc, kn, vp, vc, vn, b_ref, o_ref, lse_ref, qf, kf, vf):
        qf[...] = q_ref[...].astype(F32)
        geo.fill(kf, kp, kc, kn)
        geo.fill(vf, vp, vc, vn)
        tq = geo.tq
        low_q = _low_lanes(tq)
        bias = b_ref[...].reshape(2 * tq, geo.band)
        for own, around, j in geo.tiles():
            q2 = _two_heads(qf[geo.own(own), :], low_q)
            kb = kf[geo.around(around), :].astype(BF16)
            vb = vf[geo.around(around), :].astype(BF16)
            s = _mm_nt(q2, kb) + bias + geo.outside(j)
            m = jnp.max(s, axis=1, keepdims=True)
            e = jnp.exp(s - m)
            l = jnp.sum(e, axis=1, keepdims=True)
            o2 = _mm(e.astype(BF16), vb) / l
            lse2 = m + jnp.log(l)
            o_ref[geo.own(own), :] = jnp.where(low_q, o2[:tq], o2[tq:])
            lse_ref[geo.own(own), :] = jnp.where(low_q, lse2[:tq], lse2[tq:])

    sds = jax.ShapeDtypeStruct
    (o, lse), carried = _ride_call(
        body, "swa_fwd_%d" % r, (D_B // 128, geo.nblk),
        [cur, prev, cur, nxt, prev, cur, nxt, pl.BlockSpec((2, geo.tq, geo.band), lambda c, i: (c, 0, 0))],
        [cur, cur], [sds(q.shape, F32), sds(q.shape, F32)], [geo.plain(), geo.extended(), geo.extended()],
        (q, k, k, k, v, v, v, bias), ride)
    return o, lse, carried


def _head_stats(st, half):
    lo = HEAD_DIM * half
    return st[:, lo:lo + 1], st[:, lo + HEAD_DIM // 2:lo + HEAD_DIM // 2 + 1]


def _swa_bwd_q(q, k, v, dy, st, bias, r, ride=None):
    geo = _SwaGeometry(q.shape[0], r)
    prev, cur, nxt = geo.specs()
    bias_spec = pl.BlockSpec((2, geo.tq, geo.band), lambda c, i: (c, 0, 0))

    def body(q_ref, kp, kc, kn, vp, vc, vn, dy_ref, st_ref, b_ref, dq_ref, db_ref, qf, kf, vf, dyf):
        @pl.when(pl.program_id(1) == 0)
        def _():
            db_ref[...] = jnp.zeros_like(db_ref)

        qf[...] = q_ref[...].astype(F32)
        dyf[...] = dy_ref[...].astype(F32)
        geo.fill(kf, kp, kc, kn)
        geo.fill(vf, vp, vc, vn)
        tq = geo.tq
        low_q = _low_lanes(tq)
        bias = b_ref[...].reshape(2 * tq, geo.band)
        for own, around, j in geo.tiles():
            sts = st_ref[geo.own(own), :]
            (lse0, delta0), (lse1, delta1) = _head_stats(sts, 0), _head_stats(sts, 1)
            lse = jnp.concatenate([lse0, lse1], axis=0)
            delta = jnp.concatenate([delta0, delta1], axis=0)
            kb = kf[geo.around(around), :].astype(BF16)
            vb = vf[geo.around(around), :].astype(BF16)
            s = _mm_nt(_two_heads(qf[geo.own(own), :], low_q), kb) + bias + geo.outside(j)
            p = jnp.exp(s - lse)
            ds = p * (_mm_nt(_two_heads(dyf[geo.own(own), :], low_q), vb) - delta)
            db_ref[...] += ds.reshape(2, tq, geo.band)
            dq2 = _mm(ds.astype(BF16), kb)
            dq_ref[geo.own(own), :] = jnp.where(low_q, dq2[:tq], dq2[tq:])

    (dq, dbias), carried = _ride_call(
        body, "swa_bwd_q_%d" % r, (D_B // 128, geo.nblk),
        [cur, prev, cur, nxt, prev, cur, nxt, cur, cur, bias_spec], [cur, bias_spec],
        [jax.ShapeDtypeStruct(q.shape, F32), jax.ShapeDtypeStruct(bias.shape, F32)],
        [geo.plain(), geo.extended(), geo.extended(), geo.plain()], (q, k, k, k, v, v, v, dy, st, bias), ride)
    return dq, dbias, carried


def _swa_bwd_kv(q, k, v, dy, st, bias_kv, r):
    geo = _SwaGeometry(q.shape[0], r)
    prev, cur, nxt = geo.specs()

    def body(k_ref, v_ref, qp, qc, qn, dp_, dc_, dn_, sp, sc, sn, b_ref, dk_ref, dv_ref, kf, vf, qf, dyf, stf):
        kf[...] = k_ref[...].astype(F32)
        vf[...] = v_ref[...].astype(F32)
        geo.fill(qf, qp, qc, qn)
        geo.fill(dyf, dp_, dc_, dn_)
        geo.fill(stf, sp, sc, sn)
        band = geo.band
        low_b = _low_lanes(band)
        bias = jnp.concatenate([b_ref[0], b_ref[1]], axis=1)
        half_lanes = HEAD_DIM // 2
        for own, around, j in geo.tiles():
            ks = kf[geo.own(own), :].astype(BF16)
            vs = vf[geo.own(own), :].astype(BF16)
            q2 = _two_heads(qf[geo.around(around), :], low_b)
            dy2 = _two_heads(dyf[geo.around(around), :], low_b)
            st_t = stf[geo.around(around), :].T
            lse = jnp.concatenate([st_t[:1, :], st_t[HEAD_DIM:HEAD_DIM + 1, :]], axis=1)
            delta = jnp.concatenate([st_t[half_lanes:half_lanes + 1, :],
                                     st_t[HEAD_DIM + half_lanes:HEAD_DIM + half_lanes + 1, :]], axis=1)
            p = jnp.exp(_mm_nt(ks, q2) + bias + (geo.outside(j, copies=2) - lse))
            ds = p * (_mm_nt(vs, dy2) - delta)
            dv_ref[geo.own(own), :] = _mm(p.astype(BF16), dy2)
            dk_ref[geo.own(own), :] = _mm(ds.astype(BF16), q2)

    return pl.pallas_call(
        body, name="swa_bwd_kv_%d" % r, grid=(D_B // 128, geo.nblk),
        in_specs=[cur, cur, prev, cur, nxt, prev, cur, nxt, prev, cur, nxt,
                  pl.BlockSpec((2, geo.tq, geo.band), lambda c, i: (c, 0, 0))],
        out_specs=[cur, cur],
        out_shape=[jax.ShapeDtypeStruct(q.shape, F32), jax.ShapeDtypeStruct(q.shape, F32)],
        scratch_shapes=[geo.plain(), geo.plain(), geo.extended(), geo.extended(), geo.extended()],
        compiler_params=_cparams(("parallel", "parallel")),
    )(k, v, q, q, q, dy, dy, dy, st, st, st, bias_kv)


BIAS_ROWS = 16
BIAS_TN = 4096


def _bias_tiles(onehot, rel_bias_t):
    n = onehot.shape[0]

    def body(oh_ref, rb_ref, o_ref):
        o_ref[...] = sum(_mm_nt(piece, oh_ref[...]) for piece in _split3(rb_ref[...]))

    return pl.pallas_call(
        body, name="bias_tiles", grid=(n // BIAS_TN,),
        in_specs=[_rows(BIAS_TN, 128), _full((BIAS_ROWS, 128))],
        out_specs=pl.BlockSpec((BIAS_ROWS, BIAS_TN), lambda i: (0, i)),
        out_shape=jax.ShapeDtypeStruct((BIAS_ROWS, n), F32),
        compiler_params=_cparams(("parallel",)),
    )(onehot, rel_bias_t)


def _bias_bwd(onehot, dbias_rows, so_far, r):
    n = onehot.shape[0]

    def body(oh, d, prev_ref, g_ref):
        @pl.when(pl.program_id(0) == 0)
        def _():
            g_ref[...] = prev_ref[...]

        hi, lo, _ = _split3(d[...])
        g_ref[...] += _mm(hi, oh[...]) + _mm(lo, oh[...])

    return pl.pallas_call(
        body, name="bias_bwd_%d" % r, grid=(n // BIAS_TN,),
        in_specs=[_rows(BIAS_TN, 128), pl.BlockSpec((BIAS_ROWS, BIAS_TN), lambda i: (0, i)), _full((BIAS_ROWS, 128))],
        out_specs=_full((BIAS_ROWS, 128)),
        out_shape=jax.ShapeDtypeStruct((BIAS_ROWS, 128), F32),
        compiler_params=_cparams(("arbitrary",)),
    )(onehot, dbias_rows, so_far)


LATE = ("w_out", "w_ff1", "w_ff2", "w_ple_gate", "w_ple_proj")


def _local_step(x, p, tgt, w_in, late_shards, g_attn_pre, g_q, g_k, g_out_a, g_out_b, g_attn_post, rel_bias,
                g_mlp_pre, g_mlp_post, g_ple):
    s_len = x.shape[0]
    cc, ss = _rope_tables(s_len)
    gq2 = jnp.concatenate([g_q, g_q], axis=-1)
    gk2 = jnp.concatenate([g_k, g_k], axis=-1)
    ones128 = _group_ones(128)
    rel_bias_t = jnp.zeros((BIAS_ROWS, 128), F32).at[:N_HEADS_B, :N_BUCKETS].set(rel_bias.T)

    xn1, qpre, kpre, qa, kv, qb, kb, vb = _in_proj(x, g_attn_pre, w_in, cc, ss, gq2, gk2, ones128)
    ya, lse_a = _attn_a_fwd(qa, kv)

    tiles, outs, lses = [], [], []
    for r in DILATIONS:
        tq = SWA_TQ
        onehot = _bucket_onehot(tq, r)
        bias = _bias_tiles(onehot, rel_bias_t)[:N_HEADS_B].reshape(N_HEADS_B, tq, tq + 2 * HALF_WIN)
        bias = jnp.where(_in_window(tq), bias, NEG_BIG)
        o_r, lse_r, gathered = _swa_fwd(qb, kb, vb, bias, r, _GatherRide(late_shards) if r == DILATIONS[-1] else None)
        tiles.append((onehot, bias))
        outs.append(o_r)
        lses.append(lse_r)
    yb, lse_b = _merge_b(outs, lses)
    w_out, w_ff1, w_ff2, w_gate, w_ple = (_whole(n, g, mine) for n, g, mine in zip(LATE, gathered, late_shards))

    ycat, y2, h1, xn2 = _out_proj(ya, yb, x, g_out_a, g_out_b, w_out, g_attn_post, g_mlp_pre)
    u = _ff1(xn2, w_ff1)
    f2, h2, xn3 = _ff2(u, w_ff2, h1, g_mlp_post, g_ple)
    dh2, df2, dgl, dpp, loss, dg_ple, dg_mlp_post = _ple_loss(xn3, p, h2, f2, tgt, w_gate, w_ple, g_ple, g_mlp_post)

    grads = {"g_ple": dg_ple, "g_mlp_post": dg_mlp_post}
    grads["w_ple_gate"] = _dw(xn3, dgl, "dw_gate")
    grads["w_ple_proj"] = _dw(p, dpp, "dw_ple")
    grads["w_ff2"] = _dw(u, df2, "dw_ff2", relu2=True)
    du = _ff2_bwd(df2, w_ff2, u)
    grads["w_ff1"] = _dw(xn2, du, "dw_ff1")
    dh1, dy2, grads["g_mlp_pre"], grads["g_attn_post"] = _ff1_bwd(du, w_ff1, dh2, h1, y2, g_mlp_pre, g_attn_post)
    grads["w_out"] = _dw(ycat, dy2, "dw_out")
    dya, dyb, delta_a, st_b, grads["g_out_a"], grads["g_out_b"] = _out_proj_bwd(dy2, w_out, ya, yb, lse_b, g_out_a,
                                                                              g_out_b)

    pairs = _pair_sums(LATE, [grads[n] for n in LATE])

    dqr, dkv_t = _attn_a_bwd(qa, dya, kv, lse_a, delta_a)
    dkv_a = dkv_t.transpose(0, 2, 1)

    dqs, dks, dvs = [], [], []
    d_rel = jnp.zeros((BIAS_ROWS, 128), F32)
    for r, (onehot, bias) in zip(DILATIONS, tiles):
        dq_r, dbias, scattered = _swa_bwd_q(qb, kb, vb, dyb, st_b, bias, r,
                                            _ScatterRide(pairs) if r == DILATIONS[0] else None)
        if scattered:
            for n, half in zip(LATE, _chip_sums(LATE, pairs, scattered)):
                grads[n] = half
        bias_kv = jnp.flip(bias, axis=(1, 2))
        dk_r, dv_r = _swa_bwd_kv(qb, kb, vb, dyb, st_b, bias_kv, r)
        dbias_rows = jnp.pad(dbias.reshape(N_HEADS_B, -1), ((0, BIAS_ROWS - N_HEADS_B), (0, 0)))
        d_rel = _bias_bwd(onehot, dbias_rows, d_rel, r)
        dqs.append(dq_r)
        dks.append(dk_r)
        dvs.append(dv_r)
    grads["rel_bias"] = d_rel[:N_HEADS_B, :N_BUCKETS].T

    dproj, grad_x, grads["g_attn_pre"], dgq2, dgk2 = _in_proj_bwd(
        dqr, dkv_a, dqs, dks, dvs, qpre, kpre, x, dh1, g_attn_pre, w_in, cc, ss, gq2, gk2, ones128)
    grads["g_q"] = dgq2[:, :HEAD_DIM] + dgq2[:, HEAD_DIM:]
    grads["g_k"] = dgk2[:, :HEAD_DIM] + dgk2[:, HEAD_DIM:]
    grads["w_in"] = _dw(xn1, dproj, "dw_in")
    return loss, grad_x, grads


ANY = pl.BlockSpec(memory_space=pl.ANY)


def _position():
    return lax.axis_index("x"), lax.axis_index("y"), lax.axis_index("c")


def _other_chips(x, y):
    return [(2 * (1 - x) + y, (1 - x, y)), (2 * x + (1 - y), (x, 1 - y)), (2 * (1 - x) + (1 - y), (1 - x, 1 - y))]


def _cast_shards(shards):
    def body(*refs):
        n = len(refs) // 2
        for i_ref, o_ref in zip(refs[:n], refs[n:]):
            o_ref[...] = i_ref[...].astype(BF16)

    return pl.pallas_call(
        body, name="cast_shards",
        in_specs=[pl.BlockSpec(memory_space=pltpu.VMEM)] * len(shards),
        out_specs=[pl.BlockSpec(memory_space=pltpu.VMEM)] * len(shards),
        out_shape=[jax.ShapeDtypeStruct(s.shape, BF16) for s in shards],
        compiler_params=_cparams(),
    )(*shards)


def _gather_weights(shards):
    n = len(shards)

    ride = _GatherRide(shards)

    def body(*refs):
        ride.start(refs[:n], refs[n:2 * n], refs[2 * n:])
        ride.finish(refs[:n], refs[n:2 * n], refs[2 * n:])

    return pl.pallas_call(
        body, name="gather_weights",
        in_specs=[ANY] * n, out_specs=[ANY] * n,
        out_shape=ride.out_shape(), scratch_shapes=ride.scratch_shapes(),
    )(*shards)


class _GatherRide:
    def __init__(self, shards):
        self.operands = list(shards)
        self.n = len(shards)

    def out_shape(self):
        return [jax.ShapeDtypeStruct((N_CHIPS,) + s.shape, s.dtype) for s in self.operands]

    def scratch_shapes(self):
        return [pltpu.SemaphoreType.DMA((3, self.n))] * 4

    @staticmethod
    def _rows(ref, core):
        half = ref.shape[0] // 2
        return pl.ds(pl.multiple_of(core * half, 16), half)

    def _ici(self, ins, outs, sems, k, a, chip):
        x, y, c = _position()
        return pltpu.make_async_remote_copy(ins[a].at[self._rows(ins[a], c), :],
                                            outs[a].at[2 * x + y, self._rows(ins[a], c), :], sems[0].at[k, a],
                                            sems[1].at[k, a], device_id=(*chip, c), device_id_type=MESH)

    def _pass_on(self, ins, outs, sems, k, a, num, core):
        x, y, c = _position()
        half = outs[a].at[num, self._rows(ins[a], core), :]
        return pltpu.make_async_remote_copy(half, half, sems[2].at[k, a], sems[3].at[k, a], device_id=(x, y, 1 - c),
                                            device_id_type=MESH)

    def start(self, ins, outs, sems):
        x, y, _ = _position()
        for k, (_, chip) in enumerate(_other_chips(x, y)):
            for a in range(self.n):
                self._ici(ins, outs, sems, k, a, chip).start()

    def finish(self, ins, outs, sems):
        x, y, c = _position()
        others = _other_chips(x, y)
        for k, (num, chip) in enumerate(others):
            for a in range(self.n):
                landed = outs[a].at[num, self._rows(ins[a], c), :]
                pltpu.make_async_remote_copy(landed, landed, sems[0].at[k, a], sems[1].at[k, a], device_id=(*chip, c),
                                             device_id_type=MESH).wait_recv()
                self._pass_on(ins, outs, sems, k, a, num, c).start()
        for k, (num, chip) in enumerate(others):
            for a in range(self.n):
                self._pass_on(ins, outs, sems, k, a, num, 1 - c).wait_recv()
        for k, (num, chip) in enumerate(others):
            for a in range(self.n):
                self._ici(ins, outs, sems, k, a, chip).wait_send()
                self._pass_on(ins, outs, sems, k, a, num, c).wait_send()


def _send_sibling_half(grads, tag):
    n = len(grads)

    def body(*refs):
        ins, outs = refs[:n], refs[n:2 * n]
        send_sems, recv_sems = refs[2 * n:]
        x, y, c = _position()
        copies = []
        for a in range(n):
            half = ins[a].shape[1] // 2
            theirs = ins[a].at[:, pl.ds(pl.multiple_of((1 - c) * half, 8), half), :]
            cp = pltpu.make_async_remote_copy(theirs, outs[a], send_sems.at[a], recv_sems.at[a],
                                              device_id=(x, y, 1 - c), device_id_type=MESH)
            cp.start()
            copies.append(cp)
        for cp in copies:
            cp.wait()

    return pl.pallas_call(
        body, name="send_sibling_half_" + tag,
        in_specs=[ANY] * n, out_specs=[ANY] * n,
        out_shape=[jax.ShapeDtypeStruct((g.shape[0], g.shape[1] // 2, g.shape[2]), g.dtype) for g in grads],
        scratch_shapes=[pltpu.SemaphoreType.DMA((n,)), pltpu.SemaphoreType.DMA((n,))],
    )(*grads)


def _scatter_to_chips(pairs):
    n = len(pairs)
    ride = _ScatterRide(pairs)

    def body(*refs):
        ride.start(refs[:n], refs[n:2 * n], refs[2 * n:])
        ride.finish(refs[:n], refs[n:2 * n], refs[2 * n:])

    return pl.pallas_call(
        body, name="scatter_to_chips",
        in_specs=[ANY] * n, out_specs=[ANY] * n,
        out_shape=ride.out_shape(), scratch_shapes=ride.scratch_shapes(),
    )(*pairs)


class _ScatterRide:
    def __init__(self, pairs):
        self.operands = list(pairs)
        self.n = len(pairs)

    def out_shape(self):
        return [jax.ShapeDtypeStruct(g.shape, g.dtype) for g in self.operands]

    def scratch_shapes(self):
        return [pltpu.SemaphoreType.DMA((3, self.n))] * 2

    @staticmethod
    def _copy(ins, outs, sems, k, a, src_slot, dst_slot, chip):
        _, _, c = _position()
        return pltpu.make_async_remote_copy(ins[a].at[src_slot], outs[a].at[dst_slot], sems[0].at[k, a],
                                            sems[1].at[k, a], device_id=(*chip, c), device_id_type=MESH)

    def start(self, ins, outs, sems):
        x, y, _ = _position()
        for k, (num, chip) in enumerate(_other_chips(x, y)):
            for a in range(self.n):
                self._copy(ins, outs, sems, k, a, num, 2 * x + y, chip).start()

    def finish(self, ins, outs, sems):
        x, y, _ = _position()
        for k, (num, chip) in enumerate(_other_chips(x, y)):
            for a in range(self.n):
                self._copy(ins, outs, sems, k, a, 2 * x + y, num, chip).wait_recv()
        for k, (num, chip) in enumerate(_other_chips(x, y)):
            for a in range(self.n):
                self._copy(ins, outs, sems, k, a, num, 2 * x + y, chip).wait_send()


def _exchange_halves(halves):
    n = len(halves)

    def body(*refs):
        ins, outs = refs[:n], refs[n:2 * n]
        send_sems, recv_sems = refs[2 * n:]
        x, y, c = _position()
        copies = []
        for a in range(n):
            cp = pltpu.make_async_remote_copy(ins[a], outs[a], send_sems.at[a], recv_sems.at[a],
                                              device_id=(x, y, 1 - c), device_id_type=MESH)
            cp.start()
            copies.append(cp)
        for cp in copies:
            cp.wait()

    return pl.pallas_call(
        body, name="exchange_halves",
        in_specs=[ANY] * n, out_specs=[ANY] * n,
        out_shape=[jax.ShapeDtypeStruct(h.shape, h.dtype) for h in halves],
        scratch_shapes=[pltpu.SemaphoreType.DMA((n,)), pltpu.SemaphoreType.DMA((n,))],
    )(*halves)


def _allreduce_small(v):
    def body(v_ref, o_ref, buf, send_sems, recv_sems):
        x, y, c = _position()
        me = 4 * x + 2 * y + c
        peers = [(1 - x, y, c), (x, 1 - y, c), (x, y, 1 - c), (1 - x, 1 - y, c), (1 - x, y, 1 - c), (x, 1 - y, 1 - c),
                 (1 - x, 1 - y, 1 - c)]
        num = lambda d: 4 * d[0] + 2 * d[1] + d[2]
        buf[me] = v_ref[...]
        sends = []
        for k, peer in enumerate(peers):
            cp = pltpu.make_async_remote_copy(v_ref, buf.at[me], send_sems.at[k], recv_sems.at[k], device_id=peer,
                                              device_id_type=MESH)
            cp.start()
            sends.append(cp)
        for k, peer in enumerate(peers):
            pltpu.make_async_remote_copy(v_ref, buf.at[num(peer)], send_sems.at[k], recv_sems.at[k], device_id=peer,
                                         device_id_type=MESH).wait_recv()
        for cp in sends:
            cp.wait_send()
        total = buf[0]
        for d in range(1, 8):
            total = total + buf[d]
        o_ref[...] = total

    return pl.pallas_call(
        body, name="allreduce_small",
        in_specs=[pl.BlockSpec(memory_space=pltpu.VMEM)], out_specs=pl.BlockSpec(memory_space=pltpu.VMEM),
        out_shape=jax.ShapeDtypeStruct(v.shape, v.dtype),
        scratch_shapes=[pltpu.VMEM((8,) + v.shape, v.dtype), pltpu.SemaphoreType.DMA((7,)),
                        pltpu.SemaphoreType.DMA((7,))],
    )(v)


def _sum_leading(a, name):
    k, r, c = a.shape
    tr = min(r, 256)

    def body(a_ref, o_ref):
        total = a_ref[0].astype(F32)
        for i in range(1, k):
            total = total + a_ref[i].astype(F32)
        o_ref[...] = total

    return pl.pallas_call(
        body, name=name, grid=(r // tr,),
        in_specs=[pl.BlockSpec((k, tr, c), lambda i: (0, i, 0))],
        out_specs=pl.BlockSpec((tr, c), lambda i: (i, 0)),
        out_shape=jax.ShapeDtypeStruct((r, c), F32),
        compiler_params=_cparams(("parallel",)),
    )(a)


def _add(a, b, name):
    k, r, c = a.shape
    tr = min(r, 256)
    spec = pl.BlockSpec((k, tr, c), lambda i: (0, i, 0))

    def body(a_ref, b_ref, o_ref):
        o_ref[...] = (a_ref[...] + b_ref[...]).astype(BF16)

    return pl.pallas_call(
        body, name=name, grid=(r // tr,), in_specs=[spec, spec], out_specs=spec,
        out_shape=jax.ShapeDtypeStruct(a.shape, BF16), compiler_params=_cparams(("parallel",)),
    )(a, b)


def _adamw(w, g, m, v, name):
    r, c = w.shape
    tr = min(r, 256)
    spec = pl.BlockSpec((tr, c), lambda i: (i, 0))

    def body(w_ref, g_ref, m_ref, v_ref, d_ref, nm_ref, nv_ref):
        gv = g_ref[...]
        nm = ADAM_B1 * m_ref[...] + (1.0 - ADAM_B1) * gv
        nv = ADAM_B2 * v_ref[...] + (1.0 - ADAM_B2) * jnp.square(gv)
        m_hat = nm / (1.0 - ADAM_B1 ** ADAM_STEP)
        v_hat = nv / (1.0 - ADAM_B2 ** ADAM_STEP)
        d_ref[...] = -ADAM_LR * (m_hat / (jnp.sqrt(v_hat) + ADAM_EPS) + ADAM_WD * w_ref[...])
        nm_ref[...] = nm
        nv_ref[...] = nv

    return pl.pallas_call(
        body, name=name, grid=(r // tr,), in_specs=[spec] * 4, out_specs=[spec] * 3,
        out_shape=[jax.ShapeDtypeStruct(w.shape, F32)] * 3, compiler_params=_cparams(("parallel",)),
    )(w, g, m, v)


MATRICES = ("w_in", "w_out", "w_ff1", "w_ff2", "w_ple_gate", "w_ple_proj")
COLUMN_SHARDED = ("w_in", "w_ff1", "w_ple_proj")
SMALL = ("g_attn_pre", "g_q", "g_k", "g_out_a", "g_out_b", "g_attn_post", "rel_bias", "g_mlp_pre", "g_mlp_post",
         "g_ple")
WEIGHT_ORDER = ("w_in", "g_attn_pre", "g_q", "g_k", "g_out_a", "g_out_b", "w_out", "g_attn_post", "rel_bias",
                "g_mlp_pre", "w_ff1", "w_ff2", "g_mlp_post", "g_ple", "w_ple_gate", "w_ple_proj")
PACK_ROWS, PACK_COLS = 8, 1024


def _chip():
    return 2 * lax.axis_index("x") + lax.axis_index("y")


def _whole(name, gathered, mine):
    g = lax.dynamic_update_slice_in_dim(gathered, mine[None], _chip(), axis=0)
    if name in COLUMN_SHARDED:
        return g.transpose(1, 0, 2).reshape(g.shape[1], N_CHIPS * g.shape[2])
    return g.reshape(N_CHIPS * g.shape[1], g.shape[2])


def _pair_sums(names, grads):
    by_chip = []
    for n, g in zip(names, grads):
        if n in COLUMN_SHARDED:
            by_chip.append(g.reshape(g.shape[0], N_CHIPS, g.shape[1] // N_CHIPS).transpose(1, 0, 2))
        else:
            by_chip.append(g.reshape(N_CHIPS, g.shape[0] // N_CHIPS, g.shape[1]))
    c = lax.axis_index("c")
    pairs = []
    for n, g, other in zip(names, by_chip, _send_sibling_half(by_chip, names[0])):
        half = g.shape[1] // 2
        pairs.append(_add(lax.dynamic_slice_in_dim(g, c * half, half, axis=1), other, "pair_sum_" + n))
    return pairs


def _chip_sums(names, pairs, scattered):
    halves = []
    for n, pair, got in zip(names, pairs, scattered):
        own = lax.dynamic_slice_in_dim(pair, _chip(), 1, axis=0)
        halves.append(_sum_leading(lax.dynamic_update_slice_in_dim(got, own, _chip(), axis=0), "chip_sum_" + n))
    return halves


def _pack_small(values, extra=None):
    flat = [values[n].reshape(-1) for n in SMALL]
    used = sum(f.shape[0] for f in flat)
    tail = jnp.zeros((PACK_ROWS * PACK_COLS - used - 1,), F32)
    last = jnp.zeros((1,), F32) if extra is None else extra.reshape(1)
    return jnp.concatenate(flat + [tail, last]).reshape(PACK_ROWS, PACK_COLS)


def _unpack_small(packed, like):
    flat = packed.reshape(-1)
    out, o = {}, 0
    for n in SMALL:
        size = like[n].size
        out[n] = flat[o:o + size].reshape(like[n].shape)
        o += size
    return out, flat[-1]


def kernel(x, p, w_in, g_attn_pre, g_q, g_k, g_out_a, g_out_b, w_out, g_attn_post, rel_bias, g_mlp_pre, w_ff1, w_ff2, g_mlp_post, g_ple, w_ple_gate, w_ple_proj, loss_target, m_w_in, m_g_attn_pre, m_g_q, m_g_k, m_g_out_a, m_g_out_b, m_w_out, m_g_attn_post, m_rel_bias, m_g_mlp_pre, m_w_ff1, m_w_ff2, m_g_mlp_post, m_g_ple, m_w_ple_gate, m_w_ple_proj, v_w_in, v_g_attn_pre, v_g_q, v_g_k, v_g_out_a, v_g_out_b, v_w_out, v_g_attn_post, v_rel_bias, v_g_mlp_pre, v_w_ff1, v_w_ff2, v_g_mlp_post, v_g_ple, v_w_ple_gate, v_w_ple_proj):
    given = dict(locals())
    weights = {n: given[n] for n in WEIGHT_ORDER}
    shards = {n: weights[n][0] for n in MATRICES}

    c = lax.axis_index("c")
    own = dict(zip(MATRICES, _cast_shards([shards[n] for n in MATRICES])))
    w_in_whole = _whole("w_in", _gather_weights([own["w_in"]])[0], own["w_in"])

    loss, grad_x, grads = _local_step(
        x[0], p[0, 0], loss_target[0], w_in_whole, [own[n] for n in LATE], g_attn_pre, g_q, g_k, g_out_a, g_out_b,
        g_attn_post, rel_bias, g_mlp_pre, g_mlp_post, g_ple)

    pairs = _pair_sums(["w_in"], [grads["w_in"]])
    grads["w_in"] = _chip_sums(["w_in"], pairs, _scatter_to_chips(pairs))[0]
    halves = [grads[n] for n in MATRICES]
    grad_w = {}
    for n, mine, theirs in zip(MATRICES, halves, _exchange_halves(halves)):
        half = mine.shape[0]
        g = jnp.zeros((2 * half, mine.shape[1]), F32)
        g = lax.dynamic_update_slice_in_dim(g, mine, c * half, axis=0)
        grad_w[n] = lax.dynamic_update_slice_in_dim(g, theirs, (1 - c) * half, axis=0)

    small_like = {n: weights[n] for n in SMALL}
    reduced = _allreduce_small(_pack_small({n: grads[n] for n in SMALL}, extra=loss))
    grad_small, loss_total = _unpack_small(reduced, small_like)

    delta, new_m, new_v = {}, {}, {}
    for n in MATRICES:
        d, nm, nv = _adamw(shards[n], grad_w[n], given["m_" + n][0], given["v_" + n][0], "adamw_" + n)
        delta[n], new_m[n], new_v[n] = d[None], nm[None], nv[None]
        grad_w[n] = grad_w[n][None]
    d, nm, nv = _adamw(_pack_small(small_like), reduced, _pack_small({n: given["m_" + n] for n in SMALL}),
                       _pack_small({n: given["v_" + n] for n in SMALL}), "adamw_small")
    d_small, nm_small, nv_small = (_unpack_small(a, small_like)[0] for a in (d, nm, nv))
    for n in SMALL:
        grad_w[n], delta[n], new_m[n], new_v[n] = grad_small[n], d_small[n], nm_small[n], nv_small[n]

    return (loss_total, grad_x[None], *[grad_w[n] for n in WEIGHT_ORDER], *[delta[n] for n in WEIGHT_ORDER],
            *[new_m[n] for n in WEIGHT_ORDER], *[new_v[n] for n in WEIGHT_ORDER])
```

```python
import functools
import math

import jax
import jax.numpy as jnp
from jax import lax
from jax.experimental import pallas as pl
from jax.experimental.pallas import tpu as pltpu

F32 = jnp.float32
BF16 = jnp.bfloat16

D_MODEL = 1024
HEAD_DIM = 64
N_HEADS_A = 8
N_KV_A = 2
GROUP_A = N_HEADS_A // N_KV_A
N_HEADS_B = 8
D_A = N_HEADS_A * HEAD_DIM
D_KV_A = N_KV_A * HEAD_DIM
D_B = N_HEADS_B * HEAD_DIM
D_IN = D_A + 2 * D_KV_A + 3 * D_B
D_FF = 4 * D_MODEL
D_PLE = 256
GRID_W = 64
ROPE_THETA = 10000.0
DILATIONS = (1, 4, 16)
HALF_WIN = 64
N_BUCKETS = 32
MAX_DISTANCE = 1024
EPS = 1e-6
NEG_BIG = -1e30
Q_SCALE = HEAD_DIM ** -0.5

ADAM_LR = 0.001
ADAM_B1 = 0.9
ADAM_B2 = 0.999
ADAM_EPS = 1e-08
ADAM_WD = 0.01
ADAM_STEP = 10

N_CHIPS = 4
MESH = pl.DeviceIdType.MESH

ROW_TILE = 512
ATT_TQ = 256
ATT_TQ_BWD = 512
ATT_TK_FWD = 2048
ATT_UNROLL_FWD = 8
ATT_TK_BWD = 512
ATT_UNROLL_BWD = 8
SWA_TQ = 128
SWA_MIN_BLOCK = 1024
DW_TS = 1024
VMEM_LIMIT = 56 * 1024 * 1024

NT = (((1,), (1,)), ((), ()))
TN = (((0,), (0,)), ((), ()))


def _cparams(sem=None, vmem=VMEM_LIMIT):
    return pltpu.CompilerParams(dimension_semantics=sem, vmem_limit_bytes=vmem)


def _full(shape):
    n = len(shape)
    return pl.BlockSpec(shape, lambda *_: (0,) * n)


def _rows(tm, width):
    return pl.BlockSpec((tm, width), lambda i: (i, 0))


def _split3(a):
    a1 = a.astype(BF16)
    r = a - a1.astype(F32)
    a2 = r.astype(BF16)
    a3 = (r - a2.astype(F32)).astype(BF16)
    return a1, a2, a3


def _xdot(a, sel):
    a1, a2, a3 = _split3(a)
    d = lambda p: jnp.dot(p, sel, preferred_element_type=F32)
    return d(a1) + d(a2) + d(a3)


def _mm(a, b):
    return jnp.dot(a, b, preferred_element_type=F32)


def _mm_nt(a, b):
    return lax.dot_general(a, b, NT, preferred_element_type=F32)


def _mm_tn(a, b):
    return lax.dot_general(a, b, TN, preferred_element_type=F32)


def _rms_stats(x):
    r = lax.rsqrt(jnp.mean(x * x, axis=-1, keepdims=True) + EPS)
    return x * r, r


def _rms_bwd(dy, xh, r, g):
    gdy = dy * g
    dx = r * (gdy - xh * jnp.mean(gdy * xh, axis=-1, keepdims=True))
    dg = jnp.sum(dy * xh, axis=0, keepdims=True)
    return dx, dg


def _acc_out(ref, val):
    @pl.when(pl.program_id(0) == 0)
    def _():
        ref[...] = jnp.zeros_like(ref)

    ref[...] += val


def _swap_halves(x, first_half):
    return jnp.where(first_half, pltpu.roll(x, 96, 1), pltpu.roll(x, 32, 1))


def _first_half_mask(shape):
    return (lax.broadcasted_iota(jnp.int32, shape, 1) % HEAD_DIM) < (HEAD_DIM // 2)


def _rope_tables(s_len):
    t = jnp.arange(s_len)
    row = (t // GRID_W).astype(F32)
    col = (t % GRID_W).astype(F32)
    n_axis = HEAD_DIM // 4
    inv_freq = ROPE_THETA ** (-jnp.arange(n_axis, dtype=F32) / n_axis)
    ang = jnp.concatenate([row[:, None] * inv_freq, col[:, None] * inv_freq], axis=-1)
    c, s = jnp.cos(ang), jnp.sin(ang)
    cc = jnp.concatenate([c, c, c, c], axis=-1)
    ss = jnp.concatenate([-s, s, -s, s], axis=-1)
    return cc, ss


def _group_ones(width):
    i = jnp.arange(width)
    return (i[:, None] // HEAD_DIM == i[None, :] // HEAD_DIM).astype(BF16)


def _t5_bucket(rel):
    nb = N_BUCKETS // 2
    max_exact = nb // 2
    side = jnp.where(rel > 0, nb, 0)
    n = jnp.abs(rel)
    large = max_exact + (jnp.log(jnp.maximum(n, max_exact).astype(F32) / max_exact)
                         / math.log(MAX_DISTANCE / max_exact) * (nb - max_exact)).astype(jnp.int32)
    large = jnp.minimum(large, nb - 1)
    return side + jnp.where(n < max_exact, n, large)


def _in_window(tq):
    qi = jnp.arange(tq)
    kj = jnp.arange(tq + 2 * HALF_WIN)
    return jnp.abs(kj[None, :] - HALF_WIN - qi[:, None]) <= HALF_WIN


def _bucket_onehot(tq, dilation):
    qi = jnp.arange(tq)
    kj = jnp.arange(tq + 2 * HALF_WIN)
    rel = kj[None, :] - HALF_WIN - qi[:, None]
    bucket = _t5_bucket(rel * dilation).reshape(-1)
    return (bucket[:, None] == jnp.arange(128)[None, :]).astype(BF16)


def _in_proj(x, g1, w_in, cc, ss, gq2, gk2, ones128):
    s_len = x.shape[0]
    tm = min(ROW_TILE, s_len)

    def body(x_ref, g_ref, w_ref, cc_ref, ss_ref, gq_ref, gk_ref, one_ref,
             xn_ref, qpre_ref, kpre_ref, qa_ref, kv_ref, qb_ref, kb_ref, vb_ref):
        xh, _ = _rms_stats(x_ref[...])
        xn = (xh * g_ref[...]).astype(BF16)
        xn_ref[...] = xn
        proj = _mm(xn, w_ref[...])
        first_half = _first_half_mask((tm, 128))
        ones = one_ref[...]
        cc_t, ss_t = cc_ref[...], ss_ref[...]

        def norm_rope(xc, gain):
            ms = _xdot(xc * xc, ones) * (1.0 / HEAD_DIM)
            y = xc * lax.rsqrt(ms + EPS) * gain
            return y * cc_t + _swap_halves(y, first_half) * ss_t

        qpre_ref[...] = proj[:, :D_A]
        kpre_ref[...] = proj[:, D_A:D_A + D_KV_A]
        for c in range(D_A // 128):
            y = norm_rope(proj[:, 128 * c:128 * (c + 1)], gq_ref[...])
            qa_ref[:, 128 * c:128 * (c + 1)] = (y * Q_SCALE).astype(BF16)
        ka = norm_rope(proj[:, D_A:D_A + D_KV_A], gk_ref[...])
        o = D_A + D_KV_A
        va = proj[:, o:o + D_KV_A]
        low = _low_lanes(tm)
        kv_ref[0] = jnp.where(low, ka, pltpu.roll(va, HEAD_DIM, 1)).astype(BF16)
        kv_ref[1] = jnp.where(low, pltpu.roll(ka, HEAD_DIM, 1), va).astype(BF16)
        o += D_KV_A
        qb_ref[...] = (proj[:, o:o + D_B] * Q_SCALE).astype(BF16)
        kb_ref[...] = proj[:, o + D_B:o + 2 * D_B].astype(BF16)
        vb_ref[...] = proj[:, o + 2 * D_B:o + 3 * D_B].astype(BF16)

    sds = jax.ShapeDtypeStruct
    return pl.pallas_call(
        body, name="in_proj", grid=(s_len // tm,),
        in_specs=[_rows(tm, D_MODEL), _full((1, D_MODEL)), _full((D_MODEL, D_IN)), _rows(tm, 128), _rows(tm, 128),
                  _full((1, 128)), _full((1, 128)), _full((128, 128))],
        out_specs=[_rows(tm, D_MODEL), _rows(tm, D_A), _rows(tm, D_KV_A), _rows(tm, D_A),
                   pl.BlockSpec((N_KV_A, tm, 128), lambda i: (0, i, 0)), _rows(tm, D_B), _rows(tm, D_B),
                   _rows(tm, D_B)],
        out_shape=[sds((s_len, D_MODEL), BF16), sds((s_len, D_A), F32), sds((s_len, D_KV_A), F32),
                   sds((s_len, D_A), BF16), sds((N_KV_A, s_len, 128), BF16),
                   sds((s_len, D_B), BF16), sds((s_len, D_B), BF16), sds((s_len, D_B), BF16)],
        compiler_params=_cparams(("parallel",)),
    )(x, g1, w_in, cc, ss, gq2, gk2, ones128)


def _stat_spec(tm):
    return pl.BlockSpec((N_HEADS_B, tm, 1), lambda i: (0, i, 0))


def _merge_b(outs, lses):
    s_len = outs[0].shape[0]
    tm = min(ROW_TILE, s_len)

    def body(o0, o1, o2, l0, l1, l2, yb_ref, lse_ref):
        m_all = jnp.maximum(jnp.maximum(l0[...], l1[...]), l2[...])
        w = [jnp.exp(l[...] - m_all) for l in (l0, l1, l2)]
        den = w[0] + w[1] + w[2]
        yb_ref[...] = (w[0] * o0[...] + w[1] * o1[...] + w[2] * o2[...]) / den
        lse_ref[...] = m_all + jnp.log(den)

    return pl.pallas_call(
        body, name="merge_b", grid=(s_len // tm,),
        in_specs=[_rows(tm, D_B)] * 6,
        out_specs=[_rows(tm, D_B), _rows(tm, D_B)],
        out_shape=[jax.ShapeDtypeStruct((s_len, D_B), F32), jax.ShapeDtypeStruct((s_len, D_B), F32)],
        compiler_params=_cparams(("parallel",)),
    )(*outs, *lses)


def _out_proj(ya, yb, x, g_a, g_b, w_out, g_post, g_mlp_pre):
    s_len = x.shape[0]
    tm = min(ROW_TILE, s_len)

    def body(ya_ref, yb_ref, x_ref, ga_ref, gb_ref, w_ref, gp_ref, gm_ref, ycat_ref, y2_ref, h1_ref, xn2_ref):
        ah, _ = _rms_stats(ya_ref[...])
        bh, _ = _rms_stats(yb_ref[...])
        ycat = jnp.concatenate([ah * ga_ref[...], bh * gb_ref[...]], axis=-1).astype(BF16)
        ycat_ref[...] = ycat
        y2 = _mm(ycat, w_ref[...])
        y2_ref[...] = y2
        y2h, _ = _rms_stats(y2)
        h1 = x_ref[...] + y2h * gp_ref[...]
        h1_ref[...] = h1
        h1h, _ = _rms_stats(h1)
        xn2_ref[...] = (h1h * gm_ref[...]).astype(BF16)

    sds = jax.ShapeDtypeStruct
    return pl.pallas_call(
        body, name="out_proj", grid=(s_len // tm,),
        in_specs=[_rows(tm, D_A), _rows(tm, D_B), _rows(tm, D_MODEL), _full((1, D_A)), _full((1, D_B)),
                  _full((D_MODEL, D_MODEL)), _full((1, D_MODEL)), _full((1, D_MODEL))],
        out_specs=[_rows(tm, D_MODEL)] * 4,
        out_shape=[sds((s_len, D_MODEL), BF16), sds((s_len, D_MODEL), F32), sds((s_len, D_MODEL), F32),
                   sds((s_len, D_MODEL), BF16)],
        compiler_params=_cparams(("parallel",)),
    )(ya, yb, x, g_a, g_b, w_out, g_post, g_mlp_pre)


def _ff1(xn2, w_ff1):
    s_len = xn2.shape[0]
    tm = min(ROW_TILE, s_len)

    def body(x_ref, w_ref, u_ref):
        u_ref[...] = _mm(x_ref[...], w_ref[...])

    return pl.pallas_call(
        body, name="ff1", grid=(s_len // tm,),
        in_specs=[_rows(tm, D_MODEL), _full((D_MODEL, D_FF))],
        out_specs=_rows(tm, D_FF),
        out_shape=jax.ShapeDtypeStruct((s_len, D_FF), F32),
        compiler_params=_cparams(("parallel",)),
    )(xn2, w_ff1)


def _ff2(u, w_ff2, h1, g_post, g_ple):
    s_len = u.shape[0]
    tm = min(ROW_TILE, s_len)

    def body(u_ref, w_ref, h1_ref, gp_ref, gl_ref, f2_ref, h2_ref, xn3_ref):
        f = jnp.square(jnp.maximum(u_ref[...], 0.0)).astype(BF16)
        f2 = _mm(f, w_ref[...])
        f2_ref[...] = f2
        f2h, _ = _rms_stats(f2)
        h2 = h1_ref[...] + f2h * gp_ref[...]
        h2_ref[...] = h2
        h2h, _ = _rms_stats(h2)
        xn3_ref[...] = (h2h * gl_ref[...]).astype(BF16)

    sds = jax.ShapeDtypeStruct
    return pl.pallas_call(
        body, name="ff2", grid=(s_len // tm,),
        in_specs=[_rows(tm, D_FF), _full((D_FF, D_MODEL)), _rows(tm, D_MODEL), _full((1, D_MODEL)),
                  _full((1, D_MODEL))],
        out_specs=[_rows(tm, D_MODEL)] * 3,
        out_shape=[sds((s_len, D_MODEL), F32), sds((s_len, D_MODEL), F32), sds((s_len, D_MODEL), BF16)],
        compiler_params=_cparams(("parallel",)),
    )(u, w_ff2, h1, g_post, g_ple)


def _ple_loss(xn3, p, h2, f2, tgt, w_gate, w_ple, g_ple, g_mlp_post):
    s_len = h2.shape[0]
    tm = min(ROW_TILE, s_len)

    def body(xn3_ref, p_ref, h2_ref, f2_ref, t_ref, wg_ref, wp_ref, gl_ref, gp_ref,
             dh2_ref, df2_ref, dgl_ref, dpp_ref, loss_ref, dgple_ref, dgpost_ref):
        gate = jax.nn.sigmoid(_mm(xn3_ref[...], wg_ref[...]))
        pp = _mm(p_ref[...].astype(BF16), wp_ref[...])
        h2 = h2_ref[...]
        err = h2 + gate * pp - t_ref[...]
        sq = jnp.sum(jnp.sum(err * err, axis=1, keepdims=True), axis=0, keepdims=True)
        _acc_out(loss_ref, sq * (0.5 / D_MODEL))
        dh3 = err * (1.0 / D_MODEL)
        dgl = (dh3 * pp) * gate * (1.0 - gate)
        dgl_b = dgl.astype(BF16)
        dgl_ref[...] = dgl_b
        dpp_ref[...] = (dh3 * gate).astype(BF16)
        dxn3 = _mm_nt(dgl_b, wg_ref[...])
        h2h, r2 = _rms_stats(h2)
        dx, dg = _rms_bwd(dxn3, h2h, r2, gl_ref[...])
        _acc_out(dgple_ref, dg)
        dh2 = dh3 + dx
        dh2_ref[...] = dh2
        f2h, rf = _rms_stats(f2_ref[...])
        df2, dg = _rms_bwd(dh2, f2h, rf, gp_ref[...])
        _acc_out(dgpost_ref, dg)
        df2_ref[...] = df2.astype(BF16)

    sds = jax.ShapeDtypeStruct
    return pl.pallas_call(
        body, name="ple_loss", grid=(s_len // tm,),
        in_specs=[_rows(tm, D_MODEL), _rows(tm, D_PLE), _rows(tm, D_MODEL), _rows(tm, D_MODEL), _rows(tm, D_MODEL),
                  _full((D_MODEL, D_MODEL)), _full((D_PLE, D_MODEL)), _full((1, D_MODEL)), _full((1, D_MODEL))],
        out_specs=[_rows(tm, D_MODEL)] * 3 + [_rows(tm, D_MODEL), _full((1, 1)), _full((1, D_MODEL)),
                                              _full((1, D_MODEL))],
        out_shape=[sds((s_len, D_MODEL), F32), sds((s_len, D_MODEL), BF16), sds((s_len, D_MODEL), BF16),
                   sds((s_len, D_MODEL), BF16), sds((1, 1), F32), sds((1, D_MODEL), F32), sds((1, D_MODEL), F32)],
        compiler_params=_cparams(("arbitrary",)),
    )(xn3, p, h2, f2, tgt, w_gate, w_ple, g_ple, g_mlp_post)


def _ff2_bwd(df2, w_ff2, u):
    s_len = u.shape[0]
    tm = min(ROW_TILE, s_len)

    def body(d_ref, w_ref, u_ref, du_ref):
        df = _mm_nt(d_ref[...], w_ref[...])
        du_ref[...] = (df * (2.0 * jnp.maximum(u_ref[...], 0.0))).astype(BF16)

    return pl.pallas_call(
        body, name="ff2_bwd", grid=(s_len // tm,),
        in_specs=[_rows(tm, D_MODEL), _full((D_FF, D_MODEL)), _rows(tm, D_FF)],
        out_specs=_rows(tm, D_FF),
        out_shape=jax.ShapeDtypeStruct((s_len, D_FF), BF16),
        compiler_params=_cparams(("parallel",)),
    )(df2, w_ff2, u)


def _ff1_bwd(du, w_ff1, dh2, h1, y2, g_mlp_pre, g_post):
    s_len = du.shape[0]
    tm = min(ROW_TILE, s_len)

    def body(du_ref, w_ref, dh2_ref, h1_ref, y2_ref, gm_ref, gp_ref, dh1_ref, dy2_ref, dgm_ref, dgp_ref):
        dxn2 = _mm_nt(du_ref[...], w_ref[...])
        h1h, r1 = _rms_stats(h1_ref[...])
        dx, dg = _rms_bwd(dxn2, h1h, r1, gm_ref[...])
        _acc_out(dgm_ref, dg)
        dh1 = dh2_ref[...] + dx
        dh1_ref[...] = dh1
        y2h, ry = _rms_stats(y2_ref[...])
        dy2, dg = _rms_bwd(dh1, y2h, ry, gp_ref[...])
        _acc_out(dgp_ref, dg)
        dy2_ref[...] = dy2.astype(BF16)

    sds = jax.ShapeDtypeStruct
    return pl.pallas_call(
        body, name="ff1_bwd", grid=(s_len // tm,),
        in_specs=[_rows(tm, D_FF), _full((D_MODEL, D_FF)), _rows(tm, D_MODEL), _rows(tm, D_MODEL),
                  _rows(tm, D_MODEL), _full((1, D_MODEL)), _full((1, D_MODEL))],
        out_specs=[_rows(tm, D_MODEL), _rows(tm, D_MODEL), _full((1, D_MODEL)), _full((1, D_MODEL))],
        out_shape=[sds((s_len, D_MODEL), F32), sds((s_len, D_MODEL), BF16), sds((1, D_MODEL), F32),
                   sds((1, D_MODEL), F32)],
        compiler_params=_cparams(("arbitrary",)),
    )(du, w_ff1, dh2, h1, y2, g_mlp_pre, g_post)


def _out_proj_bwd(dy2, w_out, ya, yb, lse_b, g_a, g_b):
    s_len = ya.shape[0]
    tm = min(ROW_TILE, s_len)

    def body(d_ref, w_ref, ya_ref, yb_ref, lse_ref, ga_ref, gb_ref, dya_ref, dyb_ref, da_ref, st_ref, dga_ref,
             dgb_ref):
        dycat = _mm_nt(d_ref[...], w_ref[...])
        lane = lax.broadcasted_iota(jnp.int32, (tm, 128), 1)
        low = lane < HEAD_DIM
        is_lse = (lane % HEAD_DIM) < (HEAD_DIM // 2)

        def head_sums(prod_chunk):
            return (jnp.sum(jnp.where(low, prod_chunk, 0.0), axis=1, keepdims=True),
                    jnp.sum(jnp.where(low, 0.0, prod_chunk), axis=1, keepdims=True))

        ya = ya_ref[...]
        yh, r = _rms_stats(ya)
        dya, dg = _rms_bwd(dycat[:, :D_A], yh, r, ga_ref[...])
        _acc_out(dga_ref, dg)
        dya_ref[...] = dya
        prod = dya * ya
        for c in range(D_A // 128):
            da_ref[2 * c], da_ref[2 * c + 1] = head_sums(prod[:, 128 * c:128 * (c + 1)])

        yb = yb_ref[...]
        yh, r = _rms_stats(yb)
        dyb, dg = _rms_bwd(dycat[:, D_A:], yh, r, gb_ref[...])
        _acc_out(dgb_ref, dg)
        dyb_ref[...] = dyb.astype(BF16)
        prod = dyb * yb
        for c in range(D_B // 128):
            sl = slice(128 * c, 128 * (c + 1))
            d_lo, d_hi = head_sums(prod[:, sl])
            st_ref[:, sl] = jnp.where(is_lse, lse_ref[:, sl], jnp.where(low, d_lo, d_hi))

    sds = jax.ShapeDtypeStruct
    return pl.pallas_call(
        body, name="out_proj_bwd", grid=(s_len // tm,),
        in_specs=[_rows(tm, D_MODEL), _full((D_MODEL, D_MODEL)), _rows(tm, D_A), _rows(tm, D_B), _rows(tm, D_B),
                  _full((1, D_A)), _full((1, D_B))],
        out_specs=[_rows(tm, D_A), _rows(tm, D_B), _stat_spec(tm), _rows(tm, D_B), _full((1, D_A)),
                   _full((1, D_B))],
        out_shape=[sds((s_len, D_A), F32), sds((s_len, D_B), BF16), sds((N_HEADS_A, s_len, 1), F32),
                   sds((s_len, D_B), F32), sds((1, D_A), F32), sds((1, D_B), F32)],
        compiler_params=_cparams(("arbitrary",)),
    )(dy2, w_out, ya, yb, lse_b, g_a, g_b)


def _in_proj_bwd(dqr, dkv, dqb, dkb, dvb, qpre, kpre, x, dh1, g1, w_in, cc, ss, gq2, gk2, ones128):
    s_len = x.shape[0]
    tm = min(ROW_TILE // 2, s_len)

    def body(dqr_ref, dkv_ref, dq0, dq1, dq2, dk0, dk1, dk2, dv0, dv1, dv2, qpre_ref, kpre_ref, x_ref,
             dh1_ref, g_ref, w_ref, cc_ref, ss_ref, gq_ref, gk_ref, one_ref, dproj_ref, gx_ref, dg1_ref, dgq_ref,
             dgk_ref):
        low = _low_lanes(tm)
        dkr = jnp.where(low, dkv_ref[0], pltpu.roll(dkv_ref[1], HEAD_DIM, 1))
        dva = jnp.where(low, pltpu.roll(dkv_ref[0], HEAD_DIM, 1), dkv_ref[1])
        first_half = _first_half_mask((tm, 128))
        ones = one_ref[...]
        cc_t, ss_t = cc_ref[...], ss_ref[...]

        def norm_rope_bwd(dy, xc, gain):
            dn = dy * cc_t - _swap_halves(dy, first_half) * ss_t
            r = lax.rsqrt(_xdot(xc * xc, ones) * (1.0 / HEAD_DIM) + EPS)
            xh = xc * r
            gdy = dn * gain
            dx = r * (gdy - xh * (_xdot(gdy * xh, ones) * (1.0 / HEAD_DIM)))
            return dx, jnp.sum(dn * xh, axis=0, keepdims=True)

        dgq = jnp.zeros((1, 128), F32)
        parts = []
        for c in range(D_A // 128):
            sl = slice(128 * c, 128 * (c + 1))
            dx, dg = norm_rope_bwd(dqr_ref[:, sl] * Q_SCALE, qpre_ref[:, sl], gq_ref[...])
            parts.append(dx)
            dgq = dgq + dg
        dxk, dgk = norm_rope_bwd(dkr, kpre_ref[...], gk_ref[...])
        _acc_out(dgq_ref, dgq)
        _acc_out(dgk_ref, dgk)
        parts += [dxk, dva, (dq0[...] + dq1[...] + dq2[...]) * Q_SCALE, dk0[...] + dk1[...] + dk2[...],
                  dv0[...] + dv1[...] + dv2[...]]
        dproj = jnp.concatenate(parts, axis=-1).astype(BF16)
        dproj_ref[...] = dproj
        dxn = _mm_nt(dproj, w_ref[...])
        xh, r = _rms_stats(x_ref[...])
        dx, dg = _rms_bwd(dxn, xh, r, g_ref[...])
        _acc_out(dg1_ref, dg)
        gx_ref[...] = dh1_ref[...] + dx

    sds = jax.ShapeDtypeStruct
    return pl.pallas_call(
        body, name="in_proj_bwd", grid=(s_len // tm,),
        in_specs=[_rows(tm, D_A), pl.BlockSpec((N_KV_A, tm, 128), lambda i: (0, i, 0))] + [_rows(tm, D_B)] * 9
                 + [_rows(tm, D_A), _rows(tm, D_KV_A), _rows(tm, D_MODEL), _rows(tm, D_MODEL),
                    _full((1, D_MODEL)), _full((D_MODEL, D_IN)), _rows(tm, 128), _rows(tm, 128), _full((1, 128)),
                    _full((1, 128)), _full((128, 128))],
        out_specs=[_rows(tm, D_IN), _rows(tm, D_MODEL), _full((1, D_MODEL)), _full((1, 128)), _full((1, 128))],
        out_shape=[sds((s_len, D_IN), BF16), sds((s_len, D_MODEL), F32), sds((1, D_MODEL), F32),
                   sds((1, 128), F32), sds((1, 128), F32)],
        compiler_params=_cparams(("arbitrary",)),
    )(dqr, dkv, *dqb, *dkb, *dvb, qpre, kpre, x, dh1, g1, w_in, cc, ss, gq2, gk2, ones128)


def _dw(a, b, name, relu2=False):
    s_len, ka = a.shape
    n = b.shape[1]
    ts = min(DW_TS, s_len)
    bk = min(ka, 1024)
    bn = n if n % 1024 else 1024

    def body(a_ref, b_ref, o_ref):
        @pl.when(pl.program_id(2) == 0)
        def _():
            o_ref[...] = jnp.zeros_like(o_ref)

        av = a_ref[...]
        if relu2:
            av = jnp.square(jnp.maximum(av, 0.0))
        o_ref[...] += _mm_tn(av.astype(BF16), b_ref[...])

    return pl.pallas_call(
        body, name=name, grid=(ka // bk, n // bn, s_len // ts),
        in_specs=[pl.BlockSpec((ts, bk), lambda i, j, k: (k, i)), pl.BlockSpec((ts, bn), lambda i, j, k: (k, j))],
        out_specs=pl.BlockSpec((bk, bn), lambda i, j, k: (i, j)),
        out_shape=jax.ShapeDtypeStruct((ka, n), F32),
        compiler_params=_cparams(("parallel", "parallel", "arbitrary")),
    )(a, b)


def _stack_heads(block, low, data_low):
    parts = []
    for c in range(GROUP_A // 2):
        chunk = block[:, 128 * c:128 * (c + 1)]
        swapped = pltpu.roll(chunk, HEAD_DIM, 1)
        for h_low in (chunk, swapped) if data_low else (swapped, chunk):
            parts.append(jnp.where(low, h_low, 0.0) if data_low else jnp.where(low, 0.0, h_low))
    return jnp.concatenate(parts, axis=0).astype(BF16)


def _unstack_heads(stacked, low, tq, data_low):
    chunks = []
    for c in range(GROUP_A // 2):
        even = stacked[2 * c * tq:(2 * c + 1) * tq]
        odd = stacked[(2 * c + 1) * tq:(2 * c + 2) * tq]
        if data_low:
            chunks.append(jnp.where(low, even, pltpu.roll(odd, HEAD_DIM, 1)))
        else:
            chunks.append(jnp.where(low, pltpu.roll(even, HEAD_DIM, 1), odd))
    return chunks


def _attn_a_fwd(qa, kv):
    s_len = kv.shape[1]
    tq = min(ATT_TQ, s_len)
    tk = min(ATT_TK_FWD, s_len)
    rows = GROUP_A * tq

    def body(q_ref, kv_ref, o_ref, lse_ref):
        low = _low_lanes(tq)
        low_k = _low_lanes(tk)
        q = _stack_heads(q_ref[...].astype(F32), low, data_low=True)

        def block(j, m, acc):
            kvj = kv_ref[0, pl.ds(pl.multiple_of(j * tk, tk), tk), :]
            s = _mm_nt(q, kvj)
            m_new = jnp.maximum(m, jnp.max(s, axis=1, keepdims=True))
            p = jnp.exp(s - m_new).astype(BF16)
            return m_new, jnp.exp(m - m_new) * acc + _mm(p, jnp.where(low_k, jnp.ones_like(kvj), kvj))

        def step(j, carry):
            for u in range(unroll):
                carry = block(unroll * j + u, *carry)
            return carry

        unroll = math.gcd(s_len // tk, ATT_UNROLL_FWD)
        init = (jnp.full((rows, 1), -jnp.inf, F32), jnp.zeros((rows, 128), F32))
        m, acc = lax.fori_loop(0, s_len // (tk * unroll), step, init)
        for c, chunk in enumerate(_unstack_heads(acc / pltpu.roll(acc, HEAD_DIM, 1), low, tq, data_low=False)):
            o_ref[:, 128 * c:128 * (c + 1)] = chunk
        lse_ref[...] = (m + jnp.log(acc[:, :1])).reshape(GROUP_A, tq, 1)

    return pl.pallas_call(
        body, name="attn_a_fwd", grid=(N_KV_A, s_len // tq),
        in_specs=[pl.BlockSpec((tq, 256), lambda g, i: (i, g)),
                  pl.BlockSpec((1, s_len, 128), lambda g, i: (g, 0, 0))],
        out_specs=[pl.BlockSpec((tq, 256), lambda g, i: (i, g)),
                   pl.BlockSpec((GROUP_A, tq, 1), lambda g, i: (g, i, 0))],
        out_shape=[jax.ShapeDtypeStruct((s_len, D_A), F32),
                   jax.ShapeDtypeStruct((N_HEADS_A, s_len, 1), F32)],
        compiler_params=_cparams(("parallel", "parallel")),
    )(qa, kv)


def _attn_a_bwd(qa, dya, kv, lse, delta):
    s_len = kv.shape[1]
    tq = min(ATT_TQ_BWD, s_len)
    tk = min(ATT_TK_BWD, s_len)
    rows = GROUP_A * tq

    def body(q_ref, do_ref, kv_ref, lse_ref, dl_ref, dq_ref, dkv_ref):
        @pl.when(pl.program_id(1) == 0)
        def _():
            dkv_ref[...] = jnp.zeros_like(dkv_ref)

        low = _low_lanes(tq)
        q = _stack_heads(q_ref[...].astype(F32), low, data_low=True)
        do = _stack_heads(do_ref[...], low, data_low=False)
        lse_t = lse_ref[...].reshape(rows, 1)
        dl_t = dl_ref[...].reshape(rows, 1)
        q_t = q.astype(F32).T.astype(BF16)
        do_t = do.astype(F32).T.astype(BF16)

        def block(j, dq):
            span = pl.ds(pl.multiple_of(j * tk, tk), tk)
            kvj = kv_ref[0, span, :]
            p = jnp.exp(_mm_nt(q, kvj) - lse_t)
            ds = (p * (_mm_nt(do, kvj) - dl_t)).astype(BF16)
            dkv_ref[0, :, span] += _mm(q_t, ds) + _mm(do_t, p.astype(BF16))
            return dq + _mm(ds, kvj)

        def step(j, dq):
            for u in range(unroll):
                dq = block(unroll * j + u, dq)
            return dq

        unroll = math.gcd(s_len // tk, ATT_UNROLL_BWD)
        dq = lax.fori_loop(0, s_len // (tk * unroll), step, jnp.zeros((rows, 128), F32))
        for c, chunk in enumerate(_unstack_heads(dq, low, tq, data_low=True)):
            dq_ref[:, 128 * c:128 * (c + 1)] = chunk

    return pl.pallas_call(
        body, name="attn_a_bwd", grid=(N_KV_A, s_len // tq),
        in_specs=[pl.BlockSpec((tq, 256), lambda g, i: (i, g)),
                  pl.BlockSpec((tq, 256), lambda g, i: (i, g)),
                  pl.BlockSpec((1, s_len, 128), lambda g, i: (g, 0, 0)),
                  pl.BlockSpec((GROUP_A, tq, 1), lambda g, i: (g, i, 0)),
                  pl.BlockSpec((GROUP_A, tq, 1), lambda g, i: (g, i, 0))],
        out_specs=[pl.BlockSpec((tq, 256), lambda g, i: (i, g)),
                   pl.BlockSpec((1, 128, s_len), lambda g, i: (g, 0, 0))],
        out_shape=[jax.ShapeDtypeStruct((s_len, D_A), F32),
                   jax.ShapeDtypeStruct((N_KV_A, 128, s_len), F32)],
        compiler_params=_cparams(("parallel", "arbitrary")),
    )(qa, dya, kv, lse, delta)


class _SwaGeometry:
    def __init__(self, s_len, r):
        self.r = r
        self.tq = SWA_TQ
        self.block = min(max(SWA_MIN_BLOCK, 2 * SWA_TQ * r), s_len)
        self.halo = HALF_WIN * r
        self.nsub = self.block // (self.tq * r)
        self.band = self.tq + 2 * HALF_WIN
        self.length = s_len // r
        self.nblk = s_len // self.block
        self.nhalo = s_len // self.halo
        assert self.nsub * self.tq * r == self.block and self.block % self.halo == 0

    def specs(self):
        per = self.block // self.halo
        cur = pl.BlockSpec((self.block, 128), lambda c, i: (i, c))
        prev = pl.BlockSpec((self.halo, 128), lambda c, i: (jnp.maximum(i * per - 1, 0), c))
        nxt = pl.BlockSpec((self.halo, 128), lambda c, i: (jnp.minimum((i + 1) * per, self.nhalo - 1), c))
        return prev, cur, nxt

    def tiles(self):
        return [(rho + self.r * j * self.tq, self.halo + rho + self.r * (j * self.tq - HALF_WIN), j)
                for j in range(self.nsub) for rho in range(self.r)]

    def own(self, start):
        return pl.ds(start, self.tq, stride=self.r)

    def around(self, start):
        return pl.ds(start, self.band, stride=self.r)

    def fill(self, dst, prev_ref, cur_ref, next_ref):
        dst[:self.halo, :] = prev_ref[...].astype(F32)
        dst[self.halo:self.halo + self.block, :] = cur_ref[...].astype(F32)
        dst[self.halo + self.block:, :] = next_ref[...].astype(F32)

    def first_position(self, j):
        return (pl.program_id(1) * self.block) // self.r + j * self.tq

    def outside(self, j, copies=1):
        pos = self.first_position(j) - HALF_WIN + lax.broadcasted_iota(jnp.int32, (1, copies * self.band), 1) % self.band
        return jnp.where((pos >= 0) & (pos < self.length), 0.0, NEG_BIG)

    def extended(self):
        return pltpu.VMEM((self.block + 2 * self.halo, 128), F32)

    def plain(self):
        return pltpu.VMEM((self.block, 128), F32)


def _low_lanes(rows):
    return lax.broadcasted_iota(jnp.int32, (rows, 128), 1) < HEAD_DIM


def _one_head(x, low, half):
    return jnp.where(low if half == 0 else jnp.logical_not(low), x, 0.0).astype(BF16)


def _two_heads(x, low):
    return jnp.concatenate([_one_head(x, low, 0), _one_head(x, low, 1)], axis=0)


def _carry_ride(base_body, n_in, n_out, n_scratch, ride, grid):
    if ride is None:
        return base_body
    n = ride.n

    def body(*refs):
        o = n_in + n
        ins, ride_ins = refs[:n_in], refs[n_in:o]
        outs, ride_outs = refs[o:o + n_out], refs[o + n_out:o + n_out + n]
        o += n_out + n
        scratch, sems = refs[o:o + n_scratch], refs[o + n_scratch:]
        at_first = (pl.program_id(0) == 0) & (pl.program_id(1) == 0)
        at_last = (pl.program_id(0) == grid[0] - 1) & (pl.program_id(1) == grid[1] - 1)

        @pl.when(at_first)
        def _():
            ride.start(ride_ins, ride_outs, sems)

        base_body(*ins, *outs, *scratch)

        @pl.when(at_last)
        def _():
            ride.finish(ride_ins, ride_outs, sems)

    return body


def _ride_call(base_body, name, grid, in_specs, out_specs, out_shape, scratch_shapes, operands, ride):
    n = 0 if ride is None else ride.n
    extra = [] if ride is None else ride.operands
    outs = pl.pallas_call(
        _carry_ride(base_body, len(in_specs), len(out_specs), len(scratch_shapes), ride, grid), name=name, grid=grid,
        in_specs=list(in_specs) + [ANY] * n, out_specs=list(out_specs) + [ANY] * n,
        out_shape=list(out_shape) + ([] if ride is None else ride.out_shape()),
        scratch_shapes=list(scratch_shapes) + ([] if ride is None else ride.scratch_shapes()),
        compiler_params=_cparams(("arbitrary", "arbitrary")),
    )(*operands, *extra)
    return outs[:len(out_specs)], outs[len(out_specs):]


def _swa_fwd(q, k, v, bias, r, ride=None):
    geo = _SwaGeometry(q.shape[0], r)
    prev, cur, nxt = geo.specs()

    def body(q_ref, kp, kc, kn, vp, vc, vn, b_ref, o_ref, lse_ref, qf, kf, vf):
        qf[...] = q_ref[...].astype(F32)
        geo.fill(kf, kp, kc, kn)
        geo.fill(vf, vp, vc, vn)
        tq = geo.tq
        low_q = _low_lanes(tq)
        bias = b_ref[...].reshape(2 * tq, geo.band)
        for own, around, j in geo.tiles():
            q2 = _two_heads(qf[geo.own(own), :], low_q)
            kb = kf[geo.around(around), :].astype(BF16)
            vb = vf[geo.around(around), :].astype(BF16)
            s = _mm_nt(q2, kb) + bias + geo.outside(j)
            m = jnp.max(s, axis=1, keepdims=True)
            e = jnp.exp(s - m)
            l = jnp.sum(e, axis=1, keepdims=True)
            o2 = _mm(e.astype(BF16), vb) / l
            lse2 = m + jnp.log(l)
            o_ref[geo.own(own), :] = jnp.where(low_q, o2[:tq], o2[tq:])
            lse_ref[geo.own(own), :] = jnp.where(low_q, lse2[:tq], lse2[tq:])

    sds = jax.ShapeDtypeStruct
    (o, lse), carried = _ride_call(
        body, "swa_fwd_%d" % r, (D_B // 128, geo.nblk),
        [cur, prev, cur, nxt, prev, cur, nxt, pl.BlockSpec((2, geo.tq, geo.band), lambda c, i: (c, 0, 0))],
        [cur, cur], [sds(q.shape, F32), sds(q.shape, F32)], [geo.plain(), geo.extended(), geo.extended()],
        (q, k, k, k, v, v, v, bias), ride)
    return o, lse, carried


def _head_stats(st, half):
    lo = HEAD_DIM * half
    return st[:, lo:lo + 1], st[:, lo + HEAD_DIM // 2:lo + HEAD_DIM // 2 + 1]


def _swa_bwd_q(q, k, v, dy, st, bias, r, ride=None):
    geo = _SwaGeometry(q.shape[0], r)
    prev, cur, nxt = geo.specs()
    bias_spec = pl.BlockSpec((2, geo.tq, geo.band), lambda c, i: (c, 0, 0))

    def body(q_ref, kp, kc, kn, vp, vc, vn, dy_ref, st_ref, b_ref, dq_ref, db_ref, qf, kf, vf, dyf):
        @pl.when(pl.program_id(1) == 0)
        def _():
            db_ref[...] = jnp.zeros_like(db_ref)

        qf[...] = q_ref[...].astype(F32)
        dyf[...] = dy_ref[...].astype(F32)
        geo.fill(kf, kp, kc, kn)
        geo.fill(vf, vp, vc, vn)
        tq = geo.tq
        low_q = _low_lanes(tq)
        bias = b_ref[...].reshape(2 * tq, geo.band)
        for own, around, j in geo.tiles():
            sts = st_ref[geo.own(own), :]
            (lse0, delta0), (lse1, delta1) = _head_stats(sts, 0), _head_stats(sts, 1)
            lse = jnp.concatenate([lse0, lse1], axis=0)
            delta = jnp.concatenate([delta0, delta1], axis=0)
            kb = kf[geo.around(around), :].astype(BF16)
            vb = vf[geo.around(around), :].astype(BF16)
            s = _mm_nt(_two_heads(qf[geo.own(own), :], low_q), kb) + bias + geo.outside(j)
            p = jnp.exp(s - lse)
            ds = p * (_mm_nt(_two_heads(dyf[geo.own(own), :], low_q), vb) - delta)
            db_ref[...] += ds.reshape(2, tq, geo.band)
            dq2 = _mm(ds.astype(BF16), kb)
            dq_ref[geo.own(own), :] = jnp.where(low_q, dq2[:tq], dq2[tq:])

    (dq, dbias), carried = _ride_call(
        body, "swa_bwd_q_%d" % r, (D_B // 128, geo.nblk),
        [cur, prev, cur, nxt, prev, cur, nxt, cur, cur, bias_spec], [cur, bias_spec],
        [jax.ShapeDtypeStruct(q.shape, F32), jax.ShapeDtypeStruct(bias.shape, F32)],
        [geo.plain(), geo.extended(), geo.extended(), geo.plain()], (q, k, k, k, v, v, v, dy, st, bias), ride)
    return dq, dbias, carried


def _swa_bwd_kv(q, k, v, dy, st, bias_kv, r):
    geo = _SwaGeometry(q.shape[0], r)
    prev, cur, nxt = geo.specs()

    def body(k_ref, v_ref, qp, qc, qn, dp_, dc_, dn_, sp, sc, sn, b_ref, dk_ref, dv_ref, kf, vf, qf, dyf, stf):
        kf[...] = k_ref[...].astype(F32)
        vf[...] = v_ref[...].astype(F32)
        geo.fill(qf, qp, qc, qn)
        geo.fill(dyf, dp_, dc_, dn_)
        geo.fill(stf, sp, sc, sn)
        band = geo.band
        low_b = _low_lanes(band)
        bias = jnp.concatenate([b_ref[0], b_ref[1]], axis=1)
        half_lanes = HEAD_DIM // 2
        for own, around, j in geo.tiles():
            ks = kf[geo.own(own), :].astype(BF16)
            vs = vf[geo.own(own), :].astype(BF16)
            q2 = _two_heads(qf[geo.around(around), :], low_b)
            dy2 = _two_heads(dyf[geo.around(around), :], low_b)
            st_t = stf[geo.around(around), :].T
            lse = jnp.concatenate([st_t[:1, :], st_t[HEAD_DIM:HEAD_DIM + 1, :]], axis=1)
            delta = jnp.concatenate([st_t[half_lanes:half_lanes + 1, :],
                                     st_t[HEAD_DIM + half_lanes:HEAD_DIM + half_lanes + 1, :]], axis=1)
            p = jnp.exp(_mm_nt(ks, q2) + bias + (geo.outside(j, copies=2) - lse))
            ds = p * (_mm_nt(vs, dy2) - delta)
            dv_ref[geo.own(own), :] = _mm(p.astype(BF16), dy2)
            dk_ref[geo.own(own), :] = _mm(ds.astype(BF16), q2)

    return pl.pallas_call(
        body, name="swa_bwd_kv_%d" % r, grid=(D_B // 128, geo.nblk),
        in_specs=[cur, cur, prev, cur, nxt, prev, cur, nxt, prev, cur, nxt,
                  pl.BlockSpec((2, geo.tq, geo.band), lambda c, i: (c, 0, 0))],
        out_specs=[cur, cur],
        out_shape=[jax.ShapeDtypeStruct(q.shape, F32), jax.ShapeDtypeStruct(q.shape, F32)],
        scratch_shapes=[geo.plain(), geo.plain(), geo.extended(), geo.extended(), geo.extended()],
        compiler_params=_cparams(("parallel", "parallel")),
    )(k, v, q, q, q, dy, dy, dy, st, st, st, bias_kv)


BIAS_ROWS = 16
BIAS_TN = 4096


def _bias_tiles(onehot, rel_bias_t):
    n = onehot.shape[0]

    def body(oh_ref, rb_ref, o_ref):
        o_ref[...] = sum(_mm_nt(piece, oh_ref[...]) for piece in _split3(rb_ref[...]))

    return pl.pallas_call(
        body, name="bias_tiles", grid=(n // BIAS_TN,),
        in_specs=[_rows(BIAS_TN, 128), _full((BIAS_ROWS, 128))],
        out_specs=pl.BlockSpec((BIAS_ROWS, BIAS_TN), lambda i: (0, i)),
        out_shape=jax.ShapeDtypeStruct((BIAS_ROWS, n), F32),
        compiler_params=_cparams(("parallel",)),
    )(onehot, rel_bias_t)


def _bias_bwd(onehot, dbias_rows, so_far, r):
    n = onehot.shape[0]

    def body(oh, d, prev_ref, g_ref):
        @pl.when(pl.program_id(0) == 0)
        def _():
            g_ref[...] = prev_ref[...]

        hi, lo, _ = _split3(d[...])
        g_ref[...] += _mm(hi, oh[...]) + _mm(lo, oh[...])

    return pl.pallas_call(
        body, name="bias_bwd_%d" % r, grid=(n // BIAS_TN,),
        in_specs=[_rows(BIAS_TN, 128), pl.BlockSpec((BIAS_ROWS, BIAS_TN), lambda i: (0, i)), _full((BIAS_ROWS, 128))],
        out_specs=_full((BIAS_ROWS, 128)),
        out_shape=jax.ShapeDtypeStruct((BIAS_ROWS, 128), F32),
        compiler_params=_cparams(("arbitrary",)),
    )(onehot, dbias_rows, so_far)


LATE = ("w_out", "w_ff1", "w_ff2", "w_ple_gate", "w_ple_proj")


def _local_step(x, p, tgt, w_in, late_shards, g_attn_pre, g_q, g_k, g_out_a, g_out_b, g_attn_post, rel_bias,
                g_mlp_pre, g_mlp_post, g_ple):
    s_len = x.shape[0]
    cc, ss = _rope_tables(s_len)
    gq2 = jnp.concatenate([g_q, g_q], axis=-1)
    gk2 = jnp.concatenate([g_k, g_k], axis=-1)
    ones128 = _group_ones(128)
    rel_bias_t = jnp.zeros((BIAS_ROWS, 128), F32).at[:N_HEADS_B, :N_BUCKETS].set(rel_bias.T)

    xn1, qpre, kpre, qa, kv, qb, kb, vb = _in_proj(x, g_attn_pre, w_in, cc, ss, gq2, gk2, ones128)
    ya, lse_a = _attn_a_fwd(qa, kv)

    tiles, outs, lses = [], [], []
    for r in DILATIONS:
        tq = SWA_TQ
        onehot = _bucket_onehot(tq, r)
        bias = _bias_tiles(onehot, rel_bias_t)[:N_HEADS_B].reshape(N_HEADS_B, tq, tq + 2 * HALF_WIN)
        bias = jnp.where(_in_window(tq), bias, NEG_BIG)
        o_r, lse_r, gathered = _swa_fwd(qb, kb, vb, bias, r, _GatherRide(late_shards) if r == DILATIONS[-1] else None)
        tiles.append((onehot, bias))
        outs.append(o_r)
        lses.append(lse_r)
    yb, lse_b = _merge_b(outs, lses)
    w_out, w_ff1, w_ff2, w_gate, w_ple = (_whole(n, g, mine) for n, g, mine in zip(LATE, gathered, late_shards))

    ycat, y2, h1, xn2 = _out_proj(ya, yb, x, g_out_a, g_out_b, w_out, g_attn_post, g_mlp_pre)
    u = _ff1(xn2, w_ff1)
    f2, h2, xn3 = _ff2(u, w_ff2, h1, g_mlp_post, g_ple)
    dh2, df2, dgl, dpp, loss, dg_ple, dg_mlp_post = _ple_loss(xn3, p, h2, f2, tgt, w_gate, w_ple, g_ple, g_mlp_post)

    grads = {"g_ple": dg_ple, "g_mlp_post": dg_mlp_post}
    grads["w_ple_gate"] = _dw(xn3, dgl, "dw_gate")
    grads["w_ple_proj"] = _dw(p, dpp, "dw_ple")
    grads["w_ff2"] = _dw(u, df2, "dw_ff2", relu2=True)
    du = _ff2_bwd(df2, w_ff2, u)
    grads["w_ff1"] = _dw(xn2, du, "dw_ff1")
    dh1, dy2, grads["g_mlp_pre"], grads["g_attn_post"] = _ff1_bwd(du, w_ff1, dh2, h1, y2, g_mlp_pre, g_attn_post)
    grads["w_out"] = _dw(ycat, dy2, "dw_out")
    dya, dyb, delta_a, st_b, grads["g_out_a"], grads["g_out_b"] = _out_proj_bwd(dy2, w_out, ya, yb, lse_b, g_out_a,
                                                                              g_out_b)

    pairs = _pair_sums(LATE, [grads[n] for n in LATE])

    dqr, dkv_t = _attn_a_bwd(qa, dya, kv, lse_a, delta_a)
    dkv_a = dkv_t.transpose(0, 2, 1)

    dqs, dks, dvs = [], [], []
    d_rel = jnp.zeros((BIAS_ROWS, 128), F32)
    for r, (onehot, bias) in zip(DILATIONS, tiles):
        dq_r, dbias, scattered = _swa_bwd_q(qb, kb, vb, dyb, st_b, bias, r,
                                            _ScatterRide(pairs) if r == DILATIONS[0] else None)
        if scattered:
            for n, half in zip(LATE, _chip_sums(LATE, pairs, scattered)):
                grads[n] = half
        bias_kv = jnp.flip(bias, axis=(1, 2))
        dk_r, dv_r = _swa_bwd_kv(qb, kb, vb, dyb, st_b, bias_kv, r)
        dbias_rows = jnp.pad(dbias.reshape(N_HEADS_B, -1), ((0, BIAS_ROWS - N_HEADS_B), (0, 0)))
        d_rel = _bias_bwd(onehot, dbias_rows, d_rel, r)
        dqs.append(dq_r)
        dks.append(dk_r)
        dvs.append(dv_r)
    grads["rel_bias"] = d_rel[:N_HEADS_B, :N_BUCKETS].T

    dproj, grad_x, grads["g_attn_pre"], dgq2, dgk2 = _in_proj_bwd(
        dqr, dkv_a, dqs, dks, dvs, qpre, kpre, x, dh1, g_attn_pre, w_in, cc, ss, gq2, gk2, ones128)
    grads["g_q"] = dgq2[:, :HEAD_DIM] + dgq2[:, HEAD_DIM:]
    grads["g_k"] = dgk2[:, :HEAD_DIM] + dgk2[:, HEAD_DIM:]
    grads["w_in"] = _dw(xn1, dproj, "dw_in")
    return loss, grad_x, grads


ANY = pl.BlockSpec(memory_space=pl.ANY)


def _position():
    return lax.axis_index("x"), lax.axis_index("y"), lax.axis_index("c")


def _other_chips(x, y):
    return [(2 * (1 - x) + y, (1 - x, y)), (2 * x + (1 - y), (x, 1 - y)), (2 * (1 - x) + (1 - y), (1 - x, 1 - y))]


def _cast_shards(shards):
    def body(*refs):
        n = len(refs) // 2
        for i_ref, o_ref in zip(refs[:n], refs[n:]):
            o_ref[...] = i_ref[...].astype(BF16)

    return pl.pallas_call(
        body, name="cast_shards",
        in_specs=[pl.BlockSpec(memory_space=pltpu.VMEM)] * len(shards),
        out_specs=[pl.BlockSpec(memory_space=pltpu.VMEM)] * len(shards),
        out_shape=[jax.ShapeDtypeStruct(s.shape, BF16) for s in shards],
        compiler_params=_cparams(),
    )(*shards)


def _gather_weights(shards):
    n = len(shards)

    ride = _GatherRide(shards)

    def body(*refs):
        ride.start(refs[:n], refs[n:2 * n], refs[2 * n:])
        ride.finish(refs[:n], refs[n:2 * n], refs[2 * n:])

    return pl.pallas_call(
        body, name="gather_weights",
        in_specs=[ANY] * n, out_specs=[ANY] * n,
        out_shape=ride.out_shape(), scratch_shapes=ride.scratch_shapes(),
    )(*shards)


class _GatherRide:
    def __init__(self, shards):
        self.operands = list(shards)
        self.n = len(shards)

    def out_shape(self):
        return [jax.ShapeDtypeStruct((N_CHIPS,) + s.shape, s.dtype) for s in self.operands]

    def scratch_shapes(self):
        return [pltpu.SemaphoreType.DMA((3, self.n))] * 4

    @staticmethod
    def _rows(ref, core):
        half = ref.shape[0] // 2
        return pl.ds(pl.multiple_of(core * half, 16), half)

    def _ici(self, ins, outs, sems, k, a, chip):
        x, y, c = _position()
        return pltpu.make_async_remote_copy(ins[a].at[self._rows(ins[a], c), :],
                                            outs[a].at[2 * x + y, self._rows(ins[a], c), :], sems[0].at[k, a],
                                            sems[1].at[k, a], device_id=(*chip, c), device_id_type=MESH)

    def _pass_on(self, ins, outs, sems, k, a, num, core):
        x, y, c = _position()
        half = outs[a].at[num, self._rows(ins[a], core), :]
        return pltpu.make_async_remote_copy(half, half, sems[2].at[k, a], sems[3].at[k, a], device_id=(x, y, 1 - c),
                                            device_id_type=MESH)

    def start(self, ins, outs, sems):
        x, y, _ = _position()
        for k, (_, chip) in enumerate(_other_chips(x, y)):
            for a in range(self.n):
                self._ici(ins, outs, sems, k, a, chip).start()

    def finish(self, ins, outs, sems):
        x, y, c = _position()
        others = _other_chips(x, y)
        for k, (num, chip) in enumerate(others):
            for a in range(self.n):
                landed = outs[a].at[num, self._rows(ins[a], c), :]
                pltpu.make_async_remote_copy(landed, landed, sems[0].at[k, a], sems[1].at[k, a], device_id=(*chip, c),
                                             device_id_type=MESH).wait_recv()
                self._pass_on(ins, outs, sems, k, a, num, c).start()
        for k, (num, chip) in enumerate(others):
            for a in range(self.n):
                self._pass_on(ins, outs, sems, k, a, num, 1 - c).wait_recv()
        for k, (num, chip) in enumerate(others):
            for a in range(self.n):
                self._ici(ins, outs, sems, k, a, chip).wait_send()
                self._pass_on(ins, outs, sems, k, a, num, c).wait_send()


def _send_sibling_half(grads, tag):
    n = len(grads)

    def body(*refs):
        ins, outs = refs[:n], refs[n:2 * n]
        send_sems, recv_sems = refs[2 * n:]
        x, y, c = _position()
        copies = []
        for a in range(n):
            half = ins[a].shape[1] // 2
            theirs = ins[a].at[:, pl.ds(pl.multiple_of((1 - c) * half, 8), half), :]
            cp = pltpu.make_async_remote_copy(theirs, outs[a], send_sems.at[a], recv_sems.at[a],
                                              device_id=(x, y, 1 - c), device_id_type=MESH)
            cp.start()
            copies.append(cp)
        for cp in copies:
            cp.wait()

    return pl.pallas_call(
        body, name="send_sibling_half_" + tag,
        in_specs=[ANY] * n, out_specs=[ANY] * n,
        out_shape=[jax.ShapeDtypeStruct((g.shape[0], g.shape[1] // 2, g.shape[2]), g.dtype) for g in grads],
        scratch_shapes=[pltpu.SemaphoreType.DMA((n,)), pltpu.SemaphoreType.DMA((n,))],
    )(*grads)


def _scatter_to_chips(pairs):
    n = len(pairs)
    ride = _ScatterRide(pairs)

    def body(*refs):
        ride.start(refs[:n], refs[n:2 * n], refs[2 * n:])
        ride.finish(refs[:n], refs[n:2 * n], refs[2 * n:])

    return pl.pallas_call(
        body, name="scatter_to_chips",
        in_specs=[ANY] * n, out_specs=[ANY] * n,
        out_shape=ride.out_shape(), scratch_shapes=ride.scratch_shapes(),
    )(*pairs)


class _ScatterRide:
    def __init__(self, pairs):
        self.operands = list(pairs)
        self.n = len(pairs)

    def out_shape(self):
        return [jax.ShapeDtypeStruct(g.shape, g.dtype) for g in self.operands]

    def scratch_shapes(self):
        return [pltpu.SemaphoreType.DMA((3, self.n))] * 2

    @staticmethod
    def _copy(ins, outs, sems, k, a, src_slot, dst_slot, chip):
        _, _, c = _position()
        return pltpu.make_async_remote_copy(ins[a].at[src_slot], outs[a].at[dst_slot], sems[0].at[k, a],
                                            sems[1].at[k, a], device_id=(*chip, c), device_id_type=MESH)

    def start(self, ins, outs, sems):
        x, y, _ = _position()
        for k, (num, chip) in enumerate(_other_chips(x, y)):
            for a in range(self.n):
                self._copy(ins, outs, sems, k, a, num, 2 * x + y, chip).start()

    def finish(self, ins, outs, sems):
        x, y, _ = _position()
        for k, (num, chip) in enumerate(_other_chips(x, y)):
            for a in range(self.n):
                self._copy(ins, outs, sems, k, a, 2 * x + y, num, chip).wait_recv()
        for k, (num, chip) in enumerate(_other_chips(x, y)):
            for a in range(self.n):
                self._copy(ins, outs, sems, k, a, num, 2 * x + y, chip).wait_send()


def _exchange_halves(halves):
    n = len(halves)

    def body(*refs):
        ins, outs = refs[:n], refs[n:2 * n]
        send_sems, recv_sems = refs[2 * n:]
        x, y, c = _position()
        copies = []
        for a in range(n):
            cp = pltpu.make_async_remote_copy(ins[a], outs[a], send_sems.at[a], recv_sems.at[a],
                                              device_id=(x, y, 1 - c), device_id_type=MESH)
            cp.start()
            copies.append(cp)
        for cp in copies:
            cp.wait()

    return pl.pallas_call(
        body, name="exchange_halves",
        in_specs=[ANY] * n, out_specs=[ANY] * n,
        out_shape=[jax.ShapeDtypeStruct(h.shape, h.dtype) for h in halves],
        scratch_shapes=[pltpu.SemaphoreType.DMA((n,)), pltpu.SemaphoreType.DMA((n,))],
    )(*halves)


def _allreduce_small(v):
    def body(v_ref, o_ref, buf, send_sems, recv_sems):
        x, y, c = _position()
        me = 4 * x + 2 * y + c
        peers = [(1 - x, y, c), (x, 1 - y, c), (x, y, 1 - c), (1 - x, 1 - y, c), (1 - x, y, 1 - c), (x, 1 - y, 1 - c),
                 (1 - x, 1 - y, 1 - c)]
        num = lambda d: 4 * d[0] + 2 * d[1] + d[2]
        buf[me] = v_ref[...]
        sends = []
        for k, peer in enumerate(peers):
            cp = pltpu.make_async_remote_copy(v_ref, buf.at[me], send_sems.at[k], recv_sems.at[k], device_id=peer,
                                              device_id_type=MESH)
            cp.start()
            sends.append(cp)
        for k, peer in enumerate(peers):
            pltpu.make_async_remote_copy(v_ref, buf.at[num(peer)], send_sems.at[k], recv_sems.at[k], device_id=peer,
                                         device_id_type=MESH).wait_recv()
        for cp in sends:
            cp.wait_send()
        total = buf[0]
        for d in range(1, 8):
            total = total + buf[d]
        o_ref[...] = total

    return pl.pallas_call(
        body, name="allreduce_small",
        in_specs=[pl.BlockSpec(memory_space=pltpu.VMEM)], out_specs=pl.BlockSpec(memory_space=pltpu.VMEM),
        out_shape=jax.ShapeDtypeStruct(v.shape, v.dtype),
        scratch_shapes=[pltpu.VMEM((8,) + v.shape, v.dtype), pltpu.SemaphoreType.DMA((7,)),
                        pltpu.SemaphoreType.DMA((7,))],
    )(v)


def _sum_leading(a, name):
    k, r, c = a.shape
    tr = min(r, 256)

    def body(a_ref, o_ref):
        total = a_ref[0].astype(F32)
        for i in range(1, k):
            total = total + a_ref[i].astype(F32)
        o_ref[...] = total

    return pl.pallas_call(
        body, name=name, grid=(r // tr,),
        in_specs=[pl.BlockSpec((k, tr, c), lambda i: (0, i, 0))],
        out_specs=pl.BlockSpec((tr, c), lambda i: (i, 0)),
        out_shape=jax.ShapeDtypeStruct((r, c), F32),
        compiler_params=_cparams(("parallel",)),
    )(a)


def _add(a, b, name):
    k, r, c = a.shape
    tr = min(r, 256)
    spec = pl.BlockSpec((k, tr, c), lambda i: (0, i, 0))

    def body(a_ref, b_ref, o_ref):
        o_ref[...] = (a_ref[...] + b_ref[...]).astype(BF16)

    return pl.pallas_call(
        body, name=name, grid=(r // tr,), in_specs=[spec, spec], out_specs=spec,
        out_shape=jax.ShapeDtypeStruct(a.shape, BF16), compiler_params=_cparams(("parallel",)),
    )(a, b)


def _adamw(w, g, m, v, name):
    r, c = w.shape
    tr = min(r, 256)
    spec = pl.BlockSpec((tr, c), lambda i: (i, 0))

    def body(w_ref, g_ref, m_ref, v_ref, d_ref, nm_ref, nv_ref):
        gv = g_ref[...]
        nm = ADAM_B1 * m_ref[...] + (1.0 - ADAM_B1) * gv
        nv = ADAM_B2 * v_ref[...] + (1.0 - ADAM_B2) * jnp.square(gv)
        m_hat = nm / (1.0 - ADAM_B1 ** ADAM_STEP)
        v_hat = nv / (1.0 - ADAM_B2 ** ADAM_STEP)
        d_ref[...] = -ADAM_LR * (m_hat / (jnp.sqrt(v_hat) + ADAM_EPS) + ADAM_WD * w_ref[...])
        nm_ref[...] = nm
        nv_ref[...] = nv

    return pl.pallas_call(
        body, name=name, grid=(r // tr,), in_specs=[spec] * 4, out_specs=[spec] * 3,
        out_shape=[jax.ShapeDtypeStruct(w.shape, F32)] * 3, compiler_params=_cparams(("parallel",)),
    )(w, g, m, v)


MATRICES = ("w_in", "w_out", "w_ff1", "w_ff2", "w_ple_gate", "w_ple_proj")
COLUMN_SHARDED = ("w_in", "w_ff1", "w_ple_proj")
SMALL = ("g_attn_pre", "g_q", "g_k", "g_out_a", "g_out_b", "g_attn_post", "rel_bias", "g_mlp_pre", "g_mlp_post",
         "g_ple")
WEIGHT_ORDER = ("w_in", "g_attn_pre", "g_q", "g_k", "g_out_a", "g_out_b", "w_out", "g_attn_post", "rel_bias",
                "g_mlp_pre", "w_ff1", "w_ff2", "g_mlp_post", "g_ple", "w_ple_gate", "w_ple_proj")
PACK_ROWS, PACK_COLS = 8, 1024


def _chip():
    return 2 * lax.axis_index("x") + lax.axis_index("y")


def _whole(name, gathered, mine):
    g = lax.dynamic_update_slice_in_dim(gathered, mine[None], _chip(), axis=0)
    if name in COLUMN_SHARDED:
        return g.transpose(1, 0, 2).reshape(g.shape[1], N_CHIPS * g.shape[2])
    return g.reshape(N_CHIPS * g.shape[1], g.shape[2])


def _pair_sums(names, grads):
    by_chip = []
    for n, g in zip(names, grads):
        if n in COLUMN_SHARDED:
            by_chip.append(g.reshape(g.shape[0], N_CHIPS, g.shape[1] // N_CHIPS).transpose(1, 0, 2))
        else:
            by_chip.append(g.reshape(N_CHIPS, g.shape[0] // N_CHIPS, g.shape[1]))
    c = lax.axis_index("c")
    pairs = []
    for n, g, other in zip(names, by_chip, _send_sibling_half(by_chip, names[0])):
        half = g.shape[1] // 2
        pairs.append(_add(lax.dynamic_slice_in_dim(g, c * half, half, axis=1), other, "pair_sum_" + n))
    return pairs


def _chip_sums(names, pairs, scattered):
    halves = []
    for n, pair, got in zip(names, pairs, scattered):
        own = lax.dynamic_slice_in_dim(pair, _chip(), 1, axis=0)
        halves.append(_sum_leading(lax.dynamic_update_slice_in_dim(got, own, _chip(), axis=0), "chip_sum_" + n))
    return halves


def _pack_small(values, extra=None):
    flat = [values[n].reshape(-1) for n in SMALL]
    used = sum(f.shape[0] for f in flat)
    tail = jnp.zeros((PACK_ROWS * PACK_COLS - used - 1,), F32)
    last = jnp.zeros((1,), F32) if extra is None else extra.reshape(1)
    return jnp.concatenate(flat + [tail, last]).reshape(PACK_ROWS, PACK_COLS)


def _unpack_small(packed, like):
    flat = packed.reshape(-1)
    out, o = {}, 0
    for n in SMALL:
        size = like[n].size
        out[n] = flat[o:o + size].reshape(like[n].shape)
        o += size
    return out, flat[-1]


def kernel(x, p, w_in, g_attn_pre, g_q, g_k, g_out_a, g_out_b, w_out, g_attn_post, rel_bias, g_mlp_pre, w_ff1, w_ff2, g_mlp_post, g_ple, w_ple_gate, w_ple_proj, loss_target, m_w_in, m_g_attn_pre, m_g_q, m_g_k, m_g_out_a, m_g_out_b, m_w_out, m_g_attn_post, m_rel_bias, m_g_mlp_pre, m_w_ff1, m_w_ff2, m_g_mlp_post, m_g_ple, m_w_ple_gate, m_w_ple_proj, v_w_in, v_g_attn_pre, v_g_q, v_g_k, v_g_out_a, v_g_out_b, v_w_out, v_g_attn_post, v_rel_bias, v_g_mlp_pre, v_w_ff1, v_w_ff2, v_g_mlp_post, v_g_ple, v_w_ple_gate, v_w_ple_proj):
    given = dict(locals())
    weights = {n: given[n] for n in WEIGHT_ORDER}
    shards = {n: weights[n][0] for n in MATRICES}

    c = lax.axis_index("c")
    own = dict(zip(MATRICES, _cast_shards([shards[n] for n in MATRICES])))
    w_in_whole = _whole("w_in", _gather_weights([own["w_in"]])[0], own["w_in"])

    loss, grad_x, grads = _local_step(
        x[0], p[0, 0], loss_target[0], w_in_whole, [own[n] for n in LATE], g_attn_pre, g_q, g_k, g_out_a, g_out_b,
        g_attn_post, rel_bias, g_mlp_pre, g_mlp_post, g_ple)

    pairs = _pair_sums(["w_in"], [grads["w_in"]])
    grads["w_in"] = _chip_sums(["w_in"], pairs, _scatter_to_chips(pairs))[0]
    halves = [grads[n] for n in MATRICES]
    grad_w = {}
    for n, mine, theirs in zip(MATRICES, halves, _exchange_halves(halves)):
        half = mine.shape[0]
        g = jnp.zeros((2 * half, mine.shape[1]), F32)
        g = lax.dynamic_update_slice_in_dim(g, mine, c * half, axis=0)
        grad_w[n] = lax.dynamic_update_slice_in_dim(g, theirs, (1 - c) * half, axis=0)

    small_like = {n: weights[n] for n in SMALL}
    reduced = _allreduce_small(_pack_small({n: grads[n] for n in SMALL}, extra=loss))
    grad_small, loss_total = _unpack_small(reduced, small_like)

    delta, new_m, new_v = {}, {}, {}
    for n in MATRICES:
        d, nm, nv = _adamw(shards[n], grad_w[n], given["m_" + n][0], given["v_" + n][0], "adamw_" + n)
        delta[n], new_m[n], new_v[n] = d[None], nm[None], nv[None]
        grad_w[n] = grad_w[n][None]
    d, nm, nv = _adamw(_pack_small(small_like), reduced, _pack_small({n: given["m_" + n] for n in SMALL}),
                       _pack_small({n: given["v_" + n] for n in SMALL}), "adamw_small")
    d_small, nm_small, nv_small = (_unpack_small(a, small_like)[0] for a in (d, nm, nv))
    for n in SMALL:
        grad_w[n], delta[n], new_m[n], new_v[n] = grad_small[n], d_small[n], nm_small[n], nv_small[n]

    return (loss_total, grad_x[None], *[grad_w[n] for n in WEIGHT_ORDER], *[delta[n] for n in WEIGHT_ORDER],
            *[new_m[n] for n in WEIGHT_ORDER], *[new_v[n] for n in WEIGHT_ORDER])
```

```python
import functools
import math

import jax
import jax.numpy as jnp
from jax import lax
from jax.experimental import pallas as pl
from jax.experimental.pallas import tpu as pltpu

F32 = jnp.float32
BF16 = jnp.bfloat16

D_MODEL = 1024
HEAD_DIM = 64
N_HEADS_A = 8
N_KV_A = 2
GROUP_A = N_HEADS_A // N_KV_A
N_HEADS_B = 8
D_A = N_HEADS_A * HEAD_DIM
D_KV_A = N_KV_A * HEAD_DIM
D_B = N_HEADS_B * HEAD_DIM
D_IN = D_A + 2 * D_KV_A + 3 * D_B
D_FF = 4 * D_MODEL
D_PLE = 256
GRID_W = 64
ROPE_THETA = 10000.0
DILATIONS = (1, 4, 16)
HALF_WIN = 64
N_BUCKETS = 32
MAX_DISTANCE = 1024
EPS = 1e-6
NEG_BIG = -1e30
Q_SCALE = HEAD_DIM ** -0.5

ADAM_LR = 0.001
ADAM_B1 = 0.9
ADAM_B2 = 0.999
ADAM_EPS = 1e-08
ADAM_WD = 0.01
ADAM_STEP = 10

N_CHIPS = 4
MESH = pl.DeviceIdType.MESH

ROW_TILE = 512
ATT_TQ = 256
ATT_TQ_BWD = 512
ATT_TK_FWD = 2048
ATT_UNROLL_FWD = 8
ATT_TK_BWD = 512
ATT_UNROLL_BWD = 8
SWA_TQ = 128
SWA_MIN_BLOCK = 1024
DW_TS = 2048
VMEM_LIMIT = 56 * 1024 * 1024

NT = (((1,), (1,)), ((), ()))
TN = (((0,), (0,)), ((), ()))


def _cparams(sem=None, vmem=VMEM_LIMIT):
    return pltpu.CompilerParams(dimension_semantics=sem, vmem_limit_bytes=vmem)


def _full(shape):
    n = len(shape)
    return pl.BlockSpec(shape, lambda *_: (0,) * n)


def _rows(tm, width):
    return pl.BlockSpec((tm, width), lambda i: (i, 0))


def _split3(a):
    a1 = a.astype(BF16)
    r = a - a1.astype(F32)
    a2 = r.astype(BF16)
    a3 = (r - a2.astype(F32)).astype(BF16)
    return a1, a2, a3


def _xdot(a, sel):
    a1, a2, a3 = _split3(a)
    d = lambda p: jnp.dot(p, sel, preferred_element_type=F32)
    return d(a1) + d(a2) + d(a3)


def _mm(a, b):
    return jnp.dot(a, b, preferred_element_type=F32)


def _mm_nt(a, b):
    return lax.dot_general(a, b, NT, preferred_element_type=F32)


def _mm_tn(a, b):
    return lax.dot_general(a, b, TN, preferred_element_type=F32)


def _rms_stats(x):
    r = lax.rsqrt(jnp.mean(x * x, axis=-1, keepdims=True) + EPS)
    return x * r, r


def _rms_bwd(dy, xh, r, g):
    gdy = dy * g
    dx = r * (gdy - xh * jnp.mean(gdy * xh, axis=-1, keepdims=True))
    dg = jnp.sum(dy * xh, axis=0, keepdims=True)
    return dx, dg


def _acc_out(ref, val):
    @pl.when(pl.program_id(0) == 0)
    def _():
        ref[...] = jnp.zeros_like(ref)

    ref[...] += val


def _swap_halves(x, first_half):
    return jnp.where(first_half, pltpu.roll(x, 96, 1), pltpu.roll(x, 32, 1))


def _first_half_mask(shape):
    return (lax.broadcasted_iota(jnp.int32, shape, 1) % HEAD_DIM) < (HEAD_DIM // 2)


def _rope_tables(s_len):
    t = jnp.arange(s_len)
    row = (t // GRID_W).astype(F32)
    col = (t % GRID_W).astype(F32)
    n_axis = HEAD_DIM // 4
    inv_freq = ROPE_THETA ** (-jnp.arange(n_axis, dtype=F32) / n_axis)
    ang = jnp.concatenate([row[:, None] * inv_freq, col[:, None] * inv_freq], axis=-1)
    c, s = jnp.cos(ang), jnp.sin(ang)
    cc = jnp.concatenate([c, c, c, c], axis=-1)
    ss = jnp.concatenate([-s, s, -s, s], axis=-1)
    return cc, ss


def _group_ones(width):
    i = jnp.arange(width)
    return (i[:, None] // HEAD_DIM == i[None, :] // HEAD_DIM).astype(BF16)


def _t5_bucket(rel):
    nb = N_BUCKETS // 2
    max_exact = nb // 2
    side = jnp.where(rel > 0, nb, 0)
    n = jnp.abs(rel)
    large = max_exact + (jnp.log(jnp.maximum(n, max_exact).astype(F32) / max_exact)
                         / math.log(MAX_DISTANCE / max_exact) * (nb - max_exact)).astype(jnp.int32)
    large = jnp.minimum(large, nb - 1)
    return side + jnp.where(n < max_exact, n, large)


def _in_window(tq):
    qi = jnp.arange(tq)
    kj = jnp.arange(tq + 2 * HALF_WIN)
    return jnp.abs(kj[None, :] - HALF_WIN - qi[:, None]) <= HALF_WIN


def _bucket_onehot(tq, dilation):
    qi = jnp.arange(tq)
    kj = jnp.arange(tq + 2 * HALF_WIN)
    rel = kj[None, :] - HALF_WIN - qi[:, None]
    bucket = _t5_bucket(rel * dilation).reshape(-1)
    return (bucket[:, None] == jnp.arange(128)[None, :]).astype(BF16)


def _in_proj(x, g1, w_in, cc, ss, gq2, gk2, ones128):
    s_len = x.shape[0]
    tm = min(ROW_TILE, s_len)

    def body(x_ref, g_ref, w_ref, cc_ref, ss_ref, gq_ref, gk_ref, one_ref,
             xn_ref, qpre_ref, kpre_ref, qa_ref, kv_ref, qb_ref, kb_ref, vb_ref):
        xh, _ = _rms_stats(x_ref[...])
        xn = (xh * g_ref[...]).astype(BF16)
        xn_ref[...] = xn
        proj = _mm(xn, w_ref[...])
        first_half = _first_half_mask((tm, 128))
        ones = one_ref[...]
        cc_t, ss_t = cc_ref[...], ss_ref[...]

        def norm_rope(xc, gain):
            ms = _xdot(xc * xc, ones) * (1.0 / HEAD_DIM)
            y = xc * lax.rsqrt(ms + EPS) * gain
            return y * cc_t + _swap_halves(y, first_half) * ss_t

        qpre_ref[...] = proj[:, :D_A]
        kpre_ref[...] = proj[:, D_A:D_A + D_KV_A]
        for c in range(D_A // 128):
            y = norm_rope(proj[:, 128 * c:128 * (c + 1)], gq_ref[...])
            qa_ref[:, 128 * c:128 * (c + 1)] = (y * Q_SCALE).astype(BF16)
        ka = norm_rope(proj[:, D_A:D_A + D_KV_A], gk_ref[...])
        o = D_A + D_KV_A
        va = proj[:, o:o + D_KV_A]
        low = _low_lanes(tm)
        kv_ref[0] = jnp.where(low, ka, pltpu.roll(va, HEAD_DIM, 1)).astype(BF16)
        kv_ref[1] = jnp.where(low, pltpu.roll(ka, HEAD_DIM, 1), va).astype(BF16)
        o += D_KV_A
        qb_ref[...] = (proj[:, o:o + D_B] * Q_SCALE).astype(BF16)
        kb_ref[...] = proj[:, o + D_B:o + 2 * D_B].astype(BF16)
        vb_ref[...] = proj[:, o + 2 * D_B:o + 3 * D_B].astype(BF16)

    sds = jax.ShapeDtypeStruct
    return pl.pallas_call(
        body, name="in_proj", grid=(s_len // tm,),
        in_specs=[_rows(tm, D_MODEL), _full((1, D_MODEL)), _full((D_MODEL, D_IN)), _rows(tm, 128), _rows(tm, 128),
                  _full((1, 128)), _full((1, 128)), _full((128, 128))],
        out_specs=[_rows(tm, D_MODEL), _rows(tm, D_A), _rows(tm, D_KV_A), _rows(tm, D_A),
                   pl.BlockSpec((N_KV_A, tm, 128), lambda i: (0, i, 0)), _rows(tm, D_B), _rows(tm, D_B),
                   _rows(tm, D_B)],
        out_shape=[sds((s_len, D_MODEL), BF16), sds((s_len, D_A), F32), sds((s_len, D_KV_A), F32),
                   sds((s_len, D_A), BF16), sds((N_KV_A, s_len, 128), BF16),
                   sds((s_len, D_B), BF16), sds((s_len, D_B), BF16), sds((s_len, D_B), BF16)],
        compiler_params=_cparams(("parallel",)),
    )(x, g1, w_in, cc, ss, gq2, gk2, ones128)


def _stat_spec(tm):
    return pl.BlockSpec((N_HEADS_B, tm, 1), lambda i: (0, i, 0))


def _merge_b(outs, lses):
    s_len = outs[0].shape[0]
    tm = min(ROW_TILE, s_len)

    def body(o0, o1, o2, l0, l1, l2, yb_ref, lse_ref):
        m_all = jnp.maximum(jnp.maximum(l0[...], l1[...]), l2[...])
        w = [jnp.exp(l[...] - m_all) for l in (l0, l1, l2)]
        den = w[0] + w[1] + w[2]
        yb_ref[...] = (w[0] * o0[...] + w[1] * o1[...] + w[2] * o2[...]) / den
        lse_ref[...] = m_all + jnp.log(den)

    return pl.pallas_call(
        body, name="merge_b", grid=(s_len // tm,),
        in_specs=[_rows(tm, D_B)] * 6,
        out_specs=[_rows(tm, D_B), _rows(tm, D_B)],
        out_shape=[jax.ShapeDtypeStruct((s_len, D_B), F32), jax.ShapeDtypeStruct((s_len, D_B), F32)],
        compiler_params=_cparams(("parallel",)),
    )(*outs, *lses)


def _out_proj(ya, yb, x, g_a, g_b, w_out, g_post, g_mlp_pre):
    s_len = x.shape[0]
    tm = min(ROW_TILE, s_len)

    def body(ya_ref, yb_ref, x_ref, ga_ref, gb_ref, w_ref, gp_ref, gm_ref, ycat_ref, y2_ref, h1_ref, xn2_ref):
        ah, _ = _rms_stats(ya_ref[...])
        bh, _ = _rms_stats(yb_ref[...])
        ycat = jnp.concatenate([ah * ga_ref[...], bh * gb_ref[...]], axis=-1).astype(BF16)
        ycat_ref[...] = ycat
        y2 = _mm(ycat, w_ref[...])
        y2_ref[...] = y2
        y2h, _ = _rms_stats(y2)
        h1 = x_ref[...] + y2h * gp_ref[...]
        h1_ref[...] = h1
        h1h, _ = _rms_stats(h1)
        xn2_ref[...] = (h1h * gm_ref[...]).astype(BF16)

    sds = jax.ShapeDtypeStruct
    return pl.pallas_call(
        body, name="out_proj", grid=(s_len // tm,),
        in_specs=[_rows(tm, D_A), _rows(tm, D_B), _rows(tm, D_MODEL), _full((1, D_A)), _full((1, D_B)),
                  _full((D_MODEL, D_MODEL)), _full((1, D_MODEL)), _full((1, D_MODEL))],
        out_specs=[_rows(tm, D_MODEL)] * 4,
        out_shape=[sds((s_len, D_MODEL), BF16), sds((s_len, D_MODEL), F32), sds((s_len, D_MODEL), F32),
                   sds((s_len, D_MODEL), BF16)],
        compiler_params=_cparams(("parallel",)),
    )(ya, yb, x, g_a, g_b, w_out, g_post, g_mlp_pre)


def _ff1(xn2, w_ff1):
    s_len = xn2.shape[0]
    tm = min(ROW_TILE, s_len)

    def body(x_ref, w_ref, u_ref):
        u_ref[...] = _mm(x_ref[...], w_ref[...])

    return pl.pallas_call(
        body, name="ff1", grid=(s_len // tm,),
        in_specs=[_rows(tm, D_MODEL), _full((D_MODEL, D_FF))],
        out_specs=_rows(tm, D_FF),
        out_shape=jax.ShapeDtypeStruct((s_len, D_FF), F32),
        compiler_params=_cparams(("parallel",)),
    )(xn2, w_ff1)


def _ff2(u, w_ff2, h1, g_post, g_ple):
    s_len = u.shape[0]
    tm = min(ROW_TILE, s_len)

    def body(u_ref, w_ref, h1_ref, gp_ref, gl_ref, f2_ref, h2_ref, xn3_ref):
        f = jnp.square(jnp.maximum(u_ref[...], 0.0)).astype(BF16)
        f2 = _mm(f, w_ref[...])
        f2_ref[...] = f2
        f2h, _ = _rms_stats(f2)
        h2 = h1_ref[...] + f2h * gp_ref[...]
        h2_ref[...] = h2
        h2h, _ = _rms_stats(h2)
        xn3_ref[...] = (h2h * gl_ref[...]).astype(BF16)

    sds = jax.ShapeDtypeStruct
    return pl.pallas_call(
        body, name="ff2", grid=(s_len // tm,),
        in_specs=[_rows(tm, D_FF), _full((D_FF, D_MODEL)), _rows(tm, D_MODEL), _full((1, D_MODEL)),
                  _full((1, D_MODEL))],
        out_specs=[_rows(tm, D_MODEL)] * 3,
        out_shape=[sds((s_len, D_MODEL), F32), sds((s_len, D_MODEL), F32), sds((s_len, D_MODEL), BF16)],
        compiler_params=_cparams(("parallel",)),
    )(u, w_ff2, h1, g_post, g_ple)


def _ple_loss(xn3, p, h2, f2, tgt, w_gate, w_ple, g_ple, g_mlp_post):
    s_len = h2.shape[0]
    tm = min(ROW_TILE, s_len)

    def body(xn3_ref, p_ref, h2_ref, f2_ref, t_ref, wg_ref, wp_ref, gl_ref, gp_ref,
             dh2_ref, df2_ref, dgl_ref, dpp_ref, loss_ref, dgple_ref, dgpost_ref):
        gate = jax.nn.sigmoid(_mm(xn3_ref[...], wg_ref[...]))
        pp = _mm(p_ref[...].astype(BF16), wp_ref[...])
        h2 = h2_ref[...]
        err = h2 + gate * pp - t_ref[...]
        sq = jnp.sum(jnp.sum(err * err, axis=1, keepdims=True), axis=0, keepdims=True)
        _acc_out(loss_ref, sq * (0.5 / D_MODEL))
        dh3 = err * (1.0 / D_MODEL)
        dgl = (dh3 * pp) * gate * (1.0 - gate)
        dgl_b = dgl.astype(BF16)
        dgl_ref[...] = dgl_b
        dpp_ref[...] = (dh3 * gate).astype(BF16)
        dxn3 = _mm_nt(dgl_b, wg_ref[...])
        h2h, r2 = _rms_stats(h2)
        dx, dg = _rms_bwd(dxn3, h2h, r2, gl_ref[...])
        _acc_out(dgple_ref, dg)
        dh2 = dh3 + dx
        dh2_ref[...] = dh2
        f2h, rf = _rms_stats(f2_ref[...])
        df2, dg = _rms_bwd(dh2, f2h, rf, gp_ref[...])
        _acc_out(dgpost_ref, dg)
        df2_ref[...] = df2.astype(BF16)

    sds = jax.ShapeDtypeStruct
    return pl.pallas_call(
        body, name="ple_loss", grid=(s_len // tm,),
        in_specs=[_rows(tm, D_MODEL), _rows(tm, D_PLE), _rows(tm, D_MODEL), _rows(tm, D_MODEL), _rows(tm, D_MODEL),
                  _full((D_MODEL, D_MODEL)), _full((D_PLE, D_MODEL)), _full((1, D_MODEL)), _full((1, D_MODEL))],
        out_specs=[_rows(tm, D_MODEL)] * 3 + [_rows(tm, D_MODEL), _full((1, 1)), _full((1, D_MODEL)),
                                              _full((1, D_MODEL))],
        out_shape=[sds((s_len, D_MODEL), F32), sds((s_len, D_MODEL), BF16), sds((s_len, D_MODEL), BF16),
                   sds((s_len, D_MODEL), BF16), sds((1, 1), F32), sds((1, D_MODEL), F32), sds((1, D_MODEL), F32)],
        compiler_params=_cparams(("arbitrary",)),
    )(xn3, p, h2, f2, tgt, w_gate, w_ple, g_ple, g_mlp_post)


def _ff2_bwd(df2, w_ff2, u):
    s_len = u.shape[0]
    tm = min(ROW_TILE, s_len)

    def body(d_ref, w_ref, u_ref, du_ref):
        df = _mm_nt(d_ref[...], w_ref[...])
        du_ref[...] = (df * (2.0 * jnp.maximum(u_ref[...], 0.0))).astype(BF16)

    return pl.pallas_call(
        body, name="ff2_bwd", grid=(s_len // tm,),
        in_specs=[_rows(tm, D_MODEL), _full((D_FF, D_MODEL)), _rows(tm, D_FF)],
        out_specs=_rows(tm, D_FF),
        out_shape=jax.ShapeDtypeStruct((s_len, D_FF), BF16),
        compiler_params=_cparams(("parallel",)),
    )(df2, w_ff2, u)


def _ff1_bwd(du, w_ff1, dh2, h1, y2, g_mlp_pre, g_post):
    s_len = du.shape[0]
    tm = min(ROW_TILE, s_len)

    def body(du_ref, w_ref, dh2_ref, h1_ref, y2_ref, gm_ref, gp_ref, dh1_ref, dy2_ref, dgm_ref, dgp_ref):
        dxn2 = _mm_nt(du_ref[...], w_ref[...])
        h1h, r1 = _rms_stats(h1_ref[...])
        dx, dg = _rms_bwd(dxn2, h1h, r1, gm_ref[...])
        _acc_out(dgm_ref, dg)
        dh1 = dh2_ref[...] + dx
        dh1_ref[...] = dh1
        y2h, ry = _rms_stats(y2_ref[...])
        dy2, dg = _rms_bwd(dh1, y2h, ry, gp_ref[...])
        _acc_out(dgp_ref, dg)
        dy2_ref[...] = dy2.astype(BF16)

    sds = jax.ShapeDtypeStruct
    return pl.pallas_call(
        body, name="ff1_bwd", grid=(s_len // tm,),
        in_specs=[_rows(tm, D_FF), _full((D_MODEL, D_FF)), _rows(tm, D_MODEL), _rows(tm, D_MODEL),
                  _rows(tm, D_MODEL), _full((1, D_MODEL)), _full((1, D_MODEL))],
        out_specs=[_rows(tm, D_MODEL), _rows(tm, D_MODEL), _full((1, D_MODEL)), _full((1, D_MODEL))],
        out_shape=[sds((s_len, D_MODEL), F32), sds((s_len, D_MODEL), BF16), sds((1, D_MODEL), F32),
                   sds((1, D_MODEL), F32)],
        compiler_params=_cparams(("arbitrary",)),
    )(du, w_ff1, dh2, h1, y2, g_mlp_pre, g_post)


def _out_proj_bwd(dy2, w_out, ya, yb, lse_b, g_a, g_b):
    s_len = ya.shape[0]
    tm = min(ROW_TILE, s_len)

    def body(d_ref, w_ref, ya_ref, yb_ref, lse_ref, ga_ref, gb_ref, dya_ref, dyb_ref, da_ref, st_ref, dga_ref,
             dgb_ref):
        dycat = _mm_nt(d_ref[...], w_ref[...])
        lane = lax.broadcasted_iota(jnp.int32, (tm, 128), 1)
        low = lane < HEAD_DIM
        is_lse = (lane % HEAD_DIM) < (HEAD_DIM // 2)

        def head_sums(prod_chunk):
            return (jnp.sum(jnp.where(low, prod_chunk, 0.0), axis=1, keepdims=True),
                    jnp.sum(jnp.where(low, 0.0, prod_chunk), axis=1, keepdims=True))

        ya = ya_ref[...]
        yh, r = _rms_stats(ya)
        dya, dg = _rms_bwd(dycat[:, :D_A], yh, r, ga_ref[...])
        _acc_out(dga_ref, dg)
        dya_ref[...] = dya
        prod = dya * ya
        for c in range(D_A // 128):
            da_ref[2 * c], da_ref[2 * c + 1] = head_sums(prod[:, 128 * c:128 * (c + 1)])

        yb = yb_ref[...]
        yh, r = _rms_stats(yb)
        dyb, dg = _rms_bwd(dycat[:, D_A:], yh, r, gb_ref[...])
        _acc_out(dgb_ref, dg)
        dyb_ref[...] = dyb.astype(BF16)
        prod = dyb * yb
        for c in range(D_B // 128):
            sl = slice(128 * c, 128 * (c + 1))
            d_lo, d_hi = head_sums(prod[:, sl])
            st_ref[:, sl] = jnp.where(is_lse, lse_ref[:, sl], jnp.where(low, d_lo, d_hi))

    sds = jax.ShapeDtypeStruct
    return pl.pallas_call(
        body, name="out_proj_bwd", grid=(s_len // tm,),
        in_specs=[_rows(tm, D_MODEL), _full((D_MODEL, D_MODEL)), _rows(tm, D_A), _rows(tm, D_B), _rows(tm, D_B),
                  _full((1, D_A)), _full((1, D_B))],
        out_specs=[_rows(tm, D_A), _rows(tm, D_B), _stat_spec(tm), _rows(tm, D_B), _full((1, D_A)),
                   _full((1, D_B))],
        out_shape=[sds((s_len, D_A), F32), sds((s_len, D_B), BF16), sds((N_HEADS_A, s_len, 1), F32),
                   sds((s_len, D_B), F32), sds((1, D_A), F32), sds((1, D_B), F32)],
        compiler_params=_cparams(("arbitrary",)),
    )(dy2, w_out, ya, yb, lse_b, g_a, g_b)


def _in_proj_bwd(dqr, dkv, dqb, dkb, dvb, qpre, kpre, x, dh1, g1, w_in, cc, ss, gq2, gk2, ones128):
    s_len = x.shape[0]
    tm = min(ROW_TILE // 2, s_len)

    def body(dqr_ref, dkv_ref, dq0, dq1, dq2, dk0, dk1, dk2, dv0, dv1, dv2, qpre_ref, kpre_ref, x_ref,
             dh1_ref, g_ref, w_ref, cc_ref, ss_ref, gq_ref, gk_ref, one_ref, dproj_ref, gx_ref, dg1_ref, dgq_ref,
             dgk_ref):
        low = _low_lanes(tm)
        dkr = jnp.where(low, dkv_ref[0], pltpu.roll(dkv_ref[1], HEAD_DIM, 1))
        dva = jnp.where(low, pltpu.roll(dkv_ref[0], HEAD_DIM, 1), dkv_ref[1])
        first_half = _first_half_mask((tm, 128))
        ones = one_ref[...]
        cc_t, ss_t = cc_ref[...], ss_ref[...]

        def norm_rope_bwd(dy, xc, gain):
            dn = dy * cc_t - _swap_halves(dy, first_half) * ss_t
            r = lax.rsqrt(_xdot(xc * xc, ones) * (1.0 / HEAD_DIM) + EPS)
            xh = xc * r
            gdy = dn * gain
            dx = r * (gdy - xh * (_xdot(gdy * xh, ones) * (1.0 / HEAD_DIM)))
            return dx, jnp.sum(dn * xh, axis=0, keepdims=True)

        dgq = jnp.zeros((1, 128), F32)
        parts = []
        for c in range(D_A // 128):
            sl = slice(128 * c, 128 * (c + 1))
            dx, dg = norm_rope_bwd(dqr_ref[:, sl] * Q_SCALE, qpre_ref[:, sl], gq_ref[...])
            parts.append(dx)
            dgq = dgq + dg
        dxk, dgk = norm_rope_bwd(dkr, kpre_ref[...], gk_ref[...])
        _acc_out(dgq_ref, dgq)
        _acc_out(dgk_ref, dgk)
        parts += [dxk, dva, (dq0[...] + dq1[...] + dq2[...]) * Q_SCALE, dk0[...] + dk1[...] + dk2[...],
                  dv0[...] + dv1[...] + dv2[...]]
        dproj = jnp.concatenate(parts, axis=-1).astype(BF16)
        dproj_ref[...] = dproj
        dxn = _mm_nt(dproj, w_ref[...])
        xh, r = _rms_stats(x_ref[...])
        dx, dg = _rms_bwd(dxn, xh, r, g_ref[...])
        _acc_out(dg1_ref, dg)
        gx_ref[...] = dh1_ref[...] + dx

    sds = jax.ShapeDtypeStruct
    return pl.pallas_call(
        body, name="in_proj_bwd", grid=(s_len // tm,),
        in_specs=[_rows(tm, D_A), pl.BlockSpec((N_KV_A, tm, 128), lambda i: (0, i, 0))] + [_rows(tm, D_B)] * 9
                 + [_rows(tm, D_A), _rows(tm, D_KV_A), _rows(tm, D_MODEL), _rows(tm, D_MODEL),
                    _full((1, D_MODEL)), _full((D_MODEL, D_IN)), _rows(tm, 128), _rows(tm, 128), _full((1, 128)),
                    _full((1, 128)), _full((128, 128))],
        out_specs=[_rows(tm, D_IN), _rows(tm, D_MODEL), _full((1, D_MODEL)), _full((1, 128)), _full((1, 128))],
        out_shape=[sds((s_len, D_IN), BF16), sds((s_len, D_MODEL), F32), sds((1, D_MODEL), F32),
                   sds((1, 128), F32), sds((1, 128), F32)],
        compiler_params=_cparams(("arbitrary",)),
    )(dqr, dkv, *dqb, *dkb, *dvb, qpre, kpre, x, dh1, g1, w_in, cc, ss, gq2, gk2, ones128)


def _dw(a, b, name, relu2=False):
    s_len, ka = a.shape
    n = b.shape[1]
    ts = min(DW_TS, s_len)
    bk = min(ka, 1024)
    bn = n if n % 1024 else 1024

    def body(a_ref, b_ref, o_ref):
        @pl.when(pl.program_id(2) == 0)
        def _():
            o_ref[...] = jnp.zeros_like(o_ref)

        av = a_ref[...]
        if relu2:
            av = jnp.square(jnp.maximum(av, 0.0))
        o_ref[...] += _mm_tn(av.astype(BF16), b_ref[...])

    return pl.pallas_call(
        body, name=name, grid=(ka // bk, n // bn, s_len // ts),
        in_specs=[pl.BlockSpec((ts, bk), lambda i, j, k: (k, i)), pl.BlockSpec((ts, bn), lambda i, j, k: (k, j))],
        out_specs=pl.BlockSpec((bk, bn), lambda i, j, k: (i, j)),
        out_shape=jax.ShapeDtypeStruct((ka, n), F32),
        compiler_params=_cparams(("parallel", "parallel", "arbitrary")),
    )(a, b)


def _stack_heads(block, low, data_low):
    parts = []
    for c in range(GROUP_A // 2):
        chunk = block[:, 128 * c:128 * (c + 1)]
        swapped = pltpu.roll(chunk, HEAD_DIM, 1)
        for h_low in (chunk, swapped) if data_low else (swapped, chunk):
            parts.append(jnp.where(low, h_low, 0.0) if data_low else jnp.where(low, 0.0, h_low))
    return jnp.concatenate(parts, axis=0).astype(BF16)


def _unstack_heads(stacked, low, tq, data_low):
    chunks = []
    for c in range(GROUP_A // 2):
        even = stacked[2 * c * tq:(2 * c + 1) * tq]
        odd = stacked[(2 * c + 1) * tq:(2 * c + 2) * tq]
        if data_low:
            chunks.append(jnp.where(low, even, pltpu.roll(odd, HEAD_DIM, 1)))
        else:
            chunks.append(jnp.where(low, pltpu.roll(even, HEAD_DIM, 1), odd))
    return chunks


def _attn_a_fwd(qa, kv):
    s_len = kv.shape[1]
    tq = min(ATT_TQ, s_len)
    tk = min(ATT_TK_FWD, s_len)
    rows = GROUP_A * tq

    def body(q_ref, kv_ref, o_ref, lse_ref):
        low = _low_lanes(tq)
        low_k = _low_lanes(tk)
        q = _stack_heads(q_ref[...].astype(F32), low, data_low=True)

        def block(j, m, acc):
            kvj = kv_ref[0, pl.ds(pl.multiple_of(j * tk, tk), tk), :]
            s = _mm_nt(q, kvj)
            m_new = jnp.maximum(m, jnp.max(s, axis=1, keepdims=True))
            p = jnp.exp(s - m_new).astype(BF16)
            return m_new, jnp.exp(m - m_new) * acc + _mm(p, jnp.where(low_k, jnp.ones_like(kvj), kvj))

        def step(j, carry):
            for u in range(unroll):
                carry = block(unroll * j + u, *carry)
            return carry

        unroll = math.gcd(s_len // tk, ATT_UNROLL_FWD)
        init = (jnp.full((rows, 1), -jnp.inf, F32), jnp.zeros((rows, 128), F32))
        m, acc = lax.fori_loop(0, s_len // (tk * unroll), step, init)
        for c, chunk in enumerate(_unstack_heads(acc / pltpu.roll(acc, HEAD_DIM, 1), low, tq, data_low=False)):
            o_ref[:, 128 * c:128 * (c + 1)] = chunk
        lse_ref[...] = (m + jnp.log(acc[:, :1])).reshape(GROUP_A, tq, 1)

    return pl.pallas_call(
        body, name="attn_a_fwd", grid=(N_KV_A, s_len // tq),
        in_specs=[pl.BlockSpec((tq, 256), lambda g, i: (i, g)),
                  pl.BlockSpec((1, s_len, 128), lambda g, i: (g, 0, 0))],
        out_specs=[pl.BlockSpec((tq, 256), lambda g, i: (i, g)),
                   pl.BlockSpec((GROUP_A, tq, 1), lambda g, i: (g, i, 0))],
        out_shape=[jax.ShapeDtypeStruct((s_len, D_A), F32),
                   jax.ShapeDtypeStruct((N_HEADS_A, s_len, 1), F32)],
        compiler_params=_cparams(("parallel", "parallel")),
    )(qa, kv)


def _attn_a_bwd(qa, dya, kv, lse, delta):
    s_len = kv.shape[1]
    tq = min(ATT_TQ_BWD, s_len)
    tk = min(ATT_TK_BWD, s_len)
    rows = GROUP_A * tq

    def body(q_ref, do_ref, kv_ref, lse_ref, dl_ref, dq_ref, dkv_ref):
        @pl.when(pl.program_id(1) == 0)
        def _():
            dkv_ref[...] = jnp.zeros_like(dkv_ref)

        low = _low_lanes(tq)
        q = _stack_heads(q_ref[...].astype(F32), low, data_low=True)
        do = _stack_heads(do_ref[...], low, data_low=False)
        lse_t = lse_ref[...].reshape(rows, 1)
        dl_t = dl_ref[...].reshape(rows, 1)
        q_t = q.T
        do_t = do.T

        def block(j, dq):
            span = pl.ds(pl.multiple_of(j * tk, tk), tk)
            kvj = kv_ref[0, span, :]
            p = jnp.exp(_mm_nt(q, kvj) - lse_t)
            ds = (p * (_mm_nt(do, kvj) - dl_t)).astype(BF16)
            dkv_ref[0, :, span] += _mm(q_t, ds) + _mm(do_t, p.astype(BF16))
            return dq + _mm(ds, kvj)

        def step(j, dq):
            for u in range(unroll):
                dq = block(unroll * j + u, dq)
            return dq

        unroll = math.gcd(s_len // tk, ATT_UNROLL_BWD)
        dq = lax.fori_loop(0, s_len // (tk * unroll), step, jnp.zeros((rows, 128), F32))
        for c, chunk in enumerate(_unstack_heads(dq, low, tq, data_low=True)):
            dq_ref[:, 128 * c:128 * (c + 1)] = chunk

    return pl.pallas_call(
        body, name="attn_a_bwd", grid=(N_KV_A, s_len // tq),
        in_specs=[pl.BlockSpec((tq, 256), lambda g, i: (i, g)),
                  pl.BlockSpec((tq, 256), lambda g, i: (i, g)),
                  pl.BlockSpec((1, s_len, 128), lambda g, i: (g, 0, 0)),
                  pl.BlockSpec((GROUP_A, tq, 1), lambda g, i: (g, i, 0)),
                  pl.BlockSpec((GROUP_A, tq, 1), lambda g, i: (g, i, 0))],
        out_specs=[pl.BlockSpec((tq, 256), lambda g, i: (i, g)),
                   pl.BlockSpec((1, 128, s_len), lambda g, i: (g, 0, 0))],
        out_shape=[jax.ShapeDtypeStruct((s_len, D_A), F32),
                   jax.ShapeDtypeStruct((N_KV_A, 128, s_len), F32)],
        compiler_params=_cparams(("parallel", "arbitrary")),
    )(qa, dya, kv, lse, delta)


class _SwaGeometry:
    def __init__(self, s_len, r):
        self.r = r
        self.tq = SWA_TQ
        self.block = min(max(SWA_MIN_BLOCK, 2 * SWA_TQ * r), s_len)
        self.halo = HALF_WIN * r
        self.nsub = self.block // (self.tq * r)
        self.band = self.tq + 2 * HALF_WIN
        self.length = s_len // r
        self.nblk = s_len // self.block
        self.nhalo = s_len // self.halo
        assert self.nsub * self.tq * r == self.block and self.block % self.halo == 0

    def specs(self):
        per = self.block // self.halo
        cur = pl.BlockSpec((self.block, 128), lambda c, i: (i, c))
        prev = pl.BlockSpec((self.halo, 128), lambda c, i: (jnp.maximum(i * per - 1, 0), c))
        nxt = pl.BlockSpec((self.halo, 128), lambda c, i: (jnp.minimum((i + 1) * per, self.nhalo - 1), c))
        return prev, cur, nxt

    def tiles(self):
        return [(rho + self.r * j * self.tq, self.halo + rho + self.r * (j * self.tq - HALF_WIN), j)
                for j in range(self.nsub) for rho in range(self.r)]

    def own(self, start):
        return pl.ds(start, self.tq, stride=self.r)

    def around(self, start):
        return pl.ds(start, self.band, stride=self.r)

    def fill(self, dst, prev_ref, cur_ref, next_ref):
        dst[:self.halo, :] = prev_ref[...].astype(F32)
        dst[self.halo:self.halo + self.block, :] = cur_ref[...].astype(F32)
        dst[self.halo + self.block:, :] = next_ref[...].astype(F32)

    def first_position(self, j):
        return (pl.program_id(1) * self.block) // self.r + j * self.tq

    def outside(self, j, copies=1):
        pos = self.first_position(j) - HALF_WIN + lax.broadcasted_iota(jnp.int32, (1, copies * self.band), 1) % self.band
        return jnp.where((pos >= 0) & (pos < self.length), 0.0, NEG_BIG)

    def extended(self):
        return pltpu.VMEM((self.block + 2 * self.halo, 128), F32)

    def plain(self):
        return pltpu.VMEM((self.block, 128), F32)


def _low_lanes(rows):
    return lax.broadcasted_iota(jnp.int32, (rows, 128), 1) < HEAD_DIM


def _one_head(x, low, half):
    return jnp.where(low if half == 0 else jnp.logical_not(low), x, 0.0).astype(BF16)


def _two_heads(x, low):
    return jnp.concatenate([_one_head(x, low, 0), _one_head(x, low, 1)], axis=0)


def _carry_ride(base_body, n_in, n_out, n_scratch, ride, grid):
    if ride is None:
        return base_body
    n = ride.n

    def body(*refs):
        o = n_in + n
        ins, ride_ins = refs[:n_in], refs[n_in:o]
        outs, ride_outs = refs[o:o + n_out], refs[o + n_out:o + n_out + n]
        o += n_out + n
        scratch, sems = refs[o:o + n_scratch], refs[o + n_scratch:]
        at_first = (pl.program_id(0) == 0) & (pl.program_id(1) == 0)
        at_last = (pl.program_id(0) == grid[0] - 1) & (pl.program_id(1) == grid[1] - 1)

        @pl.when(at_first)
        def _():
            ride.start(ride_ins, ride_outs, sems)

        base_body(*ins, *outs, *scratch)

        @pl.when(at_last)
        def _():
            ride.finish(ride_ins, ride_outs, sems)

    return body


def _ride_call(base_body, name, grid, in_specs, out_specs, out_shape, scratch_shapes, operands, ride):
    n = 0 if ride is None else ride.n
    extra = [] if ride is None else ride.operands
    outs = pl.pallas_call(
        _carry_ride(base_body, len(in_specs), len(out_specs), len(scratch_shapes), ride, grid), name=name, grid=grid,
        in_specs=list(in_specs) + [ANY] * n, out_specs=list(out_specs) + [ANY] * n,
        out_shape=list(out_shape) + ([] if ride is None else ride.out_shape()),
        scratch_shapes=list(scratch_shapes) + ([] if ride is None else ride.scratch_shapes()),
        compiler_params=_cparams(("arbitrary", "arbitrary")),
    )(*operands, *extra)
    return outs[:len(out_specs)], outs[len(out_specs):]


def _swa_fwd(q, k, v, bias, r, ride=None):
    geo = _SwaGeometry(q.shape[0], r)
    prev, cur, nxt = geo.specs()

    def body(q_ref, kp, kc, kn, vp, vc, vn, b_ref, o_ref, lse_ref, qf, kf, vf):
        qf[...] = q_ref[...].astype(F32)
        geo.fill(kf, kp, kc, kn)
        geo.fill(vf, vp, vc, vn)
        tq = geo.tq
        low_q = _low_lanes(tq)
        bias = b_ref[...].reshape(2 * tq, geo.band)
        for own, around, j in geo.tiles():
            q2 = _two_heads(qf[geo.own(own), :], low_q)
            kb = kf[geo.around(around), :].astype(BF16)
            vb = vf[geo.around(around), :].astype(BF16)
            s = _mm_nt(q2, kb) + bias + geo.outside(j)
            m = jnp.max(s, axis=1, keepdims=True)
            e = jnp.exp(s - m)
            l = jnp.sum(e, axis=1, keepdims=True)
            o2 = _mm(e.astype(BF16), vb) / l
            lse2 = m + jnp.log(l)
            o_ref[geo.own(own), :] = jnp.where(low_q, o2[:tq], o2[tq:])
            lse_ref[geo.own(own), :] = jnp.where(low_q, lse2[:tq], lse2[tq:])

    sds = jax.ShapeDtypeStruct
    (o, lse), carried = _ride_call(
        body, "swa_fwd_%d" % r, (D_B // 128, geo.nblk),
        [cur, prev, cur, nxt, prev, cur, nxt, pl.BlockSpec((2, geo.tq, geo.band), lambda c, i: (c, 0, 0))],
        [cur, cur], [sds(q.shape, F32), sds(q.shape, F32)], [geo.plain(), geo.extended(), geo.extended()],
        (q, k, k, k, v, v, v, bias), ride)
    return o, lse, carried


def _head_stats(st, half):
    lo = HEAD_DIM * half
    return st[:, lo:lo + 1], st[:, lo + HEAD_DIM // 2:lo + HEAD_DIM // 2 + 1]


def _swa_bwd_q(q, k, v, dy, st, bias, r, ride=None):
    geo = _SwaGeometry(q.shape[0], r)
    prev, cur, nxt = geo.specs()
    bias_spec = pl.BlockSpec((2, geo.tq, geo.band), lambda c, i: (c, 0, 0))

    def body(q_ref, kp, kc, kn, vp, vc, vn, dy_ref, st_ref, b_ref, dq_ref, db_ref, qf, kf, vf, dyf):
        @pl.when(pl.program_id(1) == 0)
        def _():
            db_ref[...] = jnp.zeros_like(db_ref)

        qf[...] = q_ref[...].astype(F32)
        dyf[...] = dy_ref[...].astype(F32)
        geo.fill(kf, kp, kc, kn)
        geo.fill(vf, vp, vc, vn)
        tq = geo.tq
        low_q = _low_lanes(tq)
        bias = b_ref[...].reshape(2 * tq, geo.band)
        for own, around, j in geo.tiles():
            sts = st_ref[geo.own(own), :]
            (lse0, delta0), (lse1, delta1) = _head_stats(sts, 0), _head_stats(sts, 1)
            lse = jnp.concatenate([lse0, lse1], axis=0)
            delta = jnp.concatenate([delta0, delta1], axis=0)
            kb = kf[geo.around(around), :].astype(BF16)
            vb = vf[geo.around(around), :].astype(BF16)
            s = _mm_nt(_two_heads(qf[geo.own(own), :], low_q), kb) + bias + geo.outside(j)
            p = jnp.exp(s - lse)
            ds = p * (_mm_nt(_two_heads(dyf[geo.own(own), :], low_q), vb) - delta)
            db_ref[...] += ds.reshape(2, tq, geo.band)
            dq2 = _mm(ds.astype(BF16), kb)
            dq_ref[geo.own(own), :] = jnp.where(low_q, dq2[:tq], dq2[tq:])

    (dq, dbias), carried = _ride_call(
        body, "swa_bwd_q_%d" % r, (D_B // 128, geo.nblk),
        [cur, prev, cur, nxt, prev, cur, nxt, cur, cur, bias_spec], [cur, bias_spec],
        [jax.ShapeDtypeStruct(q.shape, F32), jax.ShapeDtypeStruct(bias.shape, F32)],
        [geo.plain(), geo.extended(), geo.extended(), geo.plain()], (q, k, k, k, v, v, v, dy, st, bias), ride)
    return dq, dbias, carried


def _swa_bwd_kv(q, k, v, dy, st, bias_kv, r):
    geo = _SwaGeometry(q.shape[0], r)
    prev, cur, nxt = geo.specs()

    def body(k_ref, v_ref, qp, qc, qn, dp_, dc_, dn_, sp, sc, sn, b_ref, dk_ref, dv_ref, kf, vf, qf, dyf, stf):
        kf[...] = k_ref[...].astype(F32)
        vf[...] = v_ref[...].astype(F32)
        geo.fill(qf, qp, qc, qn)
        geo.fill(dyf, dp_, dc_, dn_)
        geo.fill(stf, sp, sc, sn)
        band = geo.band
        low_b = _low_lanes(band)
        bias = jnp.concatenate([b_ref[0], b_ref[1]], axis=1)
        half_lanes = HEAD_DIM // 2
        for own, around, j in geo.tiles():
            ks = kf[geo.own(own), :].astype(BF16)
            vs = vf[geo.own(own), :].astype(BF16)
            q2 = _two_heads(qf[geo.around(around), :], low_b)
            dy2 = _two_heads(dyf[geo.around(around), :], low_b)
            st_t = stf[geo.around(around), :].T
            lse = jnp.concatenate([st_t[:1, :], st_t[HEAD_DIM:HEAD_DIM + 1, :]], axis=1)
            delta = jnp.concatenate([st_t[half_lanes:half_lanes + 1, :],
                                     st_t[HEAD_DIM + half_lanes:HEAD_DIM + half_lanes + 1, :]], axis=1)
            p = jnp.exp(_mm_nt(ks, q2) + bias + (geo.outside(j, copies=2) - lse))
            ds = p * (_mm_nt(vs, dy2) - delta)
            dv_ref[geo.own(own), :] = _mm(p.astype(BF16), dy2)
            dk_ref[geo.own(own), :] = _mm(ds.astype(BF16), q2)

    return pl.pallas_call(
        body, name="swa_bwd_kv_%d" % r, grid=(D_B // 128, geo.nblk),
        in_specs=[cur, cur, prev, cur, nxt, prev, cur, nxt, prev, cur, nxt,
                  pl.BlockSpec((2, geo.tq, geo.band), lambda c, i: (c, 0, 0))],
        out_specs=[cur, cur],
        out_shape=[jax.ShapeDtypeStruct(q.shape, F32), jax.ShapeDtypeStruct(q.shape, F32)],
        scratch_shapes=[geo.plain(), geo.plain(), geo.extended(), geo.extended(), geo.extended()],
        compiler_params=_cparams(("parallel", "parallel")),
    )(k, v, q, q, q, dy, dy, dy, st, st, st, bias_kv)


BIAS_ROWS = 16
BIAS_TN = 4096


def _bias_tiles(onehot, rel_bias_t):
    n = onehot.shape[0]

    def body(oh_ref, rb_ref, o_ref):
        o_ref[...] = sum(_mm_nt(piece, oh_ref[...]) for piece in _split3(rb_ref[...]))

    return pl.pallas_call(
        body, name="bias_tiles", grid=(n // BIAS_TN,),
        in_specs=[_rows(BIAS_TN, 128), _full((BIAS_ROWS, 128))],
        out_specs=pl.BlockSpec((BIAS_ROWS, BIAS_TN), lambda i: (0, i)),
        out_shape=jax.ShapeDtypeStruct((BIAS_ROWS, n), F32),
        compiler_params=_cparams(("parallel",)),
    )(onehot, rel_bias_t)


def _bias_bwd(onehot, dbias_rows, so_far, r):
    n = onehot.shape[0]

    def body(oh, d, prev_ref, g_ref):
        @pl.when(pl.program_id(0) == 0)
        def _():
            g_ref[...] = prev_ref[...]

        hi, lo, _ = _split3(d[...])
        g_ref[...] += _mm(hi, oh[...]) + _mm(lo, oh[...])

    return pl.pallas_call(
        body, name="bias_bwd_%d" % r, grid=(n // BIAS_TN,),
        in_specs=[_rows(BIAS_TN, 128), pl.BlockSpec((BIAS_ROWS, BIAS_TN), lambda i: (0, i)), _full((BIAS_ROWS, 128))],
        out_specs=_full((BIAS_ROWS, 128)),
        out_shape=jax.ShapeDtypeStruct((BIAS_ROWS, 128), F32),
        compiler_params=_cparams(("arbitrary",)),
    )(onehot, dbias_rows, so_far)


LATE = ("w_out", "w_ff1", "w_ff2", "w_ple_gate", "w_ple_proj")


def _local_step(x, p, tgt, w_in, late_shards, g_attn_pre, g_q, g_k, g_out_a, g_out_b, g_attn_post, rel_bias,
                g_mlp_pre, g_mlp_post, g_ple):
    s_len = x.shape[0]
    cc, ss = _rope_tables(s_len)
    gq2 = jnp.concatenate([g_q, g_q], axis=-1)
    gk2 = jnp.concatenate([g_k, g_k], axis=-1)
    ones128 = _group_ones(128)
    rel_bias_t = jnp.zeros((BIAS_ROWS, 128), F32).at[:N_HEADS_B, :N_BUCKETS].set(rel_bias.T)

    xn1, qpre, kpre, qa, kv, qb, kb, vb = _in_proj(x, g_attn_pre, w_in, cc, ss, gq2, gk2, ones128)
    ya, lse_a = _attn_a_fwd(qa, kv)

    tiles, outs, lses = [], [], []
    for r in DILATIONS:
        tq = SWA_TQ
        onehot = _bucket_onehot(tq, r)
        bias = _bias_tiles(onehot, rel_bias_t)[:N_HEADS_B].reshape(N_HEADS_B, tq, tq + 2 * HALF_WIN)
        bias = jnp.where(_in_window(tq), bias, NEG_BIG)
        o_r, lse_r, gathered = _swa_fwd(qb, kb, vb, bias, r, _GatherRide(late_shards) if r == DILATIONS[-1] else None)
        tiles.append((onehot, bias))
        outs.append(o_r)
        lses.append(lse_r)
    yb, lse_b = _merge_b(outs, lses)
    w_out, w_ff1, w_ff2, w_gate, w_ple = (_whole(n, g, mine) for n, g, mine in zip(LATE, gathered, late_shards))

    ycat, y2, h1, xn2 = _out_proj(ya, yb, x, g_out_a, g_out_b, w_out, g_attn_post, g_mlp_pre)
    u = _ff1(xn2, w_ff1)
    f2, h2, xn3 = _ff2(u, w_ff2, h1, g_mlp_post, g_ple)
    dh2, df2, dgl, dpp, loss, dg_ple, dg_mlp_post = _ple_loss(xn3, p, h2, f2, tgt, w_gate, w_ple, g_ple, g_mlp_post)

    grads = {"g_ple": dg_ple, "g_mlp_post": dg_mlp_post}
    grads["w_ple_gate"] = _dw(xn3, dgl, "dw_gate")
    grads["w_ple_proj"] = _dw(p, dpp, "dw_ple")
    grads["w_ff2"] = _dw(u, df2, "dw_ff2", relu2=True)
    du = _ff2_bwd(df2, w_ff2, u)
    grads["w_ff1"] = _dw(xn2, du, "dw_ff1")
    dh1, dy2, grads["g_mlp_pre"], grads["g_attn_post"] = _ff1_bwd(du, w_ff1, dh2, h1, y2, g_mlp_pre, g_attn_post)
    grads["w_out"] = _dw(ycat, dy2, "dw_out")
    dya, dyb, delta_a, st_b, grads["g_out_a"], grads["g_out_b"] = _out_proj_bwd(dy2, w_out, ya, yb, lse_b, g_out_a,
                                                                              g_out_b)

    pairs = _pair_sums(LATE, [grads[n] for n in LATE])

    dqr, dkv_t = _attn_a_bwd(qa, dya, kv, lse_a, delta_a)
    dkv_a = dkv_t.transpose(0, 2, 1)

    dqs, dks, dvs = [], [], []
    d_rel = jnp.zeros((BIAS_ROWS, 128), F32)
    for r, (onehot, bias) in zip(DILATIONS, tiles):
        dq_r, dbias, scattered = _swa_bwd_q(qb, kb, vb, dyb, st_b, bias, r,
                                            _ScatterRide(pairs) if r == DILATIONS[0] else None)
        if scattered:
            for n, half in zip(LATE, _chip_sums(LATE, pairs, scattered)):
                grads[n] = half
        bias_kv = jnp.flip(bias, axis=(1, 2))
        dk_r, dv_r = _swa_bwd_kv(qb, kb, vb, dyb, st_b, bias_kv, r)
        dbias_rows = jnp.pad(dbias.reshape(N_HEADS_B, -1), ((0, BIAS_ROWS - N_HEADS_B), (0, 0)))
        d_rel = _bias_bwd(onehot, dbias_rows, d_rel, r)
        dqs.append(dq_r)
        dks.append(dk_r)
        dvs.append(dv_r)
    grads["rel_bias"] = d_rel[:N_HEADS_B, :N_BUCKETS].T

    dproj, grad_x, grads["g_attn_pre"], dgq2, dgk2 = _in_proj_bwd(
        dqr, dkv_a, dqs, dks, dvs, qpre, kpre, x, dh1, g_attn_pre, w_in, cc, ss, gq2, gk2, ones128)
    grads["g_q"] = dgq2[:, :HEAD_DIM] + dgq2[:, HEAD_DIM:]
    grads["g_k"] = dgk2[:, :HEAD_DIM] + dgk2[:, HEAD_DIM:]
    grads["w_in"] = _dw(xn1, dproj, "dw_in")
    return loss, grad_x, grads


ANY = pl.BlockSpec(memory_space=pl.ANY)


def _position():
    return lax.axis_index("x"), lax.axis_index("y"), lax.axis_index("c")


def _other_chips(x, y):
    return [(2 * (1 - x) + y, (1 - x, y)), (2 * x + (1 - y), (x, 1 - y)), (2 * (1 - x) + (1 - y), (1 - x, 1 - y))]


def _cast_shards(shards):
    def body(*refs):
        n = len(refs) // 2
        for i_ref, o_ref in zip(refs[:n], refs[n:]):
            o_ref[...] = i_ref[...].astype(BF16)

    return pl.pallas_call(
        body, name="cast_shards",
        in_specs=[pl.BlockSpec(memory_space=pltpu.VMEM)] * len(shards),
        out_specs=[pl.BlockSpec(memory_space=pltpu.VMEM)] * len(shards),
        out_shape=[jax.ShapeDtypeStruct(s.shape, BF16) for s in shards],
        compiler_params=_cparams(),
    )(*shards)


def _gather_weights(shards):
    n = len(shards)

    ride = _GatherRide(shards)

    def body(*refs):
        ride.start(refs[:n], refs[n:2 * n], refs[2 * n:])
        ride.finish(refs[:n], refs[n:2 * n], refs[2 * n:])

    return pl.pallas_call(
        body, name="gather_weights",
        in_specs=[ANY] * n, out_specs=[ANY] * n,
        out_shape=ride.out_shape(), scratch_shapes=ride.scratch_shapes(),
    )(*shards)


class _GatherRide:
    def __init__(self, shards):
        self.operands = list(shards)
        self.n = len(shards)

    def out_shape(self):
        return [jax.ShapeDtypeStruct((N_CHIPS,) + s.shape, s.dtype) for s in self.operands]

    def scratch_shapes(self):
        return [pltpu.SemaphoreType.DMA((3, self.n))] * 4

    @staticmethod
    def _rows(ref, core):
        half = ref.shape[0] // 2
        return pl.ds(pl.multiple_of(core * half, 16), half)

    def _ici(self, ins, outs, sems, k, a, chip):
        x, y, c = _position()
        return pltpu.make_async_remote_copy(ins[a].at[self._rows(ins[a], c), :],
                                            outs[a].at[2 * x + y, self._rows(ins[a], c), :], sems[0].at[k, a],
                                            sems[1].at[k, a], device_id=(*chip, c), device_id_type=MESH)

    def _pass_on(self, ins, outs, sems, k, a, num, core):
        x, y, c = _position()
        half = outs[a].at[num, self._rows(ins[a], core), :]
        return pltpu.make_async_remote_copy(half, half, sems[2].at[k, a], sems[3].at[k, a], device_id=(x, y, 1 - c),
                                            device_id_type=MESH)

    def start(self, ins, outs, sems):
        x, y, _ = _position()
        for k, (_, chip) in enumerate(_other_chips(x, y)):
            for a in range(self.n):
                self._ici(ins, outs, sems, k, a, chip).start()

    def finish(self, ins, outs, sems):
        x, y, c = _position()
        others = _other_chips(x, y)
        for k, (num, chip) in enumerate(others):
            for a in range(self.n):
                landed = outs[a].at[num, self._rows(ins[a], c), :]
                pltpu.make_async_remote_copy(landed, landed, sems[0].at[k, a], sems[1].at[k, a], device_id=(*chip, c),
                                             device_id_type=MESH).wait_recv()
                self._pass_on(ins, outs, sems, k, a, num, c).start()
        for k, (num, chip) in enumerate(others):
            for a in range(self.n):
                self._pass_on(ins, outs, sems, k, a, num, 1 - c).wait_recv()
        for k, (num, chip) in enumerate(others):
            for a in range(self.n):
                self._ici(ins, outs, sems, k, a, chip).wait_send()
                self._pass_on(ins, outs, sems, k, a, num, c).wait_send()


def _send_sibling_half(grads, tag):
    n = len(grads)

    def body(*refs):
        ins, outs = refs[:n], refs[n:2 * n]
        send_sems, recv_sems = refs[2 * n:]
        x, y, c = _position()
        copies = []
        for a in range(n):
            half = ins[a].shape[1] // 2
            theirs = ins[a].at[:, pl.ds(pl.multiple_of((1 - c) * half, 8), half), :]
            cp = pltpu.make_async_remote_copy(theirs, outs[a], send_sems.at[a], recv_sems.at[a],
                                              device_id=(x, y, 1 - c), device_id_type=MESH)
            cp.start()
            copies.append(cp)
        for cp in copies:
            cp.wait()

    return pl.pallas_call(
        body, name="send_sibling_half_" + tag,
        in_specs=[ANY] * n, out_specs=[ANY] * n,
        out_shape=[jax.ShapeDtypeStruct((g.shape[0], g.shape[1] // 2, g.shape[2]), g.dtype) for g in grads],
        scratch_shapes=[pltpu.SemaphoreType.DMA((n,)), pltpu.SemaphoreType.DMA((n,))],
    )(*grads)


def _scatter_to_chips(pairs):
    n = len(pairs)
    ride = _ScatterRide(pairs)

    def body(*refs):
        ride.start(refs[:n], refs[n:2 * n], refs[2 * n:])
        ride.finish(refs[:n], refs[n:2 * n], refs[2 * n:])

    return pl.pallas_call(
        body, name="scatter_to_chips",
        in_specs=[ANY] * n, out_specs=[ANY] * n,
        out_shape=ride.out_shape(), scratch_shapes=ride.scratch_shapes(),
    )(*pairs)


class _ScatterRide:
    def __init__(self, pairs):
        self.operands = list(pairs)
        self.n = len(pairs)

    def out_shape(self):
        return [jax.ShapeDtypeStruct(g.shape, g.dtype) for g in self.operands]

    def scratch_shapes(self):
        return [pltpu.SemaphoreType.DMA((3, self.n))] * 2

    @staticmethod
    def _copy(ins, outs, sems, k, a, src_slot, dst_slot, chip):
        _, _, c = _position()
        return pltpu.make_async_remote_copy(ins[a].at[src_slot], outs[a].at[dst_slot], sems[0].at[k, a],
                                            sems[1].at[k, a], device_id=(*chip, c), device_id_type=MESH)

    def start(self, ins, outs, sems):
        x, y, _ = _position()
        for k, (num, chip) in enumerate(_other_chips(x, y)):
            for a in range(self.n):
                self._copy(ins, outs, sems, k, a, num, 2 * x + y, chip).start()

    def finish(self, ins, outs, sems):
        x, y, _ = _position()
        for k, (num, chip) in enumerate(_other_chips(x, y)):
            for a in range(self.n):
                self._copy(ins, outs, sems, k, a, 2 * x + y, num, chip).wait_recv()
        for k, (num, chip) in enumerate(_other_chips(x, y)):
            for a in range(self.n):
                self._copy(ins, outs, sems, k, a, num, 2 * x + y, chip).wait_send()


def _exchange_halves(halves):
    n = len(halves)

    def body(*refs):
        ins, outs = refs[:n], refs[n:2 * n]
        send_sems, recv_sems = refs[2 * n:]
        x, y, c = _position()
        copies = []
        for a in range(n):
            cp = pltpu.make_async_remote_copy(ins[a], outs[a], send_sems.at[a], recv_sems.at[a],
                                              device_id=(x, y, 1 - c), device_id_type=MESH)
            cp.start()
            copies.append(cp)
        for cp in copies:
            cp.wait()

    return pl.pallas_call(
        body, name="exchange_halves",
        in_specs=[ANY] * n, out_specs=[ANY] * n,
        out_shape=[jax.ShapeDtypeStruct(h.shape, h.dtype) for h in halves],
        scratch_shapes=[pltpu.SemaphoreType.DMA((n,)), pltpu.SemaphoreType.DMA((n,))],
    )(*halves)


def _allreduce_small(v):
    def body(v_ref, o_ref, buf, send_sems, recv_sems):
        x, y, c = _position()
        me = 4 * x + 2 * y + c
        peers = [(1 - x, y, c), (x, 1 - y, c), (x, y, 1 - c), (1 - x, 1 - y, c), (1 - x, y, 1 - c), (x, 1 - y, 1 - c),
                 (1 - x, 1 - y, 1 - c)]
        num = lambda d: 4 * d[0] + 2 * d[1] + d[2]
        buf[me] = v_ref[...]
        sends = []
        for k, peer in enumerate(peers):
            cp = pltpu.make_async_remote_copy(v_ref, buf.at[me], send_sems.at[k], recv_sems.at[k], device_id=peer,
                                              device_id_type=MESH)
            cp.start()
            sends.append(cp)
        for k, peer in enumerate(peers):
            pltpu.make_async_remote_copy(v_ref, buf.at[num(peer)], send_sems.at[k], recv_sems.at[k], device_id=peer,
                                         device_id_type=MESH).wait_recv()
        for cp in sends:
            cp.wait_send()
        total = buf[0]
        for d in range(1, 8):
            total = total + buf[d]
        o_ref[...] = total

    return pl.pallas_call(
        body, name="allreduce_small",
        in_specs=[pl.BlockSpec(memory_space=pltpu.VMEM)], out_specs=pl.BlockSpec(memory_space=pltpu.VMEM),
        out_shape=jax.ShapeDtypeStruct(v.shape, v.dtype),
        scratch_shapes=[pltpu.VMEM((8,) + v.shape, v.dtype), pltpu.SemaphoreType.DMA((7,)),
                        pltpu.SemaphoreType.DMA((7,))],
    )(v)


def _sum_leading(a, name):
    k, r, c = a.shape
    tr = min(r, 256)

    def body(a_ref, o_ref):
        total = a_ref[0].astype(F32)
        for i in range(1, k):
            total = total + a_ref[i].astype(F32)
        o_ref[...] = total

    return pl.pallas_call(
        body, name=name, grid=(r // tr,),
        in_specs=[pl.BlockSpec((k, tr, c), lambda i: (0, i, 0))],
        out_specs=pl.BlockSpec((tr, c), lambda i: (i, 0)),
        out_shape=jax.ShapeDtypeStruct((r, c), F32),
        compiler_params=_cparams(("parallel",)),
    )(a)


def _add(a, b, name):
    k, r, c = a.shape
    tr = min(r, 256)
    spec = pl.BlockSpec((k, tr, c), lambda i: (0, i, 0))

    def body(a_ref, b_ref, o_ref):
        o_ref[...] = (a_ref[...] + b_ref[...]).astype(BF16)

    return pl.pallas_call(
        body, name=name, grid=(r // tr,), in_specs=[spec, spec], out_specs=spec,
        out_shape=jax.ShapeDtypeStruct(a.shape, BF16), compiler_params=_cparams(("parallel",)),
    )(a, b)


def _adamw(w, g, m, v, name):
    r, c = w.shape
    tr = min(r, 256)
    spec = pl.BlockSpec((tr, c), lambda i: (i, 0))

    def body(w_ref, g_ref, m_ref, v_ref, d_ref, nm_ref, nv_ref):
        gv = g_ref[...]
        nm = ADAM_B1 * m_ref[...] + (1.0 - ADAM_B1) * gv
        nv = ADAM_B2 * v_ref[...] + (1.0 - ADAM_B2) * jnp.square(gv)
        m_hat = nm / (1.0 - ADAM_B1 ** ADAM_STEP)
        v_hat = nv / (1.0 - ADAM_B2 ** ADAM_STEP)
        d_ref[...] = -ADAM_LR * (m_hat / (jnp.sqrt(v_hat) + ADAM_EPS) + ADAM_WD * w_ref[...])
        nm_ref[...] = nm
        nv_ref[...] = nv

    return pl.pallas_call(
        body, name=name, grid=(r // tr,), in_specs=[spec] * 4, out_specs=[spec] * 3,
        out_shape=[jax.ShapeDtypeStruct(w.shape, F32)] * 3, compiler_params=_cparams(("parallel",)),
    )(w, g, m, v)


MATRICES = ("w_in", "w_out", "w_ff1", "w_ff2", "w_ple_gate", "w_ple_proj")
COLUMN_SHARDED = ("w_in", "w_ff1", "w_ple_proj")
SMALL = ("g_attn_pre", "g_q", "g_k", "g_out_a", "g_out_b", "g_attn_post", "rel_bias", "g_mlp_pre", "g_mlp_post",
         "g_ple")
WEIGHT_ORDER = ("w_in", "g_attn_pre", "g_q", "g_k", "g_out_a", "g_out_b", "w_out", "g_attn_post", "rel_bias",
                "g_mlp_pre", "w_ff1", "w_ff2", "g_mlp_post", "g_ple", "w_ple_gate", "w_ple_proj")
PACK_ROWS, PACK_COLS = 8, 1024


def _chip():
    return 2 * lax.axis_index("x") + lax.axis_index("y")


def _whole(name, gathered, mine):
    g = lax.dynamic_update_slice_in_dim(gathered, mine[None], _chip(), axis=0)
    if name in COLUMN_SHARDED:
        return g.transpose(1, 0, 2).reshape(g.shape[1], N_CHIPS * g.shape[2])
    return g.reshape(N_CHIPS * g.shape[1], g.shape[2])


def _pair_sums(names, grads):
    by_chip = []
    for n, g in zip(names, grads):
        if n in COLUMN_SHARDED:
            by_chip.append(g.reshape(g.shape[0], N_CHIPS, g.shape[1] // N_CHIPS).transpose(1, 0, 2))
        else:
            by_chip.append(g.reshape(N_CHIPS, g.shape[0] // N_CHIPS, g.shape[1]))
    c = lax.axis_index("c")
    pairs = []
    for n, g, other in zip(names, by_chip, _send_sibling_half(by_chip, names[0])):
        half = g.shape[1] // 2
        pairs.append(_add(lax.dynamic_slice_in_dim(g, c * half, half, axis=1), other, "pair_sum_" + n))
    return pairs


def _chip_sums(names, pairs, scattered):
    halves = []
    for n, pair, got in zip(names, pairs, scattered):
        own = lax.dynamic_slice_in_dim(pair, _chip(), 1, axis=0)
        halves.append(_sum_leading(lax.dynamic_update_slice_in_dim(got, own, _chip(), axis=0), "chip_sum_" + n))
    return halves


def _pack_small(values, extra=None):
    flat = [values[n].reshape(-1) for n in SMALL]
    used = sum(f.shape[0] for f in flat)
    tail = jnp.zeros((PACK_ROWS * PACK_COLS - used - 1,), F32)
    last = jnp.zeros((1,), F32) if extra is None else extra.reshape(1)
    return jnp.concatenate(flat + [tail, last]).reshape(PACK_ROWS, PACK_COLS)


def _unpack_small(packed, like):
    flat = packed.reshape(-1)
    out, o = {}, 0
    for n in SMALL:
        size = like[n].size
        out[n] = flat[o:o + size].reshape(like[n].shape)
        o += size
    return out, flat[-1]


def kernel(x, p, w_in, g_attn_pre, g_q, g_k, g_out_a, g_out_b, w_out, g_attn_post, rel_bias, g_mlp_pre, w_ff1, w_ff2, g_mlp_post, g_ple, w_ple_gate, w_ple_proj, loss_target, m_w_in, m_g_attn_pre, m_g_q, m_g_k, m_g_out_a, m_g_out_b, m_w_out, m_g_attn_post, m_rel_bias, m_g_mlp_pre, m_w_ff1, m_w_ff2, m_g_mlp_post, m_g_ple, m_w_ple_gate, m_w_ple_proj, v_w_in, v_g_attn_pre, v_g_q, v_g_k, v_g_out_a, v_g_out_b, v_w_out, v_g_attn_post, v_rel_bias, v_g_mlp_pre, v_w_ff1, v_w_ff2, v_g_mlp_post, v_g_ple, v_w_ple_gate, v_w_ple_proj):
    given = dict(locals())
    weights = {n: given[n] for n in WEIGHT_ORDER}
    shards = {n: weights[n][0] for n in MATRICES}

    c = lax.axis_index("c")
    own = dict(zip(MATRICES, _cast_shards([shards[n] for n in MATRICES])))
    w_in_whole = _whole("w_in", _gather_weights([own["w_in"]])[0], own["w_in"])

    loss, grad_x, grads = _local_step(
        x[0], p[0, 0], loss_target[0], w_in_whole, [own[n] for n in LATE], g_attn_pre, g_q, g_k, g_out_a, g_out_b,
        g_attn_post, rel_bias, g_mlp_pre, g_mlp_post, g_ple)

    pairs = _pair_sums(["w_in"], [grads["w_in"]])
    grads["w_in"] = _chip_sums(["w_in"], pairs, _scatter_to_chips(pairs))[0]
    halves = [grads[n] for n in MATRICES]
    grad_w = {}
    for n, mine, theirs in zip(MATRICES, halves, _exchange_halves(halves)):
        half = mine.shape[0]
        g = jnp.zeros((2 * half, mine.shape[1]), F32)
        g = lax.dynamic_update_slice_in_dim(g, mine, c * half, axis=0)
        grad_w[n] = lax.dynamic_update_slice_in_dim(g, theirs, (1 - c) * half, axis=0)

    small_like = {n: weights[n] for n in SMALL}
    reduced = _allreduce_small(_pack_small({n: grads[n] for n in SMALL}, extra=loss))
    grad_small, loss_total = _unpack_small(reduced, small_like)

    delta, new_m, new_v = {}, {}, {}
    for n in MATRICES:
        d, nm, nv = _adamw(shards[n], grad_w[n], given["m_" + n][0], given["v_" + n][0], "adamw_" + n)
        delta[n], new_m[n], new_v[n] = d[None], nm[None], nv[None]
        grad_w[n] = grad_w[n][None]
    d, nm, nv = _adamw(_pack_small(small_like), reduced, _pack_small({n: given["m_" + n] for n in SMALL}),
                       _pack_small({n: given["v_" + n] for n in SMALL}), "adamw_small")
    d_small, nm_small, nv_small = (_unpack_small(a, small_like)[0] for a in (d, nm, nv))
    for n in SMALL:
        grad_w[n], delta[n], new_m[n], new_v[n] = grad_small[n], d_small[n], nm_small[n], nv_small[n]

    return (loss_total, grad_x[None], *[grad_w[n] for n in WEIGHT_ORDER], *[delta[n] for n in WEIGHT_ORDER],
            *[new_m[n] for n in WEIGHT_ORDER], *[new_v[n] for n in WEIGHT_ORDER])
```

```python
import functools
import math

import jax
import jax.numpy as jnp
from jax import lax
from jax.experimental import pallas as pl
from jax.experimental.pallas import tpu as pltpu

F32 = jnp.float32
BF16 = jnp.bfloat16

D_MODEL = 1024
HEAD_DIM = 64
N_HEADS_A = 8
N_KV_A = 2
GROUP_A = N_HEADS_A // N_KV_A
N_HEADS_B = 8
D_A = N_HEADS_A * HEAD_DIM
D_KV_A = N_KV_A * HEAD_DIM
D_B = N_HEADS_B * HEAD_DIM
D_IN = D_A + 2 * D_KV_A + 3 * D_B
D_FF = 4 * D_MODEL
D_PLE = 256
GRID_W = 64
ROPE_THETA = 10000.0
DILATIONS = (1, 4, 16)
HALF_WIN = 64
N_BUCKETS = 32
MAX_DISTANCE = 1024
EPS = 1e-6
NEG_BIG = -1e30
Q_SCALE = HEAD_DIM ** -0.5

ADAM_LR = 0.001
ADAM_B1 = 0.9
ADAM_B2 = 0.999
ADAM_EPS = 1e-08
ADAM_WD = 0.01
ADAM_STEP = 10

N_CHIPS = 4
MESH = pl.DeviceIdType.MESH

ROW_TILE = 512
ATT_TQ = 256
ATT_TQ_BWD = 512
ATT_TK_FWD = 2048
ATT_UNROLL_FWD = 8
ATT_TK_BWD = 512
ATT_UNROLL_BWD = 8
SWA_TQ = 128
SWA_MIN_BLOCK = 1024
DW_TS = 2048
VMEM_LIMIT = 56 * 1024 * 1024

NT = (((1,), (1,)), ((), ()))
TN = (((0,), (0,)), ((), ()))


def _cparams(sem=None, vmem=VMEM_LIMIT):
    return pltpu.CompilerParams(dimension_semantics=sem, vmem_limit_bytes=vmem)


def _full(shape):
    n = len(shape)
    return pl.BlockSpec(shape, lambda *_: (0,) * n)


def _rows(tm, width):
    return pl.BlockSpec((tm, width), lambda i: (i, 0))


def _split3(a):
    a1 = a.astype(BF16)
    r = a - a1.astype(F32)
    a2 = r.astype(BF16)
    a3 = (r - a2.astype(F32)).astype(BF16)
    return a1, a2, a3


def _xdot(a, sel):
    a1, a2, a3 = _split3(a)
    d = lambda p: jnp.dot(p, sel, preferred_element_type=F32)
    return d(a1) + d(a2) + d(a3)


def _mm(a, b):
    return jnp.dot(a, b, preferred_element_type=F32)


def _mm_nt(a, b):
    return lax.dot_general(a, b, NT, preferred_element_type=F32)


def _mm_tn(a, b):
    return lax.dot_general(a, b, TN, preferred_element_type=F32)


def _rms_stats(x):
    r = lax.rsqrt(jnp.mean(x * x, axis=-1, keepdims=True) + EPS)
    return x * r, r


def _rms_bwd(dy, xh, r, g):
    gdy = dy * g
    dx = r * (gdy - xh * jnp.mean(gdy * xh, axis=-1, keepdims=True))
    dg = jnp.sum(dy * xh, axis=0, keepdims=True)
    return dx, dg


def _acc_out(ref, val):
    @pl.when(pl.program_id(0) == 0)
    def _():
        ref[...] = jnp.zeros_like(ref)

    ref[...] += val


def _swap_halves(x, first_half):
    return jnp.where(first_half, pltpu.roll(x, 96, 1), pltpu.roll(x, 32, 1))


def _first_half_mask(shape):
    return (lax.broadcasted_iota(jnp.int32, shape, 1) % HEAD_DIM) < (HEAD_DIM // 2)


def _rope_tables(s_len):
    t = jnp.arange(s_len)
    row = (t // GRID_W).astype(F32)
    col = (t % GRID_W).astype(F32)
    n_axis = HEAD_DIM // 4
    inv_freq = ROPE_THETA ** (-jnp.arange(n_axis, dtype=F32) / n_axis)
    ang = jnp.concatenate([row[:, None] * inv_freq, col[:, None] * inv_freq], axis=-1)
    c, s = jnp.cos(ang), jnp.sin(ang)
    cc = jnp.concatenate([c, c, c, c], axis=-1)
    ss = jnp.concatenate([-s, s, -s, s], axis=-1)
    return cc, ss


def _group_ones(width):
    i = jnp.arange(width)
    return (i[:, None] // HEAD_DIM == i[None, :] // HEAD_DIM).astype(BF16)


def _t5_bucket(rel):
    nb = N_BUCKETS // 2
    max_exact = nb // 2
    side = jnp.where(rel > 0, nb, 0)
    n = jnp.abs(rel)
    large = max_exact + (jnp.log(jnp.maximum(n, max_exact).astype(F32) / max_exact)
                         / math.log(MAX_DISTANCE / max_exact) * (nb - max_exact)).astype(jnp.int32)
    large = jnp.minimum(large, nb - 1)
    return side + jnp.where(n < max_exact, n, large)


def _in_window(tq):
    qi = jnp.arange(tq)
    kj = jnp.arange(tq + 2 * HALF_WIN)
    return jnp.abs(kj[None, :] - HALF_WIN - qi[:, None]) <= HALF_WIN


def _bucket_onehot(tq, dilation):
    qi = jnp.arange(tq)
    kj = jnp.arange(tq + 2 * HALF_WIN)
    rel = kj[None, :] - HALF_WIN - qi[:, None]
    bucket = _t5_bucket(rel * dilation).reshape(-1)
    return (bucket[:, None] == jnp.arange(128)[None, :]).astype(BF16)


def _in_proj(x, g1, w_in, cc, ss, gq2, gk2, ones128):
    s_len = x.shape[0]
    tm = min(ROW_TILE, s_len)

    def body(x_ref, g_ref, w_ref, cc_ref, ss_ref, gq_ref, gk_ref, one_ref,
             xn_ref, qpre_ref, kpre_ref, qa_ref, kv_ref, qb_ref, kb_ref, vb_ref):
        xh, _ = _rms_stats(x_ref[...])
        xn = (xh * g_ref[...]).astype(BF16)
        xn_ref[...] = xn
        proj = _mm(xn, w_ref[...])
        first_half = _first_half_mask((tm, 128))
        ones = one_ref[...]
        cc_t, ss_t = cc_ref[...], ss_ref[...]

        def norm_rope(xc, gain):
            ms = _xdot(xc * xc, ones) * (1.0 / HEAD_DIM)
            y = xc * lax.rsqrt(ms + EPS) * gain
            return y * cc_t + _swap_halves(y, first_half) * ss_t

        qpre_ref[...] = proj[:, :D_A]
        kpre_ref[...] = proj[:, D_A:D_A + D_KV_A]
        for c in range(D_A // 128):
            y = norm_rope(proj[:, 128 * c:128 * (c + 1)], gq_ref[...])
            qa_ref[:, 128 * c:128 * (c + 1)] = (y * Q_SCALE).astype(BF16)
        ka = norm_rope(proj[:, D_A:D_A + D_KV_A], gk_ref[...])
        o = D_A + D_KV_A
        va = proj[:, o:o + D_KV_A]
        low = _low_lanes(tm)
        kv_ref[0] = jnp.where(low, ka, pltpu.roll(va, HEAD_DIM, 1)).astype(BF16)
        kv_ref[1] = jnp.where(low, pltpu.roll(ka, HEAD_DIM, 1), va).astype(BF16)
        o += D_KV_A
        qb_ref[...] = (proj[:, o:o + D_B] * Q_SCALE).astype(BF16)
        kb_ref[...] = proj[:, o + D_B:o + 2 * D_B].astype(BF16)
        vb_ref[...] = proj[:, o + 2 * D_B:o + 3 * D_B].astype(BF16)

    sds = jax.ShapeDtypeStruct
    return pl.pallas_call(
        body, name="in_proj", grid=(s_len // tm,),
        in_specs=[_rows(tm, D_MODEL), _full((1, D_MODEL)), _full((D_MODEL, D_IN)), _rows(tm, 128), _rows(tm, 128),
                  _full((1, 128)), _full((1, 128)), _full((128, 128))],
        out_specs=[_rows(tm, D_MODEL), _rows(tm, D_A), _rows(tm, D_KV_A), _rows(tm, D_A),
                   pl.BlockSpec((N_KV_A, tm, 128), lambda i: (0, i, 0)), _rows(tm, D_B), _rows(tm, D_B),
                   _rows(tm, D_B)],
        out_shape=[sds((s_len, D_MODEL), BF16), sds((s_len, D_A), F32), sds((s_len, D_KV_A), F32),
                   sds((s_len, D_A), BF16), sds((N_KV_A, s_len, 128), BF16),
                   sds((s_len, D_B), BF16), sds((s_len, D_B), BF16), sds((s_len, D_B), BF16)],
        compiler_params=_cparams(("parallel",)),
    )(x, g1, w_in, cc, ss, gq2, gk2, ones128)


def _stat_spec(tm):
    return pl.BlockSpec((N_HEADS_B, tm, 1), lambda i: (0, i, 0))


def _out_proj(ya, yb, x, g_a, g_b, w_out, g_post, g_mlp_pre):
    s_len = x.shape[0]
    tm = min(ROW_TILE, s_len)

    def body(ya_ref, yb_ref, x_ref, ga_ref, gb_ref, w_ref, gp_ref, gm_ref, ycat_ref, y2_ref, h1_ref, xn2_ref):
        ah, _ = _rms_stats(ya_ref[...])
        bh, _ = _rms_stats(yb_ref[...])
        ycat = jnp.concatenate([ah * ga_ref[...], bh * gb_ref[...]], axis=-1).astype(BF16)
        ycat_ref[...] = ycat
        y2 = _mm(ycat, w_ref[...])
        y2_ref[...] = y2
        y2h, _ = _rms_stats(y2)
        h1 = x_ref[...] + y2h * gp_ref[...]
        h1_ref[...] = h1
        h1h, _ = _rms_stats(h1)
        xn2_ref[...] = (h1h * gm_ref[...]).astype(BF16)

    sds = jax.ShapeDtypeStruct
    return pl.pallas_call(
        body, name="out_proj", grid=(s_len // tm,),
        in_specs=[_rows(tm, D_A), _rows(tm, D_B), _rows(tm, D_MODEL), _full((1, D_A)), _full((1, D_B)),
                  _full((D_MODEL, D_MODEL)), _full((1, D_MODEL)), _full((1, D_MODEL))],
        out_specs=[_rows(tm, D_MODEL)] * 4,
        out_shape=[sds((s_len, D_MODEL), BF16), sds((s_len, D_MODEL), F32), sds((s_len, D_MODEL), F32),
                   sds((s_len, D_MODEL), BF16)],
        compiler_params=_cparams(("parallel",)),
    )(ya, yb, x, g_a, g_b, w_out, g_post, g_mlp_pre)


def _ff1(xn2, w_ff1):
    s_len = xn2.shape[0]
    tm = min(ROW_TILE, s_len)

    def body(x_ref, w_ref, u_ref):
        u_ref[...] = _mm(x_ref[...], w_ref[...])

    return pl.pallas_call(
        body, name="ff1", grid=(s_len // tm,),
        in_specs=[_rows(tm, D_MODEL), _full((D_MODEL, D_FF))],
        out_specs=_rows(tm, D_FF),
        out_shape=jax.ShapeDtypeStruct((s_len, D_FF), F32),
        compiler_params=_cparams(("parallel",)),
    )(xn2, w_ff1)


def _ff2(u, w_ff2, h1, g_post, g_ple):
    s_len = u.shape[0]
    tm = min(ROW_TILE, s_len)

    def body(u_ref, w_ref, h1_ref, gp_ref, gl_ref, f2_ref, h2_ref, xn3_ref):
        f = jnp.square(jnp.maximum(u_ref[...], 0.0)).astype(BF16)
        f2 = _mm(f, w_ref[...])
        f2_ref[...] = f2
        f2h, _ = _rms_stats(f2)
        h2 = h1_ref[...] + f2h * gp_ref[...]
        h2_ref[...] = h2
        h2h, _ = _rms_stats(h2)
        xn3_ref[...] = (h2h * gl_ref[...]).astype(BF16)

    sds = jax.ShapeDtypeStruct
    return pl.pallas_call(
        body, name="ff2", grid=(s_len // tm,),
        in_specs=[_rows(tm, D_FF), _full((D_FF, D_MODEL)), _rows(tm, D_MODEL), _full((1, D_MODEL)),
                  _full((1, D_MODEL))],
        out_specs=[_rows(tm, D_MODEL)] * 3,
        out_shape=[sds((s_len, D_MODEL), F32), sds((s_len, D_MODEL), F32), sds((s_len, D_MODEL), BF16)],
        compiler_params=_cparams(("parallel",)),
    )(u, w_ff2, h1, g_post, g_ple)


def _ple_loss(xn3, p, h2, f2, tgt, w_gate, w_ple, g_ple, g_mlp_post):
    s_len = h2.shape[0]
    tm = min(ROW_TILE, s_len)

    def body(xn3_ref, p_ref, h2_ref, f2_ref, t_ref, wg_ref, wp_ref, gl_ref, gp_ref,
             dh2_ref, df2_ref, dgl_ref, dpp_ref, loss_ref, dgple_ref, dgpost_ref):
        gate = jax.nn.sigmoid(_mm(xn3_ref[...], wg_ref[...]))
        pp = _mm(p_ref[...].astype(BF16), wp_ref[...])
        h2 = h2_ref[...]
        err = h2 + gate * pp - t_ref[...]
        sq = jnp.sum(jnp.sum(err * err, axis=1, keepdims=True), axis=0, keepdims=True)
        _acc_out(loss_ref, sq * (0.5 / D_MODEL))
        dh3 = err * (1.0 / D_MODEL)
        dgl = (dh3 * pp) * gate * (1.0 - gate)
        dgl_b = dgl.astype(BF16)
        dgl_ref[...] = dgl_b
        dpp_ref[...] = (dh3 * gate).astype(BF16)
        dxn3 = _mm_nt(dgl_b, wg_ref[...])
        h2h, r2 = _rms_stats(h2)
        dx, dg = _rms_bwd(dxn3, h2h, r2, gl_ref[...])
        _acc_out(dgple_ref, dg)
        dh2 = dh3 + dx
        dh2_ref[...] = dh2
        f2h, rf = _rms_stats(f2_ref[...])
        df2, dg = _rms_bwd(dh2, f2h, rf, gp_ref[...])
        _acc_out(dgpost_ref, dg)
        df2_ref[...] = df2.astype(BF16)

    sds = jax.ShapeDtypeStruct
    return pl.pallas_call(
        body, name="ple_loss", grid=(s_len // tm,),
        in_specs=[_rows(tm, D_MODEL), _rows(tm, D_PLE), _rows(tm, D_MODEL), _rows(tm, D_MODEL), _rows(tm, D_MODEL),
                  _full((D_MODEL, D_MODEL)), _full((D_PLE, D_MODEL)), _full((1, D_MODEL)), _full((1, D_MODEL))],
        out_specs=[_rows(tm, D_MODEL)] * 3 + [_rows(tm, D_MODEL), _full((1, 1)), _full((1, D_MODEL)),
                                              _full((1, D_MODEL))],
        out_shape=[sds((s_len, D_MODEL), F32), sds((s_len, D_MODEL), BF16), sds((s_len, D_MODEL), BF16),
                   sds((s_len, D_MODEL), BF16), sds((1, 1), F32), sds((1, D_MODEL), F32), sds((1, D_MODEL), F32)],
        compiler_params=_cparams(("arbitrary",)),
    )(xn3, p, h2, f2, tgt, w_gate, w_ple, g_ple, g_mlp_post)


def _ff2_bwd(df2, w_ff2, u):
    s_len = u.shape[0]
    tm = min(ROW_TILE, s_len)

    def body(d_ref, w_ref, u_ref, du_ref):
        df = _mm_nt(d_ref[...], w_ref[...])
        du_ref[...] = (df * (2.0 * jnp.maximum(u_ref[...], 0.0))).astype(BF16)

    return pl.pallas_call(
        body, name="ff2_bwd", grid=(s_len // tm,),
        in_specs=[_rows(tm, D_MODEL), _full((D_FF, D_MODEL)), _rows(tm, D_FF)],
        out_specs=_rows(tm, D_FF),
        out_shape=jax.ShapeDtypeStruct((s_len, D_FF), BF16),
        compiler_params=_cparams(("parallel",)),
    )(df2, w_ff2, u)


def _ff1_bwd(du, w_ff1, dh2, h1, y2, g_mlp_pre, g_post):
    s_len = du.shape[0]
    tm = min(ROW_TILE, s_len)

    def body(du_ref, w_ref, dh2_ref, h1_ref, y2_ref, gm_ref, gp_ref, dh1_ref, dy2_ref, dgm_ref, dgp_ref):
        dxn2 = _mm_nt(du_ref[...], w_ref[...])
        h1h, r1 = _rms_stats(h1_ref[...])
        dx, dg = _rms_bwd(dxn2, h1h, r1, gm_ref[...])
        _acc_out(dgm_ref, dg)
        dh1 = dh2_ref[...] + dx
        dh1_ref[...] = dh1
        y2h, ry = _rms_stats(y2_ref[...])
        dy2, dg = _rms_bwd(dh1, y2h, ry, gp_ref[...])
        _acc_out(dgp_ref, dg)
        dy2_ref[...] = dy2.astype(BF16)

    sds = jax.ShapeDtypeStruct
    return pl.pallas_call(
        body, name="ff1_bwd", grid=(s_len // tm,),
        in_specs=[_rows(tm, D_FF), _full((D_MODEL, D_FF)), _rows(tm, D_MODEL), _rows(tm, D_MODEL),
                  _rows(tm, D_MODEL), _full((1, D_MODEL)), _full((1, D_MODEL))],
        out_specs=[_rows(tm, D_MODEL), _rows(tm, D_MODEL), _full((1, D_MODEL)), _full((1, D_MODEL))],
        out_shape=[sds((s_len, D_MODEL), F32), sds((s_len, D_MODEL), BF16), sds((1, D_MODEL), F32),
                   sds((1, D_MODEL), F32)],
        compiler_params=_cparams(("arbitrary",)),
    )(du, w_ff1, dh2, h1, y2, g_mlp_pre, g_post)


def _out_proj_bwd(dy2, w_out, ya, yb, lse_b, g_a, g_b):
    s_len = ya.shape[0]
    tm = min(ROW_TILE, s_len)

    def body(d_ref, w_ref, ya_ref, yb_ref, lse_ref, ga_ref, gb_ref, dya_ref, dyb_ref, da_ref, st_ref, dga_ref,
             dgb_ref):
        dycat = _mm_nt(d_ref[...], w_ref[...])
        lane = lax.broadcasted_iota(jnp.int32, (tm, 128), 1)
        low = lane < HEAD_DIM
        is_lse = (lane % HEAD_DIM) < (HEAD_DIM // 2)

        def head_sums(prod_chunk):
            return (jnp.sum(jnp.where(low, prod_chunk, 0.0), axis=1, keepdims=True),
                    jnp.sum(jnp.where(low, 0.0, prod_chunk), axis=1, keepdims=True))

        ya = ya_ref[...]
        yh, r = _rms_stats(ya)
        dya, dg = _rms_bwd(dycat[:, :D_A], yh, r, ga_ref[...])
        _acc_out(dga_ref, dg)
        dya_ref[...] = dya
        prod = dya * ya
        for c in range(D_A // 128):
            da_ref[2 * c], da_ref[2 * c + 1] = head_sums(prod[:, 128 * c:128 * (c + 1)])

        yb = yb_ref[...]
        yh, r = _rms_stats(yb)
        dyb, dg = _rms_bwd(dycat[:, D_A:], yh, r, gb_ref[...])
        _acc_out(dgb_ref, dg)
        dyb_ref[...] = dyb.astype(BF16)
        prod = dyb * yb
        for c in range(D_B // 128):
            sl = slice(128 * c, 128 * (c + 1))
            d_lo, d_hi = head_sums(prod[:, sl])
            st_ref[:, sl] = jnp.where(is_lse, lse_ref[:, sl], jnp.where(low, d_lo, d_hi))

    sds = jax.ShapeDtypeStruct
    return pl.pallas_call(
        body, name="out_proj_bwd", grid=(s_len // tm,),
        in_specs=[_rows(tm, D_MODEL), _full((D_MODEL, D_MODEL)), _rows(tm, D_A), _rows(tm, D_B), _rows(tm, D_B),
                  _full((1, D_A)), _full((1, D_B))],
        out_specs=[_rows(tm, D_A), _rows(tm, D_B), _stat_spec(tm), _rows(tm, D_B), _full((1, D_A)),
                   _full((1, D_B))],
        out_shape=[sds((s_len, D_A), F32), sds((s_len, D_B), BF16), sds((N_HEADS_A, s_len, 1), F32),
                   sds((s_len, D_B), F32), sds((1, D_A), F32), sds((1, D_B), F32)],
        compiler_params=_cparams(("arbitrary",)),
    )(dy2, w_out, ya, yb, lse_b, g_a, g_b)


def _in_proj_bwd(dqr, dkv, dqb, dkb, dvb, qpre, kpre, x, dh1, g1, w_in, cc, ss, gq2, gk2, ones128):
    s_len = x.shape[0]
    tm = min(ROW_TILE // 2, s_len)

    def body(dqr_ref, dkv_ref, dq0, dq1, dq2, dk0, dk1, dk2, dv0, dv1, dv2, qpre_ref, kpre_ref, x_ref,
             dh1_ref, g_ref, w_ref, cc_ref, ss_ref, gq_ref, gk_ref, one_ref, dproj_ref, gx_ref, dg1_ref, dgq_ref,
             dgk_ref):
        low = _low_lanes(tm)
        dkr = jnp.where(low, dkv_ref[0], pltpu.roll(dkv_ref[1], HEAD_DIM, 1))
        dva = jnp.where(low, pltpu.roll(dkv_ref[0], HEAD_DIM, 1), dkv_ref[1])
        first_half = _first_half_mask((tm, 128))
        ones = one_ref[...]
        cc_t, ss_t = cc_ref[...], ss_ref[...]

        def norm_rope_bwd(dy, xc, gain):
            dn = dy * cc_t - _swap_halves(dy, first_half) * ss_t
            r = lax.rsqrt(_xdot(xc * xc, ones) * (1.0 / HEAD_DIM) + EPS)
            xh = xc * r
            gdy = dn * gain
            dx = r * (gdy - xh * (_xdot(gdy * xh, ones) * (1.0 / HEAD_DIM)))
            return dx, jnp.sum(dn * xh, axis=0, keepdims=True)

        dgq = jnp.zeros((1, 128), F32)
        parts = []
        for c in range(D_A // 128):
            sl = slice(128 * c, 128 * (c + 1))
            dx, dg = norm_rope_bwd(dqr_ref[:, sl] * Q_SCALE, qpre_ref[:, sl], gq_ref[...])
            parts.append(dx)
            dgq = dgq + dg
        dxk, dgk = norm_rope_bwd(dkr, kpre_ref[...], gk_ref[...])
        _acc_out(dgq_ref, dgq)
        _acc_out(dgk_ref, dgk)
        parts += [dxk, dva, (dq0[...] + dq1[...] + dq2[...]) * Q_SCALE, dk0[...] + dk1[...] + dk2[...],
                  dv0[...] + dv1[...] + dv2[...]]
        dproj = jnp.concatenate(parts, axis=-1).astype(BF16)
        dproj_ref[...] = dproj
        dxn = _mm_nt(dproj, w_ref[...])
        xh, r = _rms_stats(x_ref[...])
        dx, dg = _rms_bwd(dxn, xh, r, g_ref[...])
        _acc_out(dg1_ref, dg)
        gx_ref[...] = dh1_ref[...] + dx

    sds = jax.ShapeDtypeStruct
    return pl.pallas_call(
        body, name="in_proj_bwd", grid=(s_len // tm,),
        in_specs=[_rows(tm, D_A), pl.BlockSpec((N_KV_A, tm, 128), lambda i: (0, i, 0))] + [_rows(tm, D_B)] * 9
                 + [_rows(tm, D_A), _rows(tm, D_KV_A), _rows(tm, D_MODEL), _rows(tm, D_MODEL),
                    _full((1, D_MODEL)), _full((D_MODEL, D_IN)), _rows(tm, 128), _rows(tm, 128), _full((1, 128)),
                    _full((1, 128)), _full((128, 128))],
        out_specs=[_rows(tm, D_IN), _rows(tm, D_MODEL), _full((1, D_MODEL)), _full((1, 128)), _full((1, 128))],
        out_shape=[sds((s_len, D_IN), BF16), sds((s_len, D_MODEL), F32), sds((1, D_MODEL), F32),
                   sds((1, 128), F32), sds((1, 128), F32)],
        compiler_params=_cparams(("arbitrary",)),
    )(dqr, dkv, *dqb, *dkb, *dvb, qpre, kpre, x, dh1, g1, w_in, cc, ss, gq2, gk2, ones128)


def _dw(a, b, name, relu2=False):
    s_len, ka = a.shape
    n = b.shape[1]
    ts = min(DW_TS, s_len)
    bk = min(ka, 1024)
    bn = n if n % 1024 else 1024

    def body(a_ref, b_ref, o_ref):
        @pl.when(pl.program_id(2) == 0)
        def _():
            o_ref[...] = jnp.zeros_like(o_ref)

        av = a_ref[...]
        if relu2:
            av = jnp.square(jnp.maximum(av, 0.0))
        o_ref[...] += _mm_tn(av.astype(BF16), b_ref[...])

    return pl.pallas_call(
        body, name=name, grid=(ka // bk, n // bn, s_len // ts),
        in_specs=[pl.BlockSpec((ts, bk), lambda i, j, k: (k, i)), pl.BlockSpec((ts, bn), lambda i, j, k: (k, j))],
        out_specs=pl.BlockSpec((bk, bn), lambda i, j, k: (i, j)),
        out_shape=jax.ShapeDtypeStruct((ka, n), F32),
        compiler_params=_cparams(("parallel", "parallel", "arbitrary")),
    )(a, b)


def _stack_heads(block, low, data_low):
    parts = []
    for c in range(GROUP_A // 2):
        chunk = block[:, 128 * c:128 * (c + 1)]
        swapped = pltpu.roll(chunk, HEAD_DIM, 1)
        for h_low in (chunk, swapped) if data_low else (swapped, chunk):
            parts.append(jnp.where(low, h_low, 0.0) if data_low else jnp.where(low, 0.0, h_low))
    return jnp.concatenate(parts, axis=0).astype(BF16)


def _unstack_heads(stacked, low, tq, data_low):
    chunks = []
    for c in range(GROUP_A // 2):
        even = stacked[2 * c * tq:(2 * c + 1) * tq]
        odd = stacked[(2 * c + 1) * tq:(2 * c + 2) * tq]
        if data_low:
            chunks.append(jnp.where(low, even, pltpu.roll(odd, HEAD_DIM, 1)))
        else:
            chunks.append(jnp.where(low, pltpu.roll(even, HEAD_DIM, 1), odd))
    return chunks


def _attn_a_fwd(qa, kv):
    s_len = kv.shape[1]
    tq = min(ATT_TQ, s_len)
    tk = min(ATT_TK_FWD, s_len)
    rows = GROUP_A * tq

    def body(q_ref, kv_ref, o_ref, lse_ref):
        low = _low_lanes(tq)
        low_k = _low_lanes(tk)
        q = _stack_heads(q_ref[...].astype(F32), low, data_low=True)

        def block(j, m, acc):
            kvj = kv_ref[0, pl.ds(pl.multiple_of(j * tk, tk), tk), :]
            s = _mm_nt(q, kvj)
            m_new = jnp.maximum(m, jnp.max(s, axis=1, keepdims=True))
            p = jnp.exp(s - m_new).astype(BF16)
            return m_new, jnp.exp(m - m_new) * acc + _mm(p, jnp.where(low_k, jnp.ones_like(kvj), kvj))

        def step(j, carry):
            for u in range(unroll):
                carry = block(unroll * j + u, *carry)
            return carry

        unroll = math.gcd(s_len // tk, ATT_UNROLL_FWD)
        init = (jnp.full((rows, 1), -jnp.inf, F32), jnp.zeros((rows, 128), F32))
        m, acc = lax.fori_loop(0, s_len // (tk * unroll), step, init)
        for c, chunk in enumerate(_unstack_heads(acc / pltpu.roll(acc, HEAD_DIM, 1), low, tq, data_low=False)):
            o_ref[:, 128 * c:128 * (c + 1)] = chunk
        lse_ref[...] = (m + jnp.log(acc[:, :1])).reshape(GROUP_A, tq, 1)

    return pl.pallas_call(
        body, name="attn_a_fwd", grid=(N_KV_A, s_len // tq),
        in_specs=[pl.BlockSpec((tq, 256), lambda g, i: (i, g)),
                  pl.BlockSpec((1, s_len, 128), lambda g, i: (g, 0, 0))],
        out_specs=[pl.BlockSpec((tq, 256), lambda g, i: (i, g)),
                   pl.BlockSpec((GROUP_A, tq, 1), lambda g, i: (g, i, 0))],
        out_shape=[jax.ShapeDtypeStruct((s_len, D_A), F32),
                   jax.ShapeDtypeStruct((N_HEADS_A, s_len, 1), F32)],
        compiler_params=_cparams(("parallel", "parallel")),
    )(qa, kv)


def _attn_a_bwd(qa, dya, kv, lse, delta):
    s_len = kv.shape[1]
    tq = min(ATT_TQ_BWD, s_len)
    tk = min(ATT_TK_BWD, s_len)
    rows = GROUP_A * tq

    def body(q_ref, do_ref, kv_ref, lse_ref, dl_ref, dq_ref, dkv_ref):
        @pl.when(pl.program_id(1) == 0)
        def _():
            dkv_ref[...] = jnp.zeros_like(dkv_ref)

        low = _low_lanes(tq)
        q = _stack_heads(q_ref[...].astype(F32), low, data_low=True)
        do = _stack_heads(do_ref[...], low, data_low=False)
        lse_t = lse_ref[...].reshape(rows, 1)
        dl_t = dl_ref[...].reshape(rows, 1)
        q_t = q.T
        do_t = do.T

        def block(j, dq):
            span = pl.ds(pl.multiple_of(j * tk, tk), tk)
            kvj = kv_ref[0, span, :]
            p = jnp.exp(_mm_nt(q, kvj) - lse_t)
            ds = (p * (_mm_nt(do, kvj) - dl_t)).astype(BF16)
            dkv_ref[0, :, span] += _mm(q_t, ds) + _mm(do_t, p.astype(BF16))
            return dq + _mm(ds, kvj)

        def step(j, dq):
            for u in range(unroll):
                dq = block(unroll * j + u, dq)
            return dq

        unroll = math.gcd(s_len // tk, ATT_UNROLL_BWD)
        dq = lax.fori_loop(0, s_len // (tk * unroll), step, jnp.zeros((rows, 128), F32))
        for c, chunk in enumerate(_unstack_heads(dq, low, tq, data_low=True)):
            dq_ref[:, 128 * c:128 * (c + 1)] = chunk

    return pl.pallas_call(
        body, name="attn_a_bwd", grid=(N_KV_A, s_len // tq),
        in_specs=[pl.BlockSpec((tq, 256), lambda g, i: (i, g)),
                  pl.BlockSpec((tq, 256), lambda g, i: (i, g)),
                  pl.BlockSpec((1, s_len, 128), lambda g, i: (g, 0, 0)),
                  pl.BlockSpec((GROUP_A, tq, 1), lambda g, i: (g, i, 0)),
                  pl.BlockSpec((GROUP_A, tq, 1), lambda g, i: (g, i, 0))],
        out_specs=[pl.BlockSpec((tq, 256), lambda g, i: (i, g)),
                   pl.BlockSpec((1, 128, s_len), lambda g, i: (g, 0, 0))],
        out_shape=[jax.ShapeDtypeStruct((s_len, D_A), F32),
                   jax.ShapeDtypeStruct((N_KV_A, 128, s_len), F32)],
        compiler_params=_cparams(("parallel", "arbitrary")),
    )(qa, dya, kv, lse, delta)


class _SwaGeometry:
    def __init__(self, s_len, r):
        self.r = r
        self.tq = SWA_TQ
        self.block = min(max(SWA_MIN_BLOCK, 2 * SWA_TQ * r), s_len)
        self.halo = HALF_WIN * r
        self.nsub = self.block // (self.tq * r)
        self.band = self.tq + 2 * HALF_WIN
        self.length = s_len // r
        self.nblk = s_len // self.block
        self.nhalo = s_len // self.halo
        assert self.nsub * self.tq * r == self.block and self.block % self.halo == 0

    def specs(self):
        per = self.block // self.halo
        cur = pl.BlockSpec((self.block, 128), lambda c, i: (i, c))
        prev = pl.BlockSpec((self.halo, 128), lambda c, i: (jnp.maximum(i * per - 1, 0), c))
        nxt = pl.BlockSpec((self.halo, 128), lambda c, i: (jnp.minimum((i + 1) * per, self.nhalo - 1), c))
        return prev, cur, nxt

    def tiles(self):
        return [(rho + self.r * j * self.tq, self.halo + rho + self.r * (j * self.tq - HALF_WIN), j)
                for j in range(self.nsub) for rho in range(self.r)]

    def own(self, start):
        return pl.ds(start, self.tq, stride=self.r)

    def around(self, start):
        return pl.ds(start, self.band, stride=self.r)

    def fill(self, dst, prev_ref, cur_ref, next_ref):
        dst[:self.halo, :] = prev_ref[...].astype(F32)
        dst[self.halo:self.halo + self.block, :] = cur_ref[...].astype(F32)
        dst[self.halo + self.block:, :] = next_ref[...].astype(F32)

    def first_position(self, j):
        return (pl.program_id(1) * self.block) // self.r + j * self.tq

    def outside(self, j, copies=1):
        pos = self.first_position(j) - HALF_WIN + lax.broadcasted_iota(jnp.int32, (1, copies * self.band), 1) % self.band
        return jnp.where((pos >= 0) & (pos < self.length), 0.0, NEG_BIG)

    def extended(self):
        return pltpu.VMEM((self.block + 2 * self.halo, 128), F32)

    def plain(self):
        return pltpu.VMEM((self.block, 128), F32)


def _low_lanes(rows):
    return lax.broadcasted_iota(jnp.int32, (rows, 128), 1) < HEAD_DIM


def _one_head(x, low, half):
    return jnp.where(low if half == 0 else jnp.logical_not(low), x, 0.0).astype(BF16)


def _two_heads(x, low):
    return jnp.concatenate([_one_head(x, low, 0), _one_head(x, low, 1)], axis=0)


def _carry_ride(base_body, n_in, n_out, n_scratch, ride, grid):
    if ride is None:
        return base_body
    n = ride.n

    def body(*refs):
        o = n_in + n
        ins, ride_ins = refs[:n_in], refs[n_in:o]
        outs, ride_outs = refs[o:o + n_out], refs[o + n_out:o + n_out + n]
        o += n_out + n
        scratch, sems = refs[o:o + n_scratch], refs[o + n_scratch:]
        at_first = (pl.program_id(0) == 0) & (pl.program_id(1) == 0)
        at_last = (pl.program_id(0) == grid[0] - 1) & (pl.program_id(1) == grid[1] - 1)

        @pl.when(at_first)
        def _():
            ride.start(ride_ins, ride_outs, sems)

        base_body(*ins, *outs, *scratch)

        @pl.when(at_last)
        def _():
            ride.finish(ride_ins, ride_outs, sems)

    return body


def _ride_call(base_body, name, grid, in_specs, out_specs, out_shape, scratch_shapes, operands, ride):
    n = 0 if ride is None else ride.n
    extra = [] if ride is None else ride.operands
    outs = pl.pallas_call(
        _carry_ride(base_body, len(in_specs), len(out_specs), len(scratch_shapes), ride, grid), name=name, grid=grid,
        in_specs=list(in_specs) + [ANY] * n, out_specs=list(out_specs) + [ANY] * n,
        out_shape=list(out_shape) + ([] if ride is None else ride.out_shape()),
        scratch_shapes=list(scratch_shapes) + ([] if ride is None else ride.scratch_shapes()),
        compiler_params=_cparams(("arbitrary", "arbitrary")),
    )(*operands, *extra)
    return outs[:len(out_specs)], outs[len(out_specs):]


def _swa_fwd(q, k, v, bias, r, so_far=None, ride=None):
    geo = _SwaGeometry(q.shape[0], r)
    prev, cur, nxt = geo.specs()

    def body(q_ref, kp, kc, kn, vp, vc, vn, b_ref, *rest):
        if so_far is None:
            o_ref, lse_ref, qf, kf, vf = rest
        else:
            o_old_ref, lse_old_ref, o_ref, lse_ref, qf, kf, vf = rest
        qf[...] = q_ref[...].astype(F32)
        geo.fill(kf, kp, kc, kn)
        geo.fill(vf, vp, vc, vn)
        tq = geo.tq
        low_q = _low_lanes(tq)
        bias = b_ref[...].reshape(2 * tq, geo.band)
        for own, around, j in geo.tiles():
            q2 = _two_heads(qf[geo.own(own), :], low_q)
            kb = kf[geo.around(around), :].astype(BF16)
            vb = vf[geo.around(around), :].astype(BF16)
            s = _mm_nt(q2, kb) + bias + geo.outside(j)
            m = jnp.max(s, axis=1, keepdims=True)
            e = jnp.exp(s - m)
            l = jnp.sum(e, axis=1, keepdims=True)
            o2 = _mm(e.astype(BF16), vb) / l
            lse2 = m + jnp.log(l)
            o_new = jnp.where(low_q, o2[:tq], o2[tq:])
            lse_new = jnp.where(low_q, lse2[:tq], lse2[tq:])
            if so_far is not None:
                o_old, lse_old = o_old_ref[geo.own(own), :], lse_old_ref[geo.own(own), :]
                top = jnp.maximum(lse_old, lse_new)
                w_old, w_new = jnp.exp(lse_old - top), jnp.exp(lse_new - top)
                o_new = (w_old * o_old + w_new * o_new) / (w_old + w_new)
                lse_new = top + jnp.log(w_old + w_new)
            o_ref[geo.own(own), :] = o_new
            lse_ref[geo.own(own), :] = lse_new

    sds = jax.ShapeDtypeStruct
    before = () if so_far is None else tuple(so_far)
    (o, lse), carried = _ride_call(
        body, "swa_fwd_%d" % r, (D_B // 128, geo.nblk),
        [cur, prev, cur, nxt, prev, cur, nxt, pl.BlockSpec((2, geo.tq, geo.band), lambda c, i: (c, 0, 0))]
        + [cur] * len(before),
        [cur, cur], [sds(q.shape, F32), sds(q.shape, F32)], [geo.plain(), geo.extended(), geo.extended()],
        (q, k, k, k, v, v, v, bias) + before, ride)
    return o, lse, carried


def _head_stats(st, half):
    lo = HEAD_DIM * half
    return st[:, lo:lo + 1], st[:, lo + HEAD_DIM // 2:lo + HEAD_DIM // 2 + 1]


def _swa_bwd_q(q, k, v, dy, st, bias, r, ride=None):
    geo = _SwaGeometry(q.shape[0], r)
    prev, cur, nxt = geo.specs()
    bias_spec = pl.BlockSpec((2, geo.tq, geo.band), lambda c, i: (c, 0, 0))

    def body(q_ref, kp, kc, kn, vp, vc, vn, dy_ref, st_ref, b_ref, dq_ref, db_ref, qf, kf, vf, dyf):
        @pl.when(pl.program_id(1) == 0)
        def _():
            db_ref[...] = jnp.zeros_like(db_ref)

        qf[...] = q_ref[...].astype(F32)
        dyf[...] = dy_ref[...].astype(F32)
        geo.fill(kf, kp, kc, kn)
        geo.fill(vf, vp, vc, vn)
        tq = geo.tq
        low_q = _low_lanes(tq)
        bias = b_ref[...].reshape(2 * tq, geo.band)
        for own, around, j in geo.tiles():
            sts = st_ref[geo.own(own), :]
            (lse0, delta0), (lse1, delta1) = _head_stats(sts, 0), _head_stats(sts, 1)
            lse = jnp.concatenate([lse0, lse1], axis=0)
            delta = jnp.concatenate([delta0, delta1], axis=0)
            kb = kf[geo.around(around), :].astype(BF16)
            vb = vf[geo.around(around), :].astype(BF16)
            s = _mm_nt(_two_heads(qf[geo.own(own), :], low_q), kb) + bias + geo.outside(j)
            p = jnp.exp(s - lse)
            ds = p * (_mm_nt(_two_heads(dyf[geo.own(own), :], low_q), vb) - delta)
            db_ref[...] += ds.reshape(2, tq, geo.band)
            dq2 = _mm(ds.astype(BF16), kb)
            dq_ref[geo.own(own), :] = jnp.where(low_q, dq2[:tq], dq2[tq:])

    (dq, dbias), carried = _ride_call(
        body, "swa_bwd_q_%d" % r, (D_B // 128, geo.nblk),
        [cur, prev, cur, nxt, prev, cur, nxt, cur, cur, bias_spec], [cur, bias_spec],
        [jax.ShapeDtypeStruct(q.shape, F32), jax.ShapeDtypeStruct(bias.shape, F32)],
        [geo.plain(), geo.extended(), geo.extended(), geo.plain()], (q, k, k, k, v, v, v, dy, st, bias), ride)
    return dq, dbias, carried


def _swa_bwd_kv(q, k, v, dy, st, bias_kv, r):
    geo = _SwaGeometry(q.shape[0], r)
    prev, cur, nxt = geo.specs()

    def body(k_ref, v_ref, qp, qc, qn, dp_, dc_, dn_, sp, sc, sn, b_ref, dk_ref, dv_ref, kf, vf, qf, dyf, stf):
        kf[...] = k_ref[...].astype(F32)
        vf[...] = v_ref[...].astype(F32)
        geo.fill(qf, qp, qc, qn)
        geo.fill(dyf, dp_, dc_, dn_)
        geo.fill(stf, sp, sc, sn)
        band = geo.band
        low_b = _low_lanes(band)
        bias = jnp.concatenate([b_ref[0], b_ref[1]], axis=1)
        half_lanes = HEAD_DIM // 2
        for own, around, j in geo.tiles():
            ks = kf[geo.own(own), :].astype(BF16)
            vs = vf[geo.own(own), :].astype(BF16)
            q2 = _two_heads(qf[geo.around(around), :], low_b)
            dy2 = _two_heads(dyf[geo.around(around), :], low_b)
            st_t = stf[geo.around(around), :].T
            lse = jnp.concatenate([st_t[:1, :], st_t[HEAD_DIM:HEAD_DIM + 1, :]], axis=1)
            delta = jnp.concatenate([st_t[half_lanes:half_lanes + 1, :],
                                     st_t[HEAD_DIM + half_lanes:HEAD_DIM + half_lanes + 1, :]], axis=1)
            p = jnp.exp(_mm_nt(ks, q2) + bias + (geo.outside(j, copies=2) - lse))
            ds = p * (_mm_nt(vs, dy2) - delta)
            dv_ref[geo.own(own), :] = _mm(p.astype(BF16), dy2)
            dk_ref[geo.own(own), :] = _mm(ds.astype(BF16), q2)

    return pl.pallas_call(
        body, name="swa_bwd_kv_%d" % r, grid=(D_B // 128, geo.nblk),
        in_specs=[cur, cur, prev, cur, nxt, prev, cur, nxt, prev, cur, nxt,
                  pl.BlockSpec((2, geo.tq, geo.band), lambda c, i: (c, 0, 0))],
        out_specs=[cur, cur],
        out_shape=[jax.ShapeDtypeStruct(q.shape, F32), jax.ShapeDtypeStruct(q.shape, F32)],
        scratch_shapes=[geo.plain(), geo.plain(), geo.extended(), geo.extended(), geo.extended()],
        compiler_params=_cparams(("parallel", "parallel")),
    )(k, v, q, q, q, dy, dy, dy, st, st, st, bias_kv)


BIAS_ROWS = 16
BIAS_TN = 4096


def _bias_tiles(onehot, rel_bias_t):
    n = onehot.shape[0]

    def body(oh_ref, rb_ref, o_ref):
        o_ref[...] = sum(_mm_nt(piece, oh_ref[...]) for piece in _split3(rb_ref[...]))

    return pl.pallas_call(
        body, name="bias_tiles", grid=(n // BIAS_TN,),
        in_specs=[_rows(BIAS_TN, 128), _full((BIAS_ROWS, 128))],
        out_specs=pl.BlockSpec((BIAS_ROWS, BIAS_TN), lambda i: (0, i)),
        out_shape=jax.ShapeDtypeStruct((BIAS_ROWS, n), F32),
        compiler_params=_cparams(("parallel",)),
    )(onehot, rel_bias_t)


def _bias_bwd(onehot, dbias_rows, so_far, r):
    n = onehot.shape[0]

    def body(oh, d, prev_ref, g_ref):
        @pl.when(pl.program_id(0) == 0)
        def _():
            g_ref[...] = prev_ref[...]

        hi, lo, _ = _split3(d[...])
        g_ref[...] += _mm(hi, oh[...]) + _mm(lo, oh[...])

    return pl.pallas_call(
        body, name="bias_bwd_%d" % r, grid=(n // BIAS_TN,),
        in_specs=[_rows(BIAS_TN, 128), pl.BlockSpec((BIAS_ROWS, BIAS_TN), lambda i: (0, i)), _full((BIAS_ROWS, 128))],
        out_specs=_full((BIAS_ROWS, 128)),
        out_shape=jax.ShapeDtypeStruct((BIAS_ROWS, 128), F32),
        compiler_params=_cparams(("arbitrary",)),
    )(onehot, dbias_rows, so_far)


LATE = ("w_out", "w_ff1", "w_ff2", "w_ple_gate", "w_ple_proj")


def _local_step(x, p, tgt, w_in, late_shards, g_attn_pre, g_q, g_k, g_out_a, g_out_b, g_attn_post, rel_bias,
                g_mlp_pre, g_mlp_post, g_ple):
    s_len = x.shape[0]
    cc, ss = _rope_tables(s_len)
    gq2 = jnp.concatenate([g_q, g_q], axis=-1)
    gk2 = jnp.concatenate([g_k, g_k], axis=-1)
    ones128 = _group_ones(128)
    rel_bias_t = jnp.zeros((BIAS_ROWS, 128), F32).at[:N_HEADS_B, :N_BUCKETS].set(rel_bias.T)

    xn1, qpre, kpre, qa, kv, qb, kb, vb = _in_proj(x, g_attn_pre, w_in, cc, ss, gq2, gk2, ones128)
    ya, lse_a = _attn_a_fwd(qa, kv)

    tiles, joint = [], None
    for r in DILATIONS:
        tq = SWA_TQ
        onehot = _bucket_onehot(tq, r)
        bias = _bias_tiles(onehot, rel_bias_t)[:N_HEADS_B].reshape(N_HEADS_B, tq, tq + 2 * HALF_WIN)
        bias = jnp.where(_in_window(tq), bias, NEG_BIG)
        yb, lse_b, gathered = _swa_fwd(qb, kb, vb, bias, r, joint,
                                       _GatherRide(late_shards) if r == DILATIONS[-1] else None)
        tiles.append((onehot, bias))
        joint = (yb, lse_b)
    w_out, w_ff1, w_ff2, w_gate, w_ple = (_whole(n, g, mine) for n, g, mine in zip(LATE, gathered, late_shards))

    ycat, y2, h1, xn2 = _out_proj(ya, yb, x, g_out_a, g_out_b, w_out, g_attn_post, g_mlp_pre)
    u = _ff1(xn2, w_ff1)
    f2, h2, xn3 = _ff2(u, w_ff2, h1, g_mlp_post, g_ple)
    dh2, df2, dgl, dpp, loss, dg_ple, dg_mlp_post = _ple_loss(xn3, p, h2, f2, tgt, w_gate, w_ple, g_ple, g_mlp_post)

    grads = {"g_ple": dg_ple, "g_mlp_post": dg_mlp_post}
    grads["w_ple_gate"] = _dw(xn3, dgl, "dw_gate")
    grads["w_ple_proj"] = _dw(p, dpp, "dw_ple")
    grads["w_ff2"] = _dw(u, df2, "dw_ff2", relu2=True)
    du = _ff2_bwd(df2, w_ff2, u)
    grads["w_ff1"] = _dw(xn2, du, "dw_ff1")
    dh1, dy2, grads["g_mlp_pre"], grads["g_attn_post"] = _ff1_bwd(du, w_ff1, dh2, h1, y2, g_mlp_pre, g_attn_post)
    grads["w_out"] = _dw(ycat, dy2, "dw_out")
    dya, dyb, delta_a, st_b, grads["g_out_a"], grads["g_out_b"] = _out_proj_bwd(dy2, w_out, ya, yb, lse_b, g_out_a,
                                                                              g_out_b)

    pairs = _pair_sums(LATE, [grads[n] for n in LATE])

    dqr, dkv_t = _attn_a_bwd(qa, dya, kv, lse_a, delta_a)
    dkv_a = dkv_t.transpose(0, 2, 1)

    dqs, dks, dvs = [], [], []
    d_rel = jnp.zeros((BIAS_ROWS, 128), F32)
    for r, (onehot, bias) in zip(DILATIONS, tiles):
        dq_r, dbias, scattered = _swa_bwd_q(qb, kb, vb, dyb, st_b, bias, r,
                                            _ScatterRide(pairs) if r == DILATIONS[0] else None)
        if scattered:
            for n, half in zip(LATE, _chip_sums(LATE, pairs, scattered)):
                grads[n] = half
        bias_kv = jnp.flip(bias, axis=(1, 2))
        dk_r, dv_r = _swa_bwd_kv(qb, kb, vb, dyb, st_b, bias_kv, r)
        dbias_rows = jnp.pad(dbias.reshape(N_HEADS_B, -1), ((0, BIAS_ROWS - N_HEADS_B), (0, 0)))
        d_rel = _bias_bwd(onehot, dbias_rows, d_rel, r)
        dqs.append(dq_r)
        dks.append(dk_r)
        dvs.append(dv_r)
    grads["rel_bias"] = d_rel[:N_HEADS_B, :N_BUCKETS].T

    dproj, grad_x, grads["g_attn_pre"], dgq2, dgk2 = _in_proj_bwd(
        dqr, dkv_a, dqs, dks, dvs, qpre, kpre, x, dh1, g_attn_pre, w_in, cc, ss, gq2, gk2, ones128)
    grads["g_q"] = dgq2[:, :HEAD_DIM] + dgq2[:, HEAD_DIM:]
    grads["g_k"] = dgk2[:, :HEAD_DIM] + dgk2[:, HEAD_DIM:]
    grads["w_in"] = _dw(xn1, dproj, "dw_in")
    return loss, grad_x, grads


ANY = pl.BlockSpec(memory_space=pl.ANY)


def _position():
    return lax.axis_index("x"), lax.axis_index("y"), lax.axis_index("c")


def _other_chips(x, y):
    return [(2 * (1 - x) + y, (1 - x, y)), (2 * x + (1 - y), (x, 1 - y)), (2 * (1 - x) + (1 - y), (1 - x, 1 - y))]


def _cast_shards(shards):
    def body(*refs):
        n = len(refs) // 2
        for i_ref, o_ref in zip(refs[:n], refs[n:]):
            o_ref[...] = i_ref[...].astype(BF16)

    return pl.pallas_call(
        body, name="cast_shards",
        in_specs=[pl.BlockSpec(memory_space=pltpu.VMEM)] * len(shards),
        out_specs=[pl.BlockSpec(memory_space=pltpu.VMEM)] * len(shards),
        out_shape=[jax.ShapeDtypeStruct(s.shape, BF16) for s in shards],
        compiler_params=_cparams(),
    )(*shards)


def _gather_weights(shards):
    n = len(shards)

    ride = _GatherRide(shards)

    def body(*refs):
        ride.start(refs[:n], refs[n:2 * n], refs[2 * n:])
        ride.finish(refs[:n], refs[n:2 * n], refs[2 * n:])

    return pl.pallas_call(
        body, name="gather_weights",
        in_specs=[ANY] * n, out_specs=[ANY] * n,
        out_shape=ride.out_shape(), scratch_shapes=ride.scratch_shapes(),
    )(*shards)


class _GatherRide:
    def __init__(self, shards):
        self.operands = list(shards)
        self.n = len(shards)

    def out_shape(self):
        return [jax.ShapeDtypeStruct((N_CHIPS,) + s.shape, s.dtype) for s in self.operands]

    def scratch_shapes(self):
        return [pltpu.SemaphoreType.DMA((3, self.n))] * 4

    @staticmethod
    def _rows(ref, core):
        half = ref.shape[0] // 2
        return pl.ds(pl.multiple_of(core * half, 16), half)

    def _ici(self, ins, outs, sems, k, a, chip):
        x, y, c = _position()
        return pltpu.make_async_remote_copy(ins[a].at[self._rows(ins[a], c), :],
                                            outs[a].at[2 * x + y, self._rows(ins[a], c), :], sems[0].at[k, a],
                                            sems[1].at[k, a], device_id=(*chip, c), device_id_type=MESH)

    def _pass_on(self, ins, outs, sems, k, a, num, core):
        x, y, c = _position()
        half = outs[a].at[num, self._rows(ins[a], core), :]
        return pltpu.make_async_remote_copy(half, half, sems[2].at[k, a], sems[3].at[k, a], device_id=(x, y, 1 - c),
                                            device_id_type=MESH)

    def start(self, ins, outs, sems):
        x, y, _ = _position()
        for k, (_, chip) in enumerate(_other_chips(x, y)):
            for a in range(self.n):
                self._ici(ins, outs, sems, k, a, chip).start()

    def finish(self, ins, outs, sems):
        x, y, c = _position()
        others = _other_chips(x, y)
        for k, (num, chip) in enumerate(others):
            for a in range(self.n):
                landed = outs[a].at[num, self._rows(ins[a], c), :]
                pltpu.make_async_remote_copy(landed, landed, sems[0].at[k, a], sems[1].at[k, a], device_id=(*chip, c),
                                             device_id_type=MESH).wait_recv()
                self._pass_on(ins, outs, sems, k, a, num, c).start()
        for k, (num, chip) in enumerate(others):
            for a in range(self.n):
                self._pass_on(ins, outs, sems, k, a, num, 1 - c).wait_recv()
        for k, (num, chip) in enumerate(others):
            for a in range(self.n):
                self._ici(ins, outs, sems, k, a, chip).wait_send()
                self._pass_on(ins, outs, sems, k, a, num, c).wait_send()


def _send_sibling_half(grads, tag):
    n = len(grads)

    def body(*refs):
        ins, outs = refs[:n], refs[n:2 * n]
        send_sems, recv_sems = refs[2 * n:]
        x, y, c = _position()
        copies = []
        for a in range(n):
            half = ins[a].shape[1] // 2
            theirs = ins[a].at[:, pl.ds(pl.multiple_of((1 - c) * half, 8), half), :]
            cp = pltpu.make_async_remote_copy(theirs, outs[a], send_sems.at[a], recv_sems.at[a],
                                              device_id=(x, y, 1 - c), device_id_type=MESH)
            cp.start()
            copies.append(cp)
        for cp in copies:
            cp.wait()

    return pl.pallas_call(
        body, name="send_sibling_half_" + tag,
        in_specs=[ANY] * n, out_specs=[ANY] * n,
        out_shape=[jax.ShapeDtypeStruct((g.shape[0], g.shape[1] // 2, g.shape[2]), g.dtype) for g in grads],
        scratch_shapes=[pltpu.SemaphoreType.DMA((n,)), pltpu.SemaphoreType.DMA((n,))],
    )(*grads)


def _scatter_to_chips(pairs):
    n = len(pairs)
    ride = _ScatterRide(pairs)

    def body(*refs):
        ride.start(refs[:n], refs[n:2 * n], refs[2 * n:])
        ride.finish(refs[:n], refs[n:2 * n], refs[2 * n:])

    return pl.pallas_call(
        body, name="scatter_to_chips",
        in_specs=[ANY] * n, out_specs=[ANY] * n,
        out_shape=ride.out_shape(), scratch_shapes=ride.scratch_shapes(),
    )(*pairs)


class _ScatterRide:
    def __init__(self, pairs):
        self.operands = list(pairs)
        self.n = len(pairs)

    def out_shape(self):
        return [jax.ShapeDtypeStruct(g.shape, g.dtype) for g in self.operands]

    def scratch_shapes(self):
        return [pltpu.SemaphoreType.DMA((3, self.n))] * 2

    @staticmethod
    def _copy(ins, outs, sems, k, a, src_slot, dst_slot, chip):
        _, _, c = _position()
        return pltpu.make_async_remote_copy(ins[a].at[src_slot], outs[a].at[dst_slot], sems[0].at[k, a],
                                            sems[1].at[k, a], device_id=(*chip, c), device_id_type=MESH)

    def start(self, ins, outs, sems):
        x, y, _ = _position()
        for k, (num, chip) in enumerate(_other_chips(x, y)):
            for a in range(self.n):
                self._copy(ins, outs, sems, k, a, num, 2 * x + y, chip).start()

    def finish(self, ins, outs, sems):
        x, y, _ = _position()
        for k, (num, chip) in enumerate(_other_chips(x, y)):
            for a in range(self.n):
                self._copy(ins, outs, sems, k, a, 2 * x + y, num, chip).wait_recv()
        for k, (num, chip) in enumerate(_other_chips(x, y)):
            for a in range(self.n):
                self._copy(ins, outs, sems, k, a, num, 2 * x + y, chip).wait_send()


def _exchange_halves(halves):
    n = len(halves)

    def body(*refs):
        ins, outs = refs[:n], refs[n:2 * n]
        send_sems, recv_sems = refs[2 * n:]
        x, y, c = _position()
        copies = []
        for a in range(n):
            cp = pltpu.make_async_remote_copy(ins[a], outs[a], send_sems.at[a], recv_sems.at[a],
                                              device_id=(x, y, 1 - c), device_id_type=MESH)
            cp.start()
            copies.append(cp)
        for cp in copies:
            cp.wait()

    return pl.pallas_call(
        body, name="exchange_halves",
        in_specs=[ANY] * n, out_specs=[ANY] * n,
        out_shape=[jax.ShapeDtypeStruct(h.shape, h.dtype) for h in halves],
        scratch_shapes=[pltpu.SemaphoreType.DMA((n,)), pltpu.SemaphoreType.DMA((n,))],
    )(*halves)


def _allreduce_small(v):
    def body(v_ref, o_ref, buf, send_sems, recv_sems):
        x, y, c = _position()
        me = 4 * x + 2 * y + c
        peers = [(1 - x, y, c), (x, 1 - y, c), (x, y, 1 - c), (1 - x, 1 - y, c), (1 - x, y, 1 - c), (x, 1 - y, 1 - c),
                 (1 - x, 1 - y, 1 - c)]
        num = lambda d: 4 * d[0] + 2 * d[1] + d[2]
        buf[me] = v_ref[...]
        sends = []
        for k, peer in enumerate(peers):
            cp = pltpu.make_async_remote_copy(v_ref, buf.at[me], send_sems.at[k], recv_sems.at[k], device_id=peer,
                                              device_id_type=MESH)
            cp.start()
            sends.append(cp)
        for k, peer in enumerate(peers):
            pltpu.make_async_remote_copy(v_ref, buf.at[num(peer)], send_sems.at[k], recv_sems.at[k], device_id=peer,
                                         device_id_type=MESH).wait_recv()
        for cp in sends:
            cp.wait_send()
        total = buf[0]
        for d in range(1, 8):
            total = total + buf[d]
        o_ref[...] = total

    return pl.pallas_call(
        body, name="allreduce_small",
        in_specs=[pl.BlockSpec(memory_space=pltpu.VMEM)], out_specs=pl.BlockSpec(memory_space=pltpu.VMEM),
        out_shape=jax.ShapeDtypeStruct(v.shape, v.dtype),
        scratch_shapes=[pltpu.VMEM((8,) + v.shape, v.dtype), pltpu.SemaphoreType.DMA((7,)),
                        pltpu.SemaphoreType.DMA((7,))],
    )(v)


def _sum_leading(a, name):
    k, r, c = a.shape
    tr = min(r, 256)

    def body(a_ref, o_ref):
        total = a_ref[0].astype(F32)
        for i in range(1, k):
            total = total + a_ref[i].astype(F32)
        o_ref[...] = total

    return pl.pallas_call(
        body, name=name, grid=(r // tr,),
        in_specs=[pl.BlockSpec((k, tr, c), lambda i: (0, i, 0))],
        out_specs=pl.BlockSpec((tr, c), lambda i: (i, 0)),
        out_shape=jax.ShapeDtypeStruct((r, c), F32),
        compiler_params=_cparams(("parallel",)),
    )(a)


def _add(a, b, name):
    k, r, c = a.shape
    tr = min(r, 256)
    spec = pl.BlockSpec((k, tr, c), lambda i: (0, i, 0))

    def body(a_ref, b_ref, o_ref):
        o_ref[...] = (a_ref[...] + b_ref[...]).astype(BF16)

    return pl.pallas_call(
        body, name=name, grid=(r // tr,), in_specs=[spec, spec], out_specs=spec,
        out_shape=jax.ShapeDtypeStruct(a.shape, BF16), compiler_params=_cparams(("parallel",)),
    )(a, b)


def _adamw(w, g, m, v, name):
    r, c = w.shape
    tr = min(r, 256)
    spec = pl.BlockSpec((tr, c), lambda i: (i, 0))

    def body(w_ref, g_ref, m_ref, v_ref, d_ref, nm_ref, nv_ref):
        gv = g_ref[...]
        nm = ADAM_B1 * m_ref[...] + (1.0 - ADAM_B1) * gv
        nv = ADAM_B2 * v_ref[...] + (1.0 - ADAM_B2) * jnp.square(gv)
        m_hat = nm / (1.0 - ADAM_B1 ** ADAM_STEP)
        v_hat = nv / (1.0 - ADAM_B2 ** ADAM_STEP)
        d_ref[...] = -ADAM_LR * (m_hat / (jnp.sqrt(v_hat) + ADAM_EPS) + ADAM_WD * w_ref[...])
        nm_ref[...] = nm
        nv_ref[...] = nv

    return pl.pallas_call(
        body, name=name, grid=(r // tr,), in_specs=[spec] * 4, out_specs=[spec] * 3,
        out_shape=[jax.ShapeDtypeStruct(w.shape, F32)] * 3, compiler_params=_cparams(("parallel",)),
    )(w, g, m, v)


MATRICES = ("w_in", "w_out", "w_ff1", "w_ff2", "w_ple_gate", "w_ple_proj")
COLUMN_SHARDED = ("w_in", "w_ff1", "w_ple_proj")
SMALL = ("g_attn_pre", "g_q", "g_k", "g_out_a", "g_out_b", "g_attn_post", "rel_bias", "g_mlp_pre", "g_mlp_post",
         "g_ple")
WEIGHT_ORDER = ("w_in", "g_attn_pre", "g_q", "g_k", "g_out_a", "g_out_b", "w_out", "g_attn_post", "rel_bias",
                "g_mlp_pre", "w_ff1", "w_ff2", "g_mlp_post", "g_ple", "w_ple_gate", "w_ple_proj")
PACK_ROWS, PACK_COLS = 8, 1024


def _chip():
    return 2 * lax.axis_index("x") + lax.axis_index("y")


def _whole(name, gathered, mine):
    g = lax.dynamic_update_slice_in_dim(gathered, mine[None], _chip(), axis=0)
    if name in COLUMN_SHARDED:
        return g.transpose(1, 0, 2).reshape(g.shape[1], N_CHIPS * g.shape[2])
    return g.reshape(N_CHIPS * g.shape[1], g.shape[2])


def _pair_sums(names, grads):
    by_chip = []
    for n, g in zip(names, grads):
        if n in COLUMN_SHARDED:
            by_chip.append(g.reshape(g.shape[0], N_CHIPS, g.shape[1] // N_CHIPS).transpose(1, 0, 2))
        else:
            by_chip.append(g.reshape(N_CHIPS, g.shape[0] // N_CHIPS, g.shape[1]))
    c = lax.axis_index("c")
    pairs = []
    for n, g, other in zip(names, by_chip, _send_sibling_half(by_chip, names[0])):
        half = g.shape[1] // 2
        pairs.append(_add(lax.dynamic_slice_in_dim(g, c * half, half, axis=1), other, "pair_sum_" + n))
    return pairs


def _chip_sums(names, pairs, scattered):
    halves = []
    for n, pair, got in zip(names, pairs, scattered):
        own = lax.dynamic_slice_in_dim(pair, _chip(), 1, axis=0)
        halves.append(_sum_leading(lax.dynamic_update_slice_in_dim(got, own, _chip(), axis=0), "chip_sum_" + n))
    return halves


def _pack_small(values, extra=None):
    flat = [values[n].reshape(-1) for n in SMALL]
    used = sum(f.shape[0] for f in flat)
    tail = jnp.zeros((PACK_ROWS * PACK_COLS - used - 1,), F32)
    last = jnp.zeros((1,), F32) if extra is None else extra.reshape(1)
    return jnp.concatenate(flat + [tail, last]).reshape(PACK_ROWS, PACK_COLS)


def _unpack_small(packed, like):
    flat = packed.reshape(-1)
    out, o = {}, 0
    for n in SMALL:
        size = like[n].size
        out[n] = flat[o:o + size].reshape(like[n].shape)
        o += size
    return out, flat[-1]


def kernel(x, p, w_in, g_attn_pre, g_q, g_k, g_out_a, g_out_b, w_out, g_attn_post, rel_bias, g_mlp_pre, w_ff1, w_ff2, g_mlp_post, g_ple, w_ple_gate, w_ple_proj, loss_target, m_w_in, m_g_attn_pre, m_g_q, m_g_k, m_g_out_a, m_g_out_b, m_w_out, m_g_attn_post, m_rel_bias, m_g_mlp_pre, m_w_ff1, m_w_ff2, m_g_mlp_post, m_g_ple, m_w_ple_gate, m_w_ple_proj, v_w_in, v_g_attn_pre, v_g_q, v_g_k, v_g_out_a, v_g_out_b, v_w_out, v_g_attn_post, v_rel_bias, v_g_mlp_pre, v_w_ff1, v_w_ff2, v_g_mlp_post, v_g_ple, v_w_ple_gate, v_w_ple_proj):
    given = dict(locals())
    weights = {n: given[n] for n in WEIGHT_ORDER}
    shards = {n: weights[n][0] for n in MATRICES}

    c = lax.axis_index("c")
    own = dict(zip(MATRICES, _cast_shards([shards[n] for n in MATRICES])))
    w_in_whole = _whole("w_in", _gather_weights([own["w_in"]])[0], own["w_in"])

    loss, grad_x, grads = _local_step(
        x[0], p[0, 0], loss_target[0], w_in_whole, [own[n] for n in LATE], g_attn_pre, g_q, g_k, g_out_a, g_out_b,
        g_attn_post, rel_bias, g_mlp_pre, g_mlp_post, g_ple)

    pairs = _pair_sums(["w_in"], [grads["w_in"]])
    grads["w_in"] = _chip_sums(["w_in"], pairs, _scatter_to_chips(pairs))[0]
    halves = [grads[n] for n in MATRICES]
    grad_w = {}
    for n, mine, theirs in zip(MATRICES, halves, _exchange_halves(halves)):
        half = mine.shape[0]
        g = jnp.zeros((2 * half, mine.shape[1]), F32)
        g = lax.dynamic_update_slice_in_dim(g, mine, c * half, axis=0)
        grad_w[n] = lax.dynamic_update_slice_in_dim(g, theirs, (1 - c) * half, axis=0)

    small_like = {n: weights[n] for n in SMALL}
    reduced = _allreduce_small(_pack_small({n: grads[n] for n in SMALL}, extra=loss))
    grad_small, loss_total = _unpack_small(reduced, small_like)

    delta, new_m, new_v = {}, {}, {}
    for n in MATRICES:
        d, nm, nv = _adamw(shards[n], grad_w[n], given["m_" + n][0], given["v_" + n][0], "adamw_" + n)
        delta[n], new_m[n], new_v[n] = d[None], nm[None], nv[None]
        grad_w[n] = grad_w[n][None]
    d, nm, nv = _adamw(_pack_small(small_like), reduced, _pack_small({n: given["m_" + n] for n in SMALL}),
                       _pack_small({n: given["v_" + n] for n in SMALL}), "adamw_small")
    d_small, nm_small, nv_small = (_unpack_small(a, small_like)[0] for a in (d, nm, nv))
    for n in SMALL:
        grad_w[n], delta[n], new_m[n], new_v[n] = grad_small[n], d_small[n], nm_small[n], nv_small[n]

    return (loss_total, grad_x[None], *[grad_w[n] for n in WEIGHT_ORDER], *[delta[n] for n in WEIGHT_ORDER],
            *[new_m[n] for n in WEIGHT_ORDER], *[new_v[n] for n in WEIGHT_ORDER])
```

```python
import functools
import math

import jax
import jax.numpy as jnp
from jax import lax
from jax.experimental import pallas as pl
from jax.experimental.pallas import tpu as pltpu

F32 = jnp.float32
BF16 = jnp.bfloat16

D_MODEL = 1024
HEAD_DIM = 64
N_HEADS_A = 8
N_KV_A = 2
GROUP_A = N_HEADS_A // N_KV_A
N_HEADS_B = 8
D_A = N_HEADS_A * HEAD_DIM
D_KV_A = N_KV_A * HEAD_DIM
D_B = N_HEADS_B * HEAD_DIM
D_IN = D_A + 2 * D_KV_A + 3 * D_B
D_FF = 4 * D_MODEL
D_PLE = 256
GRID_W = 64
ROPE_THETA = 10000.0
DILATIONS = (1, 4, 16)
HALF_WIN = 64
N_BUCKETS = 32
MAX_DISTANCE = 1024
EPS = 1e-6
NEG_BIG = -1e30
Q_SCALE = HEAD_DIM ** -0.5

ADAM_LR = 0.001
ADAM_B1 = 0.9
ADAM_B2 = 0.999
ADAM_EPS = 1e-08
ADAM_WD = 0.01
ADAM_STEP = 10

N_CHIPS = 4
MESH = pl.DeviceIdType.MESH

ROW_TILE = 512
ATT_TQ = 256
ATT_TQ_BWD = 512
ATT_TK_FWD = 2048
ATT_UNROLL_FWD = 8
ATT_TK_BWD = 512
ATT_UNROLL_BWD = 8
SWA_TQ = 128
SWA_MIN_BLOCK = 1024
DW_TS = 2048
VMEM_LIMIT = 56 * 1024 * 1024

NT = (((1,), (1,)), ((), ()))
TN = (((0,), (0,)), ((), ()))


def _cparams(sem=None, vmem=VMEM_LIMIT):
    return pltpu.CompilerParams(dimension_semantics=sem, vmem_limit_bytes=vmem)


def _full(shape):
    n = len(shape)
    return pl.BlockSpec(shape, lambda *_: (0,) * n)


def _rows(tm, width):
    return pl.BlockSpec((tm, width), lambda i: (i, 0))


def _split3(a):
    a1 = a.astype(BF16)
    r = a - a1.astype(F32)
    a2 = r.astype(BF16)
    a3 = (r - a2.astype(F32)).astype(BF16)
    return a1, a2, a3


def _xdot(a, sel):
    a1, a2, a3 = _split3(a)
    d = lambda p: jnp.dot(p, sel, preferred_element_type=F32)
    return d(a1) + d(a2) + d(a3)


def _mm(a, b):
    return jnp.dot(a, b, preferred_element_type=F32)


def _mm_nt(a, b):
    return lax.dot_general(a, b, NT, preferred_element_type=F32)


def _mm_tn(a, b):
    return lax.dot_general(a, b, TN, preferred_element_type=F32)


def _rms_stats(x):
    r = lax.rsqrt(jnp.mean(x * x, axis=-1, keepdims=True) + EPS)
    return x * r, r


def _rms_bwd(dy, xh, r, g):
    gdy = dy * g
    dx = r * (gdy - xh * jnp.mean(gdy * xh, axis=-1, keepdims=True))
    dg = jnp.sum(dy * xh, axis=0, keepdims=True)
    return dx, dg


def _acc_out(ref, val):
    @pl.when(pl.program_id(0) == 0)
    def _():
        ref[...] = jnp.zeros_like(ref)

    ref[...] += val


def _swap_halves(x, first_half):
    return jnp.where(first_half, pltpu.roll(x, 96, 1), pltpu.roll(x, 32, 1))


def _first_half_mask(shape):
    return (lax.broadcasted_iota(jnp.int32, shape, 1) % HEAD_DIM) < (HEAD_DIM // 2)


def _rope_tables(s_len):
    t = jnp.arange(s_len)
    row = (t // GRID_W).astype(F32)
    col = (t % GRID_W).astype(F32)
    n_axis = HEAD_DIM // 4
    inv_freq = ROPE_THETA ** (-jnp.arange(n_axis, dtype=F32) / n_axis)
    ang = jnp.concatenate([row[:, None] * inv_freq, col[:, None] * inv_freq], axis=-1)
    c, s = jnp.cos(ang), jnp.sin(ang)
    cc = jnp.concatenate([c, c, c, c], axis=-1)
    ss = jnp.concatenate([-s, s, -s, s], axis=-1)
    return cc, ss


def _group_ones(width):
    i = jnp.arange(width)
    return (i[:, None] // HEAD_DIM == i[None, :] // HEAD_DIM).astype(BF16)


def _t5_bucket(rel):
    nb = N_BUCKETS // 2
    max_exact = nb // 2
    side = jnp.where(rel > 0, nb, 0)
    n = jnp.abs(rel)
    large = max_exact + (jnp.log(jnp.maximum(n, max_exact).astype(F32) / max_exact)
                         / math.log(MAX_DISTANCE / max_exact) * (nb - max_exact)).astype(jnp.int32)
    large = jnp.minimum(large, nb - 1)
    return side + jnp.where(n < max_exact, n, large)


def _in_window(tq):
    qi = jnp.arange(tq)
    kj = jnp.arange(tq + 2 * HALF_WIN)
    return jnp.abs(kj[None, :] - HALF_WIN - qi[:, None]) <= HALF_WIN


def _bucket_onehot(tq, dilation):
    qi = jnp.arange(tq)
    kj = jnp.arange(tq + 2 * HALF_WIN)
    rel = kj[None, :] - HALF_WIN - qi[:, None]
    bucket = _t5_bucket(rel * dilation).reshape(-1)
    return (bucket[:, None] == jnp.arange(128)[None, :]).astype(BF16)


def _in_proj(x, g1, w_in, cc, ss, gq2, gk2, ones128):
    s_len = x.shape[0]
    tm = min(ROW_TILE, s_len)

    def body(x_ref, g_ref, w_ref, cc_ref, ss_ref, gq_ref, gk_ref, one_ref,
             xn_ref, qpre_ref, kpre_ref, qa_ref, kv_ref, qb_ref, kb_ref, vb_ref):
        xh, _ = _rms_stats(x_ref[...])
        xn = (xh * g_ref[...]).astype(BF16)
        xn_ref[...] = xn
        proj = _mm(xn, w_ref[...])
        first_half = _first_half_mask((tm, 128))
        ones = one_ref[...]
        cc_t, ss_t = cc_ref[...], ss_ref[...]

        def norm_rope(xc, gain):
            ms = _xdot(xc * xc, ones) * (1.0 / HEAD_DIM)
            y = xc * lax.rsqrt(ms + EPS) * gain
            return y * cc_t + _swap_halves(y, first_half) * ss_t

        qpre_ref[...] = proj[:, :D_A]
        kpre_ref[...] = proj[:, D_A:D_A + D_KV_A]
        for c in range(D_A // 128):
            y = norm_rope(proj[:, 128 * c:128 * (c + 1)], gq_ref[...])
            qa_ref[:, 128 * c:128 * (c + 1)] = (y * Q_SCALE).astype(BF16)
        ka = norm_rope(proj[:, D_A:D_A + D_KV_A], gk_ref[...])
        o = D_A + D_KV_A
        va = proj[:, o:o + D_KV_A]
        low = _low_lanes(tm)
        kv_ref[0] = jnp.where(low, ka, pltpu.roll(va, HEAD_DIM, 1)).astype(BF16)
        kv_ref[1] = jnp.where(low, pltpu.roll(ka, HEAD_DIM, 1), va).astype(BF16)
        o += D_KV_A
        qb_ref[...] = (proj[:, o:o + D_B] * Q_SCALE).astype(BF16)
        kb_ref[...] = proj[:, o + D_B:o + 2 * D_B].astype(BF16)
        vb_ref[...] = proj[:, o + 2 * D_B:o + 3 * D_B].astype(BF16)

    sds = jax.ShapeDtypeStruct
    return pl.pallas_call(
        body, name="in_proj", grid=(s_len // tm,),
        in_specs=[_rows(tm, D_MODEL), _full((1, D_MODEL)), _full((D_MODEL, D_IN)), _rows(tm, 128), _rows(tm, 128),
                  _full((1, 128)), _full((1, 128)), _full((128, 128))],
        out_specs=[_rows(tm, D_MODEL), _rows(tm, D_A), _rows(tm, D_KV_A), _rows(tm, D_A),
                   pl.BlockSpec((N_KV_A, tm, 128), lambda i: (0, i, 0)), _rows(tm, D_B), _rows(tm, D_B),
                   _rows(tm, D_B)],
        out_shape=[sds((s_len, D_MODEL), BF16), sds((s_len, D_A), F32), sds((s_len, D_KV_A), F32),
                   sds((s_len, D_A), BF16), sds((N_KV_A, s_len, 128), BF16),
                   sds((s_len, D_B), BF16), sds((s_len, D_B), BF16), sds((s_len, D_B), BF16)],
        compiler_params=_cparams(("parallel",)),
    )(x, g1, w_in, cc, ss, gq2, gk2, ones128)


def _stat_spec(tm):
    return pl.BlockSpec((N_HEADS_B, tm, 1), lambda i: (0, i, 0))


def _out_proj(ya, yb, x, g_a, g_b, w_out, g_post, g_mlp_pre):
    s_len = x.shape[0]
    tm = min(ROW_TILE, s_len)

    def body(ya_ref, yb_ref, x_ref, ga_ref, gb_ref, w_ref, gp_ref, gm_ref, ycat_ref, y2_ref, h1_ref, xn2_ref):
        ah, _ = _rms_stats(ya_ref[...])
        bh, _ = _rms_stats(yb_ref[...])
        ycat = jnp.concatenate([ah * ga_ref[...], bh * gb_ref[...]], axis=-1).astype(BF16)
        ycat_ref[...] = ycat
        y2 = _mm(ycat, w_ref[...])
        y2_ref[...] = y2
        y2h, _ = _rms_stats(y2)
        h1 = x_ref[...] + y2h * gp_ref[...]
        h1_ref[...] = h1
        h1h, _ = _rms_stats(h1)
        xn2_ref[...] = (h1h * gm_ref[...]).astype(BF16)

    sds = jax.ShapeDtypeStruct
    return pl.pallas_call(
        body, name="out_proj", grid=(s_len // tm,),
        in_specs=[_rows(tm, D_A), _rows(tm, D_B), _rows(tm, D_MODEL), _full((1, D_A)), _full((1, D_B)),
                  _full((D_MODEL, D_MODEL)), _full((1, D_MODEL)), _full((1, D_MODEL))],
        out_specs=[_rows(tm, D_MODEL)] * 4,
        out_shape=[sds((s_len, D_MODEL), BF16), sds((s_len, D_MODEL), F32), sds((s_len, D_MODEL), F32),
                   sds((s_len, D_MODEL), BF16)],
        compiler_params=_cparams(("parallel",)),
    )(ya, yb, x, g_a, g_b, w_out, g_post, g_mlp_pre)


def _ff1(xn2, w_ff1):
    s_len = xn2.shape[0]
    tm = min(ROW_TILE, s_len)

    def body(x_ref, w_ref, u_ref):
        u_ref[...] = _mm(x_ref[...], w_ref[...])

    return pl.pallas_call(
        body, name="ff1", grid=(s_len // tm,),
        in_specs=[_rows(tm, D_MODEL), _full((D_MODEL, D_FF))],
        out_specs=_rows(tm, D_FF),
        out_shape=jax.ShapeDtypeStruct((s_len, D_FF), F32),
        compiler_params=_cparams(("parallel",)),
    )(xn2, w_ff1)


def _ff2(u, w_ff2, h1, g_post, g_ple):
    s_len = u.shape[0]
    tm = min(ROW_TILE, s_len)

    def body(u_ref, w_ref, h1_ref, gp_ref, gl_ref, f2_ref, h2_ref, xn3_ref):
        f = jnp.square(jnp.maximum(u_ref[...], 0.0)).astype(BF16)
        f2 = _mm(f, w_ref[...])
        f2_ref[...] = f2
        f2h, _ = _rms_stats(f2)
        h2 = h1_ref[...] + f2h * gp_ref[...]
        h2_ref[...] = h2
        h2h, _ = _rms_stats(h2)
        xn3_ref[...] = (h2h * gl_ref[...]).astype(BF16)

    sds = jax.ShapeDtypeStruct
    return pl.pallas_call(
        body, name="ff2", grid=(s_len // tm,),
        in_specs=[_rows(tm, D_FF), _full((D_FF, D_MODEL)), _rows(tm, D_MODEL), _full((1, D_MODEL)),
                  _full((1, D_MODEL))],
        out_specs=[_rows(tm, D_MODEL)] * 3,
        out_shape=[sds((s_len, D_MODEL), F32), sds((s_len, D_MODEL), F32), sds((s_len, D_MODEL), BF16)],
        compiler_params=_cparams(("parallel",)),
    )(u, w_ff2, h1, g_post, g_ple)


def _ple_loss(xn3, p, h2, f2, tgt, w_gate, w_ple, g_ple, g_mlp_post):
    s_len = h2.shape[0]
    tm = min(ROW_TILE, s_len)

    def body(xn3_ref, p_ref, h2_ref, f2_ref, t_ref, wg_ref, wp_ref, gl_ref, gp_ref,
             dh2_ref, df2_ref, dgl_ref, dpp_ref, loss_ref, dgple_ref, dgpost_ref):
        gate = jax.nn.sigmoid(_mm(xn3_ref[...], wg_ref[...]))
        pp = _mm(p_ref[...].astype(BF16), wp_ref[...])
        h2 = h2_ref[...]
        err = h2 + gate * pp - t_ref[...]
        sq = jnp.sum(jnp.sum(err * err, axis=1, keepdims=True), axis=0, keepdims=True)
        _acc_out(loss_ref, sq * (0.5 / D_MODEL))
        dh3 = err * (1.0 / D_MODEL)
        dgl = (dh3 * pp) * gate * (1.0 - gate)
        dgl_b = dgl.astype(BF16)
        dgl_ref[...] = dgl_b
        dpp_ref[...] = (dh3 * gate).astype(BF16)
        dxn3 = _mm_nt(dgl_b, wg_ref[...])
        h2h, r2 = _rms_stats(h2)
        dx, dg = _rms_bwd(dxn3, h2h, r2, gl_ref[...])
        _acc_out(dgple_ref, dg)
        dh2 = dh3 + dx
        dh2_ref[...] = dh2
        f2h, rf = _rms_stats(f2_ref[...])
        df2, dg = _rms_bwd(dh2, f2h, rf, gp_ref[...])
        _acc_out(dgpost_ref, dg)
        df2_ref[...] = df2.astype(BF16)

    sds = jax.ShapeDtypeStruct
    return pl.pallas_call(
        body, name="ple_loss", grid=(s_len // tm,),
        in_specs=[_rows(tm, D_MODEL), _rows(tm, D_PLE), _rows(tm, D_MODEL), _rows(tm, D_MODEL), _rows(tm, D_MODEL),
                  _full((D_MODEL, D_MODEL)), _full((D_PLE, D_MODEL)), _full((1, D_MODEL)), _full((1, D_MODEL))],
        out_specs=[_rows(tm, D_MODEL)] * 3 + [_rows(tm, D_MODEL), _full((1, 1)), _full((1, D_MODEL)),
                                              _full((1, D_MODEL))],
        out_shape=[sds((s_len, D_MODEL), F32), sds((s_len, D_MODEL), BF16), sds((s_len, D_MODEL), BF16),
                   sds((s_len, D_MODEL), BF16), sds((1, 1), F32), sds((1, D_MODEL), F32), sds((1, D_MODEL), F32)],
        compiler_params=_cparams(("arbitrary",)),
    )(xn3, p, h2, f2, tgt, w_gate, w_ple, g_ple, g_mlp_post)


def _ff2_bwd(df2, w_ff2, u):
    s_len = u.shape[0]
    tm = min(ROW_TILE, s_len)

    def body(d_ref, w_ref, u_ref, du_ref):
        df = _mm_nt(d_ref[...], w_ref[...])
        du_ref[...] = (df * (2.0 * jnp.maximum(u_ref[...], 0.0))).astype(BF16)

    return pl.pallas_call(
        body, name="ff2_bwd", grid=(s_len // tm,),
        in_specs=[_rows(tm, D_MODEL), _full((D_FF, D_MODEL)), _rows(tm, D_FF)],
        out_specs=_rows(tm, D_FF),
        out_shape=jax.ShapeDtypeStruct((s_len, D_FF), BF16),
        compiler_params=_cparams(("parallel",)),
    )(df2, w_ff2, u)


def _ff1_bwd(du, w_ff1, dh2, h1, y2, g_mlp_pre, g_post):
    s_len = du.shape[0]
    tm = min(ROW_TILE, s_len)

    def body(du_ref, w_ref, dh2_ref, h1_ref, y2_ref, gm_ref, gp_ref, dh1_ref, dy2_ref, dgm_ref, dgp_ref):
        dxn2 = _mm_nt(du_ref[...], w_ref[...])
        h1h, r1 = _rms_stats(h1_ref[...])
        dx, dg = _rms_bwd(dxn2, h1h, r1, gm_ref[...])
        _acc_out(dgm_ref, dg)
        dh1 = dh2_ref[...] + dx
        dh1_ref[...] = dh1
        y2h, ry = _rms_stats(y2_ref[...])
        dy2, dg = _rms_bwd(dh1, y2h, ry, gp_ref[...])
        _acc_out(dgp_ref, dg)
        dy2_ref[...] = dy2.astype(BF16)

    sds = jax.ShapeDtypeStruct
    return pl.pallas_call(
        body, name="ff1_bwd", grid=(s_len // tm,),
        in_specs=[_rows(tm, D_FF), _full((D_MODEL, D_FF)), _rows(tm, D_MODEL), _rows(tm, D_MODEL),
                  _rows(tm, D_MODEL), _full((1, D_MODEL)), _full((1, D_MODEL))],
        out_specs=[_rows(tm, D_MODEL), _rows(tm, D_MODEL), _full((1, D_MODEL)), _full((1, D_MODEL))],
        out_shape=[sds((s_len, D_MODEL), F32), sds((s_len, D_MODEL), BF16), sds((1, D_MODEL), F32),
                   sds((1, D_MODEL), F32)],
        compiler_params=_cparams(("arbitrary",)),
    )(du, w_ff1, dh2, h1, y2, g_mlp_pre, g_post)


def _out_proj_bwd(dy2, w_out, ya, yb, lse_b, g_a, g_b):
    s_len = ya.shape[0]
    tm = min(ROW_TILE, s_len)

    def body(d_ref, w_ref, ya_ref, yb_ref, lse_ref, ga_ref, gb_ref, dya_ref, dyb_ref, da_ref, st_ref, dga_ref,
             dgb_ref):
        dycat = _mm_nt(d_ref[...], w_ref[...])
        lane = lax.broadcasted_iota(jnp.int32, (tm, 128), 1)
        low = lane < HEAD_DIM
        is_lse = (lane % HEAD_DIM) < (HEAD_DIM // 2)

        def head_sums(prod_chunk):
            return (jnp.sum(jnp.where(low, prod_chunk, 0.0), axis=1, keepdims=True),
                    jnp.sum(jnp.where(low, 0.0, prod_chunk), axis=1, keepdims=True))

        ya = ya_ref[...]
        yh, r = _rms_stats(ya)
        dya, dg = _rms_bwd(dycat[:, :D_A], yh, r, ga_ref[...])
        _acc_out(dga_ref, dg)
        dya_ref[...] = dya
        prod = dya * ya
        for c in range(D_A // 128):
            da_ref[2 * c], da_ref[2 * c + 1] = head_sums(prod[:, 128 * c:128 * (c + 1)])

        yb = yb_ref[...]
        yh, r = _rms_stats(yb)
        dyb, dg = _rms_bwd(dycat[:, D_A:], yh, r, gb_ref[...])
        _acc_out(dgb_ref, dg)
        dyb_ref[...] = dyb.astype(BF16)
        prod = dyb * yb
        for c in range(D_B // 128):
            sl = slice(128 * c, 128 * (c + 1))
            d_lo, d_hi = head_sums(prod[:, sl])
            st_ref[:, sl] = jnp.where(is_lse, lse_ref[:, sl], jnp.where(low, d_lo, d_hi))

    sds = jax.ShapeDtypeStruct
    return pl.pallas_call(
        body, name="out_proj_bwd", grid=(s_len // tm,),
        in_specs=[_rows(tm, D_MODEL), _full((D_MODEL, D_MODEL)), _rows(tm, D_A), _rows(tm, D_B), _rows(tm, D_B),
                  _full((1, D_A)), _full((1, D_B))],
        out_specs=[_rows(tm, D_A), _rows(tm, D_B), _stat_spec(tm), _rows(tm, D_B), _full((1, D_A)),
                   _full((1, D_B))],
        out_shape=[sds((s_len, D_A), F32), sds((s_len, D_B), BF16), sds((N_HEADS_A, s_len, 1), F32),
                   sds((s_len, D_B), F32), sds((1, D_A), F32), sds((1, D_B), F32)],
        compiler_params=_cparams(("arbitrary",)),
    )(dy2, w_out, ya, yb, lse_b, g_a, g_b)


def _in_proj_bwd(dqr, dkv, dqb, dkb, dvb, qpre, kpre, x, dh1, g1, w_in, cc, ss, gq2, gk2, ones128):
    s_len = x.shape[0]
    tm = min(ROW_TILE, s_len)

    def body(dqr_ref, dkv_ref, dq0, dq1, dq2, dk0, dk1, dk2, dv0, dv1, dv2, qpre_ref, kpre_ref, x_ref,
             dh1_ref, g_ref, w_ref, cc_ref, ss_ref, gq_ref, gk_ref, one_ref, dproj_ref, gx_ref, dg1_ref, dgq_ref,
             dgk_ref):
        low = _low_lanes(tm)
        dkr = jnp.where(low, dkv_ref[0], pltpu.roll(dkv_ref[1], HEAD_DIM, 1))
        dva = jnp.where(low, pltpu.roll(dkv_ref[0], HEAD_DIM, 1), dkv_ref[1])
        first_half = _first_half_mask((tm, 128))
        ones = one_ref[...]
        cc_t, ss_t = cc_ref[...], ss_ref[...]

        def norm_rope_bwd(dy, xc, gain):
            dn = dy * cc_t - _swap_halves(dy, first_half) * ss_t
            r = lax.rsqrt(_xdot(xc * xc, ones) * (1.0 / HEAD_DIM) + EPS)
            xh = xc * r
            gdy = dn * gain
            dx = r * (gdy - xh * (_xdot(gdy * xh, ones) * (1.0 / HEAD_DIM)))
            return dx, jnp.sum(dn * xh, axis=0, keepdims=True)

        dgq = jnp.zeros((1, 128), F32)
        parts = []
        for c in range(D_A // 128):
            sl = slice(128 * c, 128 * (c + 1))
            dx, dg = norm_rope_bwd(dqr_ref[:, sl] * Q_SCALE, qpre_ref[:, sl], gq_ref[...])
            parts.append(dx)
            dgq = dgq + dg
        dxk, dgk = norm_rope_bwd(dkr, kpre_ref[...], gk_ref[...])
        _acc_out(dgq_ref, dgq)
        _acc_out(dgk_ref, dgk)
        total = lambda a, b, c: a[...].astype(F32) + b[...].astype(F32) + c[...].astype(F32)
        parts += [dxk, dva, total(dq0, dq1, dq2) * Q_SCALE, total(dk0, dk1, dk2), total(dv0, dv1, dv2)]
        dproj = jnp.concatenate(parts, axis=-1).astype(BF16)
        dproj_ref[...] = dproj
        dxn = _mm_nt(dproj, w_ref[...])
        xh, r = _rms_stats(x_ref[...])
        dx, dg = _rms_bwd(dxn, xh, r, g_ref[...])
        _acc_out(dg1_ref, dg)
        gx_ref[...] = dh1_ref[...] + dx

    sds = jax.ShapeDtypeStruct
    return pl.pallas_call(
        body, name="in_proj_bwd", grid=(s_len // tm,),
        in_specs=[_rows(tm, D_A), pl.BlockSpec((N_KV_A, tm, 128), lambda i: (0, i, 0))] + [_rows(tm, D_B)] * 9
                 + [_rows(tm, D_A), _rows(tm, D_KV_A), _rows(tm, D_MODEL), _rows(tm, D_MODEL),
                    _full((1, D_MODEL)), _full((D_MODEL, D_IN)), _rows(tm, 128), _rows(tm, 128), _full((1, 128)),
                    _full((1, 128)), _full((128, 128))],
        out_specs=[_rows(tm, D_IN), _rows(tm, D_MODEL), _full((1, D_MODEL)), _full((1, 128)), _full((1, 128))],
        out_shape=[sds((s_len, D_IN), BF16), sds((s_len, D_MODEL), F32), sds((1, D_MODEL), F32),
                   sds((1, 128), F32), sds((1, 128), F32)],
        compiler_params=_cparams(("arbitrary",)),
    )(dqr, dkv, *dqb, *dkb, *dvb, qpre, kpre, x, dh1, g1, w_in, cc, ss, gq2, gk2, ones128)


def _dw(a, b, name, relu2=False):
    s_len, ka = a.shape
    n = b.shape[1]
    ts = min(DW_TS, s_len)
    bk = min(ka, 1024)
    bn = n if n % 1024 else 1024

    def body(a_ref, b_ref, o_ref):
        @pl.when(pl.program_id(2) == 0)
        def _():
            o_ref[...] = jnp.zeros_like(o_ref)

        av = a_ref[...]
        if relu2:
            av = jnp.square(jnp.maximum(av, 0.0))
        o_ref[...] += _mm_tn(av.astype(BF16), b_ref[...])

    return pl.pallas_call(
        body, name=name, grid=(ka // bk, n // bn, s_len // ts),
        in_specs=[pl.BlockSpec((ts, bk), lambda i, j, k: (k, i)), pl.BlockSpec((ts, bn), lambda i, j, k: (k, j))],
        out_specs=pl.BlockSpec((bk, bn), lambda i, j, k: (i, j)),
        out_shape=jax.ShapeDtypeStruct((ka, n), F32),
        compiler_params=_cparams(("parallel", "parallel", "arbitrary")),
    )(a, b)


def _stack_heads(block, low, data_low):
    parts = []
    for c in range(GROUP_A // 2):
        chunk = block[:, 128 * c:128 * (c + 1)]
        swapped = pltpu.roll(chunk, HEAD_DIM, 1)
        for h_low in (chunk, swapped) if data_low else (swapped, chunk):
            parts.append(jnp.where(low, h_low, 0.0) if data_low else jnp.where(low, 0.0, h_low))
    return jnp.concatenate(parts, axis=0).astype(BF16)


def _unstack_heads(stacked, low, tq, data_low):
    chunks = []
    for c in range(GROUP_A // 2):
        even = stacked[2 * c * tq:(2 * c + 1) * tq]
        odd = stacked[(2 * c + 1) * tq:(2 * c + 2) * tq]
        if data_low:
            chunks.append(jnp.where(low, even, pltpu.roll(odd, HEAD_DIM, 1)))
        else:
            chunks.append(jnp.where(low, pltpu.roll(even, HEAD_DIM, 1), odd))
    return chunks


def _attn_a_fwd(qa, kv):
    s_len = kv.shape[1]
    tq = min(ATT_TQ, s_len)
    tk = min(ATT_TK_FWD, s_len)
    rows = GROUP_A * tq

    def body(q_ref, kv_ref, o_ref, lse_ref):
        low = _low_lanes(tq)
        low_k = _low_lanes(tk)
        q = _stack_heads(q_ref[...].astype(F32), low, data_low=True)

        def block(j, m, acc):
            kvj = kv_ref[0, pl.ds(pl.multiple_of(j * tk, tk), tk), :]
            s = _mm_nt(q, kvj)
            m_new = jnp.maximum(m, jnp.max(s, axis=1, keepdims=True))
            p = jnp.exp(s - m_new).astype(BF16)
            return m_new, jnp.exp(m - m_new) * acc + _mm(p, jnp.where(low_k, jnp.ones_like(kvj), kvj))

        def step(j, carry):
            for u in range(unroll):
                carry = block(unroll * j + u, *carry)
            return carry

        unroll = math.gcd(s_len // tk, ATT_UNROLL_FWD)
        init = (jnp.full((rows, 1), -jnp.inf, F32), jnp.zeros((rows, 128), F32))
        m, acc = lax.fori_loop(0, s_len // (tk * unroll), step, init)
        for c, chunk in enumerate(_unstack_heads(acc / pltpu.roll(acc, HEAD_DIM, 1), low, tq, data_low=False)):
            o_ref[:, 128 * c:128 * (c + 1)] = chunk
        lse_ref[...] = (m + jnp.log(acc[:, :1])).reshape(GROUP_A, tq, 1)

    return pl.pallas_call(
        body, name="attn_a_fwd", grid=(N_KV_A, s_len // tq),
        in_specs=[pl.BlockSpec((tq, 256), lambda g, i: (i, g)),
                  pl.BlockSpec((1, s_len, 128), lambda g, i: (g, 0, 0))],
        out_specs=[pl.BlockSpec((tq, 256), lambda g, i: (i, g)),
                   pl.BlockSpec((GROUP_A, tq, 1), lambda g, i: (g, i, 0))],
        out_shape=[jax.ShapeDtypeStruct((s_len, D_A), F32),
                   jax.ShapeDtypeStruct((N_HEADS_A, s_len, 1), F32)],
        compiler_params=_cparams(("parallel", "parallel")),
    )(qa, kv)


def _attn_a_bwd(qa, dya, kv, lse, delta):
    s_len = kv.shape[1]
    tq = min(ATT_TQ_BWD, s_len)
    tk = min(ATT_TK_BWD, s_len)
    rows = GROUP_A * tq

    def body(q_ref, do_ref, kv_ref, lse_ref, dl_ref, dq_ref, dkv_ref):
        @pl.when(pl.program_id(1) == 0)
        def _():
            dkv_ref[...] = jnp.zeros_like(dkv_ref)

        low = _low_lanes(tq)
        q = _stack_heads(q_ref[...].astype(F32), low, data_low=True)
        do = _stack_heads(do_ref[...], low, data_low=False)
        lse_t = lse_ref[...].reshape(rows, 1)
        dl_t = dl_ref[...].reshape(rows, 1)
        q_t = q.T
        do_t = do.T

        def block(j, dq):
            span = pl.ds(pl.multiple_of(j * tk, tk), tk)
            kvj = kv_ref[0, span, :]
            p = jnp.exp(_mm_nt(q, kvj) - lse_t)
            ds = (p * (_mm_nt(do, kvj) - dl_t)).astype(BF16)
            dkv_ref[0, :, span] += _mm(q_t, ds) + _mm(do_t, p.astype(BF16))
            return dq + _mm(ds, kvj)

        def step(j, dq):
            for u in range(unroll):
                dq = block(unroll * j + u, dq)
            return dq

        unroll = math.gcd(s_len // tk, ATT_UNROLL_BWD)
        dq = lax.fori_loop(0, s_len // (tk * unroll), step, jnp.zeros((rows, 128), F32))
        for c, chunk in enumerate(_unstack_heads(dq, low, tq, data_low=True)):
            dq_ref[:, 128 * c:128 * (c + 1)] = chunk

    return pl.pallas_call(
        body, name="attn_a_bwd", grid=(N_KV_A, s_len // tq),
        in_specs=[pl.BlockSpec((tq, 256), lambda g, i: (i, g)),
                  pl.BlockSpec((tq, 256), lambda g, i: (i, g)),
                  pl.BlockSpec((1, s_len, 128), lambda g, i: (g, 0, 0)),
                  pl.BlockSpec((GROUP_A, tq, 1), lambda g, i: (g, i, 0)),
                  pl.BlockSpec((GROUP_A, tq, 1), lambda g, i: (g, i, 0))],
        out_specs=[pl.BlockSpec((tq, 256), lambda g, i: (i, g)),
                   pl.BlockSpec((1, 128, s_len), lambda g, i: (g, 0, 0))],
        out_shape=[jax.ShapeDtypeStruct((s_len, D_A), F32),
                   jax.ShapeDtypeStruct((N_KV_A, 128, s_len), F32)],
        compiler_params=_cparams(("parallel", "arbitrary")),
    )(qa, dya, kv, lse, delta)


class _SwaGeometry:
    def __init__(self, s_len, r):
        self.r = r
        self.tq = SWA_TQ
        self.block = min(max(SWA_MIN_BLOCK, 2 * SWA_TQ * r), s_len)
        self.halo = HALF_WIN * r
        self.nsub = self.block // (self.tq * r)
        self.band = self.tq + 2 * HALF_WIN
        self.length = s_len // r
        self.nblk = s_len // self.block
        self.nhalo = s_len // self.halo
        assert self.nsub * self.tq * r == self.block and self.block % self.halo == 0

    def specs(self):
        per = self.block // self.halo
        cur = pl.BlockSpec((self.block, 128), lambda c, i: (i, c))
        prev = pl.BlockSpec((self.halo, 128), lambda c, i: (jnp.maximum(i * per - 1, 0), c))
        nxt = pl.BlockSpec((self.halo, 128), lambda c, i: (jnp.minimum((i + 1) * per, self.nhalo - 1), c))
        return prev, cur, nxt

    def tiles(self):
        return [(rho + self.r * j * self.tq, self.halo + rho + self.r * (j * self.tq - HALF_WIN), j)
                for j in range(self.nsub) for rho in range(self.r)]

    def own(self, start):
        return pl.ds(start, self.tq, stride=self.r)

    def around(self, start):
        return pl.ds(start, self.band, stride=self.r)

    def fill(self, dst, prev_ref, cur_ref, next_ref):
        dst[:self.halo, :] = prev_ref[...].astype(F32)
        dst[self.halo:self.halo + self.block, :] = cur_ref[...].astype(F32)
        dst[self.halo + self.block:, :] = next_ref[...].astype(F32)

    def first_position(self, j):
        return (pl.program_id(1) * self.block) // self.r + j * self.tq

    def outside(self, j, copies=1):
        pos = self.first_position(j) - HALF_WIN + lax.broadcasted_iota(jnp.int32, (1, copies * self.band), 1) % self.band
        return jnp.where((pos >= 0) & (pos < self.length), 0.0, NEG_BIG)

    def extended(self):
        return pltpu.VMEM((self.block + 2 * self.halo, 128), F32)

    def plain(self):
        return pltpu.VMEM((self.block, 128), F32)


def _low_lanes(rows):
    return lax.broadcasted_iota(jnp.int32, (rows, 128), 1) < HEAD_DIM


def _one_head(x, low, half):
    return jnp.where(low if half == 0 else jnp.logical_not(low), x, 0.0).astype(BF16)


def _two_heads(x, low):
    return jnp.concatenate([_one_head(x, low, 0), _one_head(x, low, 1)], axis=0)


def _carry_ride(base_body, n_in, n_out, n_scratch, ride, grid):
    if ride is None:
        return base_body
    n = ride.n

    def body(*refs):
        o = n_in + n
        ins, ride_ins = refs[:n_in], refs[n_in:o]
        outs, ride_outs = refs[o:o + n_out], refs[o + n_out:o + n_out + n]
        o += n_out + n
        scratch, sems = refs[o:o + n_scratch], refs[o + n_scratch:]
        at_first = (pl.program_id(0) == 0) & (pl.program_id(1) == 0)
        at_last = (pl.program_id(0) == grid[0] - 1) & (pl.program_id(1) == grid[1] - 1)

        @pl.when(at_first)
        def _():
            ride.start(ride_ins, ride_outs, sems)

        base_body(*ins, *outs, *scratch)

        @pl.when(at_last)
        def _():
            ride.finish(ride_ins, ride_outs, sems)

    return body


def _ride_call(base_body, name, grid, in_specs, out_specs, out_shape, scratch_shapes, operands, ride):
    n = 0 if ride is None else ride.n
    extra = [] if ride is None else ride.operands
    outs = pl.pallas_call(
        _carry_ride(base_body, len(in_specs), len(out_specs), len(scratch_shapes), ride, grid), name=name, grid=grid,
        in_specs=list(in_specs) + [ANY] * n, out_specs=list(out_specs) + [ANY] * n,
        out_shape=list(out_shape) + ([] if ride is None else ride.out_shape()),
        scratch_shapes=list(scratch_shapes) + ([] if ride is None else ride.scratch_shapes()),
        compiler_params=_cparams(("arbitrary", "arbitrary")),
    )(*operands, *extra)
    return outs[:len(out_specs)], outs[len(out_specs):]


def _swa_fwd(q, k, v, bias, r, so_far=None, ride=None):
    geo = _SwaGeometry(q.shape[0], r)
    prev, cur, nxt = geo.specs()

    def body(q_ref, kp, kc, kn, vp, vc, vn, b_ref, *rest):
        if so_far is None:
            o_ref, lse_ref, qf, kf, vf = rest
        else:
            o_old_ref, lse_old_ref, o_ref, lse_ref, qf, kf, vf = rest
        qf[...] = q_ref[...].astype(F32)
        geo.fill(kf, kp, kc, kn)
        geo.fill(vf, vp, vc, vn)
        tq = geo.tq
        low_q = _low_lanes(tq)
        bias = b_ref[...].reshape(2 * tq, geo.band)
        for own, around, j in geo.tiles():
            q2 = _two_heads(qf[geo.own(own), :], low_q)
            kb = kf[geo.around(around), :].astype(BF16)
            vb = vf[geo.around(around), :].astype(BF16)
            s = _mm_nt(q2, kb) + bias + geo.outside(j)
            m = jnp.max(s, axis=1, keepdims=True)
            e = jnp.exp(s - m)
            l = jnp.sum(e, axis=1, keepdims=True)
            o2 = _mm(e.astype(BF16), vb) / l
            lse2 = m + jnp.log(l)
            o_new = jnp.where(low_q, o2[:tq], o2[tq:])
            lse_new = jnp.where(low_q, lse2[:tq], lse2[tq:])
            if so_far is not None:
                o_old, lse_old = o_old_ref[geo.own(own), :], lse_old_ref[geo.own(own), :]
                top = jnp.maximum(lse_old, lse_new)
                w_old, w_new = jnp.exp(lse_old - top), jnp.exp(lse_new - top)
                o_new = (w_old * o_old + w_new * o_new) / (w_old + w_new)
                lse_new = top + jnp.log(w_old + w_new)
            o_ref[geo.own(own), :] = o_new
            lse_ref[geo.own(own), :] = lse_new

    sds = jax.ShapeDtypeStruct
    before = () if so_far is None else tuple(so_far)
    (o, lse), carried = _ride_call(
        body, "swa_fwd_%d" % r, (D_B // 128, geo.nblk),
        [cur, prev, cur, nxt, prev, cur, nxt, pl.BlockSpec((2, geo.tq, geo.band), lambda c, i: (c, 0, 0))]
        + [cur] * len(before),
        [cur, cur], [sds(q.shape, F32), sds(q.shape, F32)], [geo.plain(), geo.extended(), geo.extended()],
        (q, k, k, k, v, v, v, bias) + before, ride)
    return o, lse, carried


def _head_stats(st, half):
    lo = HEAD_DIM * half
    return st[:, lo:lo + 1], st[:, lo + HEAD_DIM // 2:lo + HEAD_DIM // 2 + 1]


def _swa_bwd_q(q, k, v, dy, st, bias, r, ride=None):
    geo = _SwaGeometry(q.shape[0], r)
    prev, cur, nxt = geo.specs()
    bias_spec = pl.BlockSpec((2, geo.tq, geo.band), lambda c, i: (c, 0, 0))

    def body(q_ref, kp, kc, kn, vp, vc, vn, dy_ref, st_ref, b_ref, dq_ref, db_ref, qf, kf, vf, dyf, dqf):
        @pl.when(pl.program_id(1) == 0)
        def _():
            db_ref[...] = jnp.zeros_like(db_ref)

        qf[...] = q_ref[...].astype(F32)
        dyf[...] = dy_ref[...].astype(F32)
        geo.fill(kf, kp, kc, kn)
        geo.fill(vf, vp, vc, vn)
        tq = geo.tq
        low_q = _low_lanes(tq)
        bias = b_ref[...].reshape(2 * tq, geo.band)
        for own, around, j in geo.tiles():
            sts = st_ref[geo.own(own), :]
            (lse0, delta0), (lse1, delta1) = _head_stats(sts, 0), _head_stats(sts, 1)
            lse = jnp.concatenate([lse0, lse1], axis=0)
            delta = jnp.concatenate([delta0, delta1], axis=0)
            kb = kf[geo.around(around), :].astype(BF16)
            vb = vf[geo.around(around), :].astype(BF16)
            s = _mm_nt(_two_heads(qf[geo.own(own), :], low_q), kb) + bias + geo.outside(j)
            p = jnp.exp(s - lse)
            ds = p * (_mm_nt(_two_heads(dyf[geo.own(own), :], low_q), vb) - delta)
            db_ref[...] += ds.reshape(2, tq, geo.band)
            dq2 = _mm(ds.astype(BF16), kb)
            dqf[geo.own(own), :] = jnp.where(low_q, dq2[:tq], dq2[tq:])
        dq_ref[...] = dqf[...].astype(BF16)

    (dq, dbias), carried = _ride_call(
        body, "swa_bwd_q_%d" % r, (D_B // 128, geo.nblk),
        [cur, prev, cur, nxt, prev, cur, nxt, cur, cur, bias_spec], [cur, bias_spec],
        [jax.ShapeDtypeStruct(q.shape, BF16), jax.ShapeDtypeStruct(bias.shape, F32)],
        [geo.plain(), geo.extended(), geo.extended(), geo.plain(), geo.plain()],
        (q, k, k, k, v, v, v, dy, st, bias), ride)
    return dq, dbias, carried


def _swa_bwd_kv(q, k, v, dy, st, bias_kv, r):
    geo = _SwaGeometry(q.shape[0], r)
    prev, cur, nxt = geo.specs()

    def body(k_ref, v_ref, qp, qc, qn, dp_, dc_, dn_, sp, sc, sn, b_ref, dk_ref, dv_ref, kf, vf, qf, dyf, stf, dkf,
             dvf):
        kf[...] = k_ref[...].astype(F32)
        vf[...] = v_ref[...].astype(F32)
        geo.fill(qf, qp, qc, qn)
        geo.fill(dyf, dp_, dc_, dn_)
        geo.fill(stf, sp, sc, sn)
        band = geo.band
        low_b = _low_lanes(band)
        bias = jnp.concatenate([b_ref[0], b_ref[1]], axis=1)
        half_lanes = HEAD_DIM // 2
        for own, around, j in geo.tiles():
            ks = kf[geo.own(own), :].astype(BF16)
            vs = vf[geo.own(own), :].astype(BF16)
            q2 = _two_heads(qf[geo.around(around), :], low_b)
            dy2 = _two_heads(dyf[geo.around(around), :], low_b)
            st_t = stf[geo.around(around), :].T
            lse = jnp.concatenate([st_t[:1, :], st_t[HEAD_DIM:HEAD_DIM + 1, :]], axis=1)
            delta = jnp.concatenate([st_t[half_lanes:half_lanes + 1, :],
                                     st_t[HEAD_DIM + half_lanes:HEAD_DIM + half_lanes + 1, :]], axis=1)
            p = jnp.exp(_mm_nt(ks, q2) + bias + (geo.outside(j, copies=2) - lse))
            ds = p * (_mm_nt(vs, dy2) - delta)
            dvf[geo.own(own), :] = _mm(p.astype(BF16), dy2)
            dkf[geo.own(own), :] = _mm(ds.astype(BF16), q2)
        dk_ref[...] = dkf[...].astype(BF16)
        dv_ref[...] = dvf[...].astype(BF16)

    return pl.pallas_call(
        body, name="swa_bwd_kv_%d" % r, grid=(D_B // 128, geo.nblk),
        in_specs=[cur, cur, prev, cur, nxt, prev, cur, nxt, prev, cur, nxt,
                  pl.BlockSpec((2, geo.tq, geo.band), lambda c, i: (c, 0, 0))],
        out_specs=[cur, cur],
        out_shape=[jax.ShapeDtypeStruct(q.shape, BF16), jax.ShapeDtypeStruct(q.shape, BF16)],
        scratch_shapes=[geo.plain(), geo.plain(), geo.extended(), geo.extended(), geo.extended(), geo.plain(),
                        geo.plain()],
        compiler_params=_cparams(("parallel", "parallel")),
    )(k, v, q, q, q, dy, dy, dy, st, st, st, bias_kv)


BIAS_ROWS = 16
BIAS_TN = 4096


def _bias_tiles(onehot, rel_bias_t):
    n = onehot.shape[0]

    def body(oh_ref, rb_ref, o_ref):
        o_ref[...] = sum(_mm_nt(piece, oh_ref[...]) for piece in _split3(rb_ref[...]))

    return pl.pallas_call(
        body, name="bias_tiles", grid=(n // BIAS_TN,),
        in_specs=[_rows(BIAS_TN, 128), _full((BIAS_ROWS, 128))],
        out_specs=pl.BlockSpec((BIAS_ROWS, BIAS_TN), lambda i: (0, i)),
        out_shape=jax.ShapeDtypeStruct((BIAS_ROWS, n), F32),
        compiler_params=_cparams(("parallel",)),
    )(onehot, rel_bias_t)


def _bias_bwd(onehot, dbias_rows, so_far, r):
    n = onehot.shape[0]

    def body(oh, d, prev_ref, g_ref):
        @pl.when(pl.program_id(0) == 0)
        def _():
            g_ref[...] = prev_ref[...]

        hi, lo, _ = _split3(d[...])
        g_ref[...] += _mm(hi, oh[...]) + _mm(lo, oh[...])

    return pl.pallas_call(
        body, name="bias_bwd_%d" % r, grid=(n // BIAS_TN,),
        in_specs=[_rows(BIAS_TN, 128), pl.BlockSpec((BIAS_ROWS, BIAS_TN), lambda i: (0, i)), _full((BIAS_ROWS, 128))],
        out_specs=_full((BIAS_ROWS, 128)),
        out_shape=jax.ShapeDtypeStruct((BIAS_ROWS, 128), F32),
        compiler_params=_cparams(("arbitrary",)),
    )(onehot, dbias_rows, so_far)


LATE = ("w_out", "w_ff1", "w_ff2", "w_ple_gate", "w_ple_proj")


def _local_step(x, p, tgt, w_in, late_shards, g_attn_pre, g_q, g_k, g_out_a, g_out_b, g_attn_post, rel_bias,
                g_mlp_pre, g_mlp_post, g_ple):
    s_len = x.shape[0]
    cc, ss = _rope_tables(s_len)
    gq2 = jnp.concatenate([g_q, g_q], axis=-1)
    gk2 = jnp.concatenate([g_k, g_k], axis=-1)
    ones128 = _group_ones(128)
    rel_bias_t = jnp.zeros((BIAS_ROWS, 128), F32).at[:N_HEADS_B, :N_BUCKETS].set(rel_bias.T)

    xn1, qpre, kpre, qa, kv, qb, kb, vb = _in_proj(x, g_attn_pre, w_in, cc, ss, gq2, gk2, ones128)
    ya, lse_a = _attn_a_fwd(qa, kv)

    tiles, joint = [], None
    for r in DILATIONS:
        tq = SWA_TQ
        onehot = _bucket_onehot(tq, r)
        bias = _bias_tiles(onehot, rel_bias_t)[:N_HEADS_B].reshape(N_HEADS_B, tq, tq + 2 * HALF_WIN)
        bias = jnp.where(_in_window(tq), bias, NEG_BIG)
        yb, lse_b, gathered = _swa_fwd(qb, kb, vb, bias, r, joint,
                                       _GatherRide(late_shards) if r == DILATIONS[-1] else None)
        tiles.append((onehot, bias))
        joint = (yb, lse_b)
    w_out, w_ff1, w_ff2, w_gate, w_ple = (_whole(n, g, mine) for n, g, mine in zip(LATE, gathered, late_shards))

    ycat, y2, h1, xn2 = _out_proj(ya, yb, x, g_out_a, g_out_b, w_out, g_attn_post, g_mlp_pre)
    u = _ff1(xn2, w_ff1)
    f2, h2, xn3 = _ff2(u, w_ff2, h1, g_mlp_post, g_ple)
    dh2, df2, dgl, dpp, loss, dg_ple, dg_mlp_post = _ple_loss(xn3, p, h2, f2, tgt, w_gate, w_ple, g_ple, g_mlp_post)

    grads = {"g_ple": dg_ple, "g_mlp_post": dg_mlp_post}
    grads["w_ple_gate"] = _dw(xn3, dgl, "dw_gate")
    grads["w_ple_proj"] = _dw(p, dpp, "dw_ple")
    grads["w_ff2"] = _dw(u, df2, "dw_ff2", relu2=True)
    du = _ff2_bwd(df2, w_ff2, u)
    grads["w_ff1"] = _dw(xn2, du, "dw_ff1")
    dh1, dy2, grads["g_mlp_pre"], grads["g_attn_post"] = _ff1_bwd(du, w_ff1, dh2, h1, y2, g_mlp_pre, g_attn_post)
    grads["w_out"] = _dw(ycat, dy2, "dw_out")
    dya, dyb, delta_a, st_b, grads["g_out_a"], grads["g_out_b"] = _out_proj_bwd(dy2, w_out, ya, yb, lse_b, g_out_a,
                                                                              g_out_b)

    pairs = _pair_sums(LATE, [grads[n] for n in LATE])

    dqr, dkv_t = _attn_a_bwd(qa, dya, kv, lse_a, delta_a)
    dkv_a = dkv_t.transpose(0, 2, 1)

    dqs, dks, dvs = [], [], []
    d_rel = jnp.zeros((BIAS_ROWS, 128), F32)
    for r, (onehot, bias) in zip(DILATIONS, tiles):
        dq_r, dbias, scattered = _swa_bwd_q(qb, kb, vb, dyb, st_b, bias, r,
                                            _ScatterRide(pairs) if r == DILATIONS[0] else None)
        if scattered:
            for n, half in zip(LATE, _chip_sums(LATE, pairs, scattered)):
                grads[n] = half
        bias_kv = jnp.flip(bias, axis=(1, 2))
        dk_r, dv_r = _swa_bwd_kv(qb, kb, vb, dyb, st_b, bias_kv, r)
        dbias_rows = jnp.pad(dbias.reshape(N_HEADS_B, -1), ((0, BIAS_ROWS - N_HEADS_B), (0, 0)))
        d_rel = _bias_bwd(onehot, dbias_rows, d_rel, r)
        dqs.append(dq_r)
        dks.append(dk_r)
        dvs.append(dv_r)
    grads["rel_bias"] = d_rel[:N_HEADS_B, :N_BUCKETS].T

    dproj, grad_x, grads["g_attn_pre"], dgq2, dgk2 = _in_proj_bwd(
        dqr, dkv_a, dqs, dks, dvs, qpre, kpre, x, dh1, g_attn_pre, w_in, cc, ss, gq2, gk2, ones128)
    grads["g_q"] = dgq2[:, :HEAD_DIM] + dgq2[:, HEAD_DIM:]
    grads["g_k"] = dgk2[:, :HEAD_DIM] + dgk2[:, HEAD_DIM:]
    grads["w_in"] = _dw(xn1, dproj, "dw_in")
    return loss, grad_x, grads


ANY = pl.BlockSpec(memory_space=pl.ANY)


def _position():
    return lax.axis_index("x"), lax.axis_index("y"), lax.axis_index("c")


def _other_chips(x, y):
    return [(2 * (1 - x) + y, (1 - x, y)), (2 * x + (1 - y), (x, 1 - y)), (2 * (1 - x) + (1 - y), (1 - x, 1 - y))]


def _cast_shards(shards):
    def body(*refs):
        n = len(refs) // 2
        for i_ref, o_ref in zip(refs[:n], refs[n:]):
            o_ref[...] = i_ref[...].astype(BF16)

    return pl.pallas_call(
        body, name="cast_shards",
        in_specs=[pl.BlockSpec(memory_space=pltpu.VMEM)] * len(shards),
        out_specs=[pl.BlockSpec(memory_space=pltpu.VMEM)] * len(shards),
        out_shape=[jax.ShapeDtypeStruct(s.shape, BF16) for s in shards],
        compiler_params=_cparams(),
    )(*shards)


def _gather_weights(shards):
    n = len(shards)

    ride = _GatherRide(shards)

    def body(*refs):
        ride.start(refs[:n], refs[n:2 * n], refs[2 * n:])
        ride.finish(refs[:n], refs[n:2 * n], refs[2 * n:])

    return pl.pallas_call(
        body, name="gather_weights",
        in_specs=[ANY] * n, out_specs=[ANY] * n,
        out_shape=ride.out_shape(), scratch_shapes=ride.scratch_shapes(),
    )(*shards)


class _GatherRide:
    def __init__(self, shards):
        self.operands = list(shards)
        self.n = len(shards)

    def out_shape(self):
        return [jax.ShapeDtypeStruct((N_CHIPS,) + s.shape, s.dtype) for s in self.operands]

    def scratch_shapes(self):
        return [pltpu.SemaphoreType.DMA((3, self.n))] * 4

    @staticmethod
    def _rows(ref, core):
        half = ref.shape[0] // 2
        return pl.ds(pl.multiple_of(core * half, 16), half)

    def _ici(self, ins, outs, sems, k, a, chip):
        x, y, c = _position()
        return pltpu.make_async_remote_copy(ins[a].at[self._rows(ins[a], c), :],
                                            outs[a].at[2 * x + y, self._rows(ins[a], c), :], sems[0].at[k, a],
                                            sems[1].at[k, a], device_id=(*chip, c), device_id_type=MESH)

    def _pass_on(self, ins, outs, sems, k, a, num, core):
        x, y, c = _position()
        half = outs[a].at[num, self._rows(ins[a], core), :]
        return pltpu.make_async_remote_copy(half, half, sems[2].at[k, a], sems[3].at[k, a], device_id=(x, y, 1 - c),
                                            device_id_type=MESH)

    def start(self, ins, outs, sems):
        x, y, _ = _position()
        for k, (_, chip) in enumerate(_other_chips(x, y)):
            for a in range(self.n):
                self._ici(ins, outs, sems, k, a, chip).start()

    def finish(self, ins, outs, sems):
        x, y, c = _position()
        others = _other_chips(x, y)
        for k, (num, chip) in enumerate(others):
            for a in range(self.n):
                landed = outs[a].at[num, self._rows(ins[a], c), :]
                pltpu.make_async_remote_copy(landed, landed, sems[0].at[k, a], sems[1].at[k, a], device_id=(*chip, c),
                                             device_id_type=MESH).wait_recv()
                self._pass_on(ins, outs, sems, k, a, num, c).start()
        for k, (num, chip) in enumerate(others):
            for a in range(self.n):
                self._pass_on(ins, outs, sems, k, a, num, 1 - c).wait_recv()
        for k, (num, chip) in enumerate(others):
            for a in range(self.n):
                self._ici(ins, outs, sems, k, a, chip).wait_send()
                self._pass_on(ins, outs, sems, k, a, num, c).wait_send()


def _send_sibling_half(grads, tag):
    n = len(grads)

    def body(*refs):
        ins, outs = refs[:n], refs[n:2 * n]
        send_sems, recv_sems = refs[2 * n:]
        x, y, c = _position()
        copies = []
        for a in range(n):
            half = ins[a].shape[1] // 2
            theirs = ins[a].at[:, pl.ds(pl.multiple_of((1 - c) * half, 8), half), :]
            cp = pltpu.make_async_remote_copy(theirs, outs[a], send_sems.at[a], recv_sems.at[a],
                                              device_id=(x, y, 1 - c), device_id_type=MESH)
            cp.start()
            copies.append(cp)
        for cp in copies:
            cp.wait()

    return pl.pallas_call(
        body, name="send_sibling_half_" + tag,
        in_specs=[ANY] * n, out_specs=[ANY] * n,
        out_shape=[jax.ShapeDtypeStruct((g.shape[0], g.shape[1] // 2, g.shape[2]), g.dtype) for g in grads],
        scratch_shapes=[pltpu.SemaphoreType.DMA((n,)), pltpu.SemaphoreType.DMA((n,))],
    )(*grads)


def _scatter_to_chips(pairs):
    n = len(pairs)
    ride = _ScatterRide(pairs)

    def body(*refs):
        ride.start(refs[:n], refs[n:2 * n], refs[2 * n:])
        ride.finish(refs[:n], refs[n:2 * n], refs[2 * n:])

    return pl.pallas_call(
        body, name="scatter_to_chips",
        in_specs=[ANY] * n, out_specs=[ANY] * n,
        out_shape=ride.out_shape(), scratch_shapes=ride.scratch_shapes(),
    )(*pairs)


class _ScatterRide:
    def __init__(self, pairs):
        self.operands = list(pairs)
        self.n = len(pairs)

    def out_shape(self):
        return [jax.ShapeDtypeStruct(g.shape, g.dtype) for g in self.operands]

    def scratch_shapes(self):
        return [pltpu.SemaphoreType.DMA((3, self.n))] * 2

    @staticmethod
    def _copy(ins, outs, sems, k, a, src_slot, dst_slot, chip):
        _, _, c = _position()
        return pltpu.make_async_remote_copy(ins[a].at[src_slot], outs[a].at[dst_slot], sems[0].at[k, a],
                                            sems[1].at[k, a], device_id=(*chip, c), device_id_type=MESH)

    def start(self, ins, outs, sems):
        x, y, _ = _position()
        for k, (num, chip) in enumerate(_other_chips(x, y)):
            for a in range(self.n):
                self._copy(ins, outs, sems, k, a, num, 2 * x + y, chip).start()

    def finish(self, ins, outs, sems):
        x, y, _ = _position()
        for k, (num, chip) in enumerate(_other_chips(x, y)):
            for a in range(self.n):
                self._copy(ins, outs, sems, k, a, 2 * x + y, num, chip).wait_recv()
        for k, (num, chip) in enumerate(_other_chips(x, y)):
            for a in range(self.n):
                self._copy(ins, outs, sems, k, a, num, 2 * x + y, chip).wait_send()


def _exchange_halves(halves):
    n = len(halves)

    def body(*refs):
        ins, outs = refs[:n], refs[n:2 * n]
        send_sems, recv_sems = refs[2 * n:]
        x, y, c = _position()
        copies = []
        for a in range(n):
            cp = pltpu.make_async_remote_copy(ins[a], outs[a], send_sems.at[a], recv_sems.at[a],
                                              device_id=(x, y, 1 - c), device_id_type=MESH)
            cp.start()
            copies.append(cp)
        for cp in copies:
            cp.wait()

    return pl.pallas_call(
        body, name="exchange_halves",
        in_specs=[ANY] * n, out_specs=[ANY] * n,
        out_shape=[jax.ShapeDtypeStruct(h.shape, h.dtype) for h in halves],
        scratch_shapes=[pltpu.SemaphoreType.DMA((n,)), pltpu.SemaphoreType.DMA((n,))],
    )(*halves)


def _allreduce_small(v):
    def body(v_ref, o_ref, buf, send_sems, recv_sems):
        x, y, c = _position()
        me = 4 * x + 2 * y + c
        peers = [(1 - x, y, c), (x, 1 - y, c), (x, y, 1 - c), (1 - x, 1 - y, c), (1 - x, y, 1 - c), (x, 1 - y, 1 - c),
                 (1 - x, 1 - y, 1 - c)]
        num = lambda d: 4 * d[0] + 2 * d[1] + d[2]
        buf[me] = v_ref[...]
        sends = []
        for k, peer in enumerate(peers):
            cp = pltpu.make_async_remote_copy(v_ref, buf.at[me], send_sems.at[k], recv_sems.at[k], device_id=peer,
                                              device_id_type=MESH)
            cp.start()
            sends.append(cp)
        for k, peer in enumerate(peers):
            pltpu.make_async_remote_copy(v_ref, buf.at[num(peer)], send_sems.at[k], recv_sems.at[k], device_id=peer,
                                         device_id_type=MESH).wait_recv()
        for cp in sends:
            cp.wait_send()
        total = buf[0]
        for d in range(1, 8):
            total = total + buf[d]
        o_ref[...] = total

    return pl.pallas_call(
        body, name="allreduce_small",
        in_specs=[pl.BlockSpec(memory_space=pltpu.VMEM)], out_specs=pl.BlockSpec(memory_space=pltpu.VMEM),
        out_shape=jax.ShapeDtypeStruct(v.shape, v.dtype),
        scratch_shapes=[pltpu.VMEM((8,) + v.shape, v.dtype), pltpu.SemaphoreType.DMA((7,)),
                        pltpu.SemaphoreType.DMA((7,))],
    )(v)


def _sum_leading(a, name):
    k, r, c = a.shape
    tr = min(r, 256)

    def body(a_ref, o_ref):
        total = a_ref[0].astype(F32)
        for i in range(1, k):
            total = total + a_ref[i].astype(F32)
        o_ref[...] = total

    return pl.pallas_call(
        body, name=name, grid=(r // tr,),
        in_specs=[pl.BlockSpec((k, tr, c), lambda i: (0, i, 0))],
        out_specs=pl.BlockSpec((tr, c), lambda i: (i, 0)),
        out_shape=jax.ShapeDtypeStruct((r, c), F32),
        compiler_params=_cparams(("parallel",)),
    )(a)


def _add(a, b, name):
    k, r, c = a.shape
    tr = min(r, 256)
    spec = pl.BlockSpec((k, tr, c), lambda i: (0, i, 0))

    def body(a_ref, b_ref, o_ref):
        o_ref[...] = (a_ref[...] + b_ref[...]).astype(BF16)

    return pl.pallas_call(
        body, name=name, grid=(r // tr,), in_specs=[spec, spec], out_specs=spec,
        out_shape=jax.ShapeDtypeStruct(a.shape, BF16), compiler_params=_cparams(("parallel",)),
    )(a, b)


def _adamw(w, g, m, v, name):
    r, c = w.shape
    tr = min(r, 256)
    spec = pl.BlockSpec((tr, c), lambda i: (i, 0))

    def body(w_ref, g_ref, m_ref, v_ref, d_ref, nm_ref, nv_ref):
        gv = g_ref[...]
        nm = ADAM_B1 * m_ref[...] + (1.0 - ADAM_B1) * gv
        nv = ADAM_B2 * v_ref[...] + (1.0 - ADAM_B2) * jnp.square(gv)
        m_hat = nm / (1.0 - ADAM_B1 ** ADAM_STEP)
        v_hat = nv / (1.0 - ADAM_B2 ** ADAM_STEP)
        d_ref[...] = -ADAM_LR * (m_hat / (jnp.sqrt(v_hat) + ADAM_EPS) + ADAM_WD * w_ref[...])
        nm_ref[...] = nm
        nv_ref[...] = nv

    return pl.pallas_call(
        body, name=name, grid=(r // tr,), in_specs=[spec] * 4, out_specs=[spec] * 3,
        out_shape=[jax.ShapeDtypeStruct(w.shape, F32)] * 3, compiler_params=_cparams(("parallel",)),
    )(w, g, m, v)


MATRICES = ("w_in", "w_out", "w_ff1", "w_ff2", "w_ple_gate", "w_ple_proj")
COLUMN_SHARDED = ("w_in", "w_ff1", "w_ple_proj")
SMALL = ("g_attn_pre", "g_q", "g_k", "g_out_a", "g_out_b", "g_attn_post", "rel_bias", "g_mlp_pre", "g_mlp_post",
         "g_ple")
WEIGHT_ORDER = ("w_in", "g_attn_pre", "g_q", "g_k", "g_out_a", "g_out_b", "w_out", "g_attn_post", "rel_bias",
                "g_mlp_pre", "w_ff1", "w_ff2", "g_mlp_post", "g_ple", "w_ple_gate", "w_ple_proj")
PACK_ROWS, PACK_COLS = 8, 1024


def _chip():
    return 2 * lax.axis_index("x") + lax.axis_index("y")


def _whole(name, gathered, mine):
    g = lax.dynamic_update_slice_in_dim(gathered, mine[None], _chip(), axis=0)
    if name in COLUMN_SHARDED:
        return g.transpose(1, 0, 2).reshape(g.shape[1], N_CHIPS * g.shape[2])
    return g.reshape(N_CHIPS * g.shape[1], g.shape[2])


def _pair_sums(names, grads):
    by_chip = []
    for n, g in zip(names, grads):
        if n in COLUMN_SHARDED:
            by_chip.append(g.reshape(g.shape[0], N_CHIPS, g.shape[1] // N_CHIPS).transpose(1, 0, 2))
        else:
            by_chip.append(g.reshape(N_CHIPS, g.shape[0] // N_CHIPS, g.shape[1]))
    c = lax.axis_index("c")
    pairs = []
    for n, g, other in zip(names, by_chip, _send_sibling_half(by_chip, names[0])):
        half = g.shape[1] // 2
        pairs.append(_add(lax.dynamic_slice_in_dim(g, c * half, half, axis=1), other, "pair_sum_" + n))
    return pairs


def _chip_sums(names, pairs, scattered):
    halves = []
    for n, pair, got in zip(names, pairs, scattered):
        own = lax.dynamic_slice_in_dim(pair, _chip(), 1, axis=0)
        halves.append(_sum_leading(lax.dynamic_update_slice_in_dim(got, own, _chip(), axis=0), "chip_sum_" + n))
    return halves


def _pack_small(values, extra=None):
    flat = [values[n].reshape(-1) for n in SMALL]
    used = sum(f.shape[0] for f in flat)
    tail = jnp.zeros((PACK_ROWS * PACK_COLS - used - 1,), F32)
    last = jnp.zeros((1,), F32) if extra is None else extra.reshape(1)
    return jnp.concatenate(flat + [tail, last]).reshape(PACK_ROWS, PACK_COLS)


def _unpack_small(packed, like):
    flat = packed.reshape(-1)
    out, o = {}, 0
    for n in SMALL:
        size = like[n].size
        out[n] = flat[o:o + size].reshape(like[n].shape)
        o += size
    return out, flat[-1]


def kernel(x, p, w_in, g_attn_pre, g_q, g_k, g_out_a, g_out_b, w_out, g_attn_post, rel_bias, g_mlp_pre, w_ff1, w_ff2, g_mlp_post, g_ple, w_ple_gate, w_ple_proj, loss_target, m_w_in, m_g_attn_pre, m_g_q, m_g_k, m_g_out_a, m_g_out_b, m_w_out, m_g_attn_post, m_rel_bias, m_g_mlp_pre, m_w_ff1, m_w_ff2, m_g_mlp_post, m_g_ple, m_w_ple_gate, m_w_ple_proj, v_w_in, v_g_attn_pre, v_g_q, v_g_k, v_g_out_a, v_g_out_b, v_w_out, v_g_attn_post, v_rel_bias, v_g_mlp_pre, v_w_ff1, v_w_ff2, v_g_mlp_post, v_g_ple, v_w_ple_gate, v_w_ple_proj):
    given = dict(locals())
    weights = {n: given[n] for n in WEIGHT_ORDER}
    shards = {n: weights[n][0] for n in MATRICES}

    c = lax.axis_index("c")
    own = dict(zip(MATRICES, _cast_shards([shards[n] for n in MATRICES])))
    w_in_whole = _whole("w_in", _gather_weights([own["w_in"]])[0], own["w_in"])

    loss, grad_x, grads = _local_step(
        x[0], p[0, 0], loss_target[0], w_in_whole, [own[n] for n in LATE], g_attn_pre, g_q, g_k, g_out_a, g_out_b,
        g_attn_post, rel_bias, g_mlp_pre, g_mlp_post, g_ple)

    pairs = _pair_sums(["w_in"], [grads["w_in"]])
    grads["w_in"] = _chip_sums(["w_in"], pairs, _scatter_to_chips(pairs))[0]
    halves = [grads[n] for n in MATRICES]
    grad_w = {}
    for n, mine, theirs in zip(MATRICES, halves, _exchange_halves(halves)):
        half = mine.shape[0]
        g = jnp.zeros((2 * half, mine.shape[1]), F32)
        g = lax.dynamic_update_slice_in_dim(g, mine, c * half, axis=0)
        grad_w[n] = lax.dynamic_update_slice_in_dim(g, theirs, (1 - c) * half, axis=0)

    small_like = {n: weights[n] for n in SMALL}
    reduced = _allreduce_small(_pack_small({n: grads[n] for n in SMALL}, extra=loss))
    grad_small, loss_total = _unpack_small(reduced, small_like)

    delta, new_m, new_v = {}, {}, {}
    for n in MATRICES:
        d, nm, nv = _adamw(shards[n], grad_w[n], given["m_" + n][0], given["v_" + n][0], "adamw_" + n)
        delta[n], new_m[n], new_v[n] = d[None], nm[None], nv[None]
        grad_w[n] = grad_w[n][None]
    d, nm, nv = _adamw(_pack_small(small_like), reduced, _pack_small({n: given["m_" + n] for n in SMALL}),
                       _pack_small({n: given["v_" + n] for n in SMALL}), "adamw_small")
    d_small, nm_small, nv_small = (_unpack_small(a, small_like)[0] for a in (d, nm, nv))
    for n in SMALL:
        grad_w[n], delta[n], new_m[n], new_v[n] = grad_small[n], d_small[n], nm_small[n], nv_small[n]

    return (loss_total, grad_x[None], *[grad_w[n] for n in WEIGHT_ORDER], *[delta[n] for n in WEIGHT_ORDER],
            *[new_m[n] for n in WEIGHT_ORDER], *[new_v[n] for n in WEIGHT_ORDER])
```

```python
import functools
import math

import jax
import jax.numpy as jnp
from jax import lax
from jax.experimental import pallas as pl
from jax.experimental.pallas import tpu as pltpu

F32 = jnp.float32
BF16 = jnp.bfloat16

D_MODEL = 1024
HEAD_DIM = 64
N_HEADS_A = 8
N_KV_A = 2
GROUP_A = N_HEADS_A // N_KV_A
N_HEADS_B = 8
D_A = N_HEADS_A * HEAD_DIM
D_KV_A = N_KV_A * HEAD_DIM
D_B = N_HEADS_B * HEAD_DIM
D_IN = D_A + 2 * D_KV_A + 3 * D_B
D_FF = 4 * D_MODEL
D_PLE = 256
GRID_W = 64
ROPE_THETA = 10000.0
DILATIONS = (1, 4, 16)
HALF_WIN = 64
N_BUCKETS = 32
MAX_DISTANCE = 1024
EPS = 1e-6
NEG_BIG = -1e30
Q_SCALE = HEAD_DIM ** -0.5

ADAM_LR = 0.001
ADAM_B1 = 0.9
ADAM_B2 = 0.999
ADAM_EPS = 1e-08
ADAM_WD = 0.01
ADAM_STEP = 10

N_CHIPS = 4
MESH = pl.DeviceIdType.MESH

ROW_TILE = 512
ATT_TQ = 256
ATT_TQ_BWD = 512
ATT_TK_FWD = 2048
ATT_UNROLL_FWD = 8
ATT_TK_BWD = 512
ATT_UNROLL_BWD = 8
SWA_TQ = 128
SWA_MIN_BLOCK = 1024
DW_TS = 2048
VMEM_LIMIT = 56 * 1024 * 1024

NT = (((1,), (1,)), ((), ()))
TN = (((0,), (0,)), ((), ()))


def _cparams(sem=None, vmem=VMEM_LIMIT):
    return pltpu.CompilerParams(dimension_semantics=sem, vmem_limit_bytes=vmem)


def _full(shape):
    n = len(shape)
    return pl.BlockSpec(shape, lambda *_: (0,) * n)


def _rows(tm, width):
    return pl.BlockSpec((tm, width), lambda i: (i, 0))


def _split3(a):
    a1 = a.astype(BF16)
    r = a - a1.astype(F32)
    a2 = r.astype(BF16)
    a3 = (r - a2.astype(F32)).astype(BF16)
    return a1, a2, a3


def _xdot(a, sel):
    a1, a2, a3 = _split3(a)
    d = lambda p: jnp.dot(p, sel, preferred_element_type=F32)
    return d(a1) + d(a2) + d(a3)


def _mm(a, b):
    return jnp.dot(a, b, preferred_element_type=F32)


def _mm_nt(a, b):
    return lax.dot_general(a, b, NT, preferred_element_type=F32)


def _mm_tn(a, b):
    return lax.dot_general(a, b, TN, preferred_element_type=F32)


def _rms_stats(x):
    r = lax.rsqrt(jnp.mean(x * x, axis=-1, keepdims=True) + EPS)
    return x * r, r


def _rms_bwd(dy, xh, r, g):
    gdy = dy * g
    dx = r * (gdy - xh * jnp.mean(gdy * xh, axis=-1, keepdims=True))
    dg = jnp.sum(dy * xh, axis=0, keepdims=True)
    return dx, dg


def _acc_out(ref, val):
    @pl.when(pl.program_id(0) == 0)
    def _():
        ref[...] = jnp.zeros_like(ref)

    ref[...] += val


def _swap_halves(x, first_half):
    return jnp.where(first_half, pltpu.roll(x, 96, 1), pltpu.roll(x, 32, 1))


def _first_half_mask(shape):
    return (lax.broadcasted_iota(jnp.int32, shape, 1) % HEAD_DIM) < (HEAD_DIM // 2)


def _rope_tables(s_len):
    t = jnp.arange(s_len)
    row = (t // GRID_W).astype(F32)
    col = (t % GRID_W).astype(F32)
    n_axis = HEAD_DIM // 4
    inv_freq = ROPE_THETA ** (-jnp.arange(n_axis, dtype=F32) / n_axis)
    ang = jnp.concatenate([row[:, None] * inv_freq, col[:, None] * inv_freq], axis=-1)
    c, s = jnp.cos(ang), jnp.sin(ang)
    cc = jnp.concatenate([c, c, c, c], axis=-1)
    ss = jnp.concatenate([-s, s, -s, s], axis=-1)
    return cc, ss


def _group_ones(width):
    i = jnp.arange(width)
    return (i[:, None] // HEAD_DIM == i[None, :] // HEAD_DIM).astype(BF16)


def _t5_bucket(rel):
    nb = N_BUCKETS // 2
    max_exact = nb // 2
    side = jnp.where(rel > 0, nb, 0)
    n = jnp.abs(rel)
    large = max_exact + (jnp.log(jnp.maximum(n, max_exact).astype(F32) / max_exact)
                         / math.log(MAX_DISTANCE / max_exact) * (nb - max_exact)).astype(jnp.int32)
    large = jnp.minimum(large, nb - 1)
    return side + jnp.where(n < max_exact, n, large)


def _in_window(tq):
    qi = jnp.arange(tq)
    kj = jnp.arange(tq + 2 * HALF_WIN)
    return jnp.abs(kj[None, :] - HALF_WIN - qi[:, None]) <= HALF_WIN


def _bucket_onehot(tq, dilation):
    qi = jnp.arange(tq)
    kj = jnp.arange(tq + 2 * HALF_WIN)
    rel = kj[None, :] - HALF_WIN - qi[:, None]
    bucket = _t5_bucket(rel * dilation).reshape(-1)
    return (bucket[:, None] == jnp.arange(128)[None, :]).astype(BF16)


def _in_proj(x, g1, w_in, cc, ss, gq2, gk2, ones128):
    s_len = x.shape[0]
    tm = min(ROW_TILE, s_len)

    def body(x_ref, g_ref, w_ref, cc_ref, ss_ref, gq_ref, gk_ref, one_ref,
             xn_ref, qpre_ref, kpre_ref, qa_ref, kv_ref, qb_ref, kb_ref, vb_ref):
        xh, _ = _rms_stats(x_ref[...])
        xn = (xh * g_ref[...]).astype(BF16)
        xn_ref[...] = xn
        proj = _mm(xn, w_ref[...])
        first_half = _first_half_mask((tm, 128))
        ones = one_ref[...]
        cc_t, ss_t = cc_ref[...], ss_ref[...]

        def norm_rope(xc, gain):
            ms = _xdot(xc * xc, ones) * (1.0 / HEAD_DIM)
            y = xc * lax.rsqrt(ms + EPS) * gain
            return y * cc_t + _swap_halves(y, first_half) * ss_t

        qpre_ref[...] = proj[:, :D_A]
        kpre_ref[...] = proj[:, D_A:D_A + D_KV_A]
        for c in range(D_A // 128):
            y = norm_rope(proj[:, 128 * c:128 * (c + 1)], gq_ref[...])
            qa_ref[:, 128 * c:128 * (c + 1)] = (y * Q_SCALE).astype(BF16)
        ka = norm_rope(proj[:, D_A:D_A + D_KV_A], gk_ref[...])
        o = D_A + D_KV_A
        va = proj[:, o:o + D_KV_A]
        low = _low_lanes(tm)
        kv_ref[0] = jnp.where(low, ka, pltpu.roll(va, HEAD_DIM, 1)).astype(BF16)
        kv_ref[1] = jnp.where(low, pltpu.roll(ka, HEAD_DIM, 1), va).astype(BF16)
        o += D_KV_A
        qb_ref[...] = (proj[:, o:o + D_B] * Q_SCALE).astype(BF16)
        kb_ref[...] = proj[:, o + D_B:o + 2 * D_B].astype(BF16)
        vb_ref[...] = proj[:, o + 2 * D_B:o + 3 * D_B].astype(BF16)

    sds = jax.ShapeDtypeStruct
    return pl.pallas_call(
        body, name="in_proj", grid=(s_len // tm,),
        in_specs=[_rows(tm, D_MODEL), _full((1, D_MODEL)), _full((D_MODEL, D_IN)), _rows(tm, 128), _rows(tm, 128),
                  _full((1, 128)), _full((1, 128)), _full((128, 128))],
        out_specs=[_rows(tm, D_MODEL), _rows(tm, D_A), _rows(tm, D_KV_A), _rows(tm, D_A),
                   pl.BlockSpec((N_KV_A, tm, 128), lambda i: (0, i, 0)), _rows(tm, D_B), _rows(tm, D_B),
                   _rows(tm, D_B)],
        out_shape=[sds((s_len, D_MODEL), BF16), sds((s_len, D_A), F32), sds((s_len, D_KV_A), F32),
                   sds((s_len, D_A), BF16), sds((N_KV_A, s_len, 128), BF16),
                   sds((s_len, D_B), BF16), sds((s_len, D_B), BF16), sds((s_len, D_B), BF16)],
        compiler_params=_cparams(("parallel",)),
    )(x, g1, w_in, cc, ss, gq2, gk2, ones128)


def _stat_spec(tm):
    return pl.BlockSpec((N_HEADS_B, tm, 1), lambda i: (0, i, 0))


def _out_proj(ya, yb, x, g_a, g_b, w_out, g_post, g_mlp_pre):
    s_len = x.shape[0]
    tm = min(ROW_TILE, s_len)

    def body(ya_ref, yb_ref, x_ref, ga_ref, gb_ref, w_ref, gp_ref, gm_ref, ycat_ref, y2_ref, h1_ref, xn2_ref):
        ah, _ = _rms_stats(ya_ref[...])
        bh, _ = _rms_stats(yb_ref[...])
        ycat = jnp.concatenate([ah * ga_ref[...], bh * gb_ref[...]], axis=-1).astype(BF16)
        ycat_ref[...] = ycat
        y2 = _mm(ycat, w_ref[...])
        y2_ref[...] = y2
        y2h, _ = _rms_stats(y2)
        h1 = x_ref[...] + y2h * gp_ref[...]
        h1_ref[...] = h1
        h1h, _ = _rms_stats(h1)
        xn2_ref[...] = (h1h * gm_ref[...]).astype(BF16)

    sds = jax.ShapeDtypeStruct
    return pl.pallas_call(
        body, name="out_proj", grid=(s_len // tm,),
        in_specs=[_rows(tm, D_A), _rows(tm, D_B), _rows(tm, D_MODEL), _full((1, D_A)), _full((1, D_B)),
                  _full((D_MODEL, D_MODEL)), _full((1, D_MODEL)), _full((1, D_MODEL))],
        out_specs=[_rows(tm, D_MODEL)] * 4,
        out_shape=[sds((s_len, D_MODEL), BF16), sds((s_len, D_MODEL), F32), sds((s_len, D_MODEL), F32),
                   sds((s_len, D_MODEL), BF16)],
        compiler_params=_cparams(("parallel",)),
    )(ya, yb, x, g_a, g_b, w_out, g_post, g_mlp_pre)


def _ff1(xn2, w_ff1):
    s_len = xn2.shape[0]
    tm = min(ROW_TILE, s_len)

    def body(x_ref, w_ref, u_ref):
        u_ref[...] = _mm(x_ref[...], w_ref[...])

    return pl.pallas_call(
        body, name="ff1", grid=(s_len // tm,),
        in_specs=[_rows(tm, D_MODEL), _full((D_MODEL, D_FF))],
        out_specs=_rows(tm, D_FF),
        out_shape=jax.ShapeDtypeStruct((s_len, D_FF), F32),
        compiler_params=_cparams(("parallel",)),
    )(xn2, w_ff1)


def _ff2(u, w_ff2, h1, g_post, g_ple):
    s_len = u.shape[0]
    tm = min(ROW_TILE, s_len)

    def body(u_ref, w_ref, h1_ref, gp_ref, gl_ref, f2_ref, h2_ref, xn3_ref):
        f = jnp.square(jnp.maximum(u_ref[...], 0.0)).astype(BF16)
        f2 = _mm(f, w_ref[...])
        f2_ref[...] = f2
        f2h, _ = _rms_stats(f2)
        h2 = h1_ref[...] + f2h * gp_ref[...]
        h2_ref[...] = h2
        h2h, _ = _rms_stats(h2)
        xn3_ref[...] = (h2h * gl_ref[...]).astype(BF16)

    sds = jax.ShapeDtypeStruct
    return pl.pallas_call(
        body, name="ff2", grid=(s_len // tm,),
        in_specs=[_rows(tm, D_FF), _full((D_FF, D_MODEL)), _rows(tm, D_MODEL), _full((1, D_MODEL)),
                  _full((1, D_MODEL))],
        out_specs=[_rows(tm, D_MODEL)] * 3,
        out_shape=[sds((s_len, D_MODEL), F32), sds((s_len, D_MODEL), F32), sds((s_len, D_MODEL), BF16)],
        compiler_params=_cparams(("parallel",)),
    )(u, w_ff2, h1, g_post, g_ple)


def _ple_loss(xn3, p, h2, f2, tgt, w_gate, w_ple, g_ple, g_mlp_post):
    s_len = h2.shape[0]
    tm = min(ROW_TILE, s_len)

    def body(xn3_ref, p_ref, h2_ref, f2_ref, t_ref, wg_ref, wp_ref, gl_ref, gp_ref,
             dh2_ref, df2_ref, dgl_ref, dpp_ref, loss_ref, dgple_ref, dgpost_ref):
        gate = jax.nn.sigmoid(_mm(xn3_ref[...], wg_ref[...]))
        pp = _mm(p_ref[...].astype(BF16), wp_ref[...])
        h2 = h2_ref[...]
        err = h2 + gate * pp - t_ref[...]
        sq = jnp.sum(jnp.sum(err * err, axis=1, keepdims=True), axis=0, keepdims=True)
        _acc_out(loss_ref, sq * (0.5 / D_MODEL))
        dh3 = err * (1.0 / D_MODEL)
        dgl = (dh3 * pp) * gate * (1.0 - gate)
        dgl_b = dgl.astype(BF16)
        dgl_ref[...] = dgl_b
        dpp_ref[...] = (dh3 * gate).astype(BF16)
        dxn3 = _mm_nt(dgl_b, wg_ref[...])
        h2h, r2 = _rms_stats(h2)
        dx, dg = _rms_bwd(dxn3, h2h, r2, gl_ref[...])
        _acc_out(dgple_ref, dg)
        dh2 = dh3 + dx
        dh2_ref[...] = dh2
        f2h, rf = _rms_stats(f2_ref[...])
        df2, dg = _rms_bwd(dh2, f2h, rf, gp_ref[...])
        _acc_out(dgpost_ref, dg)
        df2_ref[...] = df2.astype(BF16)

    sds = jax.ShapeDtypeStruct
    return pl.pallas_call(
        body, name="ple_loss", grid=(s_len // tm,),
        in_specs=[_rows(tm, D_MODEL), _rows(tm, D_PLE), _rows(tm, D_MODEL), _rows(tm, D_MODEL), _rows(tm, D_MODEL),
                  _full((D_MODEL, D_MODEL)), _full((D_PLE, D_MODEL)), _full((1, D_MODEL)), _full((1, D_MODEL))],
        out_specs=[_rows(tm, D_MODEL)] * 3 + [_rows(tm, D_MODEL), _full((1, 1)), _full((1, D_MODEL)),
                                              _full((1, D_MODEL))],
        out_shape=[sds((s_len, D_MODEL), F32), sds((s_len, D_MODEL), BF16), sds((s_len, D_MODEL), BF16),
                   sds((s_len, D_MODEL), BF16), sds((1, 1), F32), sds((1, D_MODEL), F32), sds((1, D_MODEL), F32)],
        compiler_params=_cparams(("arbitrary",)),
    )(xn3, p, h2, f2, tgt, w_gate, w_ple, g_ple, g_mlp_post)


def _ff2_bwd(df2, w_ff2_t, u):
    s_len = u.shape[0]
    tm = min(ROW_TILE, s_len)

    def body(d_ref, w_ref, u_ref, du_ref):
        df = _mm(d_ref[...], w_ref[...])
        du_ref[...] = (df * (2.0 * jnp.maximum(u_ref[...], 0.0))).astype(BF16)

    return pl.pallas_call(
        body, name="ff2_bwd", grid=(s_len // tm,),
        in_specs=[_rows(tm, D_MODEL), _full((D_MODEL, D_FF)), _rows(tm, D_FF)],
        out_specs=_rows(tm, D_FF),
        out_shape=jax.ShapeDtypeStruct((s_len, D_FF), BF16),
        compiler_params=_cparams(("parallel",)),
    )(df2, w_ff2_t, u)


def _ff1_bwd(du, w_ff1, dh2, h1, y2, g_mlp_pre, g_post):
    s_len = du.shape[0]
    tm = min(ROW_TILE, s_len)

    def body(du_ref, w_ref, dh2_ref, h1_ref, y2_ref, gm_ref, gp_ref, dh1_ref, dy2_ref, dgm_ref, dgp_ref):
        dxn2 = _mm_nt(du_ref[...], w_ref[...])
        h1h, r1 = _rms_stats(h1_ref[...])
        dx, dg = _rms_bwd(dxn2, h1h, r1, gm_ref[...])
        _acc_out(dgm_ref, dg)
        dh1 = dh2_ref[...] + dx
        dh1_ref[...] = dh1
        y2h, ry = _rms_stats(y2_ref[...])
        dy2, dg = _rms_bwd(dh1, y2h, ry, gp_ref[...])
        _acc_out(dgp_ref, dg)
        dy2_ref[...] = dy2.astype(BF16)

    sds = jax.ShapeDtypeStruct
    return pl.pallas_call(
        body, name="ff1_bwd", grid=(s_len // tm,),
        in_specs=[_rows(tm, D_FF), _full((D_MODEL, D_FF)), _rows(tm, D_MODEL), _rows(tm, D_MODEL),
                  _rows(tm, D_MODEL), _full((1, D_MODEL)), _full((1, D_MODEL))],
        out_specs=[_rows(tm, D_MODEL), _rows(tm, D_MODEL), _full((1, D_MODEL)), _full((1, D_MODEL))],
        out_shape=[sds((s_len, D_MODEL), F32), sds((s_len, D_MODEL), BF16), sds((1, D_MODEL), F32),
                   sds((1, D_MODEL), F32)],
        compiler_params=_cparams(("arbitrary",)),
    )(du, w_ff1, dh2, h1, y2, g_mlp_pre, g_post)


def _out_proj_bwd(dy2, w_out, ya, yb, lse_b, g_a, g_b):
    s_len = ya.shape[0]
    tm = min(ROW_TILE, s_len)

    def body(d_ref, w_ref, ya_ref, yb_ref, lse_ref, ga_ref, gb_ref, dya_ref, dyb_ref, da_ref, st_ref, dga_ref,
             dgb_ref):
        dycat = _mm_nt(d_ref[...], w_ref[...])
        lane = lax.broadcasted_iota(jnp.int32, (tm, 128), 1)
        low = lane < HEAD_DIM
        is_lse = (lane % HEAD_DIM) < (HEAD_DIM // 2)

        def head_sums(prod_chunk):
            return (jnp.sum(jnp.where(low, prod_chunk, 0.0), axis=1, keepdims=True),
                    jnp.sum(jnp.where(low, 0.0, prod_chunk), axis=1, keepdims=True))

        ya = ya_ref[...]
        yh, r = _rms_stats(ya)
        dya, dg = _rms_bwd(dycat[:, :D_A], yh, r, ga_ref[...])
        _acc_out(dga_ref, dg)
        dya_ref[...] = dya
        prod = dya * ya
        for c in range(D_A // 128):
            da_ref[2 * c], da_ref[2 * c + 1] = head_sums(prod[:, 128 * c:128 * (c + 1)])

        yb = yb_ref[...]
        yh, r = _rms_stats(yb)
        dyb, dg = _rms_bwd(dycat[:, D_A:], yh, r, gb_ref[...])
        _acc_out(dgb_ref, dg)
        dyb_ref[...] = dyb.astype(BF16)
        prod = dyb * yb
        for c in range(D_B // 128):
            sl = slice(128 * c, 128 * (c + 1))
            d_lo, d_hi = head_sums(prod[:, sl])
            st_ref[:, sl] = jnp.where(is_lse, lse_ref[:, sl], jnp.where(low, d_lo, d_hi))

    sds = jax.ShapeDtypeStruct
    return pl.pallas_call(
        body, name="out_proj_bwd", grid=(s_len // tm,),
        in_specs=[_rows(tm, D_MODEL), _full((D_MODEL, D_MODEL)), _rows(tm, D_A), _rows(tm, D_B), _rows(tm, D_B),
                  _full((1, D_A)), _full((1, D_B))],
        out_specs=[_rows(tm, D_A), _rows(tm, D_B), _stat_spec(tm), _rows(tm, D_B), _full((1, D_A)),
                   _full((1, D_B))],
        out_shape=[sds((s_len, D_A), F32), sds((s_len, D_B), BF16), sds((N_HEADS_A, s_len, 1), F32),
                   sds((s_len, D_B), F32), sds((1, D_A), F32), sds((1, D_B), F32)],
        compiler_params=_cparams(("arbitrary",)),
    )(dy2, w_out, ya, yb, lse_b, g_a, g_b)


def _in_proj_bwd(dqr, dkv, dqb, dkb, dvb, qpre, kpre, x, dh1, g1, w_in, cc, ss, gq2, gk2, ones128):
    s_len = x.shape[0]
    tm = min(ROW_TILE, s_len)

    def body(dqr_ref, dkv_ref, dq0, dq1, dq2, dk0, dk1, dk2, dv0, dv1, dv2, qpre_ref, kpre_ref, x_ref,
             dh1_ref, g_ref, w_ref, cc_ref, ss_ref, gq_ref, gk_ref, one_ref, dproj_ref, gx_ref, dg1_ref, dgq_ref,
             dgk_ref):
        low = _low_lanes(tm)
        dkr = jnp.where(low, dkv_ref[0], pltpu.roll(dkv_ref[1], HEAD_DIM, 1))
        dva = jnp.where(low, pltpu.roll(dkv_ref[0], HEAD_DIM, 1), dkv_ref[1])
        first_half = _first_half_mask((tm, 128))
        ones = one_ref[...]
        cc_t, ss_t = cc_ref[...], ss_ref[...]

        def norm_rope_bwd(dy, xc, gain):
            dn = dy * cc_t - _swap_halves(dy, first_half) * ss_t
            r = lax.rsqrt(_xdot(xc * xc, ones) * (1.0 / HEAD_DIM) + EPS)
            xh = xc * r
            gdy = dn * gain
            dx = r * (gdy - xh * (_xdot(gdy * xh, ones) * (1.0 / HEAD_DIM)))
            return dx, jnp.sum(dn * xh, axis=0, keepdims=True)

        dgq = jnp.zeros((1, 128), F32)
        parts = []
        for c in range(D_A // 128):
            sl = slice(128 * c, 128 * (c + 1))
            dx, dg = norm_rope_bwd(dqr_ref[:, sl] * Q_SCALE, qpre_ref[:, sl], gq_ref[...])
            parts.append(dx)
            dgq = dgq + dg
        dxk, dgk = norm_rope_bwd(dkr, kpre_ref[...], gk_ref[...])
        _acc_out(dgq_ref, dgq)
        _acc_out(dgk_ref, dgk)
        total = lambda a, b, c: a[...].astype(F32) + b[...].astype(F32) + c[...].astype(F32)
        parts += [dxk, dva, total(dq0, dq1, dq2) * Q_SCALE, total(dk0, dk1, dk2), total(dv0, dv1, dv2)]
        dproj = jnp.concatenate(parts, axis=-1).astype(BF16)
        dproj_ref[...] = dproj
        dxn = _mm_nt(dproj, w_ref[...])
        xh, r = _rms_stats(x_ref[...])
        dx, dg = _rms_bwd(dxn, xh, r, g_ref[...])
        _acc_out(dg1_ref, dg)
        gx_ref[...] = dh1_ref[...] + dx

    sds = jax.ShapeDtypeStruct
    return pl.pallas_call(
        body, name="in_proj_bwd", grid=(s_len // tm,),
        in_specs=[_rows(tm, D_A), pl.BlockSpec((N_KV_A, tm, 128), lambda i: (0, i, 0))] + [_rows(tm, D_B)] * 9
                 + [_rows(tm, D_A), _rows(tm, D_KV_A), _rows(tm, D_MODEL), _rows(tm, D_MODEL),
                    _full((1, D_MODEL)), _full((D_MODEL, D_IN)), _rows(tm, 128), _rows(tm, 128), _full((1, 128)),
                    _full((1, 128)), _full((128, 128))],
        out_specs=[_rows(tm, D_IN), _rows(tm, D_MODEL), _full((1, D_MODEL)), _full((1, 128)), _full((1, 128))],
        out_shape=[sds((s_len, D_IN), BF16), sds((s_len, D_MODEL), F32), sds((1, D_MODEL), F32),
                   sds((1, 128), F32), sds((1, 128), F32)],
        compiler_params=_cparams(("arbitrary",)),
    )(dqr, dkv, *dqb, *dkb, *dvb, qpre, kpre, x, dh1, g1, w_in, cc, ss, gq2, gk2, ones128)


def _dw(a, b, name, relu2=False):
    s_len, ka = a.shape
    n = b.shape[1]
    ts = min(DW_TS, s_len)
    bk = min(ka, 1024)
    bn = n if n % 1024 else 1024

    def body(a_ref, b_ref, o_ref):
        @pl.when(pl.program_id(2) == 0)
        def _():
            o_ref[...] = jnp.zeros_like(o_ref)

        av = a_ref[...]
        if relu2:
            av = jnp.square(jnp.maximum(av, 0.0))
        o_ref[...] += _mm_tn(av.astype(BF16), b_ref[...])

    return pl.pallas_call(
        body, name=name, grid=(ka // bk, n // bn, s_len // ts),
        in_specs=[pl.BlockSpec((ts, bk), lambda i, j, k: (k, i)), pl.BlockSpec((ts, bn), lambda i, j, k: (k, j))],
        out_specs=pl.BlockSpec((bk, bn), lambda i, j, k: (i, j)),
        out_shape=jax.ShapeDtypeStruct((ka, n), F32),
        compiler_params=_cparams(("parallel", "parallel", "arbitrary")),
    )(a, b)


def _stack_heads(block, low, data_low):
    parts = []
    for c in range(GROUP_A // 2):
        chunk = block[:, 128 * c:128 * (c + 1)]
        swapped = pltpu.roll(chunk, HEAD_DIM, 1)
        for h_low in (chunk, swapped) if data_low else (swapped, chunk):
            parts.append(jnp.where(low, h_low, 0.0) if data_low else jnp.where(low, 0.0, h_low))
    return jnp.concatenate(parts, axis=0).astype(BF16)


def _unstack_heads(stacked, low, tq, data_low):
    chunks = []
    for c in range(GROUP_A // 2):
        even = stacked[2 * c * tq:(2 * c + 1) * tq]
        odd = stacked[(2 * c + 1) * tq:(2 * c + 2) * tq]
        if data_low:
            chunks.append(jnp.where(low, even, pltpu.roll(odd, HEAD_DIM, 1)))
        else:
            chunks.append(jnp.where(low, pltpu.roll(even, HEAD_DIM, 1), odd))
    return chunks


def _attn_a_fwd(qa, kv):
    s_len = kv.shape[1]
    tq = min(ATT_TQ, s_len)
    tk = min(ATT_TK_FWD, s_len)
    rows = GROUP_A * tq

    def body(q_ref, kv_ref, o_ref, lse_ref):
        low = _low_lanes(tq)
        low_k = _low_lanes(tk)
        q = _stack_heads(q_ref[...].astype(F32), low, data_low=True)

        def block(j, m, acc):
            kvj = kv_ref[0, pl.ds(pl.multiple_of(j * tk, tk), tk), :]
            s = _mm_nt(q, kvj)
            m_new = jnp.maximum(m, jnp.max(s, axis=1, keepdims=True))
            p = jnp.exp(s - m_new).astype(BF16)
            return m_new, jnp.exp(m - m_new) * acc + _mm(p, jnp.where(low_k, jnp.ones_like(kvj), kvj))

        def step(j, carry):
            for u in range(unroll):
                carry = block(unroll * j + u, *carry)
            return carry

        unroll = math.gcd(s_len // tk, ATT_UNROLL_FWD)
        init = (jnp.full((rows, 1), -jnp.inf, F32), jnp.zeros((rows, 128), F32))
        m, acc = lax.fori_loop(0, s_len // (tk * unroll), step, init)
        for c, chunk in enumerate(_unstack_heads(acc / pltpu.roll(acc, HEAD_DIM, 1), low, tq, data_low=False)):
            o_ref[:, 128 * c:128 * (c + 1)] = chunk
        lse_ref[...] = (m + jnp.log(acc[:, :1])).reshape(GROUP_A, tq, 1)

    return pl.pallas_call(
        body, name="attn_a_fwd", grid=(N_KV_A, s_len // tq),
        in_specs=[pl.BlockSpec((tq, 256), lambda g, i: (i, g)),
                  pl.BlockSpec((1, s_len, 128), lambda g, i: (g, 0, 0))],
        out_specs=[pl.BlockSpec((tq, 256), lambda g, i: (i, g)),
                   pl.BlockSpec((GROUP_A, tq, 1), lambda g, i: (g, i, 0))],
        out_shape=[jax.ShapeDtypeStruct((s_len, D_A), F32),
                   jax.ShapeDtypeStruct((N_HEADS_A, s_len, 1), F32)],
        compiler_params=_cparams(("parallel", "parallel")),
    )(qa, kv)


def _attn_a_bwd(qa, dya, kv, lse, delta):
    s_len = kv.shape[1]
    tq = min(ATT_TQ_BWD, s_len)
    tk = min(ATT_TK_BWD, s_len)
    rows = GROUP_A * tq

    def body(q_ref, do_ref, kv_ref, lse_ref, dl_ref, dq_ref, dkv_ref):
        @pl.when(pl.program_id(1) == 0)
        def _():
            dkv_ref[...] = jnp.zeros_like(dkv_ref)

        low = _low_lanes(tq)
        q = _stack_heads(q_ref[...].astype(F32), low, data_low=True)
        do = _stack_heads(do_ref[...], low, data_low=False)
        lse_t = lse_ref[...].reshape(rows, 1)
        dl_t = dl_ref[...].reshape(rows, 1)
        q_t = q.T
        do_t = do.T

        def block(j, dq):
            span = pl.ds(pl.multiple_of(j * tk, tk), tk)
            kvj = kv_ref[0, span, :]
            p = jnp.exp(_mm_nt(q, kvj) - lse_t)
            ds = (p * (_mm_nt(do, kvj) - dl_t)).astype(BF16)
            dkv_ref[0, :, span] += _mm(q_t, ds) + _mm(do_t, p.astype(BF16))
            return dq + _mm(ds, kvj)

        def step(j, dq):
            for u in range(unroll):
                dq = block(unroll * j + u, dq)
            return dq

        unroll = math.gcd(s_len // tk, ATT_UNROLL_BWD)
        dq = lax.fori_loop(0, s_len // (tk * unroll), step, jnp.zeros((rows, 128), F32))
        for c, chunk in enumerate(_unstack_heads(dq, low, tq, data_low=True)):
            dq_ref[:, 128 * c:128 * (c + 1)] = chunk

    return pl.pallas_call(
        body, name="attn_a_bwd", grid=(N_KV_A, s_len // tq),
        in_specs=[pl.BlockSpec((tq, 256), lambda g, i: (i, g)),
                  pl.BlockSpec((tq, 256), lambda g, i: (i, g)),
                  pl.BlockSpec((1, s_len, 128), lambda g, i: (g, 0, 0)),
                  pl.BlockSpec((GROUP_A, tq, 1), lambda g, i: (g, i, 0)),
                  pl.BlockSpec((GROUP_A, tq, 1), lambda g, i: (g, i, 0))],
        out_specs=[pl.BlockSpec((tq, 256), lambda g, i: (i, g)),
                   pl.BlockSpec((1, 128, s_len), lambda g, i: (g, 0, 0))],
        out_shape=[jax.ShapeDtypeStruct((s_len, D_A), F32),
                   jax.ShapeDtypeStruct((N_KV_A, 128, s_len), F32)],
        compiler_params=_cparams(("parallel", "arbitrary")),
    )(qa, dya, kv, lse, delta)


class _SwaGeometry:
    def __init__(self, s_len, r):
        self.r = r
        self.tq = SWA_TQ
        self.block = min(max(SWA_MIN_BLOCK, 2 * SWA_TQ * r), s_len)
        self.halo = HALF_WIN * r
        self.nsub = self.block // (self.tq * r)
        self.band = self.tq + 2 * HALF_WIN
        self.length = s_len // r
        self.nblk = s_len // self.block
        self.nhalo = s_len // self.halo
        assert self.nsub * self.tq * r == self.block and self.block % self.halo == 0

    def specs(self):
        per = self.block // self.halo
        cur = pl.BlockSpec((self.block, 128), lambda c, i: (i, c))
        prev = pl.BlockSpec((self.halo, 128), lambda c, i: (jnp.maximum(i * per - 1, 0), c))
        nxt = pl.BlockSpec((self.halo, 128), lambda c, i: (jnp.minimum((i + 1) * per, self.nhalo - 1), c))
        return prev, cur, nxt

    def tiles(self):
        return [(rho + self.r * j * self.tq, self.halo + rho + self.r * (j * self.tq - HALF_WIN), j)
                for j in range(self.nsub) for rho in range(self.r)]

    def own(self, start):
        return pl.ds(start, self.tq, stride=self.r)

    def around(self, start):
        return pl.ds(start, self.band, stride=self.r)

    def fill(self, dst, prev_ref, cur_ref, next_ref):
        dst[:self.halo, :] = prev_ref[...].astype(F32)
        dst[self.halo:self.halo + self.block, :] = cur_ref[...].astype(F32)
        dst[self.halo + self.block:, :] = next_ref[...].astype(F32)

    def first_position(self, j):
        return (pl.program_id(1) * self.block) // self.r + j * self.tq

    def outside(self, j, copies=1):
        pos = self.first_position(j) - HALF_WIN + lax.broadcasted_iota(jnp.int32, (1, copies * self.band), 1) % self.band
        return jnp.where((pos >= 0) & (pos < self.length), 0.0, NEG_BIG)

    def extended(self):
        return pltpu.VMEM((self.block + 2 * self.halo, 128), F32)

    def plain(self):
        return pltpu.VMEM((self.block, 128), F32)


def _low_lanes(rows):
    return lax.broadcasted_iota(jnp.int32, (rows, 128), 1) < HEAD_DIM


def _one_head(x, low, half):
    return jnp.where(low if half == 0 else jnp.logical_not(low), x, 0.0).astype(BF16)


def _two_heads(x, low):
    return jnp.concatenate([_one_head(x, low, 0), _one_head(x, low, 1)], axis=0)


def _carry_ride(base_body, n_in, n_out, n_scratch, ride, grid):
    if ride is None:
        return base_body
    n = ride.n

    def body(*refs):
        o = n_in + n
        ins, ride_ins = refs[:n_in], refs[n_in:o]
        outs, ride_outs = refs[o:o + n_out], refs[o + n_out:o + n_out + n]
        o += n_out + n
        scratch, sems = refs[o:o + n_scratch], refs[o + n_scratch:]
        at_first = (pl.program_id(0) == 0) & (pl.program_id(1) == 0)
        at_last = (pl.program_id(0) == grid[0] - 1) & (pl.program_id(1) == grid[1] - 1)

        @pl.when(at_first)
        def _():
            ride.start(ride_ins, ride_outs, sems)

        base_body(*ins, *outs, *scratch)

        @pl.when(at_last)
        def _():
            ride.finish(ride_ins, ride_outs, sems)

    return body


def _ride_call(base_body, name, grid, in_specs, out_specs, out_shape, scratch_shapes, operands, ride):
    n = 0 if ride is None else ride.n
    extra = [] if ride is None else ride.operands
    outs = pl.pallas_call(
        _carry_ride(base_body, len(in_specs), len(out_specs), len(scratch_shapes), ride, grid), name=name, grid=grid,
        in_specs=list(in_specs) + [ANY] * n, out_specs=list(out_specs) + [ANY] * n,
        out_shape=list(out_shape) + ([] if ride is None else ride.out_shape()),
        scratch_shapes=list(scratch_shapes) + ([] if ride is None else ride.scratch_shapes()),
        compiler_params=_cparams(("arbitrary", "arbitrary")),
    )(*operands, *extra)
    return outs[:len(out_specs)], outs[len(out_specs):]


def _swa_fwd(q, k, v, bias, r, so_far=None, ride=None):
    geo = _SwaGeometry(q.shape[0], r)
    prev, cur, nxt = geo.specs()

    def body(q_ref, kp, kc, kn, vp, vc, vn, b_ref, *rest):
        if so_far is None:
            o_ref, lse_ref, qf, kf, vf = rest
        else:
            o_old_ref, lse_old_ref, o_ref, lse_ref, qf, kf, vf = rest
        qf[...] = q_ref[...].astype(F32)
        geo.fill(kf, kp, kc, kn)
        geo.fill(vf, vp, vc, vn)
        tq = geo.tq
        low_q = _low_lanes(tq)
        bias = b_ref[...].reshape(2 * tq, geo.band)
        for own, around, j in geo.tiles():
            q2 = _two_heads(qf[geo.own(own), :], low_q)
            kb = kf[geo.around(around), :].astype(BF16)
            vb = vf[geo.around(around), :].astype(BF16)
            s = _mm_nt(q2, kb) + bias + geo.outside(j)
            m = jnp.max(s, axis=1, keepdims=True)
            e = jnp.exp(s - m)
            l = jnp.sum(e, axis=1, keepdims=True)
            o2 = _mm(e.astype(BF16), vb) / l
            lse2 = m + jnp.log(l)
            o_new = jnp.where(low_q, o2[:tq], o2[tq:])
            lse_new = jnp.where(low_q, lse2[:tq], lse2[tq:])
            if so_far is not None:
                o_old, lse_old = o_old_ref[geo.own(own), :], lse_old_ref[geo.own(own), :]
                top = jnp.maximum(lse_old, lse_new)
                w_old, w_new = jnp.exp(lse_old - top), jnp.exp(lse_new - top)
                o_new = (w_old * o_old + w_new * o_new) / (w_old + w_new)
                lse_new = top + jnp.log(w_old + w_new)
            o_ref[geo.own(own), :] = o_new
            lse_ref[geo.own(own), :] = lse_new

    sds = jax.ShapeDtypeStruct
    before = () if so_far is None else tuple(so_far)
    (o, lse), carried = _ride_call(
        body, "swa_fwd_%d" % r, (D_B // 128, geo.nblk),
        [cur, prev, cur, nxt, prev, cur, nxt, pl.BlockSpec((2, geo.tq, geo.band), lambda c, i: (c, 0, 0))]
        + [cur] * len(before),
        [cur, cur], [sds(q.shape, F32), sds(q.shape, F32)], [geo.plain(), geo.extended(), geo.extended()],
        (q, k, k, k, v, v, v, bias) + before, ride)
    return o, lse, carried


def _head_stats(st, half):
    lo = HEAD_DIM * half
    return st[:, lo:lo + 1], st[:, lo + HEAD_DIM // 2:lo + HEAD_DIM // 2 + 1]


def _swa_bwd_q(q, k, v, dy, st, bias, r, ride=None):
    geo = _SwaGeometry(q.shape[0], r)
    prev, cur, nxt = geo.specs()
    bias_spec = pl.BlockSpec((2, geo.tq, geo.band), lambda c, i: (c, 0, 0))

    def body(q_ref, kp, kc, kn, vp, vc, vn, dy_ref, st_ref, b_ref, dq_ref, db_ref, qf, kf, vf, dyf, dqf):
        @pl.when(pl.program_id(1) == 0)
        def _():
            db_ref[...] = jnp.zeros_like(db_ref)

        qf[...] = q_ref[...].astype(F32)
        dyf[...] = dy_ref[...].astype(F32)
        geo.fill(kf, kp, kc, kn)
        geo.fill(vf, vp, vc, vn)
        tq = geo.tq
        low_q = _low_lanes(tq)
        bias = b_ref[...].reshape(2 * tq, geo.band)
        for own, around, j in geo.tiles():
            sts = st_ref[geo.own(own), :]
            (lse0, delta0), (lse1, delta1) = _head_stats(sts, 0), _head_stats(sts, 1)
            lse = jnp.concatenate([lse0, lse1], axis=0)
            delta = jnp.concatenate([delta0, delta1], axis=0)
            kb = kf[geo.around(around), :].astype(BF16)
            vb = vf[geo.around(around), :].astype(BF16)
            s = _mm_nt(_two_heads(qf[geo.own(own), :], low_q), kb) + bias + geo.outside(j)
            p = jnp.exp(s - lse)
            ds = p * (_mm_nt(_two_heads(dyf[geo.own(own), :], low_q), vb) - delta)
            db_ref[...] += ds.reshape(2, tq, geo.band)
            dq2 = _mm(ds.astype(BF16), kb)
            dqf[geo.own(own), :] = jnp.where(low_q, dq2[:tq], dq2[tq:])
        dq_ref[...] = dqf[...].astype(BF16)

    (dq, dbias), carried = _ride_call(
        body, "swa_bwd_q_%d" % r, (D_B // 128, geo.nblk),
        [cur, prev, cur, nxt, prev, cur, nxt, cur, cur, bias_spec], [cur, bias_spec],
        [jax.ShapeDtypeStruct(q.shape, BF16), jax.ShapeDtypeStruct(bias.shape, F32)],
        [geo.plain(), geo.extended(), geo.extended(), geo.plain(), geo.plain()],
        (q, k, k, k, v, v, v, dy, st, bias), ride)
    return dq, dbias, carried


def _swa_bwd_kv(q, k, v, dy, st, bias_kv, r):
    geo = _SwaGeometry(q.shape[0], r)
    prev, cur, nxt = geo.specs()

    def body(k_ref, v_ref, qp, qc, qn, dp_, dc_, dn_, sp, sc, sn, b_ref, dk_ref, dv_ref, kf, vf, qf, dyf, stf, dkf,
             dvf):
        kf[...] = k_ref[...].astype(F32)
        vf[...] = v_ref[...].astype(F32)
        geo.fill(qf, qp, qc, qn)
        geo.fill(dyf, dp_, dc_, dn_)
        geo.fill(stf, sp, sc, sn)
        band = geo.band
        low_b = _low_lanes(band)
        bias = jnp.concatenate([b_ref[0], b_ref[1]], axis=1)
        half_lanes = HEAD_DIM // 2
        for own, around, j in geo.tiles():
            ks = kf[geo.own(own), :].astype(BF16)
            vs = vf[geo.own(own), :].astype(BF16)
            q2 = _two_heads(qf[geo.around(around), :], low_b)
            dy2 = _two_heads(dyf[geo.around(around), :], low_b)
            st_t = stf[geo.around(around), :].T
            lse = jnp.concatenate([st_t[:1, :], st_t[HEAD_DIM:HEAD_DIM + 1, :]], axis=1)
            delta = jnp.concatenate([st_t[half_lanes:half_lanes + 1, :],
                                     st_t[HEAD_DIM + half_lanes:HEAD_DIM + half_lanes + 1, :]], axis=1)
            p = jnp.exp(_mm_nt(ks, q2) + bias + (geo.outside(j, copies=2) - lse))
            ds = p * (_mm_nt(vs, dy2) - delta)
            dvf[geo.own(own), :] = _mm(p.astype(BF16), dy2)
            dkf[geo.own(own), :] = _mm(ds.astype(BF16), q2)
        dk_ref[...] = dkf[...].astype(BF16)
        dv_ref[...] = dvf[...].astype(BF16)

    return pl.pallas_call(
        body, name="swa_bwd_kv_%d" % r, grid=(D_B // 128, geo.nblk),
        in_specs=[cur, cur, prev, cur, nxt, prev, cur, nxt, prev, cur, nxt,
                  pl.BlockSpec((2, geo.tq, geo.band), lambda c, i: (c, 0, 0))],
        out_specs=[cur, cur],
        out_shape=[jax.ShapeDtypeStruct(q.shape, BF16), jax.ShapeDtypeStruct(q.shape, BF16)],
        scratch_shapes=[geo.plain(), geo.plain(), geo.extended(), geo.extended(), geo.extended(), geo.plain(),
                        geo.plain()],
        compiler_params=_cparams(("parallel", "parallel")),
    )(k, v, q, q, q, dy, dy, dy, st, st, st, bias_kv)


BIAS_ROWS = 16
BIAS_TN = 4096


def _bias_tiles(onehot, rel_bias_t):
    n = onehot.shape[0]

    def body(oh_ref, rb_ref, o_ref):
        o_ref[...] = sum(_mm_nt(piece, oh_ref[...]) for piece in _split3(rb_ref[...]))

    return pl.pallas_call(
        body, name="bias_tiles", grid=(n // BIAS_TN,),
        in_specs=[_rows(BIAS_TN, 128), _full((BIAS_ROWS, 128))],
        out_specs=pl.BlockSpec((BIAS_ROWS, BIAS_TN), lambda i: (0, i)),
        out_shape=jax.ShapeDtypeStruct((BIAS_ROWS, n), F32),
        compiler_params=_cparams(("parallel",)),
    )(onehot, rel_bias_t)


def _bias_bwd(onehot, dbias_rows, so_far, r):
    n = onehot.shape[0]

    def body(oh, d, prev_ref, g_ref):
        @pl.when(pl.program_id(0) == 0)
        def _():
            g_ref[...] = prev_ref[...]

        hi, lo, _ = _split3(d[...])
        g_ref[...] += _mm(hi, oh[...]) + _mm(lo, oh[...])

    return pl.pallas_call(
        body, name="bias_bwd_%d" % r, grid=(n // BIAS_TN,),
        in_specs=[_rows(BIAS_TN, 128), pl.BlockSpec((BIAS_ROWS, BIAS_TN), lambda i: (0, i)), _full((BIAS_ROWS, 128))],
        out_specs=_full((BIAS_ROWS, 128)),
        out_shape=jax.ShapeDtypeStruct((BIAS_ROWS, 128), F32),
        compiler_params=_cparams(("arbitrary",)),
    )(onehot, dbias_rows, so_far)


LATE = ("w_out", "w_ff1", "w_ff2", "w_ple_gate", "w_ple_proj")


def _local_step(x, p, tgt, w_in, late_shards, g_attn_pre, g_q, g_k, g_out_a, g_out_b, g_attn_post, rel_bias,
                g_mlp_pre, g_mlp_post, g_ple):
    s_len = x.shape[0]
    cc, ss = _rope_tables(s_len)
    gq2 = jnp.concatenate([g_q, g_q], axis=-1)
    gk2 = jnp.concatenate([g_k, g_k], axis=-1)
    ones128 = _group_ones(128)
    rel_bias_t = jnp.zeros((BIAS_ROWS, 128), F32).at[:N_HEADS_B, :N_BUCKETS].set(rel_bias.T)

    xn1, qpre, kpre, qa, kv, qb, kb, vb = _in_proj(x, g_attn_pre, w_in, cc, ss, gq2, gk2, ones128)
    ya, lse_a = _attn_a_fwd(qa, kv)

    tiles, joint = [], None
    for r in DILATIONS:
        tq = SWA_TQ
        onehot = _bucket_onehot(tq, r)
        bias = _bias_tiles(onehot, rel_bias_t)[:N_HEADS_B].reshape(N_HEADS_B, tq, tq + 2 * HALF_WIN)
        bias = jnp.where(_in_window(tq), bias, NEG_BIG)
        yb, lse_b, gathered = _swa_fwd(qb, kb, vb, bias, r, joint,
                                       _GatherRide(late_shards) if r == DILATIONS[-1] else None)
        tiles.append((onehot, bias))
        joint = (yb, lse_b)
    w_out, w_ff1, w_ff2, w_gate, w_ple = (_whole(n, g, mine) for n, g, mine in zip(LATE, gathered, late_shards))

    ycat, y2, h1, xn2 = _out_proj(ya, yb, x, g_out_a, g_out_b, w_out, g_attn_post, g_mlp_pre)
    u = _ff1(xn2, w_ff1)
    f2, h2, xn3 = _ff2(u, w_ff2, h1, g_mlp_post, g_ple)
    dh2, df2, dgl, dpp, loss, dg_ple, dg_mlp_post = _ple_loss(xn3, p, h2, f2, tgt, w_gate, w_ple, g_ple, g_mlp_post)

    grads = {"g_ple": dg_ple, "g_mlp_post": dg_mlp_post}
    grads["w_ple_gate"] = _dw(xn3, dgl, "dw_gate")
    grads["w_ple_proj"] = _dw(p, dpp, "dw_ple")
    grads["w_ff2"] = _dw(u, df2, "dw_ff2", relu2=True)
    du = _ff2_bwd(df2, w_ff2.T, u)
    grads["w_ff1"] = _dw(xn2, du, "dw_ff1")
    dh1, dy2, grads["g_mlp_pre"], grads["g_attn_post"] = _ff1_bwd(du, w_ff1, dh2, h1, y2, g_mlp_pre, g_attn_post)
    grads["w_out"] = _dw(ycat, dy2, "dw_out")
    dya, dyb, delta_a, st_b, grads["g_out_a"], grads["g_out_b"] = _out_proj_bwd(dy2, w_out, ya, yb, lse_b, g_out_a,
                                                                              g_out_b)

    pairs = _pair_sums(LATE, [grads[n] for n in LATE])

    dqr, dkv_t = _attn_a_bwd(qa, dya, kv, lse_a, delta_a)
    dkv_a = dkv_t.transpose(0, 2, 1)

    dqs, dks, dvs = [], [], []
    d_rel = jnp.zeros((BIAS_ROWS, 128), F32)
    for r, (onehot, bias) in zip(DILATIONS, tiles):
        dq_r, dbias, scattered = _swa_bwd_q(qb, kb, vb, dyb, st_b, bias, r,
                                            _ScatterRide(pairs) if r == DILATIONS[0] else None)
        if scattered:
            for n, half in zip(LATE, _chip_sums(LATE, pairs, scattered)):
                grads[n] = half
        bias_kv = jnp.flip(bias, axis=(1, 2))
        dk_r, dv_r = _swa_bwd_kv(qb, kb, vb, dyb, st_b, bias_kv, r)
        dbias_rows = jnp.pad(dbias.reshape(N_HEADS_B, -1), ((0, BIAS_ROWS - N_HEADS_B), (0, 0)))
        d_rel = _bias_bwd(onehot, dbias_rows, d_rel, r)
        dqs.append(dq_r)
        dks.append(dk_r)
        dvs.append(dv_r)
    grads["rel_bias"] = d_rel[:N_HEADS_B, :N_BUCKETS].T

    dproj, grad_x, grads["g_attn_pre"], dgq2, dgk2 = _in_proj_bwd(
        dqr, dkv_a, dqs, dks, dvs, qpre, kpre, x, dh1, g_attn_pre, w_in, cc, ss, gq2, gk2, ones128)
    grads["g_q"] = dgq2[:, :HEAD_DIM] + dgq2[:, HEAD_DIM:]
    grads["g_k"] = dgk2[:, :HEAD_DIM] + dgk2[:, HEAD_DIM:]
    grads["w_in"] = _dw(xn1, dproj, "dw_in")
    return loss, grad_x, grads


ANY = pl.BlockSpec(memory_space=pl.ANY)


def _position():
    return lax.axis_index("x"), lax.axis_index("y"), lax.axis_index("c")


def _other_chips(x, y):
    return [(2 * (1 - x) + y, (1 - x, y)), (2 * x + (1 - y), (x, 1 - y)), (2 * (1 - x) + (1 - y), (1 - x, 1 - y))]


def _cast_shards(shards):
    def body(*refs):
        n = len(refs) // 2
        for i_ref, o_ref in zip(refs[:n], refs[n:]):
            o_ref[...] = i_ref[...].astype(BF16)

    return pl.pallas_call(
        body, name="cast_shards",
        in_specs=[pl.BlockSpec(memory_space=pltpu.VMEM)] * len(shards),
        out_specs=[pl.BlockSpec(memory_space=pltpu.VMEM)] * len(shards),
        out_shape=[jax.ShapeDtypeStruct(s.shape, BF16) for s in shards],
        compiler_params=_cparams(),
    )(*shards)


def _gather_weights(shards):
    n = len(shards)

    ride = _GatherRide(shards)

    def body(*refs):
        ride.start(refs[:n], refs[n:2 * n], refs[2 * n:])
        ride.finish(refs[:n], refs[n:2 * n], refs[2 * n:])

    return pl.pallas_call(
        body, name="gather_weights",
        in_specs=[ANY] * n, out_specs=[ANY] * n,
        out_shape=ride.out_shape(), scratch_shapes=ride.scratch_shapes(),
    )(*shards)


class _GatherRide:
    def __init__(self, shards):
        self.operands = list(shards)
        self.n = len(shards)

    def out_shape(self):
        return [jax.ShapeDtypeStruct((N_CHIPS,) + s.shape, s.dtype) for s in self.operands]

    def scratch_shapes(self):
        return [pltpu.SemaphoreType.DMA((3, self.n))] * 4

    @staticmethod
    def _rows(ref, core):
        half = ref.shape[0] // 2
        return pl.ds(pl.multiple_of(core * half, 16), half)

    def _ici(self, ins, outs, sems, k, a, chip):
        x, y, c = _position()
        return pltpu.make_async_remote_copy(ins[a].at[self._rows(ins[a], c), :],
                                            outs[a].at[2 * x + y, self._rows(ins[a], c), :], sems[0].at[k, a],
                                            sems[1].at[k, a], device_id=(*chip, c), device_id_type=MESH)

    def _pass_on(self, ins, outs, sems, k, a, num, core):
        x, y, c = _position()
        half = outs[a].at[num, self._rows(ins[a], core), :]
        return pltpu.make_async_remote_copy(half, half, sems[2].at[k, a], sems[3].at[k, a], device_id=(x, y, 1 - c),
                                            device_id_type=MESH)

    def start(self, ins, outs, sems):
        x, y, _ = _position()
        for k, (_, chip) in enumerate(_other_chips(x, y)):
            for a in range(self.n):
                self._ici(ins, outs, sems, k, a, chip).start()

    def finish(self, ins, outs, sems):
        x, y, c = _position()
        others = _other_chips(x, y)
        for k, (num, chip) in enumerate(others):
            for a in range(self.n):
                landed = outs[a].at[num, self._rows(ins[a], c), :]
                pltpu.make_async_remote_copy(landed, landed, sems[0].at[k, a], sems[1].at[k, a], device_id=(*chip, c),
                                             device_id_type=MESH).wait_recv()
                self._pass_on(ins, outs, sems, k, a, num, c).start()
        for k, (num, chip) in enumerate(others):
            for a in range(self.n):
                self._pass_on(ins, outs, sems, k, a, num, 1 - c).wait_recv()
        for k, (num, chip) in enumerate(others):
            for a in range(self.n):
                self._ici(ins, outs, sems, k, a, chip).wait_send()
                self._pass_on(ins, outs, sems, k, a, num, c).wait_send()


def _send_sibling_half(grads, tag):
    n = len(grads)

    def body(*refs):
        ins, outs = refs[:n], refs[n:2 * n]
        send_sems, recv_sems = refs[2 * n:]
        x, y, c = _position()
        copies = []
        for a in range(n):
            half = ins[a].shape[1] // 2
            theirs = ins[a].at[:, pl.ds(pl.multiple_of((1 - c) * half, 8), half), :]
            cp = pltpu.make_async_remote_copy(theirs, outs[a], send_sems.at[a], recv_sems.at[a],
                                              device_id=(x, y, 1 - c), device_id_type=MESH)
            cp.start()
            copies.append(cp)
        for cp in copies:
            cp.wait()

    return pl.pallas_call(
        body, name="send_sibling_half_" + tag,
        in_specs=[ANY] * n, out_specs=[ANY] * n,
        out_shape=[jax.ShapeDtypeStruct((g.shape[0], g.shape[1] // 2, g.shape[2]), g.dtype) for g in grads],
        scratch_shapes=[pltpu.SemaphoreType.DMA((n,)), pltpu.SemaphoreType.DMA((n,))],
    )(*grads)


def _scatter_to_chips(pairs):
    n = len(pairs)
    ride = _ScatterRide(pairs)

    def body(*refs):
        ride.start(refs[:n], refs[n:2 * n], refs[2 * n:])
        ride.finish(refs[:n], refs[n:2 * n], refs[2 * n:])

    return pl.pallas_call(
        body, name="scatter_to_chips",
        in_specs=[ANY] * n, out_specs=[ANY] * n,
        out_shape=ride.out_shape(), scratch_shapes=ride.scratch_shapes(),
    )(*pairs)


class _ScatterRide:
    def __init__(self, pairs):
        self.operands = list(pairs)
        self.n = len(pairs)

    def out_shape(self):
        return [jax.ShapeDtypeStruct(g.shape, g.dtype) for g in self.operands]

    def scratch_shapes(self):
        return [pltpu.SemaphoreType.DMA((3, self.n))] * 2

    @staticmethod
    def _copy(ins, outs, sems, k, a, src_slot, dst_slot, chip):
        _, _, c = _position()
        return pltpu.make_async_remote_copy(ins[a].at[src_slot], outs[a].at[dst_slot], sems[0].at[k, a],
                                            sems[1].at[k, a], device_id=(*chip, c), device_id_type=MESH)

    def start(self, ins, outs, sems):
        x, y, _ = _position()
        for k, (num, chip) in enumerate(_other_chips(x, y)):
            for a in range(self.n):
                self._copy(ins, outs, sems, k, a, num, 2 * x + y, chip).start()

    def finish(self, ins, outs, sems):
        x, y, _ = _position()
        for k, (num, chip) in enumerate(_other_chips(x, y)):
            for a in range(self.n):
                self._copy(ins, outs, sems, k, a, 2 * x + y, num, chip).wait_recv()
        for k, (num, chip) in enumerate(_other_chips(x, y)):
            for a in range(self.n):
                self._copy(ins, outs, sems, k, a, num, 2 * x + y, chip).wait_send()


def _exchange_halves(halves):
    n = len(halves)

    def body(*refs):
        ins, outs = refs[:n], refs[n:2 * n]
        send_sems, recv_sems = refs[2 * n:]
        x, y, c = _position()
        copies = []
        for a in range(n):
            cp = pltpu.make_async_remote_copy(ins[a], outs[a], send_sems.at[a], recv_sems.at[a],
                                              device_id=(x, y, 1 - c), device_id_type=MESH)
            cp.start()
            copies.append(cp)
        for cp in copies:
            cp.wait()

    return pl.pallas_call(
        body, name="exchange_halves",
        in_specs=[ANY] * n, out_specs=[ANY] * n,
        out_shape=[jax.ShapeDtypeStruct(h.shape, h.dtype) for h in halves],
        scratch_shapes=[pltpu.SemaphoreType.DMA((n,)), pltpu.SemaphoreType.DMA((n,))],
    )(*halves)


def _allreduce_small(v):
    def body(v_ref, o_ref, buf, send_sems, recv_sems):
        x, y, c = _position()
        me = 4 * x + 2 * y + c
        peers = [(1 - x, y, c), (x, 1 - y, c), (x, y, 1 - c), (1 - x, 1 - y, c), (1 - x, y, 1 - c), (x, 1 - y, 1 - c),
                 (1 - x, 1 - y, 1 - c)]
        num = lambda d: 4 * d[0] + 2 * d[1] + d[2]
        buf[me] = v_ref[...]
        sends = []
        for k, peer in enumerate(peers):
            cp = pltpu.make_async_remote_copy(v_ref, buf.at[me], send_sems.at[k], recv_sems.at[k], device_id=peer,
                                              device_id_type=MESH)
            cp.start()
            sends.append(cp)
        for k, peer in enumerate(peers):
            pltpu.make_async_remote_copy(v_ref, buf.at[num(peer)], send_sems.at[k], recv_sems.at[k], device_id=peer,
                                         device_id_type=MESH).wait_recv()
        for cp in sends:
            cp.wait_send()
        total = buf[0]
        for d in range(1, 8):
            total = total + buf[d]
        o_ref[...] = total

    return pl.pallas_call(
        body, name="allreduce_small",
        in_specs=[pl.BlockSpec(memory_space=pltpu.VMEM)], out_specs=pl.BlockSpec(memory_space=pltpu.VMEM),
        out_shape=jax.ShapeDtypeStruct(v.shape, v.dtype),
        scratch_shapes=[pltpu.VMEM((8,) + v.shape, v.dtype), pltpu.SemaphoreType.DMA((7,)),
                        pltpu.SemaphoreType.DMA((7,))],
    )(v)


def _sum_leading(a, name):
    k, r, c = a.shape
    tr = min(r, 256)

    def body(a_ref, o_ref):
        total = a_ref[0].astype(F32)
        for i in range(1, k):
            total = total + a_ref[i].astype(F32)
        o_ref[...] = total

    return pl.pallas_call(
        body, name=name, grid=(r // tr,),
        in_specs=[pl.BlockSpec((k, tr, c), lambda i: (0, i, 0))],
        out_specs=pl.BlockSpec((tr, c), lambda i: (i, 0)),
        out_shape=jax.ShapeDtypeStruct((r, c), F32),
        compiler_params=_cparams(("parallel",)),
    )(a)


def _add(a, b, name):
    k, r, c = a.shape
    tr = min(r, 256)
    spec = pl.BlockSpec((k, tr, c), lambda i: (0, i, 0))

    def body(a_ref, b_ref, o_ref):
        o_ref[...] = (a_ref[...] + b_ref[...]).astype(BF16)

    return pl.pallas_call(
        body, name=name, grid=(r // tr,), in_specs=[spec, spec], out_specs=spec,
        out_shape=jax.ShapeDtypeStruct(a.shape, BF16), compiler_params=_cparams(("parallel",)),
    )(a, b)


def _adamw(w, g, m, v, name):
    r, c = w.shape
    tr = min(r, 256)
    spec = pl.BlockSpec((tr, c), lambda i: (i, 0))

    def body(w_ref, g_ref, m_ref, v_ref, d_ref, nm_ref, nv_ref):
        gv = g_ref[...]
        nm = ADAM_B1 * m_ref[...] + (1.0 - ADAM_B1) * gv
        nv = ADAM_B2 * v_ref[...] + (1.0 - ADAM_B2) * jnp.square(gv)
        m_hat = nm / (1.0 - ADAM_B1 ** ADAM_STEP)
        v_hat = nv / (1.0 - ADAM_B2 ** ADAM_STEP)
        d_ref[...] = -ADAM_LR * (m_hat / (jnp.sqrt(v_hat) + ADAM_EPS) + ADAM_WD * w_ref[...])
        nm_ref[...] = nm
        nv_ref[...] = nv

    return pl.pallas_call(
        body, name=name, grid=(r // tr,), in_specs=[spec] * 4, out_specs=[spec] * 3,
        out_shape=[jax.ShapeDtypeStruct(w.shape, F32)] * 3, compiler_params=_cparams(("parallel",)),
    )(w, g, m, v)


MATRICES = ("w_in", "w_out", "w_ff1", "w_ff2", "w_ple_gate", "w_ple_proj")
COLUMN_SHARDED = ("w_in", "w_ff1", "w_ple_proj")
SMALL = ("g_attn_pre", "g_q", "g_k", "g_out_a", "g_out_b", "g_attn_post", "rel_bias", "g_mlp_pre", "g_mlp_post",
         "g_ple")
WEIGHT_ORDER = ("w_in", "g_attn_pre", "g_q", "g_k", "g_out_a", "g_out_b", "w_out", "g_attn_post", "rel_bias",
                "g_mlp_pre", "w_ff1", "w_ff2", "g_mlp_post", "g_ple", "w_ple_gate", "w_ple_proj")
PACK_ROWS, PACK_COLS = 8, 1024


def _chip():
    return 2 * lax.axis_index("x") + lax.axis_index("y")


def _whole(name, gathered, mine):
    g = lax.dynamic_update_slice_in_dim(gathered, mine[None], _chip(), axis=0)
    if name in COLUMN_SHARDED:
        return g.transpose(1, 0, 2).reshape(g.shape[1], N_CHIPS * g.shape[2])
    return g.reshape(N_CHIPS * g.shape[1], g.shape[2])


def _pair_sums(names, grads):
    by_chip = []
    for n, g in zip(names, grads):
        if n in COLUMN_SHARDED:
            by_chip.append(g.reshape(g.shape[0], N_CHIPS, g.shape[1] // N_CHIPS).transpose(1, 0, 2))
        else:
            by_chip.append(g.reshape(N_CHIPS, g.shape[0] // N_CHIPS, g.shape[1]))
    c = lax.axis_index("c")
    pairs = []
    for n, g, other in zip(names, by_chip, _send_sibling_half(by_chip, names[0])):
        half = g.shape[1] // 2
        pairs.append(_add(lax.dynamic_slice_in_dim(g, c * half, half, axis=1), other, "pair_sum_" + n))
    return pairs


def _chip_sums(names, pairs, scattered):
    halves = []
    for n, pair, got in zip(names, pairs, scattered):
        own = lax.dynamic_slice_in_dim(pair, _chip(), 1, axis=0)
        halves.append(_sum_leading(lax.dynamic_update_slice_in_dim(got, own, _chip(), axis=0), "chip_sum_" + n))
    return halves


def _pack_small(values, extra=None):
    flat = [values[n].reshape(-1) for n in SMALL]
    used = sum(f.shape[0] for f in flat)
    tail = jnp.zeros((PACK_ROWS * PACK_COLS - used - 1,), F32)
    last = jnp.zeros((1,), F32) if extra is None else extra.reshape(1)
    return jnp.concatenate(flat + [tail, last]).reshape(PACK_ROWS, PACK_COLS)


def _unpack_small(packed, like):
    flat = packed.reshape(-1)
    out, o = {}, 0
    for n in SMALL:
        size = like[n].size
        out[n] = flat[o:o + size].reshape(like[n].shape)
        o += size
    return out, flat[-1]


def kernel(x, p, w_in, g_attn_pre, g_q, g_k, g_out_a, g_out_b, w_out, g_attn_post, rel_bias, g_mlp_pre, w_ff1, w_ff2, g_mlp_post, g_ple, w_ple_gate, w_ple_proj, loss_target, m_w_in, m_g_attn_pre, m_g_q, m_g_k, m_g_out_a, m_g_out_b, m_w_out, m_g_attn_post, m_rel_bias, m_g_mlp_pre, m_w_ff1, m_w_ff2, m_g_mlp_post, m_g_ple, m_w_ple_gate, m_w_ple_proj, v_w_in, v_g_attn_pre, v_g_q, v_g_k, v_g_out_a, v_g_out_b, v_w_out, v_g_attn_post, v_rel_bias, v_g_mlp_pre, v_w_ff1, v_w_ff2, v_g_mlp_post, v_g_ple, v_w_ple_gate, v_w_ple_proj):
    given = dict(locals())
    weights = {n: given[n] for n in WEIGHT_ORDER}
    shards = {n: weights[n][0] for n in MATRICES}

    c = lax.axis_index("c")
    own = dict(zip(MATRICES, _cast_shards([shards[n] for n in MATRICES])))
    w_in_whole = _whole("w_in", _gather_weights([own["w_in"]])[0], own["w_in"])

    loss, grad_x, grads = _local_step(
        x[0], p[0, 0], loss_target[0], w_in_whole, [own[n] for n in LATE], g_attn_pre, g_q, g_k, g_out_a, g_out_b,
        g_attn_post, rel_bias, g_mlp_pre, g_mlp_post, g_ple)

    pairs = _pair_sums(["w_in"], [grads["w_in"]])
    grads["w_in"] = _chip_sums(["w_in"], pairs, _scatter_to_chips(pairs))[0]
    halves = [grads[n] for n in MATRICES]
    grad_w = {}
    for n, mine, theirs in zip(MATRICES, halves, _exchange_halves(halves)):
        half = mine.shape[0]
        g = jnp.zeros((2 * half, mine.shape[1]), F32)
        g = lax.dynamic_update_slice_in_dim(g, mine, c * half, axis=0)
        grad_w[n] = lax.dynamic_update_slice_in_dim(g, theirs, (1 - c) * half, axis=0)

    small_like = {n: weights[n] for n in SMALL}
    reduced = _allreduce_small(_pack_small({n: grads[n] for n in SMALL}, extra=loss))
    grad_small, loss_total = _unpack_small(reduced, small_like)

    delta, new_m, new_v = {}, {}, {}
    for n in MATRICES:
        d, nm, nv = _adamw(shards[n], grad_w[n], given["m_" + n][0], given["v_" + n][0], "adamw_" + n)
        delta[n], new_m[n], new_v[n] = d[None], nm[None], nv[None]
        grad_w[n] = grad_w[n][None]
    d, nm, nv = _adamw(_pack_small(small_like), reduced, _pack_small({n: given["m_" + n] for n in SMALL}),
                       _pack_small({n: given["v_" + n] for n in SMALL}), "adamw_small")
    d_small, nm_small, nv_small = (_unpack_small(a, small_like)[0] for a in (d, nm, nv))
    for n in SMALL:
        grad_w[n], delta[n], new_m[n], new_v[n] = grad_small[n], d_small[n], nm_small[n], nv_small[n]

    return (loss_total, grad_x[None], *[grad_w[n] for n in WEIGHT_ORDER], *[delta[n] for n in WEIGHT_ORDER],
            *[new_m[n] for n in WEIGHT_ORDER], *[new_v[n] for n in WEIGHT_ORDER])
```

```python
import functools
import math

import jax
import jax.numpy as jnp
from jax import lax
from jax.experimental import pallas as pl
from jax.experimental.pallas import tpu as pltpu

F32 = jnp.float32
BF16 = jnp.bfloat16

D_MODEL = 1024
HEAD_DIM = 64
N_HEADS_A = 8
N_KV_A = 2
GROUP_A = N_HEADS_A // N_KV_A
N_HEADS_B = 8
D_A = N_HEADS_A * HEAD_DIM
D_KV_A = N_KV_A * HEAD_DIM
D_B = N_HEADS_B * HEAD_DIM
D_IN = D_A + 2 * D_KV_A + 3 * D_B
D_FF = 4 * D_MODEL
D_PLE = 256
GRID_W = 64
ROPE_THETA = 10000.0
DILATIONS = (1, 4, 16)
HALF_WIN = 64
N_BUCKETS = 32
MAX_DISTANCE = 1024
EPS = 1e-6
NEG_BIG = -1e30
Q_SCALE = HEAD_DIM ** -0.5

ADAM_LR = 0.001
ADAM_B1 = 0.9
ADAM_B2 = 0.999
ADAM_EPS = 1e-08
ADAM_WD = 0.01
ADAM_STEP = 10

N_CHIPS = 4
MESH = pl.DeviceIdType.MESH

ROW_TILE = 512
ATT_TQ = 256
ATT_TQ_BWD = 512
ATT_TK_FWD = 2048
ATT_UNROLL_FWD = 8
ATT_TK_BWD = 512
ATT_UNROLL_BWD = 8
SWA_TQ = 128
SWA_MIN_BLOCK = 2048
DW_TS = 2048
VMEM_LIMIT = 56 * 1024 * 1024

NT = (((1,), (1,)), ((), ()))
TN = (((0,), (0,)), ((), ()))


def _cparams(sem=None, vmem=VMEM_LIMIT):
    return pltpu.CompilerParams(dimension_semantics=sem, vmem_limit_bytes=vmem)


def _full(shape):
    n = len(shape)
    return pl.BlockSpec(shape, lambda *_: (0,) * n)


def _rows(tm, width):
    return pl.BlockSpec((tm, width), lambda i: (i, 0))


def _split3(a):
    a1 = a.astype(BF16)
    r = a - a1.astype(F32)
    a2 = r.astype(BF16)
    a3 = (r - a2.astype(F32)).astype(BF16)
    return a1, a2, a3


def _xdot(a, sel):
    a1, a2, a3 = _split3(a)
    d = lambda p: jnp.dot(p, sel, preferred_element_type=F32)
    return d(a1) + d(a2) + d(a3)


def _mm(a, b):
    return jnp.dot(a, b, preferred_element_type=F32)


def _mm_nt(a, b):
    return lax.dot_general(a, b, NT, preferred_element_type=F32)


def _mm_tn(a, b):
    return lax.dot_general(a, b, TN, preferred_element_type=F32)


def _rms_stats(x):
    r = lax.rsqrt(jnp.mean(x * x, axis=-1, keepdims=True) + EPS)
    return x * r, r


def _rms_bwd(dy, xh, r, g):
    gdy = dy * g
    dx = r * (gdy - xh * jnp.mean(gdy * xh, axis=-1, keepdims=True))
    dg = jnp.sum(dy * xh, axis=0, keepdims=True)
    return dx, dg


def _acc_out(ref, val):
    @pl.when(pl.program_id(0) == 0)
    def _():
        ref[...] = jnp.zeros_like(ref)

    ref[...] += val


def _swap_halves(x, first_half):
    return jnp.where(first_half, pltpu.roll(x, 96, 1), pltpu.roll(x, 32, 1))


def _first_half_mask(shape):
    return (lax.broadcasted_iota(jnp.int32, shape, 1) % HEAD_DIM) < (HEAD_DIM // 2)


def _rope_tables(s_len):
    t = jnp.arange(s_len)
    row = (t // GRID_W).astype(F32)
    col = (t % GRID_W).astype(F32)
    n_axis = HEAD_DIM // 4
    inv_freq = ROPE_THETA ** (-jnp.arange(n_axis, dtype=F32) / n_axis)
    ang = jnp.concatenate([row[:, None] * inv_freq, col[:, None] * inv_freq], axis=-1)
    c, s = jnp.cos(ang), jnp.sin(ang)
    cc = jnp.concatenate([c, c, c, c], axis=-1)
    ss = jnp.concatenate([-s, s, -s, s], axis=-1)
    return cc, ss


def _group_ones(width):
    i = jnp.arange(width)
    return (i[:, None] // HEAD_DIM == i[None, :] // HEAD_DIM).astype(BF16)


def _t5_bucket(rel):
    nb = N_BUCKETS // 2
    max_exact = nb // 2
    side = jnp.where(rel > 0, nb, 0)
    n = jnp.abs(rel)
    large = max_exact + (jnp.log(jnp.maximum(n, max_exact).astype(F32) / max_exact)
                         / math.log(MAX_DISTANCE / max_exact) * (nb - max_exact)).astype(jnp.int32)
    large = jnp.minimum(large, nb - 1)
    return side + jnp.where(n < max_exact, n, large)


def _in_window(tq):
    qi = jnp.arange(tq)
    kj = jnp.arange(tq + 2 * HALF_WIN)
    return jnp.abs(kj[None, :] - HALF_WIN - qi[:, None]) <= HALF_WIN


def _bucket_onehot(tq, dilation):
    qi = jnp.arange(tq)
    kj = jnp.arange(tq + 2 * HALF_WIN)
    rel = kj[None, :] - HALF_WIN - qi[:, None]
    bucket = _t5_bucket(rel * dilation).reshape(-1)
    return (bucket[:, None] == jnp.arange(128)[None, :]).astype(BF16)


def _in_proj(x, g1, w_in, cc, ss, gq2, gk2, ones128):
    s_len = x.shape[0]
    tm = min(ROW_TILE, s_len)

    def body(x_ref, g_ref, w_ref, cc_ref, ss_ref, gq_ref, gk_ref, one_ref,
             xn_ref, qpre_ref, kpre_ref, qa_ref, kv_ref, qb_ref, kb_ref, vb_ref):
        xh, _ = _rms_stats(x_ref[...])
        xn = (xh * g_ref[...]).astype(BF16)
        xn_ref[...] = xn
        proj = _mm(xn, w_ref[...])
        first_half = _first_half_mask((tm, 128))
        ones = one_ref[...]
        cc_t, ss_t = cc_ref[...], ss_ref[...]

        def norm_rope(xc, gain):
            ms = _xdot(xc * xc, ones) * (1.0 / HEAD_DIM)
            y = xc * lax.rsqrt(ms + EPS) * gain
            return y * cc_t + _swap_halves(y, first_half) * ss_t

        qpre_ref[...] = proj[:, :D_A]
        kpre_ref[...] = proj[:, D_A:D_A + D_KV_A]
        for c in range(D_A // 128):
            y = norm_rope(proj[:, 128 * c:128 * (c + 1)], gq_ref[...])
            qa_ref[:, 128 * c:128 * (c + 1)] = (y * Q_SCALE).astype(BF16)
        ka = norm_rope(proj[:, D_A:D_A + D_KV_A], gk_ref[...])
        o = D_A + D_KV_A
        va = proj[:, o:o + D_KV_A]
        low = _low_lanes(tm)
        kv_ref[0] = jnp.where(low, ka, pltpu.roll(va, HEAD_DIM, 1)).astype(BF16)
        kv_ref[1] = jnp.where(low, pltpu.roll(ka, HEAD_DIM, 1), va).astype(BF16)
        o += D_KV_A
        qb_ref[...] = (proj[:, o:o + D_B] * Q_SCALE).astype(BF16)
        kb_ref[...] = proj[:, o + D_B:o + 2 * D_B].astype(BF16)
        vb_ref[...] = proj[:, o + 2 * D_B:o + 3 * D_B].astype(BF16)

    sds = jax.ShapeDtypeStruct
    return pl.pallas_call(
        body, name="in_proj", grid=(s_len // tm,),
        in_specs=[_rows(tm, D_MODEL), _full((1, D_MODEL)), _full((D_MODEL, D_IN)), _rows(tm, 128), _rows(tm, 128),
                  _full((1, 128)), _full((1, 128)), _full((128, 128))],
        out_specs=[_rows(tm, D_MODEL), _rows(tm, D_A), _rows(tm, D_KV_A), _rows(tm, D_A),
                   pl.BlockSpec((N_KV_A, tm, 128), lambda i: (0, i, 0)), _rows(tm, D_B), _rows(tm, D_B),
                   _rows(tm, D_B)],
        out_shape=[sds((s_len, D_MODEL), BF16), sds((s_len, D_A), F32), sds((s_len, D_KV_A), F32),
                   sds((s_len, D_A), BF16), sds((N_KV_A, s_len, 128), BF16),
                   sds((s_len, D_B), BF16), sds((s_len, D_B), BF16), sds((s_len, D_B), BF16)],
        compiler_params=_cparams(("parallel",)),
    )(x, g1, w_in, cc, ss, gq2, gk2, ones128)


def _stat_spec(tm):
    return pl.BlockSpec((N_HEADS_B, tm, 1), lambda i: (0, i, 0))


def _out_proj(ya, yb, x, g_a, g_b, w_out, g_post, g_mlp_pre):
    s_len = x.shape[0]
    tm = min(ROW_TILE, s_len)

    def body(ya_ref, yb_ref, x_ref, ga_ref, gb_ref, w_ref, gp_ref, gm_ref, ycat_ref, y2_ref, h1_ref, xn2_ref):
        ah, _ = _rms_stats(ya_ref[...])
        bh, _ = _rms_stats(yb_ref[...])
        ycat = jnp.concatenate([ah * ga_ref[...], bh * gb_ref[...]], axis=-1).astype(BF16)
        ycat_ref[...] = ycat
        y2 = _mm(ycat, w_ref[...])
        y2_ref[...] = y2
        y2h, _ = _rms_stats(y2)
        h1 = x_ref[...] + y2h * gp_ref[...]
        h1_ref[...] = h1
        h1h, _ = _rms_stats(h1)
        xn2_ref[...] = (h1h * gm_ref[...]).astype(BF16)

    sds = jax.ShapeDtypeStruct
    return pl.pallas_call(
        body, name="out_proj", grid=(s_len // tm,),
        in_specs=[_rows(tm, D_A), _rows(tm, D_B), _rows(tm, D_MODEL), _full((1, D_A)), _full((1, D_B)),
                  _full((D_MODEL, D_MODEL)), _full((1, D_MODEL)), _full((1, D_MODEL))],
        out_specs=[_rows(tm, D_MODEL)] * 4,
        out_shape=[sds((s_len, D_MODEL), BF16), sds((s_len, D_MODEL), F32), sds((s_len, D_MODEL), F32),
                   sds((s_len, D_MODEL), BF16)],
        compiler_params=_cparams(("parallel",)),
    )(ya, yb, x, g_a, g_b, w_out, g_post, g_mlp_pre)


def _ff1(xn2, w_ff1):
    s_len = xn2.shape[0]
    tm = min(ROW_TILE, s_len)

    def body(x_ref, w_ref, u_ref):
        u_ref[...] = _mm(x_ref[...], w_ref[...])

    return pl.pallas_call(
        body, name="ff1", grid=(s_len // tm,),
        in_specs=[_rows(tm, D_MODEL), _full((D_MODEL, D_FF))],
        out_specs=_rows(tm, D_FF),
        out_shape=jax.ShapeDtypeStruct((s_len, D_FF), F32),
        compiler_params=_cparams(("parallel",)),
    )(xn2, w_ff1)


def _ff2(u, w_ff2, h1, g_post, g_ple):
    s_len = u.shape[0]
    tm = min(ROW_TILE, s_len)

    def body(u_ref, w_ref, h1_ref, gp_ref, gl_ref, f2_ref, h2_ref, xn3_ref):
        f = jnp.square(jnp.maximum(u_ref[...], 0.0)).astype(BF16)
        f2 = _mm(f, w_ref[...])
        f2_ref[...] = f2
        f2h, _ = _rms_stats(f2)
        h2 = h1_ref[...] + f2h * gp_ref[...]
        h2_ref[...] = h2
        h2h, _ = _rms_stats(h2)
        xn3_ref[...] = (h2h * gl_ref[...]).astype(BF16)

    sds = jax.ShapeDtypeStruct
    return pl.pallas_call(
        body, name="ff2", grid=(s_len // tm,),
        in_specs=[_rows(tm, D_FF), _full((D_FF, D_MODEL)), _rows(tm, D_MODEL), _full((1, D_MODEL)),
                  _full((1, D_MODEL))],
        out_specs=[_rows(tm, D_MODEL)] * 3,
        out_shape=[sds((s_len, D_MODEL), F32), sds((s_len, D_MODEL), F32), sds((s_len, D_MODEL), BF16)],
        compiler_params=_cparams(("parallel",)),
    )(u, w_ff2, h1, g_post, g_ple)


def _ple_loss(xn3, p, h2, f2, tgt, w_gate, w_ple, g_ple, g_mlp_post):
    s_len = h2.shape[0]
    tm = min(ROW_TILE, s_len)

    def body(xn3_ref, p_ref, h2_ref, f2_ref, t_ref, wg_ref, wp_ref, gl_ref, gp_ref,
             dh2_ref, df2_ref, dgl_ref, dpp_ref, loss_ref, dgple_ref, dgpost_ref):
        gate = jax.nn.sigmoid(_mm(xn3_ref[...], wg_ref[...]))
        pp = _mm(p_ref[...].astype(BF16), wp_ref[...])
        h2 = h2_ref[...]
        err = h2 + gate * pp - t_ref[...]
        sq = jnp.sum(jnp.sum(err * err, axis=1, keepdims=True), axis=0, keepdims=True)
        _acc_out(loss_ref, sq * (0.5 / D_MODEL))
        dh3 = err * (1.0 / D_MODEL)
        dgl = (dh3 * pp) * gate * (1.0 - gate)
        dgl_b = dgl.astype(BF16)
        dgl_ref[...] = dgl_b
        dpp_ref[...] = (dh3 * gate).astype(BF16)
        dxn3 = _mm_nt(dgl_b, wg_ref[...])
        h2h, r2 = _rms_stats(h2)
        dx, dg = _rms_bwd(dxn3, h2h, r2, gl_ref[...])
        _acc_out(dgple_ref, dg)
        dh2 = dh3 + dx
        dh2_ref[...] = dh2
        f2h, rf = _rms_stats(f2_ref[...])
        df2, dg = _rms_bwd(dh2, f2h, rf, gp_ref[...])
        _acc_out(dgpost_ref, dg)
        df2_ref[...] = df2.astype(BF16)

    sds = jax.ShapeDtypeStruct
    return pl.pallas_call(
        body, name="ple_loss", grid=(s_len // tm,),
        in_specs=[_rows(tm, D_MODEL), _rows(tm, D_PLE), _rows(tm, D_MODEL), _rows(tm, D_MODEL), _rows(tm, D_MODEL),
                  _full((D_MODEL, D_MODEL)), _full((D_PLE, D_MODEL)), _full((1, D_MODEL)), _full((1, D_MODEL))],
        out_specs=[_rows(tm, D_MODEL)] * 3 + [_rows(tm, D_MODEL), _full((1, 1)), _full((1, D_MODEL)),
                                              _full((1, D_MODEL))],
        out_shape=[sds((s_len, D_MODEL), F32), sds((s_len, D_MODEL), BF16), sds((s_len, D_MODEL), BF16),
                   sds((s_len, D_MODEL), BF16), sds((1, 1), F32), sds((1, D_MODEL), F32), sds((1, D_MODEL), F32)],
        compiler_params=_cparams(("arbitrary",)),
    )(xn3, p, h2, f2, tgt, w_gate, w_ple, g_ple, g_mlp_post)


def _ff2_bwd(df2, w_ff2_t, u):
    s_len = u.shape[0]
    tm = min(ROW_TILE, s_len)

    def body(d_ref, w_ref, u_ref, du_ref):
        df = _mm(d_ref[...], w_ref[...])
        du_ref[...] = (df * (2.0 * jnp.maximum(u_ref[...], 0.0))).astype(BF16)

    return pl.pallas_call(
        body, name="ff2_bwd", grid=(s_len // tm,),
        in_specs=[_rows(tm, D_MODEL), _full((D_MODEL, D_FF)), _rows(tm, D_FF)],
        out_specs=_rows(tm, D_FF),
        out_shape=jax.ShapeDtypeStruct((s_len, D_FF), BF16),
        compiler_params=_cparams(("parallel",)),
    )(df2, w_ff2_t, u)


def _ff1_bwd(du, w_ff1, dh2, h1, y2, g_mlp_pre, g_post):
    s_len = du.shape[0]
    tm = min(ROW_TILE, s_len)

    def body(du_ref, w_ref, dh2_ref, h1_ref, y2_ref, gm_ref, gp_ref, dh1_ref, dy2_ref, dgm_ref, dgp_ref):
        dxn2 = _mm_nt(du_ref[...], w_ref[...])
        h1h, r1 = _rms_stats(h1_ref[...])
        dx, dg = _rms_bwd(dxn2, h1h, r1, gm_ref[...])
        _acc_out(dgm_ref, dg)
        dh1 = dh2_ref[...] + dx
        dh1_ref[...] = dh1
        y2h, ry = _rms_stats(y2_ref[...])
        dy2, dg = _rms_bwd(dh1, y2h, ry, gp_ref[...])
        _acc_out(dgp_ref, dg)
        dy2_ref[...] = dy2.astype(BF16)

    sds = jax.ShapeDtypeStruct
    return pl.pallas_call(
        body, name="ff1_bwd", grid=(s_len // tm,),
        in_specs=[_rows(tm, D_FF), _full((D_MODEL, D_FF)), _rows(tm, D_MODEL), _rows(tm, D_MODEL),
                  _rows(tm, D_MODEL), _full((1, D_MODEL)), _full((1, D_MODEL))],
        out_specs=[_rows(tm, D_MODEL), _rows(tm, D_MODEL), _full((1, D_MODEL)), _full((1, D_MODEL))],
        out_shape=[sds((s_len, D_MODEL), F32), sds((s_len, D_MODEL), BF16), sds((1, D_MODEL), F32),
                   sds((1, D_MODEL), F32)],
        compiler_params=_cparams(("arbitrary",)),
    )(du, w_ff1, dh2, h1, y2, g_mlp_pre, g_post)


def _out_proj_bwd(dy2, w_out, ya, yb, lse_b, g_a, g_b):
    s_len = ya.shape[0]
    tm = min(ROW_TILE, s_len)

    def body(d_ref, w_ref, ya_ref, yb_ref, lse_ref, ga_ref, gb_ref, dya_ref, dyb_ref, da_ref, st_ref, dga_ref,
             dgb_ref):
        dycat = _mm_nt(d_ref[...], w_ref[...])
        lane = lax.broadcasted_iota(jnp.int32, (tm, 128), 1)
        low = lane < HEAD_DIM
        is_lse = (lane % HEAD_DIM) < (HEAD_DIM // 2)

        def head_sums(prod_chunk):
            return (jnp.sum(jnp.where(low, prod_chunk, 0.0), axis=1, keepdims=True),
                    jnp.sum(jnp.where(low, 0.0, prod_chunk), axis=1, keepdims=True))

        ya = ya_ref[...]
        yh, r = _rms_stats(ya)
        dya, dg = _rms_bwd(dycat[:, :D_A], yh, r, ga_ref[...])
        _acc_out(dga_ref, dg)
        dya_ref[...] = dya
        prod = dya * ya
        for c in range(D_A // 128):
            da_ref[2 * c], da_ref[2 * c + 1] = head_sums(prod[:, 128 * c:128 * (c + 1)])

        yb = yb_ref[...]
        yh, r = _rms_stats(yb)
        dyb, dg = _rms_bwd(dycat[:, D_A:], yh, r, gb_ref[...])
        _acc_out(dgb_ref, dg)
        dyb_ref[...] = dyb.astype(BF16)
        prod = dyb * yb
        for c in range(D_B // 128):
            sl = slice(128 * c, 128 * (c + 1))
            d_lo, d_hi = head_sums(prod[:, sl])
            st_ref[:, sl] = jnp.where(is_lse, lse_ref[:, sl], jnp.where(low, d_lo, d_hi))

    sds = jax.ShapeDtypeStruct
    return pl.pallas_call(
        body, name="out_proj_bwd", grid=(s_len // tm,),
        in_specs=[_rows(tm, D_MODEL), _full((D_MODEL, D_MODEL)), _rows(tm, D_A), _rows(tm, D_B), _rows(tm, D_B),
                  _full((1, D_A)), _full((1, D_B))],
        out_specs=[_rows(tm, D_A), _rows(tm, D_B), _stat_spec(tm), _rows(tm, D_B), _full((1, D_A)),
                   _full((1, D_B))],
        out_shape=[sds((s_len, D_A), F32), sds((s_len, D_B), BF16), sds((N_HEADS_A, s_len, 1), F32),
                   sds((s_len, D_B), F32), sds((1, D_A), F32), sds((1, D_B), F32)],
        compiler_params=_cparams(("arbitrary",)),
    )(dy2, w_out, ya, yb, lse_b, g_a, g_b)


def _in_proj_bwd(dqr, dkv, dqb, dkb, dvb, qpre, kpre, x, dh1, g1, w_in, cc, ss, gq2, gk2, ones128):
    s_len = x.shape[0]
    tm = min(ROW_TILE, s_len)

    def body(dqr_ref, dkv_ref, dq0, dq1, dq2, dk0, dk1, dk2, dv0, dv1, dv2, qpre_ref, kpre_ref, x_ref,
             dh1_ref, g_ref, w_ref, cc_ref, ss_ref, gq_ref, gk_ref, one_ref, dproj_ref, gx_ref, dg1_ref, dgq_ref,
             dgk_ref):
        low = _low_lanes(tm)
        dkr = jnp.where(low, dkv_ref[0], pltpu.roll(dkv_ref[1], HEAD_DIM, 1))
        dva = jnp.where(low, pltpu.roll(dkv_ref[0], HEAD_DIM, 1), dkv_ref[1])
        first_half = _first_half_mask((tm, 128))
        ones = one_ref[...]
        cc_t, ss_t = cc_ref[...], ss_ref[...]

        def norm_rope_bwd(dy, xc, gain):
            dn = dy * cc_t - _swap_halves(dy, first_half) * ss_t
            r = lax.rsqrt(_xdot(xc * xc, ones) * (1.0 / HEAD_DIM) + EPS)
            xh = xc * r
            gdy = dn * gain
            dx = r * (gdy - xh * (_xdot(gdy * xh, ones) * (1.0 / HEAD_DIM)))
            return dx, jnp.sum(dn * xh, axis=0, keepdims=True)

        dgq = jnp.zeros((1, 128), F32)
        parts = []
        for c in range(D_A // 128):
            sl = slice(128 * c, 128 * (c + 1))
            dx, dg = norm_rope_bwd(dqr_ref[:, sl] * Q_SCALE, qpre_ref[:, sl], gq_ref[...])
            parts.append(dx)
            dgq = dgq + dg
        dxk, dgk = norm_rope_bwd(dkr, kpre_ref[...], gk_ref[...])
        _acc_out(dgq_ref, dgq)
        _acc_out(dgk_ref, dgk)
        total = lambda a, b, c: a[...].astype(F32) + b[...].astype(F32) + c[...].astype(F32)
        parts += [dxk, dva, total(dq0, dq1, dq2) * Q_SCALE, total(dk0, dk1, dk2), total(dv0, dv1, dv2)]
        dproj = jnp.concatenate(parts, axis=-1).astype(BF16)
        dproj_ref[...] = dproj
        dxn = _mm_nt(dproj, w_ref[...])
        xh, r = _rms_stats(x_ref[...])
        dx, dg = _rms_bwd(dxn, xh, r, g_ref[...])
        _acc_out(dg1_ref, dg)
        gx_ref[...] = dh1_ref[...] + dx

    sds = jax.ShapeDtypeStruct
    return pl.pallas_call(
        body, name="in_proj_bwd", grid=(s_len // tm,),
        in_specs=[_rows(tm, D_A), pl.BlockSpec((N_KV_A, tm, 128), lambda i: (0, i, 0))] + [_rows(tm, D_B)] * 9
                 + [_rows(tm, D_A), _rows(tm, D_KV_A), _rows(tm, D_MODEL), _rows(tm, D_MODEL),
                    _full((1, D_MODEL)), _full((D_MODEL, D_IN)), _rows(tm, 128), _rows(tm, 128), _full((1, 128)),
                    _full((1, 128)), _full((128, 128))],
        out_specs=[_rows(tm, D_IN), _rows(tm, D_MODEL), _full((1, D_MODEL)), _full((1, 128)), _full((1, 128))],
        out_shape=[sds((s_len, D_IN), BF16), sds((s_len, D_MODEL), F32), sds((1, D_MODEL), F32),
                   sds((1, 128), F32), sds((1, 128), F32)],
        compiler_params=_cparams(("arbitrary",)),
    )(dqr, dkv, *dqb, *dkb, *dvb, qpre, kpre, x, dh1, g1, w_in, cc, ss, gq2, gk2, ones128)


def _dw(a, b, name, relu2=False):
    s_len, ka = a.shape
    n = b.shape[1]
    ts = min(DW_TS, s_len)
    bk = min(ka, 1024)
    bn = n if n % 1024 else 1024

    def body(a_ref, b_ref, o_ref):
        @pl.when(pl.program_id(2) == 0)
        def _():
            o_ref[...] = jnp.zeros_like(o_ref)

        av = a_ref[...]
        if relu2:
            av = jnp.square(jnp.maximum(av, 0.0))
        o_ref[...] += _mm_tn(av.astype(BF16), b_ref[...])

    return pl.pallas_call(
        body, name=name, grid=(ka // bk, n // bn, s_len // ts),
        in_specs=[pl.BlockSpec((ts, bk), lambda i, j, k: (k, i)), pl.BlockSpec((ts, bn), lambda i, j, k: (k, j))],
        out_specs=pl.BlockSpec((bk, bn), lambda i, j, k: (i, j)),
        out_shape=jax.ShapeDtypeStruct((ka, n), F32),
        compiler_params=_cparams(("parallel", "parallel", "arbitrary")),
    )(a, b)


def _stack_heads(block, low, data_low):
    parts = []
    for c in range(GROUP_A // 2):
        chunk = block[:, 128 * c:128 * (c + 1)]
        swapped = pltpu.roll(chunk, HEAD_DIM, 1)
        for h_low in (chunk, swapped) if data_low else (swapped, chunk):
            parts.append(jnp.where(low, h_low, 0.0) if data_low else jnp.where(low, 0.0, h_low))
    return jnp.concatenate(parts, axis=0).astype(BF16)


def _unstack_heads(stacked, low, tq, data_low):
    chunks = []
    for c in range(GROUP_A // 2):
        even = stacked[2 * c * tq:(2 * c + 1) * tq]
        odd = stacked[(2 * c + 1) * tq:(2 * c + 2) * tq]
        if data_low:
            chunks.append(jnp.where(low, even, pltpu.roll(odd, HEAD_DIM, 1)))
        else:
            chunks.append(jnp.where(low, pltpu.roll(even, HEAD_DIM, 1), odd))
    return chunks


def _attn_a_fwd(qa, kv):
    s_len = kv.shape[1]
    tq = min(ATT_TQ, s_len)
    tk = min(ATT_TK_FWD, s_len)
    rows = GROUP_A * tq

    def body(q_ref, kv_ref, o_ref, lse_ref):
        low = _low_lanes(tq)
        low_k = _low_lanes(tk)
        q = _stack_heads(q_ref[...].astype(F32), low, data_low=True)

        def block(j, m, acc):
            kvj = kv_ref[0, pl.ds(pl.multiple_of(j * tk, tk), tk), :]
            s = _mm_nt(q, kvj)
            m_new = jnp.maximum(m, jnp.max(s, axis=1, keepdims=True))
            p = jnp.exp(s - m_new).astype(BF16)
            return m_new, jnp.exp(m - m_new) * acc + _mm(p, jnp.where(low_k, jnp.ones_like(kvj), kvj))

        def step(j, carry):
            for u in range(unroll):
                carry = block(unroll * j + u, *carry)
            return carry

        unroll = math.gcd(s_len // tk, ATT_UNROLL_FWD)
        init = (jnp.full((rows, 1), -jnp.inf, F32), jnp.zeros((rows, 128), F32))
        m, acc = lax.fori_loop(0, s_len // (tk * unroll), step, init)
        for c, chunk in enumerate(_unstack_heads(acc / pltpu.roll(acc, HEAD_DIM, 1), low, tq, data_low=False)):
            o_ref[:, 128 * c:128 * (c + 1)] = chunk
        lse_ref[...] = (m + jnp.log(acc[:, :1])).reshape(GROUP_A, tq, 1)

    return pl.pallas_call(
        body, name="attn_a_fwd", grid=(N_KV_A, s_len // tq),
        in_specs=[pl.BlockSpec((tq, 256), lambda g, i: (i, g)),
                  pl.BlockSpec((1, s_len, 128), lambda g, i: (g, 0, 0))],
        out_specs=[pl.BlockSpec((tq, 256), lambda g, i: (i, g)),
                   pl.BlockSpec((GROUP_A, tq, 1), lambda g, i: (g, i, 0))],
        out_shape=[jax.ShapeDtypeStruct((s_len, D_A), F32),
                   jax.ShapeDtypeStruct((N_HEADS_A, s_len, 1), F32)],
        compiler_params=_cparams(("parallel", "parallel")),
    )(qa, kv)


def _attn_a_bwd(qa, dya, kv, lse, delta):
    s_len = kv.shape[1]
    tq = min(ATT_TQ_BWD, s_len)
    tk = min(ATT_TK_BWD, s_len)
    rows = GROUP_A * tq

    def body(q_ref, do_ref, kv_ref, lse_ref, dl_ref, dq_ref, dkv_ref):
        @pl.when(pl.program_id(1) == 0)
        def _():
            dkv_ref[...] = jnp.zeros_like(dkv_ref)

        low = _low_lanes(tq)
        q = _stack_heads(q_ref[...].astype(F32), low, data_low=True)
        do = _stack_heads(do_ref[...], low, data_low=False)
        lse_t = lse_ref[...].reshape(rows, 1)
        dl_t = dl_ref[...].reshape(rows, 1)
        q_t = q.T
        do_t = do.T

        def block(j, dq):
            span = pl.ds(pl.multiple_of(j * tk, tk), tk)
            kvj = kv_ref[0, span, :]
            p = jnp.exp(_mm_nt(q, kvj) - lse_t)
            ds = (p * (_mm_nt(do, kvj) - dl_t)).astype(BF16)
            dkv_ref[0, :, span] += _mm(q_t, ds) + _mm(do_t, p.astype(BF16))
            return dq + _mm(ds, kvj)

        def step(j, dq):
            for u in range(unroll):
                dq = block(unroll * j + u, dq)
            return dq

        unroll = math.gcd(s_len // tk, ATT_UNROLL_BWD)
        dq = lax.fori_loop(0, s_len // (tk * unroll), step, jnp.zeros((rows, 128), F32))
        for c, chunk in enumerate(_unstack_heads(dq, low, tq, data_low=True)):
            dq_ref[:, 128 * c:128 * (c + 1)] = chunk

    return pl.pallas_call(
        body, name="attn_a_bwd", grid=(N_KV_A, s_len // tq),
        in_specs=[pl.BlockSpec((tq, 256), lambda g, i: (i, g)),
                  pl.BlockSpec((tq, 256), lambda g, i: (i, g)),
                  pl.BlockSpec((1, s_len, 128), lambda g, i: (g, 0, 0)),
                  pl.BlockSpec((GROUP_A, tq, 1), lambda g, i: (g, i, 0)),
                  pl.BlockSpec((GROUP_A, tq, 1), lambda g, i: (g, i, 0))],
        out_specs=[pl.BlockSpec((tq, 256), lambda g, i: (i, g)),
                   pl.BlockSpec((1, 128, s_len), lambda g, i: (g, 0, 0))],
        out_shape=[jax.ShapeDtypeStruct((s_len, D_A), F32),
                   jax.ShapeDtypeStruct((N_KV_A, 128, s_len), F32)],
        compiler_params=_cparams(("parallel", "arbitrary")),
    )(qa, dya, kv, lse, delta)


class _SwaGeometry:
    def __init__(self, s_len, r):
        self.r = r
        self.tq = SWA_TQ
        self.block = min(max(SWA_MIN_BLOCK, 2 * SWA_TQ * r), s_len)
        self.halo = HALF_WIN * r
        self.nsub = self.block // (self.tq * r)
        self.band = self.tq + 2 * HALF_WIN
        self.length = s_len // r
        self.nblk = s_len // self.block
        self.nhalo = s_len // self.halo
        assert self.nsub * self.tq * r == self.block and self.block % self.halo == 0

    def specs(self):
        per = self.block // self.halo
        cur = pl.BlockSpec((self.block, 128), lambda c, i: (i, c))
        prev = pl.BlockSpec((self.halo, 128), lambda c, i: (jnp.maximum(i * per - 1, 0), c))
        nxt = pl.BlockSpec((self.halo, 128), lambda c, i: (jnp.minimum((i + 1) * per, self.nhalo - 1), c))
        return prev, cur, nxt

    def tiles(self):
        return [(rho + self.r * j * self.tq, self.halo + rho + self.r * (j * self.tq - HALF_WIN), j)
                for j in range(self.nsub) for rho in range(self.r)]

    def own(self, start):
        return pl.ds(start, self.tq, stride=self.r)

    def around(self, start):
        return pl.ds(start, self.band, stride=self.r)

    def fill(self, dst, prev_ref, cur_ref, next_ref):
        dst[:self.halo, :] = prev_ref[...].astype(F32)
        dst[self.halo:self.halo + self.block, :] = cur_ref[...].astype(F32)
        dst[self.halo + self.block:, :] = next_ref[...].astype(F32)

    def first_position(self, j):
        return (pl.program_id(1) * self.block) // self.r + j * self.tq

    def outside(self, j, copies=1):
        pos = self.first_position(j) - HALF_WIN + lax.broadcasted_iota(jnp.int32, (1, copies * self.band), 1) % self.band
        return jnp.where((pos >= 0) & (pos < self.length), 0.0, NEG_BIG)

    def extended(self):
        return pltpu.VMEM((self.block + 2 * self.halo, 128), F32)

    def plain(self):
        return pltpu.VMEM((self.block, 128), F32)


def _low_lanes(rows):
    return lax.broadcasted_iota(jnp.int32, (rows, 128), 1) < HEAD_DIM


def _one_head(x, low, half):
    return jnp.where(low if half == 0 else jnp.logical_not(low), x, 0.0).astype(BF16)


def _two_heads(x, low):
    return jnp.concatenate([_one_head(x, low, 0), _one_head(x, low, 1)], axis=0)


def _carry_ride(base_body, n_in, n_out, n_scratch, ride, grid):
    if ride is None:
        return base_body
    n = ride.n

    def body(*refs):
        o = n_in + n
        ins, ride_ins = refs[:n_in], refs[n_in:o]
        outs, ride_outs = refs[o:o + n_out], refs[o + n_out:o + n_out + n]
        o += n_out + n
        scratch, sems = refs[o:o + n_scratch], refs[o + n_scratch:]
        at_first = (pl.program_id(0) == 0) & (pl.program_id(1) == 0)
        at_last = (pl.program_id(0) == grid[0] - 1) & (pl.program_id(1) == grid[1] - 1)

        @pl.when(at_first)
        def _():
            ride.start(ride_ins, ride_outs, sems)

        base_body(*ins, *outs, *scratch)

        @pl.when(at_last)
        def _():
            ride.finish(ride_ins, ride_outs, sems)

    return body


def _ride_call(base_body, name, grid, in_specs, out_specs, out_shape, scratch_shapes, operands, ride):
    n = 0 if ride is None else ride.n
    extra = [] if ride is None else ride.operands
    outs = pl.pallas_call(
        _carry_ride(base_body, len(in_specs), len(out_specs), len(scratch_shapes), ride, grid), name=name, grid=grid,
        in_specs=list(in_specs) + [ANY] * n, out_specs=list(out_specs) + [ANY] * n,
        out_shape=list(out_shape) + ([] if ride is None else ride.out_shape()),
        scratch_shapes=list(scratch_shapes) + ([] if ride is None else ride.scratch_shapes()),
        compiler_params=_cparams(("arbitrary", "arbitrary")),
    )(*operands, *extra)
    return outs[:len(out_specs)], outs[len(out_specs):]


def _swa_fwd(q, k, v, bias, r, so_far=None, ride=None):
    geo = _SwaGeometry(q.shape[0], r)
    prev, cur, nxt = geo.specs()

    def body(q_ref, kp, kc, kn, vp, vc, vn, b_ref, *rest):
        if so_far is None:
            o_ref, lse_ref, qf, kf, vf = rest
        else:
            o_old_ref, lse_old_ref, o_ref, lse_ref, qf, kf, vf = rest
        qf[...] = q_ref[...].astype(F32)
        geo.fill(kf, kp, kc, kn)
        geo.fill(vf, vp, vc, vn)
        tq = geo.tq
        low_q = _low_lanes(tq)
        bias = b_ref[...].reshape(2 * tq, geo.band)
        for own, around, j in geo.tiles():
            q2 = _two_heads(qf[geo.own(own), :], low_q)
            kb = kf[geo.around(around), :].astype(BF16)
            vb = vf[geo.around(around), :].astype(BF16)
            s = _mm_nt(q2, kb) + bias + geo.outside(j)
            m = jnp.max(s, axis=1, keepdims=True)
            e = jnp.exp(s - m)
            l = jnp.sum(e, axis=1, keepdims=True)
            o2 = _mm(e.astype(BF16), vb) / l
            lse2 = m + jnp.log(l)
            o_new = jnp.where(low_q, o2[:tq], o2[tq:])
            lse_new = jnp.where(low_q, lse2[:tq], lse2[tq:])
            if so_far is not None:
                o_old, lse_old = o_old_ref[geo.own(own), :], lse_old_ref[geo.own(own), :]
                top = jnp.maximum(lse_old, lse_new)
                w_old, w_new = jnp.exp(lse_old - top), jnp.exp(lse_new - top)
                o_new = (w_old * o_old + w_new * o_new) / (w_old + w_new)
                lse_new = top + jnp.log(w_old + w_new)
            o_ref[geo.own(own), :] = o_new
            lse_ref[geo.own(own), :] = lse_new

    sds = jax.ShapeDtypeStruct
    before = () if so_far is None else tuple(so_far)
    (o, lse), carried = _ride_call(
        body, "swa_fwd_%d" % r, (D_B // 128, geo.nblk),
        [cur, prev, cur, nxt, prev, cur, nxt, pl.BlockSpec((2, geo.tq, geo.band), lambda c, i: (c, 0, 0))]
        + [cur] * len(before),
        [cur, cur], [sds(q.shape, F32), sds(q.shape, F32)], [geo.plain(), geo.extended(), geo.extended()],
        (q, k, k, k, v, v, v, bias) + before, ride)
    return o, lse, carried


def _head_stats(st, half):
    lo = HEAD_DIM * half
    return st[:, lo:lo + 1], st[:, lo + HEAD_DIM // 2:lo + HEAD_DIM // 2 + 1]


def _swa_bwd_q(q, k, v, dy, st, bias, r, ride=None):
    geo = _SwaGeometry(q.shape[0], r)
    prev, cur, nxt = geo.specs()
    bias_spec = pl.BlockSpec((2, geo.tq, geo.band), lambda c, i: (c, 0, 0))

    def body(q_ref, kp, kc, kn, vp, vc, vn, dy_ref, st_ref, b_ref, dq_ref, db_ref, qf, kf, vf, dyf, dqf):
        @pl.when(pl.program_id(1) == 0)
        def _():
            db_ref[...] = jnp.zeros_like(db_ref)

        qf[...] = q_ref[...].astype(F32)
        dyf[...] = dy_ref[...].astype(F32)
        geo.fill(kf, kp, kc, kn)
        geo.fill(vf, vp, vc, vn)
        tq = geo.tq
        low_q = _low_lanes(tq)
        bias = b_ref[...].reshape(2 * tq, geo.band)
        for own, around, j in geo.tiles():
            sts = st_ref[geo.own(own), :]
            (lse0, delta0), (lse1, delta1) = _head_stats(sts, 0), _head_stats(sts, 1)
            lse = jnp.concatenate([lse0, lse1], axis=0)
            delta = jnp.concatenate([delta0, delta1], axis=0)
            kb = kf[geo.around(around), :].astype(BF16)
            vb = vf[geo.around(around), :].astype(BF16)
            s = _mm_nt(_two_heads(qf[geo.own(own), :], low_q), kb) + bias + geo.outside(j)
            p = jnp.exp(s - lse)
            ds = p * (_mm_nt(_two_heads(dyf[geo.own(own), :], low_q), vb) - delta)
            db_ref[...] += ds.reshape(2, tq, geo.band)
            dq2 = _mm(ds.astype(BF16), kb)
            dqf[geo.own(own), :] = jnp.where(low_q, dq2[:tq], dq2[tq:])
        dq_ref[...] = dqf[...].astype(BF16)

    (dq, dbias), carried = _ride_call(
        body, "swa_bwd_q_%d" % r, (D_B // 128, geo.nblk),
        [cur, prev, cur, nxt, prev, cur, nxt, cur, cur, bias_spec], [cur, bias_spec],
        [jax.ShapeDtypeStruct(q.shape, BF16), jax.ShapeDtypeStruct(bias.shape, F32)],
        [geo.plain(), geo.extended(), geo.extended(), geo.plain(), geo.plain()],
        (q, k, k, k, v, v, v, dy, st, bias), ride)
    return dq, dbias, carried


def _swa_bwd_kv(q, k, v, dy, st, bias_kv, r):
    geo = _SwaGeometry(q.shape[0], r)
    prev, cur, nxt = geo.specs()

    def body(k_ref, v_ref, qp, qc, qn, dp_, dc_, dn_, sp, sc, sn, b_ref, dk_ref, dv_ref, kf, vf, qf, dyf, stf, dkf,
             dvf):
        kf[...] = k_ref[...].astype(F32)
        vf[...] = v_ref[...].astype(F32)
        geo.fill(qf, qp, qc, qn)
        geo.fill(dyf, dp_, dc_, dn_)
        geo.fill(stf, sp, sc, sn)
        band = geo.band
        low_b = _low_lanes(band)
        bias = jnp.concatenate([b_ref[0], b_ref[1]], axis=1)
        half_lanes = HEAD_DIM // 2
        for own, around, j in geo.tiles():
            ks = kf[geo.own(own), :].astype(BF16)
            vs = vf[geo.own(own), :].astype(BF16)
            q2 = _two_heads(qf[geo.around(around), :], low_b)
            dy2 = _two_heads(dyf[geo.around(around), :], low_b)
            st_t = stf[geo.around(around), :].T
            lse = jnp.concatenate([st_t[:1, :], st_t[HEAD_DIM:HEAD_DIM + 1, :]], axis=1)
            delta = jnp.concatenate([st_t[half_lanes:half_lanes + 1, :],
                                     st_t[HEAD_DIM + half_lanes:HEAD_DIM + half_lanes + 1, :]], axis=1)
            p = jnp.exp(_mm_nt(ks, q2) + bias + (geo.outside(j, copies=2) - lse))
            ds = p * (_mm_nt(vs, dy2) - delta)
            dvf[geo.own(own), :] = _mm(p.astype(BF16), dy2)
            dkf[geo.own(own), :] = _mm(ds.astype(BF16), q2)
        dk_ref[...] = dkf[...].astype(BF16)
        dv_ref[...] = dvf[...].astype(BF16)

    return pl.pallas_call(
        body, name="swa_bwd_kv_%d" % r, grid=(D_B // 128, geo.nblk),
        in_specs=[cur, cur, prev, cur, nxt, prev, cur, nxt, prev, cur, nxt,
                  pl.BlockSpec((2, geo.tq, geo.band), lambda c, i: (c, 0, 0))],
        out_specs=[cur, cur],
        out_shape=[jax.ShapeDtypeStruct(q.shape, BF16), jax.ShapeDtypeStruct(q.shape, BF16)],
        scratch_shapes=[geo.plain(), geo.plain(), geo.extended(), geo.extended(), geo.extended(), geo.plain(),
                        geo.plain()],
        compiler_params=_cparams(("parallel", "parallel")),
    )(k, v, q, q, q, dy, dy, dy, st, st, st, bias_kv)


BIAS_ROWS = 16
BIAS_TN = 4096


def _bias_tiles(onehot, rel_bias_t):
    n = onehot.shape[0]

    def body(oh_ref, rb_ref, o_ref):
        o_ref[...] = sum(_mm_nt(piece, oh_ref[...]) for piece in _split3(rb_ref[...]))

    return pl.pallas_call(
        body, name="bias_tiles", grid=(n // BIAS_TN,),
        in_specs=[_rows(BIAS_TN, 128), _full((BIAS_ROWS, 128))],
        out_specs=pl.BlockSpec((BIAS_ROWS, BIAS_TN), lambda i: (0, i)),
        out_shape=jax.ShapeDtypeStruct((BIAS_ROWS, n), F32),
        compiler_params=_cparams(("parallel",)),
    )(onehot, rel_bias_t)


def _bias_bwd(onehot, dbias_rows, so_far, r):
    n = onehot.shape[0]

    def body(oh, d, prev_ref, g_ref):
        @pl.when(pl.program_id(0) == 0)
        def _():
            g_ref[...] = prev_ref[...]

        hi, lo, _ = _split3(d[...])
        g_ref[...] += _mm(hi, oh[...]) + _mm(lo, oh[...])

    return pl.pallas_call(
        body, name="bias_bwd_%d" % r, grid=(n // BIAS_TN,),
        in_specs=[_rows(BIAS_TN, 128), pl.BlockSpec((BIAS_ROWS, BIAS_TN), lambda i: (0, i)), _full((BIAS_ROWS, 128))],
        out_specs=_full((BIAS_ROWS, 128)),
        out_shape=jax.ShapeDtypeStruct((BIAS_ROWS, 128), F32),
        compiler_params=_cparams(("arbitrary",)),
    )(onehot, dbias_rows, so_far)


LATE = ("w_out", "w_ff1", "w_ff2", "w_ple_gate", "w_ple_proj")


def _local_step(x, p, tgt, w_in, late_shards, g_attn_pre, g_q, g_k, g_out_a, g_out_b, g_attn_post, rel_bias,
                g_mlp_pre, g_mlp_post, g_ple):
    s_len = x.shape[0]
    cc, ss = _rope_tables(s_len)
    gq2 = jnp.concatenate([g_q, g_q], axis=-1)
    gk2 = jnp.concatenate([g_k, g_k], axis=-1)
    ones128 = _group_ones(128)
    rel_bias_t = jnp.zeros((BIAS_ROWS, 128), F32).at[:N_HEADS_B, :N_BUCKETS].set(rel_bias.T)

    xn1, qpre, kpre, qa, kv, qb, kb, vb = _in_proj(x, g_attn_pre, w_in, cc, ss, gq2, gk2, ones128)
    ya, lse_a = _attn_a_fwd(qa, kv)

    tiles, joint = [], None
    for r in DILATIONS:
        tq = SWA_TQ
        onehot = _bucket_onehot(tq, r)
        bias = _bias_tiles(onehot, rel_bias_t)[:N_HEADS_B].reshape(N_HEADS_B, tq, tq + 2 * HALF_WIN)
        bias = jnp.where(_in_window(tq), bias, NEG_BIG)
        yb, lse_b, gathered = _swa_fwd(qb, kb, vb, bias, r, joint,
                                       _GatherRide(late_shards) if r == DILATIONS[-1] else None)
        tiles.append((onehot, bias))
        joint = (yb, lse_b)
    w_out, w_ff1, w_ff2, w_gate, w_ple = (_whole(n, g, mine) for n, g, mine in zip(LATE, gathered, late_shards))

    ycat, y2, h1, xn2 = _out_proj(ya, yb, x, g_out_a, g_out_b, w_out, g_attn_post, g_mlp_pre)
    u = _ff1(xn2, w_ff1)
    f2, h2, xn3 = _ff2(u, w_ff2, h1, g_mlp_post, g_ple)
    dh2, df2, dgl, dpp, loss, dg_ple, dg_mlp_post = _ple_loss(xn3, p, h2, f2, tgt, w_gate, w_ple, g_ple, g_mlp_post)

    grads = {"g_ple": dg_ple, "g_mlp_post": dg_mlp_post}
    grads["w_ple_gate"] = _dw(xn3, dgl, "dw_gate")
    grads["w_ple_proj"] = _dw(p, dpp, "dw_ple")
    grads["w_ff2"] = _dw(u, df2, "dw_ff2", relu2=True)
    du = _ff2_bwd(df2, w_ff2.T, u)
    grads["w_ff1"] = _dw(xn2, du, "dw_ff1")
    dh1, dy2, grads["g_mlp_pre"], grads["g_attn_post"] = _ff1_bwd(du, w_ff1, dh2, h1, y2, g_mlp_pre, g_attn_post)
    grads["w_out"] = _dw(ycat, dy2, "dw_out")
    dya, dyb, delta_a, st_b, grads["g_out_a"], grads["g_out_b"] = _out_proj_bwd(dy2, w_out, ya, yb, lse_b, g_out_a,
                                                                              g_out_b)

    pairs = _pair_sums(LATE, [grads[n] for n in LATE])

    dqr, dkv_t = _attn_a_bwd(qa, dya, kv, lse_a, delta_a)
    dkv_a = dkv_t.transpose(0, 2, 1)

    dqs, dks, dvs = [], [], []
    d_rel = jnp.zeros((BIAS_ROWS, 128), F32)
    for r, (onehot, bias) in zip(DILATIONS, tiles):
        dq_r, dbias, scattered = _swa_bwd_q(qb, kb, vb, dyb, st_b, bias, r,
                                            _ScatterRide(pairs) if r == DILATIONS[0] else None)
        if scattered:
            for n, half in zip(LATE, _chip_sums(LATE, pairs, scattered)):
                grads[n] = half
        bias_kv = jnp.flip(bias, axis=(1, 2))
        dk_r, dv_r = _swa_bwd_kv(qb, kb, vb, dyb, st_b, bias_kv, r)
        dbias_rows = jnp.pad(dbias.reshape(N_HEADS_B, -1), ((0, BIAS_ROWS - N_HEADS_B), (0, 0)))
        d_rel = _bias_bwd(onehot, dbias_rows, d_rel, r)
        dqs.append(dq_r)
        dks.append(dk_r)
        dvs.append(dv_r)
    grads["rel_bias"] = d_rel[:N_HEADS_B, :N_BUCKETS].T

    dproj, grad_x, grads["g_attn_pre"], dgq2, dgk2 = _in_proj_bwd(
        dqr, dkv_a, dqs, dks, dvs, qpre, kpre, x, dh1, g_attn_pre, w_in, cc, ss, gq2, gk2, ones128)
    grads["g_q"] = dgq2[:, :HEAD_DIM] + dgq2[:, HEAD_DIM:]
    grads["g_k"] = dgk2[:, :HEAD_DIM] + dgk2[:, HEAD_DIM:]
    grads["w_in"] = _dw(xn1, dproj, "dw_in")
    return loss, grad_x, grads


ANY = pl.BlockSpec(memory_space=pl.ANY)


def _position():
    return lax.axis_index("x"), lax.axis_index("y"), lax.axis_index("c")


def _other_chips(x, y):
    return [(2 * (1 - x) + y, (1 - x, y)), (2 * x + (1 - y), (x, 1 - y)), (2 * (1 - x) + (1 - y), (1 - x, 1 - y))]


def _cast_shards(shards):
    def body(*refs):
        n = len(refs) // 2
        for i_ref, o_ref in zip(refs[:n], refs[n:]):
            o_ref[...] = i_ref[...].astype(BF16)

    return pl.pallas_call(
        body, name="cast_shards",
        in_specs=[pl.BlockSpec(memory_space=pltpu.VMEM)] * len(shards),
        out_specs=[pl.BlockSpec(memory_space=pltpu.VMEM)] * len(shards),
        out_shape=[jax.ShapeDtypeStruct(s.shape, BF16) for s in shards],
        compiler_params=_cparams(),
    )(*shards)


def _gather_weights(shards):
    n = len(shards)

    ride = _GatherRide(shards)

    def body(*refs):
        ride.start(refs[:n], refs[n:2 * n], refs[2 * n:])
        ride.finish(refs[:n], refs[n:2 * n], refs[2 * n:])

    return pl.pallas_call(
        body, name="gather_weights",
        in_specs=[ANY] * n, out_specs=[ANY] * n,
        out_shape=ride.out_shape(), scratch_shapes=ride.scratch_shapes(),
    )(*shards)


class _GatherRide:
    def __init__(self, shards):
        self.operands = list(shards)
        self.n = len(shards)

    def out_shape(self):
        return [jax.ShapeDtypeStruct((N_CHIPS,) + s.shape, s.dtype) for s in self.operands]

    def scratch_shapes(self):
        return [pltpu.SemaphoreType.DMA((3, self.n))] * 4

    @staticmethod
    def _rows(ref, core):
        half = ref.shape[0] // 2
        return pl.ds(pl.multiple_of(core * half, 16), half)

    def _ici(self, ins, outs, sems, k, a, chip):
        x, y, c = _position()
        return pltpu.make_async_remote_copy(ins[a].at[self._rows(ins[a], c), :],
                                            outs[a].at[2 * x + y, self._rows(ins[a], c), :], sems[0].at[k, a],
                                            sems[1].at[k, a], device_id=(*chip, c), device_id_type=MESH)

    def _pass_on(self, ins, outs, sems, k, a, num, core):
        x, y, c = _position()
        half = outs[a].at[num, self._rows(ins[a], core), :]
        return pltpu.make_async_remote_copy(half, half, sems[2].at[k, a], sems[3].at[k, a], device_id=(x, y, 1 - c),
                                            device_id_type=MESH)

    def start(self, ins, outs, sems):
        x, y, _ = _position()
        for k, (_, chip) in enumerate(_other_chips(x, y)):
            for a in range(self.n):
                self._ici(ins, outs, sems, k, a, chip).start()

    def finish(self, ins, outs, sems):
        x, y, c = _position()
        others = _other_chips(x, y)
        for k, (num, chip) in enumerate(others):
            for a in range(self.n):
                landed = outs[a].at[num, self._rows(ins[a], c), :]
                pltpu.make_async_remote_copy(landed, landed, sems[0].at[k, a], sems[1].at[k, a], device_id=(*chip, c),
                                             device_id_type=MESH).wait_recv()
                self._pass_on(ins, outs, sems, k, a, num, c).start()
        for k, (num, chip) in enumerate(others):
            for a in range(self.n):
                self._pass_on(ins, outs, sems, k, a, num, 1 - c).wait_recv()
        for k, (num, chip) in enumerate(others):
            for a in range(self.n):
                self._ici(ins, outs, sems, k, a, chip).wait_send()
                self._pass_on(ins, outs, sems, k, a, num, c).wait_send()


def _send_sibling_half(grads, tag):
    n = len(grads)

    def body(*refs):
        ins, outs = refs[:n], refs[n:2 * n]
        send_sems, recv_sems = refs[2 * n:]
        x, y, c = _position()
        copies = []
        for a in range(n):
            half = ins[a].shape[1] // 2
            theirs = ins[a].at[:, pl.ds(pl.multiple_of((1 - c) * half, 8), half), :]
            cp = pltpu.make_async_remote_copy(theirs, outs[a], send_sems.at[a], recv_sems.at[a],
                                              device_id=(x, y, 1 - c), device_id_type=MESH)
            cp.start()
            copies.append(cp)
        for cp in copies:
            cp.wait()

    return pl.pallas_call(
        body, name="send_sibling_half_" + tag,
        in_specs=[ANY] * n, out_specs=[ANY] * n,
        out_shape=[jax.ShapeDtypeStruct((g.shape[0], g.shape[1] // 2, g.shape[2]), g.dtype) for g in grads],
        scratch_shapes=[pltpu.SemaphoreType.DMA((n,)), pltpu.SemaphoreType.DMA((n,))],
    )(*grads)


def _scatter_to_chips(pairs):
    n = len(pairs)
    ride = _ScatterRide(pairs)

    def body(*refs):
        ride.start(refs[:n], refs[n:2 * n], refs[2 * n:])
        ride.finish(refs[:n], refs[n:2 * n], refs[2 * n:])

    return pl.pallas_call(
        body, name="scatter_to_chips",
        in_specs=[ANY] * n, out_specs=[ANY] * n,
        out_shape=ride.out_shape(), scratch_shapes=ride.scratch_shapes(),
    )(*pairs)


class _ScatterRide:
    def __init__(self, pairs):
        self.operands = list(pairs)
        self.n = len(pairs)

    def out_shape(self):
        return [jax.ShapeDtypeStruct(g.shape, g.dtype) for g in self.operands]

    def scratch_shapes(self):
        return [pltpu.SemaphoreType.DMA((3, self.n))] * 2

    @staticmethod
    def _copy(ins, outs, sems, k, a, src_slot, dst_slot, chip):
        _, _, c = _position()
        return pltpu.make_async_remote_copy(ins[a].at[src_slot], outs[a].at[dst_slot], sems[0].at[k, a],
                                            sems[1].at[k, a], device_id=(*chip, c), device_id_type=MESH)

    def start(self, ins, outs, sems):
        x, y, _ = _position()
        for k, (num, chip) in enumerate(_other_chips(x, y)):
            for a in range(self.n):
                self._copy(ins, outs, sems, k, a, num, 2 * x + y, chip).start()

    def finish(self, ins, outs, sems):
        x, y, _ = _position()
        for k, (num, chip) in enumerate(_other_chips(x, y)):
            for a in range(self.n):
                self._copy(ins, outs, sems, k, a, 2 * x + y, num, chip).wait_recv()
        for k, (num, chip) in enumerate(_other_chips(x, y)):
            for a in range(self.n):
                self._copy(ins, outs, sems, k, a, num, 2 * x + y, chip).wait_send()


def _exchange_halves(halves):
    n = len(halves)

    def body(*refs):
        ins, outs = refs[:n], refs[n:2 * n]
        send_sems, recv_sems = refs[2 * n:]
        x, y, c = _position()
        copies = []
        for a in range(n):
            cp = pltpu.make_async_remote_copy(ins[a], outs[a], send_sems.at[a], recv_sems.at[a],
                                              device_id=(x, y, 1 - c), device_id_type=MESH)
            cp.start()
            copies.append(cp)
        for cp in copies:
            cp.wait()

    return pl.pallas_call(
        body, name="exchange_halves",
        in_specs=[ANY] * n, out_specs=[ANY] * n,
        out_shape=[jax.ShapeDtypeStruct(h.shape, h.dtype) for h in halves],
        scratch_shapes=[pltpu.SemaphoreType.DMA((n,)), pltpu.SemaphoreType.DMA((n,))],
    )(*halves)


def _allreduce_small(v):
    def body(v_ref, o_ref, buf, send_sems, recv_sems):
        x, y, c = _position()
        me = 4 * x + 2 * y + c
        peers = [(1 - x, y, c), (x, 1 - y, c), (x, y, 1 - c), (1 - x, 1 - y, c), (1 - x, y, 1 - c), (x, 1 - y, 1 - c),
                 (1 - x, 1 - y, 1 - c)]
        num = lambda d: 4 * d[0] + 2 * d[1] + d[2]
        buf[me] = v_ref[...]
        sends = []
        for k, peer in enumerate(peers):
            cp = pltpu.make_async_remote_copy(v_ref, buf.at[me], send_sems.at[k], recv_sems.at[k], device_id=peer,
                                              device_id_type=MESH)
            cp.start()
            sends.append(cp)
        for k, peer in enumerate(peers):
            pltpu.make_async_remote_copy(v_ref, buf.at[num(peer)], send_sems.at[k], recv_sems.at[k], device_id=peer,
                                         device_id_type=MESH).wait_recv()
        for cp in sends:
            cp.wait_send()
        total = buf[0]
        for d in range(1, 8):
            total = total + buf[d]
        o_ref[...] = total

    return pl.pallas_call(
        body, name="allreduce_small",
        in_specs=[pl.BlockSpec(memory_space=pltpu.VMEM)], out_specs=pl.BlockSpec(memory_space=pltpu.VMEM),
        out_shape=jax.ShapeDtypeStruct(v.shape, v.dtype),
        scratch_shapes=[pltpu.VMEM((8,) + v.shape, v.dtype), pltpu.SemaphoreType.DMA((7,)),
                        pltpu.SemaphoreType.DMA((7,))],
    )(v)


def _sum_leading(a, name):
    k, r, c = a.shape
    tr = min(r, 256)

    def body(a_ref, o_ref):
        total = a_ref[0].astype(F32)
        for i in range(1, k):
            total = total + a_ref[i].astype(F32)
        o_ref[...] = total

    return pl.pallas_call(
        body, name=name, grid=(r // tr,),
        in_specs=[pl.BlockSpec((k, tr, c), lambda i: (0, i, 0))],
        out_specs=pl.BlockSpec((tr, c), lambda i: (i, 0)),
        out_shape=jax.ShapeDtypeStruct((r, c), F32),
        compiler_params=_cparams(("parallel",)),
    )(a)


def _add(a, b, name):
    k, r, c = a.shape
    tr = min(r, 256)
    spec = pl.BlockSpec((k, tr, c), lambda i: (0, i, 0))

    def body(a_ref, b_ref, o_ref):
        o_ref[...] = (a_ref[...] + b_ref[...]).astype(BF16)

    return pl.pallas_call(
        body, name=name, grid=(r // tr,), in_specs=[spec, spec], out_specs=spec,
        out_shape=jax.ShapeDtypeStruct(a.shape, BF16), compiler_params=_cparams(("parallel",)),
    )(a, b)


def _adamw(w, g, m, v, name):
    r, c = w.shape
    tr = min(r, 256)
    spec = pl.BlockSpec((tr, c), lambda i: (i, 0))

    def body(w_ref, g_ref, m_ref, v_ref, d_ref, nm_ref, nv_ref):
        gv = g_ref[...]
        nm = ADAM_B1 * m_ref[...] + (1.0 - ADAM_B1) * gv
        nv = ADAM_B2 * v_ref[...] + (1.0 - ADAM_B2) * jnp.square(gv)
        m_hat = nm / (1.0 - ADAM_B1 ** ADAM_STEP)
        v_hat = nv / (1.0 - ADAM_B2 ** ADAM_STEP)
        d_ref[...] = -ADAM_LR * (m_hat / (jnp.sqrt(v_hat) + ADAM_EPS) + ADAM_WD * w_ref[...])
        nm_ref[...] = nm
        nv_ref[...] = nv

    return pl.pallas_call(
        body, name=name, grid=(r // tr,), in_specs=[spec] * 4, out_specs=[spec] * 3,
        out_shape=[jax.ShapeDtypeStruct(w.shape, F32)] * 3, compiler_params=_cparams(("parallel",)),
    )(w, g, m, v)


MATRICES = ("w_in", "w_out", "w_ff1", "w_ff2", "w_ple_gate", "w_ple_proj")
COLUMN_SHARDED = ("w_in", "w_ff1", "w_ple_proj")
SMALL = ("g_attn_pre", "g_q", "g_k", "g_out_a", "g_out_b", "g_attn_post", "rel_bias", "g_mlp_pre", "g_mlp_post",
         "g_ple")
WEIGHT_ORDER = ("w_in", "g_attn_pre", "g_q", "g_k", "g_out_a", "g_out_b", "w_out", "g_attn_post", "rel_bias",
                "g_mlp_pre", "w_ff1", "w_ff2", "g_mlp_post", "g_ple", "w_ple_gate", "w_ple_proj")
PACK_ROWS, PACK_COLS = 8, 1024


def _chip():
    return 2 * lax.axis_index("x") + lax.axis_index("y")


def _whole(name, gathered, mine):
    g = lax.dynamic_update_slice_in_dim(gathered, mine[None], _chip(), axis=0)
    if name in COLUMN_SHARDED:
        return g.transpose(1, 0, 2).reshape(g.shape[1], N_CHIPS * g.shape[2])
    return g.reshape(N_CHIPS * g.shape[1], g.shape[2])


def _pair_sums(names, grads):
    by_chip = []
    for n, g in zip(names, grads):
        if n in COLUMN_SHARDED:
            by_chip.append(g.reshape(g.shape[0], N_CHIPS, g.shape[1] // N_CHIPS).transpose(1, 0, 2))
        else:
            by_chip.append(g.reshape(N_CHIPS, g.shape[0] // N_CHIPS, g.shape[1]))
    c = lax.axis_index("c")
    pairs = []
    for n, g, other in zip(names, by_chip, _send_sibling_half(by_chip, names[0])):
        half = g.shape[1] // 2
        pairs.append(_add(lax.dynamic_slice_in_dim(g, c * half, half, axis=1), other, "pair_sum_" + n))
    return pairs


def _chip_sums(names, pairs, scattered):
    halves = []
    for n, pair, got in zip(names, pairs, scattered):
        own = lax.dynamic_slice_in_dim(pair, _chip(), 1, axis=0)
        halves.append(_sum_leading(lax.dynamic_update_slice_in_dim(got, own, _chip(), axis=0), "chip_sum_" + n))
    return halves


def _pack_small(values, extra=None):
    flat = [values[n].reshape(-1) for n in SMALL]
    used = sum(f.shape[0] for f in flat)
    tail = jnp.zeros((PACK_ROWS * PACK_COLS - used - 1,), F32)
    last = jnp.zeros((1,), F32) if extra is None else extra.reshape(1)
    return jnp.concatenate(flat + [tail, last]).reshape(PACK_ROWS, PACK_COLS)


def _unpack_small(packed, like):
    flat = packed.reshape(-1)
    out, o = {}, 0
    for n in SMALL:
        size = like[n].size
        out[n] = flat[o:o + size].reshape(like[n].shape)
        o += size
    return out, flat[-1]


def kernel(x, p, w_in, g_attn_pre, g_q, g_k, g_out_a, g_out_b, w_out, g_attn_post, rel_bias, g_mlp_pre, w_ff1, w_ff2, g_mlp_post, g_ple, w_ple_gate, w_ple_proj, loss_target, m_w_in, m_g_attn_pre, m_g_q, m_g_k, m_g_out_a, m_g_out_b, m_w_out, m_g_attn_post, m_rel_bias, m_g_mlp_pre, m_w_ff1, m_w_ff2, m_g_mlp_post, m_g_ple, m_w_ple_gate, m_w_ple_proj, v_w_in, v_g_attn_pre, v_g_q, v_g_k, v_g_out_a, v_g_out_b, v_w_out, v_g_attn_post, v_rel_bias, v_g_mlp_pre, v_w_ff1, v_w_ff2, v_g_mlp_post, v_g_ple, v_w_ple_gate, v_w_ple_proj):
    given = dict(locals())
    weights = {n: given[n] for n in WEIGHT_ORDER}
    shards = {n: weights[n][0] for n in MATRICES}

    c = lax.axis_index("c")
    own = dict(zip(MATRICES, _cast_shards([shards[n] for n in MATRICES])))
    w_in_whole = _whole("w_in", _gather_weights([own["w_in"]])[0], own["w_in"])

    loss, grad_x, grads = _local_step(
        x[0], p[0, 0], loss_target[0], w_in_whole, [own[n] for n in LATE], g_attn_pre, g_q, g_k, g_out_a, g_out_b,
        g_attn_post, rel_bias, g_mlp_pre, g_mlp_post, g_ple)

    pairs = _pair_sums(["w_in"], [grads["w_in"]])
    grads["w_in"] = _chip_sums(["w_in"], pairs, _scatter_to_chips(pairs))[0]
    halves = [grads[n] for n in MATRICES]
    grad_w = {}
    for n, mine, theirs in zip(MATRICES, halves, _exchange_halves(halves)):
        half = mine.shape[0]
        g = jnp.zeros((2 * half, mine.shape[1]), F32)
        g = lax.dynamic_update_slice_in_dim(g, mine, c * half, axis=0)
        grad_w[n] = lax.dynamic_update_slice_in_dim(g, theirs, (1 - c) * half, axis=0)

    small_like = {n: weights[n] for n in SMALL}
    reduced = _allreduce_small(_pack_small({n: grads[n] for n in SMALL}, extra=loss))
    grad_small, loss_total = _unpack_small(reduced, small_like)

    delta, new_m, new_v = {}, {}, {}
    for n in MATRICES:
        d, nm, nv = _adamw(shards[n], grad_w[n], given["m_" + n][0], given["v_" + n][0], "adamw_" + n)
        delta[n], new_m[n], new_v[n] = d[None], nm[None], nv[None]
        grad_w[n] = grad_w[n][None]
    d, nm, nv = _adamw(_pack_small(small_like), reduced, _pack_small({n: given["m_" + n] for n in SMALL}),
                       _pack_small({n: given["v_" + n] for n in SMALL}), "adamw_small")
    d_small, nm_small, nv_small = (_unpack_small(a, small_like)[0] for a in (d, nm, nv))
    for n in SMALL:
        grad_w[n], delta[n], new_m[n], new_v[n] = grad_small[n], d_small[n], nm_small[n], nv_small[n]

    return (loss_total, grad_x[None], *[grad_w[n] for n in WEIGHT_ORDER], *[delta[n] for n in WEIGHT_ORDER],
            *[new_m[n] for n in WEIGHT_ORDER], *[new_v[n] for n in WEIGHT_ORDER])
```

```python
import functools
import math

import jax
import jax.numpy as jnp
from jax import lax
from jax.experimental import pallas as pl
from jax.experimental.pallas import tpu as pltpu

F32 = jnp.float32
BF16 = jnp.bfloat16

D_MODEL = 1024
HEAD_DIM = 64
N_HEADS_A = 8
N_KV_A = 2
GROUP_A = N_HEADS_A // N_KV_A
N_HEADS_B = 8
D_A = N_HEADS_A * HEAD_DIM
D_KV_A = N_KV_A * HEAD_DIM
D_B = N_HEADS_B * HEAD_DIM
D_IN = D_A + 2 * D_KV_A + 3 * D_B
D_FF = 4 * D_MODEL
D_PLE = 256
GRID_W = 64
ROPE_THETA = 10000.0
DILATIONS = (1, 4, 16)
HALF_WIN = 64
N_BUCKETS = 32
MAX_DISTANCE = 1024
EPS = 1e-6
NEG_BIG = -1e30
Q_SCALE = HEAD_DIM ** -0.5

ADAM_LR = 0.001
ADAM_B1 = 0.9
ADAM_B2 = 0.999
ADAM_EPS = 1e-08
ADAM_WD = 0.01
ADAM_STEP = 10

N_CHIPS = 4
MESH = pl.DeviceIdType.MESH

ROW_TILE = 512
ATT_TQ = 256
ATT_TQ_BWD = 512
ATT_TK_FWD = 2048
ATT_UNROLL_FWD = 8
ATT_TK_BWD = 512
ATT_UNROLL_BWD = 8
SWA_TQ = 128
SWA_MIN_BLOCK = 4096
DW_TS = 2048
VMEM_LIMIT = 56 * 1024 * 1024

NT = (((1,), (1,)), ((), ()))
TN = (((0,), (0,)), ((), ()))


def _cparams(sem=None, vmem=VMEM_LIMIT):
    return pltpu.CompilerParams(dimension_semantics=sem, vmem_limit_bytes=vmem)


def _full(shape):
    n = len(shape)
    return pl.BlockSpec(shape, lambda *_: (0,) * n)


def _rows(tm, width):
    return pl.BlockSpec((tm, width), lambda i: (i, 0))


def _split3(a):
    a1 = a.astype(BF16)
    r = a - a1.astype(F32)
    a2 = r.astype(BF16)
    a3 = (r - a2.astype(F32)).astype(BF16)
    return a1, a2, a3


def _xdot(a, sel):
    a1, a2, a3 = _split3(a)
    d = lambda p: jnp.dot(p, sel, preferred_element_type=F32)
    return d(a1) + d(a2) + d(a3)


def _mm(a, b):
    return jnp.dot(a, b, preferred_element_type=F32)


def _mm_nt(a, b):
    return lax.dot_general(a, b, NT, preferred_element_type=F32)


def _mm_tn(a, b):
    return lax.dot_general(a, b, TN, preferred_element_type=F32)


def _rms_stats(x):
    r = lax.rsqrt(jnp.mean(x * x, axis=-1, keepdims=True) + EPS)
    return x * r, r


def _rms_bwd(dy, xh, r, g):
    gdy = dy * g
    dx = r * (gdy - xh * jnp.mean(gdy * xh, axis=-1, keepdims=True))
    dg = jnp.sum(dy * xh, axis=0, keepdims=True)
    return dx, dg


def _acc_out(ref, val):
    @pl.when(pl.program_id(0) == 0)
    def _():
        ref[...] = jnp.zeros_like(ref)

    ref[...] += val


def _swap_halves(x, first_half):
    return jnp.where(first_half, pltpu.roll(x, 96, 1), pltpu.roll(x, 32, 1))


def _first_half_mask(shape):
    return (lax.broadcasted_iota(jnp.int32, shape, 1) % HEAD_DIM) < (HEAD_DIM // 2)


def _rope_tables(s_len):
    t = jnp.arange(s_len)
    row = (t // GRID_W).astype(F32)
    col = (t % GRID_W).astype(F32)
    n_axis = HEAD_DIM // 4
    inv_freq = ROPE_THETA ** (-jnp.arange(n_axis, dtype=F32) / n_axis)
    ang = jnp.concatenate([row[:, None] * inv_freq, col[:, None] * inv_freq], axis=-1)
    c, s = jnp.cos(ang), jnp.sin(ang)
    cc = jnp.concatenate([c, c, c, c], axis=-1)
    ss = jnp.concatenate([-s, s, -s, s], axis=-1)
    return cc, ss


def _group_ones(width):
    i = jnp.arange(width)
    return (i[:, None] // HEAD_DIM == i[None, :] // HEAD_DIM).astype(BF16)


def _t5_bucket(rel):
    nb = N_BUCKETS // 2
    max_exact = nb // 2
    side = jnp.where(rel > 0, nb, 0)
    n = jnp.abs(rel)
    large = max_exact + (jnp.log(jnp.maximum(n, max_exact).astype(F32) / max_exact)
                         / math.log(MAX_DISTANCE / max_exact) * (nb - max_exact)).astype(jnp.int32)
    large = jnp.minimum(large, nb - 1)
    return side + jnp.where(n < max_exact, n, large)


def _in_window(tq):
    qi = jnp.arange(tq)
    kj = jnp.arange(tq + 2 * HALF_WIN)
    return jnp.abs(kj[None, :] - HALF_WIN - qi[:, None]) <= HALF_WIN


def _bucket_onehot(tq, dilation):
    qi = jnp.arange(tq)
    kj = jnp.arange(tq + 2 * HALF_WIN)
    rel = kj[None, :] - HALF_WIN - qi[:, None]
    bucket = _t5_bucket(rel * dilation).reshape(-1)
    return (bucket[:, None] == jnp.arange(128)[None, :]).astype(BF16)


def _in_proj(x, g1, w_in, cc, ss, gq2, gk2, ones128):
    s_len = x.shape[0]
    tm = min(ROW_TILE, s_len)

    def body(x_ref, g_ref, w_ref, cc_ref, ss_ref, gq_ref, gk_ref, one_ref,
             xn_ref, qpre_ref, kpre_ref, qa_ref, kv_ref, qb_ref, kb_ref, vb_ref):
        xh, _ = _rms_stats(x_ref[...])
        xn = (xh * g_ref[...]).astype(BF16)
        xn_ref[...] = xn
        proj = _mm(xn, w_ref[...])
        first_half = _first_half_mask((tm, 128))
        ones = one_ref[...]
        cc_t, ss_t = cc_ref[...], ss_ref[...]

        def norm_rope(xc, gain):
            ms = _xdot(xc * xc, ones) * (1.0 / HEAD_DIM)
            y = xc * lax.rsqrt(ms + EPS) * gain
            return y * cc_t + _swap_halves(y, first_half) * ss_t

        qpre_ref[...] = proj[:, :D_A]
        kpre_ref[...] = proj[:, D_A:D_A + D_KV_A]
        for c in range(D_A // 128):
            y = norm_rope(proj[:, 128 * c:128 * (c + 1)], gq_ref[...])
            qa_ref[:, 128 * c:128 * (c + 1)] = (y * Q_SCALE).astype(BF16)
        ka = norm_rope(proj[:, D_A:D_A + D_KV_A], gk_ref[...])
        o = D_A + D_KV_A
        va = proj[:, o:o + D_KV_A]
        low = _low_lanes(tm)
        kv_ref[0] = jnp.where(low, ka, pltpu.roll(va, HEAD_DIM, 1)).astype(BF16)
        kv_ref[1] = jnp.where(low, pltpu.roll(ka, HEAD_DIM, 1), va).astype(BF16)
        o += D_KV_A
        qb_ref[...] = (proj[:, o:o + D_B] * Q_SCALE).astype(BF16)
        kb_ref[...] = proj[:, o + D_B:o + 2 * D_B].astype(BF16)
        vb_ref[...] = proj[:, o + 2 * D_B:o + 3 * D_B].astype(BF16)

    sds = jax.ShapeDtypeStruct
    return pl.pallas_call(
        body, name="in_proj", grid=(s_len // tm,),
        in_specs=[_rows(tm, D_MODEL), _full((1, D_MODEL)), _full((D_MODEL, D_IN)), _rows(tm, 128), _rows(tm, 128),
                  _full((1, 128)), _full((1, 128)), _full((128, 128))],
        out_specs=[_rows(tm, D_MODEL), _rows(tm, D_A), _rows(tm, D_KV_A), _rows(tm, D_A),
                   pl.BlockSpec((N_KV_A, tm, 128), lambda i: (0, i, 0)), _rows(tm, D_B), _rows(tm, D_B),
                   _rows(tm, D_B)],
        out_shape=[sds((s_len, D_MODEL), BF16), sds((s_len, D_A), F32), sds((s_len, D_KV_A), F32),
                   sds((s_len, D_A), BF16), sds((N_KV_A, s_len, 128), BF16),
                   sds((s_len, D_B), BF16), sds((s_len, D_B), BF16), sds((s_len, D_B), BF16)],
        compiler_params=_cparams(("parallel",)),
    )(x, g1, w_in, cc, ss, gq2, gk2, ones128)


def _stat_spec(tm):
    return pl.BlockSpec((N_HEADS_B, tm, 1), lambda i: (0, i, 0))


def _out_proj(ya, yb, x, g_a, g_b, w_out, g_post, g_mlp_pre):
    s_len = x.shape[0]
    tm = min(ROW_TILE, s_len)

    def body(ya_ref, yb_ref, x_ref, ga_ref, gb_ref, w_ref, gp_ref, gm_ref, ycat_ref, y2_ref, h1_ref, xn2_ref):
        ah, _ = _rms_stats(ya_ref[...])
        bh, _ = _rms_stats(yb_ref[...])
        ycat = jnp.concatenate([ah * ga_ref[...], bh * gb_ref[...]], axis=-1).astype(BF16)
        ycat_ref[...] = ycat
        y2 = _mm(ycat, w_ref[...])
        y2_ref[...] = y2
        y2h, _ = _rms_stats(y2)
        h1 = x_ref[...] + y2h * gp_ref[...]
        h1_ref[...] = h1
        h1h, _ = _rms_stats(h1)
        xn2_ref[...] = (h1h * gm_ref[...]).astype(BF16)

    sds = jax.ShapeDtypeStruct
    return pl.pallas_call(
        body, name="out_proj", grid=(s_len // tm,),
        in_specs=[_rows(tm, D_A), _rows(tm, D_B), _rows(tm, D_MODEL), _full((1, D_A)), _full((1, D_B)),
                  _full((D_MODEL, D_MODEL)), _full((1, D_MODEL)), _full((1, D_MODEL))],
        out_specs=[_rows(tm, D_MODEL)] * 4,
        out_shape=[sds((s_len, D_MODEL), BF16), sds((s_len, D_MODEL), F32), sds((s_len, D_MODEL), F32),
                   sds((s_len, D_MODEL), BF16)],
        compiler_params=_cparams(("parallel",)),
    )(ya, yb, x, g_a, g_b, w_out, g_post, g_mlp_pre)


def _ff1(xn2, w_ff1):
    s_len = xn2.shape[0]
    tm = min(ROW_TILE, s_len)

    def body(x_ref, w_ref, u_ref):
        u_ref[...] = _mm(x_ref[...], w_ref[...])

    return pl.pallas_call(
        body, name="ff1", grid=(s_len // tm,),
        in_specs=[_rows(tm, D_MODEL), _full((D_MODEL, D_FF))],
        out_specs=_rows(tm, D_FF),
        out_shape=jax.ShapeDtypeStruct((s_len, D_FF), F32),
        compiler_params=_cparams(("parallel",)),
    )(xn2, w_ff1)


def _ff2(u, w_ff2, h1, g_post, g_ple):
    s_len = u.shape[0]
    tm = min(ROW_TILE, s_len)

    def body(u_ref, w_ref, h1_ref, gp_ref, gl_ref, f2_ref, h2_ref, xn3_ref):
        f = jnp.square(jnp.maximum(u_ref[...], 0.0)).astype(BF16)
        f2 = _mm(f, w_ref[...])
        f2_ref[...] = f2
        f2h, _ = _rms_stats(f2)
        h2 = h1_ref[...] + f2h * gp_ref[...]
        h2_ref[...] = h2
        h2h, _ = _rms_stats(h2)
        xn3_ref[...] = (h2h * gl_ref[...]).astype(BF16)

    sds = jax.ShapeDtypeStruct
    return pl.pallas_call(
        body, name="ff2", grid=(s_len // tm,),
        in_specs=[_rows(tm, D_FF), _full((D_FF, D_MODEL)), _rows(tm, D_MODEL), _full((1, D_MODEL)),
                  _full((1, D_MODEL))],
        out_specs=[_rows(tm, D_MODEL)] * 3,
        out_shape=[sds((s_len, D_MODEL), F32), sds((s_len, D_MODEL), F32), sds((s_len, D_MODEL), BF16)],
        compiler_params=_cparams(("parallel",)),
    )(u, w_ff2, h1, g_post, g_ple)


def _ple_loss(xn3, p, h2, f2, tgt, w_gate, w_ple, g_ple, g_mlp_post):
    s_len = h2.shape[0]
    tm = min(ROW_TILE, s_len)

    def body(xn3_ref, p_ref, h2_ref, f2_ref, t_ref, wg_ref, wp_ref, gl_ref, gp_ref,
             dh2_ref, df2_ref, dgl_ref, dpp_ref, loss_ref, dgple_ref, dgpost_ref):
        gate = jax.nn.sigmoid(_mm(xn3_ref[...], wg_ref[...]))
        pp = _mm(p_ref[...].astype(BF16), wp_ref[...])
        h2 = h2_ref[...]
        err = h2 + gate * pp - t_ref[...]
        sq = jnp.sum(jnp.sum(err * err, axis=1, keepdims=True), axis=0, keepdims=True)
        _acc_out(loss_ref, sq * (0.5 / D_MODEL))
        dh3 = err * (1.0 / D_MODEL)
        dgl = (dh3 * pp) * gate * (1.0 - gate)
        dgl_b = dgl.astype(BF16)
        dgl_ref[...] = dgl_b
        dpp_ref[...] = (dh3 * gate).astype(BF16)
        dxn3 = _mm_nt(dgl_b, wg_ref[...])
        h2h, r2 = _rms_stats(h2)
        dx, dg = _rms_bwd(dxn3, h2h, r2, gl_ref[...])
        _acc_out(dgple_ref, dg)
        dh2 = dh3 + dx
        dh2_ref[...] = dh2
        f2h, rf = _rms_stats(f2_ref[...])
        df2, dg = _rms_bwd(dh2, f2h, rf, gp_ref[...])
        _acc_out(dgpost_ref, dg)
        df2_ref[...] = df2.astype(BF16)

    sds = jax.ShapeDtypeStruct
    return pl.pallas_call(
        body, name="ple_loss", grid=(s_len // tm,),
        in_specs=[_rows(tm, D_MODEL), _rows(tm, D_PLE), _rows(tm, D_MODEL), _rows(tm, D_MODEL), _rows(tm, D_MODEL),
                  _full((D_MODEL, D_MODEL)), _full((D_PLE, D_MODEL)), _full((1, D_MODEL)), _full((1, D_MODEL))],
        out_specs=[_rows(tm, D_MODEL)] * 3 + [_rows(tm, D_MODEL), _full((1, 1)), _full((1, D_MODEL)),
                                              _full((1, D_MODEL))],
        out_shape=[sds((s_len, D_MODEL), F32), sds((s_len, D_MODEL), BF16), sds((s_len, D_MODEL), BF16),
                   sds((s_len, D_MODEL), BF16), sds((1, 1), F32), sds((1, D_MODEL), F32), sds((1, D_MODEL), F32)],
        compiler_params=_cparams(("arbitrary",)),
    )(xn3, p, h2, f2, tgt, w_gate, w_ple, g_ple, g_mlp_post)


def _ff2_bwd(df2, w_ff2_t, u):
    s_len = u.shape[0]
    tm = min(ROW_TILE, s_len)

    def body(d_ref, w_ref, u_ref, du_ref):
        df = _mm(d_ref[...], w_ref[...])
        du_ref[...] = (df * (2.0 * jnp.maximum(u_ref[...], 0.0))).astype(BF16)

    return pl.pallas_call(
        body, name="ff2_bwd", grid=(s_len // tm,),
        in_specs=[_rows(tm, D_MODEL), _full((D_MODEL, D_FF)), _rows(tm, D_FF)],
        out_specs=_rows(tm, D_FF),
        out_shape=jax.ShapeDtypeStruct((s_len, D_FF), BF16),
        compiler_params=_cparams(("parallel",)),
    )(df2, w_ff2_t, u)


def _ff1_bwd(du, w_ff1, dh2, h1, y2, g_mlp_pre, g_post):
    s_len = du.shape[0]
    tm = min(ROW_TILE, s_len)

    def body(du_ref, w_ref, dh2_ref, h1_ref, y2_ref, gm_ref, gp_ref, dh1_ref, dy2_ref, dgm_ref, dgp_ref):
        dxn2 = _mm_nt(du_ref[...], w_ref[...])
        h1h, r1 = _rms_stats(h1_ref[...])
        dx, dg = _rms_bwd(dxn2, h1h, r1, gm_ref[...])
        _acc_out(dgm_ref, dg)
        dh1 = dh2_ref[...] + dx
        dh1_ref[...] = dh1
        y2h, ry = _rms_stats(y2_ref[...])
        dy2, dg = _rms_bwd(dh1, y2h, ry, gp_ref[...])
        _acc_out(dgp_ref, dg)
        dy2_ref[...] = dy2.astype(BF16)

    sds = jax.ShapeDtypeStruct
    return pl.pallas_call(
        body, name="ff1_bwd", grid=(s_len // tm,),
        in_specs=[_rows(tm, D_FF), _full((D_MODEL, D_FF)), _rows(tm, D_MODEL), _rows(tm, D_MODEL),
                  _rows(tm, D_MODEL), _full((1, D_MODEL)), _full((1, D_MODEL))],
        out_specs=[_rows(tm, D_MODEL), _rows(tm, D_MODEL), _full((1, D_MODEL)), _full((1, D_MODEL))],
        out_shape=[sds((s_len, D_MODEL), F32), sds((s_len, D_MODEL), BF16), sds((1, D_MODEL), F32),
                   sds((1, D_MODEL), F32)],
        compiler_params=_cparams(("arbitrary",)),
    )(du, w_ff1, dh2, h1, y2, g_mlp_pre, g_post)


def _out_proj_bwd(dy2, w_out, ya, yb, lse_b, g_a, g_b):
    s_len = ya.shape[0]
    tm = min(ROW_TILE, s_len)

    def body(d_ref, w_ref, ya_ref, yb_ref, lse_ref, ga_ref, gb_ref, dya_ref, dyb_ref, da_ref, st_ref, dga_ref,
             dgb_ref):
        dycat = _mm_nt(d_ref[...], w_ref[...])
        lane = lax.broadcasted_iota(jnp.int32, (tm, 128), 1)
        low = lane < HEAD_DIM
        is_lse = (lane % HEAD_DIM) < (HEAD_DIM // 2)

        def head_sums(prod_chunk):
            return (jnp.sum(jnp.where(low, prod_chunk, 0.0), axis=1, keepdims=True),
                    jnp.sum(jnp.where(low, 0.0, prod_chunk), axis=1, keepdims=True))

        ya = ya_ref[...]
        yh, r = _rms_stats(ya)
        dya, dg = _rms_bwd(dycat[:, :D_A], yh, r, ga_ref[...])
        _acc_out(dga_ref, dg)
        dya_ref[...] = dya
        prod = dya * ya
        for c in range(D_A // 128):
            da_ref[2 * c], da_ref[2 * c + 1] = head_sums(prod[:, 128 * c:128 * (c + 1)])

        yb = yb_ref[...]
        yh, r = _rms_stats(yb)
        dyb, dg = _rms_bwd(dycat[:, D_A:], yh, r, gb_ref[...])
        _acc_out(dgb_ref, dg)
        dyb_ref[...] = dyb.astype(BF16)
        prod = dyb * yb
        for c in range(D_B // 128):
            sl = slice(128 * c, 128 * (c + 1))
            d_lo, d_hi = head_sums(prod[:, sl])
            st_ref[:, sl] = jnp.where(is_lse, lse_ref[:, sl], jnp.where(low, d_lo, d_hi))

    sds = jax.ShapeDtypeStruct
    return pl.pallas_call(
        body, name="out_proj_bwd", grid=(s_len // tm,),
        in_specs=[_rows(tm, D_MODEL), _full((D_MODEL, D_MODEL)), _rows(tm, D_A), _rows(tm, D_B), _rows(tm, D_B),
                  _full((1, D_A)), _full((1, D_B))],
        out_specs=[_rows(tm, D_A), _rows(tm, D_B), _stat_spec(tm), _rows(tm, D_B), _full((1, D_A)),
                   _full((1, D_B))],
        out_shape=[sds((s_len, D_A), F32), sds((s_len, D_B), BF16), sds((N_HEADS_A, s_len, 1), F32),
                   sds((s_len, D_B), F32), sds((1, D_A), F32), sds((1, D_B), F32)],
        compiler_params=_cparams(("arbitrary",)),
    )(dy2, w_out, ya, yb, lse_b, g_a, g_b)


def _in_proj_bwd(dqr, dkv, dqb, dkb, dvb, qpre, kpre, x, dh1, g1, w_in, cc, ss, gq2, gk2, ones128):
    s_len = x.shape[0]
    tm = min(ROW_TILE, s_len)

    def body(dqr_ref, dkv_ref, dq0, dq1, dq2, dk0, dk1, dk2, dv0, dv1, dv2, qpre_ref, kpre_ref, x_ref,
             dh1_ref, g_ref, w_ref, cc_ref, ss_ref, gq_ref, gk_ref, one_ref, dproj_ref, gx_ref, dg1_ref, dgq_ref,
             dgk_ref):
        low = _low_lanes(tm)
        dkr = jnp.where(low, dkv_ref[0], pltpu.roll(dkv_ref[1], HEAD_DIM, 1))
        dva = jnp.where(low, pltpu.roll(dkv_ref[0], HEAD_DIM, 1), dkv_ref[1])
        first_half = _first_half_mask((tm, 128))
        ones = one_ref[...]
        cc_t, ss_t = cc_ref[...], ss_ref[...]

        def norm_rope_bwd(dy, xc, gain):
            dn = dy * cc_t - _swap_halves(dy, first_half) * ss_t
            r = lax.rsqrt(_xdot(xc * xc, ones) * (1.0 / HEAD_DIM) + EPS)
            xh = xc * r
            gdy = dn * gain
            dx = r * (gdy - xh * (_xdot(gdy * xh, ones) * (1.0 / HEAD_DIM)))
            return dx, jnp.sum(dn * xh, axis=0, keepdims=True)

        dgq = jnp.zeros((1, 128), F32)
        parts = []
        for c in range(D_A // 128):
            sl = slice(128 * c, 128 * (c + 1))
            dx, dg = norm_rope_bwd(dqr_ref[:, sl] * Q_SCALE, qpre_ref[:, sl], gq_ref[...])
            parts.append(dx)
            dgq = dgq + dg
        dxk, dgk = norm_rope_bwd(dkr, kpre_ref[...], gk_ref[...])
        _acc_out(dgq_ref, dgq)
        _acc_out(dgk_ref, dgk)
        total = lambda a, b, c: a[...].astype(F32) + b[...].astype(F32) + c[...].astype(F32)
        parts += [dxk, dva, total(dq0, dq1, dq2) * Q_SCALE, total(dk0, dk1, dk2), total(dv0, dv1, dv2)]
        dproj = jnp.concatenate(parts, axis=-1).astype(BF16)
        dproj_ref[...] = dproj
        dxn = _mm_nt(dproj, w_ref[...])
        xh, r = _rms_stats(x_ref[...])
        dx, dg = _rms_bwd(dxn, xh, r, g_ref[...])
        _acc_out(dg1_ref, dg)
        gx_ref[...] = dh1_ref[...] + dx

    sds = jax.ShapeDtypeStruct
    return pl.pallas_call(
        body, name="in_proj_bwd", grid=(s_len // tm,),
        in_specs=[_rows(tm, D_A), pl.BlockSpec((N_KV_A, tm, 128), lambda i: (0, i, 0))] + [_rows(tm, D_B)] * 9
                 + [_rows(tm, D_A), _rows(tm, D_KV_A), _rows(tm, D_MODEL), _rows(tm, D_MODEL),
                    _full((1, D_MODEL)), _full((D_MODEL, D_IN)), _rows(tm, 128), _rows(tm, 128), _full((1, 128)),
                    _full((1, 128)), _full((128, 128))],
        out_specs=[_rows(tm, D_IN), _rows(tm, D_MODEL), _full((1, D_MODEL)), _full((1, 128)), _full((1, 128))],
        out_shape=[sds((s_len, D_IN), BF16), sds((s_len, D_MODEL), F32), sds((1, D_MODEL), F32),
                   sds((1, 128), F32), sds((1, 128), F32)],
        compiler_params=_cparams(("arbitrary",)),
    )(dqr, dkv, *dqb, *dkb, *dvb, qpre, kpre, x, dh1, g1, w_in, cc, ss, gq2, gk2, ones128)


def _dw(a, b, name, relu2=False):
    s_len, ka = a.shape
    n = b.shape[1]
    ts = min(DW_TS, s_len)
    bk = min(ka, 1024)
    bn = n if n % 1024 else 1024

    def body(a_ref, b_ref, o_ref):
        @pl.when(pl.program_id(2) == 0)
        def _():
            o_ref[...] = jnp.zeros_like(o_ref)

        av = a_ref[...]
        if relu2:
            av = jnp.square(jnp.maximum(av, 0.0))
        o_ref[...] += _mm_tn(av.astype(BF16), b_ref[...])

    return pl.pallas_call(
        body, name=name, grid=(ka // bk, n // bn, s_len // ts),
        in_specs=[pl.BlockSpec((ts, bk), lambda i, j, k: (k, i)), pl.BlockSpec((ts, bn), lambda i, j, k: (k, j))],
        out_specs=pl.BlockSpec((bk, bn), lambda i, j, k: (i, j)),
        out_shape=jax.ShapeDtypeStruct((ka, n), F32),
        compiler_params=_cparams(("parallel", "parallel", "arbitrary")),
    )(a, b)


def _stack_heads(block, low, data_low):
    parts = []
    for c in range(GROUP_A // 2):
        chunk = block[:, 128 * c:128 * (c + 1)]
        swapped = pltpu.roll(chunk, HEAD_DIM, 1)
        for h_low in (chunk, swapped) if data_low else (swapped, chunk):
            parts.append(jnp.where(low, h_low, 0.0) if data_low else jnp.where(low, 0.0, h_low))
    return jnp.concatenate(parts, axis=0).astype(BF16)


def _unstack_heads(stacked, low, tq, data_low):
    chunks = []
    for c in range(GROUP_A // 2):
        even = stacked[2 * c * tq:(2 * c + 1) * tq]
        odd = stacked[(2 * c + 1) * tq:(2 * c + 2) * tq]
        if data_low:
            chunks.append(jnp.where(low, even, pltpu.roll(odd, HEAD_DIM, 1)))
        else:
            chunks.append(jnp.where(low, pltpu.roll(even, HEAD_DIM, 1), odd))
    return chunks


def _attn_a_fwd(qa, kv):
    s_len = kv.shape[1]
    tq = min(ATT_TQ, s_len)
    tk = min(ATT_TK_FWD, s_len)
    rows = GROUP_A * tq

    def body(q_ref, kv_ref, o_ref, lse_ref):
        low = _low_lanes(tq)
        low_k = _low_lanes(tk)
        q = _stack_heads(q_ref[...].astype(F32), low, data_low=True)

        def block(j, m, acc):
            kvj = kv_ref[0, pl.ds(pl.multiple_of(j * tk, tk), tk), :]
            s = _mm_nt(q, kvj)
            m_new = jnp.maximum(m, jnp.max(s, axis=1, keepdims=True))
            p = jnp.exp(s - m_new).astype(BF16)
            return m_new, jnp.exp(m - m_new) * acc + _mm(p, jnp.where(low_k, jnp.ones_like(kvj), kvj))

        def step(j, carry):
            for u in range(unroll):
                carry = block(unroll * j + u, *carry)
            return carry

        unroll = math.gcd(s_len // tk, ATT_UNROLL_FWD)
        init = (jnp.full((rows, 1), -jnp.inf, F32), jnp.zeros((rows, 128), F32))
        m, acc = lax.fori_loop(0, s_len // (tk * unroll), step, init)
        for c, chunk in enumerate(_unstack_heads(acc / pltpu.roll(acc, HEAD_DIM, 1), low, tq, data_low=False)):
            o_ref[:, 128 * c:128 * (c + 1)] = chunk
        lse_ref[...] = (m + jnp.log(acc[:, :1])).reshape(GROUP_A, tq, 1)

    return pl.pallas_call(
        body, name="attn_a_fwd", grid=(N_KV_A, s_len // tq),
        in_specs=[pl.BlockSpec((tq, 256), lambda g, i: (i, g)),
                  pl.BlockSpec((1, s_len, 128), lambda g, i: (g, 0, 0))],
        out_specs=[pl.BlockSpec((tq, 256), lambda g, i: (i, g)),
                   pl.BlockSpec((GROUP_A, tq, 1), lambda g, i: (g, i, 0))],
        out_shape=[jax.ShapeDtypeStruct((s_len, D_A), F32),
                   jax.ShapeDtypeStruct((N_HEADS_A, s_len, 1), F32)],
        compiler_params=_cparams(("parallel", "parallel")),
    )(qa, kv)


def _attn_a_bwd(qa, dya, kv, lse, delta):
    s_len = kv.shape[1]
    tq = min(ATT_TQ_BWD, s_len)
    tk = min(ATT_TK_BWD, s_len)
    rows = GROUP_A * tq

    def body(q_ref, do_ref, kv_ref, lse_ref, dl_ref, dq_ref, dkv_ref):
        @pl.when(pl.program_id(1) == 0)
        def _():
            dkv_ref[...] = jnp.zeros_like(dkv_ref)

        low = _low_lanes(tq)
        q = _stack_heads(q_ref[...].astype(F32), low, data_low=True)
        do = _stack_heads(do_ref[...], low, data_low=False)
        lse_t = lse_ref[...].reshape(rows, 1)
        dl_t = dl_ref[...].reshape(rows, 1)
        q_t = q.T
        do_t = do.T

        def block(j, dq):
            span = pl.ds(pl.multiple_of(j * tk, tk), tk)
            kvj = kv_ref[0, span, :]
            p = jnp.exp(_mm_nt(q, kvj) - lse_t)
            ds = (p * (_mm_nt(do, kvj) - dl_t)).astype(BF16)
            dkv_ref[0, :, span] += _mm(q_t, ds) + _mm(do_t, p.astype(BF16))
            return dq + _mm(ds, kvj)

        def step(j, dq):
            for u in range(unroll):
                dq = block(unroll * j + u, dq)
            return dq

        unroll = math.gcd(s_len // tk, ATT_UNROLL_BWD)
        dq = lax.fori_loop(0, s_len // (tk * unroll), step, jnp.zeros((rows, 128), F32))
        for c, chunk in enumerate(_unstack_heads(dq, low, tq, data_low=True)):
            dq_ref[:, 128 * c:128 * (c + 1)] = chunk

    return pl.pallas_call(
        body, name="attn_a_bwd", grid=(N_KV_A, s_len // tq),
        in_specs=[pl.BlockSpec((tq, 256), lambda g, i: (i, g)),
                  pl.BlockSpec((tq, 256), lambda g, i: (i, g)),
                  pl.BlockSpec((1, s_len, 128), lambda g, i: (g, 0, 0)),
                  pl.BlockSpec((GROUP_A, tq, 1), lambda g, i: (g, i, 0)),
                  pl.BlockSpec((GROUP_A, tq, 1), lambda g, i: (g, i, 0))],
        out_specs=[pl.BlockSpec((tq, 256), lambda g, i: (i, g)),
                   pl.BlockSpec((1, 128, s_len), lambda g, i: (g, 0, 0))],
        out_shape=[jax.ShapeDtypeStruct((s_len, D_A), F32),
                   jax.ShapeDtypeStruct((N_KV_A, 128, s_len), F32)],
        compiler_params=_cparams(("parallel", "arbitrary")),
    )(qa, dya, kv, lse, delta)


class _SwaGeometry:
    def __init__(self, s_len, r):
        self.r = r
        self.tq = SWA_TQ
        self.block = min(max(SWA_MIN_BLOCK, 2 * SWA_TQ * r), s_len)
        self.halo = HALF_WIN * r
        self.nsub = self.block // (self.tq * r)
        self.band = self.tq + 2 * HALF_WIN
        self.length = s_len // r
        self.nblk = s_len // self.block
        self.nhalo = s_len // self.halo
        assert self.nsub * self.tq * r == self.block and self.block % self.halo == 0

    def specs(self):
        per = self.block // self.halo
        cur = pl.BlockSpec((self.block, 128), lambda c, i: (i, c))
        prev = pl.BlockSpec((self.halo, 128), lambda c, i: (jnp.maximum(i * per - 1, 0), c))
        nxt = pl.BlockSpec((self.halo, 128), lambda c, i: (jnp.minimum((i + 1) * per, self.nhalo - 1), c))
        return prev, cur, nxt

    def tiles(self):
        return [(rho + self.r * j * self.tq, self.halo + rho + self.r * (j * self.tq - HALF_WIN), j)
                for j in range(self.nsub) for rho in range(self.r)]

    def own(self, start):
        return pl.ds(start, self.tq, stride=self.r)

    def around(self, start):
        return pl.ds(start, self.band, stride=self.r)

    def fill(self, dst, prev_ref, cur_ref, next_ref):
        dst[:self.halo, :] = prev_ref[...].astype(F32)
        dst[self.halo:self.halo + self.block, :] = cur_ref[...].astype(F32)
        dst[self.halo + self.block:, :] = next_ref[...].astype(F32)

    def first_position(self, j):
        return (pl.program_id(1) * self.block) // self.r + j * self.tq

    def outside(self, j, copies=1):
        pos = self.first_position(j) - HALF_WIN + lax.broadcasted_iota(jnp.int32, (1, copies * self.band), 1) % self.band
        return jnp.where((pos >= 0) & (pos < self.length), 0.0, NEG_BIG)

    def extended(self):
        return pltpu.VMEM((self.block + 2 * self.halo, 128), F32)

    def plain(self):
        return pltpu.VMEM((self.block, 128), F32)


def _low_lanes(rows):
    return lax.broadcasted_iota(jnp.int32, (rows, 128), 1) < HEAD_DIM


def _one_head(x, low, half):
    return jnp.where(low if half == 0 else jnp.logical_not(low), x, 0.0).astype(BF16)


def _two_heads(x, low):
    return jnp.concatenate([_one_head(x, low, 0), _one_head(x, low, 1)], axis=0)


def _carry_ride(base_body, n_in, n_out, n_scratch, ride, grid):
    if ride is None:
        return base_body
    n = ride.n

    def body(*refs):
        o = n_in + n
        ins, ride_ins = refs[:n_in], refs[n_in:o]
        outs, ride_outs = refs[o:o + n_out], refs[o + n_out:o + n_out + n]
        o += n_out + n
        scratch, sems = refs[o:o + n_scratch], refs[o + n_scratch:]
        at_first = (pl.program_id(0) == 0) & (pl.program_id(1) == 0)
        at_last = (pl.program_id(0) == grid[0] - 1) & (pl.program_id(1) == grid[1] - 1)

        @pl.when(at_first)
        def _():
            ride.start(ride_ins, ride_outs, sems)

        base_body(*ins, *outs, *scratch)

        @pl.when(at_last)
        def _():
            ride.finish(ride_ins, ride_outs, sems)

    return body


def _ride_call(base_body, name, grid, in_specs, out_specs, out_shape, scratch_shapes, operands, ride):
    n = 0 if ride is None else ride.n
    extra = [] if ride is None else ride.operands
    outs = pl.pallas_call(
        _carry_ride(base_body, len(in_specs), len(out_specs), len(scratch_shapes), ride, grid), name=name, grid=grid,
        in_specs=list(in_specs) + [ANY] * n, out_specs=list(out_specs) + [ANY] * n,
        out_shape=list(out_shape) + ([] if ride is None else ride.out_shape()),
        scratch_shapes=list(scratch_shapes) + ([] if ride is None else ride.scratch_shapes()),
        compiler_params=_cparams(("arbitrary", "arbitrary")),
    )(*operands, *extra)
    return outs[:len(out_specs)], outs[len(out_specs):]


def _swa_fwd(q, k, v, bias, r, so_far=None, ride=None):
    geo = _SwaGeometry(q.shape[0], r)
    prev, cur, nxt = geo.specs()

    def body(q_ref, kp, kc, kn, vp, vc, vn, b_ref, *rest):
        if so_far is None:
            o_ref, lse_ref, qf, kf, vf = rest
        else:
            o_old_ref, lse_old_ref, o_ref, lse_ref, qf, kf, vf = rest
        qf[...] = q_ref[...].astype(F32)
        geo.fill(kf, kp, kc, kn)
        geo.fill(vf, vp, vc, vn)
        tq = geo.tq
        low_q = _low_lanes(tq)
        bias = b_ref[...].reshape(2 * tq, geo.band)
        for own, around, j in geo.tiles():
            q2 = _two_heads(qf[geo.own(own), :], low_q)
            kb = kf[geo.around(around), :].astype(BF16)
            vb = vf[geo.around(around), :].astype(BF16)
            s = _mm_nt(q2, kb) + bias + geo.outside(j)
            m = jnp.max(s, axis=1, keepdims=True)
            e = jnp.exp(s - m)
            l = jnp.sum(e, axis=1, keepdims=True)
            o2 = _mm(e.astype(BF16), vb) / l
            lse2 = m + jnp.log(l)
            o_new = jnp.where(low_q, o2[:tq], o2[tq:])
            lse_new = jnp.where(low_q, lse2[:tq], lse2[tq:])
            if so_far is not None:
                o_old, lse_old = o_old_ref[geo.own(own), :], lse_old_ref[geo.own(own), :]
                top = jnp.maximum(lse_old, lse_new)
                w_old, w_new = jnp.exp(lse_old - top), jnp.exp(lse_new - top)
                o_new = (w_old * o_old + w_new * o_new) / (w_old + w_new)
                lse_new = top + jnp.log(w_old + w_new)
            o_ref[geo.own(own), :] = o_new
            lse_ref[geo.own(own), :] = lse_new

    sds = jax.ShapeDtypeStruct
    before = () if so_far is None else tuple(so_far)
    (o, lse), carried = _ride_call(
        body, "swa_fwd_%d" % r, (D_B // 128, geo.nblk),
        [cur, prev, cur, nxt, prev, cur, nxt, pl.BlockSpec((2, geo.tq, geo.band), lambda c, i: (c, 0, 0))]
        + [cur] * len(before),
        [cur, cur], [sds(q.shape, F32), sds(q.shape, F32)], [geo.plain(), geo.extended(), geo.extended()],
        (q, k, k, k, v, v, v, bias) + before, ride)
    return o, lse, carried


def _head_stats(st, half):
    lo = HEAD_DIM * half
    return st[:, lo:lo + 1], st[:, lo + HEAD_DIM // 2:lo + HEAD_DIM // 2 + 1]


def _swa_bwd_q(q, k, v, dy, st, bias, r, ride=None):
    geo = _SwaGeometry(q.shape[0], r)
    prev, cur, nxt = geo.specs()
    bias_spec = pl.BlockSpec((2, geo.tq, geo.band), lambda c, i: (c, 0, 0))

    def body(q_ref, kp, kc, kn, vp, vc, vn, dy_ref, st_ref, b_ref, dq_ref, db_ref, qf, kf, vf, dyf, dqf):
        @pl.when(pl.program_id(1) == 0)
        def _():
            db_ref[...] = jnp.zeros_like(db_ref)

        qf[...] = q_ref[...].astype(F32)
        dyf[...] = dy_ref[...].astype(F32)
        geo.fill(kf, kp, kc, kn)
        geo.fill(vf, vp, vc, vn)
        tq = geo.tq
        low_q = _low_lanes(tq)
        bias = b_ref[...].reshape(2 * tq, geo.band)
        for own, around, j in geo.tiles():
            sts = st_ref[geo.own(own), :]
            (lse0, delta0), (lse1, delta1) = _head_stats(sts, 0), _head_stats(sts, 1)
            lse = jnp.concatenate([lse0, lse1], axis=0)
            delta = jnp.concatenate([delta0, delta1], axis=0)
            kb = kf[geo.around(around), :].astype(BF16)
            vb = vf[geo.around(around), :].astype(BF16)
            s = _mm_nt(_two_heads(qf[geo.own(own), :], low_q), kb) + bias + geo.outside(j)
            p = jnp.exp(s - lse)
            ds = p * (_mm_nt(_two_heads(dyf[geo.own(own), :], low_q), vb) - delta)
            db_ref[...] += ds.reshape(2, tq, geo.band)
            dq2 = _mm(ds.astype(BF16), kb)
            dqf[geo.own(own), :] = jnp.where(low_q, dq2[:tq], dq2[tq:])
        dq_ref[...] = dqf[...].astype(BF16)

    (dq, dbias), carried = _ride_call(
        body, "swa_bwd_q_%d" % r, (D_B // 128, geo.nblk),
        [cur, prev, cur, nxt, prev, cur, nxt, cur, cur, bias_spec], [cur, bias_spec],
        [jax.ShapeDtypeStruct(q.shape, BF16), jax.ShapeDtypeStruct(bias.shape, F32)],
        [geo.plain(), geo.extended(), geo.extended(), geo.plain(), geo.plain()],
        (q, k, k, k, v, v, v, dy, st, bias), ride)
    return dq, dbias, carried


def _swa_bwd_kv(q, k, v, dy, st, bias_kv, r):
    geo = _SwaGeometry(q.shape[0], r)
    prev, cur, nxt = geo.specs()

    def body(k_ref, v_ref, qp, qc, qn, dp_, dc_, dn_, sp, sc, sn, b_ref, dk_ref, dv_ref, kf, vf, qf, dyf, stf, dkf,
             dvf):
        kf[...] = k_ref[...].astype(F32)
        vf[...] = v_ref[...].astype(F32)
        geo.fill(qf, qp, qc, qn)
        geo.fill(dyf, dp_, dc_, dn_)
        geo.fill(stf, sp, sc, sn)
        band = geo.band
        low_b = _low_lanes(band)
        bias = jnp.concatenate([b_ref[0], b_ref[1]], axis=1)
        half_lanes = HEAD_DIM // 2
        for own, around, j in geo.tiles():
            ks = kf[geo.own(own), :].astype(BF16)
            vs = vf[geo.own(own), :].astype(BF16)
            q2 = _two_heads(qf[geo.around(around), :], low_b)
            dy2 = _two_heads(dyf[geo.around(around), :], low_b)
            st_t = stf[geo.around(around), :].T
            lse = jnp.concatenate([st_t[:1, :], st_t[HEAD_DIM:HEAD_DIM + 1, :]], axis=1)
            delta = jnp.concatenate([st_t[half_lanes:half_lanes + 1, :],
                                     st_t[HEAD_DIM + half_lanes:HEAD_DIM + half_lanes + 1, :]], axis=1)
            p = jnp.exp(_mm_nt(ks, q2) + bias + (geo.outside(j, copies=2) - lse))
            ds = p * (_mm_nt(vs, dy2) - delta)
            dvf[geo.own(own), :] = _mm(p.astype(BF16), dy2)
            dkf[geo.own(own), :] = _mm(ds.astype(BF16), q2)
        dk_ref[...] = dkf[...].astype(BF16)
        dv_ref[...] = dvf[...].astype(BF16)

    return pl.pallas_call(
        body, name="swa_bwd_kv_%d" % r, grid=(D_B // 128, geo.nblk),
        in_specs=[cur, cur, prev, cur, nxt, prev, cur, nxt, prev, cur, nxt,
                  pl.BlockSpec((2, geo.tq, geo.band), lambda c, i: (c, 0, 0))],
        out_specs=[cur, cur],
        out_shape=[jax.ShapeDtypeStruct(q.shape, BF16), jax.ShapeDtypeStruct(q.shape, BF16)],
        scratch_shapes=[geo.plain(), geo.plain(), geo.extended(), geo.extended(), geo.extended(), geo.plain(),
                        geo.plain()],
        compiler_params=_cparams(("parallel", "parallel")),
    )(k, v, q, q, q, dy, dy, dy, st, st, st, bias_kv)


BIAS_ROWS = 16
BIAS_TN = 4096


def _bias_tiles(onehot, rel_bias_t):
    n = onehot.shape[0]

    def body(oh_ref, rb_ref, o_ref):
        o_ref[...] = sum(_mm_nt(piece, oh_ref[...]) for piece in _split3(rb_ref[...]))

    return pl.pallas_call(
        body, name="bias_tiles", grid=(n // BIAS_TN,),
        in_specs=[_rows(BIAS_TN, 128), _full((BIAS_ROWS, 128))],
        out_specs=pl.BlockSpec((BIAS_ROWS, BIAS_TN), lambda i: (0, i)),
        out_shape=jax.ShapeDtypeStruct((BIAS_ROWS, n), F32),
        compiler_params=_cparams(("parallel",)),
    )(onehot, rel_bias_t)


def _bias_bwd(onehot, dbias_rows, so_far, r):
    n = onehot.shape[0]

    def body(oh, d, prev_ref, g_ref):
        @pl.when(pl.program_id(0) == 0)
        def _():
            g_ref[...] = prev_ref[...]

        hi, lo, _ = _split3(d[...])
        g_ref[...] += _mm(hi, oh[...]) + _mm(lo, oh[...])

    return pl.pallas_call(
        body, name="bias_bwd_%d" % r, grid=(n // BIAS_TN,),
        in_specs=[_rows(BIAS_TN, 128), pl.BlockSpec((BIAS_ROWS, BIAS_TN), lambda i: (0, i)), _full((BIAS_ROWS, 128))],
        out_specs=_full((BIAS_ROWS, 128)),
        out_shape=jax.ShapeDtypeStruct((BIAS_ROWS, 128), F32),
        compiler_params=_cparams(("arbitrary",)),
    )(onehot, dbias_rows, so_far)


LATE = ("w_out", "w_ff1", "w_ff2", "w_ple_gate", "w_ple_proj")


def _local_step(x, p, tgt, w_in, late_shards, g_attn_pre, g_q, g_k, g_out_a, g_out_b, g_attn_post, rel_bias,
                g_mlp_pre, g_mlp_post, g_ple):
    s_len = x.shape[0]
    cc, ss = _rope_tables(s_len)
    gq2 = jnp.concatenate([g_q, g_q], axis=-1)
    gk2 = jnp.concatenate([g_k, g_k], axis=-1)
    ones128 = _group_ones(128)
    rel_bias_t = jnp.zeros((BIAS_ROWS, 128), F32).at[:N_HEADS_B, :N_BUCKETS].set(rel_bias.T)

    xn1, qpre, kpre, qa, kv, qb, kb, vb = _in_proj(x, g_attn_pre, w_in, cc, ss, gq2, gk2, ones128)
    ya, lse_a = _attn_a_fwd(qa, kv)

    tiles, joint = [], None
    for r in DILATIONS:
        tq = SWA_TQ
        onehot = _bucket_onehot(tq, r)
        bias = _bias_tiles(onehot, rel_bias_t)[:N_HEADS_B].reshape(N_HEADS_B, tq, tq + 2 * HALF_WIN)
        bias = jnp.where(_in_window(tq), bias, NEG_BIG)
        yb, lse_b, gathered = _swa_fwd(qb, kb, vb, bias, r, joint,
                                       _GatherRide(late_shards) if r == DILATIONS[-1] else None)
        tiles.append((onehot, bias))
        joint = (yb, lse_b)
    w_out, w_ff1, w_ff2, w_gate, w_ple = (_whole(n, g, mine) for n, g, mine in zip(LATE, gathered, late_shards))

    ycat, y2, h1, xn2 = _out_proj(ya, yb, x, g_out_a, g_out_b, w_out, g_attn_post, g_mlp_pre)
    u = _ff1(xn2, w_ff1)
    f2, h2, xn3 = _ff2(u, w_ff2, h1, g_mlp_post, g_ple)
    dh2, df2, dgl, dpp, loss, dg_ple, dg_mlp_post = _ple_loss(xn3, p, h2, f2, tgt, w_gate, w_ple, g_ple, g_mlp_post)

    grads = {"g_ple": dg_ple, "g_mlp_post": dg_mlp_post}
    grads["w_ple_gate"] = _dw(xn3, dgl, "dw_gate")
    grads["w_ple_proj"] = _dw(p, dpp, "dw_ple")
    grads["w_ff2"] = _dw(u, df2, "dw_ff2", relu2=True)
    du = _ff2_bwd(df2, w_ff2.T, u)
    grads["w_ff1"] = _dw(xn2, du, "dw_ff1")
    dh1, dy2, grads["g_mlp_pre"], grads["g_attn_post"] = _ff1_bwd(du, w_ff1, dh2, h1, y2, g_mlp_pre, g_attn_post)
    grads["w_out"] = _dw(ycat, dy2, "dw_out")
    dya, dyb, delta_a, st_b, grads["g_out_a"], grads["g_out_b"] = _out_proj_bwd(dy2, w_out, ya, yb, lse_b, g_out_a,
                                                                              g_out_b)

    pairs = _pair_sums(LATE, [grads[n] for n in LATE])

    dqr, dkv_t = _attn_a_bwd(qa, dya, kv, lse_a, delta_a)
    dkv_a = dkv_t.transpose(0, 2, 1)

    dqs, dks, dvs = [], [], []
    d_rel = jnp.zeros((BIAS_ROWS, 128), F32)
    for r, (onehot, bias) in zip(DILATIONS, tiles):
        dq_r, dbias, scattered = _swa_bwd_q(qb, kb, vb, dyb, st_b, bias, r,
                                            _ScatterRide(pairs) if r == DILATIONS[0] else None)
        if scattered:
            for n, half in zip(LATE, _chip_sums(LATE, pairs, scattered)):
                grads[n] = half
        bias_kv = jnp.flip(bias, axis=(1, 2))
        dk_r, dv_r = _swa_bwd_kv(qb, kb, vb, dyb, st_b, bias_kv, r)
        dbias_rows = jnp.pad(dbias.reshape(N_HEADS_B, -1), ((0, BIAS_ROWS - N_HEADS_B), (0, 0)))
        d_rel = _bias_bwd(onehot, dbias_rows, d_rel, r)
        dqs.append(dq_r)
        dks.append(dk_r)
        dvs.append(dv_r)
    grads["rel_bias"] = d_rel[:N_HEADS_B, :N_BUCKETS].T

    dproj, grad_x, grads["g_attn_pre"], dgq2, dgk2 = _in_proj_bwd(
        dqr, dkv_a, dqs, dks, dvs, qpre, kpre, x, dh1, g_attn_pre, w_in, cc, ss, gq2, gk2, ones128)
    grads["g_q"] = dgq2[:, :HEAD_DIM] + dgq2[:, HEAD_DIM:]
    grads["g_k"] = dgk2[:, :HEAD_DIM] + dgk2[:, HEAD_DIM:]
    grads["w_in"] = _dw(xn1, dproj, "dw_in")
    return loss, grad_x, grads


ANY = pl.BlockSpec(memory_space=pl.ANY)


def _position():
    return lax.axis_index("x"), lax.axis_index("y"), lax.axis_index("c")


def _other_chips(x, y):
    return [(2 * (1 - x) + y, (1 - x, y)), (2 * x + (1 - y), (x, 1 - y)), (2 * (1 - x) + (1 - y), (1 - x, 1 - y))]


def _cast_shards(shards):
    def body(*refs):
        n = len(refs) // 2
        for i_ref, o_ref in zip(refs[:n], refs[n:]):
            o_ref[...] = i_ref[...].astype(BF16)

    return pl.pallas_call(
        body, name="cast_shards",
        in_specs=[pl.BlockSpec(memory_space=pltpu.VMEM)] * len(shards),
        out_specs=[pl.BlockSpec(memory_space=pltpu.VMEM)] * len(shards),
        out_shape=[jax.ShapeDtypeStruct(s.shape, BF16) for s in shards],
        compiler_params=_cparams(),
    )(*shards)


def _gather_weights(shards):
    n = len(shards)

    ride = _GatherRide(shards)

    def body(*refs):
        ride.start(refs[:n], refs[n:2 * n], refs[2 * n:])
        ride.finish(refs[:n], refs[n:2 * n], refs[2 * n:])

    return pl.pallas_call(
        body, name="gather_weights",
        in_specs=[ANY] * n, out_specs=[ANY] * n,
        out_shape=ride.out_shape(), scratch_shapes=ride.scratch_shapes(),
    )(*shards)


class _GatherRide:
    def __init__(self, shards):
        self.operands = list(shards)
        self.n = len(shards)

    def out_shape(self):
        return [jax.ShapeDtypeStruct((N_CHIPS,) + s.shape, s.dtype) for s in self.operands]

    def scratch_shapes(self):
        return [pltpu.SemaphoreType.DMA((3, self.n))] * 4

    @staticmethod
    def _rows(ref, core):
        half = ref.shape[0] // 2
        return pl.ds(pl.multiple_of(core * half, 16), half)

    def _ici(self, ins, outs, sems, k, a, chip):
        x, y, c = _position()
        return pltpu.make_async_remote_copy(ins[a].at[self._rows(ins[a], c), :],
                                            outs[a].at[2 * x + y, self._rows(ins[a], c), :], sems[0].at[k, a],
                                            sems[1].at[k, a], device_id=(*chip, c), device_id_type=MESH)

    def _pass_on(self, ins, outs, sems, k, a, num, core):
        x, y, c = _position()
        half = outs[a].at[num, self._rows(ins[a], core), :]
        return pltpu.make_async_remote_copy(half, half, sems[2].at[k, a], sems[3].at[k, a], device_id=(x, y, 1 - c),
                                            device_id_type=MESH)

    def start(self, ins, outs, sems):
        x, y, _ = _position()
        for k, (_, chip) in enumerate(_other_chips(x, y)):
            for a in range(self.n):
                self._ici(ins, outs, sems, k, a, chip).start()

    def finish(self, ins, outs, sems):
        x, y, c = _position()
        others = _other_chips(x, y)
        for k, (num, chip) in enumerate(others):
            for a in range(self.n):
                landed = outs[a].at[num, self._rows(ins[a], c), :]
                pltpu.make_async_remote_copy(landed, landed, sems[0].at[k, a], sems[1].at[k, a], device_id=(*chip, c),
                                             device_id_type=MESH).wait_recv()
                self._pass_on(ins, outs, sems, k, a, num, c).start()
        for k, (num, chip) in enumerate(others):
            for a in range(self.n):
                self._pass_on(ins, outs, sems, k, a, num, 1 - c).wait_recv()
        for k, (num, chip) in enumerate(others):
            for a in range(self.n):
                self._ici(ins, outs, sems, k, a, chip).wait_send()
                self._pass_on(ins, outs, sems, k, a, num, c).wait_send()


def _send_sibling_half(grads, tag):
    n = len(grads)

    def body(*refs):
        ins, outs = refs[:n], refs[n:2 * n]
        send_sems, recv_sems = refs[2 * n:]
        x, y, c = _position()
        copies = []
        for a in range(n):
            half = ins[a].shape[1] // 2
            theirs = ins[a].at[:, pl.ds(pl.multiple_of((1 - c) * half, 8), half), :]
            cp = pltpu.make_async_remote_copy(theirs, outs[a], send_sems.at[a], recv_sems.at[a],
                                              device_id=(x, y, 1 - c), device_id_type=MESH)
            cp.start()
            copies.append(cp)
        for cp in copies:
            cp.wait()

    return pl.pallas_call(
        body, name="send_sibling_half_" + tag,
        in_specs=[ANY] * n, out_specs=[ANY] * n,
        out_shape=[jax.ShapeDtypeStruct((g.shape[0], g.shape[1] // 2, g.shape[2]), g.dtype) for g in grads],
        scratch_shapes=[pltpu.SemaphoreType.DMA((n,)), pltpu.SemaphoreType.DMA((n,))],
    )(*grads)


def _scatter_to_chips(pairs):
    n = len(pairs)
    ride = _ScatterRide(pairs)

    def body(*refs):
        ride.start(refs[:n], refs[n:2 * n], refs[2 * n:])
        ride.finish(refs[:n], refs[n:2 * n], refs[2 * n:])

    return pl.pallas_call(
        body, name="scatter_to_chips",
        in_specs=[ANY] * n, out_specs=[ANY] * n,
        out_shape=ride.out_shape(), scratch_shapes=ride.scratch_shapes(),
    )(*pairs)


class _ScatterRide:
    def __init__(self, pairs):
        self.operands = list(pairs)
        self.n = len(pairs)

    def out_shape(self):
        return [jax.ShapeDtypeStruct(g.shape, g.dtype) for g in self.operands]

    def scratch_shapes(self):
        return [pltpu.SemaphoreType.DMA((3, self.n))] * 2

    @staticmethod
    def _copy(ins, outs, sems, k, a, src_slot, dst_slot, chip):
        _, _, c = _position()
        return pltpu.make_async_remote_copy(ins[a].at[src_slot], outs[a].at[dst_slot], sems[0].at[k, a],
                                            sems[1].at[k, a], device_id=(*chip, c), device_id_type=MESH)

    def start(self, ins, outs, sems):
        x, y, _ = _position()
        for k, (num, chip) in enumerate(_other_chips(x, y)):
            for a in range(self.n):
                self._copy(ins, outs, sems, k, a, num, 2 * x + y, chip).start()

    def finish(self, ins, outs, sems):
        x, y, _ = _position()
        for k, (num, chip) in enumerate(_other_chips(x, y)):
            for a in range(self.n):
                self._copy(ins, outs, sems, k, a, 2 * x + y, num, chip).wait_recv()
        for k, (num, chip) in enumerate(_other_chips(x, y)):
            for a in range(self.n):
                self._copy(ins, outs, sems, k, a, num, 2 * x + y, chip).wait_send()


def _exchange_halves(halves):
    n = len(halves)

    def body(*refs):
        ins, outs = refs[:n], refs[n:2 * n]
        send_sems, recv_sems = refs[2 * n:]
        x, y, c = _position()
        copies = []
        for a in range(n):
            cp = pltpu.make_async_remote_copy(ins[a], outs[a], send_sems.at[a], recv_sems.at[a],
                                              device_id=(x, y, 1 - c), device_id_type=MESH)
            cp.start()
            copies.append(cp)
        for cp in copies:
            cp.wait()

    return pl.pallas_call(
        body, name="exchange_halves",
        in_specs=[ANY] * n, out_specs=[ANY] * n,
        out_shape=[jax.ShapeDtypeStruct(h.shape, h.dtype) for h in halves],
        scratch_shapes=[pltpu.SemaphoreType.DMA((n,)), pltpu.SemaphoreType.DMA((n,))],
    )(*halves)


def _allreduce_small(v):
    def body(v_ref, o_ref, buf, send_sems, recv_sems):
        x, y, c = _position()
        me = 4 * x + 2 * y + c
        peers = [(1 - x, y, c), (x, 1 - y, c), (x, y, 1 - c), (1 - x, 1 - y, c), (1 - x, y, 1 - c), (x, 1 - y, 1 - c),
                 (1 - x, 1 - y, 1 - c)]
        num = lambda d: 4 * d[0] + 2 * d[1] + d[2]
        buf[me] = v_ref[...]
        sends = []
        for k, peer in enumerate(peers):
            cp = pltpu.make_async_remote_copy(v_ref, buf.at[me], send_sems.at[k], recv_sems.at[k], device_id=peer,
                                              device_id_type=MESH)
            cp.start()
            sends.append(cp)
        for k, peer in enumerate(peers):
            pltpu.make_async_remote_copy(v_ref, buf.at[num(peer)], send_sems.at[k], recv_sems.at[k], device_id=peer,
                                         device_id_type=MESH).wait_recv()
        for cp in sends:
            cp.wait_send()
        total = buf[0]
        for d in range(1, 8):
            total = total + buf[d]
        o_ref[...] = total

    return pl.pallas_call(
        body, name="allreduce_small",
        in_specs=[pl.BlockSpec(memory_space=pltpu.VMEM)], out_specs=pl.BlockSpec(memory_space=pltpu.VMEM),
        out_shape=jax.ShapeDtypeStruct(v.shape, v.dtype),
        scratch_shapes=[pltpu.VMEM((8,) + v.shape, v.dtype), pltpu.SemaphoreType.DMA((7,)),
                        pltpu.SemaphoreType.DMA((7,))],
    )(v)


def _sum_leading(a, name):
    k, r, c = a.shape
    tr = min(r, 256)

    def body(a_ref, o_ref):
        total = a_ref[0].astype(F32)
        for i in range(1, k):
            total = total + a_ref[i].astype(F32)
        o_ref[...] = total

    return pl.pallas_call(
        body, name=name, grid=(r // tr,),
        in_specs=[pl.BlockSpec((k, tr, c), lambda i: (0, i, 0))],
        out_specs=pl.BlockSpec((tr, c), lambda i: (i, 0)),
        out_shape=jax.ShapeDtypeStruct((r, c), F32),
        compiler_params=_cparams(("parallel",)),
    )(a)


def _add(a, b, name):
    k, r, c = a.shape
    tr = min(r, 256)
    spec = pl.BlockSpec((k, tr, c), lambda i: (0, i, 0))

    def body(a_ref, b_ref, o_ref):
        o_ref[...] = (a_ref[...] + b_ref[...]).astype(BF16)

    return pl.pallas_call(
        body, name=name, grid=(r // tr,), in_specs=[spec, spec], out_specs=spec,
        out_shape=jax.ShapeDtypeStruct(a.shape, BF16), compiler_params=_cparams(("parallel",)),
    )(a, b)


def _adamw(w, g, m, v, name):
    r, c = w.shape
    tr = min(r, 256)
    spec = pl.BlockSpec((tr, c), lambda i: (i, 0))

    def body(w_ref, g_ref, m_ref, v_ref, d_ref, nm_ref, nv_ref):
        gv = g_ref[...]
        nm = ADAM_B1 * m_ref[...] + (1.0 - ADAM_B1) * gv
        nv = ADAM_B2 * v_ref[...] + (1.0 - ADAM_B2) * jnp.square(gv)
        m_hat = nm / (1.0 - ADAM_B1 ** ADAM_STEP)
        v_hat = nv / (1.0 - ADAM_B2 ** ADAM_STEP)
        d_ref[...] = -ADAM_LR * (m_hat / (jnp.sqrt(v_hat) + ADAM_EPS) + ADAM_WD * w_ref[...])
        nm_ref[...] = nm
        nv_ref[...] = nv

    return pl.pallas_call(
        body, name=name, grid=(r // tr,), in_specs=[spec] * 4, out_specs=[spec] * 3,
        out_shape=[jax.ShapeDtypeStruct(w.shape, F32)] * 3, compiler_params=_cparams(("parallel",)),
    )(w, g, m, v)


MATRICES = ("w_in", "w_out", "w_ff1", "w_ff2", "w_ple_gate", "w_ple_proj")
COLUMN_SHARDED = ("w_in", "w_ff1", "w_ple_proj")
SMALL = ("g_attn_pre", "g_q", "g_k", "g_out_a", "g_out_b", "g_attn_post", "rel_bias", "g_mlp_pre", "g_mlp_post",
         "g_ple")
WEIGHT_ORDER = ("w_in", "g_attn_pre", "g_q", "g_k", "g_out_a", "g_out_b", "w_out", "g_attn_post", "rel_bias",
                "g_mlp_pre", "w_ff1", "w_ff2", "g_mlp_post", "g_ple", "w_ple_gate", "w_ple_proj")
PACK_ROWS, PACK_COLS = 8, 1024


def _chip():
    return 2 * lax.axis_index("x") + lax.axis_index("y")


def _whole(name, gathered, mine):
    g = lax.dynamic_update_slice_in_dim(gathered, mine[None], _chip(), axis=0)
    if name in COLUMN_SHARDED:
        return g.transpose(1, 0, 2).reshape(g.shape[1], N_CHIPS * g.shape[2])
    return g.reshape(N_CHIPS * g.shape[1], g.shape[2])


def _pair_sums(names, grads):
    by_chip = []
    for n, g in zip(names, grads):
        if n in COLUMN_SHARDED:
            by_chip.append(g.reshape(g.shape[0], N_CHIPS, g.shape[1] // N_CHIPS).transpose(1, 0, 2))
        else:
            by_chip.append(g.reshape(N_CHIPS, g.shape[0] // N_CHIPS, g.shape[1]))
    c = lax.axis_index("c")
    pairs = []
    for n, g, other in zip(names, by_chip, _send_sibling_half(by_chip, names[0])):
        half = g.shape[1] // 2
        pairs.append(_add(lax.dynamic_slice_in_dim(g, c * half, half, axis=1), other, "pair_sum_" + n))
    return pairs


def _chip_sums(names, pairs, scattered):
    halves = []
    for n, pair, got in zip(names, pairs, scattered):
        own = lax.dynamic_slice_in_dim(pair, _chip(), 1, axis=0)
        halves.append(_sum_leading(lax.dynamic_update_slice_in_dim(got, own, _chip(), axis=0), "chip_sum_" + n))
    return halves


def _pack_small(values, extra=None):
    flat = [values[n].reshape(-1) for n in SMALL]
    used = sum(f.shape[0] for f in flat)
    tail = jnp.zeros((PACK_ROWS * PACK_COLS - used - 1,), F32)
    last = jnp.zeros((1,), F32) if extra is None else extra.reshape(1)
    return jnp.concatenate(flat + [tail, last]).reshape(PACK_ROWS, PACK_COLS)


def _unpack_small(packed, like):
    flat = packed.reshape(-1)
    out, o = {}, 0
    for n in SMALL:
        size = like[n].size
        out[n] = flat[o:o + size].reshape(like[n].shape)
        o += size
    return out, flat[-1]


def kernel(x, p, w_in, g_attn_pre, g_q, g_k, g_out_a, g_out_b, w_out, g_attn_post, rel_bias, g_mlp_pre, w_ff1, w_ff2, g_mlp_post, g_ple, w_ple_gate, w_ple_proj, loss_target, m_w_in, m_g_attn_pre, m_g_q, m_g_k, m_g_out_a, m_g_out_b, m_w_out, m_g_attn_post, m_rel_bias, m_g_mlp_pre, m_w_ff1, m_w_ff2, m_g_mlp_post, m_g_ple, m_w_ple_gate, m_w_ple_proj, v_w_in, v_g_attn_pre, v_g_q, v_g_k, v_g_out_a, v_g_out_b, v_w_out, v_g_attn_post, v_rel_bias, v_g_mlp_pre, v_w_ff1, v_w_ff2, v_g_mlp_post, v_g_ple, v_w_ple_gate, v_w_ple_proj):
    given = dict(locals())
    weights = {n: given[n] for n in WEIGHT_ORDER}
    shards = {n: weights[n][0] for n in MATRICES}

    c = lax.axis_index("c")
    own = dict(zip(MATRICES, _cast_shards([shards[n] for n in MATRICES])))
    w_in_whole = _whole("w_in", _gather_weights([own["w_in"]])[0], own["w_in"])

    loss, grad_x, grads = _local_step(
        x[0], p[0, 0], loss_target[0], w_in_whole, [own[n] for n in LATE], g_attn_pre, g_q, g_k, g_out_a, g_out_b,
        g_attn_post, rel_bias, g_mlp_pre, g_mlp_post, g_ple)

    pairs = _pair_sums(["w_in"], [grads["w_in"]])
    grads["w_in"] = _chip_sums(["w_in"], pairs, _scatter_to_chips(pairs))[0]
    halves = [grads[n] for n in MATRICES]
    grad_w = {}
    for n, mine, theirs in zip(MATRICES, halves, _exchange_halves(halves)):
        half = mine.shape[0]
        g = jnp.zeros((2 * half, mine.shape[1]), F32)
        g = lax.dynamic_update_slice_in_dim(g, mine, c * half, axis=0)
        grad_w[n] = lax.dynamic_update_slice_in_dim(g, theirs, (1 - c) * half, axis=0)

    small_like = {n: weights[n] for n in SMALL}
    reduced = _allreduce_small(_pack_small({n: grads[n] for n in SMALL}, extra=loss))
    grad_small, loss_total = _unpack_small(reduced, small_like)

    delta, new_m, new_v = {}, {}, {}
    for n in MATRICES:
        d, nm, nv = _adamw(shards[n], grad_w[n], given["m_" + n][0], given["v_" + n][0], "adamw_" + n)
        delta[n], new_m[n], new_v[n] = d[None], nm[None], nv[None]
        grad_w[n] = grad_w[n][None]
    d, nm, nv = _adamw(_pack_small(small_like), reduced, _pack_small({n: given["m_" + n] for n in SMALL}),
                       _pack_small({n: given["v_" + n] for n in SMALL}), "adamw_small")
    d_small, nm_small, nv_small = (_unpack_small(a, small_like)[0] for a in (d, nm, nv))
    for n in SMALL:
        grad_w[n], delta[n], new_m[n], new_v[n] = grad_small[n], d_small[n], nm_small[n], nv_small[n]

    return (loss_total, grad_x[None], *[grad_w[n] for n in WEIGHT_ORDER], *[delta[n] for n in WEIGHT_ORDER],
            *[new_m[n] for n in WEIGHT_ORDER], *[new_v[n] for n in WEIGHT_ORDER])
```

```python
import functools
import math

import jax
import jax.numpy as jnp
from jax import lax
from jax.experimental import pallas as pl
from jax.experimental.pallas import tpu as pltpu

F32 = jnp.float32
BF16 = jnp.bfloat16

D_MODEL = 1024
HEAD_DIM = 64
N_HEADS_A = 8
N_KV_A = 2
GROUP_A = N_HEADS_A // N_KV_A
N_HEADS_B = 8
D_A = N_HEADS_A * HEAD_DIM
D_KV_A = N_KV_A * HEAD_DIM
D_B = N_HEADS_B * HEAD_DIM
D_IN = D_A + 2 * D_KV_A + 3 * D_B
D_FF = 4 * D_MODEL
D_PLE = 256
GRID_W = 64
ROPE_THETA = 10000.0
DILATIONS = (1, 4, 16)
HALF_WIN = 64
N_BUCKETS = 32
MAX_DISTANCE = 1024
EPS = 1e-6
NEG_BIG = -1e30
Q_SCALE = HEAD_DIM ** -0.5

ADAM_LR = 0.001
ADAM_B1 = 0.9
ADAM_B2 = 0.999
ADAM_EPS = 1e-08
ADAM_WD = 0.01
ADAM_STEP = 10

N_CHIPS = 4
MESH = pl.DeviceIdType.MESH

ROW_TILE = 512
ATT_TQ = 256
ATT_TQ_BWD = 512
ATT_TK_FWD = 2048
ATT_UNROLL_FWD = 8
ATT_TK_BWD = 512
ATT_UNROLL_BWD = 16
SWA_TQ = 128
SWA_MIN_BLOCK = 4096
DW_TS = 2048
VMEM_LIMIT = 56 * 1024 * 1024

NT = (((1,), (1,)), ((), ()))
TN = (((0,), (0,)), ((), ()))


def _cparams(sem=None, vmem=VMEM_LIMIT):
    return pltpu.CompilerParams(dimension_semantics=sem, vmem_limit_bytes=vmem)


def _full(shape):
    n = len(shape)
    return pl.BlockSpec(shape, lambda *_: (0,) * n)


def _rows(tm, width):
    return pl.BlockSpec((tm, width), lambda i: (i, 0))


def _split3(a):
    a1 = a.astype(BF16)
    r = a - a1.astype(F32)
    a2 = r.astype(BF16)
    a3 = (r - a2.astype(F32)).astype(BF16)
    return a1, a2, a3


def _xdot(a, sel):
    a1, a2, a3 = _split3(a)
    d = lambda p: jnp.dot(p, sel, preferred_element_type=F32)
    return d(a1) + d(a2) + d(a3)


def _mm(a, b):
    return jnp.dot(a, b, preferred_element_type=F32)


def _mm_nt(a, b):
    return lax.dot_general(a, b, NT, preferred_element_type=F32)


def _mm_tn(a, b):
    return lax.dot_general(a, b, TN, preferred_element_type=F32)


def _rms_stats(x):
    r = lax.rsqrt(jnp.mean(x * x, axis=-1, keepdims=True) + EPS)
    return x * r, r


def _rms_bwd(dy, xh, r, g):
    gdy = dy * g
    dx = r * (gdy - xh * jnp.mean(gdy * xh, axis=-1, keepdims=True))
    dg = jnp.sum(dy * xh, axis=0, keepdims=True)
    return dx, dg


def _acc_out(ref, val):
    @pl.when(pl.program_id(0) == 0)
    def _():
        ref[...] = jnp.zeros_like(ref)

    ref[...] += val


def _swap_halves(x, first_half):
    return jnp.where(first_half, pltpu.roll(x, 96, 1), pltpu.roll(x, 32, 1))


def _first_half_mask(shape):
    return (lax.broadcasted_iota(jnp.int32, shape, 1) % HEAD_DIM) < (HEAD_DIM // 2)


def _rope_tables(s_len):
    t = jnp.arange(s_len)
    row = (t // GRID_W).astype(F32)
    col = (t % GRID_W).astype(F32)
    n_axis = HEAD_DIM // 4
    inv_freq = ROPE_THETA ** (-jnp.arange(n_axis, dtype=F32) / n_axis)
    ang = jnp.concatenate([row[:, None] * inv_freq, col[:, None] * inv_freq], axis=-1)
    c, s = jnp.cos(ang), jnp.sin(ang)
    cc = jnp.concatenate([c, c, c, c], axis=-1)
    ss = jnp.concatenate([-s, s, -s, s], axis=-1)
    return cc, ss


def _group_ones(width):
    i = jnp.arange(width)
    return (i[:, None] // HEAD_DIM == i[None, :] // HEAD_DIM).astype(BF16)


def _t5_bucket(rel):
    nb = N_BUCKETS // 2
    max_exact = nb // 2
    side = jnp.where(rel > 0, nb, 0)
    n = jnp.abs(rel)
    large = max_exact + (jnp.log(jnp.maximum(n, max_exact).astype(F32) / max_exact)
                         / math.log(MAX_DISTANCE / max_exact) * (nb - max_exact)).astype(jnp.int32)
    large = jnp.minimum(large, nb - 1)
    return side + jnp.where(n < max_exact, n, large)


def _in_window(tq):
    qi = jnp.arange(tq)
    kj = jnp.arange(tq + 2 * HALF_WIN)
    return jnp.abs(kj[None, :] - HALF_WIN - qi[:, None]) <= HALF_WIN


def _bucket_onehot(tq, dilation):
    qi = jnp.arange(tq)
    kj = jnp.arange(tq + 2 * HALF_WIN)
    rel = kj[None, :] - HALF_WIN - qi[:, None]
    bucket = _t5_bucket(rel * dilation).reshape(-1)
    return (bucket[:, None] == jnp.arange(128)[None, :]).astype(BF16)


def _in_proj(x, g1, w_in, cc, ss, gq2, gk2, ones128):
    s_len = x.shape[0]
    tm = min(ROW_TILE, s_len)

    def body(x_ref, g_ref, w_ref, cc_ref, ss_ref, gq_ref, gk_ref, one_ref,
             xn_ref, qpre_ref, kpre_ref, qa_ref, kv_ref, qb_ref, kb_ref, vb_ref):
        xh, _ = _rms_stats(x_ref[...])
        xn = (xh * g_ref[...]).astype(BF16)
        xn_ref[...] = xn
        proj = _mm(xn, w_ref[...])
        first_half = _first_half_mask((tm, 128))
        ones = one_ref[...]
        cc_t, ss_t = cc_ref[...], ss_ref[...]

        def norm_rope(xc, gain):
            ms = _xdot(xc * xc, ones) * (1.0 / HEAD_DIM)
            y = xc * lax.rsqrt(ms + EPS) * gain
            return y * cc_t + _swap_halves(y, first_half) * ss_t

        qpre_ref[...] = proj[:, :D_A]
        kpre_ref[...] = proj[:, D_A:D_A + D_KV_A]
        for c in range(D_A // 128):
            y = norm_rope(proj[:, 128 * c:128 * (c + 1)], gq_ref[...])
            qa_ref[:, 128 * c:128 * (c + 1)] = (y * Q_SCALE).astype(BF16)
        ka = norm_rope(proj[:, D_A:D_A + D_KV_A], gk_ref[...])
        o = D_A + D_KV_A
        va = proj[:, o:o + D_KV_A]
        low = _low_lanes(tm)
        kv_ref[0] = jnp.where(low, ka, pltpu.roll(va, HEAD_DIM, 1)).astype(BF16)
        kv_ref[1] = jnp.where(low, pltpu.roll(ka, HEAD_DIM, 1), va).astype(BF16)
        o += D_KV_A
        qb_ref[...] = (proj[:, o:o + D_B] * Q_SCALE).astype(BF16)
        kb_ref[...] = proj[:, o + D_B:o + 2 * D_B].astype(BF16)
        vb_ref[...] = proj[:, o + 2 * D_B:o + 3 * D_B].astype(BF16)

    sds = jax.ShapeDtypeStruct
    return pl.pallas_call(
        body, name="in_proj", grid=(s_len // tm,),
        in_specs=[_rows(tm, D_MODEL), _full((1, D_MODEL)), _full((D_MODEL, D_IN)), _rows(tm, 128), _rows(tm, 128),
                  _full((1, 128)), _full((1, 128)), _full((128, 128))],
        out_specs=[_rows(tm, D_MODEL), _rows(tm, D_A), _rows(tm, D_KV_A), _rows(tm, D_A),
                   pl.BlockSpec((N_KV_A, tm, 128), lambda i: (0, i, 0)), _rows(tm, D_B), _rows(tm, D_B),
                   _rows(tm, D_B)],
        out_shape=[sds((s_len, D_MODEL), BF16), sds((s_len, D_A), F32), sds((s_len, D_KV_A), F32),
                   sds((s_len, D_A), BF16), sds((N_KV_A, s_len, 128), BF16),
                   sds((s_len, D_B), BF16), sds((s_len, D_B), BF16), sds((s_len, D_B), BF16)],
        compiler_params=_cparams(("parallel",)),
    )(x, g1, w_in, cc, ss, gq2, gk2, ones128)


def _stat_spec(tm):
    return pl.BlockSpec((N_HEADS_B, tm, 1), lambda i: (0, i, 0))


def _out_proj(ya, yb, x, g_a, g_b, w_out, g_post, g_mlp_pre):
    s_len = x.shape[0]
    tm = min(ROW_TILE, s_len)

    def body(ya_ref, yb_ref, x_ref, ga_ref, gb_ref, w_ref, gp_ref, gm_ref, ycat_ref, y2_ref, h1_ref, xn2_ref):
        ah, _ = _rms_stats(ya_ref[...])
        bh, _ = _rms_stats(yb_ref[...])
        ycat = jnp.concatenate([ah * ga_ref[...], bh * gb_ref[...]], axis=-1).astype(BF16)
        ycat_ref[...] = ycat
        y2 = _mm(ycat, w_ref[...])
        y2_ref[...] = y2
        y2h, _ = _rms_stats(y2)
        h1 = x_ref[...] + y2h * gp_ref[...]
        h1_ref[...] = h1
        h1h, _ = _rms_stats(h1)
        xn2_ref[...] = (h1h * gm_ref[...]).astype(BF16)

    sds = jax.ShapeDtypeStruct
    return pl.pallas_call(
        body, name="out_proj", grid=(s_len // tm,),
        in_specs=[_rows(tm, D_A), _rows(tm, D_B), _rows(tm, D_MODEL), _full((1, D_A)), _full((1, D_B)),
                  _full((D_MODEL, D_MODEL)), _full((1, D_MODEL)), _full((1, D_MODEL))],
        out_specs=[_rows(tm, D_MODEL)] * 4,
        out_shape=[sds((s_len, D_MODEL), BF16), sds((s_len, D_MODEL), F32), sds((s_len, D_MODEL), F32),
                   sds((s_len, D_MODEL), BF16)],
        compiler_params=_cparams(("parallel",)),
    )(ya, yb, x, g_a, g_b, w_out, g_post, g_mlp_pre)


def _ff1(xn2, w_ff1):
    s_len = xn2.shape[0]
    tm = min(ROW_TILE, s_len)

    def body(x_ref, w_ref, u_ref):
        u_ref[...] = _mm(x_ref[...], w_ref[...])

    return pl.pallas_call(
        body, name="ff1", grid=(s_len // tm,),
        in_specs=[_rows(tm, D_MODEL), _full((D_MODEL, D_FF))],
        out_specs=_rows(tm, D_FF),
        out_shape=jax.ShapeDtypeStruct((s_len, D_FF), F32),
        compiler_params=_cparams(("parallel",)),
    )(xn2, w_ff1)


def _ff2(u, w_ff2, h1, g_post, g_ple):
    s_len = u.shape[0]
    tm = min(ROW_TILE, s_len)

    def body(u_ref, w_ref, h1_ref, gp_ref, gl_ref, f2_ref, h2_ref, xn3_ref):
        f = jnp.square(jnp.maximum(u_ref[...], 0.0)).astype(BF16)
        f2 = _mm(f, w_ref[...])
        f2_ref[...] = f2
        f2h, _ = _rms_stats(f2)
        h2 = h1_ref[...] + f2h * gp_ref[...]
        h2_ref[...] = h2
        h2h, _ = _rms_stats(h2)
        xn3_ref[...] = (h2h * gl_ref[...]).astype(BF16)

    sds = jax.ShapeDtypeStruct
    return pl.pallas_call(
        body, name="ff2", grid=(s_len // tm,),
        in_specs=[_rows(tm, D_FF), _full((D_FF, D_MODEL)), _rows(tm, D_MODEL), _full((1, D_MODEL)),
                  _full((1, D_MODEL))],
        out_specs=[_rows(tm, D_MODEL)] * 3,
        out_shape=[sds((s_len, D_MODEL), F32), sds((s_len, D_MODEL), F32), sds((s_len, D_MODEL), BF16)],
        compiler_params=_cparams(("parallel",)),
    )(u, w_ff2, h1, g_post, g_ple)


def _ple_loss(xn3, p, h2, f2, tgt, w_gate, w_ple, g_ple, g_mlp_post):
    s_len = h2.shape[0]
    tm = min(ROW_TILE, s_len)

    def body(xn3_ref, p_ref, h2_ref, f2_ref, t_ref, wg_ref, wp_ref, gl_ref, gp_ref,
             dh2_ref, df2_ref, dgl_ref, dpp_ref, loss_ref, dgple_ref, dgpost_ref):
        gate = jax.nn.sigmoid(_mm(xn3_ref[...], wg_ref[...]))
        pp = _mm(p_ref[...].astype(BF16), wp_ref[...])
        h2 = h2_ref[...]
        err = h2 + gate * pp - t_ref[...]
        sq = jnp.sum(jnp.sum(err * err, axis=1, keepdims=True), axis=0, keepdims=True)
        _acc_out(loss_ref, sq * (0.5 / D_MODEL))
        dh3 = err * (1.0 / D_MODEL)
        dgl = (dh3 * pp) * gate * (1.0 - gate)
        dgl_b = dgl.astype(BF16)
        dgl_ref[...] = dgl_b
        dpp_ref[...] = (dh3 * gate).astype(BF16)
        dxn3 = _mm_nt(dgl_b, wg_ref[...])
        h2h, r2 = _rms_stats(h2)
        dx, dg = _rms_bwd(dxn3, h2h, r2, gl_ref[...])
        _acc_out(dgple_ref, dg)
        dh2 = dh3 + dx
        dh2_ref[...] = dh2
        f2h, rf = _rms_stats(f2_ref[...])
        df2, dg = _rms_bwd(dh2, f2h, rf, gp_ref[...])
        _acc_out(dgpost_ref, dg)
        df2_ref[...] = df2.astype(BF16)

    sds = jax.ShapeDtypeStruct
    return pl.pallas_call(
        body, name="ple_loss", grid=(s_len // tm,),
        in_specs=[_rows(tm, D_MODEL), _rows(tm, D_PLE), _rows(tm, D_MODEL), _rows(tm, D_MODEL), _rows(tm, D_MODEL),
                  _full((D_MODEL, D_MODEL)), _full((D_PLE, D_MODEL)), _full((1, D_MODEL)), _full((1, D_MODEL))],
        out_specs=[_rows(tm, D_MODEL)] * 3 + [_rows(tm, D_MODEL), _full((1, 1)), _full((1, D_MODEL)),
                                              _full((1, D_MODEL))],
        out_shape=[sds((s_len, D_MODEL), F32), sds((s_len, D_MODEL), BF16), sds((s_len, D_MODEL), BF16),
                   sds((s_len, D_MODEL), BF16), sds((1, 1), F32), sds((1, D_MODEL), F32), sds((1, D_MODEL), F32)],
        compiler_params=_cparams(("arbitrary",)),
    )(xn3, p, h2, f2, tgt, w_gate, w_ple, g_ple, g_mlp_post)


def _ff2_bwd(df2, w_ff2_t, u):
    s_len = u.shape[0]
    tm = min(ROW_TILE, s_len)

    def body(d_ref, w_ref, u_ref, du_ref):
        df = _mm(d_ref[...], w_ref[...])
        du_ref[...] = (df * (2.0 * jnp.maximum(u_ref[...], 0.0))).astype(BF16)

    return pl.pallas_call(
        body, name="ff2_bwd", grid=(s_len // tm,),
        in_specs=[_rows(tm, D_MODEL), _full((D_MODEL, D_FF)), _rows(tm, D_FF)],
        out_specs=_rows(tm, D_FF),
        out_shape=jax.ShapeDtypeStruct((s_len, D_FF), BF16),
        compiler_params=_cparams(("parallel",)),
    )(df2, w_ff2_t, u)


def _ff1_bwd(du, w_ff1, dh2, h1, y2, g_mlp_pre, g_post):
    s_len = du.shape[0]
    tm = min(ROW_TILE, s_len)

    def body(du_ref, w_ref, dh2_ref, h1_ref, y2_ref, gm_ref, gp_ref, dh1_ref, dy2_ref, dgm_ref, dgp_ref):
        dxn2 = _mm_nt(du_ref[...], w_ref[...])
        h1h, r1 = _rms_stats(h1_ref[...])
        dx, dg = _rms_bwd(dxn2, h1h, r1, gm_ref[...])
        _acc_out(dgm_ref, dg)
        dh1 = dh2_ref[...] + dx
        dh1_ref[...] = dh1
        y2h, ry = _rms_stats(y2_ref[...])
        dy2, dg = _rms_bwd(dh1, y2h, ry, gp_ref[...])
        _acc_out(dgp_ref, dg)
        dy2_ref[...] = dy2.astype(BF16)

    sds = jax.ShapeDtypeStruct
    return pl.pallas_call(
        body, name="ff1_bwd", grid=(s_len // tm,),
        in_specs=[_rows(tm, D_FF), _full((D_MODEL, D_FF)), _rows(tm, D_MODEL), _rows(tm, D_MODEL),
                  _rows(tm, D_MODEL), _full((1, D_MODEL)), _full((1, D_MODEL))],
        out_specs=[_rows(tm, D_MODEL), _rows(tm, D_MODEL), _full((1, D_MODEL)), _full((1, D_MODEL))],
        out_shape=[sds((s_len, D_MODEL), F32), sds((s_len, D_MODEL), BF16), sds((1, D_MODEL), F32),
                   sds((1, D_MODEL), F32)],
        compiler_params=_cparams(("arbitrary",)),
    )(du, w_ff1, dh2, h1, y2, g_mlp_pre, g_post)


def _out_proj_bwd(dy2, w_out, ya, yb, lse_b, g_a, g_b):
    s_len = ya.shape[0]
    tm = min(ROW_TILE, s_len)

    def body(d_ref, w_ref, ya_ref, yb_ref, lse_ref, ga_ref, gb_ref, dya_ref, dyb_ref, da_ref, st_ref, dga_ref,
             dgb_ref):
        dycat = _mm_nt(d_ref[...], w_ref[...])
        lane = lax.broadcasted_iota(jnp.int32, (tm, 128), 1)
        low = lane < HEAD_DIM
        is_lse = (lane % HEAD_DIM) < (HEAD_DIM // 2)

        def head_sums(prod_chunk):
            return (jnp.sum(jnp.where(low, prod_chunk, 0.0), axis=1, keepdims=True),
                    jnp.sum(jnp.where(low, 0.0, prod_chunk), axis=1, keepdims=True))

        ya = ya_ref[...]
        yh, r = _rms_stats(ya)
        dya, dg = _rms_bwd(dycat[:, :D_A], yh, r, ga_ref[...])
        _acc_out(dga_ref, dg)
        dya_ref[...] = dya
        prod = dya * ya
        for c in range(D_A // 128):
            da_ref[2 * c], da_ref[2 * c + 1] = head_sums(prod[:, 128 * c:128 * (c + 1)])

        yb = yb_ref[...]
        yh, r = _rms_stats(yb)
        dyb, dg = _rms_bwd(dycat[:, D_A:], yh, r, gb_ref[...])
        _acc_out(dgb_ref, dg)
        dyb_ref[...] = dyb.astype(BF16)
        prod = dyb * yb
        for c in range(D_B // 128):
            sl = slice(128 * c, 128 * (c + 1))
            d_lo, d_hi = head_sums(prod[:, sl])
            st_ref[:, sl] = jnp.where(is_lse, lse_ref[:, sl], jnp.where(low, d_lo, d_hi))

    sds = jax.ShapeDtypeStruct
    return pl.pallas_call(
        body, name="out_proj_bwd", grid=(s_len // tm,),
        in_specs=[_rows(tm, D_MODEL), _full((D_MODEL, D_MODEL)), _rows(tm, D_A), _rows(tm, D_B), _rows(tm, D_B),
                  _full((1, D_A)), _full((1, D_B))],
        out_specs=[_rows(tm, D_A), _rows(tm, D_B), _stat_spec(tm), _rows(tm, D_B), _full((1, D_A)),
                   _full((1, D_B))],
        out_shape=[sds((s_len, D_A), F32), sds((s_len, D_B), BF16), sds((N_HEADS_A, s_len, 1), F32),
                   sds((s_len, D_B), F32), sds((1, D_A), F32), sds((1, D_B), F32)],
        compiler_params=_cparams(("arbitrary",)),
    )(dy2, w_out, ya, yb, lse_b, g_a, g_b)


def _in_proj_bwd(dqr, dkv, dqb, dkb, dvb, qpre, kpre, x, dh1, g1, w_in, cc, ss, gq2, gk2, ones128):
    s_len = x.shape[0]
    tm = min(ROW_TILE, s_len)

    def body(dqr_ref, dkv_ref, dq0, dq1, dq2, dk0, dk1, dk2, dv0, dv1, dv2, qpre_ref, kpre_ref, x_ref,
             dh1_ref, g_ref, w_ref, cc_ref, ss_ref, gq_ref, gk_ref, one_ref, dproj_ref, gx_ref, dg1_ref, dgq_ref,
             dgk_ref):
        low = _low_lanes(tm)
        dkr = jnp.where(low, dkv_ref[0], pltpu.roll(dkv_ref[1], HEAD_DIM, 1))
        dva = jnp.where(low, pltpu.roll(dkv_ref[0], HEAD_DIM, 1), dkv_ref[1])
        first_half = _first_half_mask((tm, 128))
        ones = one_ref[...]
        cc_t, ss_t = cc_ref[...], ss_ref[...]

        def norm_rope_bwd(dy, xc, gain):
            dn = dy * cc_t - _swap_halves(dy, first_half) * ss_t
            r = lax.rsqrt(_xdot(xc * xc, ones) * (1.0 / HEAD_DIM) + EPS)
            xh = xc * r
            gdy = dn * gain
            dx = r * (gdy - xh * (_xdot(gdy * xh, ones) * (1.0 / HEAD_DIM)))
            return dx, jnp.sum(dn * xh, axis=0, keepdims=True)

        dgq = jnp.zeros((1, 128), F32)
        parts = []
        for c in range(D_A // 128):
            sl = slice(128 * c, 128 * (c + 1))
            dx, dg = norm_rope_bwd(dqr_ref[:, sl] * Q_SCALE, qpre_ref[:, sl], gq_ref[...])
            parts.append(dx)
            dgq = dgq + dg
        dxk, dgk = norm_rope_bwd(dkr, kpre_ref[...], gk_ref[...])
        _acc_out(dgq_ref, dgq)
        _acc_out(dgk_ref, dgk)
        total = lambda a, b, c: a[...].astype(F32) + b[...].astype(F32) + c[...].astype(F32)
        parts += [dxk, dva, total(dq0, dq1, dq2) * Q_SCALE, total(dk0, dk1, dk2), total(dv0, dv1, dv2)]
        dproj = jnp.concatenate(parts, axis=-1).astype(BF16)
        dproj_ref[...] = dproj
        dxn = _mm_nt(dproj, w_ref[...])
        xh, r = _rms_stats(x_ref[...])
        dx, dg = _rms_bwd(dxn, xh, r, g_ref[...])
        _acc_out(dg1_ref, dg)
        gx_ref[...] = dh1_ref[...] + dx

    sds = jax.ShapeDtypeStruct
    return pl.pallas_call(
        body, name="in_proj_bwd", grid=(s_len // tm,),
        in_specs=[_rows(tm, D_A), pl.BlockSpec((N_KV_A, tm, 128), lambda i: (0, i, 0))] + [_rows(tm, D_B)] * 9
                 + [_rows(tm, D_A), _rows(tm, D_KV_A), _rows(tm, D_MODEL), _rows(tm, D_MODEL),
                    _full((1, D_MODEL)), _full((D_MODEL, D_IN)), _rows(tm, 128), _rows(tm, 128), _full((1, 128)),
                    _full((1, 128)), _full((128, 128))],
        out_specs=[_rows(tm, D_IN), _rows(tm, D_MODEL), _full((1, D_MODEL)), _full((1, 128)), _full((1, 128))],
        out_shape=[sds((s_len, D_IN), BF16), sds((s_len, D_MODEL), F32), sds((1, D_MODEL), F32),
                   sds((1, 128), F32), sds((1, 128), F32)],
        compiler_params=_cparams(("arbitrary",)),
    )(dqr, dkv, *dqb, *dkb, *dvb, qpre, kpre, x, dh1, g1, w_in, cc, ss, gq2, gk2, ones128)


def _dw(a, b, name, relu2=False):
    s_len, ka = a.shape
    n = b.shape[1]
    ts = min(DW_TS, s_len)
    bk = min(ka, 1024)
    bn = n if n % 1024 else 1024

    def body(a_ref, b_ref, o_ref):
        @pl.when(pl.program_id(2) == 0)
        def _():
            o_ref[...] = jnp.zeros_like(o_ref)

        av = a_ref[...]
        if relu2:
            av = jnp.square(jnp.maximum(av, 0.0))
        o_ref[...] += _mm_tn(av.astype(BF16), b_ref[...])

    return pl.pallas_call(
        body, name=name, grid=(ka // bk, n // bn, s_len // ts),
        in_specs=[pl.BlockSpec((ts, bk), lambda i, j, k: (k, i)), pl.BlockSpec((ts, bn), lambda i, j, k: (k, j))],
        out_specs=pl.BlockSpec((bk, bn), lambda i, j, k: (i, j)),
        out_shape=jax.ShapeDtypeStruct((ka, n), F32),
        compiler_params=_cparams(("parallel", "parallel", "arbitrary")),
    )(a, b)


def _stack_heads(block, low, data_low):
    parts = []
    for c in range(GROUP_A // 2):
        chunk = block[:, 128 * c:128 * (c + 1)]
        swapped = pltpu.roll(chunk, HEAD_DIM, 1)
        for h_low in (chunk, swapped) if data_low else (swapped, chunk):
            parts.append(jnp.where(low, h_low, 0.0) if data_low else jnp.where(low, 0.0, h_low))
    return jnp.concatenate(parts, axis=0).astype(BF16)


def _unstack_heads(stacked, low, tq, data_low):
    chunks = []
    for c in range(GROUP_A // 2):
        even = stacked[2 * c * tq:(2 * c + 1) * tq]
        odd = stacked[(2 * c + 1) * tq:(2 * c + 2) * tq]
        if data_low:
            chunks.append(jnp.where(low, even, pltpu.roll(odd, HEAD_DIM, 1)))
        else:
            chunks.append(jnp.where(low, pltpu.roll(even, HEAD_DIM, 1), odd))
    return chunks


def _attn_a_fwd(qa, kv):
    s_len = kv.shape[1]
    tq = min(ATT_TQ, s_len)
    tk = min(ATT_TK_FWD, s_len)
    rows = GROUP_A * tq

    def body(q_ref, kv_ref, o_ref, lse_ref):
        low = _low_lanes(tq)
        low_k = _low_lanes(tk)
        q = _stack_heads(q_ref[...].astype(F32), low, data_low=True)

        def block(j, m, acc):
            kvj = kv_ref[0, pl.ds(pl.multiple_of(j * tk, tk), tk), :]
            s = _mm_nt(q, kvj)
            m_new = jnp.maximum(m, jnp.max(s, axis=1, keepdims=True))
            p = jnp.exp(s - m_new).astype(BF16)
            return m_new, jnp.exp(m - m_new) * acc + _mm(p, jnp.where(low_k, jnp.ones_like(kvj), kvj))

        def step(j, carry):
            for u in range(unroll):
                carry = block(unroll * j + u, *carry)
            return carry

        unroll = math.gcd(s_len // tk, ATT_UNROLL_FWD)
        init = (jnp.full((rows, 1), -jnp.inf, F32), jnp.zeros((rows, 128), F32))
        m, acc = lax.fori_loop(0, s_len // (tk * unroll), step, init)
        for c, chunk in enumerate(_unstack_heads(acc / pltpu.roll(acc, HEAD_DIM, 1), low, tq, data_low=False)):
            o_ref[:, 128 * c:128 * (c + 1)] = chunk
        lse_ref[...] = (m + jnp.log(acc[:, :1])).reshape(GROUP_A, tq, 1)

    return pl.pallas_call(
        body, name="attn_a_fwd", grid=(N_KV_A, s_len // tq),
        in_specs=[pl.BlockSpec((tq, 256), lambda g, i: (i, g)),
                  pl.BlockSpec((1, s_len, 128), lambda g, i: (g, 0, 0))],
        out_specs=[pl.BlockSpec((tq, 256), lambda g, i: (i, g)),
                   pl.BlockSpec((GROUP_A, tq, 1), lambda g, i: (g, i, 0))],
        out_shape=[jax.ShapeDtypeStruct((s_len, D_A), F32),
                   jax.ShapeDtypeStruct((N_HEADS_A, s_len, 1), F32)],
        compiler_params=_cparams(("parallel", "parallel")),
    )(qa, kv)


def _attn_a_bwd(qa, dya, kv, lse, delta):
    s_len = kv.shape[1]
    tq = min(ATT_TQ_BWD, s_len)
    tk = min(ATT_TK_BWD, s_len)
    rows = GROUP_A * tq

    def body(q_ref, do_ref, kv_ref, lse_ref, dl_ref, dq_ref, dkv_ref):
        @pl.when(pl.program_id(1) == 0)
        def _():
            dkv_ref[...] = jnp.zeros_like(dkv_ref)

        low = _low_lanes(tq)
        q = _stack_heads(q_ref[...].astype(F32), low, data_low=True)
        do = _stack_heads(do_ref[...], low, data_low=False)
        lse_t = lse_ref[...].reshape(rows, 1)
        dl_t = dl_ref[...].reshape(rows, 1)
        q_t = q.T
        do_t = do.T

        def block(j, dq):
            span = pl.ds(pl.multiple_of(j * tk, tk), tk)
            kvj = kv_ref[0, span, :]
            p = jnp.exp(_mm_nt(q, kvj) - lse_t)
            ds = (p * (_mm_nt(do, kvj) - dl_t)).astype(BF16)
            dkv_ref[0, :, span] += _mm(q_t, ds) + _mm(do_t, p.astype(BF16))
            return dq + _mm(ds, kvj)

        def step(j, dq):
            for u in range(unroll):
                dq = block(unroll * j + u, dq)
            return dq

        unroll = math.gcd(s_len // tk, ATT_UNROLL_BWD)
        dq = lax.fori_loop(0, s_len // (tk * unroll), step, jnp.zeros((rows, 128), F32))
        for c, chunk in enumerate(_unstack_heads(dq, low, tq, data_low=True)):
            dq_ref[:, 128 * c:128 * (c + 1)] = chunk

    return pl.pallas_call(
        body, name="attn_a_bwd", grid=(N_KV_A, s_len // tq),
        in_specs=[pl.BlockSpec((tq, 256), lambda g, i: (i, g)),
                  pl.BlockSpec((tq, 256), lambda g, i: (i, g)),
                  pl.BlockSpec((1, s_len, 128), lambda g, i: (g, 0, 0)),
                  pl.BlockSpec((GROUP_A, tq, 1), lambda g, i: (g, i, 0)),
                  pl.BlockSpec((GROUP_A, tq, 1), lambda g, i: (g, i, 0))],
        out_specs=[pl.BlockSpec((tq, 256), lambda g, i: (i, g)),
                   pl.BlockSpec((1, 128, s_len), lambda g, i: (g, 0, 0))],
        out_shape=[jax.ShapeDtypeStruct((s_len, D_A), F32),
                   jax.ShapeDtypeStruct((N_KV_A, 128, s_len), F32)],
        compiler_params=_cparams(("parallel", "arbitrary")),
    )(qa, dya, kv, lse, delta)


class _SwaGeometry:
    def __init__(self, s_len, r):
        self.r = r
        self.tq = SWA_TQ
        self.block = min(max(SWA_MIN_BLOCK, 2 * SWA_TQ * r), s_len)
        self.halo = HALF_WIN * r
        self.nsub = self.block // (self.tq * r)
        self.band = self.tq + 2 * HALF_WIN
        self.length = s_len // r
        self.nblk = s_len // self.block
        self.nhalo = s_len // self.halo
        assert self.nsub * self.tq * r == self.block and self.block % self.halo == 0

    def specs(self):
        per = self.block // self.halo
        cur = pl.BlockSpec((self.block, 128), lambda c, i: (i, c))
        prev = pl.BlockSpec((self.halo, 128), lambda c, i: (jnp.maximum(i * per - 1, 0), c))
        nxt = pl.BlockSpec((self.halo, 128), lambda c, i: (jnp.minimum((i + 1) * per, self.nhalo - 1), c))
        return prev, cur, nxt

    def tiles(self):
        return [(rho + self.r * j * self.tq, self.halo + rho + self.r * (j * self.tq - HALF_WIN), j)
                for j in range(self.nsub) for rho in range(self.r)]

    def own(self, start):
        return pl.ds(start, self.tq, stride=self.r)

    def around(self, start):
        return pl.ds(start, self.band, stride=self.r)

    def fill(self, dst, prev_ref, cur_ref, next_ref):
        dst[:self.halo, :] = prev_ref[...].astype(F32)
        dst[self.halo:self.halo + self.block, :] = cur_ref[...].astype(F32)
        dst[self.halo + self.block:, :] = next_ref[...].astype(F32)

    def first_position(self, j):
        return (pl.program_id(1) * self.block) // self.r + j * self.tq

    def outside(self, j, copies=1):
        pos = self.first_position(j) - HALF_WIN + lax.broadcasted_iota(jnp.int32, (1, copies * self.band), 1) % self.band
        return jnp.where((pos >= 0) & (pos < self.length), 0.0, NEG_BIG)

    def extended(self):
        return pltpu.VMEM((self.block + 2 * self.halo, 128), F32)

    def plain(self):
        return pltpu.VMEM((self.block, 128), F32)


def _low_lanes(rows):
    return lax.broadcasted_iota(jnp.int32, (rows, 128), 1) < HEAD_DIM


def _one_head(x, low, half):
    return jnp.where(low if half == 0 else jnp.logical_not(low), x, 0.0).astype(BF16)


def _two_heads(x, low):
    return jnp.concatenate([_one_head(x, low, 0), _one_head(x, low, 1)], axis=0)


def _carry_ride(base_body, n_in, n_out, n_scratch, ride, grid):
    if ride is None:
        return base_body
    n = ride.n

    def body(*refs):
        o = n_in + n
        ins, ride_ins = refs[:n_in], refs[n_in:o]
        outs, ride_outs = refs[o:o + n_out], refs[o + n_out:o + n_out + n]
        o += n_out + n
        scratch, sems = refs[o:o + n_scratch], refs[o + n_scratch:]
        at_first = (pl.program_id(0) == 0) & (pl.program_id(1) == 0)
        at_last = (pl.program_id(0) == grid[0] - 1) & (pl.program_id(1) == grid[1] - 1)

        @pl.when(at_first)
        def _():
            ride.start(ride_ins, ride_outs, sems)

        base_body(*ins, *outs, *scratch)

        @pl.when(at_last)
        def _():
            ride.finish(ride_ins, ride_outs, sems)

    return body


def _ride_call(base_body, name, grid, in_specs, out_specs, out_shape, scratch_shapes, operands, ride):
    n = 0 if ride is None else ride.n
    extra = [] if ride is None else ride.operands
    outs = pl.pallas_call(
        _carry_ride(base_body, len(in_specs), len(out_specs), len(scratch_shapes), ride, grid), name=name, grid=grid,
        in_specs=list(in_specs) + [ANY] * n, out_specs=list(out_specs) + [ANY] * n,
        out_shape=list(out_shape) + ([] if ride is None else ride.out_shape()),
        scratch_shapes=list(scratch_shapes) + ([] if ride is None else ride.scratch_shapes()),
        compiler_params=_cparams(("arbitrary", "arbitrary")),
    )(*operands, *extra)
    return outs[:len(out_specs)], outs[len(out_specs):]


def _swa_fwd(q, k, v, bias, r, so_far=None, ride=None):
    geo = _SwaGeometry(q.shape[0], r)
    prev, cur, nxt = geo.specs()

    def body(q_ref, kp, kc, kn, vp, vc, vn, b_ref, *rest):
        if so_far is None:
            o_ref, lse_ref, qf, kf, vf = rest
        else:
            o_old_ref, lse_old_ref, o_ref, lse_ref, qf, kf, vf = rest
        qf[...] = q_ref[...].astype(F32)
        geo.fill(kf, kp, kc, kn)
        geo.fill(vf, vp, vc, vn)
        tq = geo.tq
        low_q = _low_lanes(tq)
        bias = b_ref[...].reshape(2 * tq, geo.band)
        for own, around, j in geo.tiles():
            q2 = _two_heads(qf[geo.own(own), :], low_q)
            kb = kf[geo.around(around), :].astype(BF16)
            vb = vf[geo.around(around), :].astype(BF16)
            s = _mm_nt(q2, kb) + bias + geo.outside(j)
            m = jnp.max(s, axis=1, keepdims=True)
            e = jnp.exp(s - m)
            l = jnp.sum(e, axis=1, keepdims=True)
            o2 = _mm(e.astype(BF16), vb) / l
            lse2 = m + jnp.log(l)
            o_new = jnp.where(low_q, o2[:tq], o2[tq:])
            lse_new = jnp.where(low_q, lse2[:tq], lse2[tq:])
            if so_far is not None:
                o_old, lse_old = o_old_ref[geo.own(own), :], lse_old_ref[geo.own(own), :]
                top = jnp.maximum(lse_old, lse_new)
                w_old, w_new = jnp.exp(lse_old - top), jnp.exp(lse_new - top)
                o_new = (w_old * o_old + w_new * o_new) / (w_old + w_new)
                lse_new = top + jnp.log(w_old + w_new)
            o_ref[geo.own(own), :] = o_new
            lse_ref[geo.own(own), :] = lse_new

    sds = jax.ShapeDtypeStruct
    before = () if so_far is None else tuple(so_far)
    (o, lse), carried = _ride_call(
        body, "swa_fwd_%d" % r, (D_B // 128, geo.nblk),
        [cur, prev, cur, nxt, prev, cur, nxt, pl.BlockSpec((2, geo.tq, geo.band), lambda c, i: (c, 0, 0))]
        + [cur] * len(before),
        [cur, cur], [sds(q.shape, F32), sds(q.shape, F32)], [geo.plain(), geo.extended(), geo.extended()],
        (q, k, k, k, v, v, v, bias) + before, ride)
    return o, lse, carried


def _head_stats(st, half):
    lo = HEAD_DIM * half
    return st[:, lo:lo + 1], st[:, lo + HEAD_DIM // 2:lo + HEAD_DIM // 2 + 1]


def _swa_bwd_q(q, k, v, dy, st, bias, r, ride=None):
    geo = _SwaGeometry(q.shape[0], r)
    prev, cur, nxt = geo.specs()
    bias_spec = pl.BlockSpec((2, geo.tq, geo.band), lambda c, i: (c, 0, 0))

    def body(q_ref, kp, kc, kn, vp, vc, vn, dy_ref, st_ref, b_ref, dq_ref, db_ref, qf, kf, vf, dyf, dqf):
        @pl.when(pl.program_id(1) == 0)
        def _():
            db_ref[...] = jnp.zeros_like(db_ref)

        qf[...] = q_ref[...].astype(F32)
        dyf[...] = dy_ref[...].astype(F32)
        geo.fill(kf, kp, kc, kn)
        geo.fill(vf, vp, vc, vn)
        tq = geo.tq
        low_q = _low_lanes(tq)
        bias = b_ref[...].reshape(2 * tq, geo.band)
        for own, around, j in geo.tiles():
            sts = st_ref[geo.own(own), :]
            (lse0, delta0), (lse1, delta1) = _head_stats(sts, 0), _head_stats(sts, 1)
            lse = jnp.concatenate([lse0, lse1], axis=0)
            delta = jnp.concatenate([delta0, delta1], axis=0)
            kb = kf[geo.around(around), :].astype(BF16)
            vb = vf[geo.around(around), :].astype(BF16)
            s = _mm_nt(_two_heads(qf[geo.own(own), :], low_q), kb) + bias + geo.outside(j)
            p = jnp.exp(s - lse)
            ds = p * (_mm_nt(_two_heads(dyf[geo.own(own), :], low_q), vb) - delta)
            db_ref[...] += ds.reshape(2, tq, geo.band)
            dq2 = _mm(ds.astype(BF16), kb)
            dqf[geo.own(own), :] = jnp.where(low_q, dq2[:tq], dq2[tq:])
        dq_ref[...] = dqf[...].astype(BF16)

    (dq, dbias), carried = _ride_call(
        body, "swa_bwd_q_%d" % r, (D_B // 128, geo.nblk),
        [cur, prev, cur, nxt, prev, cur, nxt, cur, cur, bias_spec], [cur, bias_spec],
        [jax.ShapeDtypeStruct(q.shape, BF16), jax.ShapeDtypeStruct(bias.shape, F32)],
        [geo.plain(), geo.extended(), geo.extended(), geo.plain(), geo.plain()],
        (q, k, k, k, v, v, v, dy, st, bias), ride)
    return dq, dbias, carried


def _swa_bwd_kv(q, k, v, dy, st, bias_kv, r):
    geo = _SwaGeometry(q.shape[0], r)
    prev, cur, nxt = geo.specs()

    def body(k_ref, v_ref, qp, qc, qn, dp_, dc_, dn_, sp, sc, sn, b_ref, dk_ref, dv_ref, kf, vf, qf, dyf, stf, dkf,
             dvf):
        kf[...] = k_ref[...].astype(F32)
        vf[...] = v_ref[...].astype(F32)
        geo.fill(qf, qp, qc, qn)
        geo.fill(dyf, dp_, dc_, dn_)
        geo.fill(stf, sp, sc, sn)
        band = geo.band
        low_b = _low_lanes(band)
        bias = jnp.concatenate([b_ref[0], b_ref[1]], axis=1)
        half_lanes = HEAD_DIM // 2
        for own, around, j in geo.tiles():
            ks = kf[geo.own(own), :].astype(BF16)
            vs = vf[geo.own(own), :].astype(BF16)
            q2 = _two_heads(qf[geo.around(around), :], low_b)
            dy2 = _two_heads(dyf[geo.around(around), :], low_b)
            st_t = stf[geo.around(around), :].T
            lse = jnp.concatenate([st_t[:1, :], st_t[HEAD_DIM:HEAD_DIM + 1, :]], axis=1)
            delta = jnp.concatenate([st_t[half_lanes:half_lanes + 1, :],
                                     st_t[HEAD_DIM + half_lanes:HEAD_DIM + half_lanes + 1, :]], axis=1)
            p = jnp.exp(_mm_nt(ks, q2) + bias + (geo.outside(j, copies=2) - lse))
            ds = p * (_mm_nt(vs, dy2) - delta)
            dvf[geo.own(own), :] = _mm(p.astype(BF16), dy2)
            dkf[geo.own(own), :] = _mm(ds.astype(BF16), q2)
        dk_ref[...] = dkf[...].astype(BF16)
        dv_ref[...] = dvf[...].astype(BF16)

    return pl.pallas_call(
        body, name="swa_bwd_kv_%d" % r, grid=(D_B // 128, geo.nblk),
        in_specs=[cur, cur, prev, cur, nxt, prev, cur, nxt, prev, cur, nxt,
                  pl.BlockSpec((2, geo.tq, geo.band), lambda c, i: (c, 0, 0))],
        out_specs=[cur, cur],
        out_shape=[jax.ShapeDtypeStruct(q.shape, BF16), jax.ShapeDtypeStruct(q.shape, BF16)],
        scratch_shapes=[geo.plain(), geo.plain(), geo.extended(), geo.extended(), geo.extended(), geo.plain(),
                        geo.plain()],
        compiler_params=_cparams(("parallel", "parallel")),
    )(k, v, q, q, q, dy, dy, dy, st, st, st, bias_kv)


BIAS_ROWS = 16
BIAS_TN = 4096


def _bias_tiles(onehot, rel_bias_t):
    n = onehot.shape[0]

    def body(oh_ref, rb_ref, o_ref):
        o_ref[...] = sum(_mm_nt(piece, oh_ref[...]) for piece in _split3(rb_ref[...]))

    return pl.pallas_call(
        body, name="bias_tiles", grid=(n // BIAS_TN,),
        in_specs=[_rows(BIAS_TN, 128), _full((BIAS_ROWS, 128))],
        out_specs=pl.BlockSpec((BIAS_ROWS, BIAS_TN), lambda i: (0, i)),
        out_shape=jax.ShapeDtypeStruct((BIAS_ROWS, n), F32),
        compiler_params=_cparams(("parallel",)),
    )(onehot, rel_bias_t)


def _bias_bwd(onehot, dbias_rows, so_far, r):
    n = onehot.shape[0]

    def body(oh, d, prev_ref, g_ref):
        @pl.when(pl.program_id(0) == 0)
        def _():
            g_ref[...] = prev_ref[...]

        hi, lo, _ = _split3(d[...])
        g_ref[...] += _mm(hi, oh[...]) + _mm(lo, oh[...])

    return pl.pallas_call(
        body, name="bias_bwd_%d" % r, grid=(n // BIAS_TN,),
        in_specs=[_rows(BIAS_TN, 128), pl.BlockSpec((BIAS_ROWS, BIAS_TN), lambda i: (0, i)), _full((BIAS_ROWS, 128))],
        out_specs=_full((BIAS_ROWS, 128)),
        out_shape=jax.ShapeDtypeStruct((BIAS_ROWS, 128), F32),
        compiler_params=_cparams(("arbitrary",)),
    )(onehot, dbias_rows, so_far)


LATE = ("w_out", "w_ff1", "w_ff2", "w_ple_gate", "w_ple_proj")


def _local_step(x, p, tgt, w_in, late_shards, g_attn_pre, g_q, g_k, g_out_a, g_out_b, g_attn_post, rel_bias,
                g_mlp_pre, g_mlp_post, g_ple):
    s_len = x.shape[0]
    cc, ss = _rope_tables(s_len)
    gq2 = jnp.concatenate([g_q, g_q], axis=-1)
    gk2 = jnp.concatenate([g_k, g_k], axis=-1)
    ones128 = _group_ones(128)
    rel_bias_t = jnp.zeros((BIAS_ROWS, 128), F32).at[:N_HEADS_B, :N_BUCKETS].set(rel_bias.T)

    xn1, qpre, kpre, qa, kv, qb, kb, vb = _in_proj(x, g_attn_pre, w_in, cc, ss, gq2, gk2, ones128)
    ya, lse_a = _attn_a_fwd(qa, kv)

    tiles, joint = [], None
    for r in DILATIONS:
        tq = SWA_TQ
        onehot = _bucket_onehot(tq, r)
        bias = _bias_tiles(onehot, rel_bias_t)[:N_HEADS_B].reshape(N_HEADS_B, tq, tq + 2 * HALF_WIN)
        bias = jnp.where(_in_window(tq), bias, NEG_BIG)
        yb, lse_b, gathered = _swa_fwd(qb, kb, vb, bias, r, joint,
                                       _GatherRide(late_shards) if r == DILATIONS[-1] else None)
        tiles.append((onehot, bias))
        joint = (yb, lse_b)
    w_out, w_ff1, w_ff2, w_gate, w_ple = (_whole(n, g, mine) for n, g, mine in zip(LATE, gathered, late_shards))

    ycat, y2, h1, xn2 = _out_proj(ya, yb, x, g_out_a, g_out_b, w_out, g_attn_post, g_mlp_pre)
    u = _ff1(xn2, w_ff1)
    f2, h2, xn3 = _ff2(u, w_ff2, h1, g_mlp_post, g_ple)
    dh2, df2, dgl, dpp, loss, dg_ple, dg_mlp_post = _ple_loss(xn3, p, h2, f2, tgt, w_gate, w_ple, g_ple, g_mlp_post)

    grads = {"g_ple": dg_ple, "g_mlp_post": dg_mlp_post}
    grads["w_ple_gate"] = _dw(xn3, dgl, "dw_gate")
    grads["w_ple_proj"] = _dw(p, dpp, "dw_ple")
    grads["w_ff2"] = _dw(u, df2, "dw_ff2", relu2=True)
    du = _ff2_bwd(df2, w_ff2.T, u)
    grads["w_ff1"] = _dw(xn2, du, "dw_ff1")
    dh1, dy2, grads["g_mlp_pre"], grads["g_attn_post"] = _ff1_bwd(du, w_ff1, dh2, h1, y2, g_mlp_pre, g_attn_post)
    grads["w_out"] = _dw(ycat, dy2, "dw_out")
    dya, dyb, delta_a, st_b, grads["g_out_a"], grads["g_out_b"] = _out_proj_bwd(dy2, w_out, ya, yb, lse_b, g_out_a,
                                                                              g_out_b)

    pairs = _pair_sums(LATE, [grads[n] for n in LATE])

    dqr, dkv_t = _attn_a_bwd(qa, dya, kv, lse_a, delta_a)
    dkv_a = dkv_t.transpose(0, 2, 1)

    dqs, dks, dvs = [], [], []
    d_rel = jnp.zeros((BIAS_ROWS, 128), F32)
    for r, (onehot, bias) in zip(DILATIONS, tiles):
        dq_r, dbias, scattered = _swa_bwd_q(qb, kb, vb, dyb, st_b, bias, r,
                                            _ScatterRide(pairs) if r == DILATIONS[0] else None)
        if scattered:
            for n, half in zip(LATE, _chip_sums(LATE, pairs, scattered)):
                grads[n] = half
        bias_kv = jnp.flip(bias, axis=(1, 2))
        dk_r, dv_r = _swa_bwd_kv(qb, kb, vb, dyb, st_b, bias_kv, r)
        dbias_rows = jnp.pad(dbias.reshape(N_HEADS_B, -1), ((0, BIAS_ROWS - N_HEADS_B), (0, 0)))
        d_rel = _bias_bwd(onehot, dbias_rows, d_rel, r)
        dqs.append(dq_r)
        dks.append(dk_r)
        dvs.append(dv_r)
    grads["rel_bias"] = d_rel[:N_HEADS_B, :N_BUCKETS].T

    dproj, grad_x, grads["g_attn_pre"], dgq2, dgk2 = _in_proj_bwd(
        dqr, dkv_a, dqs, dks, dvs, qpre, kpre, x, dh1, g_attn_pre, w_in, cc, ss, gq2, gk2, ones128)
    grads["g_q"] = dgq2[:, :HEAD_DIM] + dgq2[:, HEAD_DIM:]
    grads["g_k"] = dgk2[:, :HEAD_DIM] + dgk2[:, HEAD_DIM:]
    grads["w_in"] = _dw(xn1, dproj, "dw_in")
    return loss, grad_x, grads


ANY = pl.BlockSpec(memory_space=pl.ANY)


def _position():
    return lax.axis_index("x"), lax.axis_index("y"), lax.axis_index("c")


def _other_chips(x, y):
    return [(2 * (1 - x) + y, (1 - x, y)), (2 * x + (1 - y), (x, 1 - y)), (2 * (1 - x) + (1 - y), (1 - x, 1 - y))]


def _cast_shards(shards):
    def body(*refs):
        n = len(refs) // 2
        for i_ref, o_ref in zip(refs[:n], refs[n:]):
            o_ref[...] = i_ref[...].astype(BF16)

    return pl.pallas_call(
        body, name="cast_shards",
        in_specs=[pl.BlockSpec(memory_space=pltpu.VMEM)] * len(shards),
        out_specs=[pl.BlockSpec(memory_space=pltpu.VMEM)] * len(shards),
        out_shape=[jax.ShapeDtypeStruct(s.shape, BF16) for s in shards],
        compiler_params=_cparams(),
    )(*shards)


def _gather_weights(shards):
    n = len(shards)

    ride = _GatherRide(shards)

    def body(*refs):
        ride.start(refs[:n], refs[n:2 * n], refs[2 * n:])
        ride.finish(refs[:n], refs[n:2 * n], refs[2 * n:])

    return pl.pallas_call(
        body, name="gather_weights",
        in_specs=[ANY] * n, out_specs=[ANY] * n,
        out_shape=ride.out_shape(), scratch_shapes=ride.scratch_shapes(),
    )(*shards)


class _GatherRide:
    def __init__(self, shards):
        self.operands = list(shards)
        self.n = len(shards)

    def out_shape(self):
        return [jax.ShapeDtypeStruct((N_CHIPS,) + s.shape, s.dtype) for s in self.operands]

    def scratch_shapes(self):
        return [pltpu.SemaphoreType.DMA((3, self.n))] * 4

    @staticmethod
    def _rows(ref, core):
        half = ref.shape[0] // 2
        return pl.ds(pl.multiple_of(core * half, 16), half)

    def _ici(self, ins, outs, sems, k, a, chip):
        x, y, c = _position()
        return pltpu.make_async_remote_copy(ins[a].at[self._rows(ins[a], c), :],
                                            outs[a].at[2 * x + y, self._rows(ins[a], c), :], sems[0].at[k, a],
                                            sems[1].at[k, a], device_id=(*chip, c), device_id_type=MESH)

    def _pass_on(self, ins, outs, sems, k, a, num, core):
        x, y, c = _position()
        half = outs[a].at[num, self._rows(ins[a], core), :]
        return pltpu.make_async_remote_copy(half, half, sems[2].at[k, a], sems[3].at[k, a], device_id=(x, y, 1 - c),
                                            device_id_type=MESH)

    def start(self, ins, outs, sems):
        x, y, _ = _position()
        for k, (_, chip) in enumerate(_other_chips(x, y)):
            for a in range(self.n):
                self._ici(ins, outs, sems, k, a, chip).start()

    def finish(self, ins, outs, sems):
        x, y, c = _position()
        others = _other_chips(x, y)
        for k, (num, chip) in enumerate(others):
            for a in range(self.n):
                landed = outs[a].at[num, self._rows(ins[a], c), :]
                pltpu.make_async_remote_copy(landed, landed, sems[0].at[k, a], sems[1].at[k, a], device_id=(*chip, c),
                                             device_id_type=MESH).wait_recv()
                self._pass_on(ins, outs, sems, k, a, num, c).start()
        for k, (num, chip) in enumerate(others):
            for a in range(self.n):
                self._pass_on(ins, outs, sems, k, a, num, 1 - c).wait_recv()
        for k, (num, chip) in enumerate(others):
            for a in range(self.n):
                self._ici(ins, outs, sems, k, a, chip).wait_send()
                self._pass_on(ins, outs, sems, k, a, num, c).wait_send()


def _send_sibling_half(grads, tag):
    n = len(grads)

    def body(*refs):
        ins, outs = refs[:n], refs[n:2 * n]
        send_sems, recv_sems = refs[2 * n:]
        x, y, c = _position()
        copies = []
        for a in range(n):
            half = ins[a].shape[1] // 2
            theirs = ins[a].at[:, pl.ds(pl.multiple_of((1 - c) * half, 8), half), :]
            cp = pltpu.make_async_remote_copy(theirs, outs[a], send_sems.at[a], recv_sems.at[a],
                                              device_id=(x, y, 1 - c), device_id_type=MESH)
            cp.start()
            copies.append(cp)
        for cp in copies:
            cp.wait()

    return pl.pallas_call(
        body, name="send_sibling_half_" + tag,
        in_specs=[ANY] * n, out_specs=[ANY] * n,
        out_shape=[jax.ShapeDtypeStruct((g.shape[0], g.shape[1] // 2, g.shape[2]), g.dtype) for g in grads],
        scratch_shapes=[pltpu.SemaphoreType.DMA((n,)), pltpu.SemaphoreType.DMA((n,))],
    )(*grads)


def _scatter_to_chips(pairs):
    n = len(pairs)
    ride = _ScatterRide(pairs)

    def body(*refs):
        ride.start(refs[:n], refs[n:2 * n], refs[2 * n:])
        ride.finish(refs[:n], refs[n:2 * n], refs[2 * n:])

    return pl.pallas_call(
        body, name="scatter_to_chips",
        in_specs=[ANY] * n, out_specs=[ANY] * n,
        out_shape=ride.out_shape(), scratch_shapes=ride.scratch_shapes(),
    )(*pairs)


class _ScatterRide:
    def __init__(self, pairs):
        self.operands = list(pairs)
        self.n = len(pairs)

    def out_shape(self):
        return [jax.ShapeDtypeStruct(g.shape, g.dtype) for g in self.operands]

    def scratch_shapes(self):
        return [pltpu.SemaphoreType.DMA((3, self.n))] * 2

    @staticmethod
    def _copy(ins, outs, sems, k, a, src_slot, dst_slot, chip):
        _, _, c = _position()
        return pltpu.make_async_remote_copy(ins[a].at[src_slot], outs[a].at[dst_slot], sems[0].at[k, a],
                                            sems[1].at[k, a], device_id=(*chip, c), device_id_type=MESH)

    def start(self, ins, outs, sems):
        x, y, _ = _position()
        for k, (num, chip) in enumerate(_other_chips(x, y)):
            for a in range(self.n):
                self._copy(ins, outs, sems, k, a, num, 2 * x + y, chip).start()

    def finish(self, ins, outs, sems):
        x, y, _ = _position()
        for k, (num, chip) in enumerate(_other_chips(x, y)):
            for a in range(self.n):
                self._copy(ins, outs, sems, k, a, 2 * x + y, num, chip).wait_recv()
        for k, (num, chip) in enumerate(_other_chips(x, y)):
            for a in range(self.n):
                self._copy(ins, outs, sems, k, a, num, 2 * x + y, chip).wait_send()


def _exchange_halves(halves):
    n = len(halves)

    def body(*refs):
        ins, outs = refs[:n], refs[n:2 * n]
        send_sems, recv_sems = refs[2 * n:]
        x, y, c = _position()
        copies = []
        for a in range(n):
            cp = pltpu.make_async_remote_copy(ins[a], outs[a], send_sems.at[a], recv_sems.at[a],
                                              device_id=(x, y, 1 - c), device_id_type=MESH)
            cp.start()
            copies.append(cp)
        for cp in copies:
            cp.wait()

    return pl.pallas_call(
        body, name="exchange_halves",
        in_specs=[ANY] * n, out_specs=[ANY] * n,
        out_shape=[jax.ShapeDtypeStruct(h.shape, h.dtype) for h in halves],
        scratch_shapes=[pltpu.SemaphoreType.DMA((n,)), pltpu.SemaphoreType.DMA((n,))],
    )(*halves)


def _allreduce_small(v):
    def body(v_ref, o_ref, buf, send_sems, recv_sems):
        x, y, c = _position()
        me = 4 * x + 2 * y + c
        peers = [(1 - x, y, c), (x, 1 - y, c), (x, y, 1 - c), (1 - x, 1 - y, c), (1 - x, y, 1 - c), (x, 1 - y, 1 - c),
                 (1 - x, 1 - y, 1 - c)]
        num = lambda d: 4 * d[0] + 2 * d[1] + d[2]
        buf[me] = v_ref[...]
        sends = []
        for k, peer in enumerate(peers):
            cp = pltpu.make_async_remote_copy(v_ref, buf.at[me], send_sems.at[k], recv_sems.at[k], device_id=peer,
                                              device_id_type=MESH)
            cp.start()
            sends.append(cp)
        for k, peer in enumerate(peers):
            pltpu.make_async_remote_copy(v_ref, buf.at[num(peer)], send_sems.at[k], recv_sems.at[k], device_id=peer,
                                         device_id_type=MESH).wait_recv()
        for cp in sends:
            cp.wait_send()
        total = buf[0]
        for d in range(1, 8):
            total = total + buf[d]
        o_ref[...] = total

    return pl.pallas_call(
        body, name="allreduce_small",
        in_specs=[pl.BlockSpec(memory_space=pltpu.VMEM)], out_specs=pl.BlockSpec(memory_space=pltpu.VMEM),
        out_shape=jax.ShapeDtypeStruct(v.shape, v.dtype),
        scratch_shapes=[pltpu.VMEM((8,) + v.shape, v.dtype), pltpu.SemaphoreType.DMA((7,)),
                        pltpu.SemaphoreType.DMA((7,))],
    )(v)


def _sum_leading(a, name):
    k, r, c = a.shape
    tr = min(r, 256)

    def body(a_ref, o_ref):
        total = a_ref[0].astype(F32)
        for i in range(1, k):
            total = total + a_ref[i].astype(F32)
        o_ref[...] = total

    return pl.pallas_call(
        body, name=name, grid=(r // tr,),
        in_specs=[pl.BlockSpec((k, tr, c), lambda i: (0, i, 0))],
        out_specs=pl.BlockSpec((tr, c), lambda i: (i, 0)),
        out_shape=jax.ShapeDtypeStruct((r, c), F32),
        compiler_params=_cparams(("parallel",)),
    )(a)


def _add(a, b, name):
    k, r, c = a.shape
    tr = min(r, 256)
    spec = pl.BlockSpec((k, tr, c), lambda i: (0, i, 0))

    def body(a_ref, b_ref, o_ref):
        o_ref[...] = (a_ref[...] + b_ref[...]).astype(BF16)

    return pl.pallas_call(
        body, name=name, grid=(r // tr,), in_specs=[spec, spec], out_specs=spec,
        out_shape=jax.ShapeDtypeStruct(a.shape, BF16), compiler_params=_cparams(("parallel",)),
    )(a, b)


def _adamw(w, g, m, v, name):
    r, c = w.shape
    tr = min(r, 256)
    spec = pl.BlockSpec((tr, c), lambda i: (i, 0))

    def body(w_ref, g_ref, m_ref, v_ref, d_ref, nm_ref, nv_ref):
        gv = g_ref[...]
        nm = ADAM_B1 * m_ref[...] + (1.0 - ADAM_B1) * gv
        nv = ADAM_B2 * v_ref[...] + (1.0 - ADAM_B2) * jnp.square(gv)
        m_hat = nm / (1.0 - ADAM_B1 ** ADAM_STEP)
        v_hat = nv / (1.0 - ADAM_B2 ** ADAM_STEP)
        d_ref[...] = -ADAM_LR * (m_hat / (jnp.sqrt(v_hat) + ADAM_EPS) + ADAM_WD * w_ref[...])
        nm_ref[...] = nm
        nv_ref[...] = nv

    return pl.pallas_call(
        body, name=name, grid=(r // tr,), in_specs=[spec] * 4, out_specs=[spec] * 3,
        out_shape=[jax.ShapeDtypeStruct(w.shape, F32)] * 3, compiler_params=_cparams(("parallel",)),
    )(w, g, m, v)


MATRICES = ("w_in", "w_out", "w_ff1", "w_ff2", "w_ple_gate", "w_ple_proj")
COLUMN_SHARDED = ("w_in", "w_ff1", "w_ple_proj")
SMALL = ("g_attn_pre", "g_q", "g_k", "g_out_a", "g_out_b", "g_attn_post", "rel_bias", "g_mlp_pre", "g_mlp_post",
         "g_ple")
WEIGHT_ORDER = ("w_in", "g_attn_pre", "g_q", "g_k", "g_out_a", "g_out_b", "w_out", "g_attn_post", "rel_bias",
                "g_mlp_pre", "w_ff1", "w_ff2", "g_mlp_post", "g_ple", "w_ple_gate", "w_ple_proj")
PACK_ROWS, PACK_COLS = 8, 1024


def _chip():
    return 2 * lax.axis_index("x") + lax.axis_index("y")


def _whole(name, gathered, mine):
    g = lax.dynamic_update_slice_in_dim(gathered, mine[None], _chip(), axis=0)
    if name in COLUMN_SHARDED:
        return g.transpose(1, 0, 2).reshape(g.shape[1], N_CHIPS * g.shape[2])
    return g.reshape(N_CHIPS * g.shape[1], g.shape[2])


def _pair_sums(names, grads):
    by_chip = []
    for n, g in zip(names, grads):
        if n in COLUMN_SHARDED:
            by_chip.append(g.reshape(g.shape[0], N_CHIPS, g.shape[1] // N_CHIPS).transpose(1, 0, 2))
        else:
            by_chip.append(g.reshape(N_CHIPS, g.shape[0] // N_CHIPS, g.shape[1]))
    c = lax.axis_index("c")
    pairs = []
    for n, g, other in zip(names, by_chip, _send_sibling_half(by_chip, names[0])):
        half = g.shape[1] // 2
        pairs.append(_add(lax.dynamic_slice_in_dim(g, c * half, half, axis=1), other, "pair_sum_" + n))
    return pairs


def _chip_sums(names, pairs, scattered):
    halves = []
    for n, pair, got in zip(names, pairs, scattered):
        own = lax.dynamic_slice_in_dim(pair, _chip(), 1, axis=0)
        halves.append(_sum_leading(lax.dynamic_update_slice_in_dim(got, own, _chip(), axis=0), "chip_sum_" + n))
    return halves


def _pack_small(values, extra=None):
    flat = [values[n].reshape(-1) for n in SMALL]
    used = sum(f.shape[0] for f in flat)
    tail = jnp.zeros((PACK_ROWS * PACK_COLS - used - 1,), F32)
    last = jnp.zeros((1,), F32) if extra is None else extra.reshape(1)
    return jnp.concatenate(flat + [tail, last]).reshape(PACK_ROWS, PACK_COLS)


def _unpack_small(packed, like):
    flat = packed.reshape(-1)
    out, o = {}, 0
    for n in SMALL:
        size = like[n].size
        out[n] = flat[o:o + size].reshape(like[n].shape)
        o += size
    return out, flat[-1]


def kernel(x, p, w_in, g_attn_pre, g_q, g_k, g_out_a, g_out_b, w_out, g_attn_post, rel_bias, g_mlp_pre, w_ff1, w_ff2, g_mlp_post, g_ple, w_ple_gate, w_ple_proj, loss_target, m_w_in, m_g_attn_pre, m_g_q, m_g_k, m_g_out_a, m_g_out_b, m_w_out, m_g_attn_post, m_rel_bias, m_g_mlp_pre, m_w_ff1, m_w_ff2, m_g_mlp_post, m_g_ple, m_w_ple_gate, m_w_ple_proj, v_w_in, v_g_attn_pre, v_g_q, v_g_k, v_g_out_a, v_g_out_b, v_w_out, v_g_attn_post, v_rel_bias, v_g_mlp_pre, v_w_ff1, v_w_ff2, v_g_mlp_post, v_g_ple, v_w_ple_gate, v_w_ple_proj):
    given = dict(locals())
    weights = {n: given[n] for n in WEIGHT_ORDER}
    shards = {n: weights[n][0] for n in MATRICES}

    c = lax.axis_index("c")
    own = dict(zip(MATRICES, _cast_shards([shards[n] for n in MATRICES])))
    w_in_whole = _whole("w_in", _gather_weights([own["w_in"]])[0], own["w_in"])

    loss, grad_x, grads = _local_step(
        x[0], p[0, 0], loss_target[0], w_in_whole, [own[n] for n in LATE], g_attn_pre, g_q, g_k, g_out_a, g_out_b,
        g_attn_post, rel_bias, g_mlp_pre, g_mlp_post, g_ple)

    pairs = _pair_sums(["w_in"], [grads["w_in"]])
    grads["w_in"] = _chip_sums(["w_in"], pairs, _scatter_to_chips(pairs))[0]
    halves = [grads[n] for n in MATRICES]
    grad_w = {}
    for n, mine, theirs in zip(MATRICES, halves, _exchange_halves(halves)):
        half = mine.shape[0]
        g = jnp.zeros((2 * half, mine.shape[1]), F32)
        g = lax.dynamic_update_slice_in_dim(g, mine, c * half, axis=0)
        grad_w[n] = lax.dynamic_update_slice_in_dim(g, theirs, (1 - c) * half, axis=0)

    small_like = {n: weights[n] for n in SMALL}
    reduced = _allreduce_small(_pack_small({n: grads[n] for n in SMALL}, extra=loss))
    grad_small, loss_total = _unpack_small(reduced, small_like)

    delta, new_m, new_v = {}, {}, {}
    for n in MATRICES:
        d, nm, nv = _adamw(shards[n], grad_w[n], given["m_" + n][0], given["v_" + n][0], "adamw_" + n)
        delta[n], new_m[n], new_v[n] = d[None], nm[None], nv[None]
        grad_w[n] = grad_w[n][None]
    d, nm, nv = _adamw(_pack_small(small_like), reduced, _pack_small({n: given["m_" + n] for n in SMALL}),
                       _pack_small({n: given["v_" + n] for n in SMALL}), "adamw_small")
    d_small, nm_small, nv_small = (_unpack_small(a, small_like)[0] for a in (d, nm, nv))
    for n in SMALL:
        grad_w[n], delta[n], new_m[n], new_v[n] = grad_small[n], d_small[n], nm_small[n], nv_small[n]

    return (loss_total, grad_x[None], *[grad_w[n] for n in WEIGHT_ORDER], *[delta[n] for n in WEIGHT_ORDER],
            *[new_m[n] for n in WEIGHT_ORDER], *[new_v[n] for n in WEIGHT_ORDER])
```
